```python
import math, functools
import jax, jax.numpy as jnp
from jax import lax
import numpy as np

D_MODEL = 1024
BATCH = 8
SEQ = 2048
DEPTH = 1
DEC_BATCH = 32
DEC_SEQ = 4
PAST_LEN = 8192
PAGE_SIZE = 128

SSM_WIDTH = D_MODEL // 2
SSM_GROUP = 16
SSM_GROUPS = SSM_WIDTH // SSM_GROUP
SSM_STATE = 64
NSA_HEADS = 8
HEAD_DIM = 64
NSA_WIDTH = NSA_HEADS * HEAD_DIM
NSA_KV = 2
NSA_REP = NSA_HEADS // NSA_KV
KV_WIDTH = NSA_KV * HEAD_DIM
BLK = 64
N_SEL = 16
WINDOW = 512
Q_CHUNK = 64
ROT_DIM = HEAD_DIM // 4
ROPE_THETA = 500000.0
MEM_LEN = 256
X_HEADS = 4
X_HEAD_DIM = 128
X_WIDTH = X_HEADS * X_HEAD_DIM
D_FF = 4 * D_MODEL
N_IN = SSM_WIDTH + NSA_WIDTH + 6 * KV_WIDTH + 3 * NSA_HEADS + 2 * D_MODEL
EPS = 1e-6
NEG_INF = -1e30
FORCE_SCORE = 1e4
F32 = jnp.float32

kernel_name = 'hybrid_s5_nsa_decode_step'


def rmsnorm(x, g):
    xf = x.astype(F32)
    y = xf * lax.rsqrt(jnp.mean(xf * xf, axis=-1, keepdims=True) + EPS) * g.astype(F32)
    return y.astype(x.dtype)


def partial_rope(x, pos):
    half = ROT_DIM // 2
    freqs = ROPE_THETA ** (-jnp.arange(half, dtype=F32) / half)
    ang = pos.astype(F32)[:, None] * freqs[None, :]
    cos = jnp.cos(ang)[:, None, :]
    sin = jnp.sin(ang)[:, None, :]
    xf = x.astype(F32)
    x1, x2 = xf[..., :half], xf[..., half:ROT_DIM]
    out = jnp.concatenate([x1 * cos - x2 * sin, x2 * cos + x1 * sin, xf[..., ROT_DIM:]], axis=-1)
    return out.astype(x.dtype)


def masked_softmax(s, mask, axis=-1):
    s = jnp.where(mask, s, NEG_INF)
    m = jnp.max(s, axis=axis, keepdims=True)
    p = jnp.exp(s - m) * mask
    return p / jnp.maximum(jnp.sum(p, axis=axis, keepdims=True), 1e-30)


def _linear_combine(left, right):
    a1, b1 = left
    a2, b2 = right
    return a1 * a2, a2 * b1 + b2


def ssm_branch(u, h0, p):
    bsz, t = u.shape[:2]
    uf = u.astype(F32)
    lam = lax.complex(p['ssm_lam_re'].astype(F32), p['ssm_lam_im'].astype(F32))
    dt = jnp.exp(p['ssm_log_dt'].astype(F32))[:, None]
    lam_bar = jnp.exp(lam * dt)
    b = lax.complex(p['ssm_b_re'].astype(F32), p['ssm_b_im'].astype(F32))
    b_bar = ((lam_bar - 1.0) / lam)[..., None] * b
    c = lax.complex(p['ssm_c_re'].astype(F32), p['ssm_c_im'].astype(F32))
    ug = uf.reshape(bsz, t, SSM_GROUPS, SSM_GROUP).astype(jnp.complex64)
    bu = jnp.einsum('gpc,btgc->btgp', b_bar, ug)
    a = jnp.broadcast_to(lam_bar, bu.shape)
    a_cum, b_cum = lax.associative_scan(_linear_combine, (a, bu), axis=1)
    h = a_cum * h0[:, None] + b_cum
    y = jnp.real(jnp.einsum('gcp,btgp->btgc', c, h)).reshape(bsz, t, SSM_WIDTH) + p['ssm_d'].astype(F32) * uf
    y = jax.nn.gelu(y)
    out = y * jax.nn.sigmoid(y @ p['w_glu'].astype(F32) + p['b_glu'].astype(F32))
    return out.astype(u.dtype), h[:, -1]


def nsa_compress(rows, pe, w1, w2):
    bsz, tk = rows.shape[:2]
    nb = tk // BLK
    blocks = rows.astype(F32).reshape(bsz, nb, BLK, NSA_KV, HEAD_DIM) + pe.astype(F32)[None, None, :, None, :]
    flat = blocks.transpose(0, 1, 3, 2, 4).reshape(bsz, nb, NSA_KV, BLK * HEAD_DIM)
    return jax.nn.gelu(flat @ w1.astype(F32)) @ w2.astype(F32)


def nsa_core(q, gates, q_pos, kcmp, vcmp, ks_t, vs_t, wk, wv, w_pos):
    bsz, tq = q.shape[:2]
    nb = kcmp.shape[1]
    qg = q.astype(F32).reshape(bsz, tq, NSA_KV, NSA_REP, HEAD_DIM) * (HEAD_DIM ** -0.5)
    blk_ids = jnp.arange(nb, dtype=jnp.int32)
    s_c = jnp.einsum('bqgrd,bngd->bqgrn', qg, kcmp)
    c_valid = ((blk_ids[None, :] + 1) * BLK - 1 <= q_pos[:, None])[None, :, None, None, :]
    p_c = masked_softmax(s_c, c_valid)
    o_c = jnp.einsum('bqgrn,bngd->bqgrd', p_c, vcmp)
    cur = q_pos // BLK
    jb = blk_ids[None, :]
    cq = cur[:, None]
    forced = ((jb == 0) | (jb == cq) | (jb == cq - 1))[None, :, None, :]
    allowed = (jb <= cq)[None, :, None, :]
    imp = jnp.where(forced, FORCE_SCORE, jnp.sum(p_c, axis=3))
    imp = jnp.where(allowed, imp, -FORCE_SCORE)
    k_sel = min(N_SEL, nb)
    _, idx = lax.top_k(imp, k_sel)
    b_ix = jnp.arange(bsz)[:, None, None, None]
    g_ix = jnp.arange(NSA_KV)[None, None, :, None]
    ks_g = ks_t[b_ix, g_ix, idx].astype(F32)
    vs_g = vs_t[b_ix, g_ix, idx].astype(F32).reshape(bsz, tq, NSA_KV, k_sel * BLK, HEAD_DIM)
    key_pos = idx[..., None] * BLK + jnp.arange(BLK, dtype=jnp.int32)
    s_mask = (key_pos <= q_pos[None, :, None, None, None]).reshape(bsz, tq, NSA_KV, 1, k_sel * BLK)
    s_s = jnp.einsum('bqgrd,bqgksd->bqgrks', qg, ks_g).reshape(bsz, tq, NSA_KV, NSA_REP, k_sel * BLK)
    p_s = masked_softmax(s_s, s_mask)
    o_s = jnp.einsum('bqgrn,bqgnd->bqgrd', p_s, vs_g)
    s_w = jnp.einsum('bqgrd,bngd->bqgrn', qg, wk.astype(F32))
    dpos = q_pos[:, None] - w_pos[None, :]
    w_valid = ((dpos >= 0) & (dpos < WINDOW) & (w_pos[None, :] >= 0))[None, :, None, None, :]
    p_w = masked_softmax(s_w, w_valid)
    o_w = jnp.einsum('bqgrn,bngd->bqgrd', p_w, wv.astype(F32))
    g = jax.nn.sigmoid(gates.astype(F32)).reshape(bsz, tq, NSA_KV, NSA_REP, 3)
    o = g[..., 0:1] * o_c + g[..., 1:2] * o_s + g[..., 2:3] * o_w
    return o.reshape(bsz, tq, NSA_WIDTH).astype(q.dtype)


def nsa_prompt(q, gates, k_cmp, v_cmp, k_sel, v_sel, k_win, v_win, p):
    bsz, t = q.shape[:2]
    nb = t // BLK
    kcmp = nsa_compress(k_cmp, p['cmp_pe_k'], p['w_cmpk1'], p['w_cmpk2'])
    vcmp = nsa_compress(v_cmp, p['cmp_pe_v'], p['w_cmpv1'], p['w_cmpv2'])
    ks_t = k_sel.reshape(bsz, nb, BLK, NSA_KV, HEAD_DIM).transpose(0, 3, 1, 2, 4)
    vs_t = v_sel.reshape(bsz, nb, BLK, NSA_KV, HEAD_DIM).transpose(0, 3, 1, 2, 4)
    wk_pad = jnp.pad(k_win, ((0, 0), (WINDOW, 0), (0, 0), (0, 0)))
    wv_pad = jnp.pad(v_win, ((0, 0), (WINDOW, 0), (0, 0), (0, 0)))
    n_chunks = t // Q_CHUNK
    q_c = q.reshape(bsz, n_chunks, Q_CHUNK, NSA_HEADS, HEAD_DIM).swapaxes(0, 1)
    g_c = gates.reshape(bsz, n_chunks, Q_CHUNK, 3 * NSA_HEADS).swapaxes(0, 1)

    def chunk(args):
        ci, qi, gi = args
        q0 = ci * Q_CHUNK
        q_pos = q0 + jnp.arange(Q_CHUNK, dtype=jnp.int32)
        w_pos = q0 - WINDOW + jnp.arange(WINDOW + Q_CHUNK, dtype=jnp.int32)
        wk_i = lax.dynamic_slice_in_dim(wk_pad, q0, WINDOW + Q_CHUNK, axis=1)
        wv_i = lax.dynamic_slice_in_dim(wv_pad, q0, WINDOW + Q_CHUNK, axis=1)
        return nsa_core(qi, gi, q_pos, kcmp, vcmp, ks_t, vs_t, wk_i, wv_i, w_pos)

    out = lax.map(chunk, (jnp.arange(n_chunks, dtype=jnp.int32), q_c, g_c))
    out = out.swapaxes(0, 1).reshape(bsz, t, NSA_WIDTH)
    rows = jnp.stack([k_cmp, v_cmp, k_sel, v_sel], axis=2)
    w_keep = min(WINDOW, t)
    win = jnp.stack([k_win, v_win], axis=2)[:, t - w_keep:]
    return out, (rows, win)


def nsa_sample(q, gates, k_cmp, v_cmp, k_sel, v_sel, k_win, v_win, p, cache_kv, page_table, win_buf):
    bsz, t = q.shape[:2]
    past_len = page_table.shape[1] * PAGE_SIZE
    past = cache_kv[page_table].reshape(bsz, past_len, 4, NSA_KV, HEAD_DIM)
    rows = jnp.stack([k_cmp, v_cmp, k_sel, v_sel], axis=2)
    total = past_len + t
    pad = (-total) % BLK
    full = jnp.pad(jnp.concatenate([past, rows.astype(past.dtype)], axis=1),
                   ((0, 0), (0, pad), (0, 0), (0, 0), (0, 0)))
    nb = (total + pad) // BLK
    kcmp = nsa_compress(full[:, :, 0], p['cmp_pe_k'], p['w_cmpk1'], p['w_cmpk2'])
    vcmp = nsa_compress(full[:, :, 1], p['cmp_pe_v'], p['w_cmpv1'], p['w_cmpv2'])
    ks_t = full[:, :, 2].reshape(bsz, nb, BLK, NSA_KV, HEAD_DIM).transpose(0, 3, 1, 2, 4)
    vs_t = full[:, :, 3].reshape(bsz, nb, BLK, NSA_KV, HEAD_DIM).transpose(0, 3, 1, 2, 4)
    w_buf_len = win_buf.shape[1]
    win_full = jnp.concatenate([win_buf, jnp.stack([k_win, v_win], axis=2).astype(win_buf.dtype)], axis=1)
    w_pos = past_len - w_buf_len + jnp.arange(w_buf_len + t, dtype=jnp.int32)
    q_pos = past_len + jnp.arange(t, dtype=jnp.int32)
    out = nsa_core(q, gates, q_pos, kcmp, vcmp, ks_t, vs_t, win_full[:, :, 0], win_full[:, :, 1], w_pos)
    return out, (rows, win_full[:, t:])


def memory_kv(mem, g_mem, w_xk, w_xv):
    bsz, m_len, _ = mem.shape
    mn = rmsnorm(mem, g_mem)
    k = (mn @ w_xk).reshape(bsz, m_len, X_HEADS, X_HEAD_DIM)
    v = (mn @ w_xv).reshape(bsz, m_len, X_HEADS, X_HEAD_DIM)
    return jnp.stack([k, v], axis=2)


def cross_attention(c, mem_kv, w_xq, w_xo):
    bsz, t, _ = c.shape
    q = (c @ w_xq).reshape(bsz, t, X_HEADS, X_HEAD_DIM).astype(F32) * (X_HEAD_DIM ** -0.5)
    s = jnp.einsum('bthd,bmhd->bhtm', q, mem_kv[:, :, 0].astype(F32))
    pr = jax.nn.softmax(s, axis=-1)
    o = jnp.einsum('bhtm,bmhd->bthd', pr, mem_kv[:, :, 1].astype(F32))
    return o.reshape(bsz, t, X_WIDTH).astype(c.dtype) @ w_xo


def decoder_layer(x, pos, h0, attend_fn, mem_kv, p):
    bsz, t, _ = x.shape
    a = rmsnorm(x, p['g_mix_pre'])
    proj = a @ p['w_in']
    o1 = SSM_WIDTH
    o2 = o1 + NSA_WIDTH
    o3 = o2 + 6 * KV_WIDTH
    o4 = o3 + 3 * NSA_HEADS
    u = proj[..., :o1]
    q = partial_rope(proj[..., o1:o2].reshape(bsz, t, NSA_HEADS, HEAD_DIM), pos)
    kv6 = proj[..., o2:o3].reshape(bsz, t, 6, NSA_KV, HEAD_DIM)
    nsa_gates = proj[..., o3:o4]
    merge_logits = proj[..., o4:]
    k_cmp = partial_rope(kv6[:, :, 0], pos)
    v_cmp = kv6[:, :, 1]
    k_sel = partial_rope(kv6[:, :, 2], pos)
    v_sel = kv6[:, :, 3]
    k_win = partial_rope(kv6[:, :, 4], pos)
    v_win = kv6[:, :, 5]
    ssm_out, h_last = ssm_branch(u, h0, p)
    nsa_out, nsa_state = attend_fn(q, nsa_gates, k_cmp, v_cmp, k_sel, v_sel, k_win, v_win, p)
    gates = jax.nn.sigmoid(merge_logits.astype(F32)).reshape(bsz, t, 2, D_MODEL)
    merged = (gates[:, :, 0] * (ssm_out @ p['w_br_ssm']).astype(F32)
              + gates[:, :, 1] * (nsa_out @ p['w_br_nsa']).astype(F32))
    x = x + rmsnorm(merged.astype(x.dtype) @ p['w_out'], p['g_mix_post'])
    c = rmsnorm(x, p['g_x_pre'])
    x = x + rmsnorm(cross_attention(c, mem_kv, p['w_xq'], p['w_xo']), p['g_x_post'])
    m = rmsnorm(x, p['g_mlp_pre'])
    hid = jnp.square(jax.nn.relu(m @ p['w_up']))
    x = x + rmsnorm(hid @ p['w_down'], p['g_mlp_post'])
    return x, h_last, nsa_state


def setup_inputs(seed: int = 0) -> dict:
    keys = iter(jax.random.split(jax.random.key(seed), 64))

    def nrm(shape, scale=1.0):
        return jax.random.normal(next(keys), shape, F32) * scale

    def gain():
        return 1.0 + nrm((DEPTH, D_MODEL), 0.02)

    n_pages = PAST_LEN // PAGE_SIZE
    n_used = DEC_BATCH * n_pages
    n_pool = n_used + max(1, n_used // 4)
    w_buf = min(WINDOW, PAST_LEN)
    page_table = jax.random.permutation(next(keys), n_pool)[:n_used].reshape(DEC_BATCH, n_pages).astype(jnp.int32)
    inv2 = 2.0 ** -0.5
    return {
        'x_prompt': nrm((BATCH, SEQ, D_MODEL)),
        'x_sample': nrm((DEC_BATCH, DEC_SEQ, D_MODEL)),
        'cache_nsa_kv': nrm((DEPTH, n_pool, PAGE_SIZE, 4, NSA_KV, HEAD_DIM)),
        'cache_win_kv': nrm((DEPTH, DEC_BATCH, w_buf, 2, NSA_KV, HEAD_DIM)),
        'state_ssm': nrm((DEPTH, DEC_BATCH, SSM_GROUPS, SSM_STATE, 2), 0.1),
        'cache_mem_kv': nrm((DEPTH, DEC_BATCH, MEM_LEN, 2, X_HEADS, X_HEAD_DIM)),
        'page_table': page_table,
        'mem_prompt': nrm((BATCH, MEM_LEN, D_MODEL)),
        'g_mix_pre': gain(),
        'w_in': nrm((DEPTH, D_MODEL, N_IN), D_MODEL ** -0.5),
        'ssm_lam_re': -0.5 + nrm((DEPTH, SSM_GROUPS, SSM_STATE), 0.01),
        'ssm_lam_im': jnp.pi * jnp.arange(SSM_STATE, dtype=F32) + nrm((DEPTH, SSM_GROUPS, SSM_STATE), 0.01),
        'ssm_log_dt': jax.random.uniform(next(keys), (DEPTH, SSM_GROUPS), F32, math.log(1e-3), math.log(1e-1)),
        'ssm_b_re': nrm((DEPTH, SSM_GROUPS, SSM_STATE, SSM_GROUP), inv2 * SSM_GROUP ** -0.5),
        'ssm_b_im': nrm((DEPTH, SSM_GROUPS, SSM_STATE, SSM_GROUP), inv2 * SSM_GROUP ** -0.5),
        'ssm_c_re': nrm((DEPTH, SSM_GROUPS, SSM_GROUP, SSM_STATE), inv2 * SSM_STATE ** -0.5),
        'ssm_c_im': nrm((DEPTH, SSM_GROUPS, SSM_GROUP, SSM_STATE), inv2 * SSM_STATE ** -0.5),
        'ssm_d': nrm((DEPTH, SSM_WIDTH)),
        'w_glu': nrm((DEPTH, SSM_WIDTH, SSM_WIDTH), SSM_WIDTH ** -0.5),
        'b_glu': nrm((DEPTH, SSM_WIDTH), 0.01),
        'cmp_pe_k': nrm((DEPTH, BLK, HEAD_DIM), 0.02),
        'w_cmpk1': nrm((DEPTH, BLK * HEAD_DIM, HEAD_DIM), (BLK * HEAD_DIM) ** -0.5),
        'w_cmpk2': nrm((DEPTH, HEAD_DIM, HEAD_DIM), HEAD_DIM ** -0.5),
        'cmp_pe_v': nrm((DEPTH, BLK, HEAD_DIM), 0.02),
        'w_cmpv1': nrm((DEPTH, BLK * HEAD_DIM, HEAD_DIM), (BLK * HEAD_DIM) ** -0.5),
        'w_cmpv2': nrm((DEPTH, HEAD_DIM, HEAD_DIM), HEAD_DIM ** -0.5),
        'w_br_ssm': nrm((DEPTH, SSM_WIDTH, D_MODEL), SSM_WIDTH ** -0.5),
        'w_br_nsa': nrm((DEPTH, NSA_WIDTH, D_MODEL), NSA_WIDTH ** -0.5),
        'w_out': nrm((DEPTH, D_MODEL, D_MODEL), D_MODEL ** -0.5),
        'g_mix_post': gain(),
        'g_x_pre': gain(),
        'g_mem': gain(),
        'w_xq': nrm((DEPTH, D_MODEL, X_WIDTH), D_MODEL ** -0.5),
        'w_xk': nrm((DEPTH, D_MODEL, X_WIDTH), D_MODEL ** -0.5),
        'w_xv': nrm((DEPTH, D_MODEL, X_WIDTH), D_MODEL ** -0.5),
        'w_xo': nrm((DEPTH, X_WIDTH, D_MODEL), X_WIDTH ** -0.5),
        'g_x_post': gain(),
        'g_mlp_pre': gain(),
        'w_up': nrm((DEPTH, D_MODEL, D_FF), D_MODEL ** -0.5),
        'w_down': nrm((DEPTH, D_FF, D_MODEL), D_FF ** -0.5),
        'g_mlp_post': gain(),
    }


def reference(x_prompt, x_sample, cache_nsa_kv, cache_win_kv, state_ssm, cache_mem_kv, page_table, mem_prompt,
              g_mix_pre, w_in, ssm_lam_re, ssm_lam_im, ssm_log_dt, ssm_b_re, ssm_b_im, ssm_c_re, ssm_c_im,
              ssm_d, w_glu, b_glu, cmp_pe_k, w_cmpk1, w_cmpk2, cmp_pe_v, w_cmpv1, w_cmpv2,
              w_br_ssm, w_br_nsa, w_out, g_mix_post, g_x_pre, g_mem, w_xq, w_xk, w_xv, w_xo, g_x_post,
              g_mlp_pre, w_up, w_down, g_mlp_post):
    pos_p = jnp.arange(x_prompt.shape[1], dtype=jnp.int32)
    pos_s = PAST_LEN + jnp.arange(x_sample.shape[1], dtype=jnp.int32)
    y_p, y_s = x_prompt, x_sample
    kv_p, kv_s, win_p, win_s, ssm_p, ssm_s, mem_p = [], [], [], [], [], [], []
    for l in range(DEPTH):
        p = dict(g_mix_pre=g_mix_pre[l], w_in=w_in[l], ssm_lam_re=ssm_lam_re[l], ssm_lam_im=ssm_lam_im[l],
                 ssm_log_dt=ssm_log_dt[l], ssm_b_re=ssm_b_re[l], ssm_b_im=ssm_b_im[l], ssm_c_re=ssm_c_re[l],
                 ssm_c_im=ssm_c_im[l], ssm_d=ssm_d[l], w_glu=w_glu[l], b_glu=b_glu[l], cmp_pe_k=cmp_pe_k[l],
                 w_cmpk1=w_cmpk1[l], w_cmpk2=w_cmpk2[l], cmp_pe_v=cmp_pe_v[l], w_cmpv1=w_cmpv1[l],
                 w_cmpv2=w_cmpv2[l], w_br_ssm=w_br_ssm[l], w_br_nsa=w_br_nsa[l], w_out=w_out[l],
                 g_mix_post=g_mix_post[l], g_x_pre=g_x_pre[l], w_xq=w_xq[l], w_xo=w_xo[l],
                 g_x_post=g_x_post[l], g_mlp_pre=g_mlp_pre[l], w_up=w_up[l], w_down=w_down[l],
                 g_mlp_post=g_mlp_post[l])
        mem_kv_l = memory_kv(mem_prompt, g_mem[l], w_xk[l], w_xv[l])
        h0_p = jnp.zeros((x_prompt.shape[0], SSM_GROUPS, SSM_STATE), jnp.complex64)
        y_p, h_p, (rows_p, wst_p) = decoder_layer(y_p, pos_p, h0_p, nsa_prompt, mem_kv_l, p)
        h0_s = lax.complex(state_ssm[l, ..., 0].astype(F32), state_ssm[l, ..., 1].astype(F32))
        sample_fn = functools.partial(nsa_sample, cache_kv=cache_nsa_kv[l], page_table=page_table,
                                      win_buf=cache_win_kv[l])
        y_s, h_s, (rows_s, wst_s) = decoder_layer(y_s, pos_s, h0_s, sample_fn, cache_mem_kv[l], p)
        kv_p.append(rows_p)
        kv_s.append(rows_s)
        win_p.append(wst_p)
        win_s.append(wst_s)
        ssm_p.append(jnp.stack([jnp.real(h_p), jnp.imag(h_p)], axis=-1))
        ssm_s.append(jnp.stack([jnp.real(h_s), jnp.imag(h_s)], axis=-1))
        mem_p.append(mem_kv_l)
    kv_prompt = jnp.stack(kv_p)
    kv_sample = jnp.stack(kv_s)
    win_prompt = jnp.stack(win_p)
    win_sample = jnp.stack(win_s)
    ssm_prompt = jnp.stack(ssm_p)
    ssm_sample = jnp.stack(ssm_s)
    mem_kv_prompt = jnp.stack(mem_p)
    return (y_p, y_s, kv_prompt, kv_sample, win_prompt, win_sample, ssm_prompt, ssm_sample, mem_kv_prompt)
```

```python
import functools
import math

import jax
import jax.numpy as jnp
from jax import lax
from jax.experimental import pallas as pl
from jax.experimental.pallas import tpu as pltpu

F32 = jnp.float32
BF16 = jnp.bfloat16

D_MODEL = 1024
SSM_WIDTH = 512
SSM_GROUP = 16
SSM_GROUPS = 32
SSM_STATE = 64
N_STATE = SSM_GROUPS * SSM_STATE
STATE_CHUNK = 512
N_CHUNKS = N_STATE // STATE_CHUNK
NSA_HEADS = 8
HEAD_DIM = 64
NSA_WIDTH = NSA_HEADS * HEAD_DIM
NSA_KV = 2
NSA_REP = NSA_HEADS // NSA_KV
KV_WIDTH = NSA_KV * HEAD_DIM
BLK = 64
N_SEL = 16
WINDOW = 512
ROT_DIM = 16
ROPE_THETA = 500000.0
PAGE_SIZE = 128
X_HEADS = 4
X_HEAD_DIM = 128
X_WIDTH = X_HEADS * X_HEAD_DIM
D_FF = 4 * D_MODEL
EPS = 1e-6
NEG_INF = -1e30
M_INIT = -1e29
FORCE_SCORE = 1e4
LANES = 128
SUBLANES = 8
VMEM_LIMIT = 56 * 1024 * 1024

N_PROJ = SSM_WIDTH + NSA_WIDTH + 6 * KV_WIDTH
N_GATE = 3 * NSA_HEADS
KEY_TILE = 128


def _params(*sem):
    return pltpu.CompilerParams(dimension_semantics=sem, vmem_limit_bytes=VMEM_LIMIT)


def _rms(x, g):
    return x * lax.rsqrt(jnp.mean(x * x, axis=-1, keepdims=True) + EPS) * g


def _gelu(x):
    return 0.5 * x * (1.0 + jnp.tanh(math.sqrt(2.0 / math.pi) * (x + 0.044715 * (x * x * x))))


def _sigmoid(x):
    return 1.0 / (1.0 + jnp.exp(-x))


def _dot(a, b):
    return jnp.dot(a, b, preferred_element_type=F32)


def _dot_nt(a, b):
    return lax.dot_general(a, b, (((1,), (1,)), ((), ())), preferred_element_type=F32)


def _proj_kernel(x_ref, g_ref, w_ref, wg_ref, cos_ref, sp_ref, sm_ref,
                 u_ref, q_ref, kv_ref, win_ref, kvb_ref, gate_ref):
    a = _rms(x_ref[...], g_ref[...]).astype(BF16)
    cos, sp, sm = cos_ref[...], sp_ref[...], sm_ref[...]

    def rope(blk):
        return blk * cos + pltpu.roll(blk, 8, 1) * sp + pltpu.roll(blk, LANES - 8, 1) * sm

    u_ref[...] = _dot(a, w_ref[:, 0:SSM_WIDTH])
    for j in range(NSA_WIDTH // LANES):
        c0 = SSM_WIDTH + j * LANES
        q_ref[:, j * LANES:(j + 1) * LANES] = rope(_dot(a, w_ref[:, c0:c0 + LANES])).astype(BF16)
    for j in range(6):
        c0 = SSM_WIDTH + NSA_WIDTH + j * LANES
        blk = _dot(a, w_ref[:, c0:c0 + LANES])
        if j % 2 == 0:
            blk = rope(blk)
        if j < 4:
            kv_ref[:, j * LANES:(j + 1) * LANES] = blk
        else:
            win_ref[:, (j - 4) * LANES:(j - 3) * LANES] = blk
        kvb_ref[:, j * LANES:(j + 1) * LANES] = blk.astype(BF16)
    gate_ref[...] = _sigmoid(_dot(a, wg_ref[...]))


def _proj(x2d, g, w, wg, tabs, *, tm, n_tab_blocks, u_time_major):
    n = x2d.shape[0]
    steps = n // tm
    if u_time_major:
        n_b = steps // n_tab_blocks
        u_shape = jax.ShapeDtypeStruct((n_tab_blocks * tm, n_b * SSM_WIDTH), F32)
        u_spec = pl.BlockSpec((tm, SSM_WIDTH), lambda i: (i % n_tab_blocks, i // n_tab_blocks))
    else:
        u_shape = jax.ShapeDtypeStruct((n, SSM_WIDTH), F32)
        u_spec = pl.BlockSpec((tm, SSM_WIDTH), lambda i: (i, 0))
    row = lambda w_: pl.BlockSpec((tm, w_), lambda i: (i, 0))
    full = lambda a: pl.BlockSpec(a.shape, lambda i: (0,) * a.ndim)
    tab = pl.BlockSpec((tm, LANES), lambda i: (i % n_tab_blocks, 0))
    return pl.pallas_call(
        _proj_kernel,
        grid=(steps,),
        in_specs=[row(D_MODEL), full(g), full(w), full(wg), tab, tab, tab],
        out_specs=[u_spec, row(NSA_WIDTH), row(4 * KV_WIDTH), row(2 * KV_WIDTH), row(6 * KV_WIDTH), row(LANES)],
        out_shape=[u_shape,
                   jax.ShapeDtypeStruct((n, NSA_WIDTH), BF16),
                   jax.ShapeDtypeStruct((n, 4 * KV_WIDTH), F32),
                   jax.ShapeDtypeStruct((n, 2 * KV_WIDTH), F32),
                   jax.ShapeDtypeStruct((n, 6 * KV_WIDTH), BF16),
                   jax.ShapeDtypeStruct((n, LANES), F32)],
        compiler_params=_params("parallel"),
        name="proj",
    )(x2d, g, w, wg, *tabs)


def _rope_tables(pos):
    half = ROT_DIM // 2
    freqs = ROPE_THETA ** (-jnp.arange(half, dtype=F32) / half)
    ang = pos.astype(F32)[:, None] * freqs[None, :]
    cos, sin = jnp.cos(ang), jnp.sin(ang)
    r = pos.shape[0]
    z8 = jnp.zeros((r, half), F32)
    rest0 = jnp.zeros((r, HEAD_DIM - ROT_DIM), F32)
    rest1 = jnp.ones((r, HEAD_DIM - ROT_DIM), F32)
    c64 = jnp.concatenate([cos, cos, rest1], axis=1)
    sp64 = jnp.concatenate([z8, sin, rest0], axis=1)
    sm64 = jnp.concatenate([-sin, z8, rest0], axis=1)
    return tuple(jnp.tile(t, (1, LANES // HEAD_DIM)) for t in (c64, sp64, sm64))


def _ssm_kernel(u_ref, h0_ref, lam_ref, bm_ref, cm_ref, d_ref, wglu_ref, bglu_ref,
                y_ref, hlast_ref, hs_ref, hstate_ref, *, tc):
    j = pl.program_id(1)

    @pl.when(j == 0)
    def _():
        hstate_ref[...] = h0_ref[0]

    u = u_ref[...]
    ub = u.astype(BF16)
    half_in = SSM_WIDTH // 2
    half_st = N_STATE
    for h in range(2):
        hs_ref[:, h * half_st:(h + 1) * half_st] = _dot(ub[:, h * half_in:(h + 1) * half_in], bm_ref[h])

    for c in range(N_CHUNKS):
        re0 = c * 2 * STATE_CHUNK
        im0 = re0 + STATE_CHUNK
        lr = jnp.broadcast_to(lam_ref[0:1, re0:re0 + STATE_CHUNK], (SUBLANES, STATE_CHUNK))
        li = jnp.broadcast_to(lam_ref[0:1, im0:im0 + STATE_CHUNK], (SUBLANES, STATE_CHUNK))

        def step(t, carry, re0=re0, im0=im0, lr=lr, li=li):
            hr, hi = carry
            r0 = pl.multiple_of(t * SUBLANES, SUBLANES)
            nr = lr * hr - li * hi + hs_ref[pl.ds(r0, SUBLANES), re0:re0 + STATE_CHUNK]
            ni = lr * hi + li * hr + hs_ref[pl.ds(r0, SUBLANES), im0:im0 + STATE_CHUNK]
            hs_ref[pl.ds(r0, SUBLANES), re0:re0 + STATE_CHUNK] = nr
            hs_ref[pl.ds(r0, SUBLANES), im0:im0 + STATE_CHUNK] = ni
            return nr, ni

        hr, hi = lax.fori_loop(0, tc, step,
                               (hstate_ref[:, re0:re0 + STATE_CHUNK], hstate_ref[:, im0:im0 + STATE_CHUNK]),
                               unroll=min(tc, 4))
        hstate_ref[:, re0:re0 + STATE_CHUNK] = hr
        hstate_ref[:, im0:im0 + STATE_CHUNK] = hi

    ys = [_dot(hs_ref[:, h * half_st:(h + 1) * half_st].astype(BF16), cm_ref[h]) for h in range(2)]
    y = jnp.concatenate(ys, axis=1) + d_ref[...] * u
    y = _gelu(y)
    z = _dot(y.astype(BF16), wglu_ref[...]) + bglu_ref[...]
    y_ref[...] = y * _sigmoid(z)

    @pl.when(j == pl.num_programs(1) - 1)
    def _():
        hlast_ref[0] = hstate_ref[...]


def _ssm(u_tb, h0, lam, bm, cm, d, wglu, bglu, *, n_groups, n_time, tc):
    rows = tc * SUBLANES
    nt = n_time // tc
    full = lambda a: pl.BlockSpec(a.shape, lambda g, j: (0,) * a.ndim)
    st = pl.BlockSpec((1, SUBLANES, 2 * N_STATE), lambda g, j: (g, 0, 0))
    return pl.pallas_call(
        functools.partial(_ssm_kernel, tc=tc),
        grid=(n_groups, nt),
        in_specs=[pl.BlockSpec((rows, SSM_WIDTH), lambda g, j: (g * nt + j, 0)), st,
                  full(lam), full(bm), full(cm), full(d), full(wglu), full(bglu)],
        out_specs=[pl.BlockSpec((rows, SSM_WIDTH), lambda g, j: (g * nt + j, 0)), st],
        out_shape=[jax.ShapeDtypeStruct((n_groups * n_time * SUBLANES, SSM_WIDTH), F32),
                   jax.ShapeDtypeStruct((n_groups, SUBLANES, 2 * N_STATE), F32)],
        scratch_shapes=[pltpu.VMEM((rows, 2 * N_STATE), F32), pltpu.VMEM((SUBLANES, 2 * N_STATE), F32)],
        compiler_params=_params("parallel", "arbitrary"),
        name="ssm",
    )(u_tb, h0, lam, bm, cm, d, wglu, bglu)


def _state_lanes(re, im):
    lead = re.shape[:-1]
    r = re.reshape(lead + (N_CHUNKS, 1, STATE_CHUNK))
    i = im.reshape(lead + (N_CHUNKS, 1, STATE_CHUNK))
    return jnp.concatenate([r, i], axis=-2).reshape(lead + (2 * N_STATE,))


def _state_unlanes(x):
    lead = x.shape[:-1]
    y = x.reshape(lead + (N_CHUNKS, 2, STATE_CHUNK))
    return y[..., 0, :].reshape(lead + (N_STATE,)), y[..., 1, :].reshape(lead + (N_STATE,))


def _ssm_params(lam_re, lam_im, log_dt, b_re, b_im, c_re, c_im):
    lam = lax.complex(lam_re.astype(F32), lam_im.astype(F32))
    dt = jnp.exp(log_dt.astype(F32))[:, None]
    lam_bar = jnp.exp(lam * dt)
    b = lax.complex(b_re.astype(F32), b_im.astype(F32))
    b_bar = ((lam_bar - 1.0) / lam)[..., None] * b
    eye = jnp.eye(SSM_GROUPS, dtype=F32)

    def in_blockdiag(x):
        return jnp.einsum('gpc,gh->gchp', x, eye).reshape(SSM_WIDTH, N_STATE)

    def out_blockdiag(x):
        return jnp.einsum('gcp,gh->gphc', x, eye).reshape(N_STATE, SSM_WIDTH)

    b_full = _state_lanes(in_blockdiag(jnp.real(b_bar)), in_blockdiag(jnp.imag(b_bar)))
    c_full = _state_lanes(out_blockdiag(c_re.astype(F32)).T, -out_blockdiag(c_im.astype(F32)).T).T
    hw, hs = SSM_WIDTH // 2, N_STATE
    bm = jnp.stack([b_full[h * hw:(h + 1) * hw, h * hs:(h + 1) * hs] for h in range(2)]).astype(BF16)
    cm = jnp.stack([c_full[h * hs:(h + 1) * hs, h * hw:(h + 1) * hw] for h in range(2)]).astype(BF16)
    lam_l = _state_lanes(jnp.real(lam_bar).reshape(1, N_STATE), jnp.imag(lam_bar).reshape(1, N_STATE))
    return lam_l, bm, cm


def _compress_kernel(x_ref, pe_ref, w1_ref, w2_ref, o_ref, *, tm):
    def body(sp, acc):
        s0 = 2 * sp
        xa = x_ref[pl.ds(s0, tm, stride=BLK), :] + pe_ref[0, pl.ds(s0, 1), :]
        xb = x_ref[pl.ds(s0 + 1, tm, stride=BLK), :] + pe_ref[0, pl.ds(s0 + 1, 1), :]
        lhs = jnp.concatenate([xa, xb], axis=1).astype(BF16)
        return acc + _dot(lhs, w1_ref[0, sp])

    acc = lax.fori_loop(0, BLK // 2, body, jnp.zeros((tm, KV_WIDTH), F32))
    o_ref[0] = _dot(_gelu(acc).astype(BF16), w2_ref[0])


def _compress(x2d, pe, w1, w2, *, n_blocks, tm):
    return pl.pallas_call(
        functools.partial(_compress_kernel, tm=tm),
        grid=(2, n_blocks // tm),
        in_specs=[pl.BlockSpec((tm * BLK, KV_WIDTH), lambda c, i: (i, c)),
                  pl.BlockSpec((1, BLK, KV_WIDTH), lambda c, i: (c, 0, 0)),
                  pl.BlockSpec((1, BLK // 2, 2 * KV_WIDTH, KV_WIDTH), lambda c, i: (c, 0, 0, 0)),
                  pl.BlockSpec((1, KV_WIDTH, KV_WIDTH), lambda c, i: (c, 0, 0))],
        out_specs=pl.BlockSpec((1, tm, KV_WIDTH), lambda c, i: (c, i, 0)),
        out_shape=jax.ShapeDtypeStruct((2, n_blocks, KV_WIDTH), F32),
        compiler_params=_params("parallel", "parallel"),
        name="compress",
    )(x2d, pe, w1, w2)


def _compress_params(pe_k, w1_k, w2_k, pe_v, w1_v, w2_v):
    def bd(w):
        z = jnp.zeros_like(w)
        return jnp.concatenate([jnp.concatenate([w, z], axis=-1), jnp.concatenate([z, w], axis=-1)], axis=-2)

    def one(pe, w1, w2):
        w1s = bd(w1.astype(F32).reshape(BLK, HEAD_DIM, HEAD_DIM))
        return (jnp.tile(pe.astype(F32), (1, NSA_KV)),
                w1s.reshape(BLK // 2, 2 * KV_WIDTH, KV_WIDTH).astype(BF16),
                bd(w2.astype(F32)).astype(BF16))

    k, v = one(pe_k, w1_k, w2_k), one(pe_v, w1_v, w2_v)
    return tuple(jnp.stack([a, b]) for a, b in zip(k, v))


def _stack_queries(q, nq):
    q = q.astype(F32)
    z = jnp.zeros((nq, HEAD_DIM), F32)
    rows = []
    for h in range(NSA_HEADS):
        blk = q[:, h * HEAD_DIM:(h + 1) * HEAD_DIM]
        rows.append(jnp.concatenate([blk, z] if h < NSA_REP else [z, blk], axis=1))
    return (jnp.concatenate(rows, axis=0) * (HEAD_DIM ** -0.5)).astype(BF16)


def _masked_softmax(s, valid):
    s = jnp.where(valid, s, NEG_INF)
    m = jnp.max(s, axis=-1, keepdims=True)
    p = jnp.exp(s - m) * valid.astype(F32)
    return p / jnp.maximum(jnp.sum(p, axis=-1, keepdims=True), 1e-30)


def _select_blocks(imp, n_ids, cur, nb):
    forced = (n_ids == 0) | (n_ids == cur) | (n_ids == cur - 1)
    imp = jnp.where(forced, FORCE_SCORE, imp)
    imp = jnp.where(n_ids <= cur, imp, -FORCE_SCORE)
    rank = jnp.zeros(imp.shape, F32)
    for m in range(nb):
        col = imp[:, m:m + 1]
        beats = (col > imp) | ((col == imp) & (n_ids > m))
        rank = rank + beats.astype(F32)
    return (rank < float(N_SEL)).astype(F32)


def _online_step(s, v, m_ref, l_ref, acc_ref):
    m_old = m_ref[...]
    m_new = jnp.maximum(m_old, jnp.max(s, axis=-1, keepdims=True))
    alpha = jnp.exp(m_old - m_new)
    p = jnp.exp(s - m_new)
    l_ref[...] = alpha * l_ref[...] + jnp.sum(p, axis=-1, keepdims=True)
    acc_ref[...] = alpha * acc_ref[...] + _dot(p.astype(BF16), v)
    m_ref[...] = m_new


def _online_reset(m_ref, l_ref, acc_ref):
    m_ref[...] = jnp.full(m_ref.shape, M_INIT, F32)
    l_ref[...] = jnp.zeros(l_ref.shape, F32)
    acc_ref[...] = jnp.zeros(acc_ref.shape, F32)


def _online_result(l_ref, acc_ref):
    return acc_ref[...] / jnp.maximum(l_ref[...], 1e-30)


def _combine_heads(gates, o_c, o_s, o_w, nq):
    outs = []
    for h in range(NSA_HEADS):
        rows = slice(h * nq, (h + 1) * nq)
        o = (gates[:, 3 * h:3 * h + 1] * o_c[rows] + gates[:, 3 * h + 1:3 * h + 2] * o_s[rows]
             + gates[:, 3 * h + 2:3 * h + 3] * o_w[rows])
        g = h // NSA_REP
        outs.append(o[:, g * HEAD_DIM:(g + 1) * HEAD_DIM])
    return jnp.concatenate(outs, axis=1)


def _nsa_prompt_kernel(q_ref, gate_ref, kc_ref, vc_ref, ks_ref, vs_ref, kw_ref, vw_ref, e_ref,
                       o_ref, bias_ref, m_ref, l_ref, acc_ref, *, nb):
    ci = pl.program_id(1)
    nq = BLK
    rows = NSA_HEADS * nq
    q2 = _stack_queries(q_ref[...], nq)
    q_pos = ci * BLK + lax.broadcasted_iota(jnp.int32, (rows, 1), 0) % nq

    n_ids = lax.broadcasted_iota(jnp.int32, (rows, nb), 1)
    s_c = _dot_nt(q2, kc_ref[0].astype(BF16))
    p_c = _masked_softmax(s_c, (n_ids + 1) * BLK - 1 <= q_pos)
    o_c = _dot(p_c.astype(BF16), vc_ref[0].astype(BF16))

    n_sel = lax.broadcasted_iota(jnp.int32, (nq, nb), 1)
    for g in range(NSA_KV):
        imp = p_c[g * NSA_REP * nq:(g * NSA_REP + 1) * nq]
        for r in range(1, NSA_REP):
            imp = imp + p_c[(g * NSA_REP + r) * nq:(g * NSA_REP + r + 1) * nq]
        sel = _select_blocks(imp, n_sel, ci, nb)
        keys = _dot(sel.astype(BF16), e_ref[...])
        bias = (keys - 1.0) * (-NEG_INF)
        for t in range(nb * BLK // KEY_TILE):
            bias_ref[g, t] = bias[:, t * KEY_TILE:(t + 1) * KEY_TILE]

    lane_pos = lax.broadcasted_iota(jnp.int32, (rows, KEY_TILE), 1)

    def sweep(k_ref, v_ref, t_lo, t_hi, use_bias, lo_off):
        _online_reset(m_ref, l_ref, acc_ref)

        def body(t, _):
            k0 = pl.multiple_of(t * KEY_TILE, KEY_TILE)
            s = _dot_nt(q2, k_ref[pl.ds(k0, KEY_TILE), :])
            if use_bias:
                b = jnp.concatenate([bias_ref[g, t] for g in range(NSA_KV) for _ in range(NSA_REP)], axis=0)
                s = s + b
            d = q_pos - (k0 + lane_pos)
            s = jnp.where((d >= 0) & (d < lo_off), s, NEG_INF)
            _online_step(s, v_ref[pl.ds(k0, KEY_TILE), :], m_ref, l_ref, acc_ref)
            return 0

        lax.fori_loop(t_lo, t_hi, body, 0)
        return _online_result(l_ref, acc_ref)

    blocks_per_tile = KEY_TILE // BLK
    t_hi = ci // blocks_per_tile + 1
    o_s = sweep(ks_ref, vs_ref, 0, t_hi, True, nb * BLK + BLK)
    w_lo = jnp.maximum(ci - WINDOW // BLK, 0) // blocks_per_tile
    o_w = sweep(kw_ref, vw_ref, w_lo, t_hi, False, WINDOW)

    o_ref[...] = _combine_heads(gate_ref[...], o_c, o_s, o_w, nq).astype(o_ref.dtype)


def _nsa_prompt(q, gates, cmp_kv, kvb, expand, *, n_seq, seq):
    nb = seq // BLK
    rows = NSA_HEADS * BLK
    chunk = lambda w_: pl.BlockSpec((BLK, w_), lambda b, c: (b * nb + c, 0))
    kv = lambda col: pl.BlockSpec((seq, KV_WIDTH), lambda b, c: (b, col))
    cmp_ = lambda which: pl.BlockSpec((1, nb, KV_WIDTH), lambda b, c: (which, b, 0))
    return pl.pallas_call(
        functools.partial(_nsa_prompt_kernel, nb=nb),
        grid=(n_seq, nb),
        in_specs=[chunk(NSA_WIDTH), chunk(LANES), cmp_(0), cmp_(1), kv(2), kv(3), kv(4), kv(5),
                  pl.BlockSpec(expand.shape, lambda b, c: (0, 0))],
        out_specs=chunk(NSA_WIDTH),
        out_shape=jax.ShapeDtypeStruct((n_seq * seq, NSA_WIDTH), BF16),
        scratch_shapes=[pltpu.VMEM((NSA_KV, nb * BLK // KEY_TILE, BLK, KEY_TILE), F32),
                        pltpu.VMEM((rows, 1), F32), pltpu.VMEM((rows, 1), F32),
                        pltpu.VMEM((rows, KV_WIDTH), F32)],
        compiler_params=_params("parallel", "arbitrary"),
        name="nsa_prompt",
    )(q, gates, cmp_kv, cmp_kv, kvb, kvb, kvb, kvb, expand)


def _nsa_sample_kernel(q_ref, gate_ref, kc_ref, vc_ref, ks_ref, vs_ref, nks_ref, nvs_ref,
                       wb_ref, nkw_ref, nvw_ref, e_ref, o_ref, bias_ref, m_ref, l_ref, acc_ref,
                       *, nq, past, nbp, tk):
    rows = NSA_HEADS * nq
    q2 = _stack_queries(q_ref[0], nq)
    q_pos = past + lax.broadcasted_iota(jnp.int32, (rows, 1), 0) % nq
    cur = past // BLK

    n_ids = lax.broadcasted_iota(jnp.int32, (rows, nbp), 1)
    s_c = _dot_nt(q2, kc_ref[0].astype(BF16))
    p_c = _masked_softmax(s_c, (n_ids + 1) * BLK - 1 <= q_pos)
    o_c = _dot(p_c.astype(BF16), vc_ref[0].astype(BF16))

    n_sel = lax.broadcasted_iota(jnp.int32, (nq, nbp), 1)
    sels = []
    for g in range(NSA_KV):
        imp = p_c[g * NSA_REP * nq:(g * NSA_REP + 1) * nq]
        for r in range(1, NSA_REP):
            imp = imp + p_c[(g * NSA_REP + r) * nq:(g * NSA_REP + r + 1) * nq]
        sel = _select_blocks(imp, n_sel, cur, cur + 1)
        sels.extend([sel] * NSA_REP)
    keys = _dot(jnp.concatenate(sels, axis=0).astype(BF16), e_ref[...])
    bias = (keys - 1.0) * (-NEG_INF)
    for t in range(past // tk):
        bias_ref[t] = bias[:, t * tk:(t + 1) * tk]

    _online_reset(m_ref, l_ref, acc_ref)

    def body(t, _):
        k0 = pl.multiple_of(t * tk, tk)
        s = _dot_nt(q2, ks_ref[pl.ds(k0, tk), :].astype(BF16)) + bias_ref[t]
        _online_step(s, vs_ref[pl.ds(k0, tk), :].astype(BF16), m_ref, l_ref, acc_ref)
        return 0

    lax.fori_loop(0, past // tk, body, 0)
    new_pos = past + lax.broadcasted_iota(jnp.int32, (rows, LANES), 1)
    s = _dot_nt(q2, nks_ref[0].astype(BF16)) + bias[:, past:past + LANES]
    s = jnp.where(new_pos <= q_pos, s, NEG_INF)
    _online_step(s, nvs_ref[0].astype(BF16), m_ref, l_ref, acc_ref)
    o_s = _online_result(l_ref, acc_ref)

    _online_reset(m_ref, l_ref, acc_ref)
    wlen = wb_ref.shape[1]
    w_pos = past - wlen + lax.broadcasted_iota(jnp.int32, (rows, wlen), 1)
    d = q_pos - w_pos
    s = _dot_nt(q2, wb_ref[0, :, 0:KV_WIDTH].astype(BF16))
    s = jnp.where((d >= 0) & (d < WINDOW) & (w_pos >= 0), s, NEG_INF)
    _online_step(s, wb_ref[0, :, KV_WIDTH:2 * KV_WIDTH].astype(BF16), m_ref, l_ref, acc_ref)
    d = q_pos - new_pos
    s = _dot_nt(q2, nkw_ref[0].astype(BF16))
    s = jnp.where((d >= 0) & (d < WINDOW), s, NEG_INF)
    _online_step(s, nvw_ref[0].astype(BF16), m_ref, l_ref, acc_ref)
    o_w = _online_result(l_ref, acc_ref)

    o_ref[0] = _combine_heads(gate_ref[0], o_c, o_s, o_w, nq).astype(o_ref.dtype)


def _nsa_sample(q3, gates3, kc, vc, past2d, new_rows, win_buf, expand, *, past, tk):
    n_seq, nq, _ = q3.shape
    nbp = kc.shape[1]
    rows = NSA_HEADS * nq
    wlen = win_buf.shape[1]
    per_seq = lambda a: pl.BlockSpec((1,) + a.shape[1:], lambda b: (b,) + (0,) * (a.ndim - 1))
    new = lambda col: pl.BlockSpec((1, LANES, KV_WIDTH), lambda b: (b, 0, col))
    return pl.pallas_call(
        functools.partial(_nsa_sample_kernel, nq=nq, past=past, nbp=nbp, tk=tk),
        grid=(n_seq,),
        in_specs=[per_seq(q3), per_seq(gates3), per_seq(kc), per_seq(vc),
                  pl.BlockSpec((past, KV_WIDTH), lambda b: (b, 2)),
                  pl.BlockSpec((past, KV_WIDTH), lambda b: (b, 3)),
                  new(2), new(3), per_seq(win_buf), new(4), new(5),
                  pl.BlockSpec(expand.shape, lambda b: (0, 0))],
        out_specs=pl.BlockSpec((1, nq, NSA_WIDTH), lambda b: (b, 0, 0)),
        out_shape=jax.ShapeDtypeStruct((n_seq, nq, NSA_WIDTH), F32),
        scratch_shapes=[pltpu.VMEM((past // tk, rows, tk), F32),
                        pltpu.VMEM((rows, 1), F32), pltpu.VMEM((rows, 1), F32),
                        pltpu.VMEM((rows, KV_WIDTH), F32)],
        compiler_params=_params("parallel"),
        name="nsa_sample",
    )(q3, gates3, kc, vc, past2d, past2d, new_rows, new_rows, win_buf, new_rows, new_rows, expand)


def _expand_matrix(nb, n_keys):
    return (jnp.arange(n_keys, dtype=jnp.int32)[None, :] // BLK
            == jnp.arange(nb, dtype=jnp.int32)[:, None]).astype(BF16)


def _merge_kernel(x_ref, ssm_ref, nsa_ref, gpre_ref, wm_ref, wbs_ref, wbn_ref, wo_ref, gpost_ref,
                  gx_ref, wxq_ref, x1_ref, qx_ref):
    x = x_ref[...]
    a = _rms(x, gpre_ref[...]).astype(BF16)
    g_ssm = _sigmoid(_dot(a, wm_ref[:, 0:D_MODEL]))
    g_nsa = _sigmoid(_dot(a, wm_ref[:, D_MODEL:2 * D_MODEL]))
    merged = (g_ssm * _dot(ssm_ref[...].astype(BF16), wbs_ref[...])
              + g_nsa * _dot(nsa_ref[...], wbn_ref[...]))
    x1 = x + _rms(_dot(merged.astype(BF16), wo_ref[...]), gpost_ref[...])
    x1_ref[...] = x1
    c = _rms(x1, gx_ref[...]).astype(BF16)
    qx_ref[...] = (_dot(c, wxq_ref[...]) * (X_HEAD_DIM ** -0.5)).astype(BF16)


def _merge(x2d, ssm_y, nsa_o, weights, *, tm, ssm_time_major, n_tab_blocks):
    n = x2d.shape[0]
    steps = n // tm
    row = lambda w_: pl.BlockSpec((tm, w_), lambda i: (i, 0))
    full = lambda a: pl.BlockSpec(a.shape, lambda i: (0,) * a.ndim)
    if ssm_time_major:
        ssm_spec = pl.BlockSpec((tm, SSM_WIDTH), lambda i: (i % n_tab_blocks, i // n_tab_blocks))
    else:
        ssm_spec = row(SSM_WIDTH)
    return pl.pallas_call(
        _merge_kernel,
        grid=(steps,),
        in_specs=[row(D_MODEL), ssm_spec, row(NSA_WIDTH)] + [full(w) for w in weights],
        out_specs=[row(D_MODEL), row(X_WIDTH)],
        out_shape=[jax.ShapeDtypeStruct((n, D_MODEL), F32), jax.ShapeDtypeStruct((n, X_WIDTH), BF16)],
        compiler_params=_params("parallel"),
        name="merge",
    )(x2d, ssm_y, nsa_o, *weights)


def _xattn_kernel(q_ref, k_ref, v_ref, o_ref):
    q = q_ref[0].astype(BF16)
    outs = []
    for h in range(X_HEADS):
        cols = slice(h * X_HEAD_DIM, (h + 1) * X_HEAD_DIM)
        s = _dot_nt(q[:, cols], k_ref[0, :, cols].astype(BF16))
        m = jnp.max(s, axis=-1, keepdims=True)
        p = jnp.exp(s - m)
        p = p / jnp.sum(p, axis=-1, keepdims=True)
        outs.append(_dot(p.astype(BF16), v_ref[0, :, cols].astype(BF16)))
    o_ref[0] = jnp.concatenate(outs, axis=1).astype(o_ref.dtype)


def _xattn(q3, mem_kv3, *, tq):
    n_seq, t, _ = q3.shape
    m_len = mem_kv3.shape[1]
    return pl.pallas_call(
        _xattn_kernel,
        grid=(n_seq, t // tq),
        in_specs=[pl.BlockSpec((1, tq, X_WIDTH), lambda b, i: (b, i, 0)),
                  pl.BlockSpec((1, m_len, X_WIDTH), lambda b, i: (b, 0, 0)),
                  pl.BlockSpec((1, m_len, X_WIDTH), lambda b, i: (b, 0, 1))],
        out_specs=pl.BlockSpec((1, tq, X_WIDTH), lambda b, i: (b, i, 0)),
        out_shape=jax.ShapeDtypeStruct((n_seq, t, X_WIDTH), q3.dtype),
        compiler_params=_params("parallel", "parallel"),
        name="xattn",
    )(q3, mem_kv3, mem_kv3)


def _mlp_kernel(x1_ref, o_ref, wxo_ref, gxp_ref, gm_ref, wup_ref, wdn_ref, gmp_ref, y_ref):
    x2 = x1_ref[...] + _rms(_dot(o_ref[...], wxo_ref[...]), gxp_ref[...])
    m = _rms(x2, gm_ref[...]).astype(BF16)
    hid = jnp.maximum(_dot(m, wup_ref[...]), 0.0)
    hid = (hid * hid).astype(BF16)
    y_ref[...] = x2 + _rms(_dot(hid, wdn_ref[...]), gmp_ref[...])


def _mlp(x1, o, weights, *, tm):
    n = x1.shape[0]
    row = lambda w_: pl.BlockSpec((tm, w_), lambda i: (i, 0))
    full = lambda a: pl.BlockSpec(a.shape, lambda i: (0,) * a.ndim)
    return pl.pallas_call(
        _mlp_kernel,
        grid=(n // tm,),
        in_specs=[row(D_MODEL), row(X_WIDTH)] + [full(w) for w in weights],
        out_specs=row(D_MODEL),
        out_shape=jax.ShapeDtypeStruct((n, D_MODEL), F32),
        compiler_params=_params("parallel"),
        name="mlp",
    )(x1, o, *weights)


def _memkv_kernel(m_ref, g_ref, w_ref, o_ref):
    o_ref[...] = _dot(_rms(m_ref[...], g_ref[...]).astype(BF16), w_ref[...])


def _memkv(mem2d, g, w, *, tm):
    n = mem2d.shape[0]
    return pl.pallas_call(
        _memkv_kernel,
        grid=(n // tm,),
        in_specs=[pl.BlockSpec((tm, D_MODEL), lambda i: (i, 0)),
                  pl.BlockSpec(g.shape, lambda i: (0, 0)), pl.BlockSpec(w.shape, lambda i: (0, 0))],
        out_specs=pl.BlockSpec((tm, 2 * X_WIDTH), lambda i: (i, 0)),
        out_shape=jax.ShapeDtypeStruct((n, 2 * X_WIDTH), F32),
        compiler_params=_params("parallel"),
        name="memkv",
    )(mem2d, g, w)


def _row(v):
    return v.astype(F32).reshape(1, -1)


def kernel(x_prompt, x_sample, cache_nsa_kv, cache_win_kv, state_ssm, cache_mem_kv, page_table, mem_prompt, g_mix_pre, w_in, ssm_lam_re, ssm_lam_im, ssm_log_dt, ssm_b_re, ssm_b_im, ssm_c_re, ssm_c_im, ssm_d, w_glu, b_glu, cmp_pe_k, w_cmpk1, w_cmpk2, cmp_pe_v, w_cmpv1, w_cmpv2, w_br_ssm, w_br_nsa, w_out, g_mix_post, g_x_pre, g_mem, w_xq, w_xk, w_xv, w_xo, g_x_post, g_mlp_pre, w_up, w_down, g_mlp_post):
    depth = w_in.shape[0]
    n_seq_p, seq, _ = x_prompt.shape
    n_seq_s, nq, _ = x_sample.shape
    past = page_table.shape[1] * PAGE_SIZE
    assert depth == 1 and seq % BLK == 0 and nq <= SUBLANES and past % BLK == 0
    assert n_seq_p == SUBLANES and n_seq_s % SUBLANES == 0

    y_p = x_prompt.reshape(n_seq_p * seq, D_MODEL)
    y_s = x_sample.reshape(n_seq_s * nq, D_MODEL)
    l = 0

    w_proj = w_in[l, :, :N_PROJ].astype(BF16)
    w_gate = jnp.pad(w_in[l, :, N_PROJ:N_PROJ + N_GATE], ((0, 0), (0, LANES - N_GATE))).astype(BF16)
    w_merge = w_in[l, :, N_PROJ + N_GATE:].astype(BF16)
    lam_l, bm, cm = _ssm_params(ssm_lam_re[l], ssm_lam_im[l], ssm_log_dt[l], ssm_b_re[l], ssm_b_im[l],
                                ssm_c_re[l], ssm_c_im[l])
    ssm_w = (lam_l, bm, cm, _row(ssm_d[l]), w_glu[l].astype(BF16), _row(b_glu[l]))
    cmp_w = _compress_params(cmp_pe_k[l], w_cmpk1[l], w_cmpk2[l], cmp_pe_v[l], w_cmpv1[l], w_cmpv2[l])
    merge_w = (_row(g_mix_pre[l]), w_merge, w_br_ssm[l].astype(BF16), w_br_nsa[l].astype(BF16),
               w_out[l].astype(BF16), _row(g_mix_post[l]), _row(g_x_pre[l]), w_xq[l].astype(BF16))
    mlp_w = (w_xo[l].astype(BF16), _row(g_x_post[l]), _row(g_mlp_pre[l]), w_up[l].astype(BF16),
             w_down[l].astype(BF16), _row(g_mlp_post[l]))
    w_mem = jnp.concatenate([w_xk[l], w_xv[l]], axis=1).astype(BF16)

    tm_p = 512 if seq % 512 == 0 else seq
    nt_p = seq // tm_p
    tabs_p = _rope_tables(jnp.arange(seq, dtype=jnp.int32))
    u_p, q_p, kv_p, win_p, kvb_p, gate_p = _proj(y_p, _row(g_mix_pre[l]), w_proj, w_gate, tabs_p,
                                                 tm=tm_p, n_tab_blocks=nt_p, u_time_major=True)
    u_p = u_p.reshape(seq * n_seq_p, SSM_WIDTH)
    h0_p = jnp.zeros((1, SUBLANES, 2 * N_STATE), F32)
    tc_p = 64 if seq % 64 == 0 else seq
    ssm_p, hl_p = _ssm(u_p, h0_p, *ssm_w, n_groups=1, n_time=seq, tc=tc_p)
    ssm_p = ssm_p.reshape(seq, n_seq_p * SSM_WIDTH)

    nb_p = seq // BLK
    n_blocks_p = n_seq_p * nb_p
    cmp_p = _compress(kv_p, *cmp_w, n_blocks=n_blocks_p, tm=min(n_blocks_p, 256))
    nsa_p = _nsa_prompt(q_p, gate_p, cmp_p, kvb_p, _expand_matrix(nb_p, seq), n_seq=n_seq_p, seq=seq)

    mem_kv_p = _memkv(mem_prompt.reshape(-1, D_MODEL), _row(g_mem[l]), w_mem, tm=256)
    m_len = mem_prompt.shape[1]
    mem_kv_p3 = mem_kv_p.reshape(n_seq_p, m_len, 2 * X_WIDTH)

    x1_p, qx_p = _merge(y_p, ssm_p, nsa_p, merge_w, tm=tm_p, ssm_time_major=True, n_tab_blocks=nt_p)
    o_p = _xattn(qx_p.reshape(n_seq_p, seq, X_WIDTH), mem_kv_p3, tq=tm_p)
    y_p = _mlp(x1_p, o_p.reshape(-1, X_WIDTH), mlp_w, tm=256)

    n_s = n_seq_s * nq
    pos_s = past + jnp.arange(nq, dtype=jnp.int32)
    tabs_s = tuple(jnp.tile(t, (n_seq_s, 1)) for t in _rope_tables(pos_s))
    u_s, q_s, kv_s, win_s, kvb_s, gate_s = _proj(y_s, _row(g_mix_pre[l]), w_proj, w_gate, tabs_s,
                                                 tm=n_s, n_tab_blocks=1, u_time_major=False)
    n_grp = n_seq_s // SUBLANES
    to_tb = lambda a: a.reshape(n_grp, SUBLANES, nq, SSM_WIDTH).transpose(0, 2, 1, 3).reshape(n_s, SSM_WIDTH)
    st = state_ssm[l].astype(F32).reshape(n_seq_s, N_STATE, 2)
    h0_s = _state_lanes(st[..., 0], st[..., 1]).reshape(n_grp, SUBLANES, 2 * N_STATE)
    ssm_s, hl_s = _ssm(to_tb(u_s), h0_s, *ssm_w, n_groups=n_grp, n_time=nq, tc=nq)
    ssm_s = ssm_s.reshape(n_grp, nq, SUBLANES, SSM_WIDTH).transpose(0, 2, 1, 3).reshape(n_s, SSM_WIDTH)

    n_pages = page_table.shape[1]
    past2d = cache_nsa_kv[l][page_table].reshape(n_seq_s * past, 4 * KV_WIDTH)
    nb_past = past // BLK
    cmp_past = _compress(past2d, *cmp_w, n_blocks=n_seq_s * nb_past, tm=min(n_seq_s * nb_past, 256))
    new_rows = jnp.pad(jnp.concatenate([kv_s, win_s], axis=1).reshape(n_seq_s, nq, 6 * KV_WIDTH),
                       ((0, 0), (0, LANES - nq), (0, 0)))
    cmp_new = _compress(new_rows[:, :BLK].reshape(n_seq_s * BLK, 6 * KV_WIDTH), *cmp_w,
                        n_blocks=n_seq_s, tm=n_seq_s)
    nbp = -(-(nb_past + 1) // LANES) * LANES
    cmp_s = jnp.concatenate([cmp_past.reshape(2, n_seq_s, nb_past, KV_WIDTH), cmp_new[:, :, None, :],
                             jnp.zeros((2, n_seq_s, nbp - nb_past - 1, KV_WIDTH), F32)], axis=2)
    win_buf = cache_win_kv[l].reshape(n_seq_s, -1, 2 * KV_WIDTH)
    nsa_s = _nsa_sample(q_s.astype(F32).reshape(n_seq_s, nq, NSA_WIDTH), gate_s.reshape(n_seq_s, nq, LANES),
                        cmp_s[0], cmp_s[1], past2d, new_rows, win_buf, _expand_matrix(nbp, past + LANES),
                        past=past, tk=min(past, 1024))

    x1_s, qx_s = _merge(y_s, ssm_s, nsa_s.reshape(n_s, NSA_WIDTH).astype(BF16), merge_w, tm=n_s,
                        ssm_time_major=False, n_tab_blocks=1)
    mem_kv_s3 = cache_mem_kv[l].reshape(n_seq_s, m_len, 2 * X_WIDTH)
    o_s = _xattn(qx_s.astype(F32).reshape(n_seq_s, nq, X_WIDTH), mem_kv_s3, tq=nq)
    y_s = _mlp(x1_s, o_s.reshape(-1, X_WIDTH).astype(BF16), mlp_w, tm=n_s)

    def ssm_state(hl, n_seq):
        re, im = _state_unlanes(hl.reshape(n_seq, 2 * N_STATE))
        return jnp.stack([re, im], axis=-1).reshape(1, n_seq, SSM_GROUPS, SSM_STATE, 2)

    w_keep = min(WINDOW, seq)
    win_prompt = win_p.reshape(n_seq_p, seq, 2, NSA_KV, HEAD_DIM)[:, seq - w_keep:]
    win_new = win_s.reshape(n_seq_s, nq, 2, NSA_KV, HEAD_DIM).astype(cache_win_kv.dtype)
    win_sample = jnp.concatenate([cache_win_kv[l], win_new], axis=1)[:, nq:]
    return (y_p.reshape(n_seq_p, seq, D_MODEL),
            y_s.reshape(n_seq_s, nq, D_MODEL),
            kv_p.reshape(1, n_seq_p, seq, 4, NSA_KV, HEAD_DIM),
            kv_s.reshape(1, n_seq_s, nq, 4, NSA_KV, HEAD_DIM),
            win_prompt[None],
            win_sample[None],
            ssm_state(hl_p, n_seq_p),
            ssm_state(hl_s, n_seq_s),
            mem_kv_p.reshape(1, n_seq_p, m_len, 2, X_HEADS, X_HEAD_DIM))
```

```python
import functools
import math

import jax
import jax.numpy as jnp
from jax import lax
from jax.experimental import pallas as pl
from jax.experimental.pallas import tpu as pltpu

F32 = jnp.float32
BF16 = jnp.bfloat16

D_MODEL = 1024
SSM_WIDTH = 512
SSM_GROUP = 16
SSM_GROUPS = 32
SSM_STATE = 64
N_STATE = SSM_GROUPS * SSM_STATE
STATE_CHUNK = 512
N_CHUNKS = N_STATE // STATE_CHUNK
NSA_HEADS = 8
HEAD_DIM = 64
NSA_WIDTH = NSA_HEADS * HEAD_DIM
NSA_KV = 2
NSA_REP = NSA_HEADS // NSA_KV
KV_WIDTH = NSA_KV * HEAD_DIM
BLK = 64
N_SEL = 16
WINDOW = 512
ROT_DIM = 16
ROPE_THETA = 500000.0
PAGE_SIZE = 128
X_HEADS = 4
X_HEAD_DIM = 128
X_WIDTH = X_HEADS * X_HEAD_DIM
D_FF = 4 * D_MODEL
EPS = 1e-6
NEG_INF = -1e30
M_INIT = -1e29
FORCE_SCORE = 1e4
LANES = 128
SUBLANES = 8
VMEM_LIMIT = 56 * 1024 * 1024

N_PROJ = SSM_WIDTH + NSA_WIDTH + 6 * KV_WIDTH
N_GATE = 3 * NSA_HEADS
SEL_CHUNK = 512


def _params(*sem):
    return pltpu.CompilerParams(dimension_semantics=sem, vmem_limit_bytes=VMEM_LIMIT)


def _rms(x, g):
    return x * lax.rsqrt(jnp.mean(x * x, axis=-1, keepdims=True) + EPS) * g


def _gelu(x):
    return 0.5 * x * (1.0 + jnp.tanh(math.sqrt(2.0 / math.pi) * (x + 0.044715 * (x * x * x))))


def _sigmoid(x):
    return 1.0 / (1.0 + jnp.exp(-x))


def _dot(a, b):
    return jnp.dot(a, b, preferred_element_type=F32)


def _dot_nt(a, b):
    return lax.dot_general(a, b, (((1,), (1,)), ((), ())), preferred_element_type=F32)


def _proj_kernel(x_ref, g_ref, w_ref, wg_ref, cos_ref, sp_ref, sm_ref,
                 u_ref, q_ref, kv_ref, win_ref, kvb_ref, gate_ref):
    a = _rms(x_ref[...], g_ref[...]).astype(BF16)
    cos, sp, sm = cos_ref[...], sp_ref[...], sm_ref[...]

    def rope(blk):
        return blk * cos + pltpu.roll(blk, 8, 1) * sp + pltpu.roll(blk, LANES - 8, 1) * sm

    u_ref[...] = _dot(a, w_ref[:, 0:SSM_WIDTH])
    for j in range(NSA_WIDTH // LANES):
        c0 = SSM_WIDTH + j * LANES
        q_ref[:, j * LANES:(j + 1) * LANES] = rope(_dot(a, w_ref[:, c0:c0 + LANES])).astype(BF16)
    for j in range(6):
        c0 = SSM_WIDTH + NSA_WIDTH + j * LANES
        blk = _dot(a, w_ref[:, c0:c0 + LANES])
        if j % 2 == 0:
            blk = rope(blk)
        if j < 4:
            kv_ref[:, j * LANES:(j + 1) * LANES] = blk
        else:
            win_ref[:, (j - 4) * LANES:(j - 3) * LANES] = blk
        kvb_ref[:, j * LANES:(j + 1) * LANES] = blk.astype(BF16)
    gate_ref[...] = _sigmoid(_dot(a, wg_ref[...]))


def _proj(x2d, g, w, wg, tabs, *, tm, n_tab_blocks, u_time_major):
    n = x2d.shape[0]
    steps = n // tm
    if u_time_major:
        n_b = steps // n_tab_blocks
        u_shape = jax.ShapeDtypeStruct((n_tab_blocks * tm, n_b * SSM_WIDTH), F32)
        u_spec = pl.BlockSpec((tm, SSM_WIDTH), lambda i: (i % n_tab_blocks, i // n_tab_blocks))
    else:
        u_shape = jax.ShapeDtypeStruct((n, SSM_WIDTH), F32)
        u_spec = pl.BlockSpec((tm, SSM_WIDTH), lambda i: (i, 0))
    row = lambda w_: pl.BlockSpec((tm, w_), lambda i: (i, 0))
    full = lambda a: pl.BlockSpec(a.shape, lambda i: (0,) * a.ndim)
    tab = pl.BlockSpec((tm, LANES), lambda i: (i % n_tab_blocks, 0))
    return pl.pallas_call(
        _proj_kernel,
        grid=(steps,),
        in_specs=[row(D_MODEL), full(g), full(w), full(wg), tab, tab, tab],
        out_specs=[u_spec, row(NSA_WIDTH), row(4 * KV_WIDTH), row(2 * KV_WIDTH), row(6 * KV_WIDTH), row(LANES)],
        out_shape=[u_shape,
                   jax.ShapeDtypeStruct((n, NSA_WIDTH), BF16),
                   jax.ShapeDtypeStruct((n, 4 * KV_WIDTH), F32),
                   jax.ShapeDtypeStruct((n, 2 * KV_WIDTH), F32),
                   jax.ShapeDtypeStruct((n, 6 * KV_WIDTH), BF16),
                   jax.ShapeDtypeStruct((n, LANES), F32)],
        compiler_params=_params("parallel"),
        name="proj",
    )(x2d, g, w, wg, *tabs)


def _rope_tables(pos):
    half = ROT_DIM // 2
    freqs = ROPE_THETA ** (-jnp.arange(half, dtype=F32) / half)
    ang = pos.astype(F32)[:, None] * freqs[None, :]
    cos, sin = jnp.cos(ang), jnp.sin(ang)
    r = pos.shape[0]
    z8 = jnp.zeros((r, half), F32)
    rest0 = jnp.zeros((r, HEAD_DIM - ROT_DIM), F32)
    rest1 = jnp.ones((r, HEAD_DIM - ROT_DIM), F32)
    c64 = jnp.concatenate([cos, cos, rest1], axis=1)
    sp64 = jnp.concatenate([z8, sin, rest0], axis=1)
    sm64 = jnp.concatenate([-sin, z8, rest0], axis=1)
    return tuple(jnp.tile(t, (1, LANES // HEAD_DIM)) for t in (c64, sp64, sm64))


def _ssm_kernel(u_ref, h0_ref, lam_ref, bm_ref, cm_ref, d_ref, wglu_ref, bglu_ref,
                y_ref, hlast_ref, hs_ref, hstate_ref, *, tc):
    j = pl.program_id(1)

    @pl.when(j == 0)
    def _():
        hstate_ref[...] = h0_ref[0]

    u = u_ref[...]
    ub = u.astype(BF16)
    half_in = SSM_WIDTH // 2
    half_st = N_STATE
    for h in range(2):
        hs_ref[:, h * half_st:(h + 1) * half_st] = _dot(ub[:, h * half_in:(h + 1) * half_in], bm_ref[h])

    for c in range(N_CHUNKS):
        re0 = c * 2 * STATE_CHUNK
        im0 = re0 + STATE_CHUNK
        lr = jnp.broadcast_to(lam_ref[0:1, re0:re0 + STATE_CHUNK], (SUBLANES, STATE_CHUNK))
        li = jnp.broadcast_to(lam_ref[0:1, im0:im0 + STATE_CHUNK], (SUBLANES, STATE_CHUNK))

        def step(t, carry, re0=re0, im0=im0, lr=lr, li=li):
            hr, hi = carry
            r0 = pl.multiple_of(t * SUBLANES, SUBLANES)
            nr = lr * hr - li * hi + hs_ref[pl.ds(r0, SUBLANES), re0:re0 + STATE_CHUNK]
            ni = lr * hi + li * hr + hs_ref[pl.ds(r0, SUBLANES), im0:im0 + STATE_CHUNK]
            hs_ref[pl.ds(r0, SUBLANES), re0:re0 + STATE_CHUNK] = nr
            hs_ref[pl.ds(r0, SUBLANES), im0:im0 + STATE_CHUNK] = ni
            return nr, ni

        hr, hi = lax.fori_loop(0, tc, step,
                               (hstate_ref[:, re0:re0 + STATE_CHUNK], hstate_ref[:, im0:im0 + STATE_CHUNK]),
                               unroll=min(tc, 4))
        hstate_ref[:, re0:re0 + STATE_CHUNK] = hr
        hstate_ref[:, im0:im0 + STATE_CHUNK] = hi

    ys = [_dot(hs_ref[:, h * half_st:(h + 1) * half_st].astype(BF16), cm_ref[h]) for h in range(2)]
    y = jnp.concatenate(ys, axis=1) + d_ref[...] * u
    y = _gelu(y)
    z = _dot(y.astype(BF16), wglu_ref[...]) + bglu_ref[...]
    y_ref[...] = y * _sigmoid(z)

    @pl.when(j == pl.num_programs(1) - 1)
    def _():
        hlast_ref[0] = hstate_ref[...]


def _ssm(u_tb, h0, lam, bm, cm, d, wglu, bglu, *, n_groups, n_time, tc):
    rows = tc * SUBLANES
    nt = n_time // tc
    full = lambda a: pl.BlockSpec(a.shape, lambda g, j: (0,) * a.ndim)
    st = pl.BlockSpec((1, SUBLANES, 2 * N_STATE), lambda g, j: (g, 0, 0))
    return pl.pallas_call(
        functools.partial(_ssm_kernel, tc=tc),
        grid=(n_groups, nt),
        in_specs=[pl.BlockSpec((rows, SSM_WIDTH), lambda g, j: (g * nt + j, 0)), st,
                  full(lam), full(bm), full(cm), full(d), full(wglu), full(bglu)],
        out_specs=[pl.BlockSpec((rows, SSM_WIDTH), lambda g, j: (g * nt + j, 0)), st],
        out_shape=[jax.ShapeDtypeStruct((n_groups * n_time * SUBLANES, SSM_WIDTH), F32),
                   jax.ShapeDtypeStruct((n_groups, SUBLANES, 2 * N_STATE), F32)],
        scratch_shapes=[pltpu.VMEM((rows, 2 * N_STATE), F32), pltpu.VMEM((SUBLANES, 2 * N_STATE), F32)],
        compiler_params=_params("parallel", "arbitrary"),
        name="ssm",
    )(u_tb, h0, lam, bm, cm, d, wglu, bglu)


def _state_lanes(re, im):
    lead = re.shape[:-1]
    r = re.reshape(lead + (N_CHUNKS, 1, STATE_CHUNK))
    i = im.reshape(lead + (N_CHUNKS, 1, STATE_CHUNK))
    return jnp.concatenate([r, i], axis=-2).reshape(lead + (2 * N_STATE,))


def _state_unlanes(x):
    lead = x.shape[:-1]
    y = x.reshape(lead + (N_CHUNKS, 2, STATE_CHUNK))
    return y[..., 0, :].reshape(lead + (N_STATE,)), y[..., 1, :].reshape(lead + (N_STATE,))


def _ssm_params(lam_re, lam_im, log_dt, b_re, b_im, c_re, c_im):
    lam = lax.complex(lam_re.astype(F32), lam_im.astype(F32))
    dt = jnp.exp(log_dt.astype(F32))[:, None]
    lam_bar = jnp.exp(lam * dt)
    b = lax.complex(b_re.astype(F32), b_im.astype(F32))
    b_bar = ((lam_bar - 1.0) / lam)[..., None] * b
    eye = jnp.eye(SSM_GROUPS, dtype=F32)

    def in_blockdiag(x):
        return jnp.einsum('gpc,gh->gchp', x, eye).reshape(SSM_WIDTH, N_STATE)

    def out_blockdiag(x):
        return jnp.einsum('gcp,gh->gphc', x, eye).reshape(N_STATE, SSM_WIDTH)

    b_full = _state_lanes(in_blockdiag(jnp.real(b_bar)), in_blockdiag(jnp.imag(b_bar)))
    c_full = _state_lanes(out_blockdiag(c_re.astype(F32)).T, -out_blockdiag(c_im.astype(F32)).T).T
    hw, hs = SSM_WIDTH // 2, N_STATE
    bm = jnp.stack([b_full[h * hw:(h + 1) * hw, h * hs:(h + 1) * hs] for h in range(2)]).astype(BF16)
    cm = jnp.stack([c_full[h * hs:(h + 1) * hs, h * hw:(h + 1) * hw] for h in range(2)]).astype(BF16)
    lam_l = _state_lanes(jnp.real(lam_bar).reshape(1, N_STATE), jnp.imag(lam_bar).reshape(1, N_STATE))
    return lam_l, bm, cm


def _compress_kernel(x_ref, pe_ref, w1_ref, w2_ref, o_ref, *, tm):
    def body(sp, acc):
        s0 = 2 * sp
        xa = x_ref[pl.ds(s0, tm, stride=BLK), :] + pe_ref[0, pl.ds(s0, 1), :]
        xb = x_ref[pl.ds(s0 + 1, tm, stride=BLK), :] + pe_ref[0, pl.ds(s0 + 1, 1), :]
        lhs = jnp.concatenate([xa, xb], axis=1).astype(BF16)
        return acc + _dot(lhs, w1_ref[0, sp])

    acc = lax.fori_loop(0, BLK // 2, body, jnp.zeros((tm, KV_WIDTH), F32))
    o_ref[0] = _dot(_gelu(acc).astype(BF16), w2_ref[0])


def _compress(x2d, pe, w1, w2, *, n_blocks, tm):
    return pl.pallas_call(
        functools.partial(_compress_kernel, tm=tm),
        grid=(2, n_blocks // tm),
        in_specs=[pl.BlockSpec((tm * BLK, KV_WIDTH), lambda c, i: (i, c)),
                  pl.BlockSpec((1, BLK, KV_WIDTH), lambda c, i: (c, 0, 0)),
                  pl.BlockSpec((1, BLK // 2, 2 * KV_WIDTH, KV_WIDTH), lambda c, i: (c, 0, 0, 0)),
                  pl.BlockSpec((1, KV_WIDTH, KV_WIDTH), lambda c, i: (c, 0, 0))],
        out_specs=pl.BlockSpec((1, tm, KV_WIDTH), lambda c, i: (c, i, 0)),
        out_shape=jax.ShapeDtypeStruct((2, n_blocks, KV_WIDTH), F32),
        compiler_params=_params("parallel", "parallel"),
        name="compress",
    )(x2d, pe, w1, w2)


def _compress_params(pe_k, w1_k, w2_k, pe_v, w1_v, w2_v):
    def bd(w):
        z = jnp.zeros_like(w)
        return jnp.concatenate([jnp.concatenate([w, z], axis=-1), jnp.concatenate([z, w], axis=-1)], axis=-2)

    def one(pe, w1, w2):
        w1s = bd(w1.astype(F32).reshape(BLK, HEAD_DIM, HEAD_DIM))
        return (jnp.tile(pe.astype(F32), (1, NSA_KV)),
                w1s.reshape(BLK // 2, 2 * KV_WIDTH, KV_WIDTH).astype(BF16),
                bd(w2.astype(F32)).astype(BF16))

    k, v = one(pe_k, w1_k, w2_k), one(pe_v, w1_v, w2_v)
    return tuple(jnp.stack([a, b]) for a, b in zip(k, v))


def _stack_queries(q, nq):
    q = q.astype(F32)
    z = jnp.zeros((nq, HEAD_DIM), F32)
    rows = []
    for h in range(NSA_HEADS):
        blk = q[:, h * HEAD_DIM:(h + 1) * HEAD_DIM]
        rows.append(jnp.concatenate([blk, z] if h < NSA_REP else [z, blk], axis=1))
    return (jnp.concatenate(rows, axis=0) * (HEAD_DIM ** -0.5)).astype(BF16)


def _masked_softmax(s, valid):
    s = jnp.where(valid, s, NEG_INF)
    m = jnp.max(s, axis=-1, keepdims=True)
    p = jnp.exp(s - m) * valid.astype(F32)
    return p / jnp.maximum(jnp.sum(p, axis=-1, keepdims=True), 1e-30)


def _select_blocks(imp, n_ids, cur, nb):
    forced = (n_ids == 0) | (n_ids == cur) | (n_ids == cur - 1)
    imp = jnp.where(forced, FORCE_SCORE, imp)
    imp = jnp.where(n_ids <= cur, imp, -FORCE_SCORE)
    rank = jnp.zeros(imp.shape, F32)
    for m in range(nb):
        col = imp[:, m:m + 1]
        beats = (col > imp) | ((col == imp) & (n_ids > m))
        rank = rank + beats.astype(F32)
    return (rank < float(N_SEL)).astype(F32)


def _select_blocks_t(imp_t, cur, nb):
    n_t = lax.broadcasted_iota(jnp.int32, imp_t.shape, 0)
    forced = (n_t == 0) | (n_t == cur) | (n_t == cur - 1)
    imp_t = jnp.where(forced, FORCE_SCORE, imp_t)
    imp_t = jnp.where(n_t <= cur, imp_t, -FORCE_SCORE)
    rank = jnp.zeros(imp_t.shape, F32)
    for m in range(nb):
        row = imp_t[m:m + 1, :]
        beats = (row > imp_t) | ((row == imp_t) & (n_t > m))
        rank = rank + beats.astype(F32)
    return (rank < float(N_SEL)).astype(F32)


def _online_step(s, v, m_ref, l_ref, acc_ref):
    m_old = m_ref[...]
    m_new = jnp.maximum(m_old, jnp.max(s, axis=-1, keepdims=True))
    alpha = jnp.exp(m_old - m_new)
    reps = s.shape[1] // LANES
    p = jnp.exp(s - (m_new if reps == 1 else jnp.concatenate([m_new] * reps, axis=1)))
    l_ref[...] = alpha * l_ref[...] + jnp.sum(p, axis=-1, keepdims=True)
    acc_ref[...] = alpha * acc_ref[...] + _dot(p.astype(BF16), v)
    m_ref[...] = m_new


def _online_reset(m_ref, l_ref, acc_ref):
    m_ref[...] = jnp.full(m_ref.shape, M_INIT, F32)
    l_ref[...] = jnp.zeros(l_ref.shape, F32)
    acc_ref[...] = jnp.zeros(acc_ref.shape, F32)


def _online_result(l_ref, acc_ref):
    return acc_ref[...] / jnp.maximum(l_ref[...], 1e-30)


def _combine_heads(gates, o_c, o_s, o_w, nq):
    outs = []
    for h in range(NSA_HEADS):
        rows = slice(h * nq, (h + 1) * nq)
        o = (gates[:, 3 * h:3 * h + 1] * o_c[rows] + gates[:, 3 * h + 1:3 * h + 2] * o_s[rows]
             + gates[:, 3 * h + 2:3 * h + 3] * o_w[rows])
        g = h // NSA_REP
        outs.append(o[:, g * HEAD_DIM:(g + 1) * HEAD_DIM])
    return jnp.concatenate(outs, axis=1)


def _nsa_prompt_kernel(q_ref, gate_ref, kc_ref, vc_ref, ks_ref, vs_ref, kw_ref, vw_ref, e_ref,
                       o_ref, bias_ref, s_ref, mx_ref, acc_ref, *, nb, ch, wk):
    ci = pl.program_id(1)
    nq = BLK
    rows = NSA_HEADS * nq
    q2 = _stack_queries(q_ref[...], nq)
    q_pos = ci * BLK + lax.broadcasted_iota(jnp.int32, (rows, 1), 0) % nq

    n_ids = lax.broadcasted_iota(jnp.int32, (rows, nb), 1)
    s_c = _dot_nt(q2, kc_ref[0].astype(BF16))
    p_c = _masked_softmax(s_c, (n_ids + 1) * BLK - 1 <= q_pos)
    o_c = _dot(p_c.astype(BF16), vc_ref[0].astype(BF16))

    imps = []
    for g in range(NSA_KV):
        imp = p_c[g * NSA_REP * nq:(g * NSA_REP + 1) * nq]
        for r in range(1, NSA_REP):
            imp = imp + p_c[(g * NSA_REP + r) * nq:(g * NSA_REP + r + 1) * nq]
        imps.append(imp)
    imp2 = jnp.concatenate([jnp.concatenate(imps, axis=0), jnp.zeros((LANES, LANES - nb), F32)], axis=1)
    sel_t = _select_blocks_t(imp2.T[0:nb], ci, nb)
    sel2 = jnp.concatenate([sel_t, jnp.zeros((LANES - nb, LANES), F32)], axis=0).T
    bias = (_dot(sel2.astype(BF16), e_ref[...]) - 1.0) * (-NEG_INF)
    for c in range(nb * BLK // ch):
        bias_ref[c] = bias[:, c * ch:(c + 1) * ch]

    d0 = (ci * BLK + lax.broadcasted_iota(jnp.int32, (rows, LANES), 0) % nq
          - lax.broadcasted_iota(jnp.int32, (rows, LANES), 1))

    def col_blocks(s):
        return [s[:, j * LANES:(j + 1) * LANES] for j in range(s.shape[1] // LANES)]

    def col_max(cols):
        m = cols[0]
        for c_ in cols[1:]:
            m = jnp.maximum(m, c_)
        return m

    def finish(acc):
        return acc[:, 0:KV_WIDTH] / jnp.maximum(acc[:, KV_WIDTH:2 * KV_WIDTH], 1e-30)

    def with_ones(v):
        return jnp.concatenate([v, jnp.ones((v.shape[0], LANES), BF16)], axis=1)

    n_ch = ci // (ch // BLK) + 1
    mx_ref[...] = jnp.full(mx_ref.shape, M_INIT, F32)

    def scores(c, causal):
        k0 = pl.multiple_of(c * ch, ch)
        s = _dot_nt(q2, ks_ref[pl.ds(k0, ch), :])
        s = s + jnp.concatenate([bias_ref[c, g * nq:(g + 1) * nq, :]
                                 for g in range(NSA_KV) for _ in range(NSA_REP)], axis=0)
        cols = col_blocks(s)
        if causal:
            cols = [jnp.where(d0 >= k0 + j * LANES, c_, NEG_INF) for j, c_ in enumerate(cols)]
        mx_ref[...] = jnp.maximum(mx_ref[...], col_max(cols))
        for j, c_ in enumerate(cols):
            s_ref[c, :, j * LANES:(j + 1) * LANES] = c_

    def pass1(c, _):
        scores(c, False)
        return 0

    lax.fori_loop(0, n_ch - 1, pass1, 0)
    scores(n_ch - 1, True)

    m_sel = jnp.broadcast_to(jnp.max(mx_ref[...], axis=-1, keepdims=True), (rows, LANES))
    acc_ref[...] = jnp.zeros(acc_ref.shape, F32)

    def pass2(c, _):
        k0 = pl.multiple_of(c * ch, ch)
        p = jnp.concatenate([jnp.exp(s_ref[c, :, j * LANES:(j + 1) * LANES] - m_sel).astype(BF16)
                             for j in range(ch // LANES)], axis=1)
        acc_ref[...] += _dot(p, with_ones(vs_ref[pl.ds(k0, ch), :]))
        return 0

    lax.fori_loop(0, n_ch, pass2, 0)
    o_s = finish(acc_ref[...])

    first = jnp.maximum(ci - WINDOW // BLK, 0) // (LANES // BLK)
    w0 = pl.multiple_of(first * LANES, LANES)
    cols = col_blocks(_dot_nt(q2, kw_ref[pl.ds(w0, wk), :]))
    last_col = len(cols) - 1
    masked = []
    for j, c_ in enumerate(cols):
        d = d0 - (w0 + j * LANES)
        if j == 0 or j == last_col:
            masked.append(jnp.where((d >= 0) & (d < WINDOW), c_, NEG_INF))
        else:
            masked.append(jnp.where(d >= 0, c_, NEG_INF))
    m_w = jnp.maximum(jnp.max(col_max(masked), axis=-1, keepdims=True), M_INIT)
    m_w = jnp.broadcast_to(m_w, (rows, LANES))
    p = jnp.concatenate([jnp.exp(c_ - m_w).astype(BF16) for c_ in masked], axis=1)
    o_w = finish(_dot(p, with_ones(vw_ref[pl.ds(w0, wk), :])))

    o_ref[...] = _combine_heads(gate_ref[...], o_c, o_s, o_w, nq).astype(o_ref.dtype)


def _nsa_prompt(q, gates, cmp_kv, kvb, expand, *, n_seq, seq):
    nb = seq // BLK
    rows = NSA_HEADS * BLK
    chunk = lambda w_: pl.BlockSpec((BLK, w_), lambda b, c: (b * nb + c, 0))
    kv = lambda col: pl.BlockSpec((seq, KV_WIDTH), lambda b, c: (b, col))
    cmp_ = lambda which: pl.BlockSpec((1, nb, KV_WIDTH), lambda b, c: (which, b, 0))
    ch = min(SEL_CHUNK, seq)
    wk = min(WINDOW + LANES, seq)
    assert seq % ch == 0 and wk % LANES == 0 and nb % 2 == 0 and nb <= LANES
    return pl.pallas_call(
        functools.partial(_nsa_prompt_kernel, nb=nb, ch=ch, wk=wk),
        grid=(n_seq, nb),
        in_specs=[chunk(NSA_WIDTH), chunk(LANES), cmp_(0), cmp_(1), kv(2), kv(3), kv(4), kv(5),
                  pl.BlockSpec(expand.shape, lambda b, c: (0, 0))],
        out_specs=chunk(NSA_WIDTH),
        out_shape=jax.ShapeDtypeStruct((n_seq * seq, NSA_WIDTH), BF16),
        scratch_shapes=[pltpu.VMEM((seq // ch, NSA_KV * BLK, ch), F32),
                        pltpu.VMEM((seq // ch, rows, ch), F32),
                        pltpu.VMEM((rows, LANES), F32),
                        pltpu.VMEM((rows, 2 * KV_WIDTH), F32)],
        compiler_params=_params("parallel", "arbitrary"),
        name="nsa_prompt",
    )(q, gates, cmp_kv, cmp_kv, kvb, kvb, kvb, kvb, expand)


def _nsa_sample_kernel(q_ref, gate_ref, kc_ref, vc_ref, ks_ref, vs_ref, nks_ref, nvs_ref,
                       wb_ref, nkw_ref, nvw_ref, e_ref, o_ref, bias_ref, m_ref, l_ref, acc_ref,
                       *, nq, past, nbp, tk):
    rows = NSA_HEADS * nq
    q2 = _stack_queries(q_ref[0], nq)
    q_pos = past + lax.broadcasted_iota(jnp.int32, (rows, 1), 0) % nq
    cur = past // BLK

    n_ids = lax.broadcasted_iota(jnp.int32, (rows, nbp), 1)
    s_c = _dot_nt(q2, kc_ref[0].astype(BF16))
    p_c = _masked_softmax(s_c, (n_ids + 1) * BLK - 1 <= q_pos)
    o_c = _dot(p_c.astype(BF16), vc_ref[0].astype(BF16))

    n_sel = lax.broadcasted_iota(jnp.int32, (nq, nbp), 1)
    sels = []
    for g in range(NSA_KV):
        imp = p_c[g * NSA_REP * nq:(g * NSA_REP + 1) * nq]
        for r in range(1, NSA_REP):
            imp = imp + p_c[(g * NSA_REP + r) * nq:(g * NSA_REP + r + 1) * nq]
        sel = _select_blocks(imp, n_sel, cur, cur + 1)
        sels.extend([sel] * NSA_REP)
    keys = _dot(jnp.concatenate(sels, axis=0).astype(BF16), e_ref[...])
    bias = (keys - 1.0) * (-NEG_INF)
    for t in range(past // tk):
        bias_ref[t] = bias[:, t * tk:(t + 1) * tk]

    _online_reset(m_ref, l_ref, acc_ref)

    def body(t, _):
        k0 = pl.multiple_of(t * tk, tk)
        s = _dot_nt(q2, ks_ref[pl.ds(k0, tk), :].astype(BF16)) + bias_ref[t]
        _online_step(s, vs_ref[pl.ds(k0, tk), :].astype(BF16), m_ref, l_ref, acc_ref)
        return 0

    lax.fori_loop(0, past // tk, body, 0)
    new_pos = past + lax.broadcasted_iota(jnp.int32, (rows, LANES), 1)
    s = _dot_nt(q2, nks_ref[0].astype(BF16)) + bias[:, past:past + LANES]
    s = jnp.where(new_pos <= q_pos, s, NEG_INF)
    _online_step(s, nvs_ref[0].astype(BF16), m_ref, l_ref, acc_ref)
    o_s = _online_result(l_ref, acc_ref)

    _online_reset(m_ref, l_ref, acc_ref)
    wlen = wb_ref.shape[1]
    w_pos = past - wlen + lax.broadcasted_iota(jnp.int32, (rows, wlen), 1)
    d = q_pos - w_pos
    s = _dot_nt(q2, wb_ref[0, :, 0:KV_WIDTH].astype(BF16))
    s = jnp.where((d >= 0) & (d < WINDOW) & (w_pos >= 0), s, NEG_INF)
    _online_step(s, wb_ref[0, :, KV_WIDTH:2 * KV_WIDTH].astype(BF16), m_ref, l_ref, acc_ref)
    d = q_pos - new_pos
    s = _dot_nt(q2, nkw_ref[0].astype(BF16))
    s = jnp.where((d >= 0) & (d < WINDOW), s, NEG_INF)
    _online_step(s, nvw_ref[0].astype(BF16), m_ref, l_ref, acc_ref)
    o_w = _online_result(l_ref, acc_ref)

    o_ref[0] = _combine_heads(gate_ref[0], o_c, o_s, o_w, nq).astype(o_ref.dtype)


def _nsa_sample(q3, gates3, kc, vc, past2d, new_rows, win_buf, expand, *, past, tk):
    n_seq, nq, _ = q3.shape
    nbp = kc.shape[1]
    rows = NSA_HEADS * nq
    wlen = win_buf.shape[1]
    per_seq = lambda a: pl.BlockSpec((1,) + a.shape[1:], lambda b: (b,) + (0,) * (a.ndim - 1))
    new = lambda col: pl.BlockSpec((1, LANES, KV_WIDTH), lambda b: (b, 0, col))
    return pl.pallas_call(
        functools.partial(_nsa_sample_kernel, nq=nq, past=past, nbp=nbp, tk=tk),
        grid=(n_seq,),
        in_specs=[per_seq(q3), per_seq(gates3), per_seq(kc), per_seq(vc),
                  pl.BlockSpec((past, KV_WIDTH), lambda b: (b, 2)),
                  pl.BlockSpec((past, KV_WIDTH), lambda b: (b, 3)),
                  new(2), new(3), per_seq(win_buf), new(4), new(5),
                  pl.BlockSpec(expand.shape, lambda b: (0, 0))],
        out_specs=pl.BlockSpec((1, nq, NSA_WIDTH), lambda b: (b, 0, 0)),
        out_shape=jax.ShapeDtypeStruct((n_seq, nq, NSA_WIDTH), F32),
        scratch_shapes=[pltpu.VMEM((past // tk, rows, tk), F32),
                        pltpu.VMEM((rows, LANES), F32), pltpu.VMEM((rows, LANES), F32),
                        pltpu.VMEM((rows, KV_WIDTH), F32)],
        compiler_params=_params("parallel"),
        name="nsa_sample",
    )(q3, gates3, kc, vc, past2d, past2d, new_rows, new_rows, win_buf, new_rows, new_rows, expand)


def _expand_matrix(nb, n_keys):
    return (jnp.arange(n_keys, dtype=jnp.int32)[None, :] // BLK
            == jnp.arange(nb, dtype=jnp.int32)[:, None]).astype(BF16)


def _merge_kernel(x_ref, ssm_ref, nsa_ref, gpre_ref, wm_ref, wbs_ref, wbn_ref, wo_ref, gpost_ref,
                  gx_ref, wxq_ref, x1_ref, qx_ref):
    x = x_ref[...]
    a = _rms(x, gpre_ref[...]).astype(BF16)
    g_ssm = _sigmoid(_dot(a, wm_ref[:, 0:D_MODEL]))
    g_nsa = _sigmoid(_dot(a, wm_ref[:, D_MODEL:2 * D_MODEL]))
    merged = (g_ssm * _dot(ssm_ref[...].astype(BF16), wbs_ref[...])
              + g_nsa * _dot(nsa_ref[...], wbn_ref[...]))
    x1 = x + _rms(_dot(merged.astype(BF16), wo_ref[...]), gpost_ref[...])
    x1_ref[...] = x1
    c = _rms(x1, gx_ref[...]).astype(BF16)
    qx_ref[...] = (_dot(c, wxq_ref[...]) * (X_HEAD_DIM ** -0.5)).astype(BF16)


def _merge(x2d, ssm_y, nsa_o, weights, *, tm, ssm_time_major, n_tab_blocks):
    n = x2d.shape[0]
    steps = n // tm
    row = lambda w_: pl.BlockSpec((tm, w_), lambda i: (i, 0))
    full = lambda a: pl.BlockSpec(a.shape, lambda i: (0,) * a.ndim)
    if ssm_time_major:
        ssm_spec = pl.BlockSpec((tm, SSM_WIDTH), lambda i: (i % n_tab_blocks, i // n_tab_blocks))
    else:
        ssm_spec = row(SSM_WIDTH)
    return pl.pallas_call(
        _merge_kernel,
        grid=(steps,),
        in_specs=[row(D_MODEL), ssm_spec, row(NSA_WIDTH)] + [full(w) for w in weights],
        out_specs=[row(D_MODEL), row(X_WIDTH)],
        out_shape=[jax.ShapeDtypeStruct((n, D_MODEL), F32), jax.ShapeDtypeStruct((n, X_WIDTH), BF16)],
        compiler_params=_params("parallel"),
        name="merge",
    )(x2d, ssm_y, nsa_o, *weights)


def _xattn_kernel(q_ref, k_ref, v_ref, o_ref):
    q = q_ref[0].astype(BF16)
    outs = []
    for h in range(X_HEADS):
        cols = slice(h * X_HEAD_DIM, (h + 1) * X_HEAD_DIM)
        s = _dot_nt(q[:, cols], k_ref[0, :, cols].astype(BF16))
        m = jnp.max(s, axis=-1, keepdims=True)
        p = jnp.exp(s - m)
        p = p / jnp.sum(p, axis=-1, keepdims=True)
        outs.append(_dot(p.astype(BF16), v_ref[0, :, cols].astype(BF16)))
    o_ref[0] = jnp.concatenate(outs, axis=1).astype(o_ref.dtype)


def _xattn(q3, mem_kv3, *, tq):
    n_seq, t, _ = q3.shape
    m_len = mem_kv3.shape[1]
    return pl.pallas_call(
        _xattn_kernel,
        grid=(n_seq, t // tq),
        in_specs=[pl.BlockSpec((1, tq, X_WIDTH), lambda b, i: (b, i, 0)),
                  pl.BlockSpec((1, m_len, X_WIDTH), lambda b, i: (b, 0, 0)),
                  pl.BlockSpec((1, m_len, X_WIDTH), lambda b, i: (b, 0, 1))],
        out_specs=pl.BlockSpec((1, tq, X_WIDTH), lambda b, i: (b, i, 0)),
        out_shape=jax.ShapeDtypeStruct((n_seq, t, X_WIDTH), q3.dtype),
        compiler_params=_params("parallel", "parallel"),
        name="xattn",
    )(q3, mem_kv3, mem_kv3)


def _mlp_kernel(x1_ref, o_ref, wxo_ref, gxp_ref, gm_ref, wup_ref, wdn_ref, gmp_ref, y_ref):
    x2 = x1_ref[...] + _rms(_dot(o_ref[...], wxo_ref[...]), gxp_ref[...])
    m = _rms(x2, gm_ref[...]).astype(BF16)
    hid = jnp.maximum(_dot(m, wup_ref[...]), 0.0)
    hid = (hid * hid).astype(BF16)
    y_ref[...] = x2 + _rms(_dot(hid, wdn_ref[...]), gmp_ref[...])


def _mlp(x1, o, weights, *, tm):
    n = x1.shape[0]
    row = lambda w_: pl.BlockSpec((tm, w_), lambda i: (i, 0))
    full = lambda a: pl.BlockSpec(a.shape, lambda i: (0,) * a.ndim)
    return pl.pallas_call(
        _mlp_kernel,
        grid=(n // tm,),
        in_specs=[row(D_MODEL), row(X_WIDTH)] + [full(w) for w in weights],
        out_specs=row(D_MODEL),
        out_shape=jax.ShapeDtypeStruct((n, D_MODEL), F32),
        compiler_params=_params("parallel"),
        name="mlp",
    )(x1, o, *weights)


def _memkv_kernel(m_ref, g_ref, w_ref, o_ref):
    o_ref[...] = _dot(_rms(m_ref[...], g_ref[...]).astype(BF16), w_ref[...])


def _memkv(mem2d, g, w, *, tm):
    n = mem2d.shape[0]
    return pl.pallas_call(
        _memkv_kernel,
        grid=(n // tm,),
        in_specs=[pl.BlockSpec((tm, D_MODEL), lambda i: (i, 0)),
                  pl.BlockSpec(g.shape, lambda i: (0, 0)), pl.BlockSpec(w.shape, lambda i: (0, 0))],
        out_specs=pl.BlockSpec((tm, 2 * X_WIDTH), lambda i: (i, 0)),
        out_shape=jax.ShapeDtypeStruct((n, 2 * X_WIDTH), F32),
        compiler_params=_params("parallel"),
        name="memkv",
    )(mem2d, g, w)


def _row(v):
    return v.astype(F32).reshape(1, -1)


def kernel(x_prompt, x_sample, cache_nsa_kv, cache_win_kv, state_ssm, cache_mem_kv, page_table, mem_prompt, g_mix_pre, w_in, ssm_lam_re, ssm_lam_im, ssm_log_dt, ssm_b_re, ssm_b_im, ssm_c_re, ssm_c_im, ssm_d, w_glu, b_glu, cmp_pe_k, w_cmpk1, w_cmpk2, cmp_pe_v, w_cmpv1, w_cmpv2, w_br_ssm, w_br_nsa, w_out, g_mix_post, g_x_pre, g_mem, w_xq, w_xk, w_xv, w_xo, g_x_post, g_mlp_pre, w_up, w_down, g_mlp_post):
    depth = w_in.shape[0]
    n_seq_p, seq, _ = x_prompt.shape
    n_seq_s, nq, _ = x_sample.shape
    past = page_table.shape[1] * PAGE_SIZE
    assert depth == 1 and seq % BLK == 0 and nq <= SUBLANES and past % BLK == 0
    assert n_seq_p == SUBLANES and n_seq_s % SUBLANES == 0

    y_p = x_prompt.reshape(n_seq_p * seq, D_MODEL)
    y_s = x_sample.reshape(n_seq_s * nq, D_MODEL)
    l = 0

    w_proj = w_in[l, :, :N_PROJ].astype(BF16)
    w_gate = jnp.pad(w_in[l, :, N_PROJ:N_PROJ + N_GATE], ((0, 0), (0, LANES - N_GATE))).astype(BF16)
    w_merge = w_in[l, :, N_PROJ + N_GATE:].astype(BF16)
    lam_l, bm, cm = _ssm_params(ssm_lam_re[l], ssm_lam_im[l], ssm_log_dt[l], ssm_b_re[l], ssm_b_im[l],
                                ssm_c_re[l], ssm_c_im[l])
    ssm_w = (lam_l, bm, cm, _row(ssm_d[l]), w_glu[l].astype(BF16), _row(b_glu[l]))
    cmp_w = _compress_params(cmp_pe_k[l], w_cmpk1[l], w_cmpk2[l], cmp_pe_v[l], w_cmpv1[l], w_cmpv2[l])
    merge_w = (_row(g_mix_pre[l]), w_merge, w_br_ssm[l].astype(BF16), w_br_nsa[l].astype(BF16),
               w_out[l].astype(BF16), _row(g_mix_post[l]), _row(g_x_pre[l]), w_xq[l].astype(BF16))
    mlp_w = (w_xo[l].astype(BF16), _row(g_x_post[l]), _row(g_mlp_pre[l]), w_up[l].astype(BF16),
             w_down[l].astype(BF16), _row(g_mlp_post[l]))
    w_mem = jnp.concatenate([w_xk[l], w_xv[l]], axis=1).astype(BF16)

    tm_p = 512 if seq % 512 == 0 else seq
    nt_p = seq // tm_p
    tabs_p = _rope_tables(jnp.arange(seq, dtype=jnp.int32))
    u_p, q_p, kv_p, win_p, kvb_p, gate_p = _proj(y_p, _row(g_mix_pre[l]), w_proj, w_gate, tabs_p,
                                                 tm=tm_p, n_tab_blocks=nt_p, u_time_major=True)
    u_p = u_p.reshape(seq * n_seq_p, SSM_WIDTH)
    h0_p = jnp.zeros((1, SUBLANES, 2 * N_STATE), F32)
    tc_p = 64 if seq % 64 == 0 else seq
    ssm_p, hl_p = _ssm(u_p, h0_p, *ssm_w, n_groups=1, n_time=seq, tc=tc_p)
    ssm_p = ssm_p.reshape(seq, n_seq_p * SSM_WIDTH)

    nb_p = seq // BLK
    n_blocks_p = n_seq_p * nb_p
    cmp_p = _compress(kv_p, *cmp_w, n_blocks=n_blocks_p, tm=min(n_blocks_p, 256))
    nsa_p = _nsa_prompt(q_p, gate_p, cmp_p, kvb_p, _expand_matrix(LANES, seq), n_seq=n_seq_p, seq=seq)

    mem_kv_p = _memkv(mem_prompt.reshape(-1, D_MODEL), _row(g_mem[l]), w_mem, tm=256)
    m_len = mem_prompt.shape[1]
    mem_kv_p3 = mem_kv_p.reshape(n_seq_p, m_len, 2 * X_WIDTH)

    x1_p, qx_p = _merge(y_p, ssm_p, nsa_p, merge_w, tm=tm_p, ssm_time_major=True, n_tab_blocks=nt_p)
    o_p = _xattn(qx_p.reshape(n_seq_p, seq, X_WIDTH), mem_kv_p3, tq=tm_p)
    y_p = _mlp(x1_p, o_p.reshape(-1, X_WIDTH), mlp_w, tm=256)

    n_s = n_seq_s * nq
    pos_s = past + jnp.arange(nq, dtype=jnp.int32)
    tabs_s = tuple(jnp.tile(t, (n_seq_s, 1)) for t in _rope_tables(pos_s))
    u_s, q_s, kv_s, win_s, kvb_s, gate_s = _proj(y_s, _row(g_mix_pre[l]), w_proj, w_gate, tabs_s,
                                                 tm=n_s, n_tab_blocks=1, u_time_major=False)
    n_grp = n_seq_s // SUBLANES
    to_tb = lambda a: a.reshape(n_grp, SUBLANES, nq, SSM_WIDTH).transpose(0, 2, 1, 3).reshape(n_s, SSM_WIDTH)
    st = state_ssm[l].astype(F32).reshape(n_seq_s, N_STATE, 2)
    h0_s = _state_lanes(st[..., 0], st[..., 1]).reshape(n_grp, SUBLANES, 2 * N_STATE)
    ssm_s, hl_s = _ssm(to_tb(u_s), h0_s, *ssm_w, n_groups=n_grp, n_time=nq, tc=nq)
    ssm_s = ssm_s.reshape(n_grp, nq, SUBLANES, SSM_WIDTH).transpose(0, 2, 1, 3).reshape(n_s, SSM_WIDTH)

    n_pages = page_table.shape[1]
    past2d = cache_nsa_kv[l][page_table].reshape(n_seq_s * past, 4 * KV_WIDTH)
    nb_past = past // BLK
    cmp_past = _compress(past2d, *cmp_w, n_blocks=n_seq_s * nb_past, tm=min(n_seq_s * nb_past, 256))
    new_rows = jnp.pad(jnp.concatenate([kv_s, win_s], axis=1).reshape(n_seq_s, nq, 6 * KV_WIDTH),
                       ((0, 0), (0, LANES - nq), (0, 0)))
    cmp_new = _compress(new_rows[:, :BLK].reshape(n_seq_s * BLK, 6 * KV_WIDTH), *cmp_w,
                        n_blocks=n_seq_s, tm=n_seq_s)
    nbp = -(-(nb_past + 1) // LANES) * LANES
    cmp_s = jnp.concatenate([cmp_past.reshape(2, n_seq_s, nb_past, KV_WIDTH), cmp_new[:, :, None, :],
                             jnp.zeros((2, n_seq_s, nbp - nb_past - 1, KV_WIDTH), F32)], axis=2)
    win_buf = cache_win_kv[l].reshape(n_seq_s, -1, 2 * KV_WIDTH)
    nsa_s = _nsa_sample(q_s.astype(F32).reshape(n_seq_s, nq, NSA_WIDTH), gate_s.reshape(n_seq_s, nq, LANES),
                        cmp_s[0], cmp_s[1], past2d, new_rows, win_buf, _expand_matrix(nbp, past + LANES),
                        past=past, tk=min(past, 1024))

    x1_s, qx_s = _merge(y_s, ssm_s, nsa_s.reshape(n_s, NSA_WIDTH).astype(BF16), merge_w, tm=n_s,
                        ssm_time_major=False, n_tab_blocks=1)
    mem_kv_s3 = cache_mem_kv[l].reshape(n_seq_s, m_len, 2 * X_WIDTH)
    o_s = _xattn(qx_s.astype(F32).reshape(n_seq_s, nq, X_WIDTH), mem_kv_s3, tq=nq)
    y_s = _mlp(x1_s, o_s.reshape(-1, X_WIDTH).astype(BF16), mlp_w, tm=n_s)

    def ssm_state(hl, n_seq):
        re, im = _state_unlanes(hl.reshape(n_seq, 2 * N_STATE))
        return jnp.stack([re, im], axis=-1).reshape(1, n_seq, SSM_GROUPS, SSM_STATE, 2)

    w_keep = min(WINDOW, seq)
    win_prompt = win_p.reshape(n_seq_p, seq, 2, NSA_KV, HEAD_DIM)[:, seq - w_keep:]
    win_new = win_s.reshape(n_seq_s, nq, 2, NSA_KV, HEAD_DIM).astype(cache_win_kv.dtype)
    win_sample = jnp.concatenate([cache_win_kv[l], win_new], axis=1)[:, nq:]
    return (y_p.reshape(n_seq_p, seq, D_MODEL),
            y_s.reshape(n_seq_s, nq, D_MODEL),
            kv_p.reshape(1, n_seq_p, seq, 4, NSA_KV, HEAD_DIM),
            kv_s.reshape(1, n_seq_s, nq, 4, NSA_KV, HEAD_DIM),
            win_prompt[None],
            win_sample[None],
            ssm_state(hl_p, n_seq_p),
            ssm_state(hl_s, n_seq_s),
            mem_kv_p.reshape(1, n_seq_p, m_len, 2, X_HEADS, X_HEAD_DIM))
```

```python
import functools
import math

import jax
import jax.numpy as jnp
from jax import lax
from jax.experimental import pallas as pl
from jax.experimental.pallas import tpu as pltpu

F32 = jnp.float32
BF16 = jnp.bfloat16

D_MODEL = 1024
SSM_WIDTH = 512
SSM_GROUP = 16
SSM_GROUPS = 32
SSM_STATE = 64
N_STATE = SSM_GROUPS * SSM_STATE
STATE_CHUNK = 512
N_CHUNKS = N_STATE // STATE_CHUNK
NSA_HEADS = 8
HEAD_DIM = 64
NSA_WIDTH = NSA_HEADS * HEAD_DIM
NSA_KV = 2
NSA_REP = NSA_HEADS // NSA_KV
KV_WIDTH = NSA_KV * HEAD_DIM
BLK = 64
N_SEL = 16
WINDOW = 512
ROT_DIM = 16
ROPE_THETA = 500000.0
PAGE_SIZE = 128
X_HEADS = 4
X_HEAD_DIM = 128
X_WIDTH = X_HEADS * X_HEAD_DIM
D_FF = 4 * D_MODEL
EPS = 1e-6
NEG_INF = -1e30
M_INIT = -1e29
FORCE_SCORE = 1e4
LANES = 128
SUBLANES = 8
VMEM_LIMIT = 56 * 1024 * 1024

N_PROJ = SSM_WIDTH + NSA_WIDTH + 6 * KV_WIDTH
N_GATE = 3 * NSA_HEADS
SEL_CHUNK = 512
PAGE_PITCH = PAGE_SIZE + SUBLANES


def _params(*sem):
    return pltpu.CompilerParams(dimension_semantics=sem, vmem_limit_bytes=VMEM_LIMIT)


def _rms(x, g):
    return x * lax.rsqrt(jnp.mean(x * x, axis=-1, keepdims=True) + EPS) * g


def _gelu(x):
    return 0.5 * x * (1.0 + jnp.tanh(math.sqrt(2.0 / math.pi) * (x + 0.044715 * (x * x * x))))


def _sigmoid(x):
    return 1.0 / (1.0 + jnp.exp(-x))


def _dot(a, b):
    return jnp.dot(a, b, preferred_element_type=F32)


def _dot_nt(a, b):
    return lax.dot_general(a, b, (((1,), (1,)), ((), ())), preferred_element_type=F32)


def _proj_kernel(x_ref, g_ref, w_ref, wg_ref, cos_ref, sp_ref, sm_ref,
                 u_ref, q_ref, kv_ref, win_ref, kvb_ref, gate_ref):
    a = _rms(x_ref[...], g_ref[...]).astype(BF16)
    cos, sp, sm = cos_ref[...], sp_ref[...], sm_ref[...]

    def rope(blk):
        return blk * cos + pltpu.roll(blk, 8, 1) * sp + pltpu.roll(blk, LANES - 8, 1) * sm

    u_ref[...] = _dot(a, w_ref[:, 0:SSM_WIDTH])
    for j in range(NSA_WIDTH // LANES):
        c0 = SSM_WIDTH + j * LANES
        q_ref[:, j * LANES:(j + 1) * LANES] = rope(_dot(a, w_ref[:, c0:c0 + LANES])).astype(BF16)
    for j in range(6):
        c0 = SSM_WIDTH + NSA_WIDTH + j * LANES
        blk = _dot(a, w_ref[:, c0:c0 + LANES])
        if j % 2 == 0:
            blk = rope(blk)
        if j < 4:
            kv_ref[:, j * LANES:(j + 1) * LANES] = blk
        else:
            win_ref[:, (j - 4) * LANES:(j - 3) * LANES] = blk
        kvb_ref[:, j * LANES:(j + 1) * LANES] = blk.astype(BF16)
    gate_ref[...] = _sigmoid(_dot(a, wg_ref[...]))


def _proj(x2d, g, w, wg, tabs, *, tm, n_tab_blocks, u_time_major):
    n = x2d.shape[0]
    steps = n // tm
    if u_time_major:
        n_b = steps // n_tab_blocks
        u_shape = jax.ShapeDtypeStruct((n_tab_blocks * tm, n_b * SSM_WIDTH), F32)
        u_spec = pl.BlockSpec((tm, SSM_WIDTH), lambda i: (i % n_tab_blocks, i // n_tab_blocks))
    else:
        u_shape = jax.ShapeDtypeStruct((n, SSM_WIDTH), F32)
        u_spec = pl.BlockSpec((tm, SSM_WIDTH), lambda i: (i, 0))
    row = lambda w_: pl.BlockSpec((tm, w_), lambda i: (i, 0))
    full = lambda a: pl.BlockSpec(a.shape, lambda i: (0,) * a.ndim)
    tab = pl.BlockSpec((tm, LANES), lambda i: (i % n_tab_blocks, 0))
    return pl.pallas_call(
        _proj_kernel,
        grid=(steps,),
        in_specs=[row(D_MODEL), full(g), full(w), full(wg), tab, tab, tab],
        out_specs=[u_spec, row(NSA_WIDTH), row(4 * KV_WIDTH), row(2 * KV_WIDTH), row(6 * KV_WIDTH), row(LANES)],
        out_shape=[u_shape,
                   jax.ShapeDtypeStruct((n, NSA_WIDTH), BF16),
                   jax.ShapeDtypeStruct((n, 4 * KV_WIDTH), F32),
                   jax.ShapeDtypeStruct((n, 2 * KV_WIDTH), F32),
                   jax.ShapeDtypeStruct((n, 6 * KV_WIDTH), BF16),
                   jax.ShapeDtypeStruct((n, LANES), F32)],
        compiler_params=_params("parallel"),
        name="proj",
    )(x2d, g, w, wg, *tabs)


def _rope_tables(pos):
    half = ROT_DIM // 2
    freqs = ROPE_THETA ** (-jnp.arange(half, dtype=F32) / half)
    ang = pos.astype(F32)[:, None] * freqs[None, :]
    cos, sin = jnp.cos(ang), jnp.sin(ang)
    r = pos.shape[0]
    z8 = jnp.zeros((r, half), F32)
    rest0 = jnp.zeros((r, HEAD_DIM - ROT_DIM), F32)
    rest1 = jnp.ones((r, HEAD_DIM - ROT_DIM), F32)
    c64 = jnp.concatenate([cos, cos, rest1], axis=1)
    sp64 = jnp.concatenate([z8, sin, rest0], axis=1)
    sm64 = jnp.concatenate([-sin, z8, rest0], axis=1)
    return tuple(jnp.tile(t, (1, LANES // HEAD_DIM)) for t in (c64, sp64, sm64))


def _ssm_kernel(u_ref, h0_ref, lam_ref, bm_ref, cm_ref, d_ref, wglu_ref, bglu_ref,
                y_ref, hlast_ref, hs_ref, hstate_ref, *, tc):
    j = pl.program_id(1)

    @pl.when(j == 0)
    def _():
        hstate_ref[...] = h0_ref[0]

    u = u_ref[...]
    ub = u.astype(BF16)
    half_in = SSM_WIDTH // 2
    half_st = N_STATE
    for h in range(2):
        hs_ref[:, h * half_st:(h + 1) * half_st] = _dot(ub[:, h * half_in:(h + 1) * half_in], bm_ref[h])

    for c in range(N_CHUNKS):
        re0 = c * 2 * STATE_CHUNK
        im0 = re0 + STATE_CHUNK
        lr = jnp.broadcast_to(lam_ref[0:1, re0:re0 + STATE_CHUNK], (SUBLANES, STATE_CHUNK))
        li = jnp.broadcast_to(lam_ref[0:1, im0:im0 + STATE_CHUNK], (SUBLANES, STATE_CHUNK))

        def step(t, carry, re0=re0, im0=im0, lr=lr, li=li):
            hr, hi = carry
            r0 = pl.multiple_of(t * SUBLANES, SUBLANES)
            nr = lr * hr - li * hi + hs_ref[pl.ds(r0, SUBLANES), re0:re0 + STATE_CHUNK]
            ni = lr * hi + li * hr + hs_ref[pl.ds(r0, SUBLANES), im0:im0 + STATE_CHUNK]
            hs_ref[pl.ds(r0, SUBLANES), re0:re0 + STATE_CHUNK] = nr
            hs_ref[pl.ds(r0, SUBLANES), im0:im0 + STATE_CHUNK] = ni
            return nr, ni

        hr, hi = lax.fori_loop(0, tc, step,
                               (hstate_ref[:, re0:re0 + STATE_CHUNK], hstate_ref[:, im0:im0 + STATE_CHUNK]),
                               unroll=min(tc, 4))
        hstate_ref[:, re0:re0 + STATE_CHUNK] = hr
        hstate_ref[:, im0:im0 + STATE_CHUNK] = hi

    ys = [_dot(hs_ref[:, h * half_st:(h + 1) * half_st].astype(BF16), cm_ref[h]) for h in range(2)]
    y = jnp.concatenate(ys, axis=1) + d_ref[...] * u
    y = _gelu(y)
    z = _dot(y.astype(BF16), wglu_ref[...]) + bglu_ref[...]
    y_ref[...] = y * _sigmoid(z)

    @pl.when(j == pl.num_programs(1) - 1)
    def _():
        hlast_ref[0] = hstate_ref[...]


def _ssm(u_tb, h0, lam, bm, cm, d, wglu, bglu, *, n_groups, n_time, tc):
    rows = tc * SUBLANES
    nt = n_time // tc
    full = lambda a: pl.BlockSpec(a.shape, lambda g, j: (0,) * a.ndim)
    st = pl.BlockSpec((1, SUBLANES, 2 * N_STATE), lambda g, j: (g, 0, 0))
    return pl.pallas_call(
        functools.partial(_ssm_kernel, tc=tc),
        grid=(n_groups, nt),
        in_specs=[pl.BlockSpec((rows, SSM_WIDTH), lambda g, j: (g * nt + j, 0)), st,
                  full(lam), full(bm), full(cm), full(d), full(wglu), full(bglu)],
        out_specs=[pl.BlockSpec((rows, SSM_WIDTH), lambda g, j: (g * nt + j, 0)), st],
        out_shape=[jax.ShapeDtypeStruct((n_groups * n_time * SUBLANES, SSM_WIDTH), F32),
                   jax.ShapeDtypeStruct((n_groups, SUBLANES, 2 * N_STATE), F32)],
        scratch_shapes=[pltpu.VMEM((rows, 2 * N_STATE), F32), pltpu.VMEM((SUBLANES, 2 * N_STATE), F32)],
        compiler_params=_params("parallel", "arbitrary"),
        name="ssm",
    )(u_tb, h0, lam, bm, cm, d, wglu, bglu)


def _state_lanes(re, im):
    lead = re.shape[:-1]
    r = re.reshape(lead + (N_CHUNKS, 1, STATE_CHUNK))
    i = im.reshape(lead + (N_CHUNKS, 1, STATE_CHUNK))
    return jnp.concatenate([r, i], axis=-2).reshape(lead + (2 * N_STATE,))


def _state_unlanes(x):
    lead = x.shape[:-1]
    y = x.reshape(lead + (N_CHUNKS, 2, STATE_CHUNK))
    return y[..., 0, :].reshape(lead + (N_STATE,)), y[..., 1, :].reshape(lead + (N_STATE,))


def _ssm_params(lam_re, lam_im, log_dt, b_re, b_im, c_re, c_im):
    lam = lax.complex(lam_re.astype(F32), lam_im.astype(F32))
    dt = jnp.exp(log_dt.astype(F32))[:, None]
    lam_bar = jnp.exp(lam * dt)
    b = lax.complex(b_re.astype(F32), b_im.astype(F32))
    b_bar = ((lam_bar - 1.0) / lam)[..., None] * b
    eye = jnp.eye(SSM_GROUPS, dtype=F32)

    def in_blockdiag(x):
        return jnp.einsum('gpc,gh->gchp', x, eye).reshape(SSM_WIDTH, N_STATE)

    def out_blockdiag(x):
        return jnp.einsum('gcp,gh->gphc', x, eye).reshape(N_STATE, SSM_WIDTH)

    b_full = _state_lanes(in_blockdiag(jnp.real(b_bar)), in_blockdiag(jnp.imag(b_bar)))
    c_full = _state_lanes(out_blockdiag(c_re.astype(F32)).T, -out_blockdiag(c_im.astype(F32)).T).T
    hw, hs = SSM_WIDTH // 2, N_STATE
    bm = jnp.stack([b_full[h * hw:(h + 1) * hw, h * hs:(h + 1) * hs] for h in range(2)]).astype(BF16)
    cm = jnp.stack([c_full[h * hs:(h + 1) * hs, h * hw:(h + 1) * hw] for h in range(2)]).astype(BF16)
    lam_l = _state_lanes(jnp.real(lam_bar).reshape(1, N_STATE), jnp.imag(lam_bar).reshape(1, N_STATE))
    return lam_l, bm, cm


def _compress_kernel(x_ref, pe_ref, w1_ref, w2_ref, o_ref, *, tm):
    def body(sp, acc):
        s0 = 2 * sp
        xa = x_ref[pl.ds(s0, tm, stride=BLK), :] + pe_ref[0, pl.ds(s0, 1), :]
        xb = x_ref[pl.ds(s0 + 1, tm, stride=BLK), :] + pe_ref[0, pl.ds(s0 + 1, 1), :]
        lhs = jnp.concatenate([xa, xb], axis=1).astype(BF16)
        return acc + _dot(lhs, w1_ref[0, sp])

    acc = lax.fori_loop(0, BLK // 2, body, jnp.zeros((tm, KV_WIDTH), F32))
    o_ref[0] = _dot(_gelu(acc).astype(BF16), w2_ref[0])


def _compress(x2d, pe, w1, w2, *, n_blocks, tm):
    return pl.pallas_call(
        functools.partial(_compress_kernel, tm=tm),
        grid=(2, n_blocks // tm),
        in_specs=[pl.BlockSpec((tm * BLK, KV_WIDTH), lambda c, i: (i, c)),
                  pl.BlockSpec((1, BLK, KV_WIDTH), lambda c, i: (c, 0, 0)),
                  pl.BlockSpec((1, BLK // 2, 2 * KV_WIDTH, KV_WIDTH), lambda c, i: (c, 0, 0, 0)),
                  pl.BlockSpec((1, KV_WIDTH, KV_WIDTH), lambda c, i: (c, 0, 0))],
        out_specs=pl.BlockSpec((1, tm, KV_WIDTH), lambda c, i: (c, i, 0)),
        out_shape=jax.ShapeDtypeStruct((2, n_blocks, KV_WIDTH), F32),
        compiler_params=_params("parallel", "parallel"),
        name="compress",
    )(x2d, pe, w1, w2)


def _compress_params(pe_k, w1_k, w2_k, pe_v, w1_v, w2_v):
    def bd(w):
        z = jnp.zeros_like(w)
        return jnp.concatenate([jnp.concatenate([w, z], axis=-1), jnp.concatenate([z, w], axis=-1)], axis=-2)

    def one(pe, w1, w2):
        w1s = bd(w1.astype(F32).reshape(BLK, HEAD_DIM, HEAD_DIM))
        return (jnp.tile(pe.astype(F32), (1, NSA_KV)),
                w1s.reshape(BLK // 2, 2 * KV_WIDTH, KV_WIDTH).astype(BF16),
                bd(w2.astype(F32)).astype(BF16))

    k, v = one(pe_k, w1_k, w2_k), one(pe_v, w1_v, w2_v)
    return tuple(jnp.stack([a, b]) for a, b in zip(k, v))


def _compress_paged_kernel(pt_ref, cache_ref, pe_ref, w1_ref, w2_ref, o_ref, buf_ref, sem, *, m):
    c = pl.program_id(0)
    i = pl.program_id(1)

    def page_copy(j):
        row0 = pl.multiple_of(j * PAGE_PITCH, SUBLANES)
        return pltpu.make_async_copy(cache_ref.at[pt_ref[i * m + j], c],
                                     buf_ref.at[pl.ds(row0, PAGE_SIZE), :], sem)

    def start(j, _):
        page_copy(j).start()
        return 0

    def wait(j, _):
        page_copy(j).wait()
        return 0

    lax.fori_loop(0, m, start, 0)
    lax.fori_loop(0, m, wait, 0)

    for kv in range(NSA_KV):
        def body(dp, acc, kv=kv):
            d0 = 2 * dp
            r0 = kv * HEAD_DIM + d0
            xa = buf_ref[pl.ds(r0, m, stride=PAGE_PITCH), :] + pe_ref[0, pl.ds(d0, 1), :]
            xb = buf_ref[pl.ds(r0 + 1, m, stride=PAGE_PITCH), :] + pe_ref[0, pl.ds(d0 + 1, 1), :]
            lhs = jnp.concatenate([xa, xb], axis=1).astype(BF16)
            return acc + _dot(lhs, w1_ref[0, dp])

        acc = lax.fori_loop(0, HEAD_DIM // 2, body, jnp.zeros((m, PAGE_SIZE), F32))
        o_ref[0, kv] = _dot(_gelu(acc).astype(BF16), w2_ref[0])


def _compress_paged(pt_flat, cache_t, pe_t, w1_t, w2, *, m):
    n = pt_flat.shape[0]
    grid_spec = pltpu.PrefetchScalarGridSpec(
        num_scalar_prefetch=1,
        grid=(2, n // m),
        in_specs=[pl.BlockSpec(memory_space=pl.ANY),
                  pl.BlockSpec((1, HEAD_DIM, PAGE_SIZE), lambda c, i, pt: (c, 0, 0)),
                  pl.BlockSpec((1, HEAD_DIM // 2, 2 * PAGE_SIZE, PAGE_SIZE), lambda c, i, pt: (c, 0, 0, 0)),
                  pl.BlockSpec((1, PAGE_SIZE, PAGE_SIZE), lambda c, i, pt: (c, 0, 0))],
        out_specs=pl.BlockSpec((1, NSA_KV, m, PAGE_SIZE), lambda c, i, pt: (c, 0, i, 0)),
        scratch_shapes=[pltpu.VMEM((m * PAGE_PITCH, PAGE_SIZE), F32), pltpu.SemaphoreType.DMA(())])
    return pl.pallas_call(
        functools.partial(_compress_paged_kernel, m=m),
        grid_spec=grid_spec,
        out_shape=jax.ShapeDtypeStruct((2, NSA_KV, n, PAGE_SIZE), F32),
        compiler_params=_params("arbitrary", "arbitrary"),
        name="compress_paged",
    )(pt_flat, cache_t, pe_t, w1_t, w2)


def _compress_paged_params(pe_k, w1_k, pe_v, w1_v):
    def bd(w):
        z = jnp.zeros_like(w)
        return jnp.concatenate([jnp.concatenate([w, z], axis=-1), jnp.concatenate([z, w], axis=-1)], axis=-2)

    def one(pe, w1):
        w1d = bd(w1.astype(F32).reshape(BLK, HEAD_DIM, HEAD_DIM).transpose(1, 0, 2))
        return (jnp.tile(pe.astype(F32).T, (1, PAGE_SIZE // BLK)),
                w1d.reshape(HEAD_DIM // 2, 2 * PAGE_SIZE, PAGE_SIZE).astype(BF16))

    k, v = one(pe_k, w1_k), one(pe_v, w1_v)
    return tuple(jnp.stack([a, b]) for a, b in zip(k, v))


def _stack_queries(q, nq):
    q = q.astype(F32)
    z = jnp.zeros((nq, HEAD_DIM), F32)
    rows = []
    for h in range(NSA_HEADS):
        blk = q[:, h * HEAD_DIM:(h + 1) * HEAD_DIM]
        rows.append(jnp.concatenate([blk, z] if h < NSA_REP else [z, blk], axis=1))
    return (jnp.concatenate(rows, axis=0) * (HEAD_DIM ** -0.5)).astype(BF16)


def _masked_softmax(s, valid):
    s = jnp.where(valid, s, NEG_INF)
    m = jnp.max(s, axis=-1, keepdims=True)
    p = jnp.exp(s - m) * valid.astype(F32)
    return p / jnp.maximum(jnp.sum(p, axis=-1, keepdims=True), 1e-30)


def _select_blocks(imp, n_ids, cur, nb):
    forced = (n_ids == 0) | (n_ids == cur) | (n_ids == cur - 1)
    imp = jnp.where(forced, FORCE_SCORE, imp)
    imp = jnp.where(n_ids <= cur, imp, -FORCE_SCORE)
    rank = jnp.zeros(imp.shape, F32)
    for m in range(nb):
        col = imp[:, m:m + 1]
        beats = (col > imp) | ((col == imp) & (n_ids > m))
        rank = rank + beats.astype(F32)
    return (rank < float(N_SEL)).astype(F32)


def _select_blocks_t(imp_t, cur, nb):
    n_t = lax.broadcasted_iota(jnp.int32, imp_t.shape, 0)
    forced = (n_t == 0) | (n_t == cur) | (n_t == cur - 1)
    imp_t = jnp.where(forced, FORCE_SCORE, imp_t)
    imp_t = jnp.where(n_t <= cur, imp_t, -FORCE_SCORE)
    rank = jnp.zeros(imp_t.shape, F32)
    for m in range(nb):
        row = imp_t[m:m + 1, :]
        beats = (row > imp_t) | ((row == imp_t) & (n_t > m))
        rank = rank + beats.astype(F32)
    return (rank < float(N_SEL)).astype(F32)


def _online_step(s, v, m_ref, l_ref, acc_ref):
    m_old = m_ref[...]
    m_new = jnp.maximum(m_old, jnp.max(s, axis=-1, keepdims=True))
    alpha = jnp.exp(m_old - m_new)
    reps = s.shape[1] // LANES
    p = jnp.exp(s - (m_new if reps == 1 else jnp.concatenate([m_new] * reps, axis=1)))
    l_ref[...] = alpha * l_ref[...] + jnp.sum(p, axis=-1, keepdims=True)
    acc_ref[...] = alpha * acc_ref[...] + _dot(p.astype(BF16), v)
    m_ref[...] = m_new


def _online_reset(m_ref, l_ref, acc_ref):
    m_ref[...] = jnp.full(m_ref.shape, M_INIT, F32)
    l_ref[...] = jnp.zeros(l_ref.shape, F32)
    acc_ref[...] = jnp.zeros(acc_ref.shape, F32)


def _online_result(l_ref, acc_ref):
    return acc_ref[...] / jnp.maximum(l_ref[...], 1e-30)


def _combine_heads(gates, o_c, o_s, o_w, nq):
    outs = []
    for h in range(NSA_HEADS):
        rows = slice(h * nq, (h + 1) * nq)
        o = (gates[:, 3 * h:3 * h + 1] * o_c[rows] + gates[:, 3 * h + 1:3 * h + 2] * o_s[rows]
             + gates[:, 3 * h + 2:3 * h + 3] * o_w[rows])
        g = h // NSA_REP
        outs.append(o[:, g * HEAD_DIM:(g + 1) * HEAD_DIM])
    return jnp.concatenate(outs, axis=1)


def _nsa_prompt_kernel(q_ref, gate_ref, kc_ref, vc_ref, ks_ref, vs_ref, kw_ref, vw_ref, e_ref,
                       o_ref, bias_ref, s_ref, mx_ref, acc_ref, *, nb, ch, wk):
    ci = pl.program_id(1)
    nq = BLK
    rows = NSA_HEADS * nq
    q2 = _stack_queries(q_ref[...], nq)
    q_pos = ci * BLK + lax.broadcasted_iota(jnp.int32, (rows, 1), 0) % nq

    n_ids = lax.broadcasted_iota(jnp.int32, (rows, nb), 1)
    s_c = _dot_nt(q2, kc_ref[0].astype(BF16))
    p_c = _masked_softmax(s_c, (n_ids + 1) * BLK - 1 <= q_pos)
    o_c = _dot(p_c.astype(BF16), vc_ref[0].astype(BF16))

    imps = []
    for g in range(NSA_KV):
        imp = p_c[g * NSA_REP * nq:(g * NSA_REP + 1) * nq]
        for r in range(1, NSA_REP):
            imp = imp + p_c[(g * NSA_REP + r) * nq:(g * NSA_REP + r + 1) * nq]
        imps.append(imp)
    imp2 = jnp.concatenate([jnp.concatenate(imps, axis=0), jnp.zeros((LANES, LANES - nb), F32)], axis=1)
    sel_t = _select_blocks_t(imp2.T[0:nb], ci, nb)
    sel2 = jnp.concatenate([sel_t, jnp.zeros((LANES - nb, LANES), F32)], axis=0).T
    bias = (_dot(sel2.astype(BF16), e_ref[...]) - 1.0) * (-NEG_INF)
    for c in range(nb * BLK // ch):
        bias_ref[c] = bias[:, c * ch:(c + 1) * ch]

    d0 = (ci * BLK + lax.broadcasted_iota(jnp.int32, (rows, LANES), 0) % nq
          - lax.broadcasted_iota(jnp.int32, (rows, LANES), 1))

    def col_blocks(s):
        return [s[:, j * LANES:(j + 1) * LANES] for j in range(s.shape[1] // LANES)]

    def col_max(cols):
        m = cols[0]
        for c_ in cols[1:]:
            m = jnp.maximum(m, c_)
        return m

    def finish(acc):
        return acc[:, 0:KV_WIDTH] / jnp.maximum(acc[:, KV_WIDTH:2 * KV_WIDTH], 1e-30)

    def with_ones(v):
        return jnp.concatenate([v, jnp.ones((v.shape[0], LANES), BF16)], axis=1)

    n_ch = ci // (ch // BLK) + 1
    mx_ref[...] = jnp.full(mx_ref.shape, M_INIT, F32)

    def scores(c, causal):
        k0 = pl.multiple_of(c * ch, ch)
        s = _dot_nt(q2, ks_ref[pl.ds(k0, ch), :])
        s = s + jnp.concatenate([bias_ref[c, g * nq:(g + 1) * nq, :]
                                 for g in range(NSA_KV) for _ in range(NSA_REP)], axis=0)
        cols = col_blocks(s)
        if causal:
            cols = [jnp.where(d0 >= k0 + j * LANES, c_, NEG_INF) for j, c_ in enumerate(cols)]
        mx_ref[...] = jnp.maximum(mx_ref[...], col_max(cols))
        for j, c_ in enumerate(cols):
            s_ref[c, :, j * LANES:(j + 1) * LANES] = c_

    def pass1(c, _):
        scores(c, False)
        return 0

    lax.fori_loop(0, n_ch - 1, pass1, 0)
    scores(n_ch - 1, True)

    m_sel = jnp.broadcast_to(jnp.max(mx_ref[...], axis=-1, keepdims=True), (rows, LANES))
    acc_ref[...] = jnp.zeros(acc_ref.shape, F32)

    def pass2(c, _):
        k0 = pl.multiple_of(c * ch, ch)
        p = jnp.concatenate([jnp.exp(s_ref[c, :, j * LANES:(j + 1) * LANES] - m_sel).astype(BF16)
                             for j in range(ch // LANES)], axis=1)
        acc_ref[...] += _dot(p, with_ones(vs_ref[pl.ds(k0, ch), :]))
        return 0

    lax.fori_loop(0, n_ch, pass2, 0)
    o_s = finish(acc_ref[...])

    first = jnp.maximum(ci - WINDOW // BLK, 0) // (LANES // BLK)
    w0 = pl.multiple_of(first * LANES, LANES)
    cols = col_blocks(_dot_nt(q2, kw_ref[pl.ds(w0, wk), :]))
    last_col = len(cols) - 1
    masked = []
    for j, c_ in enumerate(cols):
        d = d0 - (w0 + j * LANES)
        if j == 0 or j == last_col:
            masked.append(jnp.where((d >= 0) & (d < WINDOW), c_, NEG_INF))
        else:
            masked.append(jnp.where(d >= 0, c_, NEG_INF))
    m_w = jnp.maximum(jnp.max(col_max(masked), axis=-1, keepdims=True), M_INIT)
    m_w = jnp.broadcast_to(m_w, (rows, LANES))
    p = jnp.concatenate([jnp.exp(c_ - m_w).astype(BF16) for c_ in masked], axis=1)
    o_w = finish(_dot(p, with_ones(vw_ref[pl.ds(w0, wk), :])))

    o_ref[...] = _combine_heads(gate_ref[...], o_c, o_s, o_w, nq).astype(o_ref.dtype)


def _nsa_prompt(q, gates, cmp_kv, kvb, expand, *, n_seq, seq):
    nb = seq // BLK
    rows = NSA_HEADS * BLK
    chunk = lambda w_: pl.BlockSpec((BLK, w_), lambda b, c: (b * nb + c, 0))
    kv = lambda col: pl.BlockSpec((seq, KV_WIDTH), lambda b, c: (b, col))
    cmp_ = lambda which: pl.BlockSpec((1, nb, KV_WIDTH), lambda b, c: (which, b, 0))
    ch = min(SEL_CHUNK, seq)
    wk = min(WINDOW + LANES, seq)
    assert seq % ch == 0 and wk % LANES == 0 and nb % 2 == 0 and nb <= LANES
    return pl.pallas_call(
        functools.partial(_nsa_prompt_kernel, nb=nb, ch=ch, wk=wk),
        grid=(n_seq, nb),
        in_specs=[chunk(NSA_WIDTH), chunk(LANES), cmp_(0), cmp_(1), kv(2), kv(3), kv(4), kv(5),
                  pl.BlockSpec(expand.shape, lambda b, c: (0, 0))],
        out_specs=chunk(NSA_WIDTH),
        out_shape=jax.ShapeDtypeStruct((n_seq * seq, NSA_WIDTH), BF16),
        scratch_shapes=[pltpu.VMEM((seq // ch, NSA_KV * BLK, ch), F32),
                        pltpu.VMEM((seq // ch, rows, ch), F32),
                        pltpu.VMEM((rows, LANES), F32),
                        pltpu.VMEM((rows, 2 * KV_WIDTH), F32)],
        compiler_params=_params("parallel", "arbitrary"),
        name="nsa_prompt",
    )(q, gates, cmp_kv, cmp_kv, kvb, kvb, kvb, kvb, expand)


def _nsa_sample_kernel(pt_ref, q_ref, gate_ref, kc_ref, vc_ref, cache_ref, nks_ref, nvs_ref,
                       wt_ref, nkw_ref, nvw_ref, e_ref, o_ref, ks_buf, vs_buf, s_scr, sems,
                       *, nq, past, nbp, tk):
    b = pl.program_id(0)
    n_pages = past // PAGE_SIZE
    rows = NSA_HEADS * nq

    def page_copy(j, part, buf, sem):
        k0 = pl.multiple_of(j * PAGE_SIZE, PAGE_SIZE)
        return pltpu.make_async_copy(cache_ref.at[pt_ref[b * n_pages + j], part],
                                     buf.at[:, pl.ds(k0, PAGE_SIZE)], sem)

    def start(j, _):
        page_copy(j, 2, ks_buf, sems.at[0]).start()
        page_copy(j, 3, vs_buf, sems.at[1]).start()
        return 0

    lax.fori_loop(0, n_pages, start, 0)

    q2 = _stack_queries(q_ref[0], nq)
    q_pos = past + lax.broadcasted_iota(jnp.int32, (rows, 1), 0) % nq
    cur = past // BLK

    n_ids = lax.broadcasted_iota(jnp.int32, (rows, nbp), 1)
    s_c = _dot_nt(q2, kc_ref[0].astype(BF16))
    p_c = _masked_softmax(s_c, (n_ids + 1) * BLK - 1 <= q_pos)
    o_c = _dot(p_c.astype(BF16), vc_ref[0].astype(BF16))

    n_sel = lax.broadcasted_iota(jnp.int32, (nq, nbp), 1)
    sels = []
    for g in range(NSA_KV):
        imp = p_c[g * NSA_REP * nq:(g * NSA_REP + 1) * nq]
        for r in range(1, NSA_REP):
            imp = imp + p_c[(g * NSA_REP + r) * nq:(g * NSA_REP + r + 1) * nq]
        sel = _select_blocks(imp, n_sel, cur, cur + 1)
        sels.extend([sel] * NSA_REP)
    keys = _dot(jnp.concatenate(sels, axis=0).astype(BF16), e_ref[...])
    bias = (keys - 1.0) * (-NEG_INF)

    def col_max(s):
        m = s[:, 0:LANES]
        for j in range(1, s.shape[1] // LANES):
            m = jnp.maximum(m, s[:, j * LANES:(j + 1) * LANES])
        return m

    def col_sum(p):
        t = p[:, 0:LANES]
        for j in range(1, p.shape[1] // LANES):
            t = t + p[:, j * LANES:(j + 1) * LANES]
        return t

    def row_max(mx):
        return jnp.broadcast_to(jnp.max(mx, axis=-1, keepdims=True), mx.shape)

    def tiled(m, width):
        return m if width == LANES else jnp.concatenate([m] * (width // LANES), axis=1)

    new_pos = past + lax.broadcasted_iota(jnp.int32, (rows, LANES), 1)

    def wait(j, _):
        page_copy(j, 2, ks_buf, sems.at[0]).wait()
        page_copy(j, 3, vs_buf, sems.at[1]).wait()
        return 0

    lax.fori_loop(0, n_pages, wait, 0)

    mx = jnp.full((rows, LANES), M_INIT, F32)
    for t in range(past // tk):
        s = _dot(q2, ks_buf[:, t * tk:(t + 1) * tk].astype(BF16)) + bias[:, t * tk:(t + 1) * tk]
        s_scr[:, t * tk:(t + 1) * tk] = s
        mx = jnp.maximum(mx, col_max(s))
    s_new = _dot_nt(q2, nks_ref[0].astype(BF16)) + bias[:, past:past + LANES]
    s_new = jnp.where(new_pos <= q_pos, s_new, NEG_INF)
    m_s = row_max(jnp.maximum(mx, s_new))
    p_new = jnp.exp(s_new - m_s)
    acc = _dot(p_new.astype(BF16), nvs_ref[0].astype(BF16))
    lsum = p_new
    for t in range(past // tk):
        p = jnp.exp(s_scr[:, t * tk:(t + 1) * tk] - tiled(m_s, tk))
        lsum = lsum + col_sum(p)
        acc = acc + _dot_nt(p.astype(BF16), vs_buf[:, t * tk:(t + 1) * tk].astype(BF16))
    o_s = acc / jnp.maximum(jnp.sum(lsum, axis=-1, keepdims=True), 1e-30)

    wlen = wt_ref.shape[3]
    w_pos = past - wlen + lax.broadcasted_iota(jnp.int32, (rows, wlen), 1)
    d = q_pos - w_pos
    s_w = _dot(q2, wt_ref[0, 0].astype(BF16))
    s_w = jnp.where((d >= 0) & (d < WINDOW) & (w_pos >= 0), s_w, NEG_INF)
    d = q_pos - new_pos
    s_nw = _dot_nt(q2, nkw_ref[0].astype(BF16))
    s_nw = jnp.where((d >= 0) & (d < WINDOW), s_nw, NEG_INF)
    m_w = row_max(jnp.maximum(jnp.maximum(col_max(s_w), s_nw), M_INIT))
    p_w = jnp.exp(s_w - tiled(m_w, wlen))
    p_nw = jnp.exp(s_nw - m_w)
    acc = _dot_nt(p_w.astype(BF16), wt_ref[0, 1].astype(BF16)) + _dot(p_nw.astype(BF16), nvw_ref[0].astype(BF16))
    o_w = acc / jnp.maximum(jnp.sum(col_sum(p_w) + p_nw, axis=-1, keepdims=True), 1e-30)

    o_ref[0] = _combine_heads(gate_ref[0], o_c, o_s, o_w, nq).astype(o_ref.dtype)


def _nsa_sample(pt_flat, q3, gates3, kc, vc, cache_t, new_rows, win_t, expand, *, past, tk):
    n_seq, nq, _ = q3.shape
    nbp = kc.shape[1]
    rows = NSA_HEADS * nq
    per_seq = lambda a: pl.BlockSpec((1,) + a.shape[1:], lambda b, pt: (b,) + (0,) * (a.ndim - 1))
    new = lambda col: pl.BlockSpec((1, LANES, KV_WIDTH), lambda b, pt: (b, 0, col))
    grid_spec = pltpu.PrefetchScalarGridSpec(
        num_scalar_prefetch=1,
        grid=(n_seq,),
        in_specs=[per_seq(q3), per_seq(gates3), per_seq(kc), per_seq(vc),
                  pl.BlockSpec(memory_space=pl.ANY),
                  new(2), new(3), per_seq(win_t), new(4), new(5),
                  pl.BlockSpec(expand.shape, lambda b, pt: (0, 0))],
        out_specs=pl.BlockSpec((1, nq, NSA_WIDTH), lambda b, pt: (b, 0, 0)),
        scratch_shapes=[pltpu.VMEM((KV_WIDTH, past), F32), pltpu.VMEM((KV_WIDTH, past), F32),
                        pltpu.VMEM((rows, past), F32), pltpu.SemaphoreType.DMA((2,))])
    return pl.pallas_call(
        functools.partial(_nsa_sample_kernel, nq=nq, past=past, nbp=nbp, tk=tk),
        grid_spec=grid_spec,
        out_shape=jax.ShapeDtypeStruct((n_seq, nq, NSA_WIDTH), F32),
        compiler_params=_params("arbitrary"),
        name="nsa_sample",
    )(pt_flat, q3, gates3, kc, vc, cache_t, new_rows, new_rows, win_t, new_rows, new_rows, expand)


def _expand_matrix(nb, n_keys):
    return (jnp.arange(n_keys, dtype=jnp.int32)[None, :] // BLK
            == jnp.arange(nb, dtype=jnp.int32)[:, None]).astype(BF16)


def _merge_kernel(x_ref, ssm_ref, nsa_ref, gpre_ref, wm_ref, wbs_ref, wbn_ref, wo_ref, gpost_ref,
                  gx_ref, wxq_ref, x1_ref, qx_ref):
    x = x_ref[...]
    a = _rms(x, gpre_ref[...]).astype(BF16)
    g_ssm = _sigmoid(_dot(a, wm_ref[:, 0:D_MODEL]))
    g_nsa = _sigmoid(_dot(a, wm_ref[:, D_MODEL:2 * D_MODEL]))
    merged = (g_ssm * _dot(ssm_ref[...].astype(BF16), wbs_ref[...])
              + g_nsa * _dot(nsa_ref[...], wbn_ref[...]))
    x1 = x + _rms(_dot(merged.astype(BF16), wo_ref[...]), gpost_ref[...])
    x1_ref[...] = x1
    c = _rms(x1, gx_ref[...]).astype(BF16)
    qx_ref[...] = (_dot(c, wxq_ref[...]) * (X_HEAD_DIM ** -0.5)).astype(BF16)


def _merge(x2d, ssm_y, nsa_o, weights, *, tm, ssm_time_major, n_tab_blocks):
    n = x2d.shape[0]
    steps = n // tm
    row = lambda w_: pl.BlockSpec((tm, w_), lambda i: (i, 0))
    full = lambda a: pl.BlockSpec(a.shape, lambda i: (0,) * a.ndim)
    if ssm_time_major:
        ssm_spec = pl.BlockSpec((tm, SSM_WIDTH), lambda i: (i % n_tab_blocks, i // n_tab_blocks))
    else:
        ssm_spec = row(SSM_WIDTH)
    return pl.pallas_call(
        _merge_kernel,
        grid=(steps,),
        in_specs=[row(D_MODEL), ssm_spec, row(NSA_WIDTH)] + [full(w) for w in weights],
        out_specs=[row(D_MODEL), row(X_WIDTH)],
        out_shape=[jax.ShapeDtypeStruct((n, D_MODEL), F32), jax.ShapeDtypeStruct((n, X_WIDTH), BF16)],
        compiler_params=_params("parallel"),
        name="merge",
    )(x2d, ssm_y, nsa_o, *weights)


def _xattn_kernel(q_ref, k_ref, v_ref, o_ref):
    q = q_ref[0].astype(BF16)
    outs = []
    for h in range(X_HEADS):
        cols = slice(h * X_HEAD_DIM, (h + 1) * X_HEAD_DIM)
        s = _dot_nt(q[:, cols], k_ref[0, :, cols].astype(BF16))
        m = jnp.max(s, axis=-1, keepdims=True)
        p = jnp.exp(s - m)
        p = p / jnp.sum(p, axis=-1, keepdims=True)
        outs.append(_dot(p.astype(BF16), v_ref[0, :, cols].astype(BF16)))
    o_ref[0] = jnp.concatenate(outs, axis=1).astype(o_ref.dtype)


def _xattn(q3, mem_kv3, *, tq):
    n_seq, t, _ = q3.shape
    m_len = mem_kv3.shape[1]
    return pl.pallas_call(
        _xattn_kernel,
        grid=(n_seq, t // tq),
        in_specs=[pl.BlockSpec((1, tq, X_WIDTH), lambda b, i: (b, i, 0)),
                  pl.BlockSpec((1, m_len, X_WIDTH), lambda b, i: (b, 0, 0)),
                  pl.BlockSpec((1, m_len, X_WIDTH), lambda b, i: (b, 0, 1))],
        out_specs=pl.BlockSpec((1, tq, X_WIDTH), lambda b, i: (b, i, 0)),
        out_shape=jax.ShapeDtypeStruct((n_seq, t, X_WIDTH), q3.dtype),
        compiler_params=_params("parallel", "parallel"),
        name="xattn",
    )(q3, mem_kv3, mem_kv3)


def _mlp_kernel(x1_ref, o_ref, wxo_ref, gxp_ref, gm_ref, wup_ref, wdn_ref, gmp_ref, y_ref):
    x2 = x1_ref[...] + _rms(_dot(o_ref[...], wxo_ref[...]), gxp_ref[...])
    m = _rms(x2, gm_ref[...]).astype(BF16)
    hid = jnp.maximum(_dot(m, wup_ref[...]), 0.0)
    hid = (hid * hid).astype(BF16)
    y_ref[...] = x2 + _rms(_dot(hid, wdn_ref[...]), gmp_ref[...])


def _mlp(x1, o, weights, *, tm):
    n = x1.shape[0]
    row = lambda w_: pl.BlockSpec((tm, w_), lambda i: (i, 0))
    full = lambda a: pl.BlockSpec(a.shape, lambda i: (0,) * a.ndim)
    return pl.pallas_call(
        _mlp_kernel,
        grid=(n // tm,),
        in_specs=[row(D_MODEL), row(X_WIDTH)] + [full(w) for w in weights],
        out_specs=row(D_MODEL),
        out_shape=jax.ShapeDtypeStruct((n, D_MODEL), F32),
        compiler_params=_params("parallel"),
        name="mlp",
    )(x1, o, *weights)


def _memkv_kernel(m_ref, g_ref, w_ref, o_ref):
    o_ref[...] = _dot(_rms(m_ref[...], g_ref[...]).astype(BF16), w_ref[...])


def _memkv(mem2d, g, w, *, tm):
    n = mem2d.shape[0]
    return pl.pallas_call(
        _memkv_kernel,
        grid=(n // tm,),
        in_specs=[pl.BlockSpec((tm, D_MODEL), lambda i: (i, 0)),
                  pl.BlockSpec(g.shape, lambda i: (0, 0)), pl.BlockSpec(w.shape, lambda i: (0, 0))],
        out_specs=pl.BlockSpec((tm, 2 * X_WIDTH), lambda i: (i, 0)),
        out_shape=jax.ShapeDtypeStruct((n, 2 * X_WIDTH), F32),
        compiler_params=_params("parallel"),
        name="memkv",
    )(mem2d, g, w)


def _row(v):
    return v.astype(F32).reshape(1, -1)


def kernel(x_prompt, x_sample, cache_nsa_kv, cache_win_kv, state_ssm, cache_mem_kv, page_table, mem_prompt, g_mix_pre, w_in, ssm_lam_re, ssm_lam_im, ssm_log_dt, ssm_b_re, ssm_b_im, ssm_c_re, ssm_c_im, ssm_d, w_glu, b_glu, cmp_pe_k, w_cmpk1, w_cmpk2, cmp_pe_v, w_cmpv1, w_cmpv2, w_br_ssm, w_br_nsa, w_out, g_mix_post, g_x_pre, g_mem, w_xq, w_xk, w_xv, w_xo, g_x_post, g_mlp_pre, w_up, w_down, g_mlp_post):
    depth = w_in.shape[0]
    n_seq_p, seq, _ = x_prompt.shape
    n_seq_s, nq, _ = x_sample.shape
    past = page_table.shape[1] * PAGE_SIZE
    assert depth == 1 and seq % BLK == 0 and nq <= SUBLANES and past % BLK == 0
    assert n_seq_p == SUBLANES and n_seq_s % SUBLANES == 0

    y_p = x_prompt.reshape(n_seq_p * seq, D_MODEL)
    y_s = x_sample.reshape(n_seq_s * nq, D_MODEL)
    l = 0

    w_proj = w_in[l, :, :N_PROJ].astype(BF16)
    w_gate = jnp.pad(w_in[l, :, N_PROJ:N_PROJ + N_GATE], ((0, 0), (0, LANES - N_GATE))).astype(BF16)
    w_merge = w_in[l, :, N_PROJ + N_GATE:].astype(BF16)
    lam_l, bm, cm = _ssm_params(ssm_lam_re[l], ssm_lam_im[l], ssm_log_dt[l], ssm_b_re[l], ssm_b_im[l],
                                ssm_c_re[l], ssm_c_im[l])
    ssm_w = (lam_l, bm, cm, _row(ssm_d[l]), w_glu[l].astype(BF16), _row(b_glu[l]))
    cmp_w = _compress_params(cmp_pe_k[l], w_cmpk1[l], w_cmpk2[l], cmp_pe_v[l], w_cmpv1[l], w_cmpv2[l])
    merge_w = (_row(g_mix_pre[l]), w_merge, w_br_ssm[l].astype(BF16), w_br_nsa[l].astype(BF16),
               w_out[l].astype(BF16), _row(g_mix_post[l]), _row(g_x_pre[l]), w_xq[l].astype(BF16))
    mlp_w = (w_xo[l].astype(BF16), _row(g_x_post[l]), _row(g_mlp_pre[l]), w_up[l].astype(BF16),
             w_down[l].astype(BF16), _row(g_mlp_post[l]))
    w_mem = jnp.concatenate([w_xk[l], w_xv[l]], axis=1).astype(BF16)

    tm_p = 512 if seq % 512 == 0 else seq
    nt_p = seq // tm_p
    tabs_p = _rope_tables(jnp.arange(seq, dtype=jnp.int32))
    u_p, q_p, kv_p, win_p, kvb_p, gate_p = _proj(y_p, _row(g_mix_pre[l]), w_proj, w_gate, tabs_p,
                                                 tm=tm_p, n_tab_blocks=nt_p, u_time_major=True)
    u_p = u_p.reshape(seq * n_seq_p, SSM_WIDTH)
    h0_p = jnp.zeros((1, SUBLANES, 2 * N_STATE), F32)
    tc_p = 64 if seq % 64 == 0 else seq
    ssm_p, hl_p = _ssm(u_p, h0_p, *ssm_w, n_groups=1, n_time=seq, tc=tc_p)
    ssm_p = ssm_p.reshape(seq, n_seq_p * SSM_WIDTH)

    nb_p = seq // BLK
    n_blocks_p = n_seq_p * nb_p
    cmp_p = _compress(kv_p, *cmp_w, n_blocks=n_blocks_p, tm=min(n_blocks_p, 256))
    nsa_p = _nsa_prompt(q_p, gate_p, cmp_p, kvb_p, _expand_matrix(LANES, seq), n_seq=n_seq_p, seq=seq)

    mem_kv_p = _memkv(mem_prompt.reshape(-1, D_MODEL), _row(g_mem[l]), w_mem, tm=256)
    m_len = mem_prompt.shape[1]
    mem_kv_p3 = mem_kv_p.reshape(n_seq_p, m_len, 2 * X_WIDTH)

    x1_p, qx_p = _merge(y_p, ssm_p, nsa_p, merge_w, tm=tm_p, ssm_time_major=True, n_tab_blocks=nt_p)
    o_p = _xattn(qx_p.reshape(n_seq_p, seq, X_WIDTH), mem_kv_p3, tq=tm_p)
    y_p = _mlp(x1_p, o_p.reshape(-1, X_WIDTH), mlp_w, tm=256)

    n_s = n_seq_s * nq
    pos_s = past + jnp.arange(nq, dtype=jnp.int32)
    tabs_s = tuple(jnp.tile(t, (n_seq_s, 1)) for t in _rope_tables(pos_s))
    u_s, q_s, kv_s, win_s, kvb_s, gate_s = _proj(y_s, _row(g_mix_pre[l]), w_proj, w_gate, tabs_s,
                                                 tm=n_s, n_tab_blocks=1, u_time_major=False)
    n_grp = n_seq_s // SUBLANES
    to_tb = lambda a: a.reshape(n_grp, SUBLANES, nq, SSM_WIDTH).transpose(0, 2, 1, 3).reshape(n_s, SSM_WIDTH)
    st = state_ssm[l].astype(F32).reshape(n_seq_s, N_STATE, 2)
    h0_s = _state_lanes(st[..., 0], st[..., 1]).reshape(n_grp, SUBLANES, 2 * N_STATE)
    ssm_s, hl_s = _ssm(to_tb(u_s), h0_s, *ssm_w, n_groups=n_grp, n_time=nq, tc=nq)
    ssm_s = ssm_s.reshape(n_grp, nq, SUBLANES, SSM_WIDTH).transpose(0, 2, 1, 3).reshape(n_s, SSM_WIDTH)

    n_pages = page_table.shape[1]
    n_pool = cache_nsa_kv.shape[1]
    cache_t = jnp.transpose(cache_nsa_kv[l], (0, 2, 3, 4, 1)).reshape(n_pool, 4, KV_WIDTH, PAGE_SIZE)
    win_t = jnp.transpose(cache_win_kv[l], (0, 2, 3, 4, 1)).reshape(n_seq_s, 2, KV_WIDTH, -1)
    pt_flat = page_table.reshape(-1).astype(jnp.int32)
    nb_past = past // BLK
    pe_t, w1_t = _compress_paged_params(cmp_pe_k[l], w_cmpk1[l], cmp_pe_v[l], w_cmpv1[l])
    cmp_pages = _compress_paged(pt_flat, cache_t, pe_t, w1_t, cmp_w[2], m=min(n_seq_s * n_pages, 256))
    cmp_past = cmp_pages.reshape(2, NSA_KV, n_seq_s, n_pages, PAGE_SIZE // BLK, HEAD_DIM)
    cmp_past = cmp_past.transpose(0, 2, 3, 4, 1, 5)
    new_rows = jnp.pad(jnp.concatenate([kv_s, win_s], axis=1).reshape(n_seq_s, nq, 6 * KV_WIDTH),
                       ((0, 0), (0, LANES - nq), (0, 0)))
    cmp_new = _compress(new_rows[:, :BLK].reshape(n_seq_s * BLK, 6 * KV_WIDTH), *cmp_w,
                        n_blocks=n_seq_s, tm=n_seq_s)
    nbp = -(-(nb_past + 1) // LANES) * LANES
    cmp_s = jnp.concatenate([cmp_past.reshape(2, n_seq_s, nb_past, KV_WIDTH), cmp_new[:, :, None, :],
                             jnp.zeros((2, n_seq_s, nbp - nb_past - 1, KV_WIDTH), F32)], axis=2)
    nsa_s = _nsa_sample(pt_flat, q_s.astype(F32).reshape(n_seq_s, nq, NSA_WIDTH),
                        gate_s.reshape(n_seq_s, nq, LANES), cmp_s[0], cmp_s[1], cache_t, new_rows, win_t,
                        _expand_matrix(nbp, past + LANES), past=past, tk=min(past, 1024))

    x1_s, qx_s = _merge(y_s, ssm_s, nsa_s.reshape(n_s, NSA_WIDTH).astype(BF16), merge_w, tm=n_s,
                        ssm_time_major=False, n_tab_blocks=1)
    mem_kv_s3 = cache_mem_kv[l].reshape(n_seq_s, m_len, 2 * X_WIDTH)
    o_s = _xattn(qx_s.astype(F32).reshape(n_seq_s, nq, X_WIDTH), mem_kv_s3, tq=nq)
    y_s = _mlp(x1_s, o_s.reshape(-1, X_WIDTH).astype(BF16), mlp_w, tm=n_s)

    def ssm_state(hl, n_seq):
        re, im = _state_unlanes(hl.reshape(n_seq, 2 * N_STATE))
        return jnp.stack([re, im], axis=-1).reshape(1, n_seq, SSM_GROUPS, SSM_STATE, 2)

    w_keep = min(WINDOW, seq)
    win_prompt = win_p.reshape(n_seq_p, seq, 2, NSA_KV, HEAD_DIM)[:, seq - w_keep:]
    win_new = win_s.reshape(n_seq_s, nq, 2, NSA_KV, HEAD_DIM).astype(cache_win_kv.dtype)
    win_sample = jnp.concatenate([cache_win_kv[l], win_new], axis=1)[:, nq:]
    return (y_p.reshape(n_seq_p, seq, D_MODEL),
            y_s.reshape(n_seq_s, nq, D_MODEL),
            kv_p.reshape(1, n_seq_p, seq, 4, NSA_KV, HEAD_DIM),
            kv_s.reshape(1, n_seq_s, nq, 4, NSA_KV, HEAD_DIM),
            win_prompt[None],
            win_sample[None],
            ssm_state(hl_p, n_seq_p),
            ssm_state(hl_s, n_seq_s),
            mem_kv_p.reshape(1, n_seq_p, m_len, 2, X_HEADS, X_HEAD_DIM))
```

```python
import functools
import math

import jax
import jax.numpy as jnp
from jax import lax
from jax.experimental import pallas as pl
from jax.experimental.pallas import tpu as pltpu

F32 = jnp.float32
BF16 = jnp.bfloat16

D_MODEL = 1024
SSM_WIDTH = 512
SSM_GROUP = 16
SSM_GROUPS = 32
SSM_STATE = 64
N_STATE = SSM_GROUPS * SSM_STATE
STATE_CHUNK = 512
N_CHUNKS = N_STATE // STATE_CHUNK
NSA_HEADS = 8
HEAD_DIM = 64
NSA_WIDTH = NSA_HEADS * HEAD_DIM
NSA_KV = 2
NSA_REP = NSA_HEADS // NSA_KV
KV_WIDTH = NSA_KV * HEAD_DIM
BLK = 64
N_SEL = 16
WINDOW = 512
ROT_DIM = 16
ROPE_THETA = 500000.0
PAGE_SIZE = 128
X_HEADS = 4
X_HEAD_DIM = 128
X_WIDTH = X_HEADS * X_HEAD_DIM
D_FF = 4 * D_MODEL
EPS = 1e-6
NEG_INF = -1e30
M_INIT = -1e29
FORCE_SCORE = 1e4
LANES = 128
SUBLANES = 8
VMEM_LIMIT = 56 * 1024 * 1024

N_PROJ = SSM_WIDTH + NSA_WIDTH + 6 * KV_WIDTH
N_GATE = 3 * NSA_HEADS
SEL_CHUNK = 512
PAGE_PITCH = PAGE_SIZE + SUBLANES


def _params(*sem):
    return pltpu.CompilerParams(dimension_semantics=sem, vmem_limit_bytes=VMEM_LIMIT)


def _rms(x, g):
    return x * lax.rsqrt(jnp.mean(x * x, axis=-1, keepdims=True) + EPS) * g


def _gelu(x):
    return 0.5 * x * (1.0 + jnp.tanh(math.sqrt(2.0 / math.pi) * (x + 0.044715 * (x * x * x))))


def _sigmoid(x):
    return 1.0 / (1.0 + jnp.exp(-x))


def _dot(a, b):
    return jnp.dot(a, b, preferred_element_type=F32)


def _dot_nt(a, b):
    return lax.dot_general(a, b, (((1,), (1,)), ((), ())), preferred_element_type=F32)


def _proj_kernel(x_ref, g_ref, w_ref, wg_ref, cos_ref, sp_ref, sm_ref,
                 u_ref, q_ref, kv_ref, win_ref, kvb_ref, gate_ref):
    a = _rms(x_ref[...], g_ref[...]).astype(BF16)
    cos, sp, sm = cos_ref[...], sp_ref[...], sm_ref[...]

    def rope(blk):
        return blk * cos + pltpu.roll(blk, 8, 1) * sp + pltpu.roll(blk, LANES - 8, 1) * sm

    u_ref[...] = _dot(a, w_ref[:, 0:SSM_WIDTH])
    for j in range(NSA_WIDTH // LANES):
        c0 = SSM_WIDTH + j * LANES
        q_ref[:, j * LANES:(j + 1) * LANES] = rope(_dot(a, w_ref[:, c0:c0 + LANES])).astype(BF16)
    for j in range(6):
        c0 = SSM_WIDTH + NSA_WIDTH + j * LANES
        blk = _dot(a, w_ref[:, c0:c0 + LANES])
        if j % 2 == 0:
            blk = rope(blk)
        if j < 4:
            kv_ref[:, j * LANES:(j + 1) * LANES] = blk
        else:
            win_ref[:, (j - 4) * LANES:(j - 3) * LANES] = blk
        kvb_ref[:, j * LANES:(j + 1) * LANES] = blk.astype(BF16)
    gate_ref[...] = _sigmoid(_dot(a, wg_ref[...]))


def _proj(x2d, g, w, wg, tabs, *, tm, n_tab_blocks, u_time_major):
    n = x2d.shape[0]
    steps = n // tm
    if u_time_major:
        n_b = steps // n_tab_blocks
        u_shape = jax.ShapeDtypeStruct((n_tab_blocks * tm, n_b * SSM_WIDTH), F32)
        u_spec = pl.BlockSpec((tm, SSM_WIDTH), lambda i: (i % n_tab_blocks, i // n_tab_blocks))
    else:
        u_shape = jax.ShapeDtypeStruct((n, SSM_WIDTH), F32)
        u_spec = pl.BlockSpec((tm, SSM_WIDTH), lambda i: (i, 0))
    row = lambda w_: pl.BlockSpec((tm, w_), lambda i: (i, 0))
    full = lambda a: pl.BlockSpec(a.shape, lambda i: (0,) * a.ndim)
    tab = pl.BlockSpec((tm, LANES), lambda i: (i % n_tab_blocks, 0))
    return pl.pallas_call(
        _proj_kernel,
        grid=(steps,),
        in_specs=[row(D_MODEL), full(g), full(w), full(wg), tab, tab, tab],
        out_specs=[u_spec, row(NSA_WIDTH), row(4 * KV_WIDTH), row(2 * KV_WIDTH), row(6 * KV_WIDTH), row(LANES)],
        out_shape=[u_shape,
                   jax.ShapeDtypeStruct((n, NSA_WIDTH), BF16),
                   jax.ShapeDtypeStruct((n, 4 * KV_WIDTH), F32),
                   jax.ShapeDtypeStruct((n, 2 * KV_WIDTH), F32),
                   jax.ShapeDtypeStruct((n, 6 * KV_WIDTH), BF16),
                   jax.ShapeDtypeStruct((n, LANES), F32)],
        compiler_params=_params("parallel"),
        name="proj",
    )(x2d, g, w, wg, *tabs)


def _rope_tables(pos):
    half = ROT_DIM // 2
    freqs = ROPE_THETA ** (-jnp.arange(half, dtype=F32) / half)
    ang = pos.astype(F32)[:, None] * freqs[None, :]
    cos, sin = jnp.cos(ang), jnp.sin(ang)
    r = pos.shape[0]
    z8 = jnp.zeros((r, half), F32)
    rest0 = jnp.zeros((r, HEAD_DIM - ROT_DIM), F32)
    rest1 = jnp.ones((r, HEAD_DIM - ROT_DIM), F32)
    c64 = jnp.concatenate([cos, cos, rest1], axis=1)
    sp64 = jnp.concatenate([z8, sin, rest0], axis=1)
    sm64 = jnp.concatenate([-sin, z8, rest0], axis=1)
    return tuple(jnp.tile(t, (1, LANES // HEAD_DIM)) for t in (c64, sp64, sm64))


def _ssm_kernel(u_ref, h0_ref, lam_ref, bm_ref, cm_ref, d_ref, wglu_ref, bglu_ref,
                y_ref, hlast_ref, hs_ref, hstate_ref, *, tc):
    j = pl.program_id(1)

    @pl.when(j == 0)
    def _():
        hstate_ref[...] = h0_ref[0]

    u = u_ref[...]
    ub = u.astype(BF16)
    half_in = SSM_WIDTH // 2
    half_st = N_STATE
    for h in range(2):
        hs_ref[:, h * half_st:(h + 1) * half_st] = _dot(ub[:, h * half_in:(h + 1) * half_in], bm_ref[h])

    for c in range(N_CHUNKS):
        re0 = c * 2 * STATE_CHUNK
        im0 = re0 + STATE_CHUNK
        lr = jnp.broadcast_to(lam_ref[0:1, re0:re0 + STATE_CHUNK], (SUBLANES, STATE_CHUNK))
        li = jnp.broadcast_to(lam_ref[0:1, im0:im0 + STATE_CHUNK], (SUBLANES, STATE_CHUNK))

        def step(t, carry, re0=re0, im0=im0, lr=lr, li=li):
            hr, hi = carry
            r0 = pl.multiple_of(t * SUBLANES, SUBLANES)
            nr = lr * hr - li * hi + hs_ref[pl.ds(r0, SUBLANES), re0:re0 + STATE_CHUNK]
            ni = lr * hi + li * hr + hs_ref[pl.ds(r0, SUBLANES), im0:im0 + STATE_CHUNK]
            hs_ref[pl.ds(r0, SUBLANES), re0:re0 + STATE_CHUNK] = nr
            hs_ref[pl.ds(r0, SUBLANES), im0:im0 + STATE_CHUNK] = ni
            return nr, ni

        hr, hi = lax.fori_loop(0, tc, step,
                               (hstate_ref[:, re0:re0 + STATE_CHUNK], hstate_ref[:, im0:im0 + STATE_CHUNK]),
                               unroll=min(tc, 4))
        hstate_ref[:, re0:re0 + STATE_CHUNK] = hr
        hstate_ref[:, im0:im0 + STATE_CHUNK] = hi

    ys = [_dot(hs_ref[:, h * half_st:(h + 1) * half_st].astype(BF16), cm_ref[h]) for h in range(2)]
    y = jnp.concatenate(ys, axis=1) + d_ref[...] * u
    y = _gelu(y)
    z = _dot(y.astype(BF16), wglu_ref[...]) + bglu_ref[...]
    y_ref[...] = y * _sigmoid(z)

    @pl.when(j == pl.num_programs(1) - 1)
    def _():
        hlast_ref[0] = hstate_ref[...]


def _ssm(u_tb, h0, lam, bm, cm, d, wglu, bglu, *, n_groups, n_time, tc):
    rows = tc * SUBLANES
    nt = n_time // tc
    full = lambda a: pl.BlockSpec(a.shape, lambda g, j: (0,) * a.ndim)
    st = pl.BlockSpec((1, SUBLANES, 2 * N_STATE), lambda g, j: (g, 0, 0))
    return pl.pallas_call(
        functools.partial(_ssm_kernel, tc=tc),
        grid=(n_groups, nt),
        in_specs=[pl.BlockSpec((rows, SSM_WIDTH), lambda g, j: (g * nt + j, 0)), st,
                  full(lam), full(bm), full(cm), full(d), full(wglu), full(bglu)],
        out_specs=[pl.BlockSpec((rows, SSM_WIDTH), lambda g, j: (g * nt + j, 0)), st],
        out_shape=[jax.ShapeDtypeStruct((n_groups * n_time * SUBLANES, SSM_WIDTH), F32),
                   jax.ShapeDtypeStruct((n_groups, SUBLANES, 2 * N_STATE), F32)],
        scratch_shapes=[pltpu.VMEM((rows, 2 * N_STATE), F32), pltpu.VMEM((SUBLANES, 2 * N_STATE), F32)],
        compiler_params=_params("parallel", "arbitrary"),
        name="ssm",
    )(u_tb, h0, lam, bm, cm, d, wglu, bglu)


def _state_lanes(re, im):
    lead = re.shape[:-1]
    r = re.reshape(lead + (N_CHUNKS, 1, STATE_CHUNK))
    i = im.reshape(lead + (N_CHUNKS, 1, STATE_CHUNK))
    return jnp.concatenate([r, i], axis=-2).reshape(lead + (2 * N_STATE,))


def _state_unlanes(x):
    lead = x.shape[:-1]
    y = x.reshape(lead + (N_CHUNKS, 2, STATE_CHUNK))
    return y[..., 0, :].reshape(lead + (N_STATE,)), y[..., 1, :].reshape(lead + (N_STATE,))


def _ssm_params(lam_re, lam_im, log_dt, b_re, b_im, c_re, c_im):
    lam = lax.complex(lam_re.astype(F32), lam_im.astype(F32))
    dt = jnp.exp(log_dt.astype(F32))[:, None]
    lam_bar = jnp.exp(lam * dt)
    b = lax.complex(b_re.astype(F32), b_im.astype(F32))
    b_bar = ((lam_bar - 1.0) / lam)[..., None] * b
    eye = jnp.eye(SSM_GROUPS, dtype=F32)

    def in_blockdiag(x):
        return jnp.einsum('gpc,gh->gchp', x, eye).reshape(SSM_WIDTH, N_STATE)

    def out_blockdiag(x):
        return jnp.einsum('gcp,gh->gphc', x, eye).reshape(N_STATE, SSM_WIDTH)

    b_full = _state_lanes(in_blockdiag(jnp.real(b_bar)), in_blockdiag(jnp.imag(b_bar)))
    c_full = _state_lanes(out_blockdiag(c_re.astype(F32)).T, -out_blockdiag(c_im.astype(F32)).T).T
    hw, hs = SSM_WIDTH // 2, N_STATE
    bm = jnp.stack([b_full[h * hw:(h + 1) * hw, h * hs:(h + 1) * hs] for h in range(2)]).astype(BF16)
    cm = jnp.stack([c_full[h * hs:(h + 1) * hs, h * hw:(h + 1) * hw] for h in range(2)]).astype(BF16)
    lam_l = _state_lanes(jnp.real(lam_bar).reshape(1, N_STATE), jnp.imag(lam_bar).reshape(1, N_STATE))
    return lam_l, bm, cm


def _compress_kernel(x_ref, pe_ref, w1_ref, w2_ref, o_ref, *, tm):
    def body(sp, acc):
        s0 = 2 * sp
        xa = x_ref[pl.ds(s0, tm, stride=BLK), :] + pe_ref[0, pl.ds(s0, 1), :]
        xb = x_ref[pl.ds(s0 + 1, tm, stride=BLK), :] + pe_ref[0, pl.ds(s0 + 1, 1), :]
        lhs = jnp.concatenate([xa, xb], axis=1).astype(BF16)
        return acc + _dot(lhs, w1_ref[0, sp])

    acc = lax.fori_loop(0, BLK // 2, body, jnp.zeros((tm, KV_WIDTH), F32))
    o_ref[0] = _dot(_gelu(acc).astype(BF16), w2_ref[0])


def _compress(x2d, pe, w1, w2, *, n_blocks, tm):
    return pl.pallas_call(
        functools.partial(_compress_kernel, tm=tm),
        grid=(2, n_blocks // tm),
        in_specs=[pl.BlockSpec((tm * BLK, KV_WIDTH), lambda c, i: (i, c)),
                  pl.BlockSpec((1, BLK, KV_WIDTH), lambda c, i: (c, 0, 0)),
                  pl.BlockSpec((1, BLK // 2, 2 * KV_WIDTH, KV_WIDTH), lambda c, i: (c, 0, 0, 0)),
                  pl.BlockSpec((1, KV_WIDTH, KV_WIDTH), lambda c, i: (c, 0, 0))],
        out_specs=pl.BlockSpec((1, tm, KV_WIDTH), lambda c, i: (c, i, 0)),
        out_shape=jax.ShapeDtypeStruct((2, n_blocks, KV_WIDTH), F32),
        compiler_params=_params("parallel", "parallel"),
        name="compress",
    )(x2d, pe, w1, w2)


def _compress_params(pe_k, w1_k, w2_k, pe_v, w1_v, w2_v):
    def bd(w):
        z = jnp.zeros_like(w)
        return jnp.concatenate([jnp.concatenate([w, z], axis=-1), jnp.concatenate([z, w], axis=-1)], axis=-2)

    def one(pe, w1, w2):
        w1s = bd(w1.astype(F32).reshape(BLK, HEAD_DIM, HEAD_DIM))
        return (jnp.tile(pe.astype(F32), (1, NSA_KV)),
                w1s.reshape(BLK // 2, 2 * KV_WIDTH, KV_WIDTH).astype(BF16),
                bd(w2.astype(F32)).astype(BF16))

    k, v = one(pe_k, w1_k, w2_k), one(pe_v, w1_v, w2_v)
    return tuple(jnp.stack([a, b]) for a, b in zip(k, v))


def _compress_paged_kernel(pt_ref, cache_ref, pe_ref, w1_ref, w2_ref, o_ref, buf_ref, sems, *, m):
    n_grp = pl.num_programs(1)
    step = pl.program_id(0) * n_grp + pl.program_id(1)
    n_steps = pl.num_programs(0) * n_grp
    slot_rows = m * PAGE_PITCH

    def page_copy(s, j):
        slot = s % 2
        row0 = pl.multiple_of(slot * slot_rows + j * PAGE_PITCH, SUBLANES)
        return pltpu.make_async_copy(cache_ref.at[pt_ref[(s % n_grp) * m + j], s // n_grp],
                                     buf_ref.at[pl.ds(row0, PAGE_SIZE), :], sems.at[slot])

    def start_all(s):
        def start(j, _):
            page_copy(s, j).start()
            return 0
        lax.fori_loop(0, m, start, 0)

    @pl.when(step == 0)
    def _():
        start_all(step)

    @pl.when(step + 1 < n_steps)
    def _():
        start_all(step + 1)

    def wait(j, _):
        page_copy(step, j).wait()
        return 0

    lax.fori_loop(0, m, wait, 0)
    base = (step % 2) * slot_rows

    for kv in range(NSA_KV):
        def body(dp, acc, kv=kv):
            d0 = 2 * dp
            r0 = base + kv * HEAD_DIM + d0
            xa = buf_ref[pl.ds(r0, m, stride=PAGE_PITCH), :] + pe_ref[0, pl.ds(d0, 1), :]
            xb = buf_ref[pl.ds(r0 + 1, m, stride=PAGE_PITCH), :] + pe_ref[0, pl.ds(d0 + 1, 1), :]
            lhs = jnp.concatenate([xa, xb], axis=1).astype(BF16)
            return acc + _dot(lhs, w1_ref[0, dp])

        acc = lax.fori_loop(0, HEAD_DIM // 2, body, jnp.zeros((m, PAGE_SIZE), F32))
        o_ref[0, kv] = _dot(_gelu(acc).astype(BF16), w2_ref[0])


def _compress_paged(pt_flat, cache_t, pe_t, w1_t, w2, *, m):
    n = pt_flat.shape[0]
    grid_spec = pltpu.PrefetchScalarGridSpec(
        num_scalar_prefetch=1,
        grid=(2, n // m),
        in_specs=[pl.BlockSpec(memory_space=pl.ANY),
                  pl.BlockSpec((1, HEAD_DIM, PAGE_SIZE), lambda c, i, pt: (c, 0, 0)),
                  pl.BlockSpec((1, HEAD_DIM // 2, 2 * PAGE_SIZE, PAGE_SIZE), lambda c, i, pt: (c, 0, 0, 0)),
                  pl.BlockSpec((1, PAGE_SIZE, PAGE_SIZE), lambda c, i, pt: (c, 0, 0))],
        out_specs=pl.BlockSpec((1, NSA_KV, m, PAGE_SIZE), lambda c, i, pt: (c, 0, i, 0)),
        scratch_shapes=[pltpu.VMEM((2 * m * PAGE_PITCH, PAGE_SIZE), F32), pltpu.SemaphoreType.DMA((2,))])
    return pl.pallas_call(
        functools.partial(_compress_paged_kernel, m=m),
        grid_spec=grid_spec,
        out_shape=jax.ShapeDtypeStruct((2, NSA_KV, n, PAGE_SIZE), F32),
        compiler_params=_params("arbitrary", "arbitrary"),
        name="compress_paged",
    )(pt_flat, cache_t, pe_t, w1_t, w2)


def _compress_paged_params(pe_k, w1_k, pe_v, w1_v):
    def bd(w):
        z = jnp.zeros_like(w)
        return jnp.concatenate([jnp.concatenate([w, z], axis=-1), jnp.concatenate([z, w], axis=-1)], axis=-2)

    def one(pe, w1):
        w1d = bd(w1.astype(F32).reshape(BLK, HEAD_DIM, HEAD_DIM).transpose(1, 0, 2))
        return (jnp.tile(pe.astype(F32).T, (1, PAGE_SIZE // BLK)),
                w1d.reshape(HEAD_DIM // 2, 2 * PAGE_SIZE, PAGE_SIZE).astype(BF16))

    k, v = one(pe_k, w1_k), one(pe_v, w1_v)
    return tuple(jnp.stack([a, b]) for a, b in zip(k, v))


def _stack_queries(q, nq):
    q = q.astype(F32)
    z = jnp.zeros((nq, HEAD_DIM), F32)
    rows = []
    for h in range(NSA_HEADS):
        blk = q[:, h * HEAD_DIM:(h + 1) * HEAD_DIM]
        rows.append(jnp.concatenate([blk, z] if h < NSA_REP else [z, blk], axis=1))
    return (jnp.concatenate(rows, axis=0) * (HEAD_DIM ** -0.5)).astype(BF16)


def _masked_softmax(s, valid):
    s = jnp.where(valid, s, NEG_INF)
    m = jnp.max(s, axis=-1, keepdims=True)
    p = jnp.exp(s - m) * valid.astype(F32)
    return p / jnp.maximum(jnp.sum(p, axis=-1, keepdims=True), 1e-30)


def _select_blocks(imp, n_ids, cur, nb):
    forced = (n_ids == 0) | (n_ids == cur) | (n_ids == cur - 1)
    imp = jnp.where(forced, FORCE_SCORE, imp)
    imp = jnp.where(n_ids <= cur, imp, -FORCE_SCORE)
    rank = jnp.zeros(imp.shape, F32)
    for m in range(nb):
        col = imp[:, m:m + 1]
        beats = (col > imp) | ((col == imp) & (n_ids > m))
        rank = rank + beats.astype(F32)
    return (rank < float(N_SEL)).astype(F32)


def _select_blocks_t(imp_t, cur, nb):
    n_t = lax.broadcasted_iota(jnp.int32, imp_t.shape, 0)
    forced = (n_t == 0) | (n_t == cur) | (n_t == cur - 1)
    imp_t = jnp.where(forced, FORCE_SCORE, imp_t)
    imp_t = jnp.where(n_t <= cur, imp_t, -FORCE_SCORE)
    rank = jnp.zeros(imp_t.shape, F32)
    for m in range(nb):
        row = imp_t[m:m + 1, :]
        beats = (row > imp_t) | ((row == imp_t) & (n_t > m))
        rank = rank + beats.astype(F32)
    return (rank < float(N_SEL)).astype(F32)


def _combine_heads(gates, o_c, o_s, o_w, nq):
    outs = []
    for h in range(NSA_HEADS):
        rows = slice(h * nq, (h + 1) * nq)
        o = (gates[:, 3 * h:3 * h + 1] * o_c[rows] + gates[:, 3 * h + 1:3 * h + 2] * o_s[rows]
             + gates[:, 3 * h + 2:3 * h + 3] * o_w[rows])
        g = h // NSA_REP
        outs.append(o[:, g * HEAD_DIM:(g + 1) * HEAD_DIM])
    return jnp.concatenate(outs, axis=1)


def _nsa_prompt_kernel(q_ref, gate_ref, kc_ref, vc_ref, ks_ref, vs_ref, kw_ref, vw_ref, e_ref,
                       o_ref, bias_ref, s_ref, mx_ref, acc_ref, ow_ref, *, nb, ch, wk):
    ci = pl.program_id(1)
    nq = BLK
    rows = NSA_HEADS * nq
    q2 = _stack_queries(q_ref[...], nq)
    q_pos = ci * BLK + lax.broadcasted_iota(jnp.int32, (rows, 1), 0) % nq
    d0 = (ci * BLK + lax.broadcasted_iota(jnp.int32, (rows, LANES), 0) % nq
          - lax.broadcasted_iota(jnp.int32, (rows, LANES), 1))

    def col_blocks(s):
        return [s[:, j * LANES:(j + 1) * LANES] for j in range(s.shape[1] // LANES)]

    def col_max(cols):
        m = cols[0]
        for c_ in cols[1:]:
            m = jnp.maximum(m, c_)
        return m

    def finish(acc):
        return acc[:, 0:KV_WIDTH] / jnp.maximum(acc[:, KV_WIDTH:2 * KV_WIDTH], 1e-30)

    def with_ones(v):
        return jnp.concatenate([v, jnp.ones((v.shape[0], LANES), BF16)], axis=1)

    first = jnp.maximum(ci - WINDOW // BLK, 0) // (LANES // BLK)
    w0 = pl.multiple_of(first * LANES, LANES)
    cols = col_blocks(_dot_nt(q2, kw_ref[pl.ds(w0, wk), :]))
    last_col = len(cols) - 1
    masked = []
    for j, c_ in enumerate(cols):
        d = d0 - (w0 + j * LANES)
        if j == 0 or j == last_col:
            masked.append(jnp.where((d >= 0) & (d < WINDOW), c_, NEG_INF))
        else:
            masked.append(jnp.where(d >= 0, c_, NEG_INF))
    m_w = jnp.maximum(jnp.max(col_max(masked), axis=-1, keepdims=True), M_INIT)
    m_w = jnp.broadcast_to(m_w, (rows, LANES))
    p = jnp.concatenate([jnp.exp(c_ - m_w).astype(BF16) for c_ in masked], axis=1)
    ow_ref[...] = finish(_dot(p, with_ones(vw_ref[pl.ds(w0, wk), :])))

    n_ids = lax.broadcasted_iota(jnp.int32, (rows, nb), 1)
    s_c = _dot_nt(q2, kc_ref[0].astype(BF16))
    p_c = _masked_softmax(s_c, (n_ids + 1) * BLK - 1 <= q_pos)
    o_c = _dot(p_c.astype(BF16), vc_ref[0].astype(BF16))

    imps = []
    for g in range(NSA_KV):
        imp = p_c[g * NSA_REP * nq:(g * NSA_REP + 1) * nq]
        for r in range(1, NSA_REP):
            imp = imp + p_c[(g * NSA_REP + r) * nq:(g * NSA_REP + r + 1) * nq]
        imps.append(imp)
    imp2 = jnp.concatenate([jnp.concatenate(imps, axis=0), jnp.zeros((LANES, LANES - nb), F32)], axis=1)
    sel_t = _select_blocks_t(imp2.T[0:nb], ci, nb)
    sel2 = jnp.concatenate([sel_t, jnp.zeros((LANES - nb, LANES), F32)], axis=0).T
    bias = (_dot(sel2.astype(BF16), e_ref[...]) - 1.0) * (-NEG_INF)
    for c in range(nb * BLK // ch):
        bias_ref[c] = bias[:, c * ch:(c + 1) * ch]

    n_ch = ci // (ch // BLK) + 1
    mx_ref[...] = jnp.full(mx_ref.shape, M_INIT, F32)

    def scores(c, causal):
        k0 = pl.multiple_of(c * ch, ch)
        s = _dot_nt(q2, ks_ref[pl.ds(k0, ch), :])
        s = s + jnp.concatenate([bias_ref[c, g * nq:(g + 1) * nq, :]
                                 for g in range(NSA_KV) for _ in range(NSA_REP)], axis=0)
        cols = col_blocks(s)
        if causal:
            cols = [jnp.where(d0 >= k0 + j * LANES, c_, NEG_INF) for j, c_ in enumerate(cols)]
        mx_ref[...] = jnp.maximum(mx_ref[...], col_max(cols))
        for j, c_ in enumerate(cols):
            s_ref[c, :, j * LANES:(j + 1) * LANES] = c_

    def pass1(c, _):
        scores(c, False)
        return 0

    lax.fori_loop(0, n_ch - 1, pass1, 0)
    scores(n_ch - 1, True)

    m_sel = jnp.broadcast_to(jnp.max(mx_ref[...], axis=-1, keepdims=True), (rows, LANES))
    acc_ref[...] = jnp.zeros(acc_ref.shape, F32)

    def pass2(c, _):
        k0 = pl.multiple_of(c * ch, ch)
        p = jnp.concatenate([jnp.exp(s_ref[c, :, j * LANES:(j + 1) * LANES] - m_sel).astype(BF16)
                             for j in range(ch // LANES)], axis=1)
        acc_ref[...] += _dot(p, with_ones(vs_ref[pl.ds(k0, ch), :]))
        return 0

    lax.fori_loop(0, n_ch, pass2, 0)
    o_s = finish(acc_ref[...])

    o_ref[...] = _combine_heads(gate_ref[...], o_c, o_s, ow_ref[...], nq).astype(o_ref.dtype)


def _nsa_prompt(q, gates, cmp_kv, kvb, expand, *, n_seq, seq):
    nb = seq // BLK
    rows = NSA_HEADS * BLK
    chunk = lambda w_: pl.BlockSpec((BLK, w_), lambda b, c: (b * nb + c, 0))
    kv = lambda col: pl.BlockSpec((seq, KV_WIDTH), lambda b, c: (b, col))
    cmp_ = lambda which: pl.BlockSpec((1, nb, KV_WIDTH), lambda b, c: (which, b, 0))
    ch = min(SEL_CHUNK, seq)
    wk = min(WINDOW + LANES, seq)
    assert seq % ch == 0 and wk % LANES == 0 and nb % 2 == 0 and nb <= LANES
    return pl.pallas_call(
        functools.partial(_nsa_prompt_kernel, nb=nb, ch=ch, wk=wk),
        grid=(n_seq, nb),
        in_specs=[chunk(NSA_WIDTH), chunk(LANES), cmp_(0), cmp_(1), kv(2), kv(3), kv(4), kv(5),
                  pl.BlockSpec(expand.shape, lambda b, c: (0, 0))],
        out_specs=chunk(NSA_WIDTH),
        out_shape=jax.ShapeDtypeStruct((n_seq * seq, NSA_WIDTH), BF16),
        scratch_shapes=[pltpu.VMEM((seq // ch, NSA_KV * BLK, ch), F32),
                        pltpu.VMEM((seq // ch, rows, ch), F32),
                        pltpu.VMEM((rows, LANES), F32),
                        pltpu.VMEM((rows, 2 * KV_WIDTH), F32),
                        pltpu.VMEM((rows, KV_WIDTH), F32)],
        compiler_params=_params("parallel", "arbitrary"),
        name="nsa_prompt",
    )(q, gates, cmp_kv, cmp_kv, kvb, kvb, kvb, kvb, expand)


def _nsa_sample_kernel(pt_ref, q_ref, gate_ref, kc_ref, vc_ref, cache_ref, nks_ref, nvs_ref,
                       wt_ref, nkw_ref, nvw_ref, e_ref, o_ref, ks_buf, vs_buf, s_scr, sems,
                       *, nq, past, nbp, tk):
    b = pl.program_id(0)
    n_pages = past // PAGE_SIZE
    rows = NSA_HEADS * nq

    def page_copy(s, j, which):
        slot = s % 2
        k0 = pl.multiple_of(j * PAGE_SIZE, PAGE_SIZE)
        buf = (ks_buf, vs_buf)[which]
        return pltpu.make_async_copy(cache_ref.at[pt_ref[s * n_pages + j], 2 + which],
                                     buf.at[slot, :, pl.ds(k0, PAGE_SIZE)], sems.at[which, slot])

    def start_all(s):
        def start(j, _):
            page_copy(s, j, 0).start()
            page_copy(s, j, 1).start()
            return 0
        lax.fori_loop(0, n_pages, start, 0)

    @pl.when(b == 0)
    def _():
        start_all(b)

    @pl.when(b + 1 < pl.num_programs(0))
    def _():
        start_all(b + 1)

    slot = b % 2
    q2 = _stack_queries(q_ref[0], nq)
    q_pos = past + lax.broadcasted_iota(jnp.int32, (rows, 1), 0) % nq
    cur = past // BLK

    n_ids = lax.broadcasted_iota(jnp.int32, (rows, nbp), 1)
    s_c = _dot_nt(q2, kc_ref[0].astype(BF16))
    p_c = _masked_softmax(s_c, (n_ids + 1) * BLK - 1 <= q_pos)
    o_c = _dot(p_c.astype(BF16), vc_ref[0].astype(BF16))

    n_sel = lax.broadcasted_iota(jnp.int32, (nq, nbp), 1)
    sels = []
    for g in range(NSA_KV):
        imp = p_c[g * NSA_REP * nq:(g * NSA_REP + 1) * nq]
        for r in range(1, NSA_REP):
            imp = imp + p_c[(g * NSA_REP + r) * nq:(g * NSA_REP + r + 1) * nq]
        sel = _select_blocks(imp, n_sel, cur, cur + 1)
        sels.extend([sel] * NSA_REP)
    keys = _dot(jnp.concatenate(sels, axis=0).astype(BF16), e_ref[...])
    bias = (keys - 1.0) * (-NEG_INF)

    def col_max(s):
        m = s[:, 0:LANES]
        for j in range(1, s.shape[1] // LANES):
            m = jnp.maximum(m, s[:, j * LANES:(j + 1) * LANES])
        return m

    def col_sum(p):
        t = p[:, 0:LANES]
        for j in range(1, p.shape[1] // LANES):
            t = t + p[:, j * LANES:(j + 1) * LANES]
        return t

    def row_max(mx):
        return jnp.broadcast_to(jnp.max(mx, axis=-1, keepdims=True), mx.shape)

    def tiled(m, width):
        return m if width == LANES else jnp.concatenate([m] * (width // LANES), axis=1)

    new_pos = past + lax.broadcasted_iota(jnp.int32, (rows, LANES), 1)

    def wait(j, _):
        page_copy(b, j, 0).wait()
        page_copy(b, j, 1).wait()
        return 0

    lax.fori_loop(0, n_pages, wait, 0)

    mx = jnp.full((rows, LANES), M_INIT, F32)
    for t in range(past // tk):
        s = _dot(q2, ks_buf[slot, :, t * tk:(t + 1) * tk].astype(BF16)) + bias[:, t * tk:(t + 1) * tk]
        s_scr[:, t * tk:(t + 1) * tk] = s
        mx = jnp.maximum(mx, col_max(s))
    s_new = _dot_nt(q2, nks_ref[0].astype(BF16)) + bias[:, past:past + LANES]
    s_new = jnp.where(new_pos <= q_pos, s_new, NEG_INF)
    m_s = row_max(jnp.maximum(mx, s_new))
    p_new = jnp.exp(s_new - m_s)
    acc = _dot(p_new.astype(BF16), nvs_ref[0].astype(BF16))
    lsum = p_new
    for t in range(past // tk):
        p = jnp.exp(s_scr[:, t * tk:(t + 1) * tk] - tiled(m_s, tk))
        lsum = lsum + col_sum(p)
        acc = acc + _dot_nt(p.astype(BF16), vs_buf[slot, :, t * tk:(t + 1) * tk].astype(BF16))
    o_s = acc / jnp.maximum(jnp.sum(lsum, axis=-1, keepdims=True), 1e-30)

    wlen = wt_ref.shape[3]
    w_pos = past - wlen + lax.broadcasted_iota(jnp.int32, (rows, wlen), 1)
    d = q_pos - w_pos
    s_w = _dot(q2, wt_ref[0, 0].astype(BF16))
    s_w = jnp.where((d >= 0) & (d < WINDOW) & (w_pos >= 0), s_w, NEG_INF)
    d = q_pos - new_pos
    s_nw = _dot_nt(q2, nkw_ref[0].astype(BF16))
    s_nw = jnp.where((d >= 0) & (d < WINDOW), s_nw, NEG_INF)
    m_w = row_max(jnp.maximum(jnp.maximum(col_max(s_w), s_nw), M_INIT))
    p_w = jnp.exp(s_w - tiled(m_w, wlen))
    p_nw = jnp.exp(s_nw - m_w)
    acc = _dot_nt(p_w.astype(BF16), wt_ref[0, 1].astype(BF16)) + _dot(p_nw.astype(BF16), nvw_ref[0].astype(BF16))
    o_w = acc / jnp.maximum(jnp.sum(col_sum(p_w) + p_nw, axis=-1, keepdims=True), 1e-30)

    o_ref[0] = _combine_heads(gate_ref[0], o_c, o_s, o_w, nq).astype(o_ref.dtype)


def _nsa_sample(pt_flat, q3, gates3, kc, vc, cache_t, new_rows, win_t, expand, *, past, tk):
    n_seq, nq, _ = q3.shape
    nbp = kc.shape[1]
    rows = NSA_HEADS * nq
    per_seq = lambda a: pl.BlockSpec((1,) + a.shape[1:], lambda b, pt: (b,) + (0,) * (a.ndim - 1))
    new = lambda col: pl.BlockSpec((1, LANES, KV_WIDTH), lambda b, pt: (b, 0, col))
    grid_spec = pltpu.PrefetchScalarGridSpec(
        num_scalar_prefetch=1,
        grid=(n_seq,),
        in_specs=[per_seq(q3), per_seq(gates3), per_seq(kc), per_seq(vc),
                  pl.BlockSpec(memory_space=pl.ANY),
                  new(2), new(3), per_seq(win_t), new(4), new(5),
                  pl.BlockSpec(expand.shape, lambda b, pt: (0, 0))],
        out_specs=pl.BlockSpec((1, nq, NSA_WIDTH), lambda b, pt: (b, 0, 0)),
        scratch_shapes=[pltpu.VMEM((2, KV_WIDTH, past), F32), pltpu.VMEM((2, KV_WIDTH, past), F32),
                        pltpu.VMEM((rows, past), F32), pltpu.SemaphoreType.DMA((2, 2))])
    return pl.pallas_call(
        functools.partial(_nsa_sample_kernel, nq=nq, past=past, nbp=nbp, tk=tk),
        grid_spec=grid_spec,
        out_shape=jax.ShapeDtypeStruct((n_seq, nq, NSA_WIDTH), F32),
        compiler_params=_params("arbitrary"),
        name="nsa_sample",
    )(pt_flat, q3, gates3, kc, vc, cache_t, new_rows, new_rows, win_t, new_rows, new_rows, expand)


def _expand_matrix(nb, n_keys):
    return (jnp.arange(n_keys, dtype=jnp.int32)[None, :] // BLK
            == jnp.arange(nb, dtype=jnp.int32)[:, None]).astype(BF16)


def _merge_kernel(x_ref, ssm_ref, nsa_ref, gpre_ref, wm_ref, wbs_ref, wbn_ref, wo_ref, gpost_ref,
                  gx_ref, wxq_ref, x1_ref, qx_ref):
    x = x_ref[...]
    a = _rms(x, gpre_ref[...]).astype(BF16)
    g_ssm = _sigmoid(_dot(a, wm_ref[:, 0:D_MODEL]))
    g_nsa = _sigmoid(_dot(a, wm_ref[:, D_MODEL:2 * D_MODEL]))
    merged = (g_ssm * _dot(ssm_ref[...].astype(BF16), wbs_ref[...])
              + g_nsa * _dot(nsa_ref[...], wbn_ref[...]))
    x1 = x + _rms(_dot(merged.astype(BF16), wo_ref[...]), gpost_ref[...])
    x1_ref[...] = x1
    c = _rms(x1, gx_ref[...]).astype(BF16)
    qx_ref[...] = (_dot(c, wxq_ref[...]) * (X_HEAD_DIM ** -0.5)).astype(BF16)


def _merge(x2d, ssm_y, nsa_o, weights, *, tm, ssm_time_major, n_tab_blocks):
    n = x2d.shape[0]
    steps = n // tm
    row = lambda w_: pl.BlockSpec((tm, w_), lambda i: (i, 0))
    full = lambda a: pl.BlockSpec(a.shape, lambda i: (0,) * a.ndim)
    if ssm_time_major:
        ssm_spec = pl.BlockSpec((tm, SSM_WIDTH), lambda i: (i % n_tab_blocks, i // n_tab_blocks))
    else:
        ssm_spec = row(SSM_WIDTH)
    return pl.pallas_call(
        _merge_kernel,
        grid=(steps,),
        in_specs=[row(D_MODEL), ssm_spec, row(NSA_WIDTH)] + [full(w) for w in weights],
        out_specs=[row(D_MODEL), row(X_WIDTH)],
        out_shape=[jax.ShapeDtypeStruct((n, D_MODEL), F32), jax.ShapeDtypeStruct((n, X_WIDTH), BF16)],
        compiler_params=_params("parallel"),
        name="merge",
    )(x2d, ssm_y, nsa_o, *weights)


def _xattn_kernel(q_ref, k_ref, v_ref, o_ref):
    q = q_ref[0].astype(BF16)
    outs = []
    for h in range(X_HEADS):
        cols = slice(h * X_HEAD_DIM, (h + 1) * X_HEAD_DIM)
        s = _dot_nt(q[:, cols], k_ref[0, :, cols].astype(BF16))
        m = jnp.max(s, axis=-1, keepdims=True)
        p = jnp.exp(s - m)
        p = p / jnp.sum(p, axis=-1, keepdims=True)
        outs.append(_dot(p.astype(BF16), v_ref[0, :, cols].astype(BF16)))
    o_ref[0] = jnp.concatenate(outs, axis=1).astype(o_ref.dtype)


def _xattn(q3, mem_kv3, *, tq):
    n_seq, t, _ = q3.shape
    m_len = mem_kv3.shape[1]
    return pl.pallas_call(
        _xattn_kernel,
        grid=(n_seq, t // tq),
        in_specs=[pl.BlockSpec((1, tq, X_WIDTH), lambda b, i: (b, i, 0)),
                  pl.BlockSpec((1, m_len, X_WIDTH), lambda b, i: (b, 0, 0)),
                  pl.BlockSpec((1, m_len, X_WIDTH), lambda b, i: (b, 0, 1))],
        out_specs=pl.BlockSpec((1, tq, X_WIDTH), lambda b, i: (b, i, 0)),
        out_shape=jax.ShapeDtypeStruct((n_seq, t, X_WIDTH), q3.dtype),
        compiler_params=_params("parallel", "parallel"),
        name="xattn",
    )(q3, mem_kv3, mem_kv3)


def _mlp_kernel(x1_ref, o_ref, wxo_ref, gxp_ref, gm_ref, wup_ref, wdn_ref, gmp_ref, y_ref):
    x2 = x1_ref[...] + _rms(_dot(o_ref[...], wxo_ref[...]), gxp_ref[...])
    m = _rms(x2, gm_ref[...]).astype(BF16)
    hid = jnp.maximum(_dot(m, wup_ref[...]), 0.0)
    hid = (hid * hid).astype(BF16)
    y_ref[...] = x2 + _rms(_dot(hid, wdn_ref[...]), gmp_ref[...])


def _mlp(x1, o, weights, *, tm):
    n = x1.shape[0]
    row = lambda w_: pl.BlockSpec((tm, w_), lambda i: (i, 0))
    full = lambda a: pl.BlockSpec(a.shape, lambda i: (0,) * a.ndim)
    return pl.pallas_call(
        _mlp_kernel,
        grid=(n // tm,),
        in_specs=[row(D_MODEL), row(X_WIDTH)] + [full(w) for w in weights],
        out_specs=row(D_MODEL),
        out_shape=jax.ShapeDtypeStruct((n, D_MODEL), F32),
        compiler_params=_params("parallel"),
        name="mlp",
    )(x1, o, *weights)


def _memkv_kernel(m_ref, g_ref, w_ref, o_ref):
    o_ref[...] = _dot(_rms(m_ref[...], g_ref[...]).astype(BF16), w_ref[...])


def _memkv(mem2d, g, w, *, tm):
    n = mem2d.shape[0]
    return pl.pallas_call(
        _memkv_kernel,
        grid=(n // tm,),
        in_specs=[pl.BlockSpec((tm, D_MODEL), lambda i: (i, 0)),
                  pl.BlockSpec(g.shape, lambda i: (0, 0)), pl.BlockSpec(w.shape, lambda i: (0, 0))],
        out_specs=pl.BlockSpec((tm, 2 * X_WIDTH), lambda i: (i, 0)),
        out_shape=jax.ShapeDtypeStruct((n, 2 * X_WIDTH), F32),
        compiler_params=_params("parallel"),
        name="memkv",
    )(mem2d, g, w)


def _row(v):
    return v.astype(F32).reshape(1, -1)


def kernel(x_prompt, x_sample, cache_nsa_kv, cache_win_kv, state_ssm, cache_mem_kv, page_table, mem_prompt, g_mix_pre, w_in, ssm_lam_re, ssm_lam_im, ssm_log_dt, ssm_b_re, ssm_b_im, ssm_c_re, ssm_c_im, ssm_d, w_glu, b_glu, cmp_pe_k, w_cmpk1, w_cmpk2, cmp_pe_v, w_cmpv1, w_cmpv2, w_br_ssm, w_br_nsa, w_out, g_mix_post, g_x_pre, g_mem, w_xq, w_xk, w_xv, w_xo, g_x_post, g_mlp_pre, w_up, w_down, g_mlp_post):
    depth = w_in.shape[0]
    n_seq_p, seq, _ = x_prompt.shape
    n_seq_s, nq, _ = x_sample.shape
    past = page_table.shape[1] * PAGE_SIZE
    assert depth == 1 and seq % BLK == 0 and nq <= SUBLANES and past % BLK == 0
    assert n_seq_p == SUBLANES and n_seq_s % SUBLANES == 0

    y_p = x_prompt.reshape(n_seq_p * seq, D_MODEL)
    y_s = x_sample.reshape(n_seq_s * nq, D_MODEL)
    l = 0

    w_proj = w_in[l, :, :N_PROJ].astype(BF16)
    w_gate = jnp.pad(w_in[l, :, N_PROJ:N_PROJ + N_GATE], ((0, 0), (0, LANES - N_GATE))).astype(BF16)
    w_merge = w_in[l, :, N_PROJ + N_GATE:].astype(BF16)
    lam_l, bm, cm = _ssm_params(ssm_lam_re[l], ssm_lam_im[l], ssm_log_dt[l], ssm_b_re[l], ssm_b_im[l],
                                ssm_c_re[l], ssm_c_im[l])
    ssm_w = (lam_l, bm, cm, _row(ssm_d[l]), w_glu[l].astype(BF16), _row(b_glu[l]))
    cmp_w = _compress_params(cmp_pe_k[l], w_cmpk1[l], w_cmpk2[l], cmp_pe_v[l], w_cmpv1[l], w_cmpv2[l])
    merge_w = (_row(g_mix_pre[l]), w_merge, w_br_ssm[l].astype(BF16), w_br_nsa[l].astype(BF16),
               w_out[l].astype(BF16), _row(g_mix_post[l]), _row(g_x_pre[l]), w_xq[l].astype(BF16))
    mlp_w = (w_xo[l].astype(BF16), _row(g_x_post[l]), _row(g_mlp_pre[l]), w_up[l].astype(BF16),
             w_down[l].astype(BF16), _row(g_mlp_post[l]))
    w_mem = jnp.concatenate([w_xk[l], w_xv[l]], axis=1).astype(BF16)

    tm_p = 512 if seq % 512 == 0 else seq
    nt_p = seq // tm_p
    tabs_p = _rope_tables(jnp.arange(seq, dtype=jnp.int32))
    u_p, q_p, kv_p, win_p, kvb_p, gate_p = _proj(y_p, _row(g_mix_pre[l]), w_proj, w_gate, tabs_p,
                                                 tm=tm_p, n_tab_blocks=nt_p, u_time_major=True)
    u_p = u_p.reshape(seq * n_seq_p, SSM_WIDTH)
    h0_p = jnp.zeros((1, SUBLANES, 2 * N_STATE), F32)
    tc_p = 64 if seq % 64 == 0 else seq
    ssm_p, hl_p = _ssm(u_p, h0_p, *ssm_w, n_groups=1, n_time=seq, tc=tc_p)
    ssm_p = ssm_p.reshape(seq, n_seq_p * SSM_WIDTH)

    nb_p = seq // BLK
    n_blocks_p = n_seq_p * nb_p
    cmp_p = _compress(kv_p, *cmp_w, n_blocks=n_blocks_p, tm=min(n_blocks_p, 256))
    nsa_p = _nsa_prompt(q_p, gate_p, cmp_p, kvb_p, _expand_matrix(LANES, seq), n_seq=n_seq_p, seq=seq)

    mem_kv_p = _memkv(mem_prompt.reshape(-1, D_MODEL), _row(g_mem[l]), w_mem, tm=256)
    m_len = mem_prompt.shape[1]
    mem_kv_p3 = mem_kv_p.reshape(n_seq_p, m_len, 2 * X_WIDTH)

    x1_p, qx_p = _merge(y_p, ssm_p, nsa_p, merge_w, tm=tm_p, ssm_time_major=True, n_tab_blocks=nt_p)
    o_p = _xattn(qx_p.reshape(n_seq_p, seq, X_WIDTH), mem_kv_p3, tq=tm_p)
    y_p = _mlp(x1_p, o_p.reshape(-1, X_WIDTH), mlp_w, tm=256)

    n_s = n_seq_s * nq
    pos_s = past + jnp.arange(nq, dtype=jnp.int32)
    tabs_s = tuple(jnp.tile(t, (n_seq_s, 1)) for t in _rope_tables(pos_s))
    u_s, q_s, kv_s, win_s, kvb_s, gate_s = _proj(y_s, _row(g_mix_pre[l]), w_proj, w_gate, tabs_s,
                                                 tm=n_s, n_tab_blocks=1, u_time_major=False)
    n_grp = n_seq_s // SUBLANES
    to_tb = lambda a: a.reshape(n_grp, SUBLANES, nq, SSM_WIDTH).transpose(0, 2, 1, 3).reshape(n_s, SSM_WIDTH)
    st = state_ssm[l].astype(F32).reshape(n_seq_s, N_STATE, 2)
    h0_s = _state_lanes(st[..., 0], st[..., 1]).reshape(n_grp, SUBLANES, 2 * N_STATE)
    ssm_s, hl_s = _ssm(to_tb(u_s), h0_s, *ssm_w, n_groups=n_grp, n_time=nq, tc=nq)
    ssm_s = ssm_s.reshape(n_grp, nq, SUBLANES, SSM_WIDTH).transpose(0, 2, 1, 3).reshape(n_s, SSM_WIDTH)

    n_pages = page_table.shape[1]
    n_pool = cache_nsa_kv.shape[1]
    cache_t = jnp.transpose(cache_nsa_kv[l], (0, 2, 3, 4, 1)).reshape(n_pool, 4, KV_WIDTH, PAGE_SIZE)
    win_t = jnp.transpose(cache_win_kv[l], (0, 2, 3, 4, 1)).reshape(n_seq_s, 2, KV_WIDTH, -1)
    pt_flat = page_table.reshape(-1).astype(jnp.int32)
    nb_past = past // BLK
    pe_t, w1_t = _compress_paged_params(cmp_pe_k[l], w_cmpk1[l], cmp_pe_v[l], w_cmpv1[l])
    cmp_pages = _compress_paged(pt_flat, cache_t, pe_t, w1_t, cmp_w[2], m=min(n_seq_s * n_pages, 256))
    cmp_past = cmp_pages.reshape(2, NSA_KV, n_seq_s, n_pages, PAGE_SIZE // BLK, HEAD_DIM)
    cmp_past = cmp_past.transpose(0, 2, 3, 4, 1, 5)
    new_rows = jnp.pad(jnp.concatenate([kv_s, win_s], axis=1).reshape(n_seq_s, nq, 6 * KV_WIDTH),
                       ((0, 0), (0, LANES - nq), (0, 0)))
    cmp_new = _compress(new_rows[:, :BLK].reshape(n_seq_s * BLK, 6 * KV_WIDTH), *cmp_w,
                        n_blocks=n_seq_s, tm=n_seq_s)
    nbp = -(-(nb_past + 1) // LANES) * LANES
    cmp_s = jnp.concatenate([cmp_past.reshape(2, n_seq_s, nb_past, KV_WIDTH), cmp_new[:, :, None, :],
                             jnp.zeros((2, n_seq_s, nbp - nb_past - 1, KV_WIDTH), F32)], axis=2)
    nsa_s = _nsa_sample(pt_flat, q_s.astype(F32).reshape(n_seq_s, nq, NSA_WIDTH),
                        gate_s.reshape(n_seq_s, nq, LANES), cmp_s[0], cmp_s[1], cache_t, new_rows, win_t,
                        _expand_matrix(nbp, past + LANES), past=past, tk=min(past, 1024))

    x1_s, qx_s = _merge(y_s, ssm_s, nsa_s.reshape(n_s, NSA_WIDTH).astype(BF16), merge_w, tm=n_s,
                        ssm_time_major=False, n_tab_blocks=1)
    mem_kv_s3 = cache_mem_kv[l].reshape(n_seq_s, m_len, 2 * X_WIDTH)
    o_s = _xattn(qx_s.astype(F32).reshape(n_seq_s, nq, X_WIDTH), mem_kv_s3, tq=nq)
    y_s = _mlp(x1_s, o_s.reshape(-1, X_WIDTH).astype(BF16), mlp_w, tm=n_s)

    def ssm_state(hl, n_seq):
        re, im = _state_unlanes(hl.reshape(n_seq, 2 * N_STATE))
        return jnp.stack([re, im], axis=-1).reshape(1, n_seq, SSM_GROUPS, SSM_STATE, 2)

    w_keep = min(WINDOW, seq)
    win_prompt = win_p.reshape(n_seq_p, seq, 2, NSA_KV, HEAD_DIM)[:, seq - w_keep:]
    win_new = win_s.reshape(n_seq_s, nq, 2, NSA_KV, HEAD_DIM).astype(cache_win_kv.dtype)
    win_sample = jnp.concatenate([cache_win_kv[l], win_new], axis=1)[:, nq:]
    return (y_p.reshape(n_seq_p, seq, D_MODEL),
            y_s.reshape(n_seq_s, nq, D_MODEL),
            kv_p.reshape(1, n_seq_p, seq, 4, NSA_KV, HEAD_DIM),
            kv_s.reshape(1, n_seq_s, nq, 4, NSA_KV, HEAD_DIM),
            win_prompt[None],
            win_sample[None],
            ssm_state(hl_p, n_seq_p),
            ssm_state(hl_s, n_seq_s),
            mem_kv_p.reshape(1, n_seq_p, m_len, 2, X_HEADS, X_HEAD_DIM))
```

```python
import functools
import math

import jax
import jax.numpy as jnp
import numpy as np
from jax import lax
from jax.experimental import pallas as pl
from jax.experimental.pallas import tpu as pltpu

F32 = jnp.float32
BF16 = jnp.bfloat16

D_MODEL = 1024
SSM_WIDTH = 512
SSM_GROUP = 16
SSM_GROUPS = 32
SSM_STATE = 64
N_STATE = SSM_GROUPS * SSM_STATE
STATE_CHUNK = 512
N_CHUNKS = N_STATE // STATE_CHUNK
NSA_HEADS = 8
HEAD_DIM = 64
NSA_WIDTH = NSA_HEADS * HEAD_DIM
NSA_KV = 2
NSA_REP = NSA_HEADS // NSA_KV
KV_WIDTH = NSA_KV * HEAD_DIM
BLK = 64
N_SEL = 16
WINDOW = 512
ROT_DIM = 16
ROPE_THETA = 500000.0
PAGE_SIZE = 128
X_HEADS = 4
X_HEAD_DIM = 128
X_WIDTH = X_HEADS * X_HEAD_DIM
D_FF = 4 * D_MODEL
EPS = 1e-6
NEG_INF = -1e30
M_INIT = -1e29
FORCE_SCORE = 1e4
LANES = 128
SUBLANES = 8
VMEM_LIMIT = 56 * 1024 * 1024

N_PROJ = SSM_WIDTH + NSA_WIDTH + 6 * KV_WIDTH
N_GATE = 3 * NSA_HEADS
SEL_CHUNK = 512
PAGE_PITCH = PAGE_SIZE + SUBLANES


def _params(*sem):
    return pltpu.CompilerParams(dimension_semantics=sem, vmem_limit_bytes=VMEM_LIMIT)


def _rms(x, g):
    return x * lax.rsqrt(jnp.mean(x * x, axis=-1, keepdims=True) + EPS) * g


def _gelu(x):
    return 0.5 * x * (1.0 + jnp.tanh(math.sqrt(2.0 / math.pi) * (x + 0.044715 * (x * x * x))))


def _sigmoid(x):
    return 1.0 / (1.0 + jnp.exp(-x))


def _dot(a, b):
    return jnp.dot(a, b, preferred_element_type=F32)


def _dot_nt(a, b):
    return lax.dot_general(a, b, (((1,), (1,)), ((), ())), preferred_element_type=F32)


def _proj_kernel(x_ref, g_ref, w_ref, wg_ref, cos_ref, sp_ref, sm_ref,
                 u_ref, q_ref, kv_ref, win_ref, kvb_ref, gate_ref, *cmp_ref, token_minor):
    a = _rms(x_ref[...], g_ref[...]).astype(BF16)
    cos, sp, sm = cos_ref[...], sp_ref[...], sm_ref[...]

    def rope(blk):
        return blk * cos + pltpu.roll(blk, 8, 1) * sp + pltpu.roll(blk, LANES - 8, 1) * sm

    u_ref[...] = _dot(a, w_ref[:, 0:SSM_WIDTH])
    for j in range(NSA_WIDTH // LANES):
        c0 = SSM_WIDTH + j * LANES
        q_ref[:, j * LANES:(j + 1) * LANES] = rope(_dot(a, w_ref[:, c0:c0 + LANES])).astype(BF16)
    for j in range(6):
        c0 = SSM_WIDTH + NSA_WIDTH + j * LANES
        blk = _dot(a, w_ref[:, c0:c0 + LANES])
        if j % 2 == 0:
            blk = rope(blk)
        out_ref, part = (kv_ref, j) if j < 4 else (win_ref, j - 4)
        if token_minor:
            out_ref[0, part] = blk.T
            if j < 2:
                cmp_ref[0][:, j * LANES:(j + 1) * LANES] = blk
        else:
            out_ref[:, part * LANES:(part + 1) * LANES] = blk
        kvb_ref[:, j * LANES:(j + 1) * LANES] = blk.astype(BF16)
    gate_ref[...] = _sigmoid(_dot(a, wg_ref[...]))


def _proj(x2d, g, w, wg, tabs, *, tm, n_tab_blocks, prompt):
    n = x2d.shape[0]
    steps = n // tm
    row = lambda w_: pl.BlockSpec((tm, w_), lambda i: (i, 0))
    full = lambda a: pl.BlockSpec(a.shape, lambda i: (0,) * a.ndim)
    tab = pl.BlockSpec((tm, LANES), lambda i: (i % n_tab_blocks, 0))
    nt = n_tab_blocks
    if prompt:
        n_b = steps // nt
        t_len = nt * tm
        u_shape = jax.ShapeDtypeStruct((t_len, n_b * SSM_WIDTH), F32)
        u_spec = pl.BlockSpec((tm, SSM_WIDTH), lambda i: (i % nt, i // nt))
        tok_minor = lambda parts: pl.BlockSpec((1, parts, KV_WIDTH, tm), lambda i: (i // nt, 0, 0, i % nt))
        kv_specs = [tok_minor(4), tok_minor(2)]
        kv_shapes = [jax.ShapeDtypeStruct((n_b, 4, KV_WIDTH, t_len), F32),
                     jax.ShapeDtypeStruct((n_b, 2, KV_WIDTH, t_len), F32)]
        extra_specs = [row(2 * KV_WIDTH)]
        extra_shapes = [jax.ShapeDtypeStruct((n, 2 * KV_WIDTH), F32)]
    else:
        u_shape = jax.ShapeDtypeStruct((n, SSM_WIDTH), F32)
        u_spec = pl.BlockSpec((tm, SSM_WIDTH), lambda i: (i, 0))
        kv_specs = [row(4 * KV_WIDTH), row(2 * KV_WIDTH)]
        kv_shapes = [jax.ShapeDtypeStruct((n, 4 * KV_WIDTH), F32), jax.ShapeDtypeStruct((n, 2 * KV_WIDTH), F32)]
        extra_specs, extra_shapes = [], []
    return pl.pallas_call(
        functools.partial(_proj_kernel, token_minor=prompt),
        grid=(steps,),
        in_specs=[row(D_MODEL), full(g), full(w), full(wg), tab, tab, tab],
        out_specs=[u_spec, row(NSA_WIDTH)] + kv_specs + [row(6 * KV_WIDTH), row(LANES)] + extra_specs,
        out_shape=[u_shape, jax.ShapeDtypeStruct((n, NSA_WIDTH), BF16)] + kv_shapes
                  + [jax.ShapeDtypeStruct((n, 6 * KV_WIDTH), BF16), jax.ShapeDtypeStruct((n, LANES), F32)]
                  + extra_shapes,
        compiler_params=_params("parallel"),
        name="proj",
    )(x2d, g, w, wg, *tabs)


def _rope_tables(pos):
    half = ROT_DIM // 2
    freqs = ROPE_THETA ** (-jnp.arange(half, dtype=F32) / half)
    ang = pos.astype(F32)[:, None] * freqs[None, :]
    cos, sin = jnp.cos(ang), jnp.sin(ang)
    r = pos.shape[0]
    z8 = jnp.zeros((r, half), F32)
    rest0 = jnp.zeros((r, HEAD_DIM - ROT_DIM), F32)
    rest1 = jnp.ones((r, HEAD_DIM - ROT_DIM), F32)
    c64 = jnp.concatenate([cos, cos, rest1], axis=1)
    sp64 = jnp.concatenate([z8, sin, rest0], axis=1)
    sm64 = jnp.concatenate([-sin, z8, rest0], axis=1)
    return tuple(jnp.tile(t, (1, LANES // HEAD_DIM)) for t in (c64, sp64, sm64))


def _ssm_kernel(u_ref, h0_ref, lam_ref, bm_ref, cm_ref, d_ref, wglu_ref, bglu_ref,
                y_ref, hlast_ref, hs_ref, hstate_ref, *, tc):
    j = pl.program_id(1)

    @pl.when(j == 0)
    def _():
        hstate_ref[...] = h0_ref[0]

    u = u_ref[...]
    ub = u.astype(BF16)
    half_in = SSM_WIDTH // 2
    half_st = N_STATE
    for h in range(2):
        hs_ref[:, h * half_st:(h + 1) * half_st] = _dot(ub[:, h * half_in:(h + 1) * half_in], bm_ref[h])

    for c in range(N_CHUNKS):
        re0 = c * 2 * STATE_CHUNK
        im0 = re0 + STATE_CHUNK
        lr = jnp.broadcast_to(lam_ref[0:1, re0:re0 + STATE_CHUNK], (SUBLANES, STATE_CHUNK))
        li = jnp.broadcast_to(lam_ref[0:1, im0:im0 + STATE_CHUNK], (SUBLANES, STATE_CHUNK))

        def step(t, carry, re0=re0, im0=im0, lr=lr, li=li):
            hr, hi = carry
            r0 = pl.multiple_of(t * SUBLANES, SUBLANES)
            nr = lr * hr - li * hi + hs_ref[pl.ds(r0, SUBLANES), re0:re0 + STATE_CHUNK]
            ni = lr * hi + li * hr + hs_ref[pl.ds(r0, SUBLANES), im0:im0 + STATE_CHUNK]
            hs_ref[pl.ds(r0, SUBLANES), re0:re0 + STATE_CHUNK] = nr
            hs_ref[pl.ds(r0, SUBLANES), im0:im0 + STATE_CHUNK] = ni
            return nr, ni

        hr, hi = lax.fori_loop(0, tc, step,
                               (hstate_ref[:, re0:re0 + STATE_CHUNK], hstate_ref[:, im0:im0 + STATE_CHUNK]),
                               unroll=min(tc, 4))
        hstate_ref[:, re0:re0 + STATE_CHUNK] = hr
        hstate_ref[:, im0:im0 + STATE_CHUNK] = hi

    ys = [_dot(hs_ref[:, h * half_st:(h + 1) * half_st].astype(BF16), cm_ref[h]) for h in range(2)]
    y = jnp.concatenate(ys, axis=1) + d_ref[...] * u
    y = _gelu(y)
    z = _dot(y.astype(BF16), wglu_ref[...]) + bglu_ref[...]
    y_ref[...] = y * _sigmoid(z)

    @pl.when(j == pl.num_programs(1) - 1)
    def _():
        hlast_ref[0] = hstate_ref[...]


def _ssm(u_tb, h0, lam, bm, cm, d, wglu, bglu, *, n_groups, n_time, tc):
    rows = tc * SUBLANES
    nt = n_time // tc
    full = lambda a: pl.BlockSpec(a.shape, lambda g, j: (0,) * a.ndim)
    st = pl.BlockSpec((1, SUBLANES, 2 * N_STATE), lambda g, j: (g, 0, 0))
    return pl.pallas_call(
        functools.partial(_ssm_kernel, tc=tc),
        grid=(n_groups, nt),
        in_specs=[pl.BlockSpec((rows, SSM_WIDTH), lambda g, j: (g * nt + j, 0)), st,
                  full(lam), full(bm), full(cm), full(d), full(wglu), full(bglu)],
        out_specs=[pl.BlockSpec((rows, SSM_WIDTH), lambda g, j: (g * nt + j, 0)), st],
        out_shape=[jax.ShapeDtypeStruct((n_groups * n_time * SUBLANES, SSM_WIDTH), F32),
                   jax.ShapeDtypeStruct((n_groups, SUBLANES, 2 * N_STATE), F32)],
        scratch_shapes=[pltpu.VMEM((rows, 2 * N_STATE), F32), pltpu.VMEM((SUBLANES, 2 * N_STATE), F32)],
        compiler_params=_params("parallel", "arbitrary"),
        name="ssm",
    )(u_tb, h0, lam, bm, cm, d, wglu, bglu)


def _state_lanes(re, im):
    lead = re.shape[:-1]
    r = re.reshape(lead + (N_CHUNKS, 1, STATE_CHUNK))
    i = im.reshape(lead + (N_CHUNKS, 1, STATE_CHUNK))
    return jnp.concatenate([r, i], axis=-2).reshape(lead + (2 * N_STATE,))


def _state_unlanes(x):
    lead = x.shape[:-1]
    y = x.reshape(lead + (N_CHUNKS, 2, STATE_CHUNK))
    return y[..., 0, :].reshape(lead + (N_STATE,)), y[..., 1, :].reshape(lead + (N_STATE,))


def _ssm_params(lam_re, lam_im, log_dt, b_re, b_im, c_re, c_im):
    lam = lax.complex(lam_re.astype(F32), lam_im.astype(F32))
    dt = jnp.exp(log_dt.astype(F32))[:, None]
    lam_bar = jnp.exp(lam * dt)
    b = lax.complex(b_re.astype(F32), b_im.astype(F32))
    b_bar = ((lam_bar - 1.0) / lam)[..., None] * b
    eye = jnp.eye(SSM_GROUPS, dtype=F32)

    def in_blockdiag(x):
        return jnp.einsum('gpc,gh->gchp', x, eye).reshape(SSM_WIDTH, N_STATE)

    def out_blockdiag(x):
        return jnp.einsum('gcp,gh->gphc', x, eye).reshape(N_STATE, SSM_WIDTH)

    b_full = _state_lanes(in_blockdiag(jnp.real(b_bar)), in_blockdiag(jnp.imag(b_bar)))
    c_full = _state_lanes(out_blockdiag(c_re.astype(F32)).T, -out_blockdiag(c_im.astype(F32)).T).T
    hw, hs = SSM_WIDTH // 2, N_STATE
    bm = jnp.stack([b_full[h * hw:(h + 1) * hw, h * hs:(h + 1) * hs] for h in range(2)]).astype(BF16)
    cm = jnp.stack([c_full[h * hs:(h + 1) * hs, h * hw:(h + 1) * hw] for h in range(2)]).astype(BF16)
    lam_l = _state_lanes(jnp.real(lam_bar).reshape(1, N_STATE), jnp.imag(lam_bar).reshape(1, N_STATE))
    return lam_l, bm, cm


def _compress_kernel(x_ref, pe_ref, w1_ref, w2_ref, o_ref, *, tm):
    def body(sp, acc):
        s0 = 2 * sp
        xa = x_ref[pl.ds(s0, tm, stride=BLK), :] + pe_ref[0, pl.ds(s0, 1), :]
        xb = x_ref[pl.ds(s0 + 1, tm, stride=BLK), :] + pe_ref[0, pl.ds(s0 + 1, 1), :]
        lhs = jnp.concatenate([xa, xb], axis=1).astype(BF16)
        return acc + _dot(lhs, w1_ref[0, sp])

    acc = lax.fori_loop(0, BLK // 2, body, jnp.zeros((tm, KV_WIDTH), F32))
    o_ref[0] = _dot(_gelu(acc).astype(BF16), w2_ref[0])


def _compress(x2d, pe, w1, w2, *, n_blocks, tm):
    return pl.pallas_call(
        functools.partial(_compress_kernel, tm=tm),
        grid=(2, n_blocks // tm),
        in_specs=[pl.BlockSpec((tm * BLK, KV_WIDTH), lambda c, i: (i, c)),
                  pl.BlockSpec((1, BLK, KV_WIDTH), lambda c, i: (c, 0, 0)),
                  pl.BlockSpec((1, BLK // 2, 2 * KV_WIDTH, KV_WIDTH), lambda c, i: (c, 0, 0, 0)),
                  pl.BlockSpec((1, KV_WIDTH, KV_WIDTH), lambda c, i: (c, 0, 0))],
        out_specs=pl.BlockSpec((1, tm, KV_WIDTH), lambda c, i: (c, i, 0)),
        out_shape=jax.ShapeDtypeStruct((2, n_blocks, KV_WIDTH), F32),
        compiler_params=_params("parallel", "parallel"),
        name="compress",
    )(x2d, pe, w1, w2)


def _compress_params(pe_k, w1_k, w2_k, pe_v, w1_v, w2_v):
    def bd(w):
        z = jnp.zeros_like(w)
        return jnp.concatenate([jnp.concatenate([w, z], axis=-1), jnp.concatenate([z, w], axis=-1)], axis=-2)

    def one(pe, w1, w2):
        w1s = bd(w1.astype(F32).reshape(BLK, HEAD_DIM, HEAD_DIM))
        return (jnp.tile(pe.astype(F32), (1, NSA_KV)),
                w1s.reshape(BLK // 2, 2 * KV_WIDTH, KV_WIDTH).astype(BF16),
                bd(w2.astype(F32)).astype(BF16))

    k, v = one(pe_k, w1_k, w2_k), one(pe_v, w1_v, w2_v)
    return tuple(jnp.stack([a, b]) for a, b in zip(k, v))


def _compress_paged_kernel(pt_ref, cache_ref, pe_ref, w1_ref, w2_ref, o_ref, buf_ref, sems, *, m):
    step = pl.program_id(0)

    def page_copy(s, j):
        slot = s % 2
        row0 = pl.multiple_of(j * PAGE_PITCH, SUBLANES)
        return pltpu.make_async_copy(cache_ref.at[pt_ref[s * m + j], pl.ds(0, 2)],
                                     buf_ref.at[pl.ds(2 * slot, 2), pl.ds(row0, PAGE_SIZE), :],
                                     sems.at[slot])

    def start_all(s):
        def start(j, _):
            page_copy(s, j).start()
            return 0
        lax.fori_loop(0, m, start, 0, unroll=8)

    @pl.when(step == 0)
    def _():
        start_all(step)

    @pl.when(step + 1 < pl.num_programs(0))
    def _():
        start_all(step + 1)

    def wait(j, _):
        page_copy(step, j).wait()
        return 0

    lax.fori_loop(0, m, wait, 0, unroll=8)

    for c in range(2):
        tile = 2 * (step % 2) + c
        res = []
        for kv in range(NSA_KV):
            def body(dp, acc, c=c, kv=kv, tile=tile):
                d0 = 2 * dp
                r0 = kv * HEAD_DIM + d0
                xa = buf_ref[tile, pl.ds(r0, m, stride=PAGE_PITCH), :] + pe_ref[c, pl.ds(d0, 1), :]
                xb = buf_ref[tile, pl.ds(r0 + 1, m, stride=PAGE_PITCH), :] + pe_ref[c, pl.ds(d0 + 1, 1), :]
                lhs = jnp.concatenate([xa, xb], axis=1).astype(BF16)
                return acc + _dot(lhs, w1_ref[c, dp])

            acc = lax.fori_loop(0, HEAD_DIM // 2, body, jnp.zeros((m, PAGE_SIZE), F32), unroll=4)
            res.append(_dot(_gelu(acc).astype(BF16), w2_ref[c]))
        for blk in range(PAGE_SIZE // BLK):
            cols = slice(blk * HEAD_DIM, (blk + 1) * HEAD_DIM)
            o_ref[c, pl.ds(blk, m, stride=PAGE_SIZE // BLK), :] = jnp.concatenate(
                [r[:, cols] for r in res], axis=1)


def _compress_paged(pt_flat, cache_t, pe_t, w1_t, w2, *, m):
    n = pt_flat.shape[0]
    full = lambda a: pl.BlockSpec(a.shape, lambda i, pt: (0,) * a.ndim)
    grid_spec = pltpu.PrefetchScalarGridSpec(
        num_scalar_prefetch=1,
        grid=(n // m,),
        in_specs=[pl.BlockSpec(memory_space=pl.ANY), full(pe_t), full(w1_t), full(w2)],
        out_specs=pl.BlockSpec((2, m * (PAGE_SIZE // BLK), KV_WIDTH), lambda i, pt: (0, i, 0)),
        scratch_shapes=[pltpu.VMEM((4, m * PAGE_PITCH, PAGE_SIZE), F32), pltpu.SemaphoreType.DMA((2,))])
    return pl.pallas_call(
        functools.partial(_compress_paged_kernel, m=m),
        grid_spec=grid_spec,
        out_shape=jax.ShapeDtypeStruct((2, n * (PAGE_SIZE // BLK), KV_WIDTH), F32),
        compiler_params=_params("arbitrary"),
        name="compress_paged",
    )(pt_flat, cache_t, pe_t, w1_t, w2)


def _compress_paged_params(pe_k, w1_k, pe_v, w1_v):
    def bd(w):
        z = jnp.zeros_like(w)
        return jnp.concatenate([jnp.concatenate([w, z], axis=-1), jnp.concatenate([z, w], axis=-1)], axis=-2)

    def one(pe, w1):
        w1d = bd(w1.astype(F32).reshape(BLK, HEAD_DIM, HEAD_DIM).transpose(1, 0, 2))
        return (jnp.tile(pe.astype(F32).T, (1, PAGE_SIZE // BLK)),
                w1d.reshape(HEAD_DIM // 2, 2 * PAGE_SIZE, PAGE_SIZE).astype(BF16))

    k, v = one(pe_k, w1_k), one(pe_v, w1_v)
    return tuple(jnp.stack([a, b]) for a, b in zip(k, v))


def _stack_queries(q, nq):
    q = q.astype(F32)
    z = jnp.zeros((nq, HEAD_DIM), F32)
    rows = []
    for h in range(NSA_HEADS):
        blk = q[:, h * HEAD_DIM:(h + 1) * HEAD_DIM]
        rows.append(jnp.concatenate([blk, z] if h < NSA_REP else [z, blk], axis=1))
    return (jnp.concatenate(rows, axis=0) * (HEAD_DIM ** -0.5)).astype(BF16)


def _masked_softmax(s, valid):
    s = jnp.where(valid, s, NEG_INF)
    m = jnp.max(s, axis=-1, keepdims=True)
    p = jnp.exp(s - m) * valid.astype(F32)
    return p / jnp.maximum(jnp.sum(p, axis=-1, keepdims=True), 1e-30)


def _select_blocks(imp, n_ids, cur, nb):
    forced = (n_ids == 0) | (n_ids == cur) | (n_ids == cur - 1)
    imp = jnp.where(forced, FORCE_SCORE, imp)
    imp = jnp.where(n_ids <= cur, imp, -FORCE_SCORE)
    rank = jnp.zeros(imp.shape, F32)
    for m in range(nb):
        col = imp[:, m:m + 1]
        beats = (col > imp) | ((col == imp) & (n_ids > m))
        rank = rank + beats.astype(F32)
    return (rank < float(N_SEL)).astype(F32)


def _select_blocks_t(imp_t, cur, nb):
    n_t = lax.broadcasted_iota(jnp.int32, imp_t.shape, 0)
    forced = (n_t == 0) | (n_t == cur) | (n_t == cur - 1)
    imp_t = jnp.where(forced, FORCE_SCORE, imp_t)
    imp_t = jnp.where(n_t <= cur, imp_t, -FORCE_SCORE)
    rank = jnp.zeros(imp_t.shape, F32)
    for m in range(nb):
        row = imp_t[m:m + 1, :]
        beats = (row > imp_t) | ((row == imp_t) & (n_t > m))
        rank = rank + beats.astype(F32)
    return (rank < float(N_SEL)).astype(F32)


def _combine_heads(gates, o_c, o_s, o_w, nq):
    outs = []
    for h in range(NSA_HEADS):
        rows = slice(h * nq, (h + 1) * nq)
        o = (gates[:, 3 * h:3 * h + 1] * o_c[rows] + gates[:, 3 * h + 1:3 * h + 2] * o_s[rows]
             + gates[:, 3 * h + 2:3 * h + 3] * o_w[rows])
        g = h // NSA_REP
        outs.append(o[:, g * HEAD_DIM:(g + 1) * HEAD_DIM])
    return jnp.concatenate(outs, axis=1)


def _nsa_prompt_kernel(q_ref, gate_ref, kc_ref, vc_ref, ks_ref, vs_ref, kw_ref, vw_ref, e_ref,
                       o_ref, bias_ref, s_ref, mx_ref, acc_ref, ow_ref, *, nb, ch, wk):
    ci = pl.program_id(1)
    nq = BLK
    rows = NSA_HEADS * nq
    q2 = _stack_queries(q_ref[...], nq)
    q_pos = ci * BLK + lax.broadcasted_iota(jnp.int32, (rows, 1), 0) % nq
    d0 = (ci * BLK + lax.broadcasted_iota(jnp.int32, (rows, LANES), 0) % nq
          - lax.broadcasted_iota(jnp.int32, (rows, LANES), 1))

    def col_blocks(s):
        return [s[:, j * LANES:(j + 1) * LANES] for j in range(s.shape[1] // LANES)]

    def col_max(cols):
        m = cols[0]
        for c_ in cols[1:]:
            m = jnp.maximum(m, c_)
        return m

    def finish(acc):
        return acc[:, 0:KV_WIDTH] / jnp.maximum(acc[:, KV_WIDTH:2 * KV_WIDTH], 1e-30)

    def with_ones(v):
        return jnp.concatenate([v, jnp.ones((v.shape[0], LANES), BF16)], axis=1)

    first = jnp.maximum(ci - WINDOW // BLK, 0) // (LANES // BLK)
    w0 = pl.multiple_of(first * LANES, LANES)
    cols = col_blocks(_dot_nt(q2, kw_ref[pl.ds(w0, wk), :]))
    last_col = len(cols) - 1
    masked = []
    for j, c_ in enumerate(cols):
        d = d0 - (w0 + j * LANES)
        if j == 0 or j == last_col:
            masked.append(jnp.where((d >= 0) & (d < WINDOW), c_, NEG_INF))
        else:
            masked.append(jnp.where(d >= 0, c_, NEG_INF))
    m_w = jnp.maximum(jnp.max(col_max(masked), axis=-1, keepdims=True), M_INIT)
    m_w = jnp.broadcast_to(m_w, (rows, LANES))
    p = jnp.concatenate([jnp.exp(c_ - m_w).astype(BF16) for c_ in masked], axis=1)
    ow_ref[...] = finish(_dot(p, with_ones(vw_ref[pl.ds(w0, wk), :])))

    n_ids = lax.broadcasted_iota(jnp.int32, (rows, nb), 1)
    s_c = _dot_nt(q2, kc_ref[0].astype(BF16))
    p_c = _masked_softmax(s_c, (n_ids + 1) * BLK - 1 <= q_pos)
    o_c = _dot(p_c.astype(BF16), vc_ref[0].astype(BF16))

    imps = []
    for g in range(NSA_KV):
        imp = p_c[g * NSA_REP * nq:(g * NSA_REP + 1) * nq]
        for r in range(1, NSA_REP):
            imp = imp + p_c[(g * NSA_REP + r) * nq:(g * NSA_REP + r + 1) * nq]
        imps.append(imp)
    imp2 = jnp.concatenate([jnp.concatenate(imps, axis=0), jnp.zeros((LANES, LANES - nb), F32)], axis=1)
    sel_t = _select_blocks_t(imp2.T[0:nb], ci, nb)
    sel2 = jnp.concatenate([sel_t, jnp.zeros((LANES - nb, LANES), F32)], axis=0).T
    bias = (_dot(sel2.astype(BF16), e_ref[...]) - 1.0) * (-NEG_INF)
    for c in range(nb * BLK // ch):
        bias_ref[c] = bias[:, c * ch:(c + 1) * ch]

    n_ch = ci // (ch // BLK) + 1
    mx_ref[...] = jnp.full(mx_ref.shape, M_INIT, F32)

    def scores(c, causal):
        k0 = pl.multiple_of(c * ch, ch)
        s = _dot_nt(q2, ks_ref[pl.ds(k0, ch), :])
        s = s + jnp.concatenate([bias_ref[c, g * nq:(g + 1) * nq, :]
                                 for g in range(NSA_KV) for _ in range(NSA_REP)], axis=0)
        cols = col_blocks(s)
        if causal:
            cols = [jnp.where(d0 >= k0 + j * LANES, c_, NEG_INF) for j, c_ in enumerate(cols)]
        mx_ref[...] = jnp.maximum(mx_ref[...], col_max(cols))
        for j, c_ in enumerate(cols):
            s_ref[c, :, j * LANES:(j + 1) * LANES] = c_

    def pass1(c, _):
        scores(c, False)
        return 0

    lax.fori_loop(0, n_ch - 1, pass1, 0)
    scores(n_ch - 1, True)

    m_sel = jnp.broadcast_to(jnp.max(mx_ref[...], axis=-1, keepdims=True), (rows, LANES))
    acc_ref[...] = jnp.zeros(acc_ref.shape, F32)

    def pass2(c, _):
        k0 = pl.multiple_of(c * ch, ch)
        p = jnp.concatenate([jnp.exp(s_ref[c, :, j * LANES:(j + 1) * LANES] - m_sel).astype(BF16)
                             for j in range(ch // LANES)], axis=1)
        acc_ref[...] += _dot(p, with_ones(vs_ref[pl.ds(k0, ch), :]))
        return 0

    lax.fori_loop(0, n_ch, pass2, 0)
    o_s = finish(acc_ref[...])

    o_ref[...] = _combine_heads(gate_ref[...], o_c, o_s, ow_ref[...], nq).astype(o_ref.dtype)


def _nsa_prompt(q, gates, cmp_kv, kvb, expand, *, n_seq, seq):
    nb = seq // BLK
    rows = NSA_HEADS * BLK
    chunk = lambda w_: pl.BlockSpec((BLK, w_), lambda b, c: (b * nb + c, 0))
    kv = lambda col: pl.BlockSpec((seq, KV_WIDTH), lambda b, c: (b, col))
    cmp_ = lambda which: pl.BlockSpec((1, nb, KV_WIDTH), lambda b, c: (which, b, 0))
    ch = min(SEL_CHUNK, seq)
    wk = min(WINDOW + LANES, seq)
    assert seq % ch == 0 and wk % LANES == 0 and nb % 2 == 0 and nb <= LANES
    return pl.pallas_call(
        functools.partial(_nsa_prompt_kernel, nb=nb, ch=ch, wk=wk),
        grid=(n_seq, nb),
        in_specs=[chunk(NSA_WIDTH), chunk(LANES), cmp_(0), cmp_(1), kv(2), kv(3), kv(4), kv(5),
                  pl.BlockSpec(expand.shape, lambda b, c: (0, 0))],
        out_specs=chunk(NSA_WIDTH),
        out_shape=jax.ShapeDtypeStruct((n_seq * seq, NSA_WIDTH), BF16),
        scratch_shapes=[pltpu.VMEM((seq // ch, NSA_KV * BLK, ch), F32),
                        pltpu.VMEM((seq // ch, rows, ch), F32),
                        pltpu.VMEM((rows, LANES), F32),
                        pltpu.VMEM((rows, 2 * KV_WIDTH), F32),
                        pltpu.VMEM((rows, KV_WIDTH), F32)],
        compiler_params=_params("parallel", "arbitrary"),
        name="nsa_prompt",
    )(q, gates, cmp_kv, cmp_kv, kvb, kvb, kvb, kvb, expand)


def _nsa_sample_kernel(pt_ref, q_ref, gate_ref, kc_ref, vc_ref, cache_ref, nks_ref, nvs_ref,
                       wt_ref, nkw_ref, nvw_ref, e_ref, o_ref, kv_buf, s_scr, sems,
                       *, nq, past, nbp, tk):
    b = pl.program_id(0)
    n_pages = past // PAGE_SIZE
    rows = NSA_HEADS * nq

    def page_copy(s, j):
        slot = s % 2
        k0 = pl.multiple_of(j * PAGE_SIZE, PAGE_SIZE)
        return pltpu.make_async_copy(cache_ref.at[pt_ref[s * n_pages + j], pl.ds(2, 2)],
                                     kv_buf.at[pl.ds(2 * slot, 2), :, pl.ds(k0, PAGE_SIZE)], sems.at[slot])

    def start_all(s):
        def start(j, _):
            page_copy(s, j).start()
            return 0
        lax.fori_loop(0, n_pages, start, 0, unroll=8)

    @pl.when(b == 0)
    def _():
        start_all(b)

    @pl.when(b + 1 < pl.num_programs(0))
    def _():
        start_all(b + 1)

    slot = b % 2
    q2 = _stack_queries(q_ref[0], nq)
    q_pos = past + lax.broadcasted_iota(jnp.int32, (rows, 1), 0) % nq
    cur = past // BLK

    n_ids = lax.broadcasted_iota(jnp.int32, (rows, nbp), 1)
    s_c = _dot_nt(q2, kc_ref[0].astype(BF16))
    p_c = _masked_softmax(s_c, (n_ids + 1) * BLK - 1 <= q_pos)
    o_c = _dot(p_c.astype(BF16), vc_ref[0].astype(BF16))

    n_sel = lax.broadcasted_iota(jnp.int32, (nq, nbp), 1)
    sels = []
    for g in range(NSA_KV):
        imp = p_c[g * NSA_REP * nq:(g * NSA_REP + 1) * nq]
        for r in range(1, NSA_REP):
            imp = imp + p_c[(g * NSA_REP + r) * nq:(g * NSA_REP + r + 1) * nq]
        sel = _select_blocks(imp, n_sel, cur, cur + 1)
        sels.extend([sel] * NSA_REP)
    keys = _dot(jnp.concatenate(sels, axis=0).astype(BF16), e_ref[...])
    bias = (keys - 1.0) * (-NEG_INF)

    def col_max(s):
        m = s[:, 0:LANES]
        for j in range(1, s.shape[1] // LANES):
            m = jnp.maximum(m, s[:, j * LANES:(j + 1) * LANES])
        return m

    def col_sum(p):
        t = p[:, 0:LANES]
        for j in range(1, p.shape[1] // LANES):
            t = t + p[:, j * LANES:(j + 1) * LANES]
        return t

    def row_max(mx):
        return jnp.broadcast_to(jnp.max(mx, axis=-1, keepdims=True), mx.shape)

    def tiled(m, width):
        return m if width == LANES else jnp.concatenate([m] * (width // LANES), axis=1)

    new_pos = past + lax.broadcasted_iota(jnp.int32, (rows, LANES), 1)

    def wait(j, _):
        page_copy(b, j).wait()
        return 0

    lax.fori_loop(0, n_pages, wait, 0, unroll=8)

    mx = jnp.full((rows, LANES), M_INIT, F32)
    for t in range(past // tk):
        s = _dot(q2, kv_buf[2 * slot, :, t * tk:(t + 1) * tk].astype(BF16)) + bias[:, t * tk:(t + 1) * tk]
        s_scr[:, t * tk:(t + 1) * tk] = s
        mx = jnp.maximum(mx, col_max(s))
    s_new = _dot_nt(q2, nks_ref[0].astype(BF16)) + bias[:, past:past + LANES]
    s_new = jnp.where(new_pos <= q_pos, s_new, NEG_INF)
    m_s = row_max(jnp.maximum(mx, s_new))
    p_new = jnp.exp(s_new - m_s)
    acc = _dot(p_new.astype(BF16), nvs_ref[0].astype(BF16))
    lsum = p_new
    for t in range(past // tk):
        p = jnp.exp(s_scr[:, t * tk:(t + 1) * tk] - tiled(m_s, tk))
        lsum = lsum + col_sum(p)
        acc = acc + _dot_nt(p.astype(BF16), kv_buf[2 * slot + 1, :, t * tk:(t + 1) * tk].astype(BF16))
    o_s = acc / jnp.maximum(jnp.sum(lsum, axis=-1, keepdims=True), 1e-30)

    wlen = wt_ref.shape[3]
    w_pos = past - wlen + lax.broadcasted_iota(jnp.int32, (rows, wlen), 1)
    d = q_pos - w_pos
    s_w = _dot(q2, wt_ref[0, 0].astype(BF16))
    s_w = jnp.where((d >= 0) & (d < WINDOW) & (w_pos >= 0), s_w, NEG_INF)
    d = q_pos - new_pos
    s_nw = _dot_nt(q2, nkw_ref[0].astype(BF16))
    s_nw = jnp.where((d >= 0) & (d < WINDOW), s_nw, NEG_INF)
    m_w = row_max(jnp.maximum(jnp.maximum(col_max(s_w), s_nw), M_INIT))
    p_w = jnp.exp(s_w - tiled(m_w, wlen))
    p_nw = jnp.exp(s_nw - m_w)
    acc = _dot_nt(p_w.astype(BF16), wt_ref[0, 1].astype(BF16)) + _dot(p_nw.astype(BF16), nvw_ref[0].astype(BF16))
    o_w = acc / jnp.maximum(jnp.sum(col_sum(p_w) + p_nw, axis=-1, keepdims=True), 1e-30)

    o_ref[0] = _combine_heads(gate_ref[0], o_c, o_s, o_w, nq).astype(o_ref.dtype)


def _nsa_sample(pt_flat, q3, gates3, kc, vc, cache_t, new_rows, win_t, expand, *, past, tk):
    n_seq, nq, _ = q3.shape
    nbp = kc.shape[1]
    rows = NSA_HEADS * nq
    per_seq = lambda a: pl.BlockSpec((1,) + a.shape[1:], lambda b, pt: (b,) + (0,) * (a.ndim - 1))
    new = lambda col: pl.BlockSpec((1, LANES, KV_WIDTH), lambda b, pt: (b, 0, col))
    grid_spec = pltpu.PrefetchScalarGridSpec(
        num_scalar_prefetch=1,
        grid=(n_seq,),
        in_specs=[per_seq(q3), per_seq(gates3), per_seq(kc), per_seq(vc),
                  pl.BlockSpec(memory_space=pl.ANY),
                  new(2), new(3), per_seq(win_t), new(4), new(5),
                  pl.BlockSpec(expand.shape, lambda b, pt: (0, 0))],
        out_specs=pl.BlockSpec((1, nq, NSA_WIDTH), lambda b, pt: (b, 0, 0)),
        scratch_shapes=[pltpu.VMEM((4, KV_WIDTH, past), F32),
                        pltpu.VMEM((rows, past), F32), pltpu.SemaphoreType.DMA((2,))])
    return pl.pallas_call(
        functools.partial(_nsa_sample_kernel, nq=nq, past=past, nbp=nbp, tk=tk),
        grid_spec=grid_spec,
        out_shape=jax.ShapeDtypeStruct((n_seq, nq, NSA_WIDTH), F32),
        compiler_params=_params("arbitrary"),
        name="nsa_sample",
    )(pt_flat, q3, gates3, kc, vc, cache_t, new_rows, new_rows, win_t, new_rows, new_rows, expand)


def _expand_matrix(nb, n_keys):
    return jnp.asarray(np.arange(n_keys)[None, :] // BLK == np.arange(nb)[:, None], dtype=BF16)


def _merge_kernel(x_ref, ssm_ref, nsa_ref, gpre_ref, wm_ref, wbs_ref, wbn_ref, wo_ref, gpost_ref,
                  gx_ref, wxq_ref, x1_ref, qx_ref):
    x = x_ref[...]
    a = _rms(x, gpre_ref[...]).astype(BF16)
    g_ssm = _sigmoid(_dot(a, wm_ref[:, 0:D_MODEL]))
    g_nsa = _sigmoid(_dot(a, wm_ref[:, D_MODEL:2 * D_MODEL]))
    merged = (g_ssm * _dot(ssm_ref[...].astype(BF16), wbs_ref[...])
              + g_nsa * _dot(nsa_ref[...], wbn_ref[...]))
    x1 = x + _rms(_dot(merged.astype(BF16), wo_ref[...]), gpost_ref[...])
    x1_ref[...] = x1
    c = _rms(x1, gx_ref[...]).astype(BF16)
    qx_ref[...] = (_dot(c, wxq_ref[...]) * (X_HEAD_DIM ** -0.5)).astype(BF16)


def _merge(x2d, ssm_y, nsa_o, weights, *, tm, ssm_time_major, n_tab_blocks):
    n = x2d.shape[0]
    steps = n // tm
    row = lambda w_: pl.BlockSpec((tm, w_), lambda i: (i, 0))
    full = lambda a: pl.BlockSpec(a.shape, lambda i: (0,) * a.ndim)
    if ssm_time_major:
        ssm_spec = pl.BlockSpec((tm, SSM_WIDTH), lambda i: (i % n_tab_blocks, i // n_tab_blocks))
    else:
        ssm_spec = row(SSM_WIDTH)
    return pl.pallas_call(
        _merge_kernel,
        grid=(steps,),
        in_specs=[row(D_MODEL), ssm_spec, row(NSA_WIDTH)] + [full(w) for w in weights],
        out_specs=[row(D_MODEL), row(X_WIDTH)],
        out_shape=[jax.ShapeDtypeStruct((n, D_MODEL), F32), jax.ShapeDtypeStruct((n, X_WIDTH), BF16)],
        compiler_params=_params("parallel"),
        name="merge",
    )(x2d, ssm_y, nsa_o, *weights)


def _xattn_kernel(q_ref, kv_ref, o_ref, *, m_len):
    q = q_ref[0].astype(BF16)
    outs = []
    for h in range(X_HEADS):
        cols = slice(h * X_HEAD_DIM, (h + 1) * X_HEAD_DIM)
        k = kv_ref[0, pl.ds(h, m_len, stride=2 * X_HEADS), :].astype(BF16)
        v = kv_ref[0, pl.ds(X_HEADS + h, m_len, stride=2 * X_HEADS), :].astype(BF16)
        s = _dot_nt(q[:, cols], k)
        m = jnp.max(s, axis=-1, keepdims=True)
        p = jnp.exp(s - m)
        p = p / jnp.sum(p, axis=-1, keepdims=True)
        outs.append(_dot(p.astype(BF16), v))
    o_ref[0] = jnp.concatenate(outs, axis=1).astype(o_ref.dtype)


def _xattn(q3, mem_kv_rows, *, tq):
    n_seq, t, _ = q3.shape
    m_len = mem_kv_rows.shape[1] // (2 * X_HEADS)
    return pl.pallas_call(
        functools.partial(_xattn_kernel, m_len=m_len),
        grid=(n_seq, t // tq),
        in_specs=[pl.BlockSpec((1, tq, X_WIDTH), lambda b, i: (b, i, 0)),
                  pl.BlockSpec((1, m_len * 2 * X_HEADS, X_HEAD_DIM), lambda b, i: (b, 0, 0))],
        out_specs=pl.BlockSpec((1, tq, X_WIDTH), lambda b, i: (b, i, 0)),
        out_shape=jax.ShapeDtypeStruct((n_seq, t, X_WIDTH), q3.dtype),
        compiler_params=_params("parallel", "parallel"),
        name="xattn",
    )(q3, mem_kv_rows)


def _mlp_kernel(x1_ref, o_ref, wxo_ref, gxp_ref, gm_ref, wup_ref, wdn_ref, gmp_ref, y_ref):
    x2 = x1_ref[...] + _rms(_dot(o_ref[...], wxo_ref[...]), gxp_ref[...])
    m = _rms(x2, gm_ref[...]).astype(BF16)
    hid = jnp.maximum(_dot(m, wup_ref[...]), 0.0)
    hid = (hid * hid).astype(BF16)
    y_ref[...] = x2 + _rms(_dot(hid, wdn_ref[...]), gmp_ref[...])


def _mlp(x1, o, weights, *, tm):
    n = x1.shape[0]
    row = lambda w_: pl.BlockSpec((tm, w_), lambda i: (i, 0))
    full = lambda a: pl.BlockSpec(a.shape, lambda i: (0,) * a.ndim)
    return pl.pallas_call(
        _mlp_kernel,
        grid=(n // tm,),
        in_specs=[row(D_MODEL), row(X_WIDTH)] + [full(w) for w in weights],
        out_specs=row(D_MODEL),
        out_shape=jax.ShapeDtypeStruct((n, D_MODEL), F32),
        compiler_params=_params("parallel"),
        name="mlp",
    )(x1, o, *weights)


def _memkv_kernel(m_ref, g_ref, w_ref, o_ref, *, tm):
    kv = _dot(_rms(m_ref[...], g_ref[...]).astype(BF16), w_ref[...])
    n_rows = 2 * X_HEADS
    for j in range(n_rows):
        o_ref[pl.ds(j, tm, stride=n_rows), :] = kv[:, j * X_HEAD_DIM:(j + 1) * X_HEAD_DIM]


def _memkv(mem2d, g, w, *, tm):
    n = mem2d.shape[0]
    n_rows = 2 * X_HEADS
    return pl.pallas_call(
        functools.partial(_memkv_kernel, tm=tm),
        grid=(n // tm,),
        in_specs=[pl.BlockSpec((tm, D_MODEL), lambda i: (i, 0)),
                  pl.BlockSpec(g.shape, lambda i: (0, 0)), pl.BlockSpec(w.shape, lambda i: (0, 0))],
        out_specs=pl.BlockSpec((tm * n_rows, X_HEAD_DIM), lambda i: (i, 0)),
        out_shape=jax.ShapeDtypeStruct((n * n_rows, X_HEAD_DIM), F32),
        compiler_params=_params("parallel"),
        name="memkv",
    )(mem2d, g, w)


def _row(v):
    return v.astype(F32).reshape(1, -1)


def kernel(x_prompt, x_sample, cache_nsa_kv, cache_win_kv, state_ssm, cache_mem_kv, page_table, mem_prompt, g_mix_pre, w_in, ssm_lam_re, ssm_lam_im, ssm_log_dt, ssm_b_re, ssm_b_im, ssm_c_re, ssm_c_im, ssm_d, w_glu, b_glu, cmp_pe_k, w_cmpk1, w_cmpk2, cmp_pe_v, w_cmpv1, w_cmpv2, w_br_ssm, w_br_nsa, w_out, g_mix_post, g_x_pre, g_mem, w_xq, w_xk, w_xv, w_xo, g_x_post, g_mlp_pre, w_up, w_down, g_mlp_post):
    depth = w_in.shape[0]
    n_seq_p, seq, _ = x_prompt.shape
    n_seq_s, nq, _ = x_sample.shape
    past = page_table.shape[1] * PAGE_SIZE
    assert depth == 1 and seq % BLK == 0 and nq <= SUBLANES and past % BLK == 0
    assert n_seq_p == SUBLANES and n_seq_s % SUBLANES == 0

    y_p = x_prompt.reshape(n_seq_p * seq, D_MODEL)
    y_s = x_sample.reshape(n_seq_s * nq, D_MODEL)
    l = 0

    w_proj = w_in[l, :, :N_PROJ].astype(BF16)
    w_gate = jnp.pad(w_in[l, :, N_PROJ:N_PROJ + N_GATE], ((0, 0), (0, LANES - N_GATE))).astype(BF16)
    w_merge = w_in[l, :, N_PROJ + N_GATE:].astype(BF16)
    lam_l, bm, cm = _ssm_params(ssm_lam_re[l], ssm_lam_im[l], ssm_log_dt[l], ssm_b_re[l], ssm_b_im[l],
                                ssm_c_re[l], ssm_c_im[l])
    ssm_w = (lam_l, bm, cm, _row(ssm_d[l]), w_glu[l].astype(BF16), _row(b_glu[l]))
    cmp_w = _compress_params(cmp_pe_k[l], w_cmpk1[l], w_cmpk2[l], cmp_pe_v[l], w_cmpv1[l], w_cmpv2[l])
    merge_w = (_row(g_mix_pre[l]), w_merge, w_br_ssm[l].astype(BF16), w_br_nsa[l].astype(BF16),
               w_out[l].astype(BF16), _row(g_mix_post[l]), _row(g_x_pre[l]), w_xq[l].astype(BF16))
    mlp_w = (w_xo[l].astype(BF16), _row(g_x_post[l]), _row(g_mlp_pre[l]), w_up[l].astype(BF16),
             w_down[l].astype(BF16), _row(g_mlp_post[l]))
    w_mem = jnp.concatenate([w_xk[l], w_xv[l]], axis=1).astype(BF16)

    tm_p = 512 if seq % 512 == 0 else seq
    nt_p = seq // tm_p
    tabs_p = _rope_tables(jnp.arange(seq, dtype=jnp.int32))
    u_p, q_p, kvt_p, wint_p, kvb_p, gate_p, cmp_rows_p = _proj(
        y_p, _row(g_mix_pre[l]), w_proj, w_gate, tabs_p, tm=tm_p, n_tab_blocks=nt_p, prompt=True)
    u_p = u_p.reshape(seq * n_seq_p, SSM_WIDTH)
    h0_p = jnp.zeros((1, SUBLANES, 2 * N_STATE), F32)
    tc_p = 64 if seq % 64 == 0 else seq
    ssm_p, hl_p = _ssm(u_p, h0_p, *ssm_w, n_groups=1, n_time=seq, tc=tc_p)
    ssm_p = ssm_p.reshape(seq, n_seq_p * SSM_WIDTH)

    nb_p = seq // BLK
    n_blocks_p = n_seq_p * nb_p
    cmp_p = _compress(cmp_rows_p, *cmp_w, n_blocks=n_blocks_p, tm=min(n_blocks_p, 256))
    nsa_p = _nsa_prompt(q_p, gate_p, cmp_p, kvb_p, _expand_matrix(LANES, seq), n_seq=n_seq_p, seq=seq)

    mem_kv_p = _memkv(mem_prompt.reshape(-1, D_MODEL), _row(g_mem[l]), w_mem, tm=256)
    m_len = mem_prompt.shape[1]
    mem_kv_p3 = mem_kv_p.reshape(n_seq_p, m_len * 2 * X_HEADS, X_HEAD_DIM)

    x1_p, qx_p = _merge(y_p, ssm_p, nsa_p, merge_w, tm=tm_p, ssm_time_major=True, n_tab_blocks=nt_p)
    o_p = _xattn(qx_p.reshape(n_seq_p, seq, X_WIDTH), mem_kv_p3, tq=tm_p)
    y_p = _mlp(x1_p, o_p.reshape(-1, X_WIDTH), mlp_w, tm=256)

    n_s = n_seq_s * nq
    pos_s = past + jnp.arange(nq, dtype=jnp.int32)
    tabs_s = tuple(jnp.tile(t, (n_seq_s, 1)) for t in _rope_tables(pos_s))
    u_s, q_s, kv_s, win_s, kvb_s, gate_s = _proj(y_s, _row(g_mix_pre[l]), w_proj, w_gate, tabs_s,
                                                 tm=n_s, n_tab_blocks=1, prompt=False)
    n_grp = n_seq_s // SUBLANES
    to_tb = lambda a: a.reshape(n_grp, SUBLANES, nq, SSM_WIDTH).transpose(0, 2, 1, 3).reshape(n_s, SSM_WIDTH)
    st = state_ssm[l].astype(F32).reshape(n_seq_s, N_STATE, 2)
    h0_s = _state_lanes(st[..., 0], st[..., 1]).reshape(n_grp, SUBLANES, 2 * N_STATE)
    ssm_s, hl_s = _ssm(to_tb(u_s), h0_s, *ssm_w, n_groups=n_grp, n_time=nq, tc=nq)
    ssm_s = ssm_s.reshape(n_grp, nq, SUBLANES, SSM_WIDTH).transpose(0, 2, 1, 3).reshape(n_s, SSM_WIDTH)

    n_pages = page_table.shape[1]
    n_pool = cache_nsa_kv.shape[1]
    cache_t = jnp.transpose(cache_nsa_kv[l], (0, 2, 3, 4, 1)).reshape(n_pool, 4, KV_WIDTH, PAGE_SIZE)
    win_t = jnp.transpose(cache_win_kv[l], (0, 2, 3, 4, 1)).reshape(n_seq_s, 2, KV_WIDTH, -1)
    pt_flat = page_table.reshape(-1).astype(jnp.int32)
    nb_past = past // BLK
    pe_t, w1_t = _compress_paged_params(cmp_pe_k[l], w_cmpk1[l], cmp_pe_v[l], w_cmpv1[l])
    cmp_pages = _compress_paged(pt_flat, cache_t, pe_t, w1_t, cmp_w[2], m=min(n_seq_s * n_pages, 128))
    cmp_past = cmp_pages
    new_rows = jnp.pad(jnp.concatenate([kv_s, win_s], axis=1).reshape(n_seq_s, nq, 6 * KV_WIDTH),
                       ((0, 0), (0, LANES - nq), (0, 0)))
    cmp_new = _compress(new_rows[:, :BLK].reshape(n_seq_s * BLK, 6 * KV_WIDTH), *cmp_w,
                        n_blocks=n_seq_s, tm=n_seq_s)
    nbp = -(-(nb_past + 1) // LANES) * LANES
    cmp_s = jnp.concatenate([cmp_past.reshape(2, n_seq_s, nb_past, KV_WIDTH), cmp_new[:, :, None, :],
                             jnp.zeros((2, n_seq_s, nbp - nb_past - 1, KV_WIDTH), F32)], axis=2)
    nsa_s = _nsa_sample(pt_flat, q_s.astype(F32).reshape(n_seq_s, nq, NSA_WIDTH),
                        gate_s.reshape(n_seq_s, nq, LANES), cmp_s[0], cmp_s[1], cache_t, new_rows, win_t,
                        _expand_matrix(nbp, past + LANES), past=past, tk=min(past, 1024))

    x1_s, qx_s = _merge(y_s, ssm_s, nsa_s.reshape(n_s, NSA_WIDTH).astype(BF16), merge_w, tm=n_s,
                        ssm_time_major=False, n_tab_blocks=1)
    mem_kv_s3 = cache_mem_kv[l].reshape(n_seq_s, m_len * 2 * X_HEADS, X_HEAD_DIM)
    o_s = _xattn(qx_s.astype(F32).reshape(n_seq_s, nq, X_WIDTH), mem_kv_s3, tq=nq)
    y_s = _mlp(x1_s, o_s.reshape(-1, X_WIDTH).astype(BF16), mlp_w, tm=n_s)

    def ssm_state(hl, n_seq):
        re, im = _state_unlanes(hl.reshape(n_seq, 2 * N_STATE))
        return jnp.stack([re, im], axis=-1).reshape(1, n_seq, SSM_GROUPS, SSM_STATE, 2)

    def token_major(xt):
        n_seq, parts, _, t_len = xt.shape
        return xt.reshape(1, n_seq, parts, NSA_KV, HEAD_DIM, t_len).transpose(0, 1, 5, 2, 3, 4)

    w_keep = min(WINDOW, seq)
    win_new = win_s.reshape(n_seq_s, nq, 2, NSA_KV, HEAD_DIM).astype(cache_win_kv.dtype)
    win_sample = jnp.concatenate([cache_win_kv[l], win_new], axis=1)[:, nq:]
    return (y_p.reshape(n_seq_p, seq, D_MODEL),
            y_s.reshape(n_seq_s, nq, D_MODEL),
            token_major(kvt_p),
            kv_s.reshape(1, n_seq_s, nq, 4, NSA_KV, HEAD_DIM),
            token_major(wint_p[:, :, :, seq - w_keep:]),
            win_sample[None],
            ssm_state(hl_p, n_seq_p),
            ssm_state(hl_s, n_seq_s),
            mem_kv_p.reshape(1, n_seq_p, m_len, 2, X_HEADS, X_HEAD_DIM))
```

```python
import functools
import math

import jax
import jax.numpy as jnp
import numpy as np
from jax import lax
from jax.experimental import pallas as pl
from jax.experimental.pallas import tpu as pltpu

F32 = jnp.float32
BF16 = jnp.bfloat16

D_MODEL = 1024
SSM_WIDTH = 512
SSM_GROUP = 16
SSM_GROUPS = 32
SSM_STATE = 64
N_STATE = SSM_GROUPS * SSM_STATE
STATE_CHUNK = 512
N_CHUNKS = N_STATE // STATE_CHUNK
NSA_HEADS = 8
HEAD_DIM = 64
NSA_WIDTH = NSA_HEADS * HEAD_DIM
NSA_KV = 2
NSA_REP = NSA_HEADS // NSA_KV
KV_WIDTH = NSA_KV * HEAD_DIM
BLK = 64
N_SEL = 16
WINDOW = 512
ROT_DIM = 16
ROPE_THETA = 500000.0
PAGE_SIZE = 128
X_HEADS = 4
X_HEAD_DIM = 128
X_WIDTH = X_HEADS * X_HEAD_DIM
D_FF = 4 * D_MODEL
EPS = 1e-6
NEG_INF = -1e30
M_INIT = -1e29
FORCE_SCORE = 1e4
LANES = 128
SUBLANES = 8
VMEM_LIMIT = 56 * 1024 * 1024

N_PROJ = SSM_WIDTH + NSA_WIDTH + 6 * KV_WIDTH
N_GATE = 3 * NSA_HEADS
SEL_CHUNK = 512
PAGE_PITCH = PAGE_SIZE + SUBLANES


def _params(*sem):
    return pltpu.CompilerParams(dimension_semantics=sem, vmem_limit_bytes=VMEM_LIMIT)


def _rms(x, g):
    return x * lax.rsqrt(jnp.mean(x * x, axis=-1, keepdims=True) + EPS) * g


def _gelu(x):
    return 0.5 * x * (1.0 + jnp.tanh(math.sqrt(2.0 / math.pi) * (x + 0.044715 * (x * x * x))))


def _sigmoid(x):
    return 1.0 / (1.0 + jnp.exp(-x))


def _dot(a, b):
    return jnp.dot(a, b, preferred_element_type=F32)


def _dot_nt(a, b):
    return lax.dot_general(a, b, (((1,), (1,)), ((), ())), preferred_element_type=F32)


def _proj_kernel(x_ref, g_ref, w_ref, wg_ref, cos_ref, sp_ref, sm_ref,
                 u_ref, q_ref, kv_ref, win_ref, kvb_ref, gate_ref, *cmp_ref, token_minor):
    a = _rms(x_ref[...], g_ref[...]).astype(BF16)
    cos, sp, sm = cos_ref[...], sp_ref[...], sm_ref[...]

    def rope(blk):
        return blk * cos + pltpu.roll(blk, 8, 1) * sp + pltpu.roll(blk, LANES - 8, 1) * sm

    u_ref[...] = _dot(a, w_ref[:, 0:SSM_WIDTH])
    for j in range(NSA_WIDTH // LANES):
        c0 = SSM_WIDTH + j * LANES
        q_ref[:, j * LANES:(j + 1) * LANES] = rope(_dot(a, w_ref[:, c0:c0 + LANES])).astype(BF16)
    for j in range(6):
        c0 = SSM_WIDTH + NSA_WIDTH + j * LANES
        blk = _dot(a, w_ref[:, c0:c0 + LANES])
        if j % 2 == 0:
            blk = rope(blk)
        out_ref, part = (kv_ref, j) if j < 4 else (win_ref, j - 4)
        if token_minor:
            out_ref[0, part] = blk.T
            if j < 2:
                cmp_ref[0][:, j * LANES:(j + 1) * LANES] = blk
        else:
            out_ref[:, part * LANES:(part + 1) * LANES] = blk
        kvb_ref[:, j * LANES:(j + 1) * LANES] = blk.astype(BF16)
    gate_ref[...] = _sigmoid(_dot(a, wg_ref[...]))


def _proj(x2d, g, w, wg, tabs, *, tm, n_tab_blocks, prompt):
    n = x2d.shape[0]
    steps = n // tm
    row = lambda w_: pl.BlockSpec((tm, w_), lambda i: (i, 0))
    full = lambda a: pl.BlockSpec(a.shape, lambda i: (0,) * a.ndim)
    tab = pl.BlockSpec((tm, LANES), lambda i: (i % n_tab_blocks, 0))
    nt = n_tab_blocks
    if prompt:
        n_b = steps // nt
        t_len = nt * tm
        u_shape = jax.ShapeDtypeStruct((t_len, n_b * SSM_WIDTH), F32)
        u_spec = pl.BlockSpec((tm, SSM_WIDTH), lambda i: (i % nt, i // nt))
        tok_minor = lambda parts: pl.BlockSpec((1, parts, KV_WIDTH, tm), lambda i: (i // nt, 0, 0, i % nt))
        kv_specs = [tok_minor(4), tok_minor(2)]
        kv_shapes = [jax.ShapeDtypeStruct((n_b, 4, KV_WIDTH, t_len), F32),
                     jax.ShapeDtypeStruct((n_b, 2, KV_WIDTH, t_len), F32)]
        extra_specs = [row(2 * KV_WIDTH)]
        extra_shapes = [jax.ShapeDtypeStruct((n, 2 * KV_WIDTH), F32)]
    else:
        u_shape = jax.ShapeDtypeStruct((n, SSM_WIDTH), F32)
        u_spec = pl.BlockSpec((tm, SSM_WIDTH), lambda i: (i, 0))
        kv_specs = [row(4 * KV_WIDTH), row(2 * KV_WIDTH)]
        kv_shapes = [jax.ShapeDtypeStruct((n, 4 * KV_WIDTH), F32), jax.ShapeDtypeStruct((n, 2 * KV_WIDTH), F32)]
        extra_specs, extra_shapes = [], []
    return pl.pallas_call(
        functools.partial(_proj_kernel, token_minor=prompt),
        grid=(steps,),
        in_specs=[row(D_MODEL), full(g), full(w), full(wg), tab, tab, tab],
        out_specs=[u_spec, row(NSA_WIDTH)] + kv_specs + [row(6 * KV_WIDTH), row(LANES)] + extra_specs,
        out_shape=[u_shape, jax.ShapeDtypeStruct((n, NSA_WIDTH), BF16)] + kv_shapes
                  + [jax.ShapeDtypeStruct((n, 6 * KV_WIDTH), BF16), jax.ShapeDtypeStruct((n, LANES), F32)]
                  + extra_shapes,
        compiler_params=_params("parallel"),
        name="proj",
    )(x2d, g, w, wg, *tabs)


def _rope_tables(pos):
    half = ROT_DIM // 2
    freqs = ROPE_THETA ** (-jnp.arange(half, dtype=F32) / half)
    ang = pos.astype(F32)[:, None] * freqs[None, :]
    cos, sin = jnp.cos(ang), jnp.sin(ang)
    r = pos.shape[0]
    z8 = jnp.zeros((r, half), F32)
    rest0 = jnp.zeros((r, HEAD_DIM - ROT_DIM), F32)
    rest1 = jnp.ones((r, HEAD_DIM - ROT_DIM), F32)
    c64 = jnp.concatenate([cos, cos, rest1], axis=1)
    sp64 = jnp.concatenate([z8, sin, rest0], axis=1)
    sm64 = jnp.concatenate([-sin, z8, rest0], axis=1)
    return tuple(jnp.tile(t, (1, LANES // HEAD_DIM)) for t in (c64, sp64, sm64))


def _ssm_kernel(u_ref, h0_ref, lam_ref, bm_ref, cm_ref, d_ref, wglu_ref, bglu_ref,
                y_ref, hlast_ref, hs_ref, hstate_ref, *, tc):
    j = pl.program_id(1)

    @pl.when(j == 0)
    def _():
        hstate_ref[...] = h0_ref[0]

    u = u_ref[...]
    ub = u.astype(BF16)
    half_in = SSM_WIDTH // 2
    half_st = N_STATE
    for h in range(2):
        hs_ref[:, h * half_st:(h + 1) * half_st] = _dot(ub[:, h * half_in:(h + 1) * half_in], bm_ref[h])

    for c in range(N_CHUNKS):
        re0 = c * 2 * STATE_CHUNK
        im0 = re0 + STATE_CHUNK
        lr = jnp.broadcast_to(lam_ref[0:1, re0:re0 + STATE_CHUNK], (SUBLANES, STATE_CHUNK))
        li = jnp.broadcast_to(lam_ref[0:1, im0:im0 + STATE_CHUNK], (SUBLANES, STATE_CHUNK))

        def step(t, carry, re0=re0, im0=im0, lr=lr, li=li):
            hr, hi = carry
            r0 = pl.multiple_of(t * SUBLANES, SUBLANES)
            nr = lr * hr - li * hi + hs_ref[pl.ds(r0, SUBLANES), re0:re0 + STATE_CHUNK]
            ni = lr * hi + li * hr + hs_ref[pl.ds(r0, SUBLANES), im0:im0 + STATE_CHUNK]
            hs_ref[pl.ds(r0, SUBLANES), re0:re0 + STATE_CHUNK] = nr
            hs_ref[pl.ds(r0, SUBLANES), im0:im0 + STATE_CHUNK] = ni
            return nr, ni

        hr, hi = lax.fori_loop(0, tc, step,
                               (hstate_ref[:, re0:re0 + STATE_CHUNK], hstate_ref[:, im0:im0 + STATE_CHUNK]),
                               unroll=min(tc, 4))
        hstate_ref[:, re0:re0 + STATE_CHUNK] = hr
        hstate_ref[:, im0:im0 + STATE_CHUNK] = hi

    ys = [_dot(hs_ref[:, h * half_st:(h + 1) * half_st].astype(BF16), cm_ref[h]) for h in range(2)]
    y = jnp.concatenate(ys, axis=1) + d_ref[...] * u
    y = _gelu(y)
    z = _dot(y.astype(BF16), wglu_ref[...]) + bglu_ref[...]
    y_ref[...] = y * _sigmoid(z)

    @pl.when(j == pl.num_programs(1) - 1)
    def _():
        hlast_ref[0] = hstate_ref[...]


def _ssm(u_tb, h0, lam, bm, cm, d, wglu, bglu, *, n_groups, n_time, tc):
    rows = tc * SUBLANES
    nt = n_time // tc
    full = lambda a: pl.BlockSpec(a.shape, lambda g, j: (0,) * a.ndim)
    st = pl.BlockSpec((1, SUBLANES, 2 * N_STATE), lambda g, j: (g, 0, 0))
    return pl.pallas_call(
        functools.partial(_ssm_kernel, tc=tc),
        grid=(n_groups, nt),
        in_specs=[pl.BlockSpec((rows, SSM_WIDTH), lambda g, j: (g * nt + j, 0)), st,
                  full(lam), full(bm), full(cm), full(d), full(wglu), full(bglu)],
        out_specs=[pl.BlockSpec((rows, SSM_WIDTH), lambda g, j: (g * nt + j, 0)), st],
        out_shape=[jax.ShapeDtypeStruct((n_groups * n_time * SUBLANES, SSM_WIDTH), F32),
                   jax.ShapeDtypeStruct((n_groups, SUBLANES, 2 * N_STATE), F32)],
        scratch_shapes=[pltpu.VMEM((rows, 2 * N_STATE), F32), pltpu.VMEM((SUBLANES, 2 * N_STATE), F32)],
        compiler_params=_params("parallel", "arbitrary"),
        name="ssm",
    )(u_tb, h0, lam, bm, cm, d, wglu, bglu)


def _state_lanes(re, im):
    lead = re.shape[:-1]
    r = re.reshape(lead + (N_CHUNKS, 1, STATE_CHUNK))
    i = im.reshape(lead + (N_CHUNKS, 1, STATE_CHUNK))
    return jnp.concatenate([r, i], axis=-2).reshape(lead + (2 * N_STATE,))


def _state_unlanes(x):
    lead = x.shape[:-1]
    y = x.reshape(lead + (N_CHUNKS, 2, STATE_CHUNK))
    return y[..., 0, :].reshape(lead + (N_STATE,)), y[..., 1, :].reshape(lead + (N_STATE,))


def _ssm_params(lam_re, lam_im, log_dt, b_re, b_im, c_re, c_im):
    lam = lax.complex(lam_re.astype(F32), lam_im.astype(F32))
    dt = jnp.exp(log_dt.astype(F32))[:, None]
    lam_bar = jnp.exp(lam * dt)
    b = lax.complex(b_re.astype(F32), b_im.astype(F32))
    b_bar = ((lam_bar - 1.0) / lam)[..., None] * b
    eye = jnp.eye(SSM_GROUPS, dtype=F32)

    def in_blockdiag(x):
        return jnp.einsum('gpc,gh->gchp', x, eye).reshape(SSM_WIDTH, N_STATE)

    def out_blockdiag(x):
        return jnp.einsum('gcp,gh->gphc', x, eye).reshape(N_STATE, SSM_WIDTH)

    b_full = _state_lanes(in_blockdiag(jnp.real(b_bar)), in_blockdiag(jnp.imag(b_bar)))
    c_full = _state_lanes(out_blockdiag(c_re.astype(F32)).T, -out_blockdiag(c_im.astype(F32)).T).T
    hw, hs = SSM_WIDTH // 2, N_STATE
    bm = jnp.stack([b_full[h * hw:(h + 1) * hw, h * hs:(h + 1) * hs] for h in range(2)]).astype(BF16)
    cm = jnp.stack([c_full[h * hs:(h + 1) * hs, h * hw:(h + 1) * hw] for h in range(2)]).astype(BF16)
    lam_l = _state_lanes(jnp.real(lam_bar).reshape(1, N_STATE), jnp.imag(lam_bar).reshape(1, N_STATE))
    return lam_l, bm, cm


def _compress_kernel(x_ref, pe_ref, w1_ref, w2_ref, o_ref, *, tm):
    def body(sp, acc):
        s0 = 2 * sp
        xa = x_ref[pl.ds(s0, tm, stride=BLK), :] + pe_ref[0, pl.ds(s0, 1), :]
        xb = x_ref[pl.ds(s0 + 1, tm, stride=BLK), :] + pe_ref[0, pl.ds(s0 + 1, 1), :]
        lhs = jnp.concatenate([xa, xb], axis=1).astype(BF16)
        return acc + _dot(lhs, w1_ref[0, sp])

    acc = lax.fori_loop(0, BLK // 2, body, jnp.zeros((tm, KV_WIDTH), F32))
    o_ref[0] = _dot(_gelu(acc).astype(BF16), w2_ref[0])


def _compress(x2d, pe, w1, w2, *, n_blocks, tm):
    return pl.pallas_call(
        functools.partial(_compress_kernel, tm=tm),
        grid=(2, n_blocks // tm),
        in_specs=[pl.BlockSpec((tm * BLK, KV_WIDTH), lambda c, i: (i, c)),
                  pl.BlockSpec((1, BLK, KV_WIDTH), lambda c, i: (c, 0, 0)),
                  pl.BlockSpec((1, BLK // 2, 2 * KV_WIDTH, KV_WIDTH), lambda c, i: (c, 0, 0, 0)),
                  pl.BlockSpec((1, KV_WIDTH, KV_WIDTH), lambda c, i: (c, 0, 0))],
        out_specs=pl.BlockSpec((1, tm, KV_WIDTH), lambda c, i: (c, i, 0)),
        out_shape=jax.ShapeDtypeStruct((2, n_blocks, KV_WIDTH), F32),
        compiler_params=_params("parallel", "parallel"),
        name="compress",
    )(x2d, pe, w1, w2)


def _compress_params(pe_k, w1_k, w2_k, pe_v, w1_v, w2_v):
    def bd(w):
        z = jnp.zeros_like(w)
        return jnp.concatenate([jnp.concatenate([w, z], axis=-1), jnp.concatenate([z, w], axis=-1)], axis=-2)

    def one(pe, w1, w2):
        w1s = bd(w1.astype(F32).reshape(BLK, HEAD_DIM, HEAD_DIM))
        return (jnp.tile(pe.astype(F32), (1, NSA_KV)),
                w1s.reshape(BLK // 2, 2 * KV_WIDTH, KV_WIDTH).astype(BF16),
                bd(w2.astype(F32)).astype(BF16))

    k, v = one(pe_k, w1_k, w2_k), one(pe_v, w1_v, w2_v)
    return tuple(jnp.stack([a, b]) for a, b in zip(k, v))


def _compress_paged_kernel(pt_ref, cache_ref, pe_ref, w1_ref, w2_ref, o_ref, buf_ref, sems, *, m):
    step = pl.program_id(0)

    def page_copy(s, j):
        slot = s % 2
        row0 = pl.multiple_of(j * PAGE_PITCH, SUBLANES)
        return pltpu.make_async_copy(cache_ref.at[pt_ref[s * m + j], pl.ds(0, 2)],
                                     buf_ref.at[pl.ds(2 * slot, 2), pl.ds(row0, PAGE_SIZE), :],
                                     sems.at[slot])

    def start_all(s):
        def start(j, _):
            page_copy(s, j).start()
            return 0
        lax.fori_loop(0, m, start, 0, unroll=8)

    @pl.when(step == 0)
    def _():
        start_all(step)

    @pl.when(step + 1 < pl.num_programs(0))
    def _():
        start_all(step + 1)

    def wait(j, _):
        page_copy(step, j).wait()
        return 0

    lax.fori_loop(0, m, wait, 0, unroll=8)

    for c in range(2):
        tile = 2 * (step % 2) + c
        res = []
        for kv in range(NSA_KV):
            def body(dp, acc, c=c, kv=kv, tile=tile):
                d0 = 2 * dp
                r0 = kv * HEAD_DIM + d0
                xa = buf_ref[tile, pl.ds(r0, m, stride=PAGE_PITCH), :] + pe_ref[c, pl.ds(d0, 1), :]
                xb = buf_ref[tile, pl.ds(r0 + 1, m, stride=PAGE_PITCH), :] + pe_ref[c, pl.ds(d0 + 1, 1), :]
                lhs = jnp.concatenate([xa, xb], axis=1).astype(BF16)
                return acc + _dot(lhs, w1_ref[c, dp])

            acc = lax.fori_loop(0, HEAD_DIM // 2, body, jnp.zeros((m, PAGE_SIZE), F32), unroll=4)
            res.append(_dot(_gelu(acc).astype(BF16), w2_ref[c]))
        for blk in range(PAGE_SIZE // BLK):
            cols = slice(blk * HEAD_DIM, (blk + 1) * HEAD_DIM)
            o_ref[c, pl.ds(blk, m, stride=PAGE_SIZE // BLK), :] = jnp.concatenate(
                [r[:, cols] for r in res], axis=1)


def _compress_paged(pt_flat, cache_t, pe_t, w1_t, w2, *, m):
    n = pt_flat.shape[0]
    full = lambda a: pl.BlockSpec(a.shape, lambda i, pt: (0,) * a.ndim)
    grid_spec = pltpu.PrefetchScalarGridSpec(
        num_scalar_prefetch=1,
        grid=(n // m,),
        in_specs=[pl.BlockSpec(memory_space=pl.ANY), full(pe_t), full(w1_t), full(w2)],
        out_specs=pl.BlockSpec((2, m * (PAGE_SIZE // BLK), KV_WIDTH), lambda i, pt: (0, i, 0)),
        scratch_shapes=[pltpu.VMEM((4, m * PAGE_PITCH, PAGE_SIZE), F32), pltpu.SemaphoreType.DMA((2,))])
    return pl.pallas_call(
        functools.partial(_compress_paged_kernel, m=m),
        grid_spec=grid_spec,
        out_shape=jax.ShapeDtypeStruct((2, n * (PAGE_SIZE // BLK), KV_WIDTH), F32),
        compiler_params=_params("arbitrary"),
        name="compress_paged",
    )(pt_flat, cache_t, pe_t, w1_t, w2)


def _compress_paged_params(pe_k, w1_k, pe_v, w1_v):
    def bd(w):
        z = jnp.zeros_like(w)
        return jnp.concatenate([jnp.concatenate([w, z], axis=-1), jnp.concatenate([z, w], axis=-1)], axis=-2)

    def one(pe, w1):
        w1d = bd(w1.astype(F32).reshape(BLK, HEAD_DIM, HEAD_DIM).transpose(1, 0, 2))
        return (jnp.tile(pe.astype(F32).T, (1, PAGE_SIZE // BLK)),
                w1d.reshape(HEAD_DIM // 2, 2 * PAGE_SIZE, PAGE_SIZE).astype(BF16))

    k, v = one(pe_k, w1_k), one(pe_v, w1_v)
    return tuple(jnp.stack([a, b]) for a, b in zip(k, v))


def _stack_queries(q, nq):
    q = q.astype(F32)
    z = jnp.zeros((nq, HEAD_DIM), F32)
    rows = []
    for h in range(NSA_HEADS):
        blk = q[:, h * HEAD_DIM:(h + 1) * HEAD_DIM]
        rows.append(jnp.concatenate([blk, z] if h < NSA_REP else [z, blk], axis=1))
    return (jnp.concatenate(rows, axis=0) * (HEAD_DIM ** -0.5)).astype(BF16)


def _masked_softmax(s, valid):
    s = jnp.where(valid, s, NEG_INF)
    m = jnp.max(s, axis=-1, keepdims=True)
    p = jnp.exp(s - m) * valid.astype(F32)
    return p / jnp.maximum(jnp.sum(p, axis=-1, keepdims=True), 1e-30)


def _select_blocks(imp, n_ids, cur, nb):
    forced = (n_ids == 0) | (n_ids == cur) | (n_ids == cur - 1)
    imp = jnp.where(forced, FORCE_SCORE, imp)
    imp = jnp.where(n_ids <= cur, imp, -FORCE_SCORE)
    rank = jnp.zeros(imp.shape, F32)
    for m in range(nb):
        col = imp[:, m:m + 1]
        beats = (col > imp) | ((col == imp) & (n_ids > m))
        rank = rank + beats.astype(F32)
    return (rank < float(N_SEL)).astype(F32)


def _select_blocks_t(imp_t, cur, nb):
    n_t = lax.broadcasted_iota(jnp.int32, imp_t.shape, 0)
    forced = (n_t == 0) | (n_t == cur) | (n_t == cur - 1)
    imp_t = jnp.where(forced, FORCE_SCORE, imp_t)
    imp_t = jnp.where(n_t <= cur, imp_t, -FORCE_SCORE)
    rank = jnp.zeros(imp_t.shape, F32)
    for m in range(nb):
        row = imp_t[m:m + 1, :]
        beats = (row > imp_t) | ((row == imp_t) & (n_t > m))
        rank = rank + beats.astype(F32)
    return ((rank < float(N_SEL)) & (n_t <= cur)).astype(F32)


def _combine_heads(gates, o_c, o_s, o_w, nq):
    outs = []
    for h in range(NSA_HEADS):
        rows = slice(h * nq, (h + 1) * nq)
        o = (gates[:, 3 * h:3 * h + 1] * o_c[rows] + gates[:, 3 * h + 1:3 * h + 2] * o_s[rows]
             + gates[:, 3 * h + 2:3 * h + 3] * o_w[rows])
        g = h // NSA_REP
        outs.append(o[:, g * HEAD_DIM:(g + 1) * HEAD_DIM])
    return jnp.concatenate(outs, axis=1)


def _nsa_pair_kernel(qa_ref, qb_ref, ga_ref, gb_ref, kc_ref, vc_ref, ks_ref, vs_ref, kw_ref, vw_ref,
                     et_ref, tri_ref, wm_ref, oa_ref, ob_ref, q2_ref, etd_ref, s_ref, mx_ref, acc_ref,
                     *, nb, ch, wk):
    i = pl.program_id(1)
    nq = BLK
    rows = NSA_HEADS * nq
    bpc = ch // BLK
    n_units = nb // bpc + 1
    chunk_of = (i, nb - 1 - i)
    n_a = i // bpc + 1
    row_q = lax.broadcasted_iota(jnp.int32, (rows, LANES), 0) % nq
    rq_minus_lane = row_q - lax.broadcasted_iota(jnp.int32, (rows, LANES), 1)

    def col_blocks(s):
        return [s[:, j * LANES:(j + 1) * LANES] for j in range(s.shape[1] // LANES)]

    def col_max(cols):
        m = cols[0]
        for c_ in cols[1:]:
            m = jnp.maximum(m, c_)
        return m

    def finish(acc):
        return acc[:, 0:KV_WIDTH] / jnp.maximum(acc[:, KV_WIDTH:2 * KV_WIDTH], 1e-30)

    def with_ones(v):
        return jnp.concatenate([v, jnp.ones((v.shape[0], LANES), BF16)], axis=1)

    q_onehot = jnp.where(rq_minus_lane == -BLK, NEG_INF, 0.0)

    def prepare(idx, q_ref):
        ci = chunk_of[idx]
        q2 = _stack_queries(q_ref[...], nq)
        first = jnp.maximum(ci - WINDOW // BLK, 0) // (LANES // BLK)
        w0 = pl.multiple_of(first * LANES, LANES)
        delta = ci - first * (LANES // BLK)
        lhs = jnp.concatenate([q2, q_onehot.astype(BF16)], axis=1)
        rhs = jnp.concatenate([kw_ref[pl.ds(w0, wk), :], wm_ref[delta]], axis=1)
        cols = col_blocks(_dot_nt(lhs, rhs))
        m_w = jnp.maximum(jnp.max(col_max(cols), axis=-1, keepdims=True), M_INIT)
        m_w = jnp.broadcast_to(m_w, (rows, LANES))
        p = jnp.concatenate([jnp.exp(c_ - m_w).astype(BF16) for c_ in cols], axis=1)
        o_w = finish(_dot(p, with_ones(vw_ref[pl.ds(w0, wk), :])))

        q_pos = ci * BLK + lax.broadcasted_iota(jnp.int32, (rows, 1), 0) % nq
        n_ids = lax.broadcasted_iota(jnp.int32, (rows, nb), 1)
        s_c = _dot_nt(q2, kc_ref[0].astype(BF16))
        p_c = _masked_softmax(s_c, (n_ids + 1) * BLK - 1 <= q_pos)
        o_c = _dot(p_c.astype(BF16), vc_ref[0].astype(BF16))

        imps = []
        for g in range(NSA_KV):
            imp = p_c[g * NSA_REP * nq:(g * NSA_REP + 1) * nq]
            for r in range(1, NSA_REP):
                imp = imp + p_c[(g * NSA_REP + r) * nq:(g * NSA_REP + r + 1) * nq]
            imps.append(imp)
        imp2 = jnp.concatenate([jnp.concatenate(imps, axis=0), jnp.zeros((LANES, LANES - nb), F32)], axis=1)
        sel_t = _select_blocks_t(imp2.T[0:nb], ci, nb)
        sel2 = jnp.concatenate([sel_t, jnp.ones((LANES - nb, LANES), F32)], axis=0).T
        neg = (sel2 - 1.0) * (-NEG_INF)
        neg_rows = jnp.concatenate([neg[g * nq:(g + 1) * nq] for g in range(NSA_KV) for _ in range(NSA_REP)],
                                   axis=0)
        q2_ref[idx] = jnp.concatenate([q2, (neg_rows + q_onehot).astype(BF16)], axis=1)
        in_diag = lax.broadcasted_iota(jnp.int32, (ch, LANES), 0) // BLK == ci % bpc
        etd_ref[idx] = et_ref[ci // bpc] + jnp.where(in_diag, tri_ref[...], jnp.zeros((), BF16))
        return o_c, o_w

    o_ca, o_wa = prepare(0, qa_ref)
    o_cb, o_wb = prepare(1, qb_ref)

    mx_ref[...] = jnp.full(mx_ref.shape, M_INIT, F32)
    acc_ref[...] = jnp.zeros(acc_ref.shape, F32)
    units = []
    for u in range(n_units):
        which = (u >= n_a).astype(jnp.int32)
        kc = u - which * n_a
        units.append((which, kc))
        k0 = pl.multiple_of(kc * ch, ch)
        diag = kc == jnp.where(which == 1, chunk_of[1], chunk_of[0]) // bpc
        key_mask = jnp.where(diag, etd_ref[which], et_ref[kc])
        rhs = jnp.concatenate([ks_ref[pl.ds(k0, ch), :], key_mask], axis=1)
        cols = col_blocks(_dot_nt(q2_ref[which], rhs))
        mx_ref[which] = jnp.maximum(mx_ref[which], col_max(cols))
        for j, c_ in enumerate(cols):
            s_ref[u, :, j * LANES:(j + 1) * LANES] = c_

    for idx in range(2):
        mx_ref[idx] = jnp.broadcast_to(jnp.max(mx_ref[idx], axis=-1, keepdims=True), (rows, LANES))
    for u, (which, kc) in enumerate(units):
        k0 = pl.multiple_of(kc * ch, ch)
        m_u = mx_ref[which]
        p = jnp.concatenate([jnp.exp(s_ref[u, :, j * LANES:(j + 1) * LANES] - m_u).astype(BF16)
                             for j in range(ch // LANES)], axis=1)
        acc_ref[which] += _dot(p, with_ones(vs_ref[pl.ds(k0, ch), :]))

    oa_ref[...] = _combine_heads(ga_ref[...], o_ca, finish(acc_ref[0]), o_wa, nq).astype(oa_ref.dtype)
    ob_ref[...] = _combine_heads(gb_ref[...], o_cb, finish(acc_ref[1]), o_wb, nq).astype(ob_ref.dtype)


def _nsa_prompt(q, gates, cmp_kv, kvb, *, n_seq, seq):
    nb = seq // BLK
    rows = NSA_HEADS * BLK
    ch = min(SEL_CHUNK, seq)
    wk = min(WINDOW + LANES, seq)
    bpc = ch // BLK
    assert seq % ch == 0 and wk % LANES == 0 and nb % 2 == 0 and nb <= LANES
    half = nb // 2
    n_units = nb // bpc + 1
    assert nb <= BLK
    et = np.zeros((seq // ch, ch, LANES), np.float32)
    for c in range(seq // ch):
        et[c, np.arange(ch), (c * ch + np.arange(ch)) // BLK] = 1.0
    key = np.arange(ch)[:, None]
    qry = np.arange(BLK)[None, :]
    tri = np.zeros((ch, LANES), np.float32)
    tri[:, BLK:] = (key % BLK) > qry
    n_delta = WINDOW // BLK + LANES // BLK
    wm = np.zeros((n_delta, wk, LANES), np.float32)
    for dl in range(n_delta):
        dist = dl * BLK + qry - np.arange(wk)[:, None]
        wm[dl, :, BLK:] = (dist < 0) | (dist >= WINDOW)
    chunk_a = lambda w_: pl.BlockSpec((BLK, w_), lambda b, i: (b * nb + i, 0))
    chunk_b = lambda w_: pl.BlockSpec((BLK, w_), lambda b, i: (b * nb + nb - 1 - i, 0))
    out_spec = pl.BlockSpec((BLK, NSA_WIDTH), lambda b, i: (b * half + i, 0))
    kv = lambda col: pl.BlockSpec((seq, KV_WIDTH), lambda b, i: (b, col))
    cmp_ = lambda which: pl.BlockSpec((1, nb, KV_WIDTH), lambda b, i: (which, b, 0))
    out = jax.ShapeDtypeStruct((n_seq * half * BLK, NSA_WIDTH), BF16)
    o_a, o_b = pl.pallas_call(
        functools.partial(_nsa_pair_kernel, nb=nb, ch=ch, wk=wk),
        grid=(n_seq, half),
        in_specs=[chunk_a(NSA_WIDTH), chunk_b(NSA_WIDTH), chunk_a(LANES), chunk_b(LANES),
                  cmp_(0), cmp_(1), kv(2), kv(3), kv(4), kv(5),
                  pl.BlockSpec(et.shape, lambda b, i: (0, 0, 0)),
                  pl.BlockSpec(tri.shape, lambda b, i: (0, 0)),
                  pl.BlockSpec(wm.shape, lambda b, i: (0, 0, 0))],
        out_specs=[out_spec, out_spec],
        out_shape=[out, out],
        scratch_shapes=[pltpu.VMEM((2, rows, 2 * LANES), BF16),
                        pltpu.VMEM((2, ch, LANES), BF16),
                        pltpu.VMEM((n_units, rows, ch), F32),
                        pltpu.VMEM((2, rows, LANES), F32),
                        pltpu.VMEM((2, rows, 2 * KV_WIDTH), F32)],
        compiler_params=_params("parallel", "arbitrary"),
        name="nsa_prompt",
    )(q, q, gates, gates, cmp_kv, cmp_kv, kvb, kvb, kvb, kvb,
      jnp.asarray(et, dtype=BF16), jnp.asarray(tri, dtype=BF16), jnp.asarray(wm, dtype=BF16))
    o_a = o_a.reshape(n_seq, half, BLK, NSA_WIDTH)
    o_b = o_b.reshape(n_seq, half, BLK, NSA_WIDTH)[:, ::-1]
    return jnp.concatenate([o_a, o_b], axis=1).reshape(n_seq * seq, NSA_WIDTH)


def _nsa_sample_kernel(pt_ref, q_ref, gate_ref, kc_ref, vc_ref, cache_ref, nks_ref, nvs_ref,
                       wt_ref, nkw_ref, nvw_ref, e_ref, o_ref, kv_buf, s_scr, sems,
                       *, nq, past, nbp, tk):
    b = pl.program_id(0)
    n_pages = past // PAGE_SIZE
    rows = NSA_HEADS * nq

    def page_copy(s, j):
        slot = s % 2
        k0 = pl.multiple_of(j * PAGE_SIZE, PAGE_SIZE)
        return pltpu.make_async_copy(cache_ref.at[pt_ref[s * n_pages + j], pl.ds(2, 2)],
                                     kv_buf.at[pl.ds(2 * slot, 2), :, pl.ds(k0, PAGE_SIZE)], sems.at[slot])

    def start_all(s):
        def start(j, _):
            page_copy(s, j).start()
            return 0
        lax.fori_loop(0, n_pages, start, 0, unroll=8)

    @pl.when(b == 0)
    def _():
        start_all(b)

    @pl.when(b + 1 < pl.num_programs(0))
    def _():
        start_all(b + 1)

    slot = b % 2
    q2 = _stack_queries(q_ref[0], nq)
    q_pos = past + lax.broadcasted_iota(jnp.int32, (rows, 1), 0) % nq
    cur = past // BLK

    n_ids = lax.broadcasted_iota(jnp.int32, (rows, nbp), 1)
    s_c = _dot_nt(q2, kc_ref[0].astype(BF16))
    p_c = _masked_softmax(s_c, (n_ids + 1) * BLK - 1 <= q_pos)
    o_c = _dot(p_c.astype(BF16), vc_ref[0].astype(BF16))

    n_sel = lax.broadcasted_iota(jnp.int32, (nq, nbp), 1)
    sels = []
    for g in range(NSA_KV):
        imp = p_c[g * NSA_REP * nq:(g * NSA_REP + 1) * nq]
        for r in range(1, NSA_REP):
            imp = imp + p_c[(g * NSA_REP + r) * nq:(g * NSA_REP + r + 1) * nq]
        sel = _select_blocks(imp, n_sel, cur, cur + 1)
        sels.extend([sel] * NSA_REP)
    keys = _dot(jnp.concatenate(sels, axis=0).astype(BF16), e_ref[...])
    bias = (keys - 1.0) * (-NEG_INF)

    def col_max(s):
        m = s[:, 0:LANES]
        for j in range(1, s.shape[1] // LANES):
            m = jnp.maximum(m, s[:, j * LANES:(j + 1) * LANES])
        return m

    def col_sum(p):
        t = p[:, 0:LANES]
        for j in range(1, p.shape[1] // LANES):
            t = t + p[:, j * LANES:(j + 1) * LANES]
        return t

    def row_max(mx):
        return jnp.broadcast_to(jnp.max(mx, axis=-1, keepdims=True), mx.shape)

    def tiled(m, width):
        return m if width == LANES else jnp.concatenate([m] * (width // LANES), axis=1)

    new_pos = past + lax.broadcasted_iota(jnp.int32, (rows, LANES), 1)

    def wait(j, _):
        page_copy(b, j).wait()
        return 0

    lax.fori_loop(0, n_pages, wait, 0, unroll=8)

    mx = jnp.full((rows, LANES), M_INIT, F32)
    for t in range(past // tk):
        s = _dot(q2, kv_buf[2 * slot, :, t * tk:(t + 1) * tk].astype(BF16)) + bias[:, t * tk:(t + 1) * tk]
        s_scr[:, t * tk:(t + 1) * tk] = s
        mx = jnp.maximum(mx, col_max(s))
    s_new = _dot_nt(q2, nks_ref[0].astype(BF16)) + bias[:, past:past + LANES]
    s_new = jnp.where(new_pos <= q_pos, s_new, NEG_INF)
    m_s = row_max(jnp.maximum(mx, s_new))
    p_new = jnp.exp(s_new - m_s)
    acc = _dot(p_new.astype(BF16), nvs_ref[0].astype(BF16))
    lsum = p_new
    for t in range(past // tk):
        p = jnp.exp(s_scr[:, t * tk:(t + 1) * tk] - tiled(m_s, tk))
        lsum = lsum + col_sum(p)
        acc = acc + _dot_nt(p.astype(BF16), kv_buf[2 * slot + 1, :, t * tk:(t + 1) * tk].astype(BF16))
    o_s = acc / jnp.maximum(jnp.sum(lsum, axis=-1, keepdims=True), 1e-30)

    wlen = wt_ref.shape[3]
    w_pos = past - wlen + lax.broadcasted_iota(jnp.int32, (rows, wlen), 1)
    d = q_pos - w_pos
    s_w = _dot(q2, wt_ref[0, 0].astype(BF16))
    s_w = jnp.where((d >= 0) & (d < WINDOW) & (w_pos >= 0), s_w, NEG_INF)
    d = q_pos - new_pos
    s_nw = _dot_nt(q2, nkw_ref[0].astype(BF16))
    s_nw = jnp.where((d >= 0) & (d < WINDOW), s_nw, NEG_INF)
    m_w = row_max(jnp.maximum(jnp.maximum(col_max(s_w), s_nw), M_INIT))
    p_w = jnp.exp(s_w - tiled(m_w, wlen))
    p_nw = jnp.exp(s_nw - m_w)
    acc = _dot_nt(p_w.astype(BF16), wt_ref[0, 1].astype(BF16)) + _dot(p_nw.astype(BF16), nvw_ref[0].astype(BF16))
    o_w = acc / jnp.maximum(jnp.sum(col_sum(p_w) + p_nw, axis=-1, keepdims=True), 1e-30)

    o_ref[0] = _combine_heads(gate_ref[0], o_c, o_s, o_w, nq).astype(o_ref.dtype)


def _nsa_sample(pt_flat, q3, gates3, kc, vc, cache_t, new_rows, win_t, expand, *, past, tk):
    n_seq, nq, _ = q3.shape
    nbp = kc.shape[1]
    rows = NSA_HEADS * nq
    per_seq = lambda a: pl.BlockSpec((1,) + a.shape[1:], lambda b, pt: (b,) + (0,) * (a.ndim - 1))
    new = lambda col: pl.BlockSpec((1, LANES, KV_WIDTH), lambda b, pt: (b, 0, col))
    grid_spec = pltpu.PrefetchScalarGridSpec(
        num_scalar_prefetch=1,
        grid=(n_seq,),
        in_specs=[per_seq(q3), per_seq(gates3), per_seq(kc), per_seq(vc),
                  pl.BlockSpec(memory_space=pl.ANY),
                  new(2), new(3), per_seq(win_t), new(4), new(5),
                  pl.BlockSpec(expand.shape, lambda b, pt: (0, 0))],
        out_specs=pl.BlockSpec((1, nq, NSA_WIDTH), lambda b, pt: (b, 0, 0)),
        scratch_shapes=[pltpu.VMEM((4, KV_WIDTH, past), F32),
                        pltpu.VMEM((rows, past), F32), pltpu.SemaphoreType.DMA((2,))])
    return pl.pallas_call(
        functools.partial(_nsa_sample_kernel, nq=nq, past=past, nbp=nbp, tk=tk),
        grid_spec=grid_spec,
        out_shape=jax.ShapeDtypeStruct((n_seq, nq, NSA_WIDTH), F32),
        compiler_params=_params("arbitrary"),
        name="nsa_sample",
    )(pt_flat, q3, gates3, kc, vc, cache_t, new_rows, new_rows, win_t, new_rows, new_rows, expand)


def _expand_matrix(nb, n_keys):
    return jnp.asarray(np.arange(n_keys)[None, :] // BLK == np.arange(nb)[:, None], dtype=BF16)


def _merge_kernel(x_ref, ssm_ref, nsa_ref, gpre_ref, wm_ref, wbs_ref, wbn_ref, wo_ref, gpost_ref,
                  gx_ref, wxq_ref, x1_ref, qx_ref):
    x = x_ref[...]
    a = _rms(x, gpre_ref[...]).astype(BF16)
    g_ssm = _sigmoid(_dot(a, wm_ref[:, 0:D_MODEL]))
    g_nsa = _sigmoid(_dot(a, wm_ref[:, D_MODEL:2 * D_MODEL]))
    merged = (g_ssm * _dot(ssm_ref[...].astype(BF16), wbs_ref[...])
              + g_nsa * _dot(nsa_ref[...], wbn_ref[...]))
    x1 = x + _rms(_dot(merged.astype(BF16), wo_ref[...]), gpost_ref[...])
    x1_ref[...] = x1
    c = _rms(x1, gx_ref[...]).astype(BF16)
    qx_ref[...] = (_dot(c, wxq_ref[...]) * (X_HEAD_DIM ** -0.5)).astype(BF16)


def _merge(x2d, ssm_y, nsa_o, weights, *, tm, ssm_time_major, n_tab_blocks):
    n = x2d.shape[0]
    steps = n // tm
    row = lambda w_: pl.BlockSpec((tm, w_), lambda i: (i, 0))
    full = lambda a: pl.BlockSpec(a.shape, lambda i: (0,) * a.ndim)
    if ssm_time_major:
        ssm_spec = pl.BlockSpec((tm, SSM_WIDTH), lambda i: (i % n_tab_blocks, i // n_tab_blocks))
    else:
        ssm_spec = row(SSM_WIDTH)
    return pl.pallas_call(
        _merge_kernel,
        grid=(steps,),
        in_specs=[row(D_MODEL), ssm_spec, row(NSA_WIDTH)] + [full(w) for w in weights],
        out_specs=[row(D_MODEL), row(X_WIDTH)],
        out_shape=[jax.ShapeDtypeStruct((n, D_MODEL), F32), jax.ShapeDtypeStruct((n, X_WIDTH), BF16)],
        compiler_params=_params("parallel"),
        name="merge",
    )(x2d, ssm_y, nsa_o, *weights)


def _xattn_kernel(q_ref, kv_ref, o_ref, *, m_len):
    q = q_ref[0].astype(BF16)
    outs = []
    for h in range(X_HEADS):
        cols = slice(h * X_HEAD_DIM, (h + 1) * X_HEAD_DIM)
        k = kv_ref[0, pl.ds(h, m_len, stride=2 * X_HEADS), :].astype(BF16)
        v = kv_ref[0, pl.ds(X_HEADS + h, m_len, stride=2 * X_HEADS), :].astype(BF16)
        s = _dot_nt(q[:, cols], k)
        m = jnp.max(s, axis=-1, keepdims=True)
        p = jnp.exp(s - m)
        p = p / jnp.sum(p, axis=-1, keepdims=True)
        outs.append(_dot(p.astype(BF16), v))
    o_ref[0] = jnp.concatenate(outs, axis=1).astype(o_ref.dtype)


def _xattn(q3, mem_kv_rows, *, tq):
    n_seq, t, _ = q3.shape
    m_len = mem_kv_rows.shape[1] // (2 * X_HEADS)
    return pl.pallas_call(
        functools.partial(_xattn_kernel, m_len=m_len),
        grid=(n_seq, t // tq),
        in_specs=[pl.BlockSpec((1, tq, X_WIDTH), lambda b, i: (b, i, 0)),
                  pl.BlockSpec((1, m_len * 2 * X_HEADS, X_HEAD_DIM), lambda b, i: (b, 0, 0))],
        out_specs=pl.BlockSpec((1, tq, X_WIDTH), lambda b, i: (b, i, 0)),
        out_shape=jax.ShapeDtypeStruct((n_seq, t, X_WIDTH), q3.dtype),
        compiler_params=_params("parallel", "parallel"),
        name="xattn",
    )(q3, mem_kv_rows)


def _mlp_kernel(x1_ref, o_ref, wxo_ref, gxp_ref, gm_ref, wup_ref, wdn_ref, gmp_ref, y_ref):
    x2 = x1_ref[...] + _rms(_dot(o_ref[...], wxo_ref[...]), gxp_ref[...])
    m = _rms(x2, gm_ref[...]).astype(BF16)
    hid = jnp.maximum(_dot(m, wup_ref[...]), 0.0)
    hid = (hid * hid).astype(BF16)
    y_ref[...] = x2 + _rms(_dot(hid, wdn_ref[...]), gmp_ref[...])


def _mlp(x1, o, weights, *, tm):
    n = x1.shape[0]
    row = lambda w_: pl.BlockSpec((tm, w_), lambda i: (i, 0))
    full = lambda a: pl.BlockSpec(a.shape, lambda i: (0,) * a.ndim)
    return pl.pallas_call(
        _mlp_kernel,
        grid=(n // tm,),
        in_specs=[row(D_MODEL), row(X_WIDTH)] + [full(w) for w in weights],
        out_specs=row(D_MODEL),
        out_shape=jax.ShapeDtypeStruct((n, D_MODEL), F32),
        compiler_params=_params("parallel"),
        name="mlp",
    )(x1, o, *weights)


def _memkv_kernel(m_ref, g_ref, w_ref, o_ref, *, tm):
    kv = _dot(_rms(m_ref[...], g_ref[...]).astype(BF16), w_ref[...])
    n_rows = 2 * X_HEADS
    for j in range(n_rows):
        o_ref[pl.ds(j, tm, stride=n_rows), :] = kv[:, j * X_HEAD_DIM:(j + 1) * X_HEAD_DIM]


def _memkv(mem2d, g, w, *, tm):
    n = mem2d.shape[0]
    n_rows = 2 * X_HEADS
    return pl.pallas_call(
        functools.partial(_memkv_kernel, tm=tm),
        grid=(n // tm,),
        in_specs=[pl.BlockSpec((tm, D_MODEL), lambda i: (i, 0)),
                  pl.BlockSpec(g.shape, lambda i: (0, 0)), pl.BlockSpec(w.shape, lambda i: (0, 0))],
        out_specs=pl.BlockSpec((tm * n_rows, X_HEAD_DIM), lambda i: (i, 0)),
        out_shape=jax.ShapeDtypeStruct((n * n_rows, X_HEAD_DIM), F32),
        compiler_params=_params("parallel"),
        name="memkv",
    )(mem2d, g, w)


def _row(v):
    return v.astype(F32).reshape(1, -1)


def kernel(x_prompt, x_sample, cache_nsa_kv, cache_win_kv, state_ssm, cache_mem_kv, page_table, mem_prompt, g_mix_pre, w_in, ssm_lam_re, ssm_lam_im, ssm_log_dt, ssm_b_re, ssm_b_im, ssm_c_re, ssm_c_im, ssm_d, w_glu, b_glu, cmp_pe_k, w_cmpk1, w_cmpk2, cmp_pe_v, w_cmpv1, w_cmpv2, w_br_ssm, w_br_nsa, w_out, g_mix_post, g_x_pre, g_mem, w_xq, w_xk, w_xv, w_xo, g_x_post, g_mlp_pre, w_up, w_down, g_mlp_post):
    depth = w_in.shape[0]
    n_seq_p, seq, _ = x_prompt.shape
    n_seq_s, nq, _ = x_sample.shape
    past = page_table.shape[1] * PAGE_SIZE
    assert depth == 1 and seq % BLK == 0 and nq <= SUBLANES and past % BLK == 0
    assert n_seq_p == SUBLANES and n_seq_s % SUBLANES == 0

    y_p = x_prompt.reshape(n_seq_p * seq, D_MODEL)
    y_s = x_sample.reshape(n_seq_s * nq, D_MODEL)
    l = 0

    w_proj = w_in[l, :, :N_PROJ].astype(BF16)
    w_gate = jnp.pad(w_in[l, :, N_PROJ:N_PROJ + N_GATE], ((0, 0), (0, LANES - N_GATE))).astype(BF16)
    w_merge = w_in[l, :, N_PROJ + N_GATE:].astype(BF16)
    lam_l, bm, cm = _ssm_params(ssm_lam_re[l], ssm_lam_im[l], ssm_log_dt[l], ssm_b_re[l], ssm_b_im[l],
                                ssm_c_re[l], ssm_c_im[l])
    ssm_w = (lam_l, bm, cm, _row(ssm_d[l]), w_glu[l].astype(BF16), _row(b_glu[l]))
    cmp_w = _compress_params(cmp_pe_k[l], w_cmpk1[l], w_cmpk2[l], cmp_pe_v[l], w_cmpv1[l], w_cmpv2[l])
    merge_w = (_row(g_mix_pre[l]), w_merge, w_br_ssm[l].astype(BF16), w_br_nsa[l].astype(BF16),
               w_out[l].astype(BF16), _row(g_mix_post[l]), _row(g_x_pre[l]), w_xq[l].astype(BF16))
    mlp_w = (w_xo[l].astype(BF16), _row(g_x_post[l]), _row(g_mlp_pre[l]), w_up[l].astype(BF16),
             w_down[l].astype(BF16), _row(g_mlp_post[l]))
    w_mem = jnp.concatenate([w_xk[l], w_xv[l]], axis=1).astype(BF16)

    tm_p = 512 if seq % 512 == 0 else seq
    nt_p = seq // tm_p
    tabs_p = _rope_tables(jnp.arange(seq, dtype=jnp.int32))
    u_p, q_p, kvt_p, wint_p, kvb_p, gate_p, cmp_rows_p = _proj(
        y_p, _row(g_mix_pre[l]), w_proj, w_gate, tabs_p, tm=tm_p, n_tab_blocks=nt_p, prompt=True)
    u_p = u_p.reshape(seq * n_seq_p, SSM_WIDTH)
    h0_p = jnp.zeros((1, SUBLANES, 2 * N_STATE), F32)
    tc_p = 64 if seq % 64 == 0 else seq
    ssm_p, hl_p = _ssm(u_p, h0_p, *ssm_w, n_groups=1, n_time=seq, tc=tc_p)
    ssm_p = ssm_p.reshape(seq, n_seq_p * SSM_WIDTH)

    nb_p = seq // BLK
    n_blocks_p = n_seq_p * nb_p
    cmp_p = _compress(cmp_rows_p, *cmp_w, n_blocks=n_blocks_p, tm=min(n_blocks_p, 256))
    nsa_p = _nsa_prompt(q_p, gate_p, cmp_p, kvb_p, n_seq=n_seq_p, seq=seq)

    mem_kv_p = _memkv(mem_prompt.reshape(-1, D_MODEL), _row(g_mem[l]), w_mem, tm=256)
    m_len = mem_prompt.shape[1]
    mem_kv_p3 = mem_kv_p.reshape(n_seq_p, m_len * 2 * X_HEADS, X_HEAD_DIM)

    x1_p, qx_p = _merge(y_p, ssm_p, nsa_p, merge_w, tm=tm_p, ssm_time_major=True, n_tab_blocks=nt_p)
    o_p = _xattn(qx_p.reshape(n_seq_p, seq, X_WIDTH), mem_kv_p3, tq=tm_p)
    y_p = _mlp(x1_p, o_p.reshape(-1, X_WIDTH), mlp_w, tm=256)

    n_s = n_seq_s * nq
    pos_s = past + jnp.arange(nq, dtype=jnp.int32)
    tabs_s = tuple(jnp.tile(t, (n_seq_s, 1)) for t in _rope_tables(pos_s))
    u_s, q_s, kv_s, win_s, kvb_s, gate_s = _proj(y_s, _row(g_mix_pre[l]), w_proj, w_gate, tabs_s,
                                                 tm=n_s, n_tab_blocks=1, prompt=False)
    n_grp = n_seq_s // SUBLANES
    to_tb = lambda a: a.reshape(n_grp, SUBLANES, nq, SSM_WIDTH).transpose(0, 2, 1, 3).reshape(n_s, SSM_WIDTH)
    st = state_ssm[l].astype(F32).reshape(n_seq_s, N_STATE, 2)
    h0_s = _state_lanes(st[..., 0], st[..., 1]).reshape(n_grp, SUBLANES, 2 * N_STATE)
    ssm_s, hl_s = _ssm(to_tb(u_s), h0_s, *ssm_w, n_groups=n_grp, n_time=nq, tc=nq)
    ssm_s = ssm_s.reshape(n_grp, nq, SUBLANES, SSM_WIDTH).transpose(0, 2, 1, 3).reshape(n_s, SSM_WIDTH)

    n_pages = page_table.shape[1]
    n_pool = cache_nsa_kv.shape[1]
    cache_t = jnp.transpose(cache_nsa_kv[l], (0, 2, 3, 4, 1)).reshape(n_pool, 4, KV_WIDTH, PAGE_SIZE)
    win_t = jnp.transpose(cache_win_kv[l], (0, 2, 3, 4, 1)).reshape(n_seq_s, 2, KV_WIDTH, -1)
    pt_flat = page_table.reshape(-1).astype(jnp.int32)
    nb_past = past // BLK
    pe_t, w1_t = _compress_paged_params(cmp_pe_k[l], w_cmpk1[l], cmp_pe_v[l], w_cmpv1[l])
    cmp_pages = _compress_paged(pt_flat, cache_t, pe_t, w1_t, cmp_w[2], m=min(n_seq_s * n_pages, 128))
    cmp_past = cmp_pages
    new_rows = jnp.pad(jnp.concatenate([kv_s, win_s], axis=1).reshape(n_seq_s, nq, 6 * KV_WIDTH),
                       ((0, 0), (0, LANES - nq), (0, 0)))
    cmp_new = _compress(new_rows[:, :BLK].reshape(n_seq_s * BLK, 6 * KV_WIDTH), *cmp_w,
                        n_blocks=n_seq_s, tm=n_seq_s)
    nbp = -(-(nb_past + 1) // LANES) * LANES
    cmp_s = jnp.concatenate([cmp_past.reshape(2, n_seq_s, nb_past, KV_WIDTH), cmp_new[:, :, None, :],
                             jnp.zeros((2, n_seq_s, nbp - nb_past - 1, KV_WIDTH), F32)], axis=2)
    nsa_s = _nsa_sample(pt_flat, q_s.astype(F32).reshape(n_seq_s, nq, NSA_WIDTH),
                        gate_s.reshape(n_seq_s, nq, LANES), cmp_s[0], cmp_s[1], cache_t, new_rows, win_t,
                        _expand_matrix(nbp, past + LANES), past=past, tk=min(past, 1024))

    x1_s, qx_s = _merge(y_s, ssm_s, nsa_s.reshape(n_s, NSA_WIDTH).astype(BF16), merge_w, tm=n_s,
                        ssm_time_major=False, n_tab_blocks=1)
    mem_kv_s3 = cache_mem_kv[l].reshape(n_seq_s, m_len * 2 * X_HEADS, X_HEAD_DIM)
    o_s = _xattn(qx_s.astype(F32).reshape(n_seq_s, nq, X_WIDTH), mem_kv_s3, tq=nq)
    y_s = _mlp(x1_s, o_s.reshape(-1, X_WIDTH).astype(BF16), mlp_w, tm=n_s)

    def ssm_state(hl, n_seq):
        re, im = _state_unlanes(hl.reshape(n_seq, 2 * N_STATE))
        return jnp.stack([re, im], axis=-1).reshape(1, n_seq, SSM_GROUPS, SSM_STATE, 2)

    def token_major(xt):
        n_seq, parts, _, t_len = xt.shape
        return xt.reshape(1, n_seq, parts, NSA_KV, HEAD_DIM, t_len).transpose(0, 1, 5, 2, 3, 4)

    w_keep = min(WINDOW, seq)
    win_new = win_s.reshape(n_seq_s, nq, 2, NSA_KV, HEAD_DIM).astype(cache_win_kv.dtype)
    win_sample = jnp.concatenate([cache_win_kv[l], win_new], axis=1)[:, nq:]
    return (y_p.reshape(n_seq_p, seq, D_MODEL),
            y_s.reshape(n_seq_s, nq, D_MODEL),
            token_major(kvt_p),
            kv_s.reshape(1, n_seq_s, nq, 4, NSA_KV, HEAD_DIM),
            token_major(wint_p[:, :, :, seq - w_keep:]),
            win_sample[None],
            ssm_state(hl_p, n_seq_p),
            ssm_state(hl_s, n_seq_s),
            mem_kv_p.reshape(1, n_seq_p, m_len, 2, X_HEADS, X_HEAD_DIM))
```

```python
import functools
import math

import jax
import jax.numpy as jnp
import numpy as np
from jax import lax
from jax.experimental import pallas as pl
from jax.experimental.pallas import tpu as pltpu

F32 = jnp.float32
BF16 = jnp.bfloat16

D_MODEL = 1024
SSM_WIDTH = 512
SSM_GROUP = 16
SSM_GROUPS = 32
SSM_STATE = 64
N_STATE = SSM_GROUPS * SSM_STATE
STATE_CHUNK = 512
N_CHUNKS = N_STATE // STATE_CHUNK
NSA_HEADS = 8
HEAD_DIM = 64
NSA_WIDTH = NSA_HEADS * HEAD_DIM
NSA_KV = 2
NSA_REP = NSA_HEADS // NSA_KV
KV_WIDTH = NSA_KV * HEAD_DIM
BLK = 64
N_SEL = 16
WINDOW = 512
ROT_DIM = 16
ROPE_THETA = 500000.0
PAGE_SIZE = 128
X_HEADS = 4
X_HEAD_DIM = 128
X_WIDTH = X_HEADS * X_HEAD_DIM
D_FF = 4 * D_MODEL
EPS = 1e-6
NEG_INF = -1e30
M_INIT = -1e29
FORCE_SCORE = 1e4
LANES = 128
SUBLANES = 8
VMEM_LIMIT = 56 * 1024 * 1024

N_PROJ = SSM_WIDTH + NSA_WIDTH + 6 * KV_WIDTH
N_GATE = 3 * NSA_HEADS
SEL_CHUNK = 512
PAGE_PITCH = PAGE_SIZE + SUBLANES


def _params(*sem):
    return pltpu.CompilerParams(dimension_semantics=sem, vmem_limit_bytes=VMEM_LIMIT)


def _rms(x, g):
    return x * lax.rsqrt(jnp.mean(x * x, axis=-1, keepdims=True) + EPS) * g


def _gelu(x):
    return 0.5 * x * (1.0 + jnp.tanh(math.sqrt(2.0 / math.pi) * (x + 0.044715 * (x * x * x))))


def _sigmoid(x):
    return 1.0 / (1.0 + jnp.exp(-x))


def _dot(a, b):
    return jnp.dot(a, b, preferred_element_type=F32)


def _dot_nt(a, b):
    return lax.dot_general(a, b, (((1,), (1,)), ((), ())), preferred_element_type=F32)


def _proj_kernel(x_ref, g_ref, w_ref, wg_ref, cos_ref, sp_ref, sm_ref,
                 u_ref, q_ref, kv_ref, win_ref, kvb_ref, gate_ref, *cmp_ref, token_minor):
    a = _rms(x_ref[...], g_ref[...]).astype(BF16)
    cos, sp, sm = cos_ref[...], sp_ref[...], sm_ref[...]

    def rope(blk):
        return blk * cos + pltpu.roll(blk, 8, 1) * sp + pltpu.roll(blk, LANES - 8, 1) * sm

    pairs = {}

    def lane_block(j):
        if j // 2 not in pairs:
            pairs[j // 2] = _dot(a, w_ref[:, (j // 2) * 2 * LANES:(j // 2 + 1) * 2 * LANES])
        return pairs[j // 2][:, (j % 2) * LANES:(j % 2 + 1) * LANES]

    if token_minor:
        tm = x_ref.shape[0]
        n_seq = u_ref.shape[1] // tm
        for s in range(SSM_WIDTH // LANES):
            u_ref[s, pl.ds(pl.program_id(1), tm, stride=n_seq), :] = lane_block(s)
    else:
        u_ref[...] = _dot(a, w_ref[:, 0:SSM_WIDTH])
    for j in range(NSA_WIDTH // LANES):
        q_ref[:, j * LANES:(j + 1) * LANES] = rope(lane_block(SSM_WIDTH // LANES + j)).astype(BF16)
    for j in range(6):
        blk = lane_block((SSM_WIDTH + NSA_WIDTH) // LANES + j)
        if j % 2 == 0:
            blk = rope(blk)
        out_ref, part = (kv_ref, j) if j < 4 else (win_ref, j - 4)
        if token_minor:
            out_ref[0, part] = blk.T
            if j < 2:
                cmp_ref[0][:, j * LANES:(j + 1) * LANES] = blk
        else:
            out_ref[:, part * LANES:(part + 1) * LANES] = blk
        kvb_ref[:, j * LANES:(j + 1) * LANES] = blk.astype(BF16)
    gate_ref[...] = _sigmoid(_dot(a, wg_ref[...]))


def _proj(x2d, g, w, wg, tabs, *, tm, n_tab_blocks, prompt):
    n = x2d.shape[0]
    nt = n_tab_blocks
    n_b = n // (nt * tm)
    row = lambda w_: pl.BlockSpec((tm, w_), lambda t, b: (b * nt + t, 0))
    full = lambda a: pl.BlockSpec(a.shape, lambda t, b: (0,) * a.ndim)
    tab = pl.BlockSpec((tm, LANES), lambda t, b: (t, 0))
    if prompt:
        t_len = nt * tm
        n_slabs = SSM_WIDTH // LANES
        u_shape = jax.ShapeDtypeStruct((n_slabs, t_len * n_b, LANES), F32)
        u_spec = pl.BlockSpec((n_slabs, tm * n_b, LANES), lambda t, b: (0, t, 0))
        tok_minor = lambda parts: pl.BlockSpec((1, parts, KV_WIDTH, tm), lambda t, b: (b, 0, 0, t))
        kv_specs = [tok_minor(4), tok_minor(2)]
        kv_shapes = [jax.ShapeDtypeStruct((n_b, 4, KV_WIDTH, t_len), F32),
                     jax.ShapeDtypeStruct((n_b, 2, KV_WIDTH, t_len), F32)]
        extra_specs = [row(2 * KV_WIDTH)]
        extra_shapes = [jax.ShapeDtypeStruct((n, 2 * KV_WIDTH), F32)]
    else:
        u_shape = jax.ShapeDtypeStruct((n, SSM_WIDTH), F32)
        u_spec = row(SSM_WIDTH)
        kv_specs = [row(4 * KV_WIDTH), row(2 * KV_WIDTH)]
        kv_shapes = [jax.ShapeDtypeStruct((n, 4 * KV_WIDTH), F32), jax.ShapeDtypeStruct((n, 2 * KV_WIDTH), F32)]
        extra_specs, extra_shapes = [], []
    return pl.pallas_call(
        functools.partial(_proj_kernel, token_minor=prompt),
        grid=(nt, n_b),
        in_specs=[row(D_MODEL), full(g), full(w), full(wg), tab, tab, tab],
        out_specs=[u_spec, row(NSA_WIDTH)] + kv_specs + [row(6 * KV_WIDTH), row(LANES)] + extra_specs,
        out_shape=[u_shape, jax.ShapeDtypeStruct((n, NSA_WIDTH), BF16)] + kv_shapes
                  + [jax.ShapeDtypeStruct((n, 6 * KV_WIDTH), BF16), jax.ShapeDtypeStruct((n, LANES), F32)]
                  + extra_shapes,
        compiler_params=_params("parallel", "arbitrary"),
        name="proj",
    )(x2d, g, w, wg, *tabs)


def _rope_tables(pos):
    half = ROT_DIM // 2
    freqs = ROPE_THETA ** (-jnp.arange(half, dtype=F32) / half)
    ang = pos.astype(F32)[:, None] * freqs[None, :]
    cos, sin = jnp.cos(ang), jnp.sin(ang)
    r = pos.shape[0]
    z8 = jnp.zeros((r, half), F32)
    rest0 = jnp.zeros((r, HEAD_DIM - ROT_DIM), F32)
    rest1 = jnp.ones((r, HEAD_DIM - ROT_DIM), F32)
    c64 = jnp.concatenate([cos, cos, rest1], axis=1)
    sp64 = jnp.concatenate([z8, sin, rest0], axis=1)
    sm64 = jnp.concatenate([-sin, z8, rest0], axis=1)
    return tuple(jnp.tile(t, (1, LANES // HEAD_DIM)) for t in (c64, sp64, sm64))


def _ssm_kernel(u_ref, h0_ref, lam_ref, bm_ref, cm_ref, d_ref, wglu_ref, bglu_ref,
                y_ref, hlast_ref, hs_ref, hstate_ref, *, tc):
    j = pl.program_id(1)

    @pl.when(j == 0)
    def _():
        hstate_ref[...] = h0_ref[0]

    n_slabs = SSM_WIDTH // LANES
    u = jnp.concatenate([u_ref[s] for s in range(n_slabs)], axis=1)
    ub = u.astype(BF16)
    half_in = SSM_WIDTH // 2
    chunk_lanes = 2 * STATE_CHUNK
    chunks_per_half = N_CHUNKS // 2
    ys = [None, None]
    for c in range(N_CHUNKS):
        h = c // chunks_per_half
        lo = (c % chunks_per_half) * chunk_lanes
        re0 = c * chunk_lanes
        im0 = re0 + STATE_CHUNK
        hs_ref[:, re0:re0 + chunk_lanes] = _dot(ub[:, h * half_in:(h + 1) * half_in],
                                                bm_ref[h, :, lo:lo + chunk_lanes])
        lr = jnp.broadcast_to(lam_ref[0:1, re0:re0 + STATE_CHUNK], (SUBLANES, STATE_CHUNK))
        li = jnp.broadcast_to(lam_ref[0:1, im0:im0 + STATE_CHUNK], (SUBLANES, STATE_CHUNK))
        hr = hstate_ref[:, re0:re0 + STATE_CHUNK]
        hi = hstate_ref[:, im0:im0 + STATE_CHUNK]
        for t in range(tc):
            r0 = t * SUBLANES
            hr, hi = (lr * hr - li * hi + hs_ref[r0:r0 + SUBLANES, re0:re0 + STATE_CHUNK],
                      lr * hi + li * hr + hs_ref[r0:r0 + SUBLANES, im0:im0 + STATE_CHUNK])
            hs_ref[r0:r0 + SUBLANES, re0:re0 + STATE_CHUNK] = hr
            hs_ref[r0:r0 + SUBLANES, im0:im0 + STATE_CHUNK] = hi
        hstate_ref[:, re0:re0 + STATE_CHUNK] = hr
        hstate_ref[:, im0:im0 + STATE_CHUNK] = hi
        part = _dot(hs_ref[:, re0:re0 + chunk_lanes].astype(BF16), cm_ref[h, lo:lo + chunk_lanes, :])
        ys[h] = part if ys[h] is None else ys[h] + part

    y = jnp.concatenate(ys, axis=1) + d_ref[...] * u
    y = _gelu(y)
    z = _dot(y.astype(BF16), wglu_ref[...]) + bglu_ref[...]
    out = y * _sigmoid(z)
    for s in range(n_slabs):
        y_ref[s] = out[:, s * LANES:(s + 1) * LANES]

    @pl.when(j == pl.num_programs(1) - 1)
    def _():
        hlast_ref[0] = hstate_ref[...]


def _ssm(u_tb, h0, lam, bm, cm, d, wglu, bglu, *, n_groups, n_time, tc):
    rows = tc * SUBLANES
    nt = n_time // tc
    n_slabs = SSM_WIDTH // LANES
    full = lambda a: pl.BlockSpec(a.shape, lambda g, j: (0,) * a.ndim)
    st = pl.BlockSpec((1, SUBLANES, 2 * N_STATE), lambda g, j: (g, 0, 0))
    slabs = pl.BlockSpec((n_slabs, rows, LANES), lambda g, j: (0, g * nt + j, 0))
    return pl.pallas_call(
        functools.partial(_ssm_kernel, tc=tc),
        grid=(n_groups, nt),
        in_specs=[slabs, st, full(lam), full(bm), full(cm), full(d), full(wglu), full(bglu)],
        out_specs=[slabs, st],
        out_shape=[jax.ShapeDtypeStruct((n_slabs, n_groups * n_time * SUBLANES, LANES), F32),
                   jax.ShapeDtypeStruct((n_groups, SUBLANES, 2 * N_STATE), F32)],
        scratch_shapes=[pltpu.VMEM((rows, 2 * N_STATE), F32), pltpu.VMEM((SUBLANES, 2 * N_STATE), F32)],
        compiler_params=_params("parallel", "arbitrary"),
        name="ssm",
    )(u_tb, h0, lam, bm, cm, d, wglu, bglu)


def _state_lanes(re, im):
    lead = re.shape[:-1]
    r = re.reshape(lead + (N_CHUNKS, 1, STATE_CHUNK))
    i = im.reshape(lead + (N_CHUNKS, 1, STATE_CHUNK))
    return jnp.concatenate([r, i], axis=-2).reshape(lead + (2 * N_STATE,))


def _state_unlanes(x):
    lead = x.shape[:-1]
    y = x.reshape(lead + (N_CHUNKS, 2, STATE_CHUNK))
    return y[..., 0, :].reshape(lead + (N_STATE,)), y[..., 1, :].reshape(lead + (N_STATE,))


def _ssm_params(lam_re, lam_im, log_dt, b_re, b_im, c_re, c_im):
    lam = lax.complex(lam_re.astype(F32), lam_im.astype(F32))
    dt = jnp.exp(log_dt.astype(F32))[:, None]
    lam_bar = jnp.exp(lam * dt)
    b = lax.complex(b_re.astype(F32), b_im.astype(F32))
    b_bar = ((lam_bar - 1.0) / lam)[..., None] * b
    eye = jnp.eye(SSM_GROUPS, dtype=F32)

    def in_blockdiag(x):
        return jnp.einsum('gpc,gh->gchp', x, eye).reshape(SSM_WIDTH, N_STATE)

    def out_blockdiag(x):
        return jnp.einsum('gcp,gh->gphc', x, eye).reshape(N_STATE, SSM_WIDTH)

    b_full = _state_lanes(in_blockdiag(jnp.real(b_bar)), in_blockdiag(jnp.imag(b_bar)))
    c_full = _state_lanes(out_blockdiag(c_re.astype(F32)).T, -out_blockdiag(c_im.astype(F32)).T).T
    hw, hs = SSM_WIDTH // 2, N_STATE
    bm = jnp.stack([b_full[h * hw:(h + 1) * hw, h * hs:(h + 1) * hs] for h in range(2)]).astype(BF16)
    cm = jnp.stack([c_full[h * hs:(h + 1) * hs, h * hw:(h + 1) * hw] for h in range(2)]).astype(BF16)
    lam_l = _state_lanes(jnp.real(lam_bar).reshape(1, N_STATE), jnp.imag(lam_bar).reshape(1, N_STATE))
    return lam_l, bm, cm


def _compress_kernel(x_ref, pe_ref, w1_ref, w2_ref, o_ref, *, tm):
    def body(sp, acc):
        s0 = 2 * sp
        xa = x_ref[pl.ds(s0, tm, stride=BLK), :] + pe_ref[0, pl.ds(s0, 1), :]
        xb = x_ref[pl.ds(s0 + 1, tm, stride=BLK), :] + pe_ref[0, pl.ds(s0 + 1, 1), :]
        lhs = jnp.concatenate([xa, xb], axis=1).astype(BF16)
        return acc + _dot(lhs, w1_ref[0, sp])

    acc = lax.fori_loop(0, BLK // 2, body, jnp.zeros((tm, KV_WIDTH), F32))
    o_ref[0] = _dot(_gelu(acc).astype(BF16), w2_ref[0])


def _compress(x2d, pe, w1, w2, *, n_blocks, tm):
    return pl.pallas_call(
        functools.partial(_compress_kernel, tm=tm),
        grid=(2, n_blocks // tm),
        in_specs=[pl.BlockSpec((tm * BLK, KV_WIDTH), lambda c, i: (i, c)),
                  pl.BlockSpec((1, BLK, KV_WIDTH), lambda c, i: (c, 0, 0)),
                  pl.BlockSpec((1, BLK // 2, 2 * KV_WIDTH, KV_WIDTH), lambda c, i: (c, 0, 0, 0)),
                  pl.BlockSpec((1, KV_WIDTH, KV_WIDTH), lambda c, i: (c, 0, 0))],
        out_specs=pl.BlockSpec((1, tm, KV_WIDTH), lambda c, i: (c, i, 0)),
        out_shape=jax.ShapeDtypeStruct((2, n_blocks, KV_WIDTH), F32),
        compiler_params=_params("parallel", "parallel"),
        name="compress",
    )(x2d, pe, w1, w2)


def _compress_params(pe_k, w1_k, w2_k, pe_v, w1_v, w2_v):
    def bd(w):
        z = jnp.zeros_like(w)
        return jnp.concatenate([jnp.concatenate([w, z], axis=-1), jnp.concatenate([z, w], axis=-1)], axis=-2)

    def one(pe, w1, w2):
        w1s = bd(w1.astype(F32).reshape(BLK, HEAD_DIM, HEAD_DIM))
        return (jnp.tile(pe.astype(F32), (1, NSA_KV)),
                w1s.reshape(BLK // 2, 2 * KV_WIDTH, KV_WIDTH).astype(BF16),
                bd(w2.astype(F32)).astype(BF16))

    k, v = one(pe_k, w1_k, w2_k), one(pe_v, w1_v, w2_v)
    return tuple(jnp.stack([a, b]) for a, b in zip(k, v))


def _compress_paged_kernel(pt_ref, cache_ref, pe_ref, w1_ref, w2_ref, o_ref, buf_ref, sems, *, m):
    step = pl.program_id(0)

    def page_copy(s, j):
        slot = s % 2
        row0 = pl.multiple_of(j * PAGE_PITCH, SUBLANES)
        return pltpu.make_async_copy(cache_ref.at[pt_ref[s * m + j], pl.ds(0, 2)],
                                     buf_ref.at[pl.ds(2 * slot, 2), pl.ds(row0, PAGE_SIZE), :],
                                     sems.at[slot])

    def start_all(s):
        def start(j, _):
            page_copy(s, j).start()
            return 0
        lax.fori_loop(0, m, start, 0, unroll=8)

    @pl.when(step == 0)
    def _():
        start_all(step)

    @pl.when(step + 1 < pl.num_programs(0))
    def _():
        start_all(step + 1)

    def wait(j, _):
        page_copy(step, j).wait()
        return 0

    lax.fori_loop(0, m, wait, 0, unroll=8)

    for c in range(2):
        tile = 2 * (step % 2) + c
        res = []
        for kv in range(NSA_KV):
            def body(dp, acc, c=c, kv=kv, tile=tile):
                d0 = 2 * dp
                r0 = kv * HEAD_DIM + d0
                xa = buf_ref[tile, pl.ds(r0, m, stride=PAGE_PITCH), :] + pe_ref[c, pl.ds(d0, 1), :]
                xb = buf_ref[tile, pl.ds(r0 + 1, m, stride=PAGE_PITCH), :] + pe_ref[c, pl.ds(d0 + 1, 1), :]
                lhs = jnp.concatenate([xa, xb], axis=1).astype(BF16)
                return acc + _dot(lhs, w1_ref[c, dp])

            acc = lax.fori_loop(0, HEAD_DIM // 2, body, jnp.zeros((m, PAGE_SIZE), F32), unroll=4)
            res.append(_dot(_gelu(acc).astype(BF16), w2_ref[c]))
        for blk in range(PAGE_SIZE // BLK):
            cols = slice(blk * HEAD_DIM, (blk + 1) * HEAD_DIM)
            o_ref[c, pl.ds(blk, m, stride=PAGE_SIZE // BLK), :] = jnp.concatenate(
                [r[:, cols] for r in res], axis=1)


def _compress_paged(pt_flat, cache_t, pe_t, w1_t, w2, *, m):
    n = pt_flat.shape[0]
    full = lambda a: pl.BlockSpec(a.shape, lambda i, pt: (0,) * a.ndim)
    grid_spec = pltpu.PrefetchScalarGridSpec(
        num_scalar_prefetch=1,
        grid=(n // m,),
        in_specs=[pl.BlockSpec(memory_space=pl.ANY), full(pe_t), full(w1_t), full(w2)],
        out_specs=pl.BlockSpec((2, m * (PAGE_SIZE // BLK), KV_WIDTH), lambda i, pt: (0, i, 0)),
        scratch_shapes=[pltpu.VMEM((4, m * PAGE_PITCH, PAGE_SIZE), F32), pltpu.SemaphoreType.DMA((2,))])
    return pl.pallas_call(
        functools.partial(_compress_paged_kernel, m=m),
        grid_spec=grid_spec,
        out_shape=jax.ShapeDtypeStruct((2, n * (PAGE_SIZE // BLK), KV_WIDTH), F32),
        compiler_params=_params("arbitrary"),
        name="compress_paged",
    )(pt_flat, cache_t, pe_t, w1_t, w2)


def _compress_paged_params(pe_k, w1_k, pe_v, w1_v):
    def bd(w):
        z = jnp.zeros_like(w)
        return jnp.concatenate([jnp.concatenate([w, z], axis=-1), jnp.concatenate([z, w], axis=-1)], axis=-2)

    def one(pe, w1):
        w1d = bd(w1.astype(F32).reshape(BLK, HEAD_DIM, HEAD_DIM).transpose(1, 0, 2))
        return (jnp.tile(pe.astype(F32).T, (1, PAGE_SIZE // BLK)),
                w1d.reshape(HEAD_DIM // 2, 2 * PAGE_SIZE, PAGE_SIZE).astype(BF16))

    k, v = one(pe_k, w1_k), one(pe_v, w1_v)
    return tuple(jnp.stack([a, b]) for a, b in zip(k, v))


def _stack_queries(q, nq):
    q = q.astype(F32)
    z = jnp.zeros((nq, HEAD_DIM), F32)
    rows = []
    for h in range(NSA_HEADS):
        blk = q[:, h * HEAD_DIM:(h + 1) * HEAD_DIM]
        rows.append(jnp.concatenate([blk, z] if h < NSA_REP else [z, blk], axis=1))
    return (jnp.concatenate(rows, axis=0) * (HEAD_DIM ** -0.5)).astype(BF16)


def _masked_softmax(s, valid):
    s = jnp.where(valid, s, NEG_INF)
    m = jnp.max(s, axis=-1, keepdims=True)
    p = jnp.exp(s - m) * valid.astype(F32)
    return p / jnp.maximum(jnp.sum(p, axis=-1, keepdims=True), 1e-30)


def _select_blocks(imp, n_ids, cur, nb):
    forced = (n_ids == 0) | (n_ids == cur) | (n_ids == cur - 1)
    imp = jnp.where(forced, FORCE_SCORE, imp)
    imp = jnp.where(n_ids <= cur, imp, -FORCE_SCORE)
    rank = jnp.zeros(imp.shape, F32)
    for m in range(nb):
        col = imp[:, m:m + 1]
        beats = (col > imp) | ((col == imp) & (n_ids > m))
        rank = rank + beats.astype(F32)
    return (rank < float(N_SEL)).astype(F32)


def _select_blocks_t(imp_t, cur, nb):
    n_t = lax.broadcasted_iota(jnp.int32, imp_t.shape, 0)
    forced = (n_t == 0) | (n_t == cur) | (n_t == cur - 1)
    imp_t = jnp.where(forced, FORCE_SCORE, imp_t)
    imp_t = jnp.where(n_t <= cur, imp_t, -FORCE_SCORE)
    rank = jnp.zeros(imp_t.shape, F32)
    for m in range(nb):
        row = imp_t[m:m + 1, :]
        beats = (row > imp_t) | ((row == imp_t) & (n_t > m))
        rank = rank + beats.astype(F32)
    return ((rank < float(N_SEL)) & (n_t <= cur)).astype(F32)


def _combine_heads(gates, o_c, o_s, o_w, nq):
    outs = []
    for h in range(NSA_HEADS):
        rows = slice(h * nq, (h + 1) * nq)
        o = (gates[:, 3 * h:3 * h + 1] * o_c[rows] + gates[:, 3 * h + 1:3 * h + 2] * o_s[rows]
             + gates[:, 3 * h + 2:3 * h + 3] * o_w[rows])
        g = h // NSA_REP
        outs.append(o[:, g * HEAD_DIM:(g + 1) * HEAD_DIM])
    return jnp.concatenate(outs, axis=1)


def _nsa_pair_kernel(qa_ref, qb_ref, ga_ref, gb_ref, kc_ref, vc_ref, ks_ref, vs_ref, kw_ref, vw_ref,
                     et_ref, tri_ref, wm_ref, oa_ref, ob_ref, q2_ref, etd_ref, s_ref, mx_ref, acc_ref,
                     *, nb, ch, wk):
    i = pl.program_id(1)
    nq = BLK
    rows = NSA_HEADS * nq
    bpc = ch // BLK
    n_units = nb // bpc + 1
    chunk_of = (i, nb - 1 - i)
    n_a = i // bpc + 1
    row_q = lax.broadcasted_iota(jnp.int32, (rows, LANES), 0) % nq
    rq_minus_lane = row_q - lax.broadcasted_iota(jnp.int32, (rows, LANES), 1)

    def col_blocks(s):
        return [s[:, j * LANES:(j + 1) * LANES] for j in range(s.shape[1] // LANES)]

    def col_max(cols):
        m = cols[0]
        for c_ in cols[1:]:
            m = jnp.maximum(m, c_)
        return m

    def finish(acc):
        return acc[:, 0:KV_WIDTH] / jnp.maximum(acc[:, KV_WIDTH:2 * KV_WIDTH], 1e-30)

    def with_ones(v):
        return jnp.concatenate([v, jnp.ones((v.shape[0], LANES), BF16)], axis=1)

    q_onehot = jnp.where(rq_minus_lane == -BLK, NEG_INF, 0.0)

    def prepare(idx, q_ref):
        ci = chunk_of[idx]
        q2 = _stack_queries(q_ref[...], nq)
        first = jnp.maximum(ci - WINDOW // BLK, 0) // (LANES // BLK)
        w0 = pl.multiple_of(first * LANES, LANES)
        delta = ci - first * (LANES // BLK)
        lhs = jnp.concatenate([q2, q_onehot.astype(BF16)], axis=1)
        rhs = jnp.concatenate([kw_ref[pl.ds(w0, wk), :], wm_ref[delta]], axis=1)
        cols = col_blocks(_dot_nt(lhs, rhs))
        m_w = jnp.maximum(jnp.max(col_max(cols), axis=-1, keepdims=True), M_INIT)
        m_w = jnp.broadcast_to(m_w, (rows, LANES))
        p = jnp.concatenate([jnp.exp(c_ - m_w).astype(BF16) for c_ in cols], axis=1)
        o_w = finish(_dot(p, with_ones(vw_ref[pl.ds(w0, wk), :])))

        q_pos = ci * BLK + lax.broadcasted_iota(jnp.int32, (rows, 1), 0) % nq
        n_ids = lax.broadcasted_iota(jnp.int32, (rows, nb), 1)
        s_c = _dot_nt(q2, kc_ref[0].astype(BF16))
        p_c = _masked_softmax(s_c, (n_ids + 1) * BLK - 1 <= q_pos)
        o_c = _dot(p_c.astype(BF16), vc_ref[0].astype(BF16))

        imps = []
        for g in range(NSA_KV):
            imp = p_c[g * NSA_REP * nq:(g * NSA_REP + 1) * nq]
            for r in range(1, NSA_REP):
                imp = imp + p_c[(g * NSA_REP + r) * nq:(g * NSA_REP + r + 1) * nq]
            imps.append(imp)
        imp2 = jnp.concatenate([jnp.concatenate(imps, axis=0), jnp.zeros((LANES, LANES - nb), F32)], axis=1)
        sel_t = _select_blocks_t(imp2.T[0:nb], ci, nb)
        sel2 = jnp.concatenate([sel_t, jnp.ones((LANES - nb, LANES), F32)], axis=0).T
        neg = (sel2 - 1.0) * (-NEG_INF)
        neg_rows = jnp.concatenate([neg[g * nq:(g + 1) * nq] for g in range(NSA_KV) for _ in range(NSA_REP)],
                                   axis=0)
        q2_ref[idx] = jnp.concatenate([q2, (neg_rows + q_onehot).astype(BF16)], axis=1)
        in_diag = lax.broadcasted_iota(jnp.int32, (ch, LANES), 0) // BLK == ci % bpc
        etd_ref[idx] = et_ref[ci // bpc] + jnp.where(in_diag, tri_ref[...], jnp.zeros((), BF16))
        return o_c, o_w

    o_ca, o_wa = prepare(0, qa_ref)
    o_cb, o_wb = prepare(1, qb_ref)

    mx_ref[...] = jnp.full(mx_ref.shape, M_INIT, F32)
    acc_ref[...] = jnp.zeros(acc_ref.shape, F32)
    units = []
    for u in range(n_units):
        which = (u >= n_a).astype(jnp.int32)
        kc = u - which * n_a
        units.append((which, kc))
        k0 = pl.multiple_of(kc * ch, ch)
        diag = kc == jnp.where(which == 1, chunk_of[1], chunk_of[0]) // bpc
        key_mask = jnp.where(diag, etd_ref[which], et_ref[kc])
        rhs = jnp.concatenate([ks_ref[pl.ds(k0, ch), :], key_mask], axis=1)
        cols = col_blocks(_dot_nt(q2_ref[which], rhs))
        mx_ref[which] = jnp.maximum(mx_ref[which], col_max(cols))
        for j, c_ in enumerate(cols):
            s_ref[u, :, j * LANES:(j + 1) * LANES] = c_

    for idx in range(2):
        mx_ref[idx] = jnp.broadcast_to(jnp.max(mx_ref[idx], axis=-1, keepdims=True), (rows, LANES))
    for u, (which, kc) in enumerate(units):
        k0 = pl.multiple_of(kc * ch, ch)
        m_u = mx_ref[which]
        p = jnp.concatenate([jnp.exp(s_ref[u, :, j * LANES:(j + 1) * LANES] - m_u).astype(BF16)
                             for j in range(ch // LANES)], axis=1)
        acc_ref[which] += _dot(p, with_ones(vs_ref[pl.ds(k0, ch), :]))

    oa_ref[...] = _combine_heads(ga_ref[...], o_ca, finish(acc_ref[0]), o_wa, nq).astype(oa_ref.dtype)
    ob_ref[...] = _combine_heads(gb_ref[...], o_cb, finish(acc_ref[1]), o_wb, nq).astype(ob_ref.dtype)


def _nsa_prompt(q, gates, cmp_kv, kvb, *, n_seq, seq):
    nb = seq // BLK
    rows = NSA_HEADS * BLK
    ch = min(SEL_CHUNK, seq)
    wk = min(WINDOW + LANES, seq)
    bpc = ch // BLK
    assert seq % ch == 0 and wk % LANES == 0 and nb % 2 == 0 and nb <= LANES
    half = nb // 2
    n_units = nb // bpc + 1
    assert nb <= BLK
    et = np.zeros((seq // ch, ch, LANES), np.float32)
    for c in range(seq // ch):
        et[c, np.arange(ch), (c * ch + np.arange(ch)) // BLK] = 1.0
    key = np.arange(ch)[:, None]
    qry = np.arange(BLK)[None, :]
    tri = np.zeros((ch, LANES), np.float32)
    tri[:, BLK:] = (key % BLK) > qry
    n_delta = WINDOW // BLK + LANES // BLK
    wm = np.zeros((n_delta, wk, LANES), np.float32)
    for dl in range(n_delta):
        dist = dl * BLK + qry - np.arange(wk)[:, None]
        wm[dl, :, BLK:] = (dist < 0) | (dist >= WINDOW)
    chunk_a = lambda w_: pl.BlockSpec((BLK, w_), lambda b, i: (b * nb + i, 0))
    chunk_b = lambda w_: pl.BlockSpec((BLK, w_), lambda b, i: (b * nb + nb - 1 - i, 0))
    out_spec = pl.BlockSpec((BLK, NSA_WIDTH), lambda b, i: (b * half + i, 0))
    kv = lambda col: pl.BlockSpec((seq, KV_WIDTH), lambda b, i: (b, col))
    cmp_ = lambda which: pl.BlockSpec((1, nb, KV_WIDTH), lambda b, i: (which, b, 0))
    out = jax.ShapeDtypeStruct((n_seq * half * BLK, NSA_WIDTH), BF16)
    o_a, o_b = pl.pallas_call(
        functools.partial(_nsa_pair_kernel, nb=nb, ch=ch, wk=wk),
        grid=(n_seq, half),
        in_specs=[chunk_a(NSA_WIDTH), chunk_b(NSA_WIDTH), chunk_a(LANES), chunk_b(LANES),
                  cmp_(0), cmp_(1), kv(2), kv(3), kv(4), kv(5),
                  pl.BlockSpec(et.shape, lambda b, i: (0, 0, 0)),
                  pl.BlockSpec(tri.shape, lambda b, i: (0, 0)),
                  pl.BlockSpec(wm.shape, lambda b, i: (0, 0, 0))],
        out_specs=[out_spec, out_spec],
        out_shape=[out, out],
        scratch_shapes=[pltpu.VMEM((2, rows, 2 * LANES), BF16),
                        pltpu.VMEM((2, ch, LANES), BF16),
                        pltpu.VMEM((n_units, rows, ch), F32),
                        pltpu.VMEM((2, rows, LANES), F32),
                        pltpu.VMEM((2, rows, 2 * KV_WIDTH), F32)],
        compiler_params=_params("parallel", "arbitrary"),
        name="nsa_prompt",
    )(q, q, gates, gates, cmp_kv, cmp_kv, kvb, kvb, kvb, kvb,
      jnp.asarray(et, dtype=BF16), jnp.asarray(tri, dtype=BF16), jnp.asarray(wm, dtype=BF16))
    o_a = o_a.reshape(n_seq, half, BLK, NSA_WIDTH)
    o_b = o_b.reshape(n_seq, half, BLK, NSA_WIDTH)[:, ::-1]
    return jnp.concatenate([o_a, o_b], axis=1).reshape(n_seq * seq, NSA_WIDTH)


def _nsa_sample_kernel(pt_ref, q_ref, gate_ref, kc_ref, vc_ref, cache_ref, nks_ref, nvs_ref,
                       wt_ref, nkw_ref, nvw_ref, e_ref, o_ref, kv_buf, s_scr, sems,
                       *, nq, past, nbp, tk):
    b = pl.program_id(0)
    n_pages = past // PAGE_SIZE
    rows = NSA_HEADS * nq

    def page_copy(s, j):
        slot = s % 2
        k0 = pl.multiple_of(j * PAGE_SIZE, PAGE_SIZE)
        return pltpu.make_async_copy(cache_ref.at[pt_ref[s * n_pages + j], pl.ds(2, 2)],
                                     kv_buf.at[pl.ds(2 * slot, 2), :, pl.ds(k0, PAGE_SIZE)], sems.at[slot])

    def start_all(s):
        def start(j, _):
            page_copy(s, j).start()
            return 0
        lax.fori_loop(0, n_pages, start, 0, unroll=8)

    @pl.when(b == 0)
    def _():
        start_all(b)

    @pl.when(b + 1 < pl.num_programs(0))
    def _():
        start_all(b + 1)

    slot = b % 2
    q2 = _stack_queries(q_ref[0], nq)
    q_pos = past + lax.broadcasted_iota(jnp.int32, (rows, 1), 0) % nq
    cur = past // BLK

    n_ids = lax.broadcasted_iota(jnp.int32, (rows, nbp), 1)
    s_c = _dot_nt(q2, kc_ref[0].astype(BF16))
    p_c = _masked_softmax(s_c, (n_ids + 1) * BLK - 1 <= q_pos)
    o_c = _dot(p_c.astype(BF16), vc_ref[0].astype(BF16))

    n_sel = lax.broadcasted_iota(jnp.int32, (nq, nbp), 1)
    sels = []
    for g in range(NSA_KV):
        imp = p_c[g * NSA_REP * nq:(g * NSA_REP + 1) * nq]
        for r in range(1, NSA_REP):
            imp = imp + p_c[(g * NSA_REP + r) * nq:(g * NSA_REP + r + 1) * nq]
        sel = _select_blocks(imp, n_sel, cur, cur + 1)
        sels.extend([sel] * NSA_REP)
    keys = _dot(jnp.concatenate(sels, axis=0).astype(BF16), e_ref[...])
    bias = (keys - 1.0) * (-NEG_INF)

    def col_max(s):
        m = s[:, 0:LANES]
        for j in range(1, s.shape[1] // LANES):
            m = jnp.maximum(m, s[:, j * LANES:(j + 1) * LANES])
        return m

    def col_sum(p):
        t = p[:, 0:LANES]
        for j in range(1, p.shape[1] // LANES):
            t = t + p[:, j * LANES:(j + 1) * LANES]
        return t

    def row_max(mx):
        return jnp.broadcast_to(jnp.max(mx, axis=-1, keepdims=True), mx.shape)

    def tiled(m, width):
        return m if width == LANES else jnp.concatenate([m] * (width // LANES), axis=1)

    new_pos = past + lax.broadcasted_iota(jnp.int32, (rows, LANES), 1)

    def wait(j, _):
        page_copy(b, j).wait()
        return 0

    lax.fori_loop(0, n_pages, wait, 0, unroll=8)

    mx = jnp.full((rows, LANES), M_INIT, F32)
    for t in range(past // tk):
        s = _dot(q2, kv_buf[2 * slot, :, t * tk:(t + 1) * tk].astype(BF16)) + bias[:, t * tk:(t + 1) * tk]
        s_scr[:, t * tk:(t + 1) * tk] = s
        mx = jnp.maximum(mx, col_max(s))
    s_new = _dot_nt(q2, nks_ref[0].astype(BF16)) + bias[:, past:past + LANES]
    s_new = jnp.where(new_pos <= q_pos, s_new, NEG_INF)
    m_s = row_max(jnp.maximum(mx, s_new))
    p_new = jnp.exp(s_new - m_s)
    acc = _dot(p_new.astype(BF16), nvs_ref[0].astype(BF16))
    lsum = p_new
    for t in range(past // tk):
        p = jnp.exp(s_scr[:, t * tk:(t + 1) * tk] - tiled(m_s, tk))
        lsum = lsum + col_sum(p)
        acc = acc + _dot_nt(p.astype(BF16), kv_buf[2 * slot + 1, :, t * tk:(t + 1) * tk].astype(BF16))
    o_s = acc / jnp.maximum(jnp.sum(lsum, axis=-1, keepdims=True), 1e-30)

    wlen = wt_ref.shape[3]
    w_pos = past - wlen + lax.broadcasted_iota(jnp.int32, (rows, wlen), 1)
    d = q_pos - w_pos
    s_w = _dot(q2, wt_ref[0, 0].astype(BF16))
    s_w = jnp.where((d >= 0) & (d < WINDOW) & (w_pos >= 0), s_w, NEG_INF)
    d = q_pos - new_pos
    s_nw = _dot_nt(q2, nkw_ref[0].astype(BF16))
    s_nw = jnp.where((d >= 0) & (d < WINDOW), s_nw, NEG_INF)
    m_w = row_max(jnp.maximum(jnp.maximum(col_max(s_w), s_nw), M_INIT))
    p_w = jnp.exp(s_w - tiled(m_w, wlen))
    p_nw = jnp.exp(s_nw - m_w)
    acc = _dot_nt(p_w.astype(BF16), wt_ref[0, 1].astype(BF16)) + _dot(p_nw.astype(BF16), nvw_ref[0].astype(BF16))
    o_w = acc / jnp.maximum(jnp.sum(col_sum(p_w) + p_nw, axis=-1, keepdims=True), 1e-30)

    o_ref[0] = _combine_heads(gate_ref[0], o_c, o_s, o_w, nq).astype(o_ref.dtype)


def _nsa_sample(pt_flat, q3, gates3, kc, vc, cache_t, new_rows, win_t, expand, *, past, tk):
    n_seq, nq, _ = q3.shape
    nbp = kc.shape[1]
    rows = NSA_HEADS * nq
    per_seq = lambda a: pl.BlockSpec((1,) + a.shape[1:], lambda b, pt: (b,) + (0,) * (a.ndim - 1))
    new = lambda col: pl.BlockSpec((1, LANES, KV_WIDTH), lambda b, pt: (b, 0, col))
    grid_spec = pltpu.PrefetchScalarGridSpec(
        num_scalar_prefetch=1,
        grid=(n_seq,),
        in_specs=[per_seq(q3), per_seq(gates3), per_seq(kc), per_seq(vc),
                  pl.BlockSpec(memory_space=pl.ANY),
                  new(2), new(3), per_seq(win_t), new(4), new(5),
                  pl.BlockSpec(expand.shape, lambda b, pt: (0, 0))],
        out_specs=pl.BlockSpec((1, nq, NSA_WIDTH), lambda b, pt: (b, 0, 0)),
        scratch_shapes=[pltpu.VMEM((4, KV_WIDTH, past), F32),
                        pltpu.VMEM((rows, past), F32), pltpu.SemaphoreType.DMA((2,))])
    return pl.pallas_call(
        functools.partial(_nsa_sample_kernel, nq=nq, past=past, nbp=nbp, tk=tk),
        grid_spec=grid_spec,
        out_shape=jax.ShapeDtypeStruct((n_seq, nq, NSA_WIDTH), F32),
        compiler_params=_params("arbitrary"),
        name="nsa_sample",
    )(pt_flat, q3, gates3, kc, vc, cache_t, new_rows, new_rows, win_t, new_rows, new_rows, expand)


def _expand_matrix(nb, n_keys):
    return jnp.asarray(np.arange(n_keys)[None, :] // BLK == np.arange(nb)[:, None], dtype=BF16)


def _merge_kernel(x_ref, ssm_ref, nsa_ref, gpre_ref, wm_ref, wbs_ref, wbn_ref, wo_ref, gpost_ref,
                  gx_ref, wxq_ref, x1_ref, qx_ref):
    x = x_ref[...]
    tm = x.shape[0]
    n_seq = ssm_ref.shape[1] // tm
    ssm = jnp.concatenate([ssm_ref[s, pl.ds(pl.program_id(1), tm, stride=n_seq), :]
                           for s in range(SSM_WIDTH // LANES)], axis=1)
    a = _rms(x, gpre_ref[...]).astype(BF16)
    g_ssm = _sigmoid(_dot(a, wm_ref[:, 0:D_MODEL]))
    g_nsa = _sigmoid(_dot(a, wm_ref[:, D_MODEL:2 * D_MODEL]))
    merged = (g_ssm * _dot(ssm.astype(BF16), wbs_ref[...])
              + g_nsa * _dot(nsa_ref[...], wbn_ref[...]))
    x1 = x + _rms(_dot(merged.astype(BF16), wo_ref[...]), gpost_ref[...])
    x1_ref[...] = x1
    c = _rms(x1, gx_ref[...]).astype(BF16)
    qx_ref[...] = (_dot(c, wxq_ref[...]) * (X_HEAD_DIM ** -0.5)).astype(BF16)


def _merge(x2d, ssm_slabs, nsa_o, weights, *, tm, n_tiles):
    n = x2d.shape[0]
    n_b = n // (n_tiles * tm)
    row = lambda w_: pl.BlockSpec((tm, w_), lambda t, b: (b * n_tiles + t, 0))
    full = lambda a: pl.BlockSpec(a.shape, lambda t, b: (0,) * a.ndim)
    ssm_spec = pl.BlockSpec((SSM_WIDTH // LANES, tm * n_b, LANES), lambda t, b: (0, t, 0))
    return pl.pallas_call(
        _merge_kernel,
        grid=(n_tiles, n_b),
        in_specs=[row(D_MODEL), ssm_spec, row(NSA_WIDTH)] + [full(w) for w in weights],
        out_specs=[row(D_MODEL), row(X_WIDTH)],
        out_shape=[jax.ShapeDtypeStruct((n, D_MODEL), F32), jax.ShapeDtypeStruct((n, X_WIDTH), BF16)],
        compiler_params=_params("parallel", "arbitrary"),
        name="merge",
    )(x2d, ssm_slabs, nsa_o, *weights)


def _xattn_kernel(q_ref, kv_ref, o_ref, *, m_len):
    q = q_ref[0].astype(BF16)
    outs = []
    for h in range(X_HEADS):
        cols = slice(h * X_HEAD_DIM, (h + 1) * X_HEAD_DIM)
        k = kv_ref[0, pl.ds(h, m_len, stride=2 * X_HEADS), :].astype(BF16)
        v = kv_ref[0, pl.ds(X_HEADS + h, m_len, stride=2 * X_HEADS), :].astype(BF16)
        s = _dot_nt(q[:, cols], k)
        m = jnp.max(s, axis=-1, keepdims=True)
        p = jnp.exp(s - m)
        p = p / jnp.sum(p, axis=-1, keepdims=True)
        outs.append(_dot(p.astype(BF16), v))
    o_ref[0] = jnp.concatenate(outs, axis=1).astype(o_ref.dtype)


def _xattn(q3, mem_kv_rows, *, tq):
    n_seq, t, _ = q3.shape
    m_len = mem_kv_rows.shape[1] // (2 * X_HEADS)
    return pl.pallas_call(
        functools.partial(_xattn_kernel, m_len=m_len),
        grid=(n_seq, t // tq),
        in_specs=[pl.BlockSpec((1, tq, X_WIDTH), lambda b, i: (b, i, 0)),
                  pl.BlockSpec((1, m_len * 2 * X_HEADS, X_HEAD_DIM), lambda b, i: (b, 0, 0))],
        out_specs=pl.BlockSpec((1, tq, X_WIDTH), lambda b, i: (b, i, 0)),
        out_shape=jax.ShapeDtypeStruct((n_seq, t, X_WIDTH), q3.dtype),
        compiler_params=_params("parallel", "parallel"),
        name="xattn",
    )(q3, mem_kv_rows)


def _mlp_kernel(x1_ref, o_ref, wxo_ref, gxp_ref, gm_ref, wup_ref, wdn_ref, gmp_ref, y_ref):
    x2 = x1_ref[...] + _rms(_dot(o_ref[...], wxo_ref[...]), gxp_ref[...])
    m = _rms(x2, gm_ref[...]).astype(BF16)
    hid = jnp.maximum(_dot(m, wup_ref[...]), 0.0)
    hid = (hid * hid).astype(BF16)
    y_ref[...] = x2 + _rms(_dot(hid, wdn_ref[...]), gmp_ref[...])


def _mlp(x1, o, weights, *, tm):
    n = x1.shape[0]
    row = lambda w_: pl.BlockSpec((tm, w_), lambda i: (i, 0))
    full = lambda a: pl.BlockSpec(a.shape, lambda i: (0,) * a.ndim)
    return pl.pallas_call(
        _mlp_kernel,
        grid=(n // tm,),
        in_specs=[row(D_MODEL), row(X_WIDTH)] + [full(w) for w in weights],
        out_specs=row(D_MODEL),
        out_shape=jax.ShapeDtypeStruct((n, D_MODEL), F32),
        compiler_params=_params("parallel"),
        name="mlp",
    )(x1, o, *weights)


def _memkv_kernel(m_ref, g_ref, w_ref, o_ref, *, tm):
    kv = _dot(_rms(m_ref[...], g_ref[...]).astype(BF16), w_ref[...])
    n_rows = 2 * X_HEADS
    for j in range(n_rows):
        o_ref[pl.ds(j, tm, stride=n_rows), :] = kv[:, j * X_HEAD_DIM:(j + 1) * X_HEAD_DIM]


def _memkv(mem2d, g, w, *, tm):
    n = mem2d.shape[0]
    n_rows = 2 * X_HEADS
    return pl.pallas_call(
        functools.partial(_memkv_kernel, tm=tm),
        grid=(n // tm,),
        in_specs=[pl.BlockSpec((tm, D_MODEL), lambda i: (i, 0)),
                  pl.BlockSpec(g.shape, lambda i: (0, 0)), pl.BlockSpec(w.shape, lambda i: (0, 0))],
        out_specs=pl.BlockSpec((tm * n_rows, X_HEAD_DIM), lambda i: (i, 0)),
        out_shape=jax.ShapeDtypeStruct((n * n_rows, X_HEAD_DIM), F32),
        compiler_params=_params("parallel"),
        name="memkv",
    )(mem2d, g, w)


def _row(v):
    return v.astype(F32).reshape(1, -1)


def kernel(x_prompt, x_sample, cache_nsa_kv, cache_win_kv, state_ssm, cache_mem_kv, page_table, mem_prompt, g_mix_pre, w_in, ssm_lam_re, ssm_lam_im, ssm_log_dt, ssm_b_re, ssm_b_im, ssm_c_re, ssm_c_im, ssm_d, w_glu, b_glu, cmp_pe_k, w_cmpk1, w_cmpk2, cmp_pe_v, w_cmpv1, w_cmpv2, w_br_ssm, w_br_nsa, w_out, g_mix_post, g_x_pre, g_mem, w_xq, w_xk, w_xv, w_xo, g_x_post, g_mlp_pre, w_up, w_down, g_mlp_post):
    depth = w_in.shape[0]
    n_seq_p, seq, _ = x_prompt.shape
    n_seq_s, nq, _ = x_sample.shape
    past = page_table.shape[1] * PAGE_SIZE
    assert depth == 1 and seq % BLK == 0 and nq <= SUBLANES and past % BLK == 0
    assert n_seq_p == SUBLANES and n_seq_s % SUBLANES == 0

    y_p = x_prompt.reshape(n_seq_p * seq, D_MODEL)
    y_s = x_sample.reshape(n_seq_s * nq, D_MODEL)
    l = 0

    w_proj = w_in[l, :, :N_PROJ].astype(BF16)
    w_gate = jnp.pad(w_in[l, :, N_PROJ:N_PROJ + N_GATE], ((0, 0), (0, LANES - N_GATE))).astype(BF16)
    w_merge = w_in[l, :, N_PROJ + N_GATE:].astype(BF16)
    lam_l, bm, cm = _ssm_params(ssm_lam_re[l], ssm_lam_im[l], ssm_log_dt[l], ssm_b_re[l], ssm_b_im[l],
                                ssm_c_re[l], ssm_c_im[l])
    ssm_w = (lam_l, bm, cm, _row(ssm_d[l]), w_glu[l].astype(BF16), _row(b_glu[l]))
    cmp_w = _compress_params(cmp_pe_k[l], w_cmpk1[l], w_cmpk2[l], cmp_pe_v[l], w_cmpv1[l], w_cmpv2[l])
    merge_w = (_row(g_mix_pre[l]), w_merge, w_br_ssm[l].astype(BF16), w_br_nsa[l].astype(BF16),
               w_out[l].astype(BF16), _row(g_mix_post[l]), _row(g_x_pre[l]), w_xq[l].astype(BF16))
    mlp_w = (w_xo[l].astype(BF16), _row(g_x_post[l]), _row(g_mlp_pre[l]), w_up[l].astype(BF16),
             w_down[l].astype(BF16), _row(g_mlp_post[l]))
    w_mem = jnp.concatenate([w_xk[l], w_xv[l]], axis=1).astype(BF16)

    tm_p = 512 if seq % 512 == 0 else seq
    nt_p = seq // tm_p
    tabs_p = _rope_tables(jnp.arange(seq, dtype=jnp.int32))
    u_p, q_p, kvt_p, wint_p, kvb_p, gate_p, cmp_rows_p = _proj(
        y_p, _row(g_mix_pre[l]), w_proj, w_gate, tabs_p, tm=tm_p, n_tab_blocks=nt_p, prompt=True)
    h0_p = jnp.zeros((1, SUBLANES, 2 * N_STATE), F32)
    tc_p = 64 if seq % 64 == 0 else seq
    ssm_p, hl_p = _ssm(u_p, h0_p, *ssm_w, n_groups=1, n_time=seq, tc=tc_p)

    nb_p = seq // BLK
    n_blocks_p = n_seq_p * nb_p
    cmp_p = _compress(cmp_rows_p, *cmp_w, n_blocks=n_blocks_p, tm=min(n_blocks_p, 256))
    nsa_p = _nsa_prompt(q_p, gate_p, cmp_p, kvb_p, n_seq=n_seq_p, seq=seq)

    mem_kv_p = _memkv(mem_prompt.reshape(-1, D_MODEL), _row(g_mem[l]), w_mem, tm=256)
    m_len = mem_prompt.shape[1]
    mem_kv_p3 = mem_kv_p.reshape(n_seq_p, m_len * 2 * X_HEADS, X_HEAD_DIM)

    tm_m = 256 if seq % 256 == 0 else seq
    x1_p, qx_p = _merge(y_p, ssm_p, nsa_p, merge_w, tm=tm_m, n_tiles=seq // tm_m)
    o_p = _xattn(qx_p.reshape(n_seq_p, seq, X_WIDTH), mem_kv_p3, tq=tm_p)
    y_p = _mlp(x1_p, o_p.reshape(-1, X_WIDTH), mlp_w, tm=256)

    n_s = n_seq_s * nq
    pos_s = past + jnp.arange(nq, dtype=jnp.int32)
    tabs_s = tuple(jnp.tile(t, (n_seq_s, 1)) for t in _rope_tables(pos_s))
    u_s, q_s, kv_s, win_s, kvb_s, gate_s = _proj(y_s, _row(g_mix_pre[l]), w_proj, w_gate, tabs_s,
                                                 tm=n_s, n_tab_blocks=1, prompt=False)
    n_grp = n_seq_s // SUBLANES
    n_slabs = SSM_WIDTH // LANES
    u_s = (u_s.reshape(n_grp, SUBLANES, nq, n_slabs, LANES).transpose(3, 0, 2, 1, 4)
           .reshape(n_slabs, n_s, LANES))
    st = state_ssm[l].astype(F32).reshape(n_seq_s, N_STATE, 2)
    h0_s = _state_lanes(st[..., 0], st[..., 1]).reshape(n_grp, SUBLANES, 2 * N_STATE)
    ssm_s, hl_s = _ssm(u_s, h0_s, *ssm_w, n_groups=n_grp, n_time=nq, tc=nq)
    ssm_s = (ssm_s.reshape(n_slabs, n_grp, nq, SUBLANES, LANES).transpose(0, 1, 3, 2, 4)
             .reshape(n_slabs, n_s, LANES))

    n_pages = page_table.shape[1]
    n_pool = cache_nsa_kv.shape[1]
    cache_t = jnp.transpose(cache_nsa_kv[l], (0, 2, 3, 4, 1)).reshape(n_pool, 4, KV_WIDTH, PAGE_SIZE)
    win_t = jnp.transpose(cache_win_kv[l], (0, 2, 3, 4, 1)).reshape(n_seq_s, 2, KV_WIDTH, -1)
    pt_flat = page_table.reshape(-1).astype(jnp.int32)
    nb_past = past // BLK
    pe_t, w1_t = _compress_paged_params(cmp_pe_k[l], w_cmpk1[l], cmp_pe_v[l], w_cmpv1[l])
    cmp_pages = _compress_paged(pt_flat, cache_t, pe_t, w1_t, cmp_w[2], m=min(n_seq_s * n_pages, 128))
    cmp_past = cmp_pages
    new_rows = jnp.pad(jnp.concatenate([kv_s, win_s], axis=1).reshape(n_seq_s, nq, 6 * KV_WIDTH),
                       ((0, 0), (0, LANES - nq), (0, 0)))
    cmp_new = _compress(new_rows[:, :BLK].reshape(n_seq_s * BLK, 6 * KV_WIDTH), *cmp_w,
                        n_blocks=n_seq_s, tm=n_seq_s)
    nbp = -(-(nb_past + 1) // LANES) * LANES
    cmp_s = jnp.concatenate([cmp_past.reshape(2, n_seq_s, nb_past, KV_WIDTH), cmp_new[:, :, None, :],
                             jnp.zeros((2, n_seq_s, nbp - nb_past - 1, KV_WIDTH), F32)], axis=2)
    nsa_s = _nsa_sample(pt_flat, q_s.astype(F32).reshape(n_seq_s, nq, NSA_WIDTH),
                        gate_s.reshape(n_seq_s, nq, LANES), cmp_s[0], cmp_s[1], cache_t, new_rows, win_t,
                        _expand_matrix(nbp, past + LANES), past=past, tk=min(past, 1024))

    x1_s, qx_s = _merge(y_s, ssm_s, nsa_s.reshape(n_s, NSA_WIDTH).astype(BF16), merge_w, tm=n_s, n_tiles=1)
    mem_kv_s3 = cache_mem_kv[l].reshape(n_seq_s, m_len * 2 * X_HEADS, X_HEAD_DIM)
    o_s = _xattn(qx_s.astype(F32).reshape(n_seq_s, nq, X_WIDTH), mem_kv_s3, tq=nq)
    y_s = _mlp(x1_s, o_s.reshape(-1, X_WIDTH).astype(BF16), mlp_w, tm=n_s)

    def ssm_state(hl, n_seq):
        re, im = _state_unlanes(hl.reshape(n_seq, 2 * N_STATE))
        return jnp.stack([re, im], axis=-1).reshape(1, n_seq, SSM_GROUPS, SSM_STATE, 2)

    def token_major(xt):
        n_seq, parts, _, t_len = xt.shape
        return xt.reshape(1, n_seq, parts, NSA_KV, HEAD_DIM, t_len).transpose(0, 1, 5, 2, 3, 4)

    w_keep = min(WINDOW, seq)
    win_new = win_s.reshape(n_seq_s, nq, 2, NSA_KV, HEAD_DIM).astype(cache_win_kv.dtype)
    win_sample = jnp.concatenate([cache_win_kv[l], win_new], axis=1)[:, nq:]
    return (y_p.reshape(n_seq_p, seq, D_MODEL),
            y_s.reshape(n_seq_s, nq, D_MODEL),
            token_major(kvt_p),
            kv_s.reshape(1, n_seq_s, nq, 4, NSA_KV, HEAD_DIM),
            token_major(wint_p[:, :, :, seq - w_keep:]),
            win_sample[None],
            ssm_state(hl_p, n_seq_p),
            ssm_state(hl_s, n_seq_s),
            mem_kv_p.reshape(1, n_seq_p, m_len, 2, X_HEADS, X_HEAD_DIM))
```

```python
import functools
import math

import jax
import jax.numpy as jnp
import numpy as np
from jax import lax
from jax.experimental import pallas as pl
from jax.experimental.pallas import tpu as pltpu

F32 = jnp.float32
BF16 = jnp.bfloat16

D_MODEL = 1024
SSM_WIDTH = 512
SSM_GROUP = 16
SSM_GROUPS = 32
SSM_STATE = 64
N_STATE = SSM_GROUPS * SSM_STATE
STATE_CHUNK = 512
N_CHUNKS = N_STATE // STATE_CHUNK
NSA_HEADS = 8
HEAD_DIM = 64
NSA_WIDTH = NSA_HEADS * HEAD_DIM
NSA_KV = 2
NSA_REP = NSA_HEADS // NSA_KV
KV_WIDTH = NSA_KV * HEAD_DIM
BLK = 64
N_SEL = 16
WINDOW = 512
ROT_DIM = 16
ROPE_THETA = 500000.0
PAGE_SIZE = 128
X_HEADS = 4
X_HEAD_DIM = 128
X_WIDTH = X_HEADS * X_HEAD_DIM
D_FF = 4 * D_MODEL
EPS = 1e-6
NEG_INF = -1e30
M_INIT = -1e29
FORCE_SCORE = 1e4
LANES = 128
SUBLANES = 8
VMEM_LIMIT = 56 * 1024 * 1024

N_PROJ = SSM_WIDTH + NSA_WIDTH + 6 * KV_WIDTH
N_GATE = 3 * NSA_HEADS
SEL_CHUNK = 512
PAGE_PITCH = PAGE_SIZE + SUBLANES


def _params(*sem):
    return pltpu.CompilerParams(dimension_semantics=sem, vmem_limit_bytes=VMEM_LIMIT)


def _rms(x, g):
    return x * lax.rsqrt(jnp.mean(x * x, axis=-1, keepdims=True) + EPS) * g


def _gelu(x):
    return 0.5 * x * (1.0 + jnp.tanh(math.sqrt(2.0 / math.pi) * (x + 0.044715 * (x * x * x))))


def _sigmoid(x):
    return 1.0 / (1.0 + jnp.exp(-x))


def _dot(a, b):
    return jnp.dot(a, b, preferred_element_type=F32)


def _dot_nt(a, b):
    return lax.dot_general(a, b, (((1,), (1,)), ((), ())), preferred_element_type=F32)


def _proj_kernel(x_ref, g_ref, w_ref, wg_ref, cos_ref, sp_ref, sm_ref,
                 u_ref, q_ref, kv_ref, win_ref, kvb_ref, gate_ref, *cmp_ref, token_minor):
    a = _rms(x_ref[...], g_ref[...]).astype(BF16)
    cos, sp, sm = cos_ref[...], sp_ref[...], sm_ref[...]

    def rope(blk):
        return blk * cos + pltpu.roll(blk, 8, 1) * sp + pltpu.roll(blk, LANES - 8, 1) * sm

    pairs = {}

    def lane_block(j):
        if j // 2 not in pairs:
            pairs[j // 2] = _dot(a, w_ref[:, (j // 2) * 2 * LANES:(j // 2 + 1) * 2 * LANES])
        return pairs[j // 2][:, (j % 2) * LANES:(j % 2 + 1) * LANES]

    if token_minor:
        tm = x_ref.shape[0]
        n_seq = u_ref.shape[1] // tm
        for s in range(SSM_WIDTH // LANES):
            u_ref[s, pl.ds(pl.program_id(1), tm, stride=n_seq), :] = lane_block(s)
    else:
        u_ref[...] = _dot(a, w_ref[:, 0:SSM_WIDTH])
    for j in range(NSA_WIDTH // LANES):
        q_ref[:, j * LANES:(j + 1) * LANES] = rope(lane_block(SSM_WIDTH // LANES + j)).astype(BF16)
    for j in range(6):
        blk = lane_block((SSM_WIDTH + NSA_WIDTH) // LANES + j)
        if j % 2 == 0:
            blk = rope(blk)
        out_ref, part = (kv_ref, j) if j < 4 else (win_ref, j - 4)
        if token_minor:
            out_ref[0, part] = blk.T
            if j < 2:
                cmp_ref[0][:, j * LANES:(j + 1) * LANES] = blk
        else:
            out_ref[:, part * LANES:(part + 1) * LANES] = blk
        kvb_ref[:, j * LANES:(j + 1) * LANES] = blk.astype(BF16)
    gate_ref[...] = _sigmoid(_dot(a, wg_ref[...]))


def _proj(x2d, g, w, wg, tabs, *, tm, n_tab_blocks, prompt):
    n = x2d.shape[0]
    nt = n_tab_blocks
    n_b = n // (nt * tm)
    row = lambda w_: pl.BlockSpec((tm, w_), lambda t, b: (b * nt + t, 0))
    full = lambda a: pl.BlockSpec(a.shape, lambda t, b: (0,) * a.ndim)
    tab = pl.BlockSpec((tm, LANES), lambda t, b: (t, 0))
    if prompt:
        t_len = nt * tm
        n_slabs = SSM_WIDTH // LANES
        u_shape = jax.ShapeDtypeStruct((n_slabs, t_len * n_b, LANES), F32)
        u_spec = pl.BlockSpec((n_slabs, tm * n_b, LANES), lambda t, b: (0, t, 0))
        tok_minor = lambda parts: pl.BlockSpec((1, parts, KV_WIDTH, tm), lambda t, b: (b, 0, 0, t))
        kv_specs = [tok_minor(4), tok_minor(2)]
        kv_shapes = [jax.ShapeDtypeStruct((n_b, 4, KV_WIDTH, t_len), F32),
                     jax.ShapeDtypeStruct((n_b, 2, KV_WIDTH, t_len), F32)]
        extra_specs = [row(2 * KV_WIDTH)]
        extra_shapes = [jax.ShapeDtypeStruct((n, 2 * KV_WIDTH), F32)]
    else:
        u_shape = jax.ShapeDtypeStruct((n, SSM_WIDTH), F32)
        u_spec = row(SSM_WIDTH)
        kv_specs = [row(4 * KV_WIDTH), row(2 * KV_WIDTH)]
        kv_shapes = [jax.ShapeDtypeStruct((n, 4 * KV_WIDTH), F32), jax.ShapeDtypeStruct((n, 2 * KV_WIDTH), F32)]
        extra_specs, extra_shapes = [], []
    return pl.pallas_call(
        functools.partial(_proj_kernel, token_minor=prompt),
        grid=(nt, n_b),
        in_specs=[row(D_MODEL), full(g), full(w), full(wg), tab, tab, tab],
        out_specs=[u_spec, row(NSA_WIDTH)] + kv_specs + [row(6 * KV_WIDTH), row(LANES)] + extra_specs,
        out_shape=[u_shape, jax.ShapeDtypeStruct((n, NSA_WIDTH), BF16)] + kv_shapes
                  + [jax.ShapeDtypeStruct((n, 6 * KV_WIDTH), BF16), jax.ShapeDtypeStruct((n, LANES), F32)]
                  + extra_shapes,
        compiler_params=_params("parallel", "arbitrary"),
        name="proj",
    )(x2d, g, w, wg, *tabs)


def _rope_tables(pos):
    half = ROT_DIM // 2
    freqs = ROPE_THETA ** (-jnp.arange(half, dtype=F32) / half)
    ang = pos.astype(F32)[:, None] * freqs[None, :]
    cos, sin = jnp.cos(ang), jnp.sin(ang)
    r = pos.shape[0]
    z8 = jnp.zeros((r, half), F32)
    rest0 = jnp.zeros((r, HEAD_DIM - ROT_DIM), F32)
    rest1 = jnp.ones((r, HEAD_DIM - ROT_DIM), F32)
    c64 = jnp.concatenate([cos, cos, rest1], axis=1)
    sp64 = jnp.concatenate([z8, sin, rest0], axis=1)
    sm64 = jnp.concatenate([-sin, z8, rest0], axis=1)
    return tuple(jnp.tile(t, (1, LANES // HEAD_DIM)) for t in (c64, sp64, sm64))


def _ssm_kernel(u_ref, h0_ref, lam_ref, bm_ref, cm_ref, d_ref, wglu_ref, bglu_ref,
                y_ref, hlast_ref, hs_ref, hstate_ref, *, tc):
    j = pl.program_id(1)

    @pl.when(j == 0)
    def _():
        hstate_ref[...] = h0_ref[0]

    n_slabs = SSM_WIDTH // LANES
    u = jnp.concatenate([u_ref[s] for s in range(n_slabs)], axis=1)
    ub = u.astype(BF16)
    half_in = SSM_WIDTH // 2
    chunk_lanes = 2 * STATE_CHUNK
    chunks_per_half = N_CHUNKS // 2
    ys = [None, None]
    for c in range(N_CHUNKS):
        h = c // chunks_per_half
        lo = (c % chunks_per_half) * chunk_lanes
        re0 = c * chunk_lanes
        im0 = re0 + STATE_CHUNK
        hs_ref[:, re0:re0 + chunk_lanes] = _dot(ub[:, h * half_in:(h + 1) * half_in],
                                                bm_ref[h, :, lo:lo + chunk_lanes])
        lr = jnp.broadcast_to(lam_ref[0:1, re0:re0 + STATE_CHUNK], (SUBLANES, STATE_CHUNK))
        li = jnp.broadcast_to(lam_ref[0:1, im0:im0 + STATE_CHUNK], (SUBLANES, STATE_CHUNK))
        hr = hstate_ref[:, re0:re0 + STATE_CHUNK]
        hi = hstate_ref[:, im0:im0 + STATE_CHUNK]
        for t in range(tc):
            r0 = t * SUBLANES
            hr, hi = (lr * hr - li * hi + hs_ref[r0:r0 + SUBLANES, re0:re0 + STATE_CHUNK],
                      lr * hi + li * hr + hs_ref[r0:r0 + SUBLANES, im0:im0 + STATE_CHUNK])
            hs_ref[r0:r0 + SUBLANES, re0:re0 + STATE_CHUNK] = hr
            hs_ref[r0:r0 + SUBLANES, im0:im0 + STATE_CHUNK] = hi
        hstate_ref[:, re0:re0 + STATE_CHUNK] = hr
        hstate_ref[:, im0:im0 + STATE_CHUNK] = hi
        part = _dot(hs_ref[:, re0:re0 + chunk_lanes].astype(BF16), cm_ref[h, lo:lo + chunk_lanes, :])
        ys[h] = part if ys[h] is None else ys[h] + part

    y = jnp.concatenate(ys, axis=1) + d_ref[...] * u
    y = _gelu(y)
    z = _dot(y.astype(BF16), wglu_ref[...]) + bglu_ref[...]
    out = y * _sigmoid(z)
    for s in range(n_slabs):
        y_ref[s] = out[:, s * LANES:(s + 1) * LANES]

    @pl.when(j == pl.num_programs(1) - 1)
    def _():
        hlast_ref[0] = hstate_ref[...]


def _ssm(u_tb, h0, lam, bm, cm, d, wglu, bglu, *, n_groups, n_time, tc):
    rows = tc * SUBLANES
    nt = n_time // tc
    n_slabs = SSM_WIDTH // LANES
    full = lambda a: pl.BlockSpec(a.shape, lambda g, j: (0,) * a.ndim)
    st = pl.BlockSpec((1, SUBLANES, 2 * N_STATE), lambda g, j: (g, 0, 0))
    slabs = pl.BlockSpec((n_slabs, rows, LANES), lambda g, j: (0, g * nt + j, 0))
    return pl.pallas_call(
        functools.partial(_ssm_kernel, tc=tc),
        grid=(n_groups, nt),
        in_specs=[slabs, st, full(lam), full(bm), full(cm), full(d), full(wglu), full(bglu)],
        out_specs=[slabs, st],
        out_shape=[jax.ShapeDtypeStruct((n_slabs, n_groups * n_time * SUBLANES, LANES), F32),
                   jax.ShapeDtypeStruct((n_groups, SUBLANES, 2 * N_STATE), F32)],
        scratch_shapes=[pltpu.VMEM((rows, 2 * N_STATE), F32), pltpu.VMEM((SUBLANES, 2 * N_STATE), F32)],
        compiler_params=_params("parallel", "arbitrary"),
        name="ssm",
    )(u_tb, h0, lam, bm, cm, d, wglu, bglu)


def _state_lanes(re, im):
    lead = re.shape[:-1]
    r = re.reshape(lead + (N_CHUNKS, 1, STATE_CHUNK))
    i = im.reshape(lead + (N_CHUNKS, 1, STATE_CHUNK))
    return jnp.concatenate([r, i], axis=-2).reshape(lead + (2 * N_STATE,))


def _state_unlanes(x):
    lead = x.shape[:-1]
    y = x.reshape(lead + (N_CHUNKS, 2, STATE_CHUNK))
    return y[..., 0, :].reshape(lead + (N_STATE,)), y[..., 1, :].reshape(lead + (N_STATE,))


def _ssm_params(lam_re, lam_im, log_dt, b_re, b_im, c_re, c_im):
    lam = lax.complex(lam_re.astype(F32), lam_im.astype(F32))
    dt = jnp.exp(log_dt.astype(F32))[:, None]
    lam_bar = jnp.exp(lam * dt)
    b = lax.complex(b_re.astype(F32), b_im.astype(F32))
    b_bar = ((lam_bar - 1.0) / lam)[..., None] * b
    eye = jnp.eye(SSM_GROUPS, dtype=F32)

    def in_blockdiag(x):
        return jnp.einsum('gpc,gh->gchp', x, eye).reshape(SSM_WIDTH, N_STATE)

    def out_blockdiag(x):
        return jnp.einsum('gcp,gh->gphc', x, eye).reshape(N_STATE, SSM_WIDTH)

    b_full = _state_lanes(in_blockdiag(jnp.real(b_bar)), in_blockdiag(jnp.imag(b_bar)))
    c_full = _state_lanes(out_blockdiag(c_re.astype(F32)).T, -out_blockdiag(c_im.astype(F32)).T).T
    hw, hs = SSM_WIDTH // 2, N_STATE
    bm = jnp.stack([b_full[h * hw:(h + 1) * hw, h * hs:(h + 1) * hs] for h in range(2)]).astype(BF16)
    cm = jnp.stack([c_full[h * hs:(h + 1) * hs, h * hw:(h + 1) * hw] for h in range(2)]).astype(BF16)
    lam_l = _state_lanes(jnp.real(lam_bar).reshape(1, N_STATE), jnp.imag(lam_bar).reshape(1, N_STATE))
    return lam_l, bm, cm


def _compress_kernel(x_ref, pe_ref, w1_ref, w2_ref, o_ref, *, tm):
    def body(sp, acc):
        s0 = 2 * sp
        xa = x_ref[pl.ds(s0, tm, stride=BLK), :] + pe_ref[0, pl.ds(s0, 1), :]
        xb = x_ref[pl.ds(s0 + 1, tm, stride=BLK), :] + pe_ref[0, pl.ds(s0 + 1, 1), :]
        lhs = jnp.concatenate([xa, xb], axis=1).astype(BF16)
        return acc + _dot(lhs, w1_ref[0, sp])

    acc = lax.fori_loop(0, BLK // 2, body, jnp.zeros((tm, KV_WIDTH), F32))
    o_ref[0] = _dot(_gelu(acc).astype(BF16), w2_ref[0])


def _compress(x2d, pe, w1, w2, *, n_blocks, tm):
    return pl.pallas_call(
        functools.partial(_compress_kernel, tm=tm),
        grid=(2, n_blocks // tm),
        in_specs=[pl.BlockSpec((tm * BLK, KV_WIDTH), lambda c, i: (i, c)),
                  pl.BlockSpec((1, BLK, KV_WIDTH), lambda c, i: (c, 0, 0)),
                  pl.BlockSpec((1, BLK // 2, 2 * KV_WIDTH, KV_WIDTH), lambda c, i: (c, 0, 0, 0)),
                  pl.BlockSpec((1, KV_WIDTH, KV_WIDTH), lambda c, i: (c, 0, 0))],
        out_specs=pl.BlockSpec((1, tm, KV_WIDTH), lambda c, i: (c, i, 0)),
        out_shape=jax.ShapeDtypeStruct((2, n_blocks, KV_WIDTH), F32),
        compiler_params=_params("parallel", "parallel"),
        name="compress",
    )(x2d, pe, w1, w2)


def _compress_params(pe_k, w1_k, w2_k, pe_v, w1_v, w2_v):
    def bd(w):
        z = jnp.zeros_like(w)
        return jnp.concatenate([jnp.concatenate([w, z], axis=-1), jnp.concatenate([z, w], axis=-1)], axis=-2)

    def one(pe, w1, w2):
        w1s = bd(w1.astype(F32).reshape(BLK, HEAD_DIM, HEAD_DIM))
        return (jnp.tile(pe.astype(F32), (1, NSA_KV)),
                w1s.reshape(BLK // 2, 2 * KV_WIDTH, KV_WIDTH).astype(BF16),
                bd(w2.astype(F32)).astype(BF16))

    k, v = one(pe_k, w1_k, w2_k), one(pe_v, w1_v, w2_v)
    return tuple(jnp.stack([a, b]) for a, b in zip(k, v))


def _compress_paged_kernel(pt_ref, cache_ref, pe_ref, w1_ref, w2_ref, o_ref, buf_ref, sems, *, m):
    step = pl.program_id(0)

    def page_copy(s, j):
        slot = s % 2
        row0 = pl.multiple_of(j * PAGE_PITCH, SUBLANES)
        return pltpu.make_async_copy(cache_ref.at[pt_ref[s * m + j], pl.ds(0, 2)],
                                     buf_ref.at[pl.ds(2 * slot, 2), pl.ds(row0, PAGE_SIZE), :],
                                     sems.at[slot])

    def start_all(s):
        def start(j, _):
            page_copy(s, j).start()
            return 0
        lax.fori_loop(0, m, start, 0, unroll=8)

    @pl.when(step == 0)
    def _():
        start_all(step)

    @pl.when(step + 1 < pl.num_programs(0))
    def _():
        start_all(step + 1)

    def wait(j, _):
        page_copy(step, j).wait()
        return 0

    lax.fori_loop(0, m, wait, 0, unroll=8)

    for c in range(2):
        tile = 2 * (step % 2) + c
        res = []
        for kv in range(NSA_KV):
            def body(dp, acc, c=c, kv=kv, tile=tile):
                d0 = 2 * dp
                r0 = kv * HEAD_DIM + d0
                xa = buf_ref[tile, pl.ds(r0, m, stride=PAGE_PITCH), :] + pe_ref[c, pl.ds(d0, 1), :]
                xb = buf_ref[tile, pl.ds(r0 + 1, m, stride=PAGE_PITCH), :] + pe_ref[c, pl.ds(d0 + 1, 1), :]
                lhs = jnp.concatenate([xa, xb], axis=1).astype(BF16)
                return acc + _dot(lhs, w1_ref[c, dp])

            acc = lax.fori_loop(0, HEAD_DIM // 2, body, jnp.zeros((m, PAGE_SIZE), F32), unroll=4)
            res.append(_dot(_gelu(acc).astype(BF16), w2_ref[c]))
        for blk in range(PAGE_SIZE // BLK):
            cols = slice(blk * HEAD_DIM, (blk + 1) * HEAD_DIM)
            o_ref[c, pl.ds(blk, m, stride=PAGE_SIZE // BLK), :] = jnp.concatenate(
                [r[:, cols] for r in res], axis=1)


def _compress_paged(pt_flat, cache_t, pe_t, w1_t, w2, *, m):
    n = pt_flat.shape[0]
    full = lambda a: pl.BlockSpec(a.shape, lambda i, pt: (0,) * a.ndim)
    grid_spec = pltpu.PrefetchScalarGridSpec(
        num_scalar_prefetch=1,
        grid=(n // m,),
        in_specs=[pl.BlockSpec(memory_space=pl.ANY), full(pe_t), full(w1_t), full(w2)],
        out_specs=pl.BlockSpec((2, m * (PAGE_SIZE // BLK), KV_WIDTH), lambda i, pt: (0, i, 0)),
        scratch_shapes=[pltpu.VMEM((4, m * PAGE_PITCH, PAGE_SIZE), F32), pltpu.SemaphoreType.DMA((2,))])
    return pl.pallas_call(
        functools.partial(_compress_paged_kernel, m=m),
        grid_spec=grid_spec,
        out_shape=jax.ShapeDtypeStruct((2, n * (PAGE_SIZE // BLK), KV_WIDTH), F32),
        compiler_params=_params("arbitrary"),
        name="compress_paged",
    )(pt_flat, cache_t, pe_t, w1_t, w2)


def _compress_paged_params(pe_k, w1_k, pe_v, w1_v):
    def bd(w):
        z = jnp.zeros_like(w)
        return jnp.concatenate([jnp.concatenate([w, z], axis=-1), jnp.concatenate([z, w], axis=-1)], axis=-2)

    def one(pe, w1):
        w1d = bd(w1.astype(F32).reshape(BLK, HEAD_DIM, HEAD_DIM).transpose(1, 0, 2))
        return (jnp.tile(pe.astype(F32).T, (1, PAGE_SIZE // BLK)),
                w1d.reshape(HEAD_DIM // 2, 2 * PAGE_SIZE, PAGE_SIZE).astype(BF16))

    k, v = one(pe_k, w1_k), one(pe_v, w1_v)
    return tuple(jnp.stack([a, b]) for a, b in zip(k, v))


def _stack_queries(q, nq):
    q = q.astype(F32)
    z = jnp.zeros((nq, HEAD_DIM), F32)
    rows = []
    for h in range(NSA_HEADS):
        blk = q[:, h * HEAD_DIM:(h + 1) * HEAD_DIM]
        rows.append(jnp.concatenate([blk, z] if h < NSA_REP else [z, blk], axis=1))
    return (jnp.concatenate(rows, axis=0) * (HEAD_DIM ** -0.5)).astype(BF16)


def _masked_softmax(s, valid):
    s = jnp.where(valid, s, NEG_INF)
    m = jnp.max(s, axis=-1, keepdims=True)
    p = jnp.exp(s - m) * valid.astype(F32)
    return p / jnp.maximum(jnp.sum(p, axis=-1, keepdims=True), 1e-30)


def _select_blocks(imp, n_ids, cur, nb):
    forced = (n_ids == 0) | (n_ids == cur) | (n_ids == cur - 1)
    imp = jnp.where(forced, FORCE_SCORE, imp)
    imp = jnp.where(n_ids <= cur, imp, -FORCE_SCORE)
    rank = jnp.zeros(imp.shape, F32)
    for m in range(nb):
        col = imp[:, m:m + 1]
        beats = (col > imp) | ((col == imp) & (n_ids > m))
        rank = rank + beats.astype(F32)
    return (rank < float(N_SEL)).astype(F32)


def _select_blocks_t(imp_t, cur, nb):
    n_t = lax.broadcasted_iota(jnp.int32, imp_t.shape, 0)
    forced = (n_t == 0) | (n_t == cur) | (n_t == cur - 1)
    imp_t = jnp.where(forced, FORCE_SCORE, imp_t)
    imp_t = jnp.where(n_t <= cur, imp_t, -FORCE_SCORE)
    rank = jnp.zeros(imp_t.shape, F32)
    for m in range(nb):
        row = imp_t[m:m + 1, :]
        beats = (row > imp_t) | ((row == imp_t) & (n_t > m))
        rank = rank + beats.astype(F32)
    return ((rank < float(N_SEL)) & (n_t <= cur)).astype(F32)


def _combine_heads(gates, o_c, o_s, o_w, nq):
    outs = []
    for h in range(NSA_HEADS):
        rows = slice(h * nq, (h + 1) * nq)
        o = (gates[:, 3 * h:3 * h + 1] * o_c[rows] + gates[:, 3 * h + 1:3 * h + 2] * o_s[rows]
             + gates[:, 3 * h + 2:3 * h + 3] * o_w[rows])
        g = h // NSA_REP
        outs.append(o[:, g * HEAD_DIM:(g + 1) * HEAD_DIM])
    return jnp.concatenate(outs, axis=1)


def _nsa_pair_kernel(qa_ref, qb_ref, ga_ref, gb_ref, kc_ref, vc_ref, ks_ref, vs_ref, kw_ref, vw_ref,
                     et_ref, tri_ref, wm_ref, oa_ref, ob_ref, q2_ref, etd_ref, s_ref, mx_ref, acc_ref,
                     *, nb, ch, wk):
    i = pl.program_id(1)
    nq = BLK
    rows = NSA_HEADS * nq
    bpc = ch // BLK
    n_units = nb // bpc + 1
    chunk_of = (i, nb - 1 - i)
    n_a = i // bpc + 1
    row_q = lax.broadcasted_iota(jnp.int32, (rows, LANES), 0) % nq
    rq_minus_lane = row_q - lax.broadcasted_iota(jnp.int32, (rows, LANES), 1)

    def col_blocks(s):
        return [s[:, j * LANES:(j + 1) * LANES] for j in range(s.shape[1] // LANES)]

    def col_max(cols):
        m = cols[0]
        for c_ in cols[1:]:
            m = jnp.maximum(m, c_)
        return m

    def finish(acc):
        return acc[:, 0:KV_WIDTH] / jnp.maximum(acc[:, KV_WIDTH:2 * KV_WIDTH], 1e-30)

    def with_ones(v):
        return jnp.concatenate([v, jnp.ones((v.shape[0], LANES), BF16)], axis=1)

    q_onehot = jnp.where(rq_minus_lane == -BLK, NEG_INF, 0.0)

    def prepare(idx, q_ref):
        ci = chunk_of[idx]
        q2 = _stack_queries(q_ref[...], nq)
        first = jnp.maximum(ci - WINDOW // BLK, 0) // (LANES // BLK)
        w0 = pl.multiple_of(first * LANES, LANES)
        delta = ci - first * (LANES // BLK)
        lhs = jnp.concatenate([q2, q_onehot.astype(BF16)], axis=1)
        rhs = jnp.concatenate([kw_ref[pl.ds(w0, wk), :], wm_ref[delta]], axis=1)
        cols = col_blocks(_dot_nt(lhs, rhs))
        m_w = jnp.maximum(jnp.max(col_max(cols), axis=-1, keepdims=True), M_INIT)
        m_w = jnp.broadcast_to(m_w, (rows, LANES))
        p = jnp.concatenate([jnp.exp(c_ - m_w).astype(BF16) for c_ in cols], axis=1)
        o_w = finish(_dot(p, with_ones(vw_ref[pl.ds(w0, wk), :])))

        q_pos = ci * BLK + lax.broadcasted_iota(jnp.int32, (rows, 1), 0) % nq
        n_ids = lax.broadcasted_iota(jnp.int32, (rows, nb), 1)
        s_c = _dot_nt(q2, kc_ref[0].astype(BF16))
        p_c = _masked_softmax(s_c, (n_ids + 1) * BLK - 1 <= q_pos)
        o_c = _dot(p_c.astype(BF16), vc_ref[0].astype(BF16))

        imps = []
        for g in range(NSA_KV):
            imp = p_c[g * NSA_REP * nq:(g * NSA_REP + 1) * nq]
            for r in range(1, NSA_REP):
                imp = imp + p_c[(g * NSA_REP + r) * nq:(g * NSA_REP + r + 1) * nq]
            imps.append(imp)
        imp2 = jnp.concatenate([jnp.concatenate(imps, axis=0), jnp.zeros((LANES, LANES - nb), F32)], axis=1)
        sel_t = _select_blocks_t(imp2.T[0:nb], ci, nb)
        sel2 = jnp.concatenate([sel_t, jnp.ones((LANES - nb, LANES), F32)], axis=0).T
        neg = (sel2 - 1.0) * (-NEG_INF)
        neg_rows = jnp.concatenate([neg[g * nq:(g + 1) * nq] for g in range(NSA_KV) for _ in range(NSA_REP)],
                                   axis=0)
        q2_ref[idx] = jnp.concatenate([q2, (neg_rows + q_onehot).astype(BF16)], axis=1)
        in_diag = lax.broadcasted_iota(jnp.int32, (ch, LANES), 0) // BLK == ci % bpc
        etd_ref[idx] = et_ref[ci // bpc] + jnp.where(in_diag, tri_ref[...], jnp.zeros((), BF16))
        return o_c, o_w

    o_ca, o_wa = prepare(0, qa_ref)
    o_cb, o_wb = prepare(1, qb_ref)

    mx_ref[...] = jnp.full(mx_ref.shape, M_INIT, F32)
    acc_ref[...] = jnp.zeros(acc_ref.shape, F32)
    units = []
    for u in range(n_units):
        which = (u >= n_a).astype(jnp.int32)
        kc = u - which * n_a
        units.append((which, kc))
        k0 = pl.multiple_of(kc * ch, ch)
        diag = kc == jnp.where(which == 1, chunk_of[1], chunk_of[0]) // bpc
        key_mask = jnp.where(diag, etd_ref[which], et_ref[kc])
        rhs = jnp.concatenate([ks_ref[pl.ds(k0, ch), :], key_mask], axis=1)
        cols = col_blocks(_dot_nt(q2_ref[which], rhs))
        mx_ref[which] = jnp.maximum(mx_ref[which], col_max(cols))
        for j, c_ in enumerate(cols):
            s_ref[u, :, j * LANES:(j + 1) * LANES] = c_

    for idx in range(2):
        mx_ref[idx] = jnp.broadcast_to(jnp.max(mx_ref[idx], axis=-1, keepdims=True), (rows, LANES))
    for u, (which, kc) in enumerate(units):
        k0 = pl.multiple_of(kc * ch, ch)
        m_u = mx_ref[which]
        p = jnp.concatenate([jnp.exp(s_ref[u, :, j * LANES:(j + 1) * LANES] - m_u).astype(BF16)
                             for j in range(ch // LANES)], axis=1)
        acc_ref[which] += _dot(p, with_ones(vs_ref[pl.ds(k0, ch), :]))

    oa_ref[...] = _combine_heads(ga_ref[...], o_ca, finish(acc_ref[0]), o_wa, nq).astype(oa_ref.dtype)
    ob_ref[...] = _combine_heads(gb_ref[...], o_cb, finish(acc_ref[1]), o_wb, nq).astype(ob_ref.dtype)


def _nsa_prompt(q, gates, cmp_kv, kvb, *, n_seq, seq):
    nb = seq // BLK
    rows = NSA_HEADS * BLK
    ch = min(SEL_CHUNK, seq)
    wk = min(WINDOW + LANES, seq)
    bpc = ch // BLK
    assert seq % ch == 0 and wk % LANES == 0 and nb % 2 == 0 and nb <= LANES
    half = nb // 2
    n_units = nb // bpc + 1
    assert nb <= BLK
    et = np.zeros((seq // ch, ch, LANES), np.float32)
    for c in range(seq // ch):
        et[c, np.arange(ch), (c * ch + np.arange(ch)) // BLK] = 1.0
    key = np.arange(ch)[:, None]
    qry = np.arange(BLK)[None, :]
    tri = np.zeros((ch, LANES), np.float32)
    tri[:, BLK:] = (key % BLK) > qry
    n_delta = WINDOW // BLK + LANES // BLK
    wm = np.zeros((n_delta, wk, LANES), np.float32)
    for dl in range(n_delta):
        dist = dl * BLK + qry - np.arange(wk)[:, None]
        wm[dl, :, BLK:] = (dist < 0) | (dist >= WINDOW)
    chunk_a = lambda w_: pl.BlockSpec((BLK, w_), lambda b, i: (b * nb + i, 0))
    chunk_b = lambda w_: pl.BlockSpec((BLK, w_), lambda b, i: (b * nb + nb - 1 - i, 0))
    out_spec = pl.BlockSpec((BLK, NSA_WIDTH), lambda b, i: (b * half + i, 0))
    kv = lambda col: pl.BlockSpec((seq, KV_WIDTH), lambda b, i: (b, col))
    cmp_ = lambda which: pl.BlockSpec((1, nb, KV_WIDTH), lambda b, i: (which, b, 0))
    out = jax.ShapeDtypeStruct((n_seq * half * BLK, NSA_WIDTH), BF16)
    o_a, o_b = pl.pallas_call(
        functools.partial(_nsa_pair_kernel, nb=nb, ch=ch, wk=wk),
        grid=(n_seq, half),
        in_specs=[chunk_a(NSA_WIDTH), chunk_b(NSA_WIDTH), chunk_a(LANES), chunk_b(LANES),
                  cmp_(0), cmp_(1), kv(2), kv(3), kv(4), kv(5),
                  pl.BlockSpec(et.shape, lambda b, i: (0, 0, 0)),
                  pl.BlockSpec(tri.shape, lambda b, i: (0, 0)),
                  pl.BlockSpec(wm.shape, lambda b, i: (0, 0, 0))],
        out_specs=[out_spec, out_spec],
        out_shape=[out, out],
        scratch_shapes=[pltpu.VMEM((2, rows, 2 * LANES), BF16),
                        pltpu.VMEM((2, ch, LANES), BF16),
                        pltpu.VMEM((n_units, rows, ch), F32),
                        pltpu.VMEM((2, rows, LANES), F32),
                        pltpu.VMEM((2, rows, 2 * KV_WIDTH), F32)],
        compiler_params=_params("parallel", "arbitrary"),
        name="nsa_prompt",
    )(q, q, gates, gates, cmp_kv, cmp_kv, kvb, kvb, kvb, kvb,
      jnp.asarray(et, dtype=BF16), jnp.asarray(tri, dtype=BF16), jnp.asarray(wm, dtype=BF16))
    return o_a, o_b


def _unmirror(o_a, o_b, n_seq):
    half = o_a.shape[0] // (n_seq * BLK)
    o_a = o_a.reshape(n_seq, half, BLK, NSA_WIDTH)
    o_b = o_b.reshape(n_seq, half, BLK, NSA_WIDTH)[:, ::-1]
    return jnp.concatenate([o_a, o_b], axis=1).reshape(-1, NSA_WIDTH)


def _nsa_sample_kernel(pt_ref, q_ref, gate_ref, kc_ref, vc_ref, cache_ref, nks_ref, nvs_ref,
                       wt_ref, nkw_ref, nvw_ref, e_ref, o_ref, kv_buf, s_scr, sems,
                       *, nq, past, nbp, tk):
    b = pl.program_id(0)
    n_pages = past // PAGE_SIZE
    rows = NSA_HEADS * nq

    def page_copy(s, j):
        slot = s % 2
        k0 = pl.multiple_of(j * PAGE_SIZE, PAGE_SIZE)
        return pltpu.make_async_copy(cache_ref.at[pt_ref[s * n_pages + j], pl.ds(2, 2)],
                                     kv_buf.at[pl.ds(2 * slot, 2), :, pl.ds(k0, PAGE_SIZE)], sems.at[slot])

    def start_all(s):
        def start(j, _):
            page_copy(s, j).start()
            return 0
        lax.fori_loop(0, n_pages, start, 0, unroll=8)

    @pl.when(b == 0)
    def _():
        start_all(b)

    @pl.when(b + 1 < pl.num_programs(0))
    def _():
        start_all(b + 1)

    slot = b % 2
    q2 = _stack_queries(q_ref[0], nq)
    q_pos = past + lax.broadcasted_iota(jnp.int32, (rows, 1), 0) % nq
    cur = past // BLK

    n_ids = lax.broadcasted_iota(jnp.int32, (rows, nbp), 1)
    s_c = _dot_nt(q2, kc_ref[0].astype(BF16))
    p_c = _masked_softmax(s_c, (n_ids + 1) * BLK - 1 <= q_pos)
    o_c = _dot(p_c.astype(BF16), vc_ref[0].astype(BF16))

    n_sel = lax.broadcasted_iota(jnp.int32, (nq, nbp), 1)
    sels = []
    for g in range(NSA_KV):
        imp = p_c[g * NSA_REP * nq:(g * NSA_REP + 1) * nq]
        for r in range(1, NSA_REP):
            imp = imp + p_c[(g * NSA_REP + r) * nq:(g * NSA_REP + r + 1) * nq]
        sel = _select_blocks(imp, n_sel, cur, cur + 1)
        sels.extend([sel] * NSA_REP)
    keys = _dot(jnp.concatenate(sels, axis=0).astype(BF16), e_ref[...])
    bias = (keys - 1.0) * (-NEG_INF)

    def col_max(s):
        m = s[:, 0:LANES]
        for j in range(1, s.shape[1] // LANES):
            m = jnp.maximum(m, s[:, j * LANES:(j + 1) * LANES])
        return m

    def col_sum(p):
        t = p[:, 0:LANES]
        for j in range(1, p.shape[1] // LANES):
            t = t + p[:, j * LANES:(j + 1) * LANES]
        return t

    def row_max(mx):
        return jnp.broadcast_to(jnp.max(mx, axis=-1, keepdims=True), mx.shape)

    def tiled(m, width):
        return m if width == LANES else jnp.concatenate([m] * (width // LANES), axis=1)

    new_pos = past + lax.broadcasted_iota(jnp.int32, (rows, LANES), 1)

    def wait(j, _):
        page_copy(b, j).wait()
        return 0

    lax.fori_loop(0, n_pages, wait, 0, unroll=8)

    mx = jnp.full((rows, LANES), M_INIT, F32)
    for t in range(past // tk):
        s = _dot(q2, kv_buf[2 * slot, :, t * tk:(t + 1) * tk].astype(BF16)) + bias[:, t * tk:(t + 1) * tk]
        s_scr[:, t * tk:(t + 1) * tk] = s
        mx = jnp.maximum(mx, col_max(s))
    s_new = _dot_nt(q2, nks_ref[0].astype(BF16)) + bias[:, past:past + LANES]
    s_new = jnp.where(new_pos <= q_pos, s_new, NEG_INF)
    m_s = row_max(jnp.maximum(mx, s_new))
    p_new = jnp.exp(s_new - m_s)
    acc = _dot(p_new.astype(BF16), nvs_ref[0].astype(BF16))
    lsum = p_new
    for t in range(past // tk):
        p = jnp.exp(s_scr[:, t * tk:(t + 1) * tk] - tiled(m_s, tk))
        lsum = lsum + col_sum(p)
        acc = acc + _dot_nt(p.astype(BF16), kv_buf[2 * slot + 1, :, t * tk:(t + 1) * tk].astype(BF16))
    o_s = acc / jnp.maximum(jnp.sum(lsum, axis=-1, keepdims=True), 1e-30)

    wlen = wt_ref.shape[3]
    w_pos = past - wlen + lax.broadcasted_iota(jnp.int32, (rows, wlen), 1)
    d = q_pos - w_pos
    s_w = _dot(q2, wt_ref[0, 0].astype(BF16))
    s_w = jnp.where((d >= 0) & (d < WINDOW) & (w_pos >= 0), s_w, NEG_INF)
    d = q_pos - new_pos
    s_nw = _dot_nt(q2, nkw_ref[0].astype(BF16))
    s_nw = jnp.where((d >= 0) & (d < WINDOW), s_nw, NEG_INF)
    m_w = row_max(jnp.maximum(jnp.maximum(col_max(s_w), s_nw), M_INIT))
    p_w = jnp.exp(s_w - tiled(m_w, wlen))
    p_nw = jnp.exp(s_nw - m_w)
    acc = _dot_nt(p_w.astype(BF16), wt_ref[0, 1].astype(BF16)) + _dot(p_nw.astype(BF16), nvw_ref[0].astype(BF16))
    o_w = acc / jnp.maximum(jnp.sum(col_sum(p_w) + p_nw, axis=-1, keepdims=True), 1e-30)

    o_ref[0] = _combine_heads(gate_ref[0], o_c, o_s, o_w, nq).astype(o_ref.dtype)


def _nsa_sample(pt_flat, q3, gates3, kc, vc, cache_t, new_rows, win_t, expand, *, past, tk):
    n_seq, nq, _ = q3.shape
    nbp = kc.shape[1]
    rows = NSA_HEADS * nq
    per_seq = lambda a: pl.BlockSpec((1,) + a.shape[1:], lambda b, pt: (b,) + (0,) * (a.ndim - 1))
    new = lambda col: pl.BlockSpec((1, LANES, KV_WIDTH), lambda b, pt: (b, 0, col))
    grid_spec = pltpu.PrefetchScalarGridSpec(
        num_scalar_prefetch=1,
        grid=(n_seq,),
        in_specs=[per_seq(q3), per_seq(gates3), per_seq(kc), per_seq(vc),
                  pl.BlockSpec(memory_space=pl.ANY),
                  new(2), new(3), per_seq(win_t), new(4), new(5),
                  pl.BlockSpec(expand.shape, lambda b, pt: (0, 0))],
        out_specs=pl.BlockSpec((1, nq, NSA_WIDTH), lambda b, pt: (b, 0, 0)),
        scratch_shapes=[pltpu.VMEM((4, KV_WIDTH, past), F32),
                        pltpu.VMEM((rows, past), F32), pltpu.SemaphoreType.DMA((2,))])
    return pl.pallas_call(
        functools.partial(_nsa_sample_kernel, nq=nq, past=past, nbp=nbp, tk=tk),
        grid_spec=grid_spec,
        out_shape=jax.ShapeDtypeStruct((n_seq, nq, NSA_WIDTH), F32),
        compiler_params=_params("arbitrary"),
        name="nsa_sample",
    )(pt_flat, q3, gates3, kc, vc, cache_t, new_rows, new_rows, win_t, new_rows, new_rows, expand)


def _expand_matrix(nb, n_keys):
    return jnp.asarray(np.arange(n_keys)[None, :] // BLK == np.arange(nb)[:, None], dtype=BF16)


def _merge_kernel(x_ref, ssm_ref, *refs, mirrored):
    nsa_refs = refs[:2] if mirrored else refs[:1]
    gpre_ref, wm_ref, wbs_ref, wbn_ref, wo_ref, gpost_ref, gx_ref, wxq_ref, x1_ref, qx_ref = refs[len(nsa_refs):]
    x = x_ref[...]
    tm = x.shape[0]
    if mirrored:
        upper = nsa_refs[1][...]
        n_chunks = tm // BLK
        upper = jnp.concatenate([upper[(n_chunks - 1 - s) * BLK:(n_chunks - s) * BLK] for s in range(n_chunks)],
                                axis=0)
        nsa = jnp.where(pl.program_id(0) < pl.num_programs(0) // 2, nsa_refs[0][...], upper)
    else:
        nsa = nsa_refs[0][...]
    n_seq = ssm_ref.shape[1] // tm
    ssm = jnp.concatenate([ssm_ref[s, pl.ds(pl.program_id(1), tm, stride=n_seq), :]
                           for s in range(SSM_WIDTH // LANES)], axis=1)
    a = _rms(x, gpre_ref[...]).astype(BF16)
    g_ssm = _sigmoid(_dot(a, wm_ref[:, 0:D_MODEL]))
    g_nsa = _sigmoid(_dot(a, wm_ref[:, D_MODEL:2 * D_MODEL]))
    merged = (g_ssm * _dot(ssm.astype(BF16), wbs_ref[...])
              + g_nsa * _dot(nsa, wbn_ref[...]))
    x1 = x + _rms(_dot(merged.astype(BF16), wo_ref[...]), gpost_ref[...])
    x1_ref[...] = x1
    c = _rms(x1, gx_ref[...]).astype(BF16)
    qx_ref[...] = (_dot(c, wxq_ref[...]) * (X_HEAD_DIM ** -0.5)).astype(BF16)


def _merge(x2d, ssm_slabs, nsa_o, weights, *, tm, n_tiles):
    n = x2d.shape[0]
    n_b = n // (n_tiles * tm)
    row = lambda w_: pl.BlockSpec((tm, w_), lambda t, b: (b * n_tiles + t, 0))
    full = lambda a: pl.BlockSpec(a.shape, lambda t, b: (0,) * a.ndim, pipeline_mode=pl.Buffered(1))
    ssm_spec = pl.BlockSpec((SSM_WIDTH // LANES, tm * n_b, LANES), lambda t, b: (0, t, 0))
    mirrored = isinstance(nsa_o, tuple)
    if mirrored:
        half = n_tiles // 2
        nsa_specs = [pl.BlockSpec((tm, NSA_WIDTH), lambda t, b: (b * half + jnp.minimum(t, half - 1), 0)),
                     pl.BlockSpec((tm, NSA_WIDTH), lambda t, b: (b * half + jnp.minimum(n_tiles - 1 - t, half - 1), 0))]
        nsa_args = list(nsa_o)
    else:
        nsa_specs, nsa_args = [row(NSA_WIDTH)], [nsa_o]
    return pl.pallas_call(
        functools.partial(_merge_kernel, mirrored=mirrored),
        grid=(n_tiles, n_b),
        in_specs=[row(D_MODEL), ssm_spec] + nsa_specs + [full(w) for w in weights],
        out_specs=[row(D_MODEL), row(X_WIDTH)],
        out_shape=[jax.ShapeDtypeStruct((n, D_MODEL), F32), jax.ShapeDtypeStruct((n, X_WIDTH), BF16)],
        compiler_params=_params("parallel", "arbitrary"),
        name="merge",
    )(x2d, ssm_slabs, *nsa_args, *weights)


def _xattn_tile(q, kv_ref, m_len):
    outs = []
    for h in range(X_HEADS):
        cols = slice(h * X_HEAD_DIM, (h + 1) * X_HEAD_DIM)
        k = kv_ref[0, pl.ds(h, m_len, stride=2 * X_HEADS), :].astype(BF16)
        v = kv_ref[0, pl.ds(X_HEADS + h, m_len, stride=2 * X_HEADS), :].astype(BF16)
        s = _dot_nt(q[:, cols], k)
        m = jnp.max(s, axis=-1, keepdims=True)
        p = jnp.exp(s - m)
        p = p / jnp.sum(p, axis=-1, keepdims=True)
        outs.append(_dot(p.astype(BF16), v))
    return jnp.concatenate(outs, axis=1)


def _xattn_kernel(q_ref, kv_ref, o_ref, *, m_len):
    o_ref[0] = _xattn_tile(q_ref[0].astype(BF16), kv_ref, m_len).astype(o_ref.dtype)


def _xattn(q3, mem_kv_rows, *, tq):
    n_seq, t, _ = q3.shape
    m_len = mem_kv_rows.shape[1] // (2 * X_HEADS)
    return pl.pallas_call(
        functools.partial(_xattn_kernel, m_len=m_len),
        grid=(n_seq, t // tq),
        in_specs=[pl.BlockSpec((1, tq, X_WIDTH), lambda b, i: (b, i, 0)),
                  pl.BlockSpec((1, m_len * 2 * X_HEADS, X_HEAD_DIM), lambda b, i: (b, 0, 0))],
        out_specs=pl.BlockSpec((1, tq, X_WIDTH), lambda b, i: (b, i, 0)),
        out_shape=jax.ShapeDtypeStruct((n_seq, t, X_WIDTH), q3.dtype),
        compiler_params=_params("parallel", "parallel"),
        name="xattn",
    )(q3, mem_kv_rows)


def _mlp_kernel(x1_ref, o_ref, *refs, m_len):
    if m_len:
        kv_ref, refs = refs[0], refs[1:]
        o = _xattn_tile(o_ref[...], kv_ref, m_len).astype(BF16)
    else:
        o = o_ref[...]
    wxo_ref, gxp_ref, gm_ref, wup_ref, wdn_ref, gmp_ref, y_ref = refs
    x2 = x1_ref[...] + _rms(_dot(o, wxo_ref[...]), gxp_ref[...])
    m = _rms(x2, gm_ref[...]).astype(BF16)
    hid = jnp.maximum(_dot(m, wup_ref[...]), 0.0)
    hid = (hid * hid).astype(BF16)
    y_ref[...] = x2 + _rms(_dot(hid, wdn_ref[...]), gmp_ref[...])


def _mlp(x1, o, weights, *, tm, mem_kv_rows=None, tiles_per_seq=1):
    n = x1.shape[0]
    row = lambda w_: pl.BlockSpec((tm, w_), lambda i: (i, 0))
    full = lambda a: pl.BlockSpec(a.shape, lambda i: (0,) * a.ndim, pipeline_mode=pl.Buffered(1))
    m_len, kv_specs, kv_args = 0, [], []
    if mem_kv_rows is not None:
        m_len = mem_kv_rows.shape[1] // (2 * X_HEADS)
        kv_specs = [pl.BlockSpec((1,) + mem_kv_rows.shape[1:], lambda i: (i // tiles_per_seq, 0, 0))]
        kv_args = [mem_kv_rows]
    return pl.pallas_call(
        functools.partial(_mlp_kernel, m_len=m_len),
        grid=(n // tm,),
        in_specs=[row(D_MODEL), row(X_WIDTH)] + kv_specs + [full(w) for w in weights],
        out_specs=row(D_MODEL),
        out_shape=jax.ShapeDtypeStruct((n, D_MODEL), F32),
        compiler_params=_params("parallel"),
        name="mlp",
    )(x1, o, *kv_args, *weights)


def _memkv_kernel(m_ref, g_ref, w_ref, o_ref, *, tm):
    kv = _dot(_rms(m_ref[...], g_ref[...]).astype(BF16), w_ref[...])
    n_rows = 2 * X_HEADS
    for j in range(n_rows):
        o_ref[pl.ds(j, tm, stride=n_rows), :] = kv[:, j * X_HEAD_DIM:(j + 1) * X_HEAD_DIM]


def _memkv(mem2d, g, w, *, tm):
    n = mem2d.shape[0]
    n_rows = 2 * X_HEADS
    return pl.pallas_call(
        functools.partial(_memkv_kernel, tm=tm),
        grid=(n // tm,),
        in_specs=[pl.BlockSpec((tm, D_MODEL), lambda i: (i, 0)),
                  pl.BlockSpec(g.shape, lambda i: (0, 0)), pl.BlockSpec(w.shape, lambda i: (0, 0))],
        out_specs=pl.BlockSpec((tm * n_rows, X_HEAD_DIM), lambda i: (i, 0)),
        out_shape=jax.ShapeDtypeStruct((n * n_rows, X_HEAD_DIM), F32),
        compiler_params=_params("parallel"),
        name="memkv",
    )(mem2d, g, w)


def _row(v):
    return v.astype(F32).reshape(1, -1)


def kernel(x_prompt, x_sample, cache_nsa_kv, cache_win_kv, state_ssm, cache_mem_kv, page_table, mem_prompt, g_mix_pre, w_in, ssm_lam_re, ssm_lam_im, ssm_log_dt, ssm_b_re, ssm_b_im, ssm_c_re, ssm_c_im, ssm_d, w_glu, b_glu, cmp_pe_k, w_cmpk1, w_cmpk2, cmp_pe_v, w_cmpv1, w_cmpv2, w_br_ssm, w_br_nsa, w_out, g_mix_post, g_x_pre, g_mem, w_xq, w_xk, w_xv, w_xo, g_x_post, g_mlp_pre, w_up, w_down, g_mlp_post):
    depth = w_in.shape[0]
    n_seq_p, seq, _ = x_prompt.shape
    n_seq_s, nq, _ = x_sample.shape
    past = page_table.shape[1] * PAGE_SIZE
    assert depth == 1 and seq % BLK == 0 and nq <= SUBLANES and past % BLK == 0
    assert n_seq_p == SUBLANES and n_seq_s % SUBLANES == 0

    y_p = x_prompt.reshape(n_seq_p * seq, D_MODEL)
    y_s = x_sample.reshape(n_seq_s * nq, D_MODEL)
    l = 0

    w_proj = w_in[l, :, :N_PROJ].astype(BF16)
    w_gate = jnp.pad(w_in[l, :, N_PROJ:N_PROJ + N_GATE], ((0, 0), (0, LANES - N_GATE))).astype(BF16)
    w_merge = w_in[l, :, N_PROJ + N_GATE:].astype(BF16)
    lam_l, bm, cm = _ssm_params(ssm_lam_re[l], ssm_lam_im[l], ssm_log_dt[l], ssm_b_re[l], ssm_b_im[l],
                                ssm_c_re[l], ssm_c_im[l])
    ssm_w = (lam_l, bm, cm, _row(ssm_d[l]), w_glu[l].astype(BF16), _row(b_glu[l]))
    cmp_w = _compress_params(cmp_pe_k[l], w_cmpk1[l], w_cmpk2[l], cmp_pe_v[l], w_cmpv1[l], w_cmpv2[l])
    merge_w = (_row(g_mix_pre[l]), w_merge, w_br_ssm[l].astype(BF16), w_br_nsa[l].astype(BF16),
               w_out[l].astype(BF16), _row(g_mix_post[l]), _row(g_x_pre[l]), w_xq[l].astype(BF16))
    mlp_w = (w_xo[l].astype(BF16), _row(g_x_post[l]), _row(g_mlp_pre[l]), w_up[l].astype(BF16),
             w_down[l].astype(BF16), _row(g_mlp_post[l]))
    w_mem = jnp.concatenate([w_xk[l], w_xv[l]], axis=1).astype(BF16)

    tm_p = 512 if seq % 512 == 0 else seq
    nt_p = seq // tm_p
    tabs_p = _rope_tables(jnp.arange(seq, dtype=jnp.int32))
    u_p, q_p, kvt_p, wint_p, kvb_p, gate_p, cmp_rows_p = _proj(
        y_p, _row(g_mix_pre[l]), w_proj, w_gate, tabs_p, tm=tm_p, n_tab_blocks=nt_p, prompt=True)
    h0_p = jnp.zeros((1, SUBLANES, 2 * N_STATE), F32)
    tc_p = 64 if seq % 64 == 0 else seq
    ssm_p, hl_p = _ssm(u_p, h0_p, *ssm_w, n_groups=1, n_time=seq, tc=tc_p)

    nb_p = seq // BLK
    n_blocks_p = n_seq_p * nb_p
    cmp_p = _compress(cmp_rows_p, *cmp_w, n_blocks=n_blocks_p, tm=min(n_blocks_p, 256))
    nsa_p = _nsa_prompt(q_p, gate_p, cmp_p, kvb_p, n_seq=n_seq_p, seq=seq)

    mem_kv_p = _memkv(mem_prompt.reshape(-1, D_MODEL), _row(g_mem[l]), w_mem, tm=256)
    m_len = mem_prompt.shape[1]
    mem_kv_p3 = mem_kv_p.reshape(n_seq_p, m_len * 2 * X_HEADS, X_HEAD_DIM)

    if nt_p % 2:
        nsa_p = _unmirror(*nsa_p, n_seq_p)
    x1_p, qx_p = _merge(y_p, ssm_p, nsa_p, merge_w, tm=tm_p, n_tiles=nt_p)
    y_p = _mlp(x1_p, qx_p, mlp_w, tm=tm_p, mem_kv_rows=mem_kv_p3, tiles_per_seq=nt_p)

    n_s = n_seq_s * nq
    pos_s = past + jnp.arange(nq, dtype=jnp.int32)
    tabs_s = tuple(jnp.tile(t, (n_seq_s, 1)) for t in _rope_tables(pos_s))
    u_s, q_s, kv_s, win_s, kvb_s, gate_s = _proj(y_s, _row(g_mix_pre[l]), w_proj, w_gate, tabs_s,
                                                 tm=n_s, n_tab_blocks=1, prompt=False)
    n_grp = n_seq_s // SUBLANES
    n_slabs = SSM_WIDTH // LANES
    u_s = (u_s.reshape(n_grp, SUBLANES, nq, n_slabs, LANES).transpose(3, 0, 2, 1, 4)
           .reshape(n_slabs, n_s, LANES))
    st = state_ssm[l].astype(F32).reshape(n_seq_s, N_STATE, 2)
    h0_s = _state_lanes(st[..., 0], st[..., 1]).reshape(n_grp, SUBLANES, 2 * N_STATE)
    ssm_s, hl_s = _ssm(u_s, h0_s, *ssm_w, n_groups=n_grp, n_time=nq, tc=nq)
    ssm_s = (ssm_s.reshape(n_slabs, n_grp, nq, SUBLANES, LANES).transpose(0, 1, 3, 2, 4)
             .reshape(n_slabs, n_s, LANES))

    n_pages = page_table.shape[1]
    n_pool = cache_nsa_kv.shape[1]
    cache_t = jnp.transpose(cache_nsa_kv[l], (0, 2, 3, 4, 1)).reshape(n_pool, 4, KV_WIDTH, PAGE_SIZE)
    win_t = jnp.transpose(cache_win_kv[l], (0, 2, 3, 4, 1)).reshape(n_seq_s, 2, KV_WIDTH, -1)
    pt_flat = page_table.reshape(-1).astype(jnp.int32)
    nb_past = past // BLK
    pe_t, w1_t = _compress_paged_params(cmp_pe_k[l], w_cmpk1[l], cmp_pe_v[l], w_cmpv1[l])
    cmp_pages = _compress_paged(pt_flat, cache_t, pe_t, w1_t, cmp_w[2], m=min(n_seq_s * n_pages, 128))
    cmp_past = cmp_pages
    new_rows = jnp.pad(jnp.concatenate([kv_s, win_s], axis=1).reshape(n_seq_s, nq, 6 * KV_WIDTH),
                       ((0, 0), (0, LANES - nq), (0, 0)))
    cmp_new = _compress(new_rows[:, :BLK].reshape(n_seq_s * BLK, 6 * KV_WIDTH), *cmp_w,
                        n_blocks=n_seq_s, tm=n_seq_s)
    nbp = -(-(nb_past + 1) // LANES) * LANES
    cmp_s = jnp.concatenate([cmp_past.reshape(2, n_seq_s, nb_past, KV_WIDTH), cmp_new[:, :, None, :],
                             jnp.zeros((2, n_seq_s, nbp - nb_past - 1, KV_WIDTH), F32)], axis=2)
    nsa_s = _nsa_sample(pt_flat, q_s.astype(F32).reshape(n_seq_s, nq, NSA_WIDTH),
                        gate_s.reshape(n_seq_s, nq, LANES), cmp_s[0], cmp_s[1], cache_t, new_rows, win_t,
                        _expand_matrix(nbp, past + LANES), past=past, tk=min(past, 1024))

    x1_s, qx_s = _merge(y_s, ssm_s, nsa_s.reshape(n_s, NSA_WIDTH).astype(BF16), merge_w, tm=n_s, n_tiles=1)
    mem_kv_s3 = cache_mem_kv[l].reshape(n_seq_s, m_len * 2 * X_HEADS, X_HEAD_DIM)
    o_s = _xattn(qx_s.astype(F32).reshape(n_seq_s, nq, X_WIDTH), mem_kv_s3, tq=nq)
    y_s = _mlp(x1_s, o_s.reshape(-1, X_WIDTH).astype(BF16), mlp_w, tm=n_s)

    def ssm_state(hl, n_seq):
        re, im = _state_unlanes(hl.reshape(n_seq, 2 * N_STATE))
        return jnp.stack([re, im], axis=-1).reshape(1, n_seq, SSM_GROUPS, SSM_STATE, 2)

    def token_major(xt):
        n_seq, parts, _, t_len = xt.shape
        return xt.reshape(1, n_seq, parts, NSA_KV, HEAD_DIM, t_len).transpose(0, 1, 5, 2, 3, 4)

    w_keep = min(WINDOW, seq)
    win_new = win_s.reshape(n_seq_s, nq, 2, NSA_KV, HEAD_DIM).astype(cache_win_kv.dtype)
    win_sample = jnp.concatenate([cache_win_kv[l], win_new], axis=1)[:, nq:]
    return (y_p.reshape(n_seq_p, seq, D_MODEL),
            y_s.reshape(n_seq_s, nq, D_MODEL),
            token_major(kvt_p),
            kv_s.reshape(1, n_seq_s, nq, 4, NSA_KV, HEAD_DIM),
            token_major(wint_p[:, :, :, seq - w_keep:]),
            win_sample[None],
            ssm_state(hl_p, n_seq_p),
            ssm_state(hl_s, n_seq_s),
            mem_kv_p.reshape(1, n_seq_p, m_len, 2, X_HEADS, X_HEAD_DIM))
```

```python
import functools
import math

import jax
import jax.numpy as jnp
import numpy as np
from jax import lax
from jax.experimental import pallas as pl
from jax.experimental.pallas import tpu as pltpu

F32 = jnp.float32
BF16 = jnp.bfloat16

D_MODEL = 1024
SSM_WIDTH = 512
SSM_GROUP = 16
SSM_GROUPS = 32
SSM_STATE = 64
N_STATE = SSM_GROUPS * SSM_STATE
STATE_CHUNK = 512
N_CHUNKS = N_STATE // STATE_CHUNK
NSA_HEADS = 8
HEAD_DIM = 64
NSA_WIDTH = NSA_HEADS * HEAD_DIM
NSA_KV = 2
NSA_REP = NSA_HEADS // NSA_KV
KV_WIDTH = NSA_KV * HEAD_DIM
BLK = 64
N_SEL = 16
WINDOW = 512
ROT_DIM = 16
ROPE_THETA = 500000.0
PAGE_SIZE = 128
X_HEADS = 4
X_HEAD_DIM = 128
X_WIDTH = X_HEADS * X_HEAD_DIM
D_FF = 4 * D_MODEL
EPS = 1e-6
NEG_INF = -1e30
M_INIT = -1e29
FORCE_SCORE = 1e4
LANES = 128
SUBLANES = 8
VMEM_LIMIT = 56 * 1024 * 1024

N_PROJ = SSM_WIDTH + NSA_WIDTH + 6 * KV_WIDTH
N_GATE = 3 * NSA_HEADS
SEL_CHUNK = 512
PAGE_PITCH = PAGE_SIZE + SUBLANES


def _params(*sem):
    return pltpu.CompilerParams(dimension_semantics=sem, vmem_limit_bytes=VMEM_LIMIT)


def _rms(x, g):
    return x * lax.rsqrt(jnp.mean(x * x, axis=-1, keepdims=True) + EPS) * g


def _gelu(x):
    return 0.5 * x * (1.0 + jnp.tanh(math.sqrt(2.0 / math.pi) * (x + 0.044715 * (x * x * x))))


def _sigmoid(x):
    return 1.0 / (1.0 + jnp.exp(-x))


def _dot(a, b):
    return jnp.dot(a, b, preferred_element_type=F32)


def _dot_nt(a, b):
    return lax.dot_general(a, b, (((1,), (1,)), ((), ())), preferred_element_type=F32)


def _proj_kernel(x_ref, g_ref, w_ref, wg_ref, cos_ref, sp_ref, sm_ref,
                 u_ref, q_ref, kv_ref, win_ref, kvb_ref, gate_ref, *cmp_ref, token_minor):
    a = _rms(x_ref[...], g_ref[...]).astype(BF16)
    cos, sp, sm = cos_ref[...], sp_ref[...], sm_ref[...]

    def rope(blk):
        return blk * cos + pltpu.roll(blk, 8, 1) * sp + pltpu.roll(blk, LANES - 8, 1) * sm

    pairs = {}

    def lane_block(j):
        if j // 2 not in pairs:
            pairs[j // 2] = _dot(a, w_ref[:, (j // 2) * 2 * LANES:(j // 2 + 1) * 2 * LANES])
        return pairs[j // 2][:, (j % 2) * LANES:(j % 2 + 1) * LANES]

    if token_minor:
        tm = x_ref.shape[0]
        n_seq = u_ref.shape[1] // tm
        for s in range(SSM_WIDTH // LANES):
            u_ref[s, pl.ds(pl.program_id(1), tm, stride=n_seq), :] = lane_block(s)
    else:
        u_ref[...] = _dot(a, w_ref[:, 0:SSM_WIDTH])
    for j in range(NSA_WIDTH // LANES):
        q_ref[:, j * LANES:(j + 1) * LANES] = rope(lane_block(SSM_WIDTH // LANES + j)).astype(BF16)
    for j in range(6):
        blk = lane_block((SSM_WIDTH + NSA_WIDTH) // LANES + j)
        if j % 2 == 0:
            blk = rope(blk)
        out_ref, part = (kv_ref, j) if j < 4 else (win_ref, j - 4)
        if token_minor:
            out_ref[0, part] = blk.T
            if j < 2:
                cmp_ref[0][:, j * LANES:(j + 1) * LANES] = blk
        else:
            out_ref[:, part * LANES:(part + 1) * LANES] = blk
        kvb_ref[:, j * LANES:(j + 1) * LANES] = blk.astype(BF16)
    gate_ref[...] = _sigmoid(_dot(a, wg_ref[...]))


def _proj(x2d, g, w, wg, tabs, *, tm, n_tab_blocks, prompt):
    n = x2d.shape[0]
    nt = n_tab_blocks
    n_b = n // (nt * tm)
    row = lambda w_: pl.BlockSpec((tm, w_), lambda t, b: (b * nt + t, 0))
    full = lambda a: pl.BlockSpec(a.shape, lambda t, b: (0,) * a.ndim)
    tab = pl.BlockSpec((tm, LANES), lambda t, b: (t, 0))
    if prompt:
        t_len = nt * tm
        n_slabs = SSM_WIDTH // LANES
        u_shape = jax.ShapeDtypeStruct((n_slabs, t_len * n_b, LANES), F32)
        u_spec = pl.BlockSpec((n_slabs, tm * n_b, LANES), lambda t, b: (0, t, 0))
        tok_minor = lambda parts: pl.BlockSpec((1, parts, KV_WIDTH, tm), lambda t, b: (b, 0, 0, t))
        kv_specs = [tok_minor(4), tok_minor(2)]
        kv_shapes = [jax.ShapeDtypeStruct((n_b, 4, KV_WIDTH, t_len), F32),
                     jax.ShapeDtypeStruct((n_b, 2, KV_WIDTH, t_len), F32)]
        extra_specs = [row(2 * KV_WIDTH)]
        extra_shapes = [jax.ShapeDtypeStruct((n, 2 * KV_WIDTH), F32)]
    else:
        u_shape = jax.ShapeDtypeStruct((n, SSM_WIDTH), F32)
        u_spec = row(SSM_WIDTH)
        kv_specs = [row(4 * KV_WIDTH), row(2 * KV_WIDTH)]
        kv_shapes = [jax.ShapeDtypeStruct((n, 4 * KV_WIDTH), F32), jax.ShapeDtypeStruct((n, 2 * KV_WIDTH), F32)]
        extra_specs, extra_shapes = [], []
    return pl.pallas_call(
        functools.partial(_proj_kernel, token_minor=prompt),
        grid=(nt, n_b),
        in_specs=[row(D_MODEL), full(g), full(w), full(wg), tab, tab, tab],
        out_specs=[u_spec, row(NSA_WIDTH)] + kv_specs + [row(6 * KV_WIDTH), row(LANES)] + extra_specs,
        out_shape=[u_shape, jax.ShapeDtypeStruct((n, NSA_WIDTH), BF16)] + kv_shapes
                  + [jax.ShapeDtypeStruct((n, 6 * KV_WIDTH), BF16), jax.ShapeDtypeStruct((n, LANES), F32)]
                  + extra_shapes,
        compiler_params=_params("parallel", "arbitrary"),
        name="proj",
    )(x2d, g, w, wg, *tabs)


def _rope_tables(pos):
    half = ROT_DIM // 2
    freqs = ROPE_THETA ** (-jnp.arange(half, dtype=F32) / half)
    ang = pos.astype(F32)[:, None] * freqs[None, :]
    cos, sin = jnp.cos(ang), jnp.sin(ang)
    r = pos.shape[0]
    z8 = jnp.zeros((r, half), F32)
    rest0 = jnp.zeros((r, HEAD_DIM - ROT_DIM), F32)
    rest1 = jnp.ones((r, HEAD_DIM - ROT_DIM), F32)
    c64 = jnp.concatenate([cos, cos, rest1], axis=1)
    sp64 = jnp.concatenate([z8, sin, rest0], axis=1)
    sm64 = jnp.concatenate([-sin, z8, rest0], axis=1)
    return tuple(jnp.tile(t, (1, LANES // HEAD_DIM)) for t in (c64, sp64, sm64))


def _ssm_kernel(u_ref, h0_ref, lam_ref, bm_ref, cm_ref, d_ref, wglu_ref, bglu_ref,
                y_ref, hlast_ref, hs_ref, hstate_ref, *, tc):
    j = pl.program_id(1)

    @pl.when(j == 0)
    def _():
        hstate_ref[...] = h0_ref[0]

    n_slabs = SSM_WIDTH // LANES
    u = jnp.concatenate([u_ref[s] for s in range(n_slabs)], axis=1)
    ub = u.astype(BF16)
    half_in = SSM_WIDTH // 2
    chunk_lanes = 2 * STATE_CHUNK
    chunks_per_half = N_CHUNKS // 2
    ys = [None, None]
    for c in range(N_CHUNKS):
        h = c // chunks_per_half
        lo = (c % chunks_per_half) * chunk_lanes
        re0 = c * chunk_lanes
        im0 = re0 + STATE_CHUNK
        hs_ref[:, re0:re0 + chunk_lanes] = _dot(ub[:, h * half_in:(h + 1) * half_in],
                                                bm_ref[h, :, lo:lo + chunk_lanes])
        lr = jnp.broadcast_to(lam_ref[0:1, re0:re0 + STATE_CHUNK], (SUBLANES, STATE_CHUNK))
        li = jnp.broadcast_to(lam_ref[0:1, im0:im0 + STATE_CHUNK], (SUBLANES, STATE_CHUNK))
        hr = hstate_ref[:, re0:re0 + STATE_CHUNK]
        hi = hstate_ref[:, im0:im0 + STATE_CHUNK]
        for t in range(tc):
            r0 = t * SUBLANES
            hr, hi = (lr * hr - li * hi + hs_ref[r0:r0 + SUBLANES, re0:re0 + STATE_CHUNK],
                      lr * hi + li * hr + hs_ref[r0:r0 + SUBLANES, im0:im0 + STATE_CHUNK])
            hs_ref[r0:r0 + SUBLANES, re0:re0 + STATE_CHUNK] = hr
            hs_ref[r0:r0 + SUBLANES, im0:im0 + STATE_CHUNK] = hi
        hstate_ref[:, re0:re0 + STATE_CHUNK] = hr
        hstate_ref[:, im0:im0 + STATE_CHUNK] = hi
        part = _dot(hs_ref[:, re0:re0 + chunk_lanes].astype(BF16), cm_ref[h, lo:lo + chunk_lanes, :])
        ys[h] = part if ys[h] is None else ys[h] + part

    y = jnp.concatenate(ys, axis=1) + d_ref[...] * u
    y = _gelu(y)
    z = _dot(y.astype(BF16), wglu_ref[...]) + bglu_ref[...]
    out = y * _sigmoid(z)
    for s in range(n_slabs):
        y_ref[s] = out[:, s * LANES:(s + 1) * LANES]

    @pl.when(j == pl.num_programs(1) - 1)
    def _():
        hlast_ref[0] = hstate_ref[...]


def _ssm(u_tb, h0, lam, bm, cm, d, wglu, bglu, *, n_groups, n_time, tc):
    rows = tc * SUBLANES
    nt = n_time // tc
    n_slabs = SSM_WIDTH // LANES
    full = lambda a: pl.BlockSpec(a.shape, lambda g, j: (0,) * a.ndim)
    st = pl.BlockSpec((1, SUBLANES, 2 * N_STATE), lambda g, j: (g, 0, 0))
    slabs = pl.BlockSpec((n_slabs, rows, LANES), lambda g, j: (0, g * nt + j, 0))
    return pl.pallas_call(
        functools.partial(_ssm_kernel, tc=tc),
        grid=(n_groups, nt),
        in_specs=[slabs, st, full(lam), full(bm), full(cm), full(d), full(wglu), full(bglu)],
        out_specs=[slabs, st],
        out_shape=[jax.ShapeDtypeStruct((n_slabs, n_groups * n_time * SUBLANES, LANES), F32),
                   jax.ShapeDtypeStruct((n_groups, SUBLANES, 2 * N_STATE), F32)],
        scratch_shapes=[pltpu.VMEM((rows, 2 * N_STATE), F32), pltpu.VMEM((SUBLANES, 2 * N_STATE), F32)],
        compiler_params=_params("parallel", "arbitrary"),
        name="ssm",
    )(u_tb, h0, lam, bm, cm, d, wglu, bglu)


def _state_lanes(re, im):
    lead = re.shape[:-1]
    r = re.reshape(lead + (N_CHUNKS, 1, STATE_CHUNK))
    i = im.reshape(lead + (N_CHUNKS, 1, STATE_CHUNK))
    return jnp.concatenate([r, i], axis=-2).reshape(lead + (2 * N_STATE,))


def _state_unlanes(x):
    lead = x.shape[:-1]
    y = x.reshape(lead + (N_CHUNKS, 2, STATE_CHUNK))
    return y[..., 0, :].reshape(lead + (N_STATE,)), y[..., 1, :].reshape(lead + (N_STATE,))


def _ssm_params(lam_re, lam_im, log_dt, b_re, b_im, c_re, c_im):
    lam = lax.complex(lam_re.astype(F32), lam_im.astype(F32))
    dt = jnp.exp(log_dt.astype(F32))[:, None]
    lam_bar = jnp.exp(lam * dt)
    b = lax.complex(b_re.astype(F32), b_im.astype(F32))
    b_bar = ((lam_bar - 1.0) / lam)[..., None] * b
    eye = jnp.eye(SSM_GROUPS, dtype=F32)

    def in_blockdiag(x):
        return jnp.einsum('gpc,gh->gchp', x, eye).reshape(SSM_WIDTH, N_STATE)

    def out_blockdiag(x):
        return jnp.einsum('gcp,gh->gphc', x, eye).reshape(N_STATE, SSM_WIDTH)

    b_full = _state_lanes(in_blockdiag(jnp.real(b_bar)), in_blockdiag(jnp.imag(b_bar)))
    c_full = _state_lanes(out_blockdiag(c_re.astype(F32)).T, -out_blockdiag(c_im.astype(F32)).T).T
    hw, hs = SSM_WIDTH // 2, N_STATE
    bm = jnp.stack([b_full[h * hw:(h + 1) * hw, h * hs:(h + 1) * hs] for h in range(2)]).astype(BF16)
    cm = jnp.stack([c_full[h * hs:(h + 1) * hs, h * hw:(h + 1) * hw] for h in range(2)]).astype(BF16)
    lam_l = _state_lanes(jnp.real(lam_bar).reshape(1, N_STATE), jnp.imag(lam_bar).reshape(1, N_STATE))
    return lam_l, bm, cm


def _compress_kernel(x_ref, pe_ref, w1_ref, w2_ref, o_ref, *, tm):
    def body(sp, acc):
        s0 = 2 * sp
        xa = x_ref[pl.ds(s0, tm, stride=BLK), :] + pe_ref[0, pl.ds(s0, 1), :]
        xb = x_ref[pl.ds(s0 + 1, tm, stride=BLK), :] + pe_ref[0, pl.ds(s0 + 1, 1), :]
        lhs = jnp.concatenate([xa, xb], axis=1).astype(BF16)
        return acc + _dot(lhs, w1_ref[0, sp])

    acc = lax.fori_loop(0, BLK // 2, body, jnp.zeros((tm, KV_WIDTH), F32))
    o_ref[0] = _dot(_gelu(acc).astype(BF16), w2_ref[0])


def _compress(x2d, pe, w1, w2, *, n_blocks, tm):
    return pl.pallas_call(
        functools.partial(_compress_kernel, tm=tm),
        grid=(2, n_blocks // tm),
        in_specs=[pl.BlockSpec((tm * BLK, KV_WIDTH), lambda c, i: (i, c)),
                  pl.BlockSpec((1, BLK, KV_WIDTH), lambda c, i: (c, 0, 0)),
                  pl.BlockSpec((1, BLK // 2, 2 * KV_WIDTH, KV_WIDTH), lambda c, i: (c, 0, 0, 0)),
                  pl.BlockSpec((1, KV_WIDTH, KV_WIDTH), lambda c, i: (c, 0, 0))],
        out_specs=pl.BlockSpec((1, tm, KV_WIDTH), lambda c, i: (c, i, 0)),
        out_shape=jax.ShapeDtypeStruct((2, n_blocks, KV_WIDTH), F32),
        compiler_params=_params("parallel", "parallel"),
        name="compress",
    )(x2d, pe, w1, w2)


def _compress_params(pe_k, w1_k, w2_k, pe_v, w1_v, w2_v):
    def bd(w):
        z = jnp.zeros_like(w)
        return jnp.concatenate([jnp.concatenate([w, z], axis=-1), jnp.concatenate([z, w], axis=-1)], axis=-2)

    def one(pe, w1, w2):
        w1s = bd(w1.astype(F32).reshape(BLK, HEAD_DIM, HEAD_DIM))
        return (jnp.tile(pe.astype(F32), (1, NSA_KV)),
                w1s.reshape(BLK // 2, 2 * KV_WIDTH, KV_WIDTH).astype(BF16),
                bd(w2.astype(F32)).astype(BF16))

    k, v = one(pe_k, w1_k, w2_k), one(pe_v, w1_v, w2_v)
    return tuple(jnp.stack([a, b]) for a, b in zip(k, v))


def _compress_paged_kernel(pt_ref, cache_ref, pe_ref, w1_ref, w2_ref, o_ref, buf_ref, sems, *, m):
    step = pl.program_id(0)

    def page_copy(s, j):
        slot = s % 2
        row0 = pl.multiple_of(j * PAGE_PITCH, SUBLANES)
        return pltpu.make_async_copy(cache_ref.at[pt_ref[s * m + j], pl.ds(0, 2)],
                                     buf_ref.at[pl.ds(2 * slot, 2), pl.ds(row0, PAGE_SIZE), :],
                                     sems.at[slot])

    def start_all(s):
        def start(j, _):
            page_copy(s, j).start()
            return 0
        lax.fori_loop(0, m, start, 0, unroll=8)

    @pl.when(step == 0)
    def _():
        start_all(step)

    @pl.when(step + 1 < pl.num_programs(0))
    def _():
        start_all(step + 1)

    def wait(j, _):
        page_copy(step, j).wait()
        return 0

    lax.fori_loop(0, m, wait, 0, unroll=8)

    for c in range(2):
        tile = 2 * (step % 2) + c
        res = []
        for kv in range(NSA_KV):
            def body(dp, acc, c=c, kv=kv, tile=tile):
                d0 = 2 * dp
                r0 = kv * HEAD_DIM + d0
                xa = buf_ref[tile, pl.ds(r0, m, stride=PAGE_PITCH), :] + pe_ref[c, pl.ds(d0, 1), :]
                xb = buf_ref[tile, pl.ds(r0 + 1, m, stride=PAGE_PITCH), :] + pe_ref[c, pl.ds(d0 + 1, 1), :]
                lhs = jnp.concatenate([xa, xb], axis=1).astype(BF16)
                return acc + _dot(lhs, w1_ref[c, dp])

            acc = lax.fori_loop(0, HEAD_DIM // 2, body, jnp.zeros((m, PAGE_SIZE), F32), unroll=4)
            res.append(_dot(_gelu(acc).astype(BF16), w2_ref[c]))
        for blk in range(PAGE_SIZE // BLK):
            cols = slice(blk * HEAD_DIM, (blk + 1) * HEAD_DIM)
            o_ref[c, pl.ds(blk, m, stride=PAGE_SIZE // BLK), :] = jnp.concatenate(
                [r[:, cols] for r in res], axis=1)


def _compress_paged(pt_flat, cache_t, pe_t, w1_t, w2, *, m):
    n = pt_flat.shape[0]
    full = lambda a: pl.BlockSpec(a.shape, lambda i, pt: (0,) * a.ndim)
    grid_spec = pltpu.PrefetchScalarGridSpec(
        num_scalar_prefetch=1,
        grid=(n // m,),
        in_specs=[pl.BlockSpec(memory_space=pl.ANY), full(pe_t), full(w1_t), full(w2)],
        out_specs=pl.BlockSpec((2, m * (PAGE_SIZE // BLK), KV_WIDTH), lambda i, pt: (0, i, 0)),
        scratch_shapes=[pltpu.VMEM((4, m * PAGE_PITCH, PAGE_SIZE), F32), pltpu.SemaphoreType.DMA((2,))])
    return pl.pallas_call(
        functools.partial(_compress_paged_kernel, m=m),
        grid_spec=grid_spec,
        out_shape=jax.ShapeDtypeStruct((2, n * (PAGE_SIZE // BLK), KV_WIDTH), F32),
        compiler_params=_params("arbitrary"),
        name="compress_paged",
    )(pt_flat, cache_t, pe_t, w1_t, w2)


def _compress_paged_params(pe_k, w1_k, pe_v, w1_v):
    def bd(w):
        z = jnp.zeros_like(w)
        return jnp.concatenate([jnp.concatenate([w, z], axis=-1), jnp.concatenate([z, w], axis=-1)], axis=-2)

    def one(pe, w1):
        w1d = bd(w1.astype(F32).reshape(BLK, HEAD_DIM, HEAD_DIM).transpose(1, 0, 2))
        return (jnp.tile(pe.astype(F32).T, (1, PAGE_SIZE // BLK)),
                w1d.reshape(HEAD_DIM // 2, 2 * PAGE_SIZE, PAGE_SIZE).astype(BF16))

    k, v = one(pe_k, w1_k), one(pe_v, w1_v)
    return tuple(jnp.stack([a, b]) for a, b in zip(k, v))


def _stack_queries(q, nq):
    q = q.astype(F32)
    z = jnp.zeros((nq, HEAD_DIM), F32)
    rows = []
    for h in range(NSA_HEADS):
        blk = q[:, h * HEAD_DIM:(h + 1) * HEAD_DIM]
        rows.append(jnp.concatenate([blk, z] if h < NSA_REP else [z, blk], axis=1))
    return (jnp.concatenate(rows, axis=0) * (HEAD_DIM ** -0.5)).astype(BF16)


def _masked_softmax(s, valid):
    s = jnp.where(valid, s, NEG_INF)
    m = jnp.max(s, axis=-1, keepdims=True)
    p = jnp.exp(s - m) * valid.astype(F32)
    return p / jnp.maximum(jnp.sum(p, axis=-1, keepdims=True), 1e-30)


def _select_blocks(imp, n_ids, cur, nb):
    forced = (n_ids == 0) | (n_ids == cur) | (n_ids == cur - 1)
    imp = jnp.where(forced, FORCE_SCORE, imp)
    imp = jnp.where(n_ids <= cur, imp, -FORCE_SCORE)
    rank = jnp.zeros(imp.shape, F32)
    for m in range(nb):
        col = imp[:, m:m + 1]
        beats = (col > imp) | ((col == imp) & (n_ids > m))
        rank = rank + beats.astype(F32)
    return (rank < float(N_SEL)).astype(F32)


def _select_blocks_t(imp_t, cur, nb):
    n_t = lax.broadcasted_iota(jnp.int32, imp_t.shape, 0)
    forced = (n_t == 0) | (n_t == cur) | (n_t == cur - 1)
    imp_t = jnp.where(forced, FORCE_SCORE, imp_t)
    imp_t = jnp.where(n_t <= cur, imp_t, -FORCE_SCORE)
    rank = jnp.zeros(imp_t.shape, F32)
    for m in range(nb):
        row = imp_t[m:m + 1, :]
        beats = (row > imp_t) | ((row == imp_t) & (n_t > m))
        rank = rank + beats.astype(F32)
    return ((rank < float(N_SEL)) & (n_t <= cur)).astype(F32)


def _combine_heads(gates, o_c, o_s, o_w, nq):
    outs = []
    for h in range(NSA_HEADS):
        rows = slice(h * nq, (h + 1) * nq)
        o = (gates[:, 3 * h:3 * h + 1] * o_c[rows] + gates[:, 3 * h + 1:3 * h + 2] * o_s[rows]
             + gates[:, 3 * h + 2:3 * h + 3] * o_w[rows])
        g = h // NSA_REP
        outs.append(o[:, g * HEAD_DIM:(g + 1) * HEAD_DIM])
    return jnp.concatenate(outs, axis=1)


def _nsa_pair_kernel(qa_ref, qb_ref, ga_ref, gb_ref, kc_ref, vc_ref, ks_ref, vs_ref, kw_ref, vw_ref,
                     et_ref, tri_ref, wm_ref, oa_ref, ob_ref, q2_ref, etd_ref, s_ref, *, nb, ch, wk):
    i = pl.program_id(1)
    nq = BLK
    rows = NSA_HEADS * nq
    bpc = ch // BLK
    n_units = nb // bpc + 1
    chunk_of = (i, nb - 1 - i)
    n_a = i // bpc + 1
    row_q = lax.broadcasted_iota(jnp.int32, (rows, LANES), 0) % nq
    rq_minus_lane = row_q - lax.broadcasted_iota(jnp.int32, (rows, LANES), 1)

    def col_blocks(s):
        return [s[:, j * LANES:(j + 1) * LANES] for j in range(s.shape[1] // LANES)]

    def col_max(cols):
        m = cols[0]
        for c_ in cols[1:]:
            m = jnp.maximum(m, c_)
        return m

    def finish(acc):
        return acc[:, 0:KV_WIDTH] / jnp.maximum(acc[:, KV_WIDTH:2 * KV_WIDTH], 1e-30)

    def with_ones(v):
        return jnp.concatenate([v, jnp.ones((v.shape[0], LANES), BF16)], axis=1)

    q_onehot = jnp.where(rq_minus_lane == -BLK, NEG_INF, 0.0)

    def prepare(idx, q_ref):
        ci = chunk_of[idx]
        q2 = _stack_queries(q_ref[...], nq)
        first = jnp.maximum(ci - WINDOW // BLK, 0) // (LANES // BLK)
        w0 = pl.multiple_of(first * LANES, LANES)
        delta = ci - first * (LANES // BLK)
        lhs = jnp.concatenate([q2, q_onehot.astype(BF16)], axis=1)
        rhs = jnp.concatenate([kw_ref[pl.ds(w0, wk), :], wm_ref[delta]], axis=1)
        cols = col_blocks(_dot_nt(lhs, rhs))
        m_w = jnp.maximum(jnp.max(col_max(cols), axis=-1, keepdims=True), M_INIT)
        m_w = jnp.broadcast_to(m_w, (rows, LANES))
        p = jnp.concatenate([jnp.exp(c_ - m_w).astype(BF16) for c_ in cols], axis=1)
        o_w = finish(_dot(p, with_ones(vw_ref[pl.ds(w0, wk), :])))

        q_pos = ci * BLK + lax.broadcasted_iota(jnp.int32, (rows, 1), 0) % nq
        n_ids = lax.broadcasted_iota(jnp.int32, (rows, nb), 1)
        s_c = _dot_nt(q2, kc_ref[0].astype(BF16))
        p_c = _masked_softmax(s_c, (n_ids + 1) * BLK - 1 <= q_pos)
        o_c = _dot(p_c.astype(BF16), vc_ref[0].astype(BF16))

        imps = []
        for g in range(NSA_KV):
            imp = p_c[g * NSA_REP * nq:(g * NSA_REP + 1) * nq]
            for r in range(1, NSA_REP):
                imp = imp + p_c[(g * NSA_REP + r) * nq:(g * NSA_REP + r + 1) * nq]
            imps.append(imp)
        imp2 = jnp.concatenate([jnp.concatenate(imps, axis=0), jnp.zeros((LANES, LANES - nb), F32)], axis=1)
        sel_t = _select_blocks_t(imp2.T[0:nb], ci, nb)
        sel2 = jnp.concatenate([sel_t, jnp.ones((LANES - nb, LANES), F32)], axis=0).T
        neg = (sel2 - 1.0) * (-NEG_INF)
        neg_rows = jnp.concatenate([neg[g * nq:(g + 1) * nq] for g in range(NSA_KV) for _ in range(NSA_REP)],
                                   axis=0)
        q2_ref[idx] = jnp.concatenate([q2, (neg_rows + q_onehot).astype(BF16)], axis=1)
        in_diag = lax.broadcasted_iota(jnp.int32, (ch, LANES), 0) // BLK == ci % bpc
        etd_ref[idx] = et_ref[ci // bpc] + jnp.where(in_diag, tri_ref[...], jnp.zeros((), BF16))
        return o_c, o_w

    o_ca, o_wa = prepare(0, qa_ref)
    o_cb, o_wb = prepare(1, qb_ref)

    n_a_max = (nb // 2 - 1) // bpc + 1
    mx = [jnp.full((rows, LANES), M_INIT, F32)] * 2
    units = []
    for u in range(n_units):
        which = 0 if u == 0 else 1 if u >= n_a_max else (u >= n_a).astype(jnp.int32)
        static = isinstance(which, int)
        kc = u - which * n_a
        k0 = kc * ch if isinstance(kc, int) else pl.multiple_of(kc * ch, ch)
        units.append((which, k0))
        owner = chunk_of[which] if static else jnp.where(which == 1, chunk_of[1], chunk_of[0])
        key_mask = jnp.where(kc == owner // bpc, etd_ref[which], et_ref[kc])
        rhs = jnp.concatenate([ks_ref[pl.ds(k0, ch), :], key_mask], axis=1)
        cols = col_blocks(_dot_nt(q2_ref[which], rhs))
        cm = col_max(cols)
        if static:
            mx[which] = jnp.maximum(mx[which], cm)
        else:
            mx = [jnp.maximum(mx[0], jnp.where(which == 0, cm, M_INIT)),
                  jnp.maximum(mx[1], jnp.where(which == 1, cm, M_INIT))]
        for j, c_ in enumerate(cols):
            s_ref[u, :, j * LANES:(j + 1) * LANES] = c_

    m = [jnp.broadcast_to(jnp.max(mx_i, axis=-1, keepdims=True), (rows, LANES)) for mx_i in mx]
    acc = [None, None]

    def add(total, part):
        return part if total is None else total + part

    for u, (which, k0) in enumerate(units):
        static = isinstance(which, int)
        m_u = m[which] if static else jnp.where(which == 1, m[1], m[0])
        p = jnp.concatenate([jnp.exp(s_ref[u, :, j * LANES:(j + 1) * LANES] - m_u).astype(BF16)
                             for j in range(ch // LANES)], axis=1)
        part = _dot(p, with_ones(vs_ref[pl.ds(k0, ch), :]))
        if static:
            acc[which] = add(acc[which], part)
        else:
            w_b = (which == 1).astype(F32)
            acc = [add(acc[0], part * (1.0 - w_b)), add(acc[1], part * w_b)]

    oa_ref[...] = _combine_heads(ga_ref[...], o_ca, finish(acc[0]), o_wa, nq).astype(oa_ref.dtype)
    ob_ref[...] = _combine_heads(gb_ref[...], o_cb, finish(acc[1]), o_wb, nq).astype(ob_ref.dtype)


def _nsa_prompt(q, gates, cmp_kv, kvb, *, n_seq, seq):
    nb = seq // BLK
    rows = NSA_HEADS * BLK
    ch = min(SEL_CHUNK, seq)
    wk = min(WINDOW + LANES, seq)
    bpc = ch // BLK
    assert seq % ch == 0 and wk % LANES == 0 and nb % 2 == 0 and nb <= LANES
    half = nb // 2
    n_units = nb // bpc + 1
    assert nb <= BLK
    et = np.zeros((seq // ch, ch, LANES), np.float32)
    for c in range(seq // ch):
        et[c, np.arange(ch), (c * ch + np.arange(ch)) // BLK] = 1.0
    key = np.arange(ch)[:, None]
    qry = np.arange(BLK)[None, :]
    tri = np.zeros((ch, LANES), np.float32)
    tri[:, BLK:] = (key % BLK) > qry
    n_delta = WINDOW // BLK + LANES // BLK
    wm = np.zeros((n_delta, wk, LANES), np.float32)
    for dl in range(n_delta):
        dist = dl * BLK + qry - np.arange(wk)[:, None]
        wm[dl, :, BLK:] = (dist < 0) | (dist >= WINDOW)
    chunk_a = lambda w_: pl.BlockSpec((BLK, w_), lambda b, i: (b * nb + i, 0))
    chunk_b = lambda w_: pl.BlockSpec((BLK, w_), lambda b, i: (b * nb + nb - 1 - i, 0))
    out_spec = pl.BlockSpec((BLK, NSA_WIDTH), lambda b, i: (b * half + i, 0))
    kv = lambda col: pl.BlockSpec((seq, KV_WIDTH), lambda b, i: (b, col))
    cmp_ = lambda which: pl.BlockSpec((1, nb, KV_WIDTH), lambda b, i: (which, b, 0))
    out = jax.ShapeDtypeStruct((n_seq * half * BLK, NSA_WIDTH), BF16)
    o_a, o_b = pl.pallas_call(
        functools.partial(_nsa_pair_kernel, nb=nb, ch=ch, wk=wk),
        grid=(n_seq, half),
        in_specs=[chunk_a(NSA_WIDTH), chunk_b(NSA_WIDTH), chunk_a(LANES), chunk_b(LANES),
                  cmp_(0), cmp_(1), kv(2), kv(3), kv(4), kv(5),
                  pl.BlockSpec(et.shape, lambda b, i: (0, 0, 0)),
                  pl.BlockSpec(tri.shape, lambda b, i: (0, 0)),
                  pl.BlockSpec(wm.shape, lambda b, i: (0, 0, 0))],
        out_specs=[out_spec, out_spec],
        out_shape=[out, out],
        scratch_shapes=[pltpu.VMEM((2, rows, 2 * LANES), BF16),
                        pltpu.VMEM((2, ch, LANES), BF16),
                        pltpu.VMEM((n_units, rows, ch), F32)],
        compiler_params=_params("parallel", "arbitrary"),
        name="nsa_prompt",
    )(q, q, gates, gates, cmp_kv, cmp_kv, kvb, kvb, kvb, kvb,
      jnp.asarray(et, dtype=BF16), jnp.asarray(tri, dtype=BF16), jnp.asarray(wm, dtype=BF16))
    return o_a, o_b


def _unmirror(o_a, o_b, n_seq):
    half = o_a.shape[0] // (n_seq * BLK)
    o_a = o_a.reshape(n_seq, half, BLK, NSA_WIDTH)
    o_b = o_b.reshape(n_seq, half, BLK, NSA_WIDTH)[:, ::-1]
    return jnp.concatenate([o_a, o_b], axis=1).reshape(-1, NSA_WIDTH)


def _nsa_sample_kernel(pt_ref, q_ref, gate_ref, kc_ref, vc_ref, cache_ref, nks_ref, nvs_ref,
                       wt_ref, nkw_ref, nvw_ref, e_ref, o_ref, kv_buf, s_scr, sems,
                       *, nq, past, nbp, tk):
    b = pl.program_id(0)
    n_pages = past // PAGE_SIZE
    rows = NSA_HEADS * nq

    def page_copy(s, j):
        slot = s % 2
        k0 = pl.multiple_of(j * PAGE_SIZE, PAGE_SIZE)
        return pltpu.make_async_copy(cache_ref.at[pt_ref[s * n_pages + j], pl.ds(2, 2)],
                                     kv_buf.at[pl.ds(2 * slot, 2), :, pl.ds(k0, PAGE_SIZE)], sems.at[slot])

    def start_all(s):
        def start(j, _):
            page_copy(s, j).start()
            return 0
        lax.fori_loop(0, n_pages, start, 0, unroll=8)

    @pl.when(b == 0)
    def _():
        start_all(b)

    @pl.when(b + 1 < pl.num_programs(0))
    def _():
        start_all(b + 1)

    slot = b % 2
    q2 = _stack_queries(q_ref[0], nq)
    q_pos = past + lax.broadcasted_iota(jnp.int32, (rows, 1), 0) % nq
    cur = past // BLK

    n_ids = lax.broadcasted_iota(jnp.int32, (rows, nbp), 1)
    s_c = _dot_nt(q2, kc_ref[0].astype(BF16))
    p_c = _masked_softmax(s_c, (n_ids + 1) * BLK - 1 <= q_pos)
    o_c = _dot(p_c.astype(BF16), vc_ref[0].astype(BF16))

    n_sel = lax.broadcasted_iota(jnp.int32, (nq, nbp), 1)
    sels = []
    for g in range(NSA_KV):
        imp = p_c[g * NSA_REP * nq:(g * NSA_REP + 1) * nq]
        for r in range(1, NSA_REP):
            imp = imp + p_c[(g * NSA_REP + r) * nq:(g * NSA_REP + r + 1) * nq]
        sel = _select_blocks(imp, n_sel, cur, cur + 1)
        sels.extend([sel] * NSA_REP)
    keys = _dot(jnp.concatenate(sels, axis=0).astype(BF16), e_ref[...])
    bias = (keys - 1.0) * (-NEG_INF)

    def col_max(s):
        m = s[:, 0:LANES]
        for j in range(1, s.shape[1] // LANES):
            m = jnp.maximum(m, s[:, j * LANES:(j + 1) * LANES])
        return m

    def col_sum(p):
        t = p[:, 0:LANES]
        for j in range(1, p.shape[1] // LANES):
            t = t + p[:, j * LANES:(j + 1) * LANES]
        return t

    def row_max(mx):
        return jnp.broadcast_to(jnp.max(mx, axis=-1, keepdims=True), mx.shape)

    def tiled(m, width):
        return m if width == LANES else jnp.concatenate([m] * (width // LANES), axis=1)

    new_pos = past + lax.broadcasted_iota(jnp.int32, (rows, LANES), 1)

    def wait(j, _):
        page_copy(b, j).wait()
        return 0

    lax.fori_loop(0, n_pages, wait, 0, unroll=8)

    mx = jnp.full((rows, LANES), M_INIT, F32)
    for t in range(past // tk):
        s = _dot(q2, kv_buf[2 * slot, :, t * tk:(t + 1) * tk].astype(BF16)) + bias[:, t * tk:(t + 1) * tk]
        s_scr[:, t * tk:(t + 1) * tk] = s
        mx = jnp.maximum(mx, col_max(s))
    s_new = _dot_nt(q2, nks_ref[0].astype(BF16)) + bias[:, past:past + LANES]
    s_new = jnp.where(new_pos <= q_pos, s_new, NEG_INF)
    m_s = row_max(jnp.maximum(mx, s_new))
    p_new = jnp.exp(s_new - m_s)
    acc = _dot(p_new.astype(BF16), nvs_ref[0].astype(BF16))
    lsum = p_new
    for t in range(past // tk):
        p = jnp.exp(s_scr[:, t * tk:(t + 1) * tk] - tiled(m_s, tk))
        lsum = lsum + col_sum(p)
        acc = acc + _dot_nt(p.astype(BF16), kv_buf[2 * slot + 1, :, t * tk:(t + 1) * tk].astype(BF16))
    o_s = acc / jnp.maximum(jnp.sum(lsum, axis=-1, keepdims=True), 1e-30)

    wlen = wt_ref.shape[3]
    w_pos = past - wlen + lax.broadcasted_iota(jnp.int32, (rows, wlen), 1)
    d = q_pos - w_pos
    s_w = _dot(q2, wt_ref[0, 0].astype(BF16))
    s_w = jnp.where((d >= 0) & (d < WINDOW) & (w_pos >= 0), s_w, NEG_INF)
    d = q_pos - new_pos
    s_nw = _dot_nt(q2, nkw_ref[0].astype(BF16))
    s_nw = jnp.where((d >= 0) & (d < WINDOW), s_nw, NEG_INF)
    m_w = row_max(jnp.maximum(jnp.maximum(col_max(s_w), s_nw), M_INIT))
    p_w = jnp.exp(s_w - tiled(m_w, wlen))
    p_nw = jnp.exp(s_nw - m_w)
    acc = _dot_nt(p_w.astype(BF16), wt_ref[0, 1].astype(BF16)) + _dot(p_nw.astype(BF16), nvw_ref[0].astype(BF16))
    o_w = acc / jnp.maximum(jnp.sum(col_sum(p_w) + p_nw, axis=-1, keepdims=True), 1e-30)

    o_ref[0] = _combine_heads(gate_ref[0], o_c, o_s, o_w, nq).astype(o_ref.dtype)


def _nsa_sample(pt_flat, q3, gates3, kc, vc, cache_t, new_rows, win_t, expand, *, past, tk):
    n_seq, nq, _ = q3.shape
    nbp = kc.shape[1]
    rows = NSA_HEADS * nq
    per_seq = lambda a: pl.BlockSpec((1,) + a.shape[1:], lambda b, pt: (b,) + (0,) * (a.ndim - 1))
    new = lambda col: pl.BlockSpec((1, LANES, KV_WIDTH), lambda b, pt: (b, 0, col))
    grid_spec = pltpu.PrefetchScalarGridSpec(
        num_scalar_prefetch=1,
        grid=(n_seq,),
        in_specs=[per_seq(q3), per_seq(gates3), per_seq(kc), per_seq(vc),
                  pl.BlockSpec(memory_space=pl.ANY),
                  new(2), new(3), per_seq(win_t), new(4), new(5),
                  pl.BlockSpec(expand.shape, lambda b, pt: (0, 0))],
        out_specs=pl.BlockSpec((1, nq, NSA_WIDTH), lambda b, pt: (b, 0, 0)),
        scratch_shapes=[pltpu.VMEM((4, KV_WIDTH, past), F32),
                        pltpu.VMEM((rows, past), F32), pltpu.SemaphoreType.DMA((2,))])
    return pl.pallas_call(
        functools.partial(_nsa_sample_kernel, nq=nq, past=past, nbp=nbp, tk=tk),
        grid_spec=grid_spec,
        out_shape=jax.ShapeDtypeStruct((n_seq, nq, NSA_WIDTH), F32),
        compiler_params=_params("arbitrary"),
        name="nsa_sample",
    )(pt_flat, q3, gates3, kc, vc, cache_t, new_rows, new_rows, win_t, new_rows, new_rows, expand)


def _expand_matrix(nb, n_keys):
    return jnp.asarray(np.arange(n_keys)[None, :] // BLK == np.arange(nb)[:, None], dtype=BF16)


def _merge_kernel(x_ref, ssm_ref, *refs, mirrored):
    nsa_refs = refs[:2] if mirrored else refs[:1]
    gpre_ref, wm_ref, wbs_ref, wbn_ref, wo_ref, gpost_ref, gx_ref, wxq_ref, x1_ref, qx_ref = refs[len(nsa_refs):]
    x = x_ref[...]
    tm = x.shape[0]
    if mirrored:
        upper = nsa_refs[1][...]
        n_chunks = tm // BLK
        upper = jnp.concatenate([upper[(n_chunks - 1 - s) * BLK:(n_chunks - s) * BLK] for s in range(n_chunks)],
                                axis=0)
        nsa = jnp.where(pl.program_id(0) < pl.num_programs(0) // 2, nsa_refs[0][...], upper)
    else:
        nsa = nsa_refs[0][...]
    n_seq = ssm_ref.shape[1] // tm
    ssm = jnp.concatenate([ssm_ref[s, pl.ds(pl.program_id(1), tm, stride=n_seq), :]
                           for s in range(SSM_WIDTH // LANES)], axis=1)
    a = _rms(x, gpre_ref[...]).astype(BF16)
    g_ssm = _sigmoid(_dot(a, wm_ref[:, 0:D_MODEL]))
    g_nsa = _sigmoid(_dot(a, wm_ref[:, D_MODEL:2 * D_MODEL]))
    merged = (g_ssm * _dot(ssm.astype(BF16), wbs_ref[...])
              + g_nsa * _dot(nsa, wbn_ref[...]))
    x1 = x + _rms(_dot(merged.astype(BF16), wo_ref[...]), gpost_ref[...])
    x1_ref[...] = x1
    c = _rms(x1, gx_ref[...]).astype(BF16)
    qx_ref[...] = (_dot(c, wxq_ref[...]) * (X_HEAD_DIM ** -0.5)).astype(BF16)


def _merge(x2d, ssm_slabs, nsa_o, weights, *, tm, n_tiles):
    n = x2d.shape[0]
    n_b = n // (n_tiles * tm)
    row = lambda w_: pl.BlockSpec((tm, w_), lambda t, b: (b * n_tiles + t, 0))
    full = lambda a: pl.BlockSpec(a.shape, lambda t, b: (0,) * a.ndim, pipeline_mode=pl.Buffered(1))
    ssm_spec = pl.BlockSpec((SSM_WIDTH // LANES, tm * n_b, LANES), lambda t, b: (0, t, 0))
    mirrored = isinstance(nsa_o, tuple)
    if mirrored:
        half = n_tiles // 2
        nsa_specs = [pl.BlockSpec((tm, NSA_WIDTH), lambda t, b: (b * half + jnp.minimum(t, half - 1), 0)),
                     pl.BlockSpec((tm, NSA_WIDTH), lambda t, b: (b * half + jnp.minimum(n_tiles - 1 - t, half - 1), 0))]
        nsa_args = list(nsa_o)
    else:
        nsa_specs, nsa_args = [row(NSA_WIDTH)], [nsa_o]
    return pl.pallas_call(
        functools.partial(_merge_kernel, mirrored=mirrored),
        grid=(n_tiles, n_b),
        in_specs=[row(D_MODEL), ssm_spec] + nsa_specs + [full(w) for w in weights],
        out_specs=[row(D_MODEL), row(X_WIDTH)],
        out_shape=[jax.ShapeDtypeStruct((n, D_MODEL), F32), jax.ShapeDtypeStruct((n, X_WIDTH), BF16)],
        compiler_params=_params("parallel", "arbitrary"),
        name="merge",
    )(x2d, ssm_slabs, *nsa_args, *weights)


def _xattn_tile(q, kv_ref, m_len):
    outs = []
    for h in range(X_HEADS):
        cols = slice(h * X_HEAD_DIM, (h + 1) * X_HEAD_DIM)
        k = kv_ref[0, pl.ds(h, m_len, stride=2 * X_HEADS), :].astype(BF16)
        v = kv_ref[0, pl.ds(X_HEADS + h, m_len, stride=2 * X_HEADS), :].astype(BF16)
        s = _dot_nt(q[:, cols], k)
        m = jnp.max(s, axis=-1, keepdims=True)
        p = jnp.exp(s - m)
        p = p / jnp.sum(p, axis=-1, keepdims=True)
        outs.append(_dot(p.astype(BF16), v))
    return jnp.concatenate(outs, axis=1)


def _xattn_kernel(q_ref, kv_ref, o_ref, *, m_len):
    o_ref[0] = _xattn_tile(q_ref[0].astype(BF16), kv_ref, m_len).astype(o_ref.dtype)


def _xattn(q3, mem_kv_rows, *, tq):
    n_seq, t, _ = q3.shape
    m_len = mem_kv_rows.shape[1] // (2 * X_HEADS)
    return pl.pallas_call(
        functools.partial(_xattn_kernel, m_len=m_len),
        grid=(n_seq, t // tq),
        in_specs=[pl.BlockSpec((1, tq, X_WIDTH), lambda b, i: (b, i, 0)),
                  pl.BlockSpec((1, m_len * 2 * X_HEADS, X_HEAD_DIM), lambda b, i: (b, 0, 0))],
        out_specs=pl.BlockSpec((1, tq, X_WIDTH), lambda b, i: (b, i, 0)),
        out_shape=jax.ShapeDtypeStruct((n_seq, t, X_WIDTH), q3.dtype),
        compiler_params=_params("parallel", "parallel"),
        name="xattn",
    )(q3, mem_kv_rows)


def _mlp_kernel(x1_ref, o_ref, *refs, m_len):
    if m_len:
        kv_ref, refs = refs[0], refs[1:]
        o = _xattn_tile(o_ref[...], kv_ref, m_len).astype(BF16)
    else:
        o = o_ref[...]
    wxo_ref, gxp_ref, gm_ref, wup_ref, wdn_ref, gmp_ref, y_ref = refs
    x2 = x1_ref[...] + _rms(_dot(o, wxo_ref[...]), gxp_ref[...])
    m = _rms(x2, gm_ref[...]).astype(BF16)
    hid = jnp.maximum(_dot(m, wup_ref[...]), 0.0)
    hid = (hid * hid).astype(BF16)
    y_ref[...] = x2 + _rms(_dot(hid, wdn_ref[...]), gmp_ref[...])


def _mlp(x1, o, weights, *, tm, mem_kv_rows=None, tiles_per_seq=1):
    n = x1.shape[0]
    row = lambda w_: pl.BlockSpec((tm, w_), lambda i: (i, 0))
    full = lambda a: pl.BlockSpec(a.shape, lambda i: (0,) * a.ndim, pipeline_mode=pl.Buffered(1))
    m_len, kv_specs, kv_args = 0, [], []
    if mem_kv_rows is not None:
        m_len = mem_kv_rows.shape[1] // (2 * X_HEADS)
        kv_specs = [pl.BlockSpec((1,) + mem_kv_rows.shape[1:], lambda i: (i // tiles_per_seq, 0, 0))]
        kv_args = [mem_kv_rows]
    return pl.pallas_call(
        functools.partial(_mlp_kernel, m_len=m_len),
        grid=(n // tm,),
        in_specs=[row(D_MODEL), row(X_WIDTH)] + kv_specs + [full(w) for w in weights],
        out_specs=row(D_MODEL),
        out_shape=jax.ShapeDtypeStruct((n, D_MODEL), F32),
        compiler_params=_params("parallel"),
        name="mlp",
    )(x1, o, *kv_args, *weights)


def _memkv_kernel(m_ref, g_ref, w_ref, o_ref, *, tm):
    kv = _dot(_rms(m_ref[...], g_ref[...]).astype(BF16), w_ref[...])
    n_rows = 2 * X_HEADS
    for j in range(n_rows):
        o_ref[pl.ds(j, tm, stride=n_rows), :] = kv[:, j * X_HEAD_DIM:(j + 1) * X_HEAD_DIM]


def _memkv(mem2d, g, w, *, tm):
    n = mem2d.shape[0]
    n_rows = 2 * X_HEADS
    return pl.pallas_call(
        functools.partial(_memkv_kernel, tm=tm),
        grid=(n // tm,),
        in_specs=[pl.BlockSpec((tm, D_MODEL), lambda i: (i, 0)),
                  pl.BlockSpec(g.shape, lambda i: (0, 0)), pl.BlockSpec(w.shape, lambda i: (0, 0))],
        out_specs=pl.BlockSpec((tm * n_rows, X_HEAD_DIM), lambda i: (i, 0)),
        out_shape=jax.ShapeDtypeStruct((n * n_rows, X_HEAD_DIM), F32),
        compiler_params=_params("parallel"),
        name="memkv",
    )(mem2d, g, w)


def _row(v):
    return v.astype(F32).reshape(1, -1)


def kernel(x_prompt, x_sample, cache_nsa_kv, cache_win_kv, state_ssm, cache_mem_kv, page_table, mem_prompt, g_mix_pre, w_in, ssm_lam_re, ssm_lam_im, ssm_log_dt, ssm_b_re, ssm_b_im, ssm_c_re, ssm_c_im, ssm_d, w_glu, b_glu, cmp_pe_k, w_cmpk1, w_cmpk2, cmp_pe_v, w_cmpv1, w_cmpv2, w_br_ssm, w_br_nsa, w_out, g_mix_post, g_x_pre, g_mem, w_xq, w_xk, w_xv, w_xo, g_x_post, g_mlp_pre, w_up, w_down, g_mlp_post):
    depth = w_in.shape[0]
    n_seq_p, seq, _ = x_prompt.shape
    n_seq_s, nq, _ = x_sample.shape
    past = page_table.shape[1] * PAGE_SIZE
    assert depth == 1 and seq % BLK == 0 and nq <= SUBLANES and past % BLK == 0
    assert n_seq_p == SUBLANES and n_seq_s % SUBLANES == 0

    y_p = x_prompt.reshape(n_seq_p * seq, D_MODEL)
    y_s = x_sample.reshape(n_seq_s * nq, D_MODEL)
    l = 0

    w_proj = w_in[l, :, :N_PROJ].astype(BF16)
    w_gate = jnp.pad(w_in[l, :, N_PROJ:N_PROJ + N_GATE], ((0, 0), (0, LANES - N_GATE))).astype(BF16)
    w_merge = w_in[l, :, N_PROJ + N_GATE:].astype(BF16)
    lam_l, bm, cm = _ssm_params(ssm_lam_re[l], ssm_lam_im[l], ssm_log_dt[l], ssm_b_re[l], ssm_b_im[l],
                                ssm_c_re[l], ssm_c_im[l])
    ssm_w = (lam_l, bm, cm, _row(ssm_d[l]), w_glu[l].astype(BF16), _row(b_glu[l]))
    cmp_w = _compress_params(cmp_pe_k[l], w_cmpk1[l], w_cmpk2[l], cmp_pe_v[l], w_cmpv1[l], w_cmpv2[l])
    merge_w = (_row(g_mix_pre[l]), w_merge, w_br_ssm[l].astype(BF16), w_br_nsa[l].astype(BF16),
               w_out[l].astype(BF16), _row(g_mix_post[l]), _row(g_x_pre[l]), w_xq[l].astype(BF16))
    mlp_w = (w_xo[l].astype(BF16), _row(g_x_post[l]), _row(g_mlp_pre[l]), w_up[l].astype(BF16),
             w_down[l].astype(BF16), _row(g_mlp_post[l]))
    w_mem = jnp.concatenate([w_xk[l], w_xv[l]], axis=1).astype(BF16)

    tm_p = 512 if seq % 512 == 0 else seq
    nt_p = seq // tm_p
    tabs_p = _rope_tables(jnp.arange(seq, dtype=jnp.int32))
    u_p, q_p, kvt_p, wint_p, kvb_p, gate_p, cmp_rows_p = _proj(
        y_p, _row(g_mix_pre[l]), w_proj, w_gate, tabs_p, tm=tm_p, n_tab_blocks=nt_p, prompt=True)
    h0_p = jnp.zeros((1, SUBLANES, 2 * N_STATE), F32)
    tc_p = 64 if seq % 64 == 0 else seq
    ssm_p, hl_p = _ssm(u_p, h0_p, *ssm_w, n_groups=1, n_time=seq, tc=tc_p)

    nb_p = seq // BLK
    n_blocks_p = n_seq_p * nb_p
    cmp_p = _compress(cmp_rows_p, *cmp_w, n_blocks=n_blocks_p, tm=min(n_blocks_p, 256))
    nsa_p = _nsa_prompt(q_p, gate_p, cmp_p, kvb_p, n_seq=n_seq_p, seq=seq)

    mem_kv_p = _memkv(mem_prompt.reshape(-1, D_MODEL), _row(g_mem[l]), w_mem, tm=256)
    m_len = mem_prompt.shape[1]
    mem_kv_p3 = mem_kv_p.reshape(n_seq_p, m_len * 2 * X_HEADS, X_HEAD_DIM)

    if nt_p % 2:
        nsa_p = _unmirror(*nsa_p, n_seq_p)
    x1_p, qx_p = _merge(y_p, ssm_p, nsa_p, merge_w, tm=tm_p, n_tiles=nt_p)
    y_p = _mlp(x1_p, qx_p, mlp_w, tm=tm_p, mem_kv_rows=mem_kv_p3, tiles_per_seq=nt_p)

    n_s = n_seq_s * nq
    pos_s = past + jnp.arange(nq, dtype=jnp.int32)
    tabs_s = tuple(jnp.tile(t, (n_seq_s, 1)) for t in _rope_tables(pos_s))
    u_s, q_s, kv_s, win_s, kvb_s, gate_s = _proj(y_s, _row(g_mix_pre[l]), w_proj, w_gate, tabs_s,
                                                 tm=n_s, n_tab_blocks=1, prompt=False)
    n_grp = n_seq_s // SUBLANES
    n_slabs = SSM_WIDTH // LANES
    u_s = (u_s.reshape(n_grp, SUBLANES, nq, n_slabs, LANES).transpose(3, 0, 2, 1, 4)
           .reshape(n_slabs, n_s, LANES))
    st = state_ssm[l].astype(F32).reshape(n_seq_s, N_STATE, 2)
    h0_s = _state_lanes(st[..., 0], st[..., 1]).reshape(n_grp, SUBLANES, 2 * N_STATE)
    ssm_s, hl_s = _ssm(u_s, h0_s, *ssm_w, n_groups=n_grp, n_time=nq, tc=nq)
    ssm_s = (ssm_s.reshape(n_slabs, n_grp, nq, SUBLANES, LANES).transpose(0, 1, 3, 2, 4)
             .reshape(n_slabs, n_s, LANES))

    n_pages = page_table.shape[1]
    n_pool = cache_nsa_kv.shape[1]
    cache_t = jnp.transpose(cache_nsa_kv[l], (0, 2, 3, 4, 1)).reshape(n_pool, 4, KV_WIDTH, PAGE_SIZE)
    win_t = jnp.transpose(cache_win_kv[l], (0, 2, 3, 4, 1)).reshape(n_seq_s, 2, KV_WIDTH, -1)
    pt_flat = page_table.reshape(-1).astype(jnp.int32)
    nb_past = past // BLK
    pe_t, w1_t = _compress_paged_params(cmp_pe_k[l], w_cmpk1[l], cmp_pe_v[l], w_cmpv1[l])
    cmp_pages = _compress_paged(pt_flat, cache_t, pe_t, w1_t, cmp_w[2], m=min(n_seq_s * n_pages, 128))
    cmp_past = cmp_pages
    new_rows = jnp.pad(jnp.concatenate([kv_s, win_s], axis=1).reshape(n_seq_s, nq, 6 * KV_WIDTH),
                       ((0, 0), (0, LANES - nq), (0, 0)))
    cmp_new = _compress(new_rows[:, :BLK].reshape(n_seq_s * BLK, 6 * KV_WIDTH), *cmp_w,
                        n_blocks=n_seq_s, tm=n_seq_s)
    nbp = -(-(nb_past + 1) // LANES) * LANES
    cmp_s = jnp.concatenate([cmp_past.reshape(2, n_seq_s, nb_past, KV_WIDTH), cmp_new[:, :, None, :],
                             jnp.zeros((2, n_seq_s, nbp - nb_past - 1, KV_WIDTH), F32)], axis=2)
    nsa_s = _nsa_sample(pt_flat, q_s.astype(F32).reshape(n_seq_s, nq, NSA_WIDTH),
                        gate_s.reshape(n_seq_s, nq, LANES), cmp_s[0], cmp_s[1], cache_t, new_rows, win_t,
                        _expand_matrix(nbp, past + LANES), past=past, tk=min(past, 1024))

    x1_s, qx_s = _merge(y_s, ssm_s, nsa_s.reshape(n_s, NSA_WIDTH).astype(BF16), merge_w, tm=n_s, n_tiles=1)
    mem_kv_s3 = cache_mem_kv[l].reshape(n_seq_s, m_len * 2 * X_HEADS, X_HEAD_DIM)
    o_s = _xattn(qx_s.astype(F32).reshape(n_seq_s, nq, X_WIDTH), mem_kv_s3, tq=nq)
    y_s = _mlp(x1_s, o_s.reshape(-1, X_WIDTH).astype(BF16), mlp_w, tm=n_s)

    def ssm_state(hl, n_seq):
        re, im = _state_unlanes(hl.reshape(n_seq, 2 * N_STATE))
        return jnp.stack([re, im], axis=-1).reshape(1, n_seq, SSM_GROUPS, SSM_STATE, 2)

    def token_major(xt):
        n_seq, parts, _, t_len = xt.shape
        return xt.reshape(1, n_seq, parts, NSA_KV, HEAD_DIM, t_len).transpose(0, 1, 5, 2, 3, 4)

    w_keep = min(WINDOW, seq)
    win_new = win_s.reshape(n_seq_s, nq, 2, NSA_KV, HEAD_DIM).astype(cache_win_kv.dtype)
    win_sample = jnp.concatenate([cache_win_kv[l], win_new], axis=1)[:, nq:]
    return (y_p.reshape(n_seq_p, seq, D_MODEL),
            y_s.reshape(n_seq_s, nq, D_MODEL),
            token_major(kvt_p),
            kv_s.reshape(1, n_seq_s, nq, 4, NSA_KV, HEAD_DIM),
            token_major(wint_p[:, :, :, seq - w_keep:]),
            win_sample[None],
            ssm_state(hl_p, n_seq_p),
            ssm_state(hl_s, n_seq_s),
            mem_kv_p.reshape(1, n_seq_p, m_len, 2, X_HEADS, X_HEAD_DIM))
```

```python
import functools
import math

import jax
import jax.numpy as jnp
import numpy as np
from jax import lax
from jax.experimental import pallas as pl
from jax.experimental.pallas import tpu as pltpu

F32 = jnp.float32
BF16 = jnp.bfloat16

D_MODEL = 1024
SSM_WIDTH = 512
SSM_GROUP = 16
SSM_GROUPS = 32
SSM_STATE = 64
N_STATE = SSM_GROUPS * SSM_STATE
STATE_CHUNK = 512
N_CHUNKS = N_STATE // STATE_CHUNK
NSA_HEADS = 8
HEAD_DIM = 64
NSA_WIDTH = NSA_HEADS * HEAD_DIM
NSA_KV = 2
NSA_REP = NSA_HEADS // NSA_KV
KV_WIDTH = NSA_KV * HEAD_DIM
BLK = 64
N_SEL = 16
WINDOW = 512
ROT_DIM = 16
ROPE_THETA = 500000.0
PAGE_SIZE = 128
X_HEADS = 4
X_HEAD_DIM = 128
X_WIDTH = X_HEADS * X_HEAD_DIM
D_FF = 4 * D_MODEL
EPS = 1e-6
NEG_INF = -1e30
M_INIT = -1e29
FORCE_SCORE = 1e4
LANES = 128
SUBLANES = 8
VMEM_LIMIT = 56 * 1024 * 1024

N_PROJ = SSM_WIDTH + NSA_WIDTH + 6 * KV_WIDTH
N_GATE = 3 * NSA_HEADS
SEL_CHUNK = 512
PAGE_PITCH = PAGE_SIZE + SUBLANES


def _params(*sem):
    return pltpu.CompilerParams(dimension_semantics=sem, vmem_limit_bytes=VMEM_LIMIT)


def _rms(x, g):
    return x * lax.rsqrt(jnp.mean(x * x, axis=-1, keepdims=True) + EPS) * g


def _gelu(x):
    return 0.5 * x * (1.0 + jnp.tanh(math.sqrt(2.0 / math.pi) * (x + 0.044715 * (x * x * x))))


def _sigmoid(x):
    return 1.0 / (1.0 + jnp.exp(-x))


def _dot(a, b):
    return jnp.dot(a, b, preferred_element_type=F32)


def _dot_nt(a, b):
    return lax.dot_general(a, b, (((1,), (1,)), ((), ())), preferred_element_type=F32)


def _proj_kernel(x_ref, g_ref, w_ref, wg_ref, cos_ref, sp_ref, sm_ref,
                 u_ref, q_ref, kv_ref, win_ref, kvb_ref, gate_ref, *cmp_ref, token_minor):
    a = _rms(x_ref[...], g_ref[...]).astype(BF16)
    cos, sp, sm = cos_ref[...], sp_ref[...], sm_ref[...]

    def rope(blk):
        return blk * cos + pltpu.roll(blk, 8, 1) * sp + pltpu.roll(blk, LANES - 8, 1) * sm

    pairs = {}

    def lane_block(j):
        if j // 2 not in pairs:
            pairs[j // 2] = _dot(a, w_ref[:, (j // 2) * 2 * LANES:(j // 2 + 1) * 2 * LANES])
        return pairs[j // 2][:, (j % 2) * LANES:(j % 2 + 1) * LANES]

    if token_minor:
        tm = x_ref.shape[0]
        n_seq = u_ref.shape[1] // tm
        for s in range(SSM_WIDTH // LANES):
            u_ref[s, pl.ds(pl.program_id(1), tm, stride=n_seq), :] = lane_block(s)
    else:
        u_ref[...] = _dot(a, w_ref[:, 0:SSM_WIDTH])
    for j in range(NSA_WIDTH // LANES):
        q_ref[:, j * LANES:(j + 1) * LANES] = rope(lane_block(SSM_WIDTH // LANES + j)).astype(BF16)
    for j in range(6):
        blk = lane_block((SSM_WIDTH + NSA_WIDTH) // LANES + j)
        if j % 2 == 0:
            blk = rope(blk)
        out_ref, part = (kv_ref, j) if j < 4 else (win_ref, j - 4)
        if token_minor:
            out_ref[0, part] = blk.T
            if j < 2:
                cmp_ref[0][:, j * LANES:(j + 1) * LANES] = blk
        else:
            out_ref[:, part * LANES:(part + 1) * LANES] = blk
        kvb_ref[:, j * LANES:(j + 1) * LANES] = blk.astype(BF16)
    gate_ref[...] = _sigmoid(_dot(a, wg_ref[...]))


def _proj(x2d, g, w, wg, tabs, *, tm, n_tab_blocks, prompt):
    n = x2d.shape[0]
    nt = n_tab_blocks
    n_b = n // (nt * tm)
    row = lambda w_: pl.BlockSpec((tm, w_), lambda t, b: (b * nt + t, 0))
    full = lambda a: pl.BlockSpec(a.shape, lambda t, b: (0,) * a.ndim)
    tab = pl.BlockSpec((tm, LANES), lambda t, b: (t, 0))
    if prompt:
        t_len = nt * tm
        n_slabs = SSM_WIDTH // LANES
        u_shape = jax.ShapeDtypeStruct((n_slabs, t_len * n_b, LANES), F32)
        u_spec = pl.BlockSpec((n_slabs, tm * n_b, LANES), lambda t, b: (0, t, 0))
        tok_minor = lambda parts: pl.BlockSpec((1, parts, KV_WIDTH, tm), lambda t, b: (b, 0, 0, t))
        kv_specs = [tok_minor(4), tok_minor(2)]
        kv_shapes = [jax.ShapeDtypeStruct((n_b, 4, KV_WIDTH, t_len), F32),
                     jax.ShapeDtypeStruct((n_b, 2, KV_WIDTH, t_len), F32)]
        extra_specs = [row(2 * KV_WIDTH)]
        extra_shapes = [jax.ShapeDtypeStruct((n, 2 * KV_WIDTH), F32)]
    else:
        u_shape = jax.ShapeDtypeStruct((n, SSM_WIDTH), F32)
        u_spec = row(SSM_WIDTH)
        kv_specs = [row(4 * KV_WIDTH), row(2 * KV_WIDTH)]
        kv_shapes = [jax.ShapeDtypeStruct((n, 4 * KV_WIDTH), F32), jax.ShapeDtypeStruct((n, 2 * KV_WIDTH), F32)]
        extra_specs, extra_shapes = [], []
    return pl.pallas_call(
        functools.partial(_proj_kernel, token_minor=prompt),
        grid=(nt, n_b),
        in_specs=[row(D_MODEL), full(g), full(w), full(wg), tab, tab, tab],
        out_specs=[u_spec, row(NSA_WIDTH)] + kv_specs + [row(6 * KV_WIDTH), row(LANES)] + extra_specs,
        out_shape=[u_shape, jax.ShapeDtypeStruct((n, NSA_WIDTH), BF16)] + kv_shapes
                  + [jax.ShapeDtypeStruct((n, 6 * KV_WIDTH), BF16), jax.ShapeDtypeStruct((n, LANES), F32)]
                  + extra_shapes,
        compiler_params=_params("parallel", "arbitrary"),
        name="proj",
    )(x2d, g, w, wg, *tabs)


def _rope_tables(pos):
    half = ROT_DIM // 2
    freqs = ROPE_THETA ** (-jnp.arange(half, dtype=F32) / half)
    ang = pos.astype(F32)[:, None] * freqs[None, :]
    cos, sin = jnp.cos(ang), jnp.sin(ang)
    r = pos.shape[0]
    z8 = jnp.zeros((r, half), F32)
    rest0 = jnp.zeros((r, HEAD_DIM - ROT_DIM), F32)
    rest1 = jnp.ones((r, HEAD_DIM - ROT_DIM), F32)
    c64 = jnp.concatenate([cos, cos, rest1], axis=1)
    sp64 = jnp.concatenate([z8, sin, rest0], axis=1)
    sm64 = jnp.concatenate([-sin, z8, rest0], axis=1)
    return tuple(jnp.tile(t, (1, LANES // HEAD_DIM)) for t in (c64, sp64, sm64))


def _ssm_kernel(u_ref, h0_ref, lam_ref, bm_ref, cm_ref, d_ref, wglu_ref, bglu_ref,
                y_ref, hlast_ref, hs_ref, hstate_ref, *, tc):
    j = pl.program_id(1)

    @pl.when(j == 0)
    def _():
        hstate_ref[...] = h0_ref[0]

    n_slabs = SSM_WIDTH // LANES
    u = jnp.concatenate([u_ref[s] for s in range(n_slabs)], axis=1)
    ub = u.astype(BF16)
    half_in = SSM_WIDTH // 2
    chunk_lanes = 2 * STATE_CHUNK
    chunks_per_half = N_CHUNKS // 2
    ys = [None, None]
    for c in range(N_CHUNKS):
        h = c // chunks_per_half
        re0 = c * chunk_lanes
        im0 = re0 + STATE_CHUNK
        u_half = ub[:, h * half_in:(h + 1) * half_in]
        hs_ref[:, re0:re0 + STATE_CHUNK] = _dot(u_half, bm_ref[c, 0])
        hs_ref[:, im0:im0 + STATE_CHUNK] = _dot(u_half, bm_ref[c, 1])
        lr = jnp.broadcast_to(lam_ref[0:1, re0:re0 + STATE_CHUNK], (SUBLANES, STATE_CHUNK))
        li = jnp.broadcast_to(lam_ref[0:1, im0:im0 + STATE_CHUNK], (SUBLANES, STATE_CHUNK))
        hr = hstate_ref[:, re0:re0 + STATE_CHUNK]
        hi = hstate_ref[:, im0:im0 + STATE_CHUNK]
        for t in range(tc):
            r0 = t * SUBLANES
            hr, hi = (lr * hr - li * hi + hs_ref[r0:r0 + SUBLANES, re0:re0 + STATE_CHUNK],
                      lr * hi + li * hr + hs_ref[r0:r0 + SUBLANES, im0:im0 + STATE_CHUNK])
            hs_ref[r0:r0 + SUBLANES, re0:re0 + STATE_CHUNK] = hr
            hs_ref[r0:r0 + SUBLANES, im0:im0 + STATE_CHUNK] = hi
        hstate_ref[:, re0:re0 + STATE_CHUNK] = hr
        hstate_ref[:, im0:im0 + STATE_CHUNK] = hi
        part = (_dot(hs_ref[:, re0:re0 + STATE_CHUNK].astype(BF16), cm_ref[c, 0])
                + _dot(hs_ref[:, im0:im0 + STATE_CHUNK].astype(BF16), cm_ref[c, 1]))
        ys[h] = part if ys[h] is None else ys[h] + part

    y = jnp.concatenate(ys, axis=1) + d_ref[...] * u
    y = _gelu(y)
    z = _dot(y.astype(BF16), wglu_ref[...]) + bglu_ref[...]
    out = y * _sigmoid(z)
    for s in range(n_slabs):
        y_ref[s] = out[:, s * LANES:(s + 1) * LANES]

    @pl.when(j == pl.num_programs(1) - 1)
    def _():
        hlast_ref[0] = hstate_ref[...]


def _ssm(u_tb, h0, lam, bm, cm, d, wglu, bglu, *, n_groups, n_time, tc):
    rows = tc * SUBLANES
    nt = n_time // tc
    n_slabs = SSM_WIDTH // LANES
    full = lambda a: pl.BlockSpec(a.shape, lambda g, j: (0,) * a.ndim)
    st = pl.BlockSpec((1, SUBLANES, 2 * N_STATE), lambda g, j: (g, 0, 0))
    slabs = pl.BlockSpec((n_slabs, rows, LANES), lambda g, j: (0, g * nt + j, 0))
    return pl.pallas_call(
        functools.partial(_ssm_kernel, tc=tc),
        grid=(n_groups, nt),
        in_specs=[slabs, st, full(lam), full(bm), full(cm), full(d), full(wglu), full(bglu)],
        out_specs=[slabs, st],
        out_shape=[jax.ShapeDtypeStruct((n_slabs, n_groups * n_time * SUBLANES, LANES), F32),
                   jax.ShapeDtypeStruct((n_groups, SUBLANES, 2 * N_STATE), F32)],
        scratch_shapes=[pltpu.VMEM((rows, 2 * N_STATE), F32), pltpu.VMEM((SUBLANES, 2 * N_STATE), F32)],
        compiler_params=_params("parallel", "arbitrary"),
        name="ssm",
    )(u_tb, h0, lam, bm, cm, d, wglu, bglu)


def _state_lanes(re, im):
    lead = re.shape[:-1]
    r = re.reshape(lead + (N_CHUNKS, 1, STATE_CHUNK))
    i = im.reshape(lead + (N_CHUNKS, 1, STATE_CHUNK))
    return jnp.concatenate([r, i], axis=-2).reshape(lead + (2 * N_STATE,))


def _state_unlanes(x):
    lead = x.shape[:-1]
    y = x.reshape(lead + (N_CHUNKS, 2, STATE_CHUNK))
    return y[..., 0, :].reshape(lead + (N_STATE,)), y[..., 1, :].reshape(lead + (N_STATE,))


def _ssm_params(lam_re, lam_im, log_dt, b_re, b_im, c_re, c_im):
    lr, li = lam_re.astype(F32), lam_im.astype(F32)
    dt = jnp.exp(log_dt.astype(F32))[:, None]
    mag = jnp.exp(lr * dt)
    bar_re, bar_im = mag * jnp.cos(li * dt), mag * jnp.sin(li * dt)
    den = lr * lr + li * li
    f_re = ((bar_re - 1.0) * lr + bar_im * li) / den
    f_im = (bar_im * lr - (bar_re - 1.0) * li) / den
    b_re, b_im = b_re.astype(F32), b_im.astype(F32)
    bb_re = f_re[..., None] * b_re - f_im[..., None] * b_im
    bb_im = f_re[..., None] * b_im + f_im[..., None] * b_re
    g_chunk = STATE_CHUNK // SSM_STATE
    g_half = SSM_GROUPS // 2

    def same_group(c):
        h = c // (N_CHUNKS // 2)
        return (h * g_half + np.arange(g_half)[:, None] == c * g_chunk + np.arange(g_chunk)[None, :]
                ).astype(np.float32)

    def in_blockdiag(x, c):
        t = x[c * g_chunk:(c + 1) * g_chunk].transpose(2, 0, 1)[None]
        return (t * same_group(c)[:, None, :, None]).reshape(SSM_WIDTH // 2, STATE_CHUNK)

    def out_blockdiag(x, c):
        t = x[c * g_chunk:(c + 1) * g_chunk].transpose(0, 2, 1)[:, :, None, :]
        return (t * same_group(c).T[:, None, :, None]).reshape(STATE_CHUNK, SSM_WIDTH // 2)

    bm = jnp.stack([jnp.stack([in_blockdiag(bb_re, c), in_blockdiag(bb_im, c)])
                    for c in range(N_CHUNKS)]).astype(BF16)
    cm = jnp.stack([jnp.stack([out_blockdiag(c_re.astype(F32), c), out_blockdiag(-c_im.astype(F32), c)])
                    for c in range(N_CHUNKS)]).astype(BF16)
    lam_l = _state_lanes(bar_re.reshape(1, N_STATE), bar_im.reshape(1, N_STATE))
    return lam_l, bm, cm


def _compress_kernel(x_ref, pe_ref, w1_ref, w2_ref, o_ref, *, tm):
    def body(sp, acc):
        s0 = 2 * sp
        xa = x_ref[pl.ds(s0, tm, stride=BLK), :] + pe_ref[0, pl.ds(s0, 1), :]
        xb = x_ref[pl.ds(s0 + 1, tm, stride=BLK), :] + pe_ref[0, pl.ds(s0 + 1, 1), :]
        lhs = jnp.concatenate([xa, xb], axis=1).astype(BF16)
        return acc + _dot(lhs, w1_ref[0, sp])

    acc = lax.fori_loop(0, BLK // 2, body, jnp.zeros((tm, KV_WIDTH), F32))
    o_ref[0] = _dot(_gelu(acc).astype(BF16), w2_ref[0])


def _compress(x2d, pe, w1, w2, *, n_blocks, tm):
    return pl.pallas_call(
        functools.partial(_compress_kernel, tm=tm),
        grid=(2, n_blocks // tm),
        in_specs=[pl.BlockSpec((tm * BLK, KV_WIDTH), lambda c, i: (i, c)),
                  pl.BlockSpec((1, BLK, KV_WIDTH), lambda c, i: (c, 0, 0)),
                  pl.BlockSpec((1, BLK // 2, 2 * KV_WIDTH, KV_WIDTH), lambda c, i: (c, 0, 0, 0)),
                  pl.BlockSpec((1, KV_WIDTH, KV_WIDTH), lambda c, i: (c, 0, 0))],
        out_specs=pl.BlockSpec((1, tm, KV_WIDTH), lambda c, i: (c, i, 0)),
        out_shape=jax.ShapeDtypeStruct((2, n_blocks, KV_WIDTH), F32),
        compiler_params=_params("parallel", "parallel"),
        name="compress",
    )(x2d, pe, w1, w2)


def _compress_params(pe_k, w1_k, w2_k, pe_v, w1_v, w2_v):
    def bd(w):
        z = jnp.zeros_like(w)
        return jnp.concatenate([jnp.concatenate([w, z], axis=-1), jnp.concatenate([z, w], axis=-1)], axis=-2)

    def one(pe, w1, w2):
        w1s = bd(w1.astype(F32).reshape(BLK, HEAD_DIM, HEAD_DIM))
        return (jnp.tile(pe.astype(F32), (1, NSA_KV)),
                w1s.reshape(BLK // 2, 2 * KV_WIDTH, KV_WIDTH).astype(BF16),
                bd(w2.astype(F32)).astype(BF16))

    k, v = one(pe_k, w1_k, w2_k), one(pe_v, w1_v, w2_v)
    return tuple(jnp.stack([a, b]) for a, b in zip(k, v))


def _compress_paged_kernel(pt_ref, cache_ref, pe_ref, w1_ref, w2_ref, o_ref, buf_ref, sems, *, m):
    step = pl.program_id(0)

    def page_copy(s, j):
        slot = s % 2
        row0 = pl.multiple_of(j * PAGE_PITCH, SUBLANES)
        return pltpu.make_async_copy(cache_ref.at[pt_ref[s * m + j], pl.ds(0, 2)],
                                     buf_ref.at[pl.ds(2 * slot, 2), pl.ds(row0, PAGE_SIZE), :],
                                     sems.at[slot])

    def start_all(s):
        def start(j, _):
            page_copy(s, j).start()
            return 0
        lax.fori_loop(0, m, start, 0, unroll=8)

    @pl.when(step == 0)
    def _():
        start_all(step)

    @pl.when(step + 1 < pl.num_programs(0))
    def _():
        start_all(step + 1)

    def wait(j, _):
        page_copy(step, j).wait()
        return 0

    lax.fori_loop(0, m, wait, 0, unroll=8)

    for c in range(2):
        tile = 2 * (step % 2) + c
        res = []
        for kv in range(NSA_KV):
            acc = jnp.zeros((m, PAGE_SIZE), F32)
            for dp in range(HEAD_DIM // 2):
                d0 = 2 * dp
                r0 = kv * HEAD_DIM + d0
                xa = buf_ref[tile, pl.ds(r0, m, stride=PAGE_PITCH), :] + pe_ref[c, d0:d0 + 1, :]
                xb = buf_ref[tile, pl.ds(r0 + 1, m, stride=PAGE_PITCH), :] + pe_ref[c, d0 + 1:d0 + 2, :]
                acc = acc + _dot(jnp.concatenate([xa, xb], axis=1).astype(BF16), w1_ref[c, dp])
            res.append(_dot(_gelu(acc).astype(BF16), w2_ref[c]))
        for blk in range(PAGE_SIZE // BLK):
            cols = slice(blk * HEAD_DIM, (blk + 1) * HEAD_DIM)
            o_ref[c, pl.ds(blk, m, stride=PAGE_SIZE // BLK), :] = jnp.concatenate(
                [r[:, cols] for r in res], axis=1)


def _compress_paged(pt_flat, cache_t, pe_t, w1_t, w2, *, m):
    n = pt_flat.shape[0]
    full = lambda a: pl.BlockSpec(a.shape, lambda i, pt: (0,) * a.ndim)
    grid_spec = pltpu.PrefetchScalarGridSpec(
        num_scalar_prefetch=1,
        grid=(n // m,),
        in_specs=[pl.BlockSpec(memory_space=pl.ANY), full(pe_t), full(w1_t), full(w2)],
        out_specs=pl.BlockSpec((2, m * (PAGE_SIZE // BLK), KV_WIDTH), lambda i, pt: (0, i, 0)),
        scratch_shapes=[pltpu.VMEM((4, m * PAGE_PITCH, PAGE_SIZE), F32), pltpu.SemaphoreType.DMA((2,))])
    return pl.pallas_call(
        functools.partial(_compress_paged_kernel, m=m),
        grid_spec=grid_spec,
        out_shape=jax.ShapeDtypeStruct((2, n * (PAGE_SIZE // BLK), KV_WIDTH), F32),
        compiler_params=_params("arbitrary"),
        name="compress_paged",
    )(pt_flat, cache_t, pe_t, w1_t, w2)


def _compress_paged_params(pe_k, w1_k, pe_v, w1_v):
    def bd(w):
        z = jnp.zeros_like(w)
        return jnp.concatenate([jnp.concatenate([w, z], axis=-1), jnp.concatenate([z, w], axis=-1)], axis=-2)

    def one(pe, w1):
        w1d = bd(w1.astype(F32).reshape(BLK, HEAD_DIM, HEAD_DIM).transpose(1, 0, 2))
        return (jnp.tile(pe.astype(F32).T, (1, PAGE_SIZE // BLK)),
                w1d.reshape(HEAD_DIM // 2, 2 * PAGE_SIZE, PAGE_SIZE).astype(BF16))

    k, v = one(pe_k, w1_k), one(pe_v, w1_v)
    return tuple(jnp.stack([a, b]) for a, b in zip(k, v))


def _stack_queries(q, nq):
    q = q.astype(F32)
    z = jnp.zeros((nq, HEAD_DIM), F32)
    rows = []
    for h in range(NSA_HEADS):
        blk = q[:, h * HEAD_DIM:(h + 1) * HEAD_DIM]
        rows.append(jnp.concatenate([blk, z] if h < NSA_REP else [z, blk], axis=1))
    return (jnp.concatenate(rows, axis=0) * (HEAD_DIM ** -0.5)).astype(BF16)


def _masked_softmax(s, valid):
    s = jnp.where(valid, s, NEG_INF)
    m = jnp.max(s, axis=-1, keepdims=True)
    p = jnp.exp(s - m) * valid.astype(F32)
    return p / jnp.maximum(jnp.sum(p, axis=-1, keepdims=True), 1e-30)


def _select_blocks(imp, n_ids, cur, nb):
    forced = (n_ids == 0) | (n_ids == cur) | (n_ids == cur - 1)
    imp = jnp.where(forced, FORCE_SCORE, imp)
    imp = jnp.where(n_ids <= cur, imp, -FORCE_SCORE)
    rank = jnp.zeros(imp.shape, F32)
    for m in range(nb):
        col = imp[:, m:m + 1]
        beats = (col > imp) | ((col == imp) & (n_ids > m))
        rank = rank + beats.astype(F32)
    return (rank < float(N_SEL)).astype(F32)


def _select_blocks_t(imp_t, cur, nb):
    n_t = lax.broadcasted_iota(jnp.int32, imp_t.shape, 0)
    forced = (n_t == 0) | (n_t == cur) | (n_t == cur - 1)
    imp_t = jnp.where(forced, FORCE_SCORE, imp_t)
    imp_t = jnp.where(n_t <= cur, imp_t, -FORCE_SCORE)
    rank = jnp.zeros(imp_t.shape, F32)
    for m in range(nb):
        row = imp_t[m:m + 1, :]
        beats = (row > imp_t) | ((row == imp_t) & (n_t > m))
        rank = rank + beats.astype(F32)
    return ((rank < float(N_SEL)) & (n_t <= cur)).astype(F32)


def _combine_heads(gates, o_c, o_s, o_w, nq):
    outs = []
    for h in range(NSA_HEADS):
        rows = slice(h * nq, (h + 1) * nq)
        o = (gates[:, 3 * h:3 * h + 1] * o_c[rows] + gates[:, 3 * h + 1:3 * h + 2] * o_s[rows]
             + gates[:, 3 * h + 2:3 * h + 3] * o_w[rows])
        g = h // NSA_REP
        outs.append(o[:, g * HEAD_DIM:(g + 1) * HEAD_DIM])
    return jnp.concatenate(outs, axis=1)


def _nsa_pair_kernel(qa_ref, qb_ref, ga_ref, gb_ref, kc_ref, vc_ref, ks_ref, vs_ref, kw_ref, vw_ref,
                     et_ref, tri_ref, wm_ref, oa_ref, ob_ref, q2_ref, etd_ref, s_ref, *, nb, ch, wk):
    i = pl.program_id(1)
    nq = BLK
    rows = NSA_HEADS * nq
    bpc = ch // BLK
    n_units = nb // bpc + 1
    chunk_of = (i, nb - 1 - i)
    n_a = i // bpc + 1
    row_q = lax.broadcasted_iota(jnp.int32, (rows, LANES), 0) % nq
    rq_minus_lane = row_q - lax.broadcasted_iota(jnp.int32, (rows, LANES), 1)

    def col_blocks(s):
        return [s[:, j * LANES:(j + 1) * LANES] for j in range(s.shape[1] // LANES)]

    def col_max(cols):
        m = cols[0]
        for c_ in cols[1:]:
            m = jnp.maximum(m, c_)
        return m

    def finish(acc):
        return acc[:, 0:KV_WIDTH] / jnp.maximum(acc[:, KV_WIDTH:2 * KV_WIDTH], 1e-30)

    def with_ones(v):
        return jnp.concatenate([v, jnp.ones((v.shape[0], LANES), BF16)], axis=1)

    q_onehot = jnp.where(rq_minus_lane == -BLK, NEG_INF, 0.0)

    def prepare(idx, q_ref):
        ci = chunk_of[idx]
        q2 = _stack_queries(q_ref[...], nq)
        first = jnp.maximum(ci - WINDOW // BLK, 0) // (LANES // BLK)
        w0 = pl.multiple_of(first * LANES, LANES)
        delta = ci - first * (LANES // BLK)
        lhs = jnp.concatenate([q2, q_onehot.astype(BF16)], axis=1)
        rhs = jnp.concatenate([kw_ref[pl.ds(w0, wk), :], wm_ref[delta]], axis=1)
        cols = col_blocks(_dot_nt(lhs, rhs))
        m_w = jnp.maximum(jnp.max(col_max(cols), axis=-1, keepdims=True), M_INIT)
        m_w = jnp.broadcast_to(m_w, (rows, LANES))
        p = jnp.concatenate([jnp.exp(c_ - m_w).astype(BF16) for c_ in cols], axis=1)
        o_w = finish(_dot(p, with_ones(vw_ref[pl.ds(w0, wk), :])))

        q_pos = ci * BLK + lax.broadcasted_iota(jnp.int32, (rows, 1), 0) % nq
        n_ids = lax.broadcasted_iota(jnp.int32, (rows, nb), 1)
        s_c = _dot_nt(q2, kc_ref[0].astype(BF16))
        p_c = _masked_softmax(s_c, (n_ids + 1) * BLK - 1 <= q_pos)
        o_c = _dot(p_c.astype(BF16), vc_ref[0].astype(BF16))

        imps = []
        for g in range(NSA_KV):
            imp = p_c[g * NSA_REP * nq:(g * NSA_REP + 1) * nq]
            for r in range(1, NSA_REP):
                imp = imp + p_c[(g * NSA_REP + r) * nq:(g * NSA_REP + r + 1) * nq]
            imps.append(imp)
        imp2 = jnp.concatenate([jnp.concatenate(imps, axis=0), jnp.zeros((LANES, LANES - nb), F32)], axis=1)
        sel_t = _select_blocks_t(imp2.T[0:nb], ci, nb)
        sel2 = jnp.concatenate([sel_t, jnp.ones((LANES - nb, LANES), F32)], axis=0).T
        neg = (sel2 - 1.0) * (-NEG_INF)
        neg_rows = jnp.concatenate([neg[g * nq:(g + 1) * nq] for g in range(NSA_KV) for _ in range(NSA_REP)],
                                   axis=0)
        q2_ref[idx] = jnp.concatenate([q2, (neg_rows + q_onehot).astype(BF16)], axis=1)
        in_diag = lax.broadcasted_iota(jnp.int32, (ch, LANES), 0) // BLK == ci % bpc
        etd_ref[idx] = et_ref[ci // bpc] + jnp.where(in_diag, tri_ref[...], jnp.zeros((), BF16))
        return o_c, o_w

    o_ca, o_wa = prepare(0, qa_ref)
    o_cb, o_wb = prepare(1, qb_ref)

    n_a_max = (nb // 2 - 1) // bpc + 1
    mx = [jnp.full((rows, LANES), M_INIT, F32)] * 2
    units = []
    for u in range(n_units):
        which = 0 if u == 0 else 1 if u >= n_a_max else (u >= n_a).astype(jnp.int32)
        static = isinstance(which, int)
        kc = u - which * n_a
        k0 = kc * ch if isinstance(kc, int) else pl.multiple_of(kc * ch, ch)
        units.append((which, k0))
        owner = chunk_of[which] if static else jnp.where(which == 1, chunk_of[1], chunk_of[0])
        key_mask = jnp.where(kc == owner // bpc, etd_ref[which], et_ref[kc])
        rhs = jnp.concatenate([ks_ref[pl.ds(k0, ch), :], key_mask], axis=1)
        cols = col_blocks(_dot_nt(q2_ref[which], rhs))
        cm = col_max(cols)
        if static:
            mx[which] = jnp.maximum(mx[which], cm)
        else:
            mx = [jnp.maximum(mx[0], jnp.where(which == 0, cm, M_INIT)),
                  jnp.maximum(mx[1], jnp.where(which == 1, cm, M_INIT))]
        for j, c_ in enumerate(cols):
            s_ref[u, :, j * LANES:(j + 1) * LANES] = c_

    m = [jnp.broadcast_to(jnp.max(mx_i, axis=-1, keepdims=True), (rows, LANES)) for mx_i in mx]
    acc = [None, None]

    def add(total, part):
        return part if total is None else total + part

    for u, (which, k0) in enumerate(units):
        static = isinstance(which, int)
        m_u = m[which] if static else jnp.where(which == 1, m[1], m[0])
        p = jnp.concatenate([jnp.exp(s_ref[u, :, j * LANES:(j + 1) * LANES] - m_u).astype(BF16)
                             for j in range(ch // LANES)], axis=1)
        part = _dot(p, with_ones(vs_ref[pl.ds(k0, ch), :]))
        if static:
            acc[which] = add(acc[which], part)
        else:
            w_b = (which == 1).astype(F32)
            acc = [add(acc[0], part * (1.0 - w_b)), add(acc[1], part * w_b)]

    oa_ref[...] = _combine_heads(ga_ref[...], o_ca, finish(acc[0]), o_wa, nq).astype(oa_ref.dtype)
    ob_ref[...] = _combine_heads(gb_ref[...], o_cb, finish(acc[1]), o_wb, nq).astype(ob_ref.dtype)


def _nsa_prompt(q, gates, cmp_kv, kvb, *, n_seq, seq):
    nb = seq // BLK
    rows = NSA_HEADS * BLK
    ch = min(SEL_CHUNK, seq)
    wk = min(WINDOW + LANES, seq)
    bpc = ch // BLK
    assert seq % ch == 0 and wk % LANES == 0 and nb % 2 == 0 and nb <= LANES
    half = nb // 2
    n_units = nb // bpc + 1
    assert nb <= BLK
    et = np.zeros((seq // ch, ch, LANES), np.float32)
    for c in range(seq // ch):
        et[c, np.arange(ch), (c * ch + np.arange(ch)) // BLK] = 1.0
    key = np.arange(ch)[:, None]
    qry = np.arange(BLK)[None, :]
    tri = np.zeros((ch, LANES), np.float32)
    tri[:, BLK:] = (key % BLK) > qry
    n_delta = WINDOW // BLK + LANES // BLK
    wm = np.zeros((n_delta, wk, LANES), np.float32)
    for dl in range(n_delta):
        dist = dl * BLK + qry - np.arange(wk)[:, None]
        wm[dl, :, BLK:] = (dist < 0) | (dist >= WINDOW)
    chunk_a = lambda w_: pl.BlockSpec((BLK, w_), lambda b, i: (b * nb + i, 0))
    chunk_b = lambda w_: pl.BlockSpec((BLK, w_), lambda b, i: (b * nb + nb - 1 - i, 0))
    out_spec = pl.BlockSpec((BLK, NSA_WIDTH), lambda b, i: (b * half + i, 0))
    kv = lambda col: pl.BlockSpec((seq, KV_WIDTH), lambda b, i: (b, col))
    cmp_ = lambda which: pl.BlockSpec((1, nb, KV_WIDTH), lambda b, i: (which, b, 0))
    out = jax.ShapeDtypeStruct((n_seq * half * BLK, NSA_WIDTH), BF16)
    o_a, o_b = pl.pallas_call(
        functools.partial(_nsa_pair_kernel, nb=nb, ch=ch, wk=wk),
        grid=(n_seq, half),
        in_specs=[chunk_a(NSA_WIDTH), chunk_b(NSA_WIDTH), chunk_a(LANES), chunk_b(LANES),
                  cmp_(0), cmp_(1), kv(2), kv(3), kv(4), kv(5),
                  pl.BlockSpec(et.shape, lambda b, i: (0, 0, 0)),
                  pl.BlockSpec(tri.shape, lambda b, i: (0, 0)),
                  pl.BlockSpec(wm.shape, lambda b, i: (0, 0, 0))],
        out_specs=[out_spec, out_spec],
        out_shape=[out, out],
        scratch_shapes=[pltpu.VMEM((2, rows, 2 * LANES), BF16),
                        pltpu.VMEM((2, ch, LANES), BF16),
                        pltpu.VMEM((n_units, rows, ch), F32)],
        compiler_params=_params("parallel", "arbitrary"),
        name="nsa_prompt",
    )(q, q, gates, gates, cmp_kv, cmp_kv, kvb, kvb, kvb, kvb,
      jnp.asarray(et, dtype=BF16), jnp.asarray(tri, dtype=BF16), jnp.asarray(wm, dtype=BF16))
    return o_a, o_b


def _unmirror(o_a, o_b, n_seq):
    half = o_a.shape[0] // (n_seq * BLK)
    o_a = o_a.reshape(n_seq, half, BLK, NSA_WIDTH)
    o_b = o_b.reshape(n_seq, half, BLK, NSA_WIDTH)[:, ::-1]
    return jnp.concatenate([o_a, o_b], axis=1).reshape(-1, NSA_WIDTH)


def _nsa_sample_kernel(pt_ref, q_ref, gate_ref, kc_ref, vc_ref, cache_ref, nks_ref, nvs_ref,
                       wt_ref, nkw_ref, nvw_ref, e_ref, o_ref, kv_buf, s_scr, sems,
                       *, nq, past, nbp, tk):
    b = pl.program_id(0)
    n_pages = past // PAGE_SIZE
    rows = NSA_HEADS * nq

    def page_copy(s, j):
        slot = s % 2
        k0 = pl.multiple_of(j * PAGE_SIZE, PAGE_SIZE)
        return pltpu.make_async_copy(cache_ref.at[pt_ref[s * n_pages + j], pl.ds(2, 2)],
                                     kv_buf.at[pl.ds(2 * slot, 2), :, pl.ds(k0, PAGE_SIZE)], sems.at[slot])

    def start_all(s):
        def start(j, _):
            page_copy(s, j).start()
            return 0
        lax.fori_loop(0, n_pages, start, 0, unroll=8)

    @pl.when(b == 0)
    def _():
        start_all(b)

    @pl.when(b + 1 < pl.num_programs(0))
    def _():
        start_all(b + 1)

    slot = b % 2
    q2 = _stack_queries(q_ref[0], nq)
    q_pos = past + lax.broadcasted_iota(jnp.int32, (rows, 1), 0) % nq
    cur = past // BLK

    n_ids = lax.broadcasted_iota(jnp.int32, (rows, nbp), 1)
    s_c = _dot_nt(q2, kc_ref[0].astype(BF16))
    p_c = _masked_softmax(s_c, (n_ids + 1) * BLK - 1 <= q_pos)
    o_c = _dot(p_c.astype(BF16), vc_ref[0].astype(BF16))

    n_sel = lax.broadcasted_iota(jnp.int32, (nq, nbp), 1)
    sels = []
    for g in range(NSA_KV):
        imp = p_c[g * NSA_REP * nq:(g * NSA_REP + 1) * nq]
        for r in range(1, NSA_REP):
            imp = imp + p_c[(g * NSA_REP + r) * nq:(g * NSA_REP + r + 1) * nq]
        sel = _select_blocks(imp, n_sel, cur, cur + 1)
        sels.extend([sel] * NSA_REP)
    keys = _dot(jnp.concatenate(sels, axis=0).astype(BF16), e_ref[...])
    bias = (keys - 1.0) * (-NEG_INF)

    def col_max(s):
        m = s[:, 0:LANES]
        for j in range(1, s.shape[1] // LANES):
            m = jnp.maximum(m, s[:, j * LANES:(j + 1) * LANES])
        return m

    def col_sum(p):
        t = p[:, 0:LANES]
        for j in range(1, p.shape[1] // LANES):
            t = t + p[:, j * LANES:(j + 1) * LANES]
        return t

    def row_max(mx):
        return jnp.broadcast_to(jnp.max(mx, axis=-1, keepdims=True), mx.shape)

    def tiled(m, width):
        return m if width == LANES else jnp.concatenate([m] * (width // LANES), axis=1)

    new_pos = past + lax.broadcasted_iota(jnp.int32, (rows, LANES), 1)

    def wait(j, _):
        page_copy(b, j).wait()
        return 0

    lax.fori_loop(0, n_pages, wait, 0, unroll=8)

    mx = jnp.full((rows, LANES), M_INIT, F32)
    for t in range(past // tk):
        s = _dot(q2, kv_buf[2 * slot, :, t * tk:(t + 1) * tk].astype(BF16)) + bias[:, t * tk:(t + 1) * tk]
        s_scr[:, t * tk:(t + 1) * tk] = s
        mx = jnp.maximum(mx, col_max(s))
    s_new = _dot_nt(q2, nks_ref[0].astype(BF16)) + bias[:, past:past + LANES]
    s_new = jnp.where(new_pos <= q_pos, s_new, NEG_INF)
    m_s = row_max(jnp.maximum(mx, s_new))
    p_new = jnp.exp(s_new - m_s)
    acc = _dot(p_new.astype(BF16), nvs_ref[0].astype(BF16))
    lsum = p_new
    for t in range(past // tk):
        p = jnp.exp(s_scr[:, t * tk:(t + 1) * tk] - tiled(m_s, tk))
        lsum = lsum + col_sum(p)
        acc = acc + _dot_nt(p.astype(BF16), kv_buf[2 * slot + 1, :, t * tk:(t + 1) * tk].astype(BF16))
    o_s = acc / jnp.maximum(jnp.sum(lsum, axis=-1, keepdims=True), 1e-30)

    wlen = wt_ref.shape[3]
    w_pos = past - wlen + lax.broadcasted_iota(jnp.int32, (rows, wlen), 1)
    d = q_pos - w_pos
    s_w = _dot(q2, wt_ref[0, 0].astype(BF16))
    s_w = jnp.where((d >= 0) & (d < WINDOW) & (w_pos >= 0), s_w, NEG_INF)
    d = q_pos - new_pos
    s_nw = _dot_nt(q2, nkw_ref[0].astype(BF16))
    s_nw = jnp.where((d >= 0) & (d < WINDOW), s_nw, NEG_INF)
    m_w = row_max(jnp.maximum(jnp.maximum(col_max(s_w), s_nw), M_INIT))
    p_w = jnp.exp(s_w - tiled(m_w, wlen))
    p_nw = jnp.exp(s_nw - m_w)
    acc = _dot_nt(p_w.astype(BF16), wt_ref[0, 1].astype(BF16)) + _dot(p_nw.astype(BF16), nvw_ref[0].astype(BF16))
    o_w = acc / jnp.maximum(jnp.sum(col_sum(p_w) + p_nw, axis=-1, keepdims=True), 1e-30)

    o_ref[0] = _combine_heads(gate_ref[0], o_c, o_s, o_w, nq).astype(o_ref.dtype)


def _nsa_sample(pt_flat, q3, gates3, kc, vc, cache_t, new_rows, win_t, expand, *, past, tk):
    n_seq, nq, _ = q3.shape
    nbp = kc.shape[1]
    rows = NSA_HEADS * nq
    per_seq = lambda a: pl.BlockSpec((1,) + a.shape[1:], lambda b, pt: (b,) + (0,) * (a.ndim - 1))
    new = lambda col: pl.BlockSpec((1, LANES, KV_WIDTH), lambda b, pt: (b, 0, col))
    grid_spec = pltpu.PrefetchScalarGridSpec(
        num_scalar_prefetch=1,
        grid=(n_seq,),
        in_specs=[per_seq(q3), per_seq(gates3), per_seq(kc), per_seq(vc),
                  pl.BlockSpec(memory_space=pl.ANY),
                  new(2), new(3), per_seq(win_t), new(4), new(5),
                  pl.BlockSpec(expand.shape, lambda b, pt: (0, 0))],
        out_specs=pl.BlockSpec((1, nq, NSA_WIDTH), lambda b, pt: (b, 0, 0)),
        scratch_shapes=[pltpu.VMEM((4, KV_WIDTH, past), F32),
                        pltpu.VMEM((rows, past), F32), pltpu.SemaphoreType.DMA((2,))])
    return pl.pallas_call(
        functools.partial(_nsa_sample_kernel, nq=nq, past=past, nbp=nbp, tk=tk),
        grid_spec=grid_spec,
        out_shape=jax.ShapeDtypeStruct((n_seq, nq, NSA_WIDTH), F32),
        compiler_params=_params("arbitrary"),
        name="nsa_sample",
    )(pt_flat, q3, gates3, kc, vc, cache_t, new_rows, new_rows, win_t, new_rows, new_rows, expand)


def _expand_matrix(nb, n_keys):
    return jnp.asarray(np.arange(n_keys)[None, :] // BLK == np.arange(nb)[:, None], dtype=BF16)


def _merge_kernel(x_ref, ssm_ref, *refs, mirrored):
    nsa_refs = refs[:2] if mirrored else refs[:1]
    gpre_ref, wm_ref, wbs_ref, wbn_ref, wo_ref, gpost_ref, gx_ref, wxq_ref, x1_ref, qx_ref = refs[len(nsa_refs):]
    x = x_ref[...]
    tm = x.shape[0]
    if mirrored:
        upper = nsa_refs[1][...]
        n_chunks = tm // BLK
        upper = jnp.concatenate([upper[(n_chunks - 1 - s) * BLK:(n_chunks - s) * BLK] for s in range(n_chunks)],
                                axis=0)
        nsa = jnp.where(pl.program_id(0) < pl.num_programs(0) // 2, nsa_refs[0][...], upper)
    else:
        nsa = nsa_refs[0][...]
    n_seq = ssm_ref.shape[1] // tm
    ssm = jnp.concatenate([ssm_ref[s, pl.ds(pl.program_id(1), tm, stride=n_seq), :]
                           for s in range(SSM_WIDTH // LANES)], axis=1)
    a = _rms(x, gpre_ref[...]).astype(BF16)
    g_ssm = _sigmoid(_dot(a, wm_ref[:, 0:D_MODEL]))
    g_nsa = _sigmoid(_dot(a, wm_ref[:, D_MODEL:2 * D_MODEL]))
    merged = (g_ssm * _dot(ssm.astype(BF16), wbs_ref[...])
              + g_nsa * _dot(nsa, wbn_ref[...]))
    x1 = x + _rms(_dot(merged.astype(BF16), wo_ref[...]), gpost_ref[...])
    x1_ref[...] = x1
    c = _rms(x1, gx_ref[...]).astype(BF16)
    qx_ref[...] = (_dot(c, wxq_ref[...]) * (X_HEAD_DIM ** -0.5)).astype(BF16)


def _merge(x2d, ssm_slabs, nsa_o, weights, *, tm, n_tiles):
    n = x2d.shape[0]
    n_b = n // (n_tiles * tm)
    row = lambda w_: pl.BlockSpec((tm, w_), lambda t, b: (b * n_tiles + t, 0))
    full = lambda a: pl.BlockSpec(a.shape, lambda t, b: (0,) * a.ndim, pipeline_mode=pl.Buffered(1))
    ssm_spec = pl.BlockSpec((SSM_WIDTH // LANES, tm * n_b, LANES), lambda t, b: (0, t, 0))
    mirrored = isinstance(nsa_o, tuple)
    if mirrored:
        half = n_tiles // 2
        nsa_specs = [pl.BlockSpec((tm, NSA_WIDTH), lambda t, b: (b * half + jnp.minimum(t, half - 1), 0)),
                     pl.BlockSpec((tm, NSA_WIDTH), lambda t, b: (b * half + jnp.minimum(n_tiles - 1 - t, half - 1), 0))]
        nsa_args = list(nsa_o)
    else:
        nsa_specs, nsa_args = [row(NSA_WIDTH)], [nsa_o]
    return pl.pallas_call(
        functools.partial(_merge_kernel, mirrored=mirrored),
        grid=(n_tiles, n_b),
        in_specs=[row(D_MODEL), ssm_spec] + nsa_specs + [full(w) for w in weights],
        out_specs=[row(D_MODEL), row(X_WIDTH)],
        out_shape=[jax.ShapeDtypeStruct((n, D_MODEL), F32), jax.ShapeDtypeStruct((n, X_WIDTH), BF16)],
        compiler_params=_params("parallel", "arbitrary"),
        name="merge",
    )(x2d, ssm_slabs, *nsa_args, *weights)


def _xattn_tile(q, kv_ref, m_len, seq=0):
    outs = []
    for h in range(X_HEADS):
        cols = slice(h * X_HEAD_DIM, (h + 1) * X_HEAD_DIM)
        k = kv_ref[seq, pl.ds(h, m_len, stride=2 * X_HEADS), :].astype(BF16)
        v = kv_ref[seq, pl.ds(X_HEADS + h, m_len, stride=2 * X_HEADS), :].astype(BF16)
        s = _dot_nt(q[:, cols], k)
        m = jnp.max(s, axis=-1, keepdims=True)
        p = jnp.exp(s - m)
        p = p / jnp.sum(p, axis=-1, keepdims=True)
        outs.append(_dot(p.astype(BF16), v))
    return jnp.concatenate(outs, axis=1)


def _xattn_kernel(q_ref, kv_ref, o_ref, *, m_len):
    for s in range(q_ref.shape[0]):
        o_ref[s] = _xattn_tile(q_ref[s].astype(BF16), kv_ref, m_len, s).astype(o_ref.dtype)


def _xattn(q3, mem_kv_rows, *, seqs_per_step):
    n_seq, t, _ = q3.shape
    m_len = mem_kv_rows.shape[1] // (2 * X_HEADS)
    sb = seqs_per_step
    return pl.pallas_call(
        functools.partial(_xattn_kernel, m_len=m_len),
        grid=(n_seq // sb,),
        in_specs=[pl.BlockSpec((sb, t, X_WIDTH), lambda b: (b, 0, 0)),
                  pl.BlockSpec((sb, m_len * 2 * X_HEADS, X_HEAD_DIM), lambda b: (b, 0, 0))],
        out_specs=pl.BlockSpec((sb, t, X_WIDTH), lambda b: (b, 0, 0)),
        out_shape=jax.ShapeDtypeStruct((n_seq, t, X_WIDTH), q3.dtype),
        compiler_params=_params("parallel"),
        name="xattn",
    )(q3, mem_kv_rows)


def _mlp_kernel(x1_ref, o_ref, *refs, m_len):
    if m_len:
        kv_ref, refs = refs[0], refs[1:]
        o = _xattn_tile(o_ref[...], kv_ref, m_len).astype(BF16)
    else:
        o = o_ref[...]
    wxo_ref, gxp_ref, gm_ref, wup_ref, wdn_ref, gmp_ref, y_ref = refs
    x2 = x1_ref[...] + _rms(_dot(o, wxo_ref[...]), gxp_ref[...])
    m = _rms(x2, gm_ref[...]).astype(BF16)
    hid = jnp.maximum(_dot(m, wup_ref[...]), 0.0)
    hid = (hid * hid).astype(BF16)
    y_ref[...] = x2 + _rms(_dot(hid, wdn_ref[...]), gmp_ref[...])


def _mlp(x1, o, weights, *, tm, mem_kv_rows=None, tiles_per_seq=1):
    n = x1.shape[0]
    row = lambda w_: pl.BlockSpec((tm, w_), lambda i: (i, 0))
    full = lambda a: pl.BlockSpec(a.shape, lambda i: (0,) * a.ndim, pipeline_mode=pl.Buffered(1))
    m_len, kv_specs, kv_args = 0, [], []
    if mem_kv_rows is not None:
        m_len = mem_kv_rows.shape[1] // (2 * X_HEADS)
        kv_specs = [pl.BlockSpec((1,) + mem_kv_rows.shape[1:], lambda i: (i // tiles_per_seq, 0, 0))]
        kv_args = [mem_kv_rows]
    return pl.pallas_call(
        functools.partial(_mlp_kernel, m_len=m_len),
        grid=(n // tm,),
        in_specs=[row(D_MODEL), row(X_WIDTH)] + kv_specs + [full(w) for w in weights],
        out_specs=row(D_MODEL),
        out_shape=jax.ShapeDtypeStruct((n, D_MODEL), F32),
        compiler_params=_params("parallel"),
        name="mlp",
    )(x1, o, *kv_args, *weights)


def _memkv_kernel(m_ref, g_ref, w_ref, o_ref, *, tm):
    kv = _dot(_rms(m_ref[...], g_ref[...]).astype(BF16), w_ref[...])
    n_rows = 2 * X_HEADS
    for j in range(n_rows):
        o_ref[pl.ds(j, tm, stride=n_rows), :] = kv[:, j * X_HEAD_DIM:(j + 1) * X_HEAD_DIM]


def _memkv(mem2d, g, w, *, tm):
    n = mem2d.shape[0]
    n_rows = 2 * X_HEADS
    return pl.pallas_call(
        functools.partial(_memkv_kernel, tm=tm),
        grid=(n // tm,),
        in_specs=[pl.BlockSpec((tm, D_MODEL), lambda i: (i, 0)),
                  pl.BlockSpec(g.shape, lambda i: (0, 0)), pl.BlockSpec(w.shape, lambda i: (0, 0))],
        out_specs=pl.BlockSpec((tm * n_rows, X_HEAD_DIM), lambda i: (i, 0)),
        out_shape=jax.ShapeDtypeStruct((n * n_rows, X_HEAD_DIM), F32),
        compiler_params=_params("parallel"),
        name="memkv",
    )(mem2d, g, w)


def _row(v):
    return v.astype(F32).reshape(1, -1)


def kernel(x_prompt, x_sample, cache_nsa_kv, cache_win_kv, state_ssm, cache_mem_kv, page_table, mem_prompt, g_mix_pre, w_in, ssm_lam_re, ssm_lam_im, ssm_log_dt, ssm_b_re, ssm_b_im, ssm_c_re, ssm_c_im, ssm_d, w_glu, b_glu, cmp_pe_k, w_cmpk1, w_cmpk2, cmp_pe_v, w_cmpv1, w_cmpv2, w_br_ssm, w_br_nsa, w_out, g_mix_post, g_x_pre, g_mem, w_xq, w_xk, w_xv, w_xo, g_x_post, g_mlp_pre, w_up, w_down, g_mlp_post):
    depth = w_in.shape[0]
    n_seq_p, seq, _ = x_prompt.shape
    n_seq_s, nq, _ = x_sample.shape
    past = page_table.shape[1] * PAGE_SIZE
    assert depth == 1 and seq % BLK == 0 and nq <= SUBLANES and past % BLK == 0
    assert n_seq_p == SUBLANES and n_seq_s % SUBLANES == 0

    y_p = x_prompt.reshape(n_seq_p * seq, D_MODEL)
    y_s = x_sample.reshape(n_seq_s * nq, D_MODEL)
    l = 0

    w_proj = w_in[l, :, :N_PROJ].astype(BF16)
    w_gate = jnp.pad(w_in[l, :, N_PROJ:N_PROJ + N_GATE], ((0, 0), (0, LANES - N_GATE))).astype(BF16)
    w_merge = w_in[l, :, N_PROJ + N_GATE:].astype(BF16)
    lam_l, bm, cm = _ssm_params(ssm_lam_re[l], ssm_lam_im[l], ssm_log_dt[l], ssm_b_re[l], ssm_b_im[l],
                                ssm_c_re[l], ssm_c_im[l])
    ssm_w = (lam_l, bm, cm, _row(ssm_d[l]), w_glu[l].astype(BF16), _row(b_glu[l]))
    cmp_w = _compress_params(cmp_pe_k[l], w_cmpk1[l], w_cmpk2[l], cmp_pe_v[l], w_cmpv1[l], w_cmpv2[l])
    merge_w = (_row(g_mix_pre[l]), w_merge, w_br_ssm[l].astype(BF16), w_br_nsa[l].astype(BF16),
               w_out[l].astype(BF16), _row(g_mix_post[l]), _row(g_x_pre[l]), w_xq[l].astype(BF16))
    mlp_w = (w_xo[l].astype(BF16), _row(g_x_post[l]), _row(g_mlp_pre[l]), w_up[l].astype(BF16),
             w_down[l].astype(BF16), _row(g_mlp_post[l]))
    w_mem = jnp.concatenate([w_xk[l], w_xv[l]], axis=1).astype(BF16)

    tm_p = 512 if seq % 512 == 0 else seq
    nt_p = seq // tm_p
    tabs_p = _rope_tables(jnp.arange(seq, dtype=jnp.int32))
    u_p, q_p, kvt_p, wint_p, kvb_p, gate_p, cmp_rows_p = _proj(
        y_p, _row(g_mix_pre[l]), w_proj, w_gate, tabs_p, tm=tm_p, n_tab_blocks=nt_p, prompt=True)
    h0_p = jnp.zeros((1, SUBLANES, 2 * N_STATE), F32)
    tc_p = 64 if seq % 64 == 0 else seq
    ssm_p, hl_p = _ssm(u_p, h0_p, *ssm_w, n_groups=1, n_time=seq, tc=tc_p)

    nb_p = seq // BLK
    n_blocks_p = n_seq_p * nb_p
    cmp_p = _compress(cmp_rows_p, *cmp_w, n_blocks=n_blocks_p, tm=min(n_blocks_p, 256))
    nsa_p = _nsa_prompt(q_p, gate_p, cmp_p, kvb_p, n_seq=n_seq_p, seq=seq)

    mem_kv_p = _memkv(mem_prompt.reshape(-1, D_MODEL), _row(g_mem[l]), w_mem, tm=256)
    m_len = mem_prompt.shape[1]
    mem_kv_p3 = mem_kv_p.reshape(n_seq_p, m_len * 2 * X_HEADS, X_HEAD_DIM)

    if nt_p % 2:
        nsa_p = _unmirror(*nsa_p, n_seq_p)
    x1_p, qx_p = _merge(y_p, ssm_p, nsa_p, merge_w, tm=tm_p, n_tiles=nt_p)
    y_p = _mlp(x1_p, qx_p, mlp_w, tm=tm_p, mem_kv_rows=mem_kv_p3, tiles_per_seq=nt_p)

    n_s = n_seq_s * nq
    pos_s = past + jnp.arange(nq, dtype=jnp.int32)
    tabs_s = tuple(jnp.tile(t, (n_seq_s, 1)) for t in _rope_tables(pos_s))
    u_s, q_s, kv_s, win_s, kvb_s, gate_s = _proj(y_s, _row(g_mix_pre[l]), w_proj, w_gate, tabs_s,
                                                 tm=n_s, n_tab_blocks=1, prompt=False)
    n_grp = n_seq_s // SUBLANES
    n_slabs = SSM_WIDTH // LANES
    u_s = (u_s.reshape(n_grp, SUBLANES, nq, n_slabs, LANES).transpose(3, 0, 2, 1, 4)
           .reshape(n_slabs, n_s, LANES))
    st = state_ssm[l].astype(F32).reshape(n_seq_s, N_STATE, 2)
    h0_s = _state_lanes(st[..., 0], st[..., 1]).reshape(n_grp, SUBLANES, 2 * N_STATE)
    ssm_s, hl_s = _ssm(u_s, h0_s, *ssm_w, n_groups=n_grp, n_time=nq, tc=nq)
    ssm_s = (ssm_s.reshape(n_slabs, n_grp, nq, SUBLANES, LANES).transpose(0, 1, 3, 2, 4)
             .reshape(n_slabs, n_s, LANES))

    n_pages = page_table.shape[1]
    n_pool = cache_nsa_kv.shape[1]
    cache_t = jnp.transpose(cache_nsa_kv[l], (0, 2, 3, 4, 1)).reshape(n_pool, 4, KV_WIDTH, PAGE_SIZE)
    win_t = jnp.transpose(cache_win_kv[l], (0, 2, 3, 4, 1)).reshape(n_seq_s, 2, KV_WIDTH, -1)
    pt_flat = page_table.reshape(-1).astype(jnp.int32)
    nb_past = past // BLK
    pe_t, w1_t = _compress_paged_params(cmp_pe_k[l], w_cmpk1[l], cmp_pe_v[l], w_cmpv1[l])
    cmp_pages = _compress_paged(pt_flat, cache_t, pe_t, w1_t, cmp_w[2], m=min(n_seq_s * n_pages, 128))
    cmp_past = cmp_pages
    new_rows = jnp.pad(jnp.concatenate([kv_s, win_s], axis=1).reshape(n_seq_s, nq, 6 * KV_WIDTH),
                       ((0, 0), (0, LANES - nq), (0, 0)))
    cmp_new = _compress(new_rows[:, :BLK].reshape(n_seq_s * BLK, 6 * KV_WIDTH), *cmp_w,
                        n_blocks=n_seq_s, tm=n_seq_s)
    nbp = -(-(nb_past + 1) // LANES) * LANES
    cmp_s = jnp.concatenate([cmp_past.reshape(2, n_seq_s, nb_past, KV_WIDTH), cmp_new[:, :, None, :],
                             jnp.zeros((2, n_seq_s, nbp - nb_past - 1, KV_WIDTH), F32)], axis=2)
    nsa_s = _nsa_sample(pt_flat, q_s.astype(F32).reshape(n_seq_s, nq, NSA_WIDTH),
                        gate_s.reshape(n_seq_s, nq, LANES), cmp_s[0], cmp_s[1], cache_t, new_rows, win_t,
                        _expand_matrix(nbp, past + LANES), past=past, tk=min(past, 1024))

    x1_s, qx_s = _merge(y_s, ssm_s, nsa_s.reshape(n_s, NSA_WIDTH).astype(BF16), merge_w, tm=n_s, n_tiles=1)
    mem_kv_s3 = cache_mem_kv[l].reshape(n_seq_s, m_len * 2 * X_HEADS, X_HEAD_DIM)
    o_s = _xattn(qx_s.astype(F32).reshape(n_seq_s, nq, X_WIDTH), mem_kv_s3, seqs_per_step=SUBLANES)
    y_s = _mlp(x1_s, o_s.reshape(-1, X_WIDTH).astype(BF16), mlp_w, tm=n_s)

    def ssm_state(hl, n_seq):
        re, im = _state_unlanes(hl.reshape(n_seq, 2 * N_STATE))
        return jnp.stack([re, im], axis=-1).reshape(1, n_seq, SSM_GROUPS, SSM_STATE, 2)

    def token_major(xt):
        n_seq, parts, _, t_len = xt.shape
        return xt.reshape(1, n_seq, parts, NSA_KV, HEAD_DIM, t_len).transpose(0, 1, 5, 2, 3, 4)

    w_keep = min(WINDOW, seq)
    win_new = win_s.reshape(n_seq_s, nq, 2, NSA_KV, HEAD_DIM).astype(cache_win_kv.dtype)
    win_sample = jnp.concatenate([cache_win_kv[l], win_new], axis=1)[:, nq:]
    return (y_p.reshape(n_seq_p, seq, D_MODEL),
            y_s.reshape(n_seq_s, nq, D_MODEL),
            token_major(kvt_p),
            kv_s.reshape(1, n_seq_s, nq, 4, NSA_KV, HEAD_DIM),
            token_major(wint_p[:, :, :, seq - w_keep:]),
            win_sample[None],
            ssm_state(hl_p, n_seq_p),
            ssm_state(hl_s, n_seq_s),
            mem_kv_p.reshape(1, n_seq_p, m_len, 2, X_HEADS, X_HEAD_DIM))
```

```python
import functools
import math

import jax
import jax.numpy as jnp
import numpy as np
from jax import lax
from jax.experimental import pallas as pl
from jax.experimental.pallas import tpu as pltpu

F32 = jnp.float32
BF16 = jnp.bfloat16

D_MODEL = 1024
SSM_WIDTH = 512
SSM_GROUP = 16
SSM_GROUPS = 32
SSM_STATE = 64
N_STATE = SSM_GROUPS * SSM_STATE
STATE_CHUNK = 512
N_CHUNKS = N_STATE // STATE_CHUNK
NSA_HEADS = 8
HEAD_DIM = 64
NSA_WIDTH = NSA_HEADS * HEAD_DIM
NSA_KV = 2
NSA_REP = NSA_HEADS // NSA_KV
KV_WIDTH = NSA_KV * HEAD_DIM
BLK = 64
N_SEL = 16
WINDOW = 512
ROT_DIM = 16
ROPE_THETA = 500000.0
PAGE_SIZE = 128
X_HEADS = 4
X_HEAD_DIM = 128
X_WIDTH = X_HEADS * X_HEAD_DIM
D_FF = 4 * D_MODEL
EPS = 1e-6
NEG_INF = -1e30
M_INIT = -1e29
FORCE_SCORE = 1e4
LANES = 128
SUBLANES = 8
VMEM_LIMIT = 56 * 1024 * 1024

N_PROJ = SSM_WIDTH + NSA_WIDTH + 6 * KV_WIDTH
N_GATE = 3 * NSA_HEADS
SEL_CHUNK = 512
PAGE_PITCH = PAGE_SIZE + SUBLANES


def _params(*sem):
    return pltpu.CompilerParams(dimension_semantics=sem, vmem_limit_bytes=VMEM_LIMIT)


def _rms(x, g):
    return x * lax.rsqrt(jnp.mean(x * x, axis=-1, keepdims=True) + EPS) * g


def _gelu(x):
    return 0.5 * x * (1.0 + jnp.tanh(math.sqrt(2.0 / math.pi) * (x + 0.044715 * (x * x * x))))


def _sigmoid(x):
    return 1.0 / (1.0 + jnp.exp(-x))


def _dot(a, b):
    return jnp.dot(a, b, preferred_element_type=F32)


def _dot_nt(a, b):
    return lax.dot_general(a, b, (((1,), (1,)), ((), ())), preferred_element_type=F32)


def _proj_kernel(x_ref, g_ref, w_ref, wg_ref, cos_ref, sp_ref, sm_ref,
                 u_ref, q_ref, kv_ref, win_ref, kvb_ref, gate_ref, *cmp_ref, token_minor):
    a = _rms(x_ref[...], g_ref[...]).astype(BF16)
    cos, sp, sm = cos_ref[...], sp_ref[...], sm_ref[...]

    def rope(blk):
        return blk * cos + pltpu.roll(blk, 8, 1) * sp + pltpu.roll(blk, LANES - 8, 1) * sm

    pairs = {}

    def lane_block(j):
        if j // 2 not in pairs:
            pairs[j // 2] = _dot(a, w_ref[:, (j // 2) * 2 * LANES:(j // 2 + 1) * 2 * LANES])
        return pairs[j // 2][:, (j % 2) * LANES:(j % 2 + 1) * LANES]

    if token_minor:
        tm = x_ref.shape[0]
        n_seq = u_ref.shape[1] // tm
        for s in range(SSM_WIDTH // LANES):
            u_ref[s, pl.ds(pl.program_id(1), tm, stride=n_seq), :] = lane_block(s)
    else:
        u_ref[...] = _dot(a, w_ref[:, 0:SSM_WIDTH])
    for j in range(NSA_WIDTH // LANES):
        q_ref[:, j * LANES:(j + 1) * LANES] = rope(lane_block(SSM_WIDTH // LANES + j)).astype(BF16)
    for j in range(6):
        blk = lane_block((SSM_WIDTH + NSA_WIDTH) // LANES + j)
        if j % 2 == 0:
            blk = rope(blk)
        out_ref, part = (kv_ref, j) if j < 4 else (win_ref, j - 4)
        if token_minor:
            out_ref[0, part] = blk.T
            if j < 2:
                cmp_ref[0][:, j * LANES:(j + 1) * LANES] = blk
        else:
            out_ref[:, part * LANES:(part + 1) * LANES] = blk
        kvb_ref[:, j * LANES:(j + 1) * LANES] = blk.astype(BF16)
    gate_ref[...] = _sigmoid(_dot(a, wg_ref[...]))


def _proj(x2d, g, w, wg, tabs, *, tm, n_tab_blocks, prompt):
    n = x2d.shape[0]
    nt = n_tab_blocks
    n_b = n // (nt * tm)
    row = lambda w_: pl.BlockSpec((tm, w_), lambda t, b: (b * nt + t, 0))
    full = lambda a: pl.BlockSpec(a.shape, lambda t, b: (0,) * a.ndim)
    tab = pl.BlockSpec((tm, LANES), lambda t, b: (t, 0))
    if prompt:
        t_len = nt * tm
        n_slabs = SSM_WIDTH // LANES
        u_shape = jax.ShapeDtypeStruct((n_slabs, t_len * n_b, LANES), F32)
        u_spec = pl.BlockSpec((n_slabs, tm * n_b, LANES), lambda t, b: (0, t, 0))
        tok_minor = lambda parts: pl.BlockSpec((1, parts, KV_WIDTH, tm), lambda t, b: (b, 0, 0, t))
        kv_specs = [tok_minor(4), tok_minor(2)]
        kv_shapes = [jax.ShapeDtypeStruct((n_b, 4, KV_WIDTH, t_len), F32),
                     jax.ShapeDtypeStruct((n_b, 2, KV_WIDTH, t_len), F32)]
        extra_specs = [row(2 * KV_WIDTH)]
        extra_shapes = [jax.ShapeDtypeStruct((n, 2 * KV_WIDTH), F32)]
    else:
        u_shape = jax.ShapeDtypeStruct((n, SSM_WIDTH), F32)
        u_spec = row(SSM_WIDTH)
        kv_specs = [row(4 * KV_WIDTH), row(2 * KV_WIDTH)]
        kv_shapes = [jax.ShapeDtypeStruct((n, 4 * KV_WIDTH), F32), jax.ShapeDtypeStruct((n, 2 * KV_WIDTH), F32)]
        extra_specs, extra_shapes = [], []
    return pl.pallas_call(
        functools.partial(_proj_kernel, token_minor=prompt),
        grid=(nt, n_b),
        in_specs=[row(D_MODEL), full(g), full(w), full(wg), tab, tab, tab],
        out_specs=[u_spec, row(NSA_WIDTH)] + kv_specs + [row(6 * KV_WIDTH), row(LANES)] + extra_specs,
        out_shape=[u_shape, jax.ShapeDtypeStruct((n, NSA_WIDTH), BF16)] + kv_shapes
                  + [jax.ShapeDtypeStruct((n, 6 * KV_WIDTH), BF16), jax.ShapeDtypeStruct((n, LANES), F32)]
                  + extra_shapes,
        compiler_params=_params("parallel", "arbitrary"),
        name="proj",
    )(x2d, g, w, wg, *tabs)


def _rope_tables(pos):
    half = ROT_DIM // 2
    freqs = ROPE_THETA ** (-np.arange(half, dtype=np.float64) / half)
    ang = np.asarray(pos, np.float64)[:, None] * freqs[None, :]
    cos, sin = np.cos(ang), np.sin(ang)
    r = ang.shape[0]
    z8 = np.zeros((r, half))
    rest0 = np.zeros((r, HEAD_DIM - ROT_DIM))
    rest1 = np.ones((r, HEAD_DIM - ROT_DIM))
    c64 = np.concatenate([cos, cos, rest1], axis=1)
    sp64 = np.concatenate([z8, sin, rest0], axis=1)
    sm64 = np.concatenate([-sin, z8, rest0], axis=1)
    return tuple(np.tile(t, (1, LANES // HEAD_DIM)).astype(np.float32) for t in (c64, sp64, sm64))


def _ssm_kernel(u_ref, h0_ref, lam_ref, bm_ref, cm_ref, d_ref, wglu_ref, bglu_ref,
                y_ref, hlast_ref, hs_ref, hstate_ref, *, tc):
    j = pl.program_id(1)

    @pl.when(j == 0)
    def _():
        hstate_ref[...] = h0_ref[0]

    n_slabs = SSM_WIDTH // LANES
    u = jnp.concatenate([u_ref[s] for s in range(n_slabs)], axis=1)
    ub = u.astype(BF16)
    half_in = SSM_WIDTH // 2
    chunk_lanes = 2 * STATE_CHUNK
    chunks_per_half = N_CHUNKS // 2
    ys = [None, None]
    for c in range(N_CHUNKS):
        h = c // chunks_per_half
        re0 = c * chunk_lanes
        im0 = re0 + STATE_CHUNK
        u_half = ub[:, h * half_in:(h + 1) * half_in]
        hs_ref[:, re0:re0 + STATE_CHUNK] = _dot(u_half, bm_ref[c, 0])
        hs_ref[:, im0:im0 + STATE_CHUNK] = _dot(u_half, bm_ref[c, 1])
        lr = jnp.broadcast_to(lam_ref[0:1, re0:re0 + STATE_CHUNK], (SUBLANES, STATE_CHUNK))
        li = jnp.broadcast_to(lam_ref[0:1, im0:im0 + STATE_CHUNK], (SUBLANES, STATE_CHUNK))
        hr = hstate_ref[:, re0:re0 + STATE_CHUNK]
        hi = hstate_ref[:, im0:im0 + STATE_CHUNK]
        for t in range(tc):
            r0 = t * SUBLANES
            hr, hi = (lr * hr - li * hi + hs_ref[r0:r0 + SUBLANES, re0:re0 + STATE_CHUNK],
                      lr * hi + li * hr + hs_ref[r0:r0 + SUBLANES, im0:im0 + STATE_CHUNK])
            hs_ref[r0:r0 + SUBLANES, re0:re0 + STATE_CHUNK] = hr
            hs_ref[r0:r0 + SUBLANES, im0:im0 + STATE_CHUNK] = hi
        hstate_ref[:, re0:re0 + STATE_CHUNK] = hr
        hstate_ref[:, im0:im0 + STATE_CHUNK] = hi
        part = (_dot(hs_ref[:, re0:re0 + STATE_CHUNK].astype(BF16), cm_ref[c, 0])
                + _dot(hs_ref[:, im0:im0 + STATE_CHUNK].astype(BF16), cm_ref[c, 1]))
        ys[h] = part if ys[h] is None else ys[h] + part

    y = jnp.concatenate(ys, axis=1) + d_ref[...] * u
    y = _gelu(y)
    z = _dot(y.astype(BF16), wglu_ref[...]) + bglu_ref[...]
    out = y * _sigmoid(z)
    for s in range(n_slabs):
        y_ref[s] = out[:, s * LANES:(s + 1) * LANES]

    @pl.when(j == pl.num_programs(1) - 1)
    def _():
        hlast_ref[0] = hstate_ref[...]


def _ssm(u_tb, h0, lam, bm, cm, d, wglu, bglu, *, n_groups, n_time, tc):
    rows = tc * SUBLANES
    nt = n_time // tc
    n_slabs = SSM_WIDTH // LANES
    full = lambda a: pl.BlockSpec(a.shape, lambda g, j: (0,) * a.ndim)
    st = pl.BlockSpec((1, SUBLANES, 2 * N_STATE), lambda g, j: (g, 0, 0))
    slabs = pl.BlockSpec((n_slabs, rows, LANES), lambda g, j: (0, g * nt + j, 0))
    return pl.pallas_call(
        functools.partial(_ssm_kernel, tc=tc),
        grid=(n_groups, nt),
        in_specs=[slabs, st, full(lam), full(bm), full(cm), full(d), full(wglu), full(bglu)],
        out_specs=[slabs, st],
        out_shape=[jax.ShapeDtypeStruct((n_slabs, n_groups * n_time * SUBLANES, LANES), F32),
                   jax.ShapeDtypeStruct((n_groups, SUBLANES, 2 * N_STATE), F32)],
        scratch_shapes=[pltpu.VMEM((rows, 2 * N_STATE), F32), pltpu.VMEM((SUBLANES, 2 * N_STATE), F32)],
        compiler_params=_params("parallel", "arbitrary"),
        name="ssm",
    )(u_tb, h0, lam, bm, cm, d, wglu, bglu)


def _state_lanes(re, im):
    lead = re.shape[:-1]
    r = re.reshape(lead + (N_CHUNKS, 1, STATE_CHUNK))
    i = im.reshape(lead + (N_CHUNKS, 1, STATE_CHUNK))
    return jnp.concatenate([r, i], axis=-2).reshape(lead + (2 * N_STATE,))


def _state_unlanes(x):
    lead = x.shape[:-1]
    y = x.reshape(lead + (N_CHUNKS, 2, STATE_CHUNK))
    return y[..., 0, :].reshape(lead + (N_STATE,)), y[..., 1, :].reshape(lead + (N_STATE,))


def _ssm_params(lam_re, lam_im, log_dt, b_re, b_im, c_re, c_im):
    lr, li = lam_re.astype(F32), lam_im.astype(F32)
    dt = jnp.exp(log_dt.astype(F32))[:, None]
    mag = jnp.exp(lr * dt)
    bar_re, bar_im = mag * jnp.cos(li * dt), mag * jnp.sin(li * dt)
    den = lr * lr + li * li
    f_re = ((bar_re - 1.0) * lr + bar_im * li) / den
    f_im = (bar_im * lr - (bar_re - 1.0) * li) / den
    b_re, b_im = b_re.astype(F32), b_im.astype(F32)
    bb_re = f_re[..., None] * b_re - f_im[..., None] * b_im
    bb_im = f_re[..., None] * b_im + f_im[..., None] * b_re
    g_chunk = STATE_CHUNK // SSM_STATE
    g_half = SSM_GROUPS // 2

    def same_group(c):
        h = c // (N_CHUNKS // 2)
        return (h * g_half + np.arange(g_half)[:, None] == c * g_chunk + np.arange(g_chunk)[None, :]
                ).astype(np.float32)

    def in_blockdiag(x, c):
        t = x[c * g_chunk:(c + 1) * g_chunk].transpose(2, 0, 1)[None]
        return (t * same_group(c)[:, None, :, None]).reshape(SSM_WIDTH // 2, STATE_CHUNK)

    def out_blockdiag(x, c):
        t = x[c * g_chunk:(c + 1) * g_chunk].transpose(0, 2, 1)[:, :, None, :]
        return (t * same_group(c).T[:, None, :, None]).reshape(STATE_CHUNK, SSM_WIDTH // 2)

    bm = jnp.stack([jnp.stack([in_blockdiag(bb_re, c), in_blockdiag(bb_im, c)])
                    for c in range(N_CHUNKS)]).astype(BF16)
    cm = jnp.stack([jnp.stack([out_blockdiag(c_re.astype(F32), c), out_blockdiag(-c_im.astype(F32), c)])
                    for c in range(N_CHUNKS)]).astype(BF16)
    lam_l = _state_lanes(bar_re.reshape(1, N_STATE), bar_im.reshape(1, N_STATE))
    return lam_l, bm, cm


def _compress_kernel(x_ref, pe_ref, w1_ref, w2_ref, o_ref, *, tm):
    def body(sp, acc):
        s0 = 2 * sp
        xa = x_ref[pl.ds(s0, tm, stride=BLK), :] + pe_ref[0, pl.ds(s0, 1), :]
        xb = x_ref[pl.ds(s0 + 1, tm, stride=BLK), :] + pe_ref[0, pl.ds(s0 + 1, 1), :]
        lhs = jnp.concatenate([xa, xb], axis=1).astype(BF16)
        return acc + _dot(lhs, w1_ref[0, sp])

    acc = lax.fori_loop(0, BLK // 2, body, jnp.zeros((tm, KV_WIDTH), F32))
    o_ref[0] = _dot(_gelu(acc).astype(BF16), w2_ref[0])


def _compress(x2d, pe, w1, w2, *, n_blocks, tm):
    return pl.pallas_call(
        functools.partial(_compress_kernel, tm=tm),
        grid=(2, n_blocks // tm),
        in_specs=[pl.BlockSpec((tm * BLK, KV_WIDTH), lambda c, i: (i, c)),
                  pl.BlockSpec((1, BLK, KV_WIDTH), lambda c, i: (c, 0, 0)),
                  pl.BlockSpec((1, BLK // 2, 2 * KV_WIDTH, KV_WIDTH), lambda c, i: (c, 0, 0, 0)),
                  pl.BlockSpec((1, KV_WIDTH, KV_WIDTH), lambda c, i: (c, 0, 0))],
        out_specs=pl.BlockSpec((1, tm, KV_WIDTH), lambda c, i: (c, i, 0)),
        out_shape=jax.ShapeDtypeStruct((2, n_blocks, KV_WIDTH), F32),
        compiler_params=_params("parallel", "parallel"),
        name="compress",
    )(x2d, pe, w1, w2)


def _compress_params(pe_k, w1_k, w2_k, pe_v, w1_v, w2_v):
    def bd(w):
        z = jnp.zeros_like(w)
        return jnp.concatenate([jnp.concatenate([w, z], axis=-1), jnp.concatenate([z, w], axis=-1)], axis=-2)

    def one(pe, w1, w2):
        w1s = bd(w1.astype(F32).reshape(BLK, HEAD_DIM, HEAD_DIM))
        return (jnp.tile(pe.astype(F32), (1, NSA_KV)),
                w1s.reshape(BLK // 2, 2 * KV_WIDTH, KV_WIDTH).astype(BF16),
                bd(w2.astype(F32)).astype(BF16))

    k, v = one(pe_k, w1_k, w2_k), one(pe_v, w1_v, w2_v)
    return tuple(jnp.stack([a, b]) for a, b in zip(k, v))


def _compress_paged_kernel(pt_ref, cache_ref, pe_ref, w1_ref, w2_ref, o_ref, buf_ref, sems, *, m):
    step = pl.program_id(0)

    def page_copy(s, j):
        slot = s % 2
        row0 = pl.multiple_of(j * PAGE_PITCH, SUBLANES)
        return pltpu.make_async_copy(cache_ref.at[pt_ref[s * m + j], pl.ds(0, 2)],
                                     buf_ref.at[pl.ds(2 * slot, 2), pl.ds(row0, PAGE_SIZE), :],
                                     sems.at[slot])

    def start_all(s):
        def start(j, _):
            page_copy(s, j).start()
            return 0
        lax.fori_loop(0, m, start, 0, unroll=8)

    @pl.when(step == 0)
    def _():
        start_all(step)

    @pl.when(step + 1 < pl.num_programs(0))
    def _():
        start_all(step + 1)

    def wait(j, _):
        page_copy(step, j).wait()
        return 0

    lax.fori_loop(0, m, wait, 0, unroll=8)

    for c in range(2):
        tile = 2 * (step % 2) + c
        res = []
        for kv in range(NSA_KV):
            acc = jnp.zeros((m, PAGE_SIZE), F32)
            for dp in range(HEAD_DIM // 2):
                d0 = 2 * dp
                r0 = kv * HEAD_DIM + d0
                xa = buf_ref[tile, pl.ds(r0, m, stride=PAGE_PITCH), :] + pe_ref[c, d0:d0 + 1, :]
                xb = buf_ref[tile, pl.ds(r0 + 1, m, stride=PAGE_PITCH), :] + pe_ref[c, d0 + 1:d0 + 2, :]
                acc = acc + _dot(jnp.concatenate([xa, xb], axis=1).astype(BF16), w1_ref[c, dp])
            res.append(_dot(_gelu(acc).astype(BF16), w2_ref[c]))
        for blk in range(PAGE_SIZE // BLK):
            cols = slice(blk * HEAD_DIM, (blk + 1) * HEAD_DIM)
            o_ref[c, pl.ds(blk, m, stride=PAGE_SIZE // BLK), :] = jnp.concatenate(
                [r[:, cols] for r in res], axis=1)


def _compress_paged(pt_flat, cache_t, pe_t, w1_t, w2, *, m):
    n = pt_flat.shape[0]
    full = lambda a: pl.BlockSpec(a.shape, lambda i, pt: (0,) * a.ndim)
    grid_spec = pltpu.PrefetchScalarGridSpec(
        num_scalar_prefetch=1,
        grid=(n // m,),
        in_specs=[pl.BlockSpec(memory_space=pl.ANY), full(pe_t), full(w1_t), full(w2)],
        out_specs=pl.BlockSpec((2, m * (PAGE_SIZE // BLK), KV_WIDTH), lambda i, pt: (0, i, 0)),
        scratch_shapes=[pltpu.VMEM((4, m * PAGE_PITCH, PAGE_SIZE), F32), pltpu.SemaphoreType.DMA((2,))])
    return pl.pallas_call(
        functools.partial(_compress_paged_kernel, m=m),
        grid_spec=grid_spec,
        out_shape=jax.ShapeDtypeStruct((2, n * (PAGE_SIZE // BLK), KV_WIDTH), F32),
        compiler_params=_params("arbitrary"),
        name="compress_paged",
    )(pt_flat, cache_t, pe_t, w1_t, w2)


def _compress_paged_params(pe_k, w1_k, pe_v, w1_v):
    def bd(w):
        z = jnp.zeros_like(w)
        return jnp.concatenate([jnp.concatenate([w, z], axis=-1), jnp.concatenate([z, w], axis=-1)], axis=-2)

    def one(pe, w1):
        w1d = bd(w1.astype(F32).reshape(BLK, HEAD_DIM, HEAD_DIM).transpose(1, 0, 2))
        return (jnp.tile(pe.astype(F32).T, (1, PAGE_SIZE // BLK)),
                w1d.reshape(HEAD_DIM // 2, 2 * PAGE_SIZE, PAGE_SIZE).astype(BF16))

    k, v = one(pe_k, w1_k), one(pe_v, w1_v)
    return tuple(jnp.stack([a, b]) for a, b in zip(k, v))


def _stack_queries(q, nq):
    q = q.astype(F32)
    z = jnp.zeros((nq, HEAD_DIM), F32)
    rows = []
    for h in range(NSA_HEADS):
        blk = q[:, h * HEAD_DIM:(h + 1) * HEAD_DIM]
        rows.append(jnp.concatenate([blk, z] if h < NSA_REP else [z, blk], axis=1))
    return (jnp.concatenate(rows, axis=0) * (HEAD_DIM ** -0.5)).astype(BF16)


def _masked_softmax(s, valid):
    s = jnp.where(valid, s, NEG_INF)
    m = jnp.max(s, axis=-1, keepdims=True)
    p = jnp.exp(s - m) * valid.astype(F32)
    return p / jnp.maximum(jnp.sum(p, axis=-1, keepdims=True), 1e-30)


def _select_blocks(imp, n_ids, cur, nb):
    forced = (n_ids == 0) | (n_ids == cur) | (n_ids == cur - 1)
    imp = jnp.where(forced, FORCE_SCORE, imp)
    imp = jnp.where(n_ids <= cur, imp, -FORCE_SCORE)
    rank = jnp.zeros(imp.shape, F32)
    for m in range(nb):
        col = imp[:, m:m + 1]
        beats = (col > imp) | ((col == imp) & (n_ids > m))
        rank = rank + beats.astype(F32)
    return (rank < float(N_SEL)).astype(F32)


def _select_blocks_t(imp_t, cur, nb):
    n_t = lax.broadcasted_iota(jnp.int32, imp_t.shape, 0)
    forced = (n_t == 0) | (n_t == cur) | (n_t == cur - 1)
    imp_t = jnp.where(forced, FORCE_SCORE, imp_t)
    imp_t = jnp.where(n_t <= cur, imp_t, -FORCE_SCORE)
    rank = jnp.zeros(imp_t.shape, F32)
    for m in range(nb):
        row = imp_t[m:m + 1, :]
        beats = (row > imp_t) | ((row == imp_t) & (n_t > m))
        rank = rank + beats.astype(F32)
    return ((rank < float(N_SEL)) & (n_t <= cur)).astype(F32)


def _combine_start(gates, o_c, o_w, nq):
    parts = []
    for h in range(NSA_HEADS):
        rows = slice(h * nq, (h + 1) * nq)
        parts.append((gates[:, 3 * h:3 * h + 1] * o_c[rows] + gates[:, 3 * h + 2:3 * h + 3] * o_w[rows],
                      jnp.broadcast_to(gates[:, 3 * h + 1:3 * h + 2], (nq, KV_WIDTH))))
    return parts


def _combine_finish(parts, o_s, nq):
    outs = []
    for h, (rest, g_s) in enumerate(parts):
        o = rest + g_s * o_s[h * nq:(h + 1) * nq]
        g = h // NSA_REP
        outs.append(o[:, g * HEAD_DIM:(g + 1) * HEAD_DIM])
    return jnp.concatenate(outs, axis=1)


def _combine_heads(gates, o_c, o_s, o_w, nq):
    return _combine_finish(_combine_start(gates, o_c, o_w, nq), o_s, nq)


def _nsa_pair_kernel(qa_ref, qb_ref, ga_ref, gb_ref, kc_ref, vc_ref, ks_ref, vs_ref, kw_ref, vw_ref,
                     et_ref, tri_ref, wm_ref, oa_ref, ob_ref, q2_ref, etd_ref, s_ref, *, nb, ch, wk):
    i = pl.program_id(1)
    nq = BLK
    rows = NSA_HEADS * nq
    bpc = ch // BLK
    n_units = nb // bpc + 1
    chunk_of = (i, nb - 1 - i)
    n_a = i // bpc + 1
    row_q = lax.broadcasted_iota(jnp.int32, (rows, LANES), 0) % nq
    rq_minus_lane = row_q - lax.broadcasted_iota(jnp.int32, (rows, LANES), 1)

    def col_blocks(s):
        return [s[:, j * LANES:(j + 1) * LANES] for j in range(s.shape[1] // LANES)]

    def col_max(cols):
        m = cols[0]
        for c_ in cols[1:]:
            m = jnp.maximum(m, c_)
        return m

    def finish(acc):
        return acc[:, 0:KV_WIDTH] / jnp.maximum(acc[:, KV_WIDTH:2 * KV_WIDTH], 1e-30)

    def with_ones(v):
        return jnp.concatenate([v, jnp.ones((v.shape[0], LANES), BF16)], axis=1)

    q_onehot = jnp.where(rq_minus_lane == -BLK, NEG_INF, 0.0)

    def prepare(idx, q_ref, g_ref):
        ci = chunk_of[idx]
        q2 = _stack_queries(q_ref[...], nq)
        first = jnp.maximum(ci - WINDOW // BLK, 0) // (LANES // BLK)
        w0 = pl.multiple_of(first * LANES, LANES)
        delta = ci - first * (LANES // BLK)
        lhs = jnp.concatenate([q2, q_onehot.astype(BF16)], axis=1)
        rhs = jnp.concatenate([kw_ref[pl.ds(w0, wk), :], wm_ref[delta]], axis=1)
        cols = col_blocks(_dot_nt(lhs, rhs))
        yield
        q_pos = ci * BLK + lax.broadcasted_iota(jnp.int32, (rows, 1), 0) % nq
        n_ids = lax.broadcasted_iota(jnp.int32, (rows, nb), 1)
        s_c = _dot_nt(q2, kc_ref[0].astype(BF16))
        yield
        m_w = jnp.maximum(jnp.max(col_max(cols), axis=-1, keepdims=True), M_INIT)
        m_w = jnp.broadcast_to(m_w, (rows, LANES))
        p = jnp.concatenate([jnp.exp(c_ - m_w).astype(BF16) for c_ in cols], axis=1)
        yield
        p_c = _masked_softmax(s_c, (n_ids + 1) * BLK - 1 <= q_pos)
        yield
        o_w = finish(_dot(p, with_ones(vw_ref[pl.ds(w0, wk), :])))
        o_c = _dot(p_c.astype(BF16), vc_ref[0].astype(BF16))
        yield
        gated = _combine_start(g_ref[...], o_c, o_w, nq)
        yield

        imps = []
        for g in range(NSA_KV):
            imp = p_c[g * NSA_REP * nq:(g * NSA_REP + 1) * nq]
            for r in range(1, NSA_REP):
                imp = imp + p_c[(g * NSA_REP + r) * nq:(g * NSA_REP + r + 1) * nq]
            imps.append(imp)
        imp2 = jnp.concatenate([jnp.concatenate(imps, axis=0), jnp.zeros((LANES, LANES - nb), F32)], axis=1)
        imp_t = imp2.T[0:nb]
        yield
        sel_t = _select_blocks_t(imp_t, ci, nb)
        yield
        sel2 = jnp.concatenate([sel_t, jnp.ones((LANES - nb, LANES), F32)], axis=0).T
        neg = (sel2 - 1.0) * (-NEG_INF)
        neg_rows = jnp.concatenate([neg[g * nq:(g + 1) * nq] for g in range(NSA_KV) for _ in range(NSA_REP)],
                                   axis=0)
        q2_ref[idx] = jnp.concatenate([q2, (neg_rows + q_onehot).astype(BF16)], axis=1)
        in_diag = lax.broadcasted_iota(jnp.int32, (ch, LANES), 0) // BLK == ci % bpc
        etd_ref[idx] = et_ref[ci // bpc] + jnp.where(in_diag, tri_ref[...], jnp.zeros((), BF16))
        return gated

    def lockstep(*stagewise):
        results = [None] * len(stagewise)
        live = list(range(len(stagewise)))
        while live:
            for k in list(live):
                try:
                    next(stagewise[k])
                except StopIteration as done:
                    results[k] = done.value
                    live.remove(k)
        return results

    gated_a, gated_b = lockstep(prepare(0, qa_ref, ga_ref), prepare(1, qb_ref, gb_ref))

    n_a_max = (nb // 2 - 1) // bpc + 1
    mx = [jnp.full((rows, LANES), M_INIT, F32)] * 2
    units = []
    for u in range(n_units):
        which = 0 if u == 0 else 1 if u >= n_a_max else (u >= n_a).astype(jnp.int32)
        static = isinstance(which, int)
        kc = u - which * n_a
        k0 = kc * ch if isinstance(kc, int) else pl.multiple_of(kc * ch, ch)
        units.append((which, k0))
        owner = chunk_of[which] if static else jnp.where(which == 1, chunk_of[1], chunk_of[0])
        key_mask = jnp.where(kc == owner // bpc, etd_ref[which], et_ref[kc])
        rhs = jnp.concatenate([ks_ref[pl.ds(k0, ch), :], key_mask], axis=1)
        cols = col_blocks(_dot_nt(q2_ref[which], rhs))
        cm = col_max(cols)
        if static:
            mx[which] = jnp.maximum(mx[which], cm)
        else:
            mx = [jnp.maximum(mx[0], jnp.where(which == 0, cm, M_INIT)),
                  jnp.maximum(mx[1], jnp.where(which == 1, cm, M_INIT))]
        for j, c_ in enumerate(cols):
            s_ref[u, :, j * LANES:(j + 1) * LANES] = c_

    m = [jnp.broadcast_to(jnp.max(mx_i, axis=-1, keepdims=True), (rows, LANES)) for mx_i in mx]
    acc = [None, None]

    def add(total, part):
        return part if total is None else total + part

    for u, (which, k0) in enumerate(units):
        static = isinstance(which, int)
        m_u = m[which] if static else jnp.where(which == 1, m[1], m[0])
        p = jnp.concatenate([jnp.exp(s_ref[u, :, j * LANES:(j + 1) * LANES] - m_u).astype(BF16)
                             for j in range(ch // LANES)], axis=1)
        part = _dot(p, with_ones(vs_ref[pl.ds(k0, ch), :]))
        if static:
            acc[which] = add(acc[which], part)
        else:
            w_b = (which == 1).astype(F32)
            acc = [add(acc[0], part * (1.0 - w_b)), add(acc[1], part * w_b)]

    oa_ref[...] = _combine_finish(gated_a, finish(acc[0]), nq).astype(oa_ref.dtype)
    ob_ref[...] = _combine_finish(gated_b, finish(acc[1]), nq).astype(ob_ref.dtype)


def _nsa_prompt(q, gates, cmp_kv, kvb, *, n_seq, seq):
    nb = seq // BLK
    rows = NSA_HEADS * BLK
    ch = min(SEL_CHUNK, seq)
    wk = min(WINDOW + LANES, seq)
    bpc = ch // BLK
    assert seq % ch == 0 and wk % LANES == 0 and nb % 2 == 0 and nb <= LANES
    half = nb // 2
    n_units = nb // bpc + 1
    assert nb <= BLK
    et = np.zeros((seq // ch, ch, LANES), np.float32)
    for c in range(seq // ch):
        et[c, np.arange(ch), (c * ch + np.arange(ch)) // BLK] = 1.0
    key = np.arange(ch)[:, None]
    qry = np.arange(BLK)[None, :]
    tri = np.zeros((ch, LANES), np.float32)
    tri[:, BLK:] = (key % BLK) > qry
    n_delta = WINDOW // BLK + LANES // BLK
    wm = np.zeros((n_delta, wk, LANES), np.float32)
    for dl in range(n_delta):
        dist = dl * BLK + qry - np.arange(wk)[:, None]
        wm[dl, :, BLK:] = (dist < 0) | (dist >= WINDOW)
    chunk_a = lambda w_: pl.BlockSpec((BLK, w_), lambda b, i: (b * nb + i, 0))
    chunk_b = lambda w_: pl.BlockSpec((BLK, w_), lambda b, i: (b * nb + nb - 1 - i, 0))
    out_spec = pl.BlockSpec((BLK, NSA_WIDTH), lambda b, i: (b * half + i, 0))
    kv = lambda col: pl.BlockSpec((seq, KV_WIDTH), lambda b, i: (b, col))
    cmp_ = lambda which: pl.BlockSpec((1, nb, KV_WIDTH), lambda b, i: (which, b, 0))
    out = jax.ShapeDtypeStruct((n_seq * half * BLK, NSA_WIDTH), BF16)
    o_a, o_b = pl.pallas_call(
        functools.partial(_nsa_pair_kernel, nb=nb, ch=ch, wk=wk),
        grid=(n_seq, half),
        in_specs=[chunk_a(NSA_WIDTH), chunk_b(NSA_WIDTH), chunk_a(LANES), chunk_b(LANES),
                  cmp_(0), cmp_(1), kv(2), kv(3), kv(4), kv(5),
                  pl.BlockSpec(et.shape, lambda b, i: (0, 0, 0)),
                  pl.BlockSpec(tri.shape, lambda b, i: (0, 0)),
                  pl.BlockSpec(wm.shape, lambda b, i: (0, 0, 0))],
        out_specs=[out_spec, out_spec],
        out_shape=[out, out],
        scratch_shapes=[pltpu.VMEM((2, rows, 2 * LANES), BF16),
                        pltpu.VMEM((2, ch, LANES), BF16),
                        pltpu.VMEM((n_units, rows, ch), F32)],
        compiler_params=_params("parallel", "arbitrary"),
        name="nsa_prompt",
    )(q, q, gates, gates, cmp_kv, cmp_kv, kvb, kvb, kvb, kvb,
      jnp.asarray(et, dtype=BF16), jnp.asarray(tri, dtype=BF16), jnp.asarray(wm, dtype=BF16))
    return o_a, o_b


def _unmirror(o_a, o_b, n_seq):
    half = o_a.shape[0] // (n_seq * BLK)
    o_a = o_a.reshape(n_seq, half, BLK, NSA_WIDTH)
    o_b = o_b.reshape(n_seq, half, BLK, NSA_WIDTH)[:, ::-1]
    return jnp.concatenate([o_a, o_b], axis=1).reshape(-1, NSA_WIDTH)


def _nsa_sample_kernel(pt_ref, q_ref, gate_ref, kc_ref, vc_ref, cache_ref, nks_ref, nvs_ref,
                       wt_ref, nkw_ref, nvw_ref, e_ref, o_ref, kv_buf, s_scr, sems,
                       *, nq, past, nbp, tk):
    b = pl.program_id(0)
    n_pages = past // PAGE_SIZE
    rows = NSA_HEADS * nq

    def page_copy(s, j):
        slot = s % 2
        k0 = pl.multiple_of(j * PAGE_SIZE, PAGE_SIZE)
        return pltpu.make_async_copy(cache_ref.at[pt_ref[s * n_pages + j], pl.ds(2, 2)],
                                     kv_buf.at[pl.ds(2 * slot, 2), :, pl.ds(k0, PAGE_SIZE)], sems.at[slot])

    def start_all(s):
        def start(j, _):
            page_copy(s, j).start()
            return 0
        lax.fori_loop(0, n_pages, start, 0, unroll=8)

    @pl.when(b == 0)
    def _():
        start_all(b)

    @pl.when(b + 1 < pl.num_programs(0))
    def _():
        start_all(b + 1)

    slot = b % 2
    q2 = _stack_queries(q_ref[0], nq)
    q_pos = past + lax.broadcasted_iota(jnp.int32, (rows, 1), 0) % nq
    cur = past // BLK

    n_ids = lax.broadcasted_iota(jnp.int32, (rows, nbp), 1)
    s_c = _dot_nt(q2, kc_ref[0].astype(BF16))
    p_c = _masked_softmax(s_c, (n_ids + 1) * BLK - 1 <= q_pos)
    o_c = _dot(p_c.astype(BF16), vc_ref[0].astype(BF16))

    n_sel = lax.broadcasted_iota(jnp.int32, (nq, nbp), 1)
    sels = []
    for g in range(NSA_KV):
        imp = p_c[g * NSA_REP * nq:(g * NSA_REP + 1) * nq]
        for r in range(1, NSA_REP):
            imp = imp + p_c[(g * NSA_REP + r) * nq:(g * NSA_REP + r + 1) * nq]
        sel = _select_blocks(imp, n_sel, cur, cur + 1)
        sels.extend([sel] * NSA_REP)
    keys = _dot(jnp.concatenate(sels, axis=0).astype(BF16), e_ref[...])
    bias = (keys - 1.0) * (-NEG_INF)

    def col_max(s):
        m = s[:, 0:LANES]
        for j in range(1, s.shape[1] // LANES):
            m = jnp.maximum(m, s[:, j * LANES:(j + 1) * LANES])
        return m

    def col_sum(p):
        t = p[:, 0:LANES]
        for j in range(1, p.shape[1] // LANES):
            t = t + p[:, j * LANES:(j + 1) * LANES]
        return t

    def row_max(mx):
        return jnp.broadcast_to(jnp.max(mx, axis=-1, keepdims=True), mx.shape)

    def tiled(m, width):
        return m if width == LANES else jnp.concatenate([m] * (width // LANES), axis=1)

    new_pos = past + lax.broadcasted_iota(jnp.int32, (rows, LANES), 1)

    def wait(j, _):
        page_copy(b, j).wait()
        return 0

    lax.fori_loop(0, n_pages, wait, 0, unroll=8)

    mx = jnp.full((rows, LANES), M_INIT, F32)
    for t in range(past // tk):
        s = _dot(q2, kv_buf[2 * slot, :, t * tk:(t + 1) * tk].astype(BF16)) + bias[:, t * tk:(t + 1) * tk]
        s_scr[:, t * tk:(t + 1) * tk] = s
        mx = jnp.maximum(mx, col_max(s))
    s_new = _dot_nt(q2, nks_ref[0].astype(BF16)) + bias[:, past:past + LANES]
    s_new = jnp.where(new_pos <= q_pos, s_new, NEG_INF)
    m_s = row_max(jnp.maximum(mx, s_new))
    p_new = jnp.exp(s_new - m_s)
    acc = _dot(p_new.astype(BF16), nvs_ref[0].astype(BF16))
    lsum = p_new
    for t in range(past // tk):
        p = jnp.exp(s_scr[:, t * tk:(t + 1) * tk] - tiled(m_s, tk))
        lsum = lsum + col_sum(p)
        acc = acc + _dot_nt(p.astype(BF16), kv_buf[2 * slot + 1, :, t * tk:(t + 1) * tk].astype(BF16))
    o_s = acc / jnp.maximum(jnp.sum(lsum, axis=-1, keepdims=True), 1e-30)

    wlen = wt_ref.shape[3]
    w_pos = past - wlen + lax.broadcasted_iota(jnp.int32, (rows, wlen), 1)
    d = q_pos - w_pos
    s_w = _dot(q2, wt_ref[0, 0].astype(BF16))
    s_w = jnp.where((d >= 0) & (d < WINDOW) & (w_pos >= 0), s_w, NEG_INF)
    d = q_pos - new_pos
    s_nw = _dot_nt(q2, nkw_ref[0].astype(BF16))
    s_nw = jnp.where((d >= 0) & (d < WINDOW), s_nw, NEG_INF)
    m_w = row_max(jnp.maximum(jnp.maximum(col_max(s_w), s_nw), M_INIT))
    p_w = jnp.exp(s_w - tiled(m_w, wlen))
    p_nw = jnp.exp(s_nw - m_w)
    acc = _dot_nt(p_w.astype(BF16), wt_ref[0, 1].astype(BF16)) + _dot(p_nw.astype(BF16), nvw_ref[0].astype(BF16))
    o_w = acc / jnp.maximum(jnp.sum(col_sum(p_w) + p_nw, axis=-1, keepdims=True), 1e-30)

    o_ref[0] = _combine_heads(gate_ref[0], o_c, o_s, o_w, nq).astype(o_ref.dtype)


def _nsa_sample(pt_flat, q3, gates3, kc, vc, cache_t, new_rows, win_t, expand, *, past, tk):
    n_seq, nq, _ = q3.shape
    nbp = kc.shape[1]
    rows = NSA_HEADS * nq
    per_seq = lambda a: pl.BlockSpec((1,) + a.shape[1:], lambda b, pt: (b,) + (0,) * (a.ndim - 1))
    new = lambda col: pl.BlockSpec((1, LANES, KV_WIDTH), lambda b, pt: (b, 0, col))
    grid_spec = pltpu.PrefetchScalarGridSpec(
        num_scalar_prefetch=1,
        grid=(n_seq,),
        in_specs=[per_seq(q3), per_seq(gates3), per_seq(kc), per_seq(vc),
                  pl.BlockSpec(memory_space=pl.ANY),
                  new(2), new(3), per_seq(win_t), new(4), new(5),
                  pl.BlockSpec(expand.shape, lambda b, pt: (0, 0))],
        out_specs=pl.BlockSpec((1, nq, NSA_WIDTH), lambda b, pt: (b, 0, 0)),
        scratch_shapes=[pltpu.VMEM((4, KV_WIDTH, past), F32),
                        pltpu.VMEM((rows, past), F32), pltpu.SemaphoreType.DMA((2,))])
    return pl.pallas_call(
        functools.partial(_nsa_sample_kernel, nq=nq, past=past, nbp=nbp, tk=tk),
        grid_spec=grid_spec,
        out_shape=jax.ShapeDtypeStruct((n_seq, nq, NSA_WIDTH), F32),
        compiler_params=_params("arbitrary"),
        name="nsa_sample",
    )(pt_flat, q3, gates3, kc, vc, cache_t, new_rows, new_rows, win_t, new_rows, new_rows, expand)


def _expand_matrix(nb, n_keys):
    return jnp.asarray(np.arange(n_keys)[None, :] // BLK == np.arange(nb)[:, None], dtype=BF16)


def _merge_kernel(x_ref, ssm_ref, *refs, mirrored):
    nsa_refs = refs[:2] if mirrored else refs[:1]
    gpre_ref, wm_ref, wbs_ref, wbn_ref, wo_ref, gpost_ref, gx_ref, wxq_ref, x1_ref, qx_ref = refs[len(nsa_refs):]
    x = x_ref[...]
    tm = x.shape[0]
    if mirrored:
        upper = nsa_refs[1][...]
        n_chunks = tm // BLK
        upper = jnp.concatenate([upper[(n_chunks - 1 - s) * BLK:(n_chunks - s) * BLK] for s in range(n_chunks)],
                                axis=0)
        nsa = jnp.where(pl.program_id(0) < pl.num_programs(0) // 2, nsa_refs[0][...], upper)
    else:
        nsa = nsa_refs[0][...]
    n_seq = ssm_ref.shape[1] // tm
    ssm = jnp.concatenate([ssm_ref[s, pl.ds(pl.program_id(1), tm, stride=n_seq), :]
                           for s in range(SSM_WIDTH // LANES)], axis=1)
    a = _rms(x, gpre_ref[...]).astype(BF16)
    g_ssm = _sigmoid(_dot(a, wm_ref[:, 0:D_MODEL]))
    g_nsa = _sigmoid(_dot(a, wm_ref[:, D_MODEL:2 * D_MODEL]))
    merged = (g_ssm * _dot(ssm.astype(BF16), wbs_ref[...])
              + g_nsa * _dot(nsa, wbn_ref[...]))
    x1 = x + _rms(_dot(merged.astype(BF16), wo_ref[...]), gpost_ref[...])
    x1_ref[...] = x1
    c = _rms(x1, gx_ref[...]).astype(BF16)
    qx_ref[...] = (_dot(c, wxq_ref[...]) * (X_HEAD_DIM ** -0.5)).astype(BF16)


def _merge(x2d, ssm_slabs, nsa_o, weights, *, tm, n_tiles):
    n = x2d.shape[0]
    n_b = n // (n_tiles * tm)
    row = lambda w_: pl.BlockSpec((tm, w_), lambda t, b: (b * n_tiles + t, 0))
    full = lambda a: pl.BlockSpec(a.shape, lambda t, b: (0,) * a.ndim, pipeline_mode=pl.Buffered(1))
    ssm_spec = pl.BlockSpec((SSM_WIDTH // LANES, tm * n_b, LANES), lambda t, b: (0, t, 0))
    mirrored = isinstance(nsa_o, tuple)
    if mirrored:
        half = n_tiles // 2
        nsa_specs = [pl.BlockSpec((tm, NSA_WIDTH), lambda t, b: (b * half + jnp.minimum(t, half - 1), 0)),
                     pl.BlockSpec((tm, NSA_WIDTH), lambda t, b: (b * half + jnp.minimum(n_tiles - 1 - t, half - 1), 0))]
        nsa_args = list(nsa_o)
    else:
        nsa_specs, nsa_args = [row(NSA_WIDTH)], [nsa_o]
    return pl.pallas_call(
        functools.partial(_merge_kernel, mirrored=mirrored),
        grid=(n_tiles, n_b),
        in_specs=[row(D_MODEL), ssm_spec] + nsa_specs + [full(w) for w in weights],
        out_specs=[row(D_MODEL), row(X_WIDTH)],
        out_shape=[jax.ShapeDtypeStruct((n, D_MODEL), F32), jax.ShapeDtypeStruct((n, X_WIDTH), BF16)],
        compiler_params=_params("parallel", "arbitrary"),
        name="merge",
    )(x2d, ssm_slabs, *nsa_args, *weights)


def _xattn_tile(q, kv_ref, m_len, seq=0):
    outs = []
    for h in range(X_HEADS):
        cols = slice(h * X_HEAD_DIM, (h + 1) * X_HEAD_DIM)
        k = kv_ref[seq, pl.ds(h, m_len, stride=2 * X_HEADS), :].astype(BF16)
        v = kv_ref[seq, pl.ds(X_HEADS + h, m_len, stride=2 * X_HEADS), :].astype(BF16)
        s = _dot_nt(q[:, cols], k)
        m = jnp.max(s, axis=-1, keepdims=True)
        p = jnp.exp(s - m)
        p = p / jnp.sum(p, axis=-1, keepdims=True)
        outs.append(_dot(p.astype(BF16), v))
    return jnp.concatenate(outs, axis=1)


def _xattn_kernel(q_ref, kv_ref, o_ref, *, m_len):
    for s in range(q_ref.shape[0]):
        o_ref[s] = _xattn_tile(q_ref[s].astype(BF16), kv_ref, m_len, s).astype(o_ref.dtype)


def _xattn(q3, mem_kv_rows, *, seqs_per_step):
    n_seq, t, _ = q3.shape
    m_len = mem_kv_rows.shape[1] // (2 * X_HEADS)
    sb = seqs_per_step
    return pl.pallas_call(
        functools.partial(_xattn_kernel, m_len=m_len),
        grid=(n_seq // sb,),
        in_specs=[pl.BlockSpec((sb, t, X_WIDTH), lambda b: (b, 0, 0)),
                  pl.BlockSpec((sb, m_len * 2 * X_HEADS, X_HEAD_DIM), lambda b: (b, 0, 0))],
        out_specs=pl.BlockSpec((sb, t, X_WIDTH), lambda b: (b, 0, 0)),
        out_shape=jax.ShapeDtypeStruct((n_seq, t, X_WIDTH), q3.dtype),
        compiler_params=_params("parallel"),
        name="xattn",
    )(q3, mem_kv_rows)


def _mlp_kernel(x1_ref, o_ref, *refs, m_len):
    if m_len:
        kv_ref, refs = refs[0], refs[1:]
        o = _xattn_tile(o_ref[...], kv_ref, m_len).astype(BF16)
    else:
        o = o_ref[...]
    wxo_ref, gxp_ref, gm_ref, wup_ref, wdn_ref, gmp_ref, y_ref = refs
    x2 = x1_ref[...] + _rms(_dot(o, wxo_ref[...]), gxp_ref[...])
    m = _rms(x2, gm_ref[...]).astype(BF16)
    hid = jnp.maximum(_dot(m, wup_ref[...]), 0.0)
    hid = (hid * hid).astype(BF16)
    y_ref[...] = x2 + _rms(_dot(hid, wdn_ref[...]), gmp_ref[...])


def _mlp(x1, o, weights, *, tm, mem_kv_rows=None, tiles_per_seq=1):
    n = x1.shape[0]
    row = lambda w_: pl.BlockSpec((tm, w_), lambda i: (i, 0))
    full = lambda a: pl.BlockSpec(a.shape, lambda i: (0,) * a.ndim, pipeline_mode=pl.Buffered(1))
    m_len, kv_specs, kv_args = 0, [], []
    if mem_kv_rows is not None:
        m_len = mem_kv_rows.shape[1] // (2 * X_HEADS)
        kv_specs = [pl.BlockSpec((1,) + mem_kv_rows.shape[1:], lambda i: (i // tiles_per_seq, 0, 0))]
        kv_args = [mem_kv_rows]
    return pl.pallas_call(
        functools.partial(_mlp_kernel, m_len=m_len),
        grid=(n // tm,),
        in_specs=[row(D_MODEL), row(X_WIDTH)] + kv_specs + [full(w) for w in weights],
        out_specs=row(D_MODEL),
        out_shape=jax.ShapeDtypeStruct((n, D_MODEL), F32),
        compiler_params=_params("parallel"),
        name="mlp",
    )(x1, o, *kv_args, *weights)


def _memkv_kernel(m_ref, g_ref, w_ref, o_ref, *, tm):
    kv = _dot(_rms(m_ref[...], g_ref[...]).astype(BF16), w_ref[...])
    n_rows = 2 * X_HEADS
    for j in range(n_rows):
        o_ref[pl.ds(j, tm, stride=n_rows), :] = kv[:, j * X_HEAD_DIM:(j + 1) * X_HEAD_DIM]


def _memkv(mem2d, g, w, *, tm):
    n = mem2d.shape[0]
    n_rows = 2 * X_HEADS
    return pl.pallas_call(
        functools.partial(_memkv_kernel, tm=tm),
        grid=(n // tm,),
        in_specs=[pl.BlockSpec((tm, D_MODEL), lambda i: (i, 0)),
                  pl.BlockSpec(g.shape, lambda i: (0, 0)), pl.BlockSpec(w.shape, lambda i: (0, 0))],
        out_specs=pl.BlockSpec((tm * n_rows, X_HEAD_DIM), lambda i: (i, 0)),
        out_shape=jax.ShapeDtypeStruct((n * n_rows, X_HEAD_DIM), F32),
        compiler_params=_params("parallel"),
        name="memkv",
    )(mem2d, g, w)


def _row(v):
    return v.astype(F32).reshape(1, -1)


def kernel(x_prompt, x_sample, cache_nsa_kv, cache_win_kv, state_ssm, cache_mem_kv, page_table, mem_prompt, g_mix_pre, w_in, ssm_lam_re, ssm_lam_im, ssm_log_dt, ssm_b_re, ssm_b_im, ssm_c_re, ssm_c_im, ssm_d, w_glu, b_glu, cmp_pe_k, w_cmpk1, w_cmpk2, cmp_pe_v, w_cmpv1, w_cmpv2, w_br_ssm, w_br_nsa, w_out, g_mix_post, g_x_pre, g_mem, w_xq, w_xk, w_xv, w_xo, g_x_post, g_mlp_pre, w_up, w_down, g_mlp_post):
    depth = w_in.shape[0]
    n_seq_p, seq, _ = x_prompt.shape
    n_seq_s, nq, _ = x_sample.shape
    past = page_table.shape[1] * PAGE_SIZE
    assert depth == 1 and seq % BLK == 0 and nq <= SUBLANES and past % BLK == 0
    assert n_seq_p == SUBLANES and n_seq_s % SUBLANES == 0

    y_p = x_prompt.reshape(n_seq_p * seq, D_MODEL)
    y_s = x_sample.reshape(n_seq_s * nq, D_MODEL)
    l = 0

    w_proj = w_in[l, :, :N_PROJ].astype(BF16)
    w_gate = jnp.pad(w_in[l, :, N_PROJ:N_PROJ + N_GATE], ((0, 0), (0, LANES - N_GATE))).astype(BF16)
    w_merge = w_in[l, :, N_PROJ + N_GATE:].astype(BF16)
    lam_l, bm, cm = _ssm_params(ssm_lam_re[l], ssm_lam_im[l], ssm_log_dt[l], ssm_b_re[l], ssm_b_im[l],
                                ssm_c_re[l], ssm_c_im[l])
    ssm_w = (lam_l, bm, cm, _row(ssm_d[l]), w_glu[l].astype(BF16), _row(b_glu[l]))
    cmp_w = _compress_params(cmp_pe_k[l], w_cmpk1[l], w_cmpk2[l], cmp_pe_v[l], w_cmpv1[l], w_cmpv2[l])
    merge_w = (_row(g_mix_pre[l]), w_merge, w_br_ssm[l].astype(BF16), w_br_nsa[l].astype(BF16),
               w_out[l].astype(BF16), _row(g_mix_post[l]), _row(g_x_pre[l]), w_xq[l].astype(BF16))
    mlp_w = (w_xo[l].astype(BF16), _row(g_x_post[l]), _row(g_mlp_pre[l]), w_up[l].astype(BF16),
             w_down[l].astype(BF16), _row(g_mlp_post[l]))
    w_mem = jnp.concatenate([w_xk[l], w_xv[l]], axis=1).astype(BF16)

    tm_p = 512 if seq % 512 == 0 else seq
    nt_p = seq // tm_p
    tabs_p = _rope_tables(np.arange(seq))
    u_p, q_p, kvt_p, wint_p, kvb_p, gate_p, cmp_rows_p = _proj(
        y_p, _row(g_mix_pre[l]), w_proj, w_gate, tabs_p, tm=tm_p, n_tab_blocks=nt_p, prompt=True)
    h0_p = jnp.zeros((1, SUBLANES, 2 * N_STATE), F32)
    tc_p = 64 if seq % 64 == 0 else seq
    ssm_p, hl_p = _ssm(u_p, h0_p, *ssm_w, n_groups=1, n_time=seq, tc=tc_p)

    nb_p = seq // BLK
    n_blocks_p = n_seq_p * nb_p
    cmp_p = _compress(cmp_rows_p, *cmp_w, n_blocks=n_blocks_p, tm=min(n_blocks_p, 256))
    nsa_p = _nsa_prompt(q_p, gate_p, cmp_p, kvb_p, n_seq=n_seq_p, seq=seq)

    mem_kv_p = _memkv(mem_prompt.reshape(-1, D_MODEL), _row(g_mem[l]), w_mem, tm=256)
    m_len = mem_prompt.shape[1]
    mem_kv_p3 = mem_kv_p.reshape(n_seq_p, m_len * 2 * X_HEADS, X_HEAD_DIM)

    if nt_p % 2:
        nsa_p = _unmirror(*nsa_p, n_seq_p)
    x1_p, qx_p = _merge(y_p, ssm_p, nsa_p, merge_w, tm=tm_p, n_tiles=nt_p)
    y_p = _mlp(x1_p, qx_p, mlp_w, tm=tm_p, mem_kv_rows=mem_kv_p3, tiles_per_seq=nt_p)

    n_s = n_seq_s * nq
    tabs_s = tuple(np.tile(t, (n_seq_s, 1)) for t in _rope_tables(past + np.arange(nq)))
    u_s, q_s, kv_s, win_s, kvb_s, gate_s = _proj(y_s, _row(g_mix_pre[l]), w_proj, w_gate, tabs_s,
                                                 tm=n_s, n_tab_blocks=1, prompt=False)
    n_grp = n_seq_s // SUBLANES
    n_slabs = SSM_WIDTH // LANES
    u_s = (u_s.reshape(n_grp, SUBLANES, nq, n_slabs, LANES).transpose(3, 0, 2, 1, 4)
           .reshape(n_slabs, n_s, LANES))
    st = state_ssm[l].astype(F32).reshape(n_seq_s, N_STATE, 2)
    h0_s = _state_lanes(st[..., 0], st[..., 1]).reshape(n_grp, SUBLANES, 2 * N_STATE)
    ssm_s, hl_s = _ssm(u_s, h0_s, *ssm_w, n_groups=n_grp, n_time=nq, tc=nq)
    ssm_s = (ssm_s.reshape(n_slabs, n_grp, nq, SUBLANES, LANES).transpose(0, 1, 3, 2, 4)
             .reshape(n_slabs, n_s, LANES))

    n_pages = page_table.shape[1]
    n_pool = cache_nsa_kv.shape[1]
    cache_t = jnp.transpose(cache_nsa_kv[l], (0, 2, 3, 4, 1)).reshape(n_pool, 4, KV_WIDTH, PAGE_SIZE)
    win_t = jnp.transpose(cache_win_kv[l], (0, 2, 3, 4, 1)).reshape(n_seq_s, 2, KV_WIDTH, -1)
    pt_flat = page_table.reshape(-1).astype(jnp.int32)
    nb_past = past // BLK
    pe_t, w1_t = _compress_paged_params(cmp_pe_k[l], w_cmpk1[l], cmp_pe_v[l], w_cmpv1[l])
    cmp_pages = _compress_paged(pt_flat, cache_t, pe_t, w1_t, cmp_w[2], m=min(n_seq_s * n_pages, 128))
    cmp_past = cmp_pages
    new_rows = jnp.pad(jnp.concatenate([kv_s, win_s], axis=1).reshape(n_seq_s, nq, 6 * KV_WIDTH),
                       ((0, 0), (0, LANES - nq), (0, 0)))
    cmp_new = _compress(new_rows[:, :BLK].reshape(n_seq_s * BLK, 6 * KV_WIDTH), *cmp_w,
                        n_blocks=n_seq_s, tm=n_seq_s)
    nbp = -(-(nb_past + 1) // LANES) * LANES
    cmp_s = jnp.concatenate([cmp_past.reshape(2, n_seq_s, nb_past, KV_WIDTH), cmp_new[:, :, None, :],
                             jnp.zeros((2, n_seq_s, nbp - nb_past - 1, KV_WIDTH), F32)], axis=2)
    nsa_s = _nsa_sample(pt_flat, q_s.astype(F32).reshape(n_seq_s, nq, NSA_WIDTH),
                        gate_s.reshape(n_seq_s, nq, LANES), cmp_s[0], cmp_s[1], cache_t, new_rows, win_t,
                        _expand_matrix(nbp, past + LANES), past=past, tk=min(past, 1024))

    x1_s, qx_s = _merge(y_s, ssm_s, nsa_s.reshape(n_s, NSA_WIDTH).astype(BF16), merge_w, tm=n_s, n_tiles=1)
    mem_kv_s3 = cache_mem_kv[l].reshape(n_seq_s, m_len * 2 * X_HEADS, X_HEAD_DIM)
    o_s = _xattn(qx_s.astype(F32).reshape(n_seq_s, nq, X_WIDTH), mem_kv_s3, seqs_per_step=SUBLANES)
    y_s = _mlp(x1_s, o_s.reshape(-1, X_WIDTH).astype(BF16), mlp_w, tm=n_s)

    def ssm_state(hl, n_seq):
        re, im = _state_unlanes(hl.reshape(n_seq, 2 * N_STATE))
        return jnp.stack([re, im], axis=-1).reshape(1, n_seq, SSM_GROUPS, SSM_STATE, 2)

    def token_major(xt):
        n_seq, parts, _, t_len = xt.shape
        return xt.reshape(1, n_seq, parts, NSA_KV, HEAD_DIM, t_len).transpose(0, 1, 5, 2, 3, 4)

    w_keep = min(WINDOW, seq)
    win_new = win_s.reshape(n_seq_s, nq, 2, NSA_KV, HEAD_DIM).astype(cache_win_kv.dtype)
    win_sample = jnp.concatenate([cache_win_kv[l], win_new], axis=1)[:, nq:]
    return (y_p.reshape(n_seq_p, seq, D_MODEL),
            y_s.reshape(n_seq_s, nq, D_MODEL),
            token_major(kvt_p),
            kv_s.reshape(1, n_seq_s, nq, 4, NSA_KV, HEAD_DIM),
            token_major(wint_p[:, :, :, seq - w_keep:]),
            win_sample[None],
            ssm_state(hl_p, n_seq_p),
            ssm_state(hl_s, n_seq_s),
            mem_kv_p.reshape(1, n_seq_p, m_len, 2, X_HEADS, X_HEAD_DIM))
```

```python
import functools
import math

import jax
import jax.numpy as jnp
import numpy as np
from jax import lax
from jax.experimental import pallas as pl
from jax.experimental.pallas import tpu as pltpu

F32 = jnp.float32
BF16 = jnp.bfloat16

D_MODEL = 1024
SSM_WIDTH = 512
SSM_GROUP = 16
SSM_GROUPS = 32
SSM_STATE = 64
N_STATE = SSM_GROUPS * SSM_STATE
STATE_CHUNK = 512
N_CHUNKS = N_STATE // STATE_CHUNK
NSA_HEADS = 8
HEAD_DIM = 64
NSA_WIDTH = NSA_HEADS * HEAD_DIM
NSA_KV = 2
NSA_REP = NSA_HEADS // NSA_KV
KV_WIDTH = NSA_KV * HEAD_DIM
BLK = 64
N_SEL = 16
WINDOW = 512
ROT_DIM = 16
ROPE_THETA = 500000.0
PAGE_SIZE = 128
X_HEADS = 4
X_HEAD_DIM = 128
X_WIDTH = X_HEADS * X_HEAD_DIM
D_FF = 4 * D_MODEL
EPS = 1e-6
NEG_INF = -1e30
M_INIT = -1e29
FORCE_SCORE = 1e4
LANES = 128
SUBLANES = 8
VMEM_LIMIT = 56 * 1024 * 1024

N_PROJ = SSM_WIDTH + NSA_WIDTH + 6 * KV_WIDTH
N_GATE = 3 * NSA_HEADS
SEL_CHUNK = 512
PAGE_PITCH = PAGE_SIZE + SUBLANES


def _params(*sem):
    return pltpu.CompilerParams(dimension_semantics=sem, vmem_limit_bytes=VMEM_LIMIT)


def _rms(x, g):
    return x * lax.rsqrt(jnp.mean(x * x, axis=-1, keepdims=True) + EPS) * g


def _gelu(x):
    return 0.5 * x * (1.0 + jnp.tanh(math.sqrt(2.0 / math.pi) * (x + 0.044715 * (x * x * x))))


def _sigmoid(x):
    return 1.0 / (1.0 + jnp.exp(-x))


def _dot(a, b):
    return jnp.dot(a, b, preferred_element_type=F32)


def _dot_nt(a, b):
    return lax.dot_general(a, b, (((1,), (1,)), ((), ())), preferred_element_type=F32)


def _lockstep(*stagewise):
    results = [None] * len(stagewise)
    live = list(range(len(stagewise)))
    while live:
        for k in list(live):
            try:
                next(stagewise[k])
            except StopIteration as done:
                results[k] = done.value
                live.remove(k)
    return results


def _proj_kernel(x_ref, g_ref, w_ref, wg_ref, cos_ref, sp_ref, sm_ref,
                 u_ref, q_ref, kv_ref, win_ref, kvb_ref, gate_ref, *, token_minor):
    a = _rms(x_ref[...], g_ref[...]).astype(BF16)
    cos, sp, sm = cos_ref[...], sp_ref[...], sm_ref[...]

    def rope(blk):
        return blk * cos + pltpu.roll(blk, 8, 1) * sp + pltpu.roll(blk, LANES - 8, 1) * sm

    pairs = {}

    def lane_block(j):
        if j // 2 not in pairs:
            pairs[j // 2] = _dot(a, w_ref[:, (j // 2) * 2 * LANES:(j // 2 + 1) * 2 * LANES])
        return pairs[j // 2][:, (j % 2) * LANES:(j % 2 + 1) * LANES]

    if token_minor:
        tm = x_ref.shape[0]
        n_seq = u_ref.shape[1] // tm
        for s in range(SSM_WIDTH // LANES):
            u_ref[s, pl.ds(pl.program_id(1), tm, stride=n_seq), :] = lane_block(s)
    else:
        u_ref[...] = _dot(a, w_ref[:, 0:SSM_WIDTH])
    for j in range(NSA_WIDTH // LANES):
        q_ref[:, j * LANES:(j + 1) * LANES] = rope(lane_block(SSM_WIDTH // LANES + j)).astype(BF16)
    for j in range(6):
        blk = lane_block((SSM_WIDTH + NSA_WIDTH) // LANES + j)
        if j % 2 == 0:
            blk = rope(blk)
        out_ref, part = (kv_ref, j) if j < 4 else (win_ref, j - 4)
        if token_minor:
            out_ref[0, part] = blk.T
        else:
            out_ref[:, part * LANES:(part + 1) * LANES] = blk
        kvb_ref[:, j * LANES:(j + 1) * LANES] = blk.astype(BF16)
    gate_ref[...] = _sigmoid(_dot(a, wg_ref[...]))


def _proj(x2d, g, w, wg, tabs, *, tm, n_tab_blocks, prompt):
    n = x2d.shape[0]
    nt = n_tab_blocks
    n_b = n // (nt * tm)
    row = lambda w_: pl.BlockSpec((tm, w_), lambda t, b: (b * nt + t, 0))
    full = lambda a: pl.BlockSpec(a.shape, lambda t, b: (0,) * a.ndim)
    tab = pl.BlockSpec((tm, LANES), lambda t, b: (t, 0))
    if prompt:
        t_len = nt * tm
        n_slabs = SSM_WIDTH // LANES
        u_shape = jax.ShapeDtypeStruct((n_slabs, t_len * n_b, LANES), F32)
        u_spec = pl.BlockSpec((n_slabs, tm * n_b, LANES), lambda t, b: (0, t, 0))
        tok_minor = lambda parts: pl.BlockSpec((1, parts, KV_WIDTH, tm), lambda t, b: (b, 0, 0, t))
        kv_specs = [tok_minor(4), tok_minor(2)]
        kv_shapes = [jax.ShapeDtypeStruct((n_b, 4, KV_WIDTH, t_len), F32),
                     jax.ShapeDtypeStruct((n_b, 2, KV_WIDTH, t_len), F32)]
    else:
        u_shape = jax.ShapeDtypeStruct((n, SSM_WIDTH), F32)
        u_spec = row(SSM_WIDTH)
        kv_specs = [row(4 * KV_WIDTH), row(2 * KV_WIDTH)]
        kv_shapes = [jax.ShapeDtypeStruct((n, 4 * KV_WIDTH), F32), jax.ShapeDtypeStruct((n, 2 * KV_WIDTH), F32)]
    return pl.pallas_call(
        functools.partial(_proj_kernel, token_minor=prompt),
        grid=(nt, n_b),
        in_specs=[row(D_MODEL), full(g), full(w), full(wg), tab, tab, tab],
        out_specs=[u_spec, row(NSA_WIDTH)] + kv_specs + [row(6 * KV_WIDTH), row(LANES)],
        out_shape=[u_shape, jax.ShapeDtypeStruct((n, NSA_WIDTH), BF16)] + kv_shapes
                  + [jax.ShapeDtypeStruct((n, 6 * KV_WIDTH), BF16), jax.ShapeDtypeStruct((n, LANES), F32)],
        compiler_params=_params("parallel", "arbitrary"),
        name="proj",
    )(x2d, g, w, wg, *tabs)


def _rope_tables(pos):
    half = ROT_DIM // 2
    freqs = ROPE_THETA ** (-np.arange(half, dtype=np.float64) / half)
    ang = np.asarray(pos, np.float64)[:, None] * freqs[None, :]
    cos, sin = np.cos(ang), np.sin(ang)
    r = ang.shape[0]
    z8 = np.zeros((r, half))
    rest0 = np.zeros((r, HEAD_DIM - ROT_DIM))
    rest1 = np.ones((r, HEAD_DIM - ROT_DIM))
    c64 = np.concatenate([cos, cos, rest1], axis=1)
    sp64 = np.concatenate([z8, sin, rest0], axis=1)
    sm64 = np.concatenate([-sin, z8, rest0], axis=1)
    return tuple(np.tile(t, (1, LANES // HEAD_DIM)).astype(np.float32) for t in (c64, sp64, sm64))


def _ssm_kernel(u_ref, h0_ref, lam_ref, bm_ref, cm_ref, d_ref, wglu_ref, bglu_ref,
                y_ref, hlast_ref, hs_ref, hstate_ref, *, tc):
    j = pl.program_id(1)

    @pl.when(j == 0)
    def _():
        hstate_ref[...] = h0_ref[0]

    n_slabs = SSM_WIDTH // LANES
    u = jnp.concatenate([u_ref[s] for s in range(n_slabs)], axis=1)
    ub = u.astype(BF16)
    half_in = SSM_WIDTH // 2
    chunk_lanes = 2 * STATE_CHUNK
    chunks_per_half = N_CHUNKS // 2
    ys = [None, None]
    for c in range(N_CHUNKS):
        h = c // chunks_per_half
        re0 = c * chunk_lanes
        im0 = re0 + STATE_CHUNK
        u_half = ub[:, h * half_in:(h + 1) * half_in]
        hs_ref[:, re0:re0 + STATE_CHUNK] = _dot(u_half, bm_ref[c, 0])
        hs_ref[:, im0:im0 + STATE_CHUNK] = _dot(u_half, bm_ref[c, 1])
        lr = jnp.broadcast_to(lam_ref[0:1, re0:re0 + STATE_CHUNK], (SUBLANES, STATE_CHUNK))
        li = jnp.broadcast_to(lam_ref[0:1, im0:im0 + STATE_CHUNK], (SUBLANES, STATE_CHUNK))
        hr = hstate_ref[:, re0:re0 + STATE_CHUNK]
        hi = hstate_ref[:, im0:im0 + STATE_CHUNK]
        for t in range(tc):
            r0 = t * SUBLANES
            hr, hi = (lr * hr - li * hi + hs_ref[r0:r0 + SUBLANES, re0:re0 + STATE_CHUNK],
                      lr * hi + li * hr + hs_ref[r0:r0 + SUBLANES, im0:im0 + STATE_CHUNK])
            hs_ref[r0:r0 + SUBLANES, re0:re0 + STATE_CHUNK] = hr
            hs_ref[r0:r0 + SUBLANES, im0:im0 + STATE_CHUNK] = hi
        hstate_ref[:, re0:re0 + STATE_CHUNK] = hr
        hstate_ref[:, im0:im0 + STATE_CHUNK] = hi
        part = (_dot(hs_ref[:, re0:re0 + STATE_CHUNK].astype(BF16), cm_ref[c, 0])
                + _dot(hs_ref[:, im0:im0 + STATE_CHUNK].astype(BF16), cm_ref[c, 1]))
        ys[h] = part if ys[h] is None else ys[h] + part

    y = jnp.concatenate(ys, axis=1) + d_ref[...] * u
    y = _gelu(y)
    z = _dot(y.astype(BF16), wglu_ref[...]) + bglu_ref[...]
    out = y * _sigmoid(z)
    for s in range(n_slabs):
        y_ref[s] = out[:, s * LANES:(s + 1) * LANES]

    @pl.when(j == pl.num_programs(1) - 1)
    def _():
        hlast_ref[0] = hstate_ref[...]


def _ssm(u_tb, h0, lam, bm, cm, d, wglu, bglu, *, n_groups, n_time, tc):
    rows = tc * SUBLANES
    nt = n_time // tc
    n_slabs = SSM_WIDTH // LANES
    full = lambda a: pl.BlockSpec(a.shape, lambda g, j: (0,) * a.ndim)
    st = pl.BlockSpec((1, SUBLANES, 2 * N_STATE), lambda g, j: (g, 0, 0))
    slabs = pl.BlockSpec((n_slabs, rows, LANES), lambda g, j: (0, g * nt + j, 0))
    return pl.pallas_call(
        functools.partial(_ssm_kernel, tc=tc),
        grid=(n_groups, nt),
        in_specs=[slabs, st, full(lam), full(bm), full(cm), full(d), full(wglu), full(bglu)],
        out_specs=[slabs, st],
        out_shape=[jax.ShapeDtypeStruct((n_slabs, n_groups * n_time * SUBLANES, LANES), F32),
                   jax.ShapeDtypeStruct((n_groups, SUBLANES, 2 * N_STATE), F32)],
        scratch_shapes=[pltpu.VMEM((rows, 2 * N_STATE), F32), pltpu.VMEM((SUBLANES, 2 * N_STATE), F32)],
        compiler_params=_params("parallel", "arbitrary"),
        name="ssm",
    )(u_tb, h0, lam, bm, cm, d, wglu, bglu)


def _state_lanes(re, im):
    lead = re.shape[:-1]
    r = re.reshape(lead + (N_CHUNKS, 1, STATE_CHUNK))
    i = im.reshape(lead + (N_CHUNKS, 1, STATE_CHUNK))
    return jnp.concatenate([r, i], axis=-2).reshape(lead + (2 * N_STATE,))


def _state_unlanes(x):
    lead = x.shape[:-1]
    y = x.reshape(lead + (N_CHUNKS, 2, STATE_CHUNK))
    return y[..., 0, :].reshape(lead + (N_STATE,)), y[..., 1, :].reshape(lead + (N_STATE,))


def _ssm_params(lam_re, lam_im, log_dt, b_re, b_im, c_re, c_im):
    lr, li = lam_re.astype(F32), lam_im.astype(F32)
    dt = jnp.exp(log_dt.astype(F32))[:, None]
    mag = jnp.exp(lr * dt)
    bar_re, bar_im = mag * jnp.cos(li * dt), mag * jnp.sin(li * dt)
    den = lr * lr + li * li
    f_re = ((bar_re - 1.0) * lr + bar_im * li) / den
    f_im = (bar_im * lr - (bar_re - 1.0) * li) / den
    b_re, b_im = b_re.astype(F32), b_im.astype(F32)
    bb_re = f_re[..., None] * b_re - f_im[..., None] * b_im
    bb_im = f_re[..., None] * b_im + f_im[..., None] * b_re
    g_chunk = STATE_CHUNK // SSM_STATE
    g_half = SSM_GROUPS // 2

    def same_group(c):
        h = c // (N_CHUNKS // 2)
        return (h * g_half + np.arange(g_half)[:, None] == c * g_chunk + np.arange(g_chunk)[None, :]
                ).astype(np.float32)

    def in_blockdiag(x, c):
        t = x[c * g_chunk:(c + 1) * g_chunk].transpose(2, 0, 1)[None]
        return (t * same_group(c)[:, None, :, None]).reshape(SSM_WIDTH // 2, STATE_CHUNK)

    def out_blockdiag(x, c):
        t = x[c * g_chunk:(c + 1) * g_chunk].transpose(0, 2, 1)[:, :, None, :]
        return (t * same_group(c).T[:, None, :, None]).reshape(STATE_CHUNK, SSM_WIDTH // 2)

    bm = jnp.stack([jnp.stack([in_blockdiag(bb_re, c), in_blockdiag(bb_im, c)])
                    for c in range(N_CHUNKS)]).astype(BF16)
    cm = jnp.stack([jnp.stack([out_blockdiag(c_re.astype(F32), c), out_blockdiag(-c_im.astype(F32), c)])
                    for c in range(N_CHUNKS)]).astype(BF16)
    lam_l = _state_lanes(bar_re.reshape(1, N_STATE), bar_im.reshape(1, N_STATE))
    return lam_l, bm, cm


def _compress_kernel(x_ref, pe_ref, w1_ref, w2_ref, o_ref, *, tm):
    acc = jnp.zeros((tm, KV_WIDTH), F32)
    for sp in range(BLK // 2):
        s0 = 2 * sp
        xa = x_ref[pl.ds(s0, tm, stride=BLK), :] + pe_ref[0, s0:s0 + 1, :]
        xb = x_ref[pl.ds(s0 + 1, tm, stride=BLK), :] + pe_ref[0, s0 + 1:s0 + 2, :]
        acc = acc + _dot(jnp.concatenate([xa, xb], axis=1).astype(BF16), w1_ref[0, sp])
    o_ref[0] = _dot(_gelu(acc).astype(BF16), w2_ref[0])


def _compress(x2d, pe, w1, w2, *, n_blocks, tm):
    return pl.pallas_call(
        functools.partial(_compress_kernel, tm=tm),
        grid=(2, n_blocks // tm),
        in_specs=[pl.BlockSpec((tm * BLK, KV_WIDTH), lambda c, i: (i, c)),
                  pl.BlockSpec((1, BLK, KV_WIDTH), lambda c, i: (c, 0, 0)),
                  pl.BlockSpec((1, BLK // 2, 2 * KV_WIDTH, KV_WIDTH), lambda c, i: (c, 0, 0, 0)),
                  pl.BlockSpec((1, KV_WIDTH, KV_WIDTH), lambda c, i: (c, 0, 0))],
        out_specs=pl.BlockSpec((1, tm, KV_WIDTH), lambda c, i: (c, i, 0)),
        out_shape=jax.ShapeDtypeStruct((2, n_blocks, KV_WIDTH), F32),
        compiler_params=_params("parallel", "parallel"),
        name="compress",
    )(x2d, pe, w1, w2)


def _compress_params(pe_k, w1_k, w2_k, pe_v, w1_v, w2_v):
    def bd(w):
        z = jnp.zeros_like(w)
        return jnp.concatenate([jnp.concatenate([w, z], axis=-1), jnp.concatenate([z, w], axis=-1)], axis=-2)

    def one(pe, w1, w2):
        w1s = bd(w1.astype(F32).reshape(BLK, HEAD_DIM, HEAD_DIM))
        return (jnp.tile(pe.astype(F32), (1, NSA_KV)),
                w1s.reshape(BLK // 2, 2 * KV_WIDTH, KV_WIDTH).astype(BF16),
                bd(w2.astype(F32)).astype(BF16))

    k, v = one(pe_k, w1_k, w2_k), one(pe_v, w1_v, w2_v)
    return tuple(jnp.stack([a, b]) for a, b in zip(k, v))


def _compress_paged_kernel(pt_ref, cache_ref, pe_ref, w1_ref, w2_ref, o_ref, buf_ref, sems, *, m,
                           pages_per_seq):
    step = pl.program_id(0)

    def page_copy(s, j):
        slot = s % 2
        row0 = pl.multiple_of(j * PAGE_PITCH, SUBLANES)
        idx = s * m + j
        if pages_per_seq:
            tok0 = pl.multiple_of((idx % pages_per_seq) * PAGE_SIZE, PAGE_SIZE)
            src = cache_ref.at[idx // pages_per_seq, pl.ds(0, 2), :, pl.ds(tok0, PAGE_SIZE)]
        else:
            src = cache_ref.at[pt_ref[idx], pl.ds(0, 2)]
        return pltpu.make_async_copy(src, buf_ref.at[pl.ds(2 * slot, 2), pl.ds(row0, PAGE_SIZE), :],
                                     sems.at[slot])

    def start_all(s):
        def start(j, _):
            page_copy(s, j).start()
            return 0
        lax.fori_loop(0, m, start, 0, unroll=8)

    @pl.when(step == 0)
    def _():
        start_all(step)

    @pl.when(step + 1 < pl.num_programs(0))
    def _():
        start_all(step + 1)

    def wait(j, _):
        page_copy(step, j).wait()
        return 0

    lax.fori_loop(0, m, wait, 0, unroll=8)

    for c in range(2):
        tile = 2 * (step % 2) + c
        res = []
        for kv in range(NSA_KV):
            acc = jnp.zeros((m, PAGE_SIZE), F32)
            for dp in range(HEAD_DIM // 2):
                d0 = 2 * dp
                r0 = kv * HEAD_DIM + d0
                xa = buf_ref[tile, pl.ds(r0, m, stride=PAGE_PITCH), :] + pe_ref[c, d0:d0 + 1, :]
                xb = buf_ref[tile, pl.ds(r0 + 1, m, stride=PAGE_PITCH), :] + pe_ref[c, d0 + 1:d0 + 2, :]
                acc = acc + _dot(jnp.concatenate([xa, xb], axis=1).astype(BF16), w1_ref[c, dp])
            res.append(_dot(_gelu(acc).astype(BF16), w2_ref[c]))
        for blk in range(PAGE_SIZE // BLK):
            cols = slice(blk * HEAD_DIM, (blk + 1) * HEAD_DIM)
            o_ref[c, pl.ds(blk, m, stride=PAGE_SIZE // BLK), :] = jnp.concatenate(
                [r[:, cols] for r in res], axis=1)


def _compress_paged(pt_flat, cache_t, pe_t, w1_t, w2, *, m, pages_per_seq=0):
    n = pt_flat.shape[0]
    full = lambda a: pl.BlockSpec(a.shape, lambda i, pt: (0,) * a.ndim)
    grid_spec = pltpu.PrefetchScalarGridSpec(
        num_scalar_prefetch=1,
        grid=(n // m,),
        in_specs=[pl.BlockSpec(memory_space=pl.ANY), full(pe_t), full(w1_t), full(w2)],
        out_specs=pl.BlockSpec((2, m * (PAGE_SIZE // BLK), KV_WIDTH), lambda i, pt: (0, i, 0)),
        scratch_shapes=[pltpu.VMEM((4, m * PAGE_PITCH, PAGE_SIZE), F32), pltpu.SemaphoreType.DMA((2,))])
    return pl.pallas_call(
        functools.partial(_compress_paged_kernel, m=m, pages_per_seq=pages_per_seq),
        grid_spec=grid_spec,
        out_shape=jax.ShapeDtypeStruct((2, n * (PAGE_SIZE // BLK), KV_WIDTH), F32),
        compiler_params=_params("arbitrary"),
        name="compress_paged",
    )(pt_flat, cache_t, pe_t, w1_t, w2)


def _compress_paged_params(pe_k, w1_k, pe_v, w1_v):
    def bd(w):
        z = jnp.zeros_like(w)
        return jnp.concatenate([jnp.concatenate([w, z], axis=-1), jnp.concatenate([z, w], axis=-1)], axis=-2)

    def one(pe, w1):
        w1d = bd(w1.astype(F32).reshape(BLK, HEAD_DIM, HEAD_DIM).transpose(1, 0, 2))
        return (jnp.tile(pe.astype(F32).T, (1, PAGE_SIZE // BLK)),
                w1d.reshape(HEAD_DIM // 2, 2 * PAGE_SIZE, PAGE_SIZE).astype(BF16))

    k, v = one(pe_k, w1_k), one(pe_v, w1_v)
    return tuple(jnp.stack([a, b]) for a, b in zip(k, v))


def _stack_queries(q, nq):
    q = q.astype(F32)
    z = jnp.zeros((nq, HEAD_DIM), F32)
    rows = []
    for h in range(NSA_HEADS):
        blk = q[:, h * HEAD_DIM:(h + 1) * HEAD_DIM]
        rows.append(jnp.concatenate([blk, z] if h < NSA_REP else [z, blk], axis=1))
    return (jnp.concatenate(rows, axis=0) * (HEAD_DIM ** -0.5)).astype(BF16)


def _masked_softmax(s, valid):
    s = jnp.where(valid, s, NEG_INF)
    m = jnp.max(s, axis=-1, keepdims=True)
    p = jnp.exp(s - m) * valid.astype(F32)
    return p / jnp.maximum(jnp.sum(p, axis=-1, keepdims=True), 1e-30)


def _select_blocks(imp, n_ids, cur, nb):
    forced = (n_ids == 0) | (n_ids == cur) | (n_ids == cur - 1)
    imp = jnp.where(forced, FORCE_SCORE, imp)
    imp = jnp.where(n_ids <= cur, imp, -FORCE_SCORE)
    rank = jnp.zeros(imp.shape, F32)
    for m in range(nb):
        col = imp[:, m:m + 1]
        beats = (col > imp) | ((col == imp) & (n_ids > m))
        rank = rank + beats.astype(F32)
    return (rank < float(N_SEL)).astype(F32)


def _select_blocks_t(imp_t, cur, nb):
    n_t = lax.broadcasted_iota(jnp.int32, imp_t.shape, 0)
    forced = (n_t == 0) | (n_t == cur) | (n_t == cur - 1)
    imp_t = jnp.where(forced, FORCE_SCORE, imp_t)
    imp_t = jnp.where(n_t <= cur, imp_t, -FORCE_SCORE)
    rank = jnp.zeros(imp_t.shape, F32)
    for m in range(nb):
        row = imp_t[m:m + 1, :]
        beats = (row > imp_t) | ((row == imp_t) & (n_t > m))
        rank = rank + beats.astype(F32)
    return ((rank < float(N_SEL)) & (n_t <= cur)).astype(F32)


def _combine_start(gates, o_c, o_w, nq):
    parts = []
    for h in range(NSA_HEADS):
        rows = slice(h * nq, (h + 1) * nq)
        parts.append((gates[:, 3 * h:3 * h + 1] * o_c[rows] + gates[:, 3 * h + 2:3 * h + 3] * o_w[rows],
                      jnp.broadcast_to(gates[:, 3 * h + 1:3 * h + 2], (nq, KV_WIDTH))))
    return parts


def _combine_finish(parts, o_s, nq):
    outs = []
    for h, (rest, g_s) in enumerate(parts):
        o = rest + g_s * o_s[h * nq:(h + 1) * nq]
        g = h // NSA_REP
        outs.append(o[:, g * HEAD_DIM:(g + 1) * HEAD_DIM])
    return jnp.concatenate(outs, axis=1)


def _combine_heads(gates, o_c, o_s, o_w, nq):
    return _combine_finish(_combine_start(gates, o_c, o_w, nq), o_s, nq)


def _nsa_pair_kernel(qa_ref, qb_ref, ga_ref, gb_ref, kc_ref, vc_ref, ks_ref, vs_ref, kw_ref, vw_ref,
                     et_ref, tri_ref, wm_ref, oa_ref, ob_ref, q2_ref, etd_ref, s_ref, *, nb, ch, wk):
    i = pl.program_id(1)
    nq = BLK
    rows = NSA_HEADS * nq
    bpc = ch // BLK
    n_units = nb // bpc + 1
    chunk_of = (i, nb - 1 - i)
    n_a = i // bpc + 1
    row_q = lax.broadcasted_iota(jnp.int32, (rows, LANES), 0) % nq
    rq_minus_lane = row_q - lax.broadcasted_iota(jnp.int32, (rows, LANES), 1)

    def col_blocks(s):
        return [s[:, j * LANES:(j + 1) * LANES] for j in range(s.shape[1] // LANES)]

    def col_max(cols):
        m = cols[0]
        for c_ in cols[1:]:
            m = jnp.maximum(m, c_)
        return m

    def finish(acc):
        return acc[:, 0:KV_WIDTH] / jnp.maximum(acc[:, KV_WIDTH:2 * KV_WIDTH], 1e-30)

    def with_ones(v):
        return jnp.concatenate([v, jnp.ones((v.shape[0], LANES), BF16)], axis=1)

    q_onehot = jnp.where(rq_minus_lane == -BLK, NEG_INF, 0.0)

    def prepare(idx, q_ref, g_ref):
        ci = chunk_of[idx]
        q2 = _stack_queries(q_ref[...], nq)
        first = jnp.maximum(ci - WINDOW // BLK, 0) // (LANES // BLK)
        w0 = pl.multiple_of(first * LANES, LANES)
        delta = ci - first * (LANES // BLK)
        lhs = jnp.concatenate([q2, q_onehot.astype(BF16)], axis=1)
        rhs = jnp.concatenate([kw_ref[pl.ds(w0, wk), :], wm_ref[delta]], axis=1)
        cols = col_blocks(_dot_nt(lhs, rhs))
        yield
        q_pos = ci * BLK + lax.broadcasted_iota(jnp.int32, (rows, 1), 0) % nq
        n_ids = lax.broadcasted_iota(jnp.int32, (rows, nb), 1)
        s_c = _dot_nt(q2, kc_ref[0].astype(BF16))
        yield
        m_w = jnp.maximum(jnp.max(col_max(cols), axis=-1, keepdims=True), M_INIT)
        m_w = jnp.broadcast_to(m_w, (rows, LANES))
        p = jnp.concatenate([jnp.exp(c_ - m_w).astype(BF16) for c_ in cols], axis=1)
        yield
        p_c = _masked_softmax(s_c, (n_ids + 1) * BLK - 1 <= q_pos)
        yield
        o_w = finish(_dot(p, with_ones(vw_ref[pl.ds(w0, wk), :])))
        o_c = _dot(p_c.astype(BF16), vc_ref[0].astype(BF16))
        yield
        gated = _combine_start(g_ref[...], o_c, o_w, nq)
        yield

        imps = []
        for g in range(NSA_KV):
            imp = p_c[g * NSA_REP * nq:(g * NSA_REP + 1) * nq]
            for r in range(1, NSA_REP):
                imp = imp + p_c[(g * NSA_REP + r) * nq:(g * NSA_REP + r + 1) * nq]
            imps.append(imp)
        imp2 = jnp.concatenate([jnp.concatenate(imps, axis=0), jnp.zeros((LANES, LANES - nb), F32)], axis=1)
        imp_t = imp2.T[0:nb]
        yield
        sel_t = _select_blocks_t(imp_t, ci, nb)
        yield
        sel2 = jnp.concatenate([sel_t, jnp.ones((LANES - nb, LANES), F32)], axis=0).T
        neg = (sel2 - 1.0) * (-NEG_INF)
        neg_rows = jnp.concatenate([neg[g * nq:(g + 1) * nq] for g in range(NSA_KV) for _ in range(NSA_REP)],
                                   axis=0)
        q2_ref[idx] = jnp.concatenate([q2, (neg_rows + q_onehot).astype(BF16)], axis=1)
        in_diag = lax.broadcasted_iota(jnp.int32, (ch, LANES), 0) // BLK == ci % bpc
        etd_ref[idx] = et_ref[ci // bpc] + jnp.where(in_diag, tri_ref[...], jnp.zeros((), BF16))
        return gated

    gated_a, gated_b = _lockstep(prepare(0, qa_ref, ga_ref), prepare(1, qb_ref, gb_ref))

    n_a_max = (nb // 2 - 1) // bpc + 1
    mx = [jnp.full((rows, LANES), M_INIT, F32)] * 2
    units = []
    for u in range(n_units):
        which = 0 if u == 0 else 1 if u >= n_a_max else (u >= n_a).astype(jnp.int32)
        static = isinstance(which, int)
        kc = u - which * n_a
        k0 = kc * ch if isinstance(kc, int) else pl.multiple_of(kc * ch, ch)
        units.append((which, k0))
        owner = chunk_of[which] if static else jnp.where(which == 1, chunk_of[1], chunk_of[0])
        key_mask = jnp.where(kc == owner // bpc, etd_ref[which], et_ref[kc])
        rhs = jnp.concatenate([ks_ref[pl.ds(k0, ch), :], key_mask], axis=1)
        cols = col_blocks(_dot_nt(q2_ref[which], rhs))
        cm = col_max(cols)
        if static:
            mx[which] = jnp.maximum(mx[which], cm)
        else:
            mx = [jnp.maximum(mx[0], jnp.where(which == 0, cm, M_INIT)),
                  jnp.maximum(mx[1], jnp.where(which == 1, cm, M_INIT))]
        for j, c_ in enumerate(cols):
            s_ref[u, :, j * LANES:(j + 1) * LANES] = c_

    m = [jnp.broadcast_to(jnp.max(mx_i, axis=-1, keepdims=True), (rows, LANES)) for mx_i in mx]
    acc = [None, None]

    def add(total, part):
        return part if total is None else total + part

    for u, (which, k0) in enumerate(units):
        static = isinstance(which, int)
        m_u = m[which] if static else jnp.where(which == 1, m[1], m[0])
        p = jnp.concatenate([jnp.exp(s_ref[u, :, j * LANES:(j + 1) * LANES] - m_u).astype(BF16)
                             for j in range(ch // LANES)], axis=1)
        part = _dot(p, with_ones(vs_ref[pl.ds(k0, ch), :]))
        if static:
            acc[which] = add(acc[which], part)
        else:
            w_b = (which == 1).astype(F32)
            acc = [add(acc[0], part * (1.0 - w_b)), add(acc[1], part * w_b)]

    oa_ref[...] = _combine_finish(gated_a, finish(acc[0]), nq).astype(oa_ref.dtype)
    ob_ref[...] = _combine_finish(gated_b, finish(acc[1]), nq).astype(ob_ref.dtype)


def _nsa_prompt(q, gates, cmp_kv, kvb, *, n_seq, seq):
    nb = seq // BLK
    rows = NSA_HEADS * BLK
    ch = min(SEL_CHUNK, seq)
    wk = min(WINDOW + LANES, seq)
    bpc = ch // BLK
    assert seq % ch == 0 and wk % LANES == 0 and nb % 2 == 0 and nb <= LANES
    half = nb // 2
    n_units = nb // bpc + 1
    assert nb <= BLK
    et = np.zeros((seq // ch, ch, LANES), np.float32)
    for c in range(seq // ch):
        et[c, np.arange(ch), (c * ch + np.arange(ch)) // BLK] = 1.0
    key = np.arange(ch)[:, None]
    qry = np.arange(BLK)[None, :]
    tri = np.zeros((ch, LANES), np.float32)
    tri[:, BLK:] = (key % BLK) > qry
    n_delta = WINDOW // BLK + LANES // BLK
    wm = np.zeros((n_delta, wk, LANES), np.float32)
    for dl in range(n_delta):
        dist = dl * BLK + qry - np.arange(wk)[:, None]
        wm[dl, :, BLK:] = (dist < 0) | (dist >= WINDOW)
    chunk_a = lambda w_: pl.BlockSpec((BLK, w_), lambda b, i: (b * nb + i, 0))
    chunk_b = lambda w_: pl.BlockSpec((BLK, w_), lambda b, i: (b * nb + nb - 1 - i, 0))
    out_spec = pl.BlockSpec((BLK, NSA_WIDTH), lambda b, i: (b * half + i, 0))
    kv = lambda col: pl.BlockSpec((seq, KV_WIDTH), lambda b, i: (b, col))
    cmp_ = lambda which: pl.BlockSpec((1, nb, KV_WIDTH), lambda b, i: (which, b, 0))
    out = jax.ShapeDtypeStruct((n_seq * half * BLK, NSA_WIDTH), BF16)
    o_a, o_b = pl.pallas_call(
        functools.partial(_nsa_pair_kernel, nb=nb, ch=ch, wk=wk),
        grid=(n_seq, half),
        in_specs=[chunk_a(NSA_WIDTH), chunk_b(NSA_WIDTH), chunk_a(LANES), chunk_b(LANES),
                  cmp_(0), cmp_(1), kv(2), kv(3), kv(4), kv(5),
                  pl.BlockSpec(et.shape, lambda b, i: (0, 0, 0)),
                  pl.BlockSpec(tri.shape, lambda b, i: (0, 0)),
                  pl.BlockSpec(wm.shape, lambda b, i: (0, 0, 0))],
        out_specs=[out_spec, out_spec],
        out_shape=[out, out],
        scratch_shapes=[pltpu.VMEM((2, rows, 2 * LANES), BF16),
                        pltpu.VMEM((2, ch, LANES), BF16),
                        pltpu.VMEM((n_units, rows, ch), F32)],
        compiler_params=_params("parallel", "arbitrary"),
        name="nsa_prompt",
    )(q, q, gates, gates, cmp_kv, cmp_kv, kvb, kvb, kvb, kvb,
      jnp.asarray(et, dtype=BF16), jnp.asarray(tri, dtype=BF16), jnp.asarray(wm, dtype=BF16))
    return o_a, o_b


def _unmirror(o_a, o_b, n_seq):
    half = o_a.shape[0] // (n_seq * BLK)
    o_a = o_a.reshape(n_seq, half, BLK, NSA_WIDTH)
    o_b = o_b.reshape(n_seq, half, BLK, NSA_WIDTH)[:, ::-1]
    return jnp.concatenate([o_a, o_b], axis=1).reshape(-1, NSA_WIDTH)


def _nsa_sample_kernel(pt_ref, q_ref, gate_ref, kc_ref, vc_ref, cache_ref, nks_ref, nvs_ref,
                       wt_ref, nkw_ref, nvw_ref, e_ref, o_ref, kv_buf, s_scr, sems,
                       *, nq, past, nbp, tk):
    b = pl.program_id(0)
    n_pages = past // PAGE_SIZE
    rows = NSA_HEADS * nq

    def page_copy(s, j):
        slot = s % 2
        k0 = pl.multiple_of(j * PAGE_SIZE, PAGE_SIZE)
        return pltpu.make_async_copy(cache_ref.at[pt_ref[s * n_pages + j], pl.ds(2, 2)],
                                     kv_buf.at[pl.ds(2 * slot, 2), :, pl.ds(k0, PAGE_SIZE)], sems.at[slot])

    def start_all(s):
        def start(j, _):
            page_copy(s, j).start()
            return 0
        lax.fori_loop(0, n_pages, start, 0, unroll=8)

    @pl.when(b == 0)
    def _():
        start_all(b)

    @pl.when(b + 1 < pl.num_programs(0))
    def _():
        start_all(b + 1)

    slot = b % 2
    q2 = _stack_queries(q_ref[0], nq)
    q_pos = past + lax.broadcasted_iota(jnp.int32, (rows, 1), 0) % nq
    cur = past // BLK

    n_ids = lax.broadcasted_iota(jnp.int32, (rows, nbp), 1)
    s_c = _dot_nt(q2, kc_ref[0].astype(BF16))
    p_c = _masked_softmax(s_c, (n_ids + 1) * BLK - 1 <= q_pos)
    o_c = _dot(p_c.astype(BF16), vc_ref[0].astype(BF16))

    n_sel = lax.broadcasted_iota(jnp.int32, (nq, nbp), 1)
    sels = []
    for g in range(NSA_KV):
        imp = p_c[g * NSA_REP * nq:(g * NSA_REP + 1) * nq]
        for r in range(1, NSA_REP):
            imp = imp + p_c[(g * NSA_REP + r) * nq:(g * NSA_REP + r + 1) * nq]
        sel = _select_blocks(imp, n_sel, cur, cur + 1)
        sels.extend([sel] * NSA_REP)
    keys = _dot(jnp.concatenate(sels, axis=0).astype(BF16), e_ref[...])
    bias = (keys - 1.0) * (-NEG_INF)

    def col_max(s):
        m = s[:, 0:LANES]
        for j in range(1, s.shape[1] // LANES):
            m = jnp.maximum(m, s[:, j * LANES:(j + 1) * LANES])
        return m

    def col_sum(p):
        t = p[:, 0:LANES]
        for j in range(1, p.shape[1] // LANES):
            t = t + p[:, j * LANES:(j + 1) * LANES]
        return t

    def row_max(mx):
        return jnp.broadcast_to(jnp.max(mx, axis=-1, keepdims=True), mx.shape)

    def tiled(m, width):
        return m if width == LANES else jnp.concatenate([m] * (width // LANES), axis=1)

    new_pos = past + lax.broadcasted_iota(jnp.int32, (rows, LANES), 1)

    def wait(j, _):
        page_copy(b, j).wait()
        return 0

    lax.fori_loop(0, n_pages, wait, 0, unroll=8)

    mx = jnp.full((rows, LANES), M_INIT, F32)
    for t in range(past // tk):
        s = _dot(q2, kv_buf[2 * slot, :, t * tk:(t + 1) * tk].astype(BF16)) + bias[:, t * tk:(t + 1) * tk]
        s_scr[:, t * tk:(t + 1) * tk] = s
        mx = jnp.maximum(mx, col_max(s))
    s_new = _dot_nt(q2, nks_ref[0].astype(BF16)) + bias[:, past:past + LANES]
    s_new = jnp.where(new_pos <= q_pos, s_new, NEG_INF)
    m_s = row_max(jnp.maximum(mx, s_new))
    p_new = jnp.exp(s_new - m_s)
    acc = _dot(p_new.astype(BF16), nvs_ref[0].astype(BF16))
    lsum = p_new
    for t in range(past // tk):
        p = jnp.exp(s_scr[:, t * tk:(t + 1) * tk] - tiled(m_s, tk))
        lsum = lsum + col_sum(p)
        acc = acc + _dot_nt(p.astype(BF16), kv_buf[2 * slot + 1, :, t * tk:(t + 1) * tk].astype(BF16))
    o_s = acc / jnp.maximum(jnp.sum(lsum, axis=-1, keepdims=True), 1e-30)

    wlen = wt_ref.shape[3]
    w_pos = past - wlen + lax.broadcasted_iota(jnp.int32, (rows, wlen), 1)
    d = q_pos - w_pos
    s_w = _dot(q2, wt_ref[0, 0].astype(BF16))
    s_w = jnp.where((d >= 0) & (d < WINDOW) & (w_pos >= 0), s_w, NEG_INF)
    d = q_pos - new_pos
    s_nw = _dot_nt(q2, nkw_ref[0].astype(BF16))
    s_nw = jnp.where((d >= 0) & (d < WINDOW), s_nw, NEG_INF)
    m_w = row_max(jnp.maximum(jnp.maximum(col_max(s_w), s_nw), M_INIT))
    p_w = jnp.exp(s_w - tiled(m_w, wlen))
    p_nw = jnp.exp(s_nw - m_w)
    acc = _dot_nt(p_w.astype(BF16), wt_ref[0, 1].astype(BF16)) + _dot(p_nw.astype(BF16), nvw_ref[0].astype(BF16))
    o_w = acc / jnp.maximum(jnp.sum(col_sum(p_w) + p_nw, axis=-1, keepdims=True), 1e-30)

    o_ref[0] = _combine_heads(gate_ref[0], o_c, o_s, o_w, nq).astype(o_ref.dtype)


def _nsa_sample(pt_flat, q3, gates3, kc, vc, cache_t, new_rows, win_t, expand, *, past, tk):
    n_seq, nq, _ = q3.shape
    nbp = kc.shape[1]
    rows = NSA_HEADS * nq
    per_seq = lambda a: pl.BlockSpec((1,) + a.shape[1:], lambda b, pt: (b,) + (0,) * (a.ndim - 1))
    new = lambda col: pl.BlockSpec((1, LANES, KV_WIDTH), lambda b, pt: (b, 0, col))
    grid_spec = pltpu.PrefetchScalarGridSpec(
        num_scalar_prefetch=1,
        grid=(n_seq,),
        in_specs=[per_seq(q3), per_seq(gates3), per_seq(kc), per_seq(vc),
                  pl.BlockSpec(memory_space=pl.ANY),
                  new(2), new(3), per_seq(win_t), new(4), new(5),
                  pl.BlockSpec(expand.shape, lambda b, pt: (0, 0))],
        out_specs=pl.BlockSpec((1, nq, NSA_WIDTH), lambda b, pt: (b, 0, 0)),
        scratch_shapes=[pltpu.VMEM((4, KV_WIDTH, past), F32),
                        pltpu.VMEM((rows, past), F32), pltpu.SemaphoreType.DMA((2,))])
    return pl.pallas_call(
        functools.partial(_nsa_sample_kernel, nq=nq, past=past, nbp=nbp, tk=tk),
        grid_spec=grid_spec,
        out_shape=jax.ShapeDtypeStruct((n_seq, nq, NSA_WIDTH), F32),
        compiler_params=_params("arbitrary"),
        name="nsa_sample",
    )(pt_flat, q3, gates3, kc, vc, cache_t, new_rows, new_rows, win_t, new_rows, new_rows, expand)


def _expand_matrix(nb, n_keys):
    return jnp.asarray(np.arange(n_keys)[None, :] // BLK == np.arange(nb)[:, None], dtype=BF16)


def _merge_kernel(x_ref, ssm_ref, *refs, mirrored):
    nsa_refs = refs[:2] if mirrored else refs[:1]
    gpre_ref, wm_ref, wbs_ref, wbn_ref, wo_ref, gpost_ref, gx_ref, wxq_ref, x1_ref, qx_ref = refs[len(nsa_refs):]
    x = x_ref[...]
    tm = x.shape[0]
    if mirrored:
        upper = nsa_refs[1][...]
        n_chunks = tm // BLK
        upper = jnp.concatenate([upper[(n_chunks - 1 - s) * BLK:(n_chunks - s) * BLK] for s in range(n_chunks)],
                                axis=0)
        nsa = jnp.where(pl.program_id(0) < pl.num_programs(0) // 2, nsa_refs[0][...], upper)
    else:
        nsa = nsa_refs[0][...]
    n_seq = ssm_ref.shape[1] // tm
    ssm = jnp.concatenate([ssm_ref[s, pl.ds(pl.program_id(1), tm, stride=n_seq), :]
                           for s in range(SSM_WIDTH // LANES)], axis=1)
    a = _rms(x, gpre_ref[...]).astype(BF16)
    g_ssm = _sigmoid(_dot(a, wm_ref[:, 0:D_MODEL]))
    g_nsa = _sigmoid(_dot(a, wm_ref[:, D_MODEL:2 * D_MODEL]))
    merged = (g_ssm * _dot(ssm.astype(BF16), wbs_ref[...])
              + g_nsa * _dot(nsa, wbn_ref[...]))
    x1 = x + _rms(_dot(merged.astype(BF16), wo_ref[...]), gpost_ref[...])
    x1_ref[...] = x1
    c = _rms(x1, gx_ref[...]).astype(BF16)
    qx_ref[...] = (_dot(c, wxq_ref[...]) * (X_HEAD_DIM ** -0.5)).astype(BF16)


def _merge(x2d, ssm_slabs, nsa_o, weights, *, tm, n_tiles):
    n = x2d.shape[0]
    n_b = n // (n_tiles * tm)
    row = lambda w_: pl.BlockSpec((tm, w_), lambda t, b: (b * n_tiles + t, 0))
    full = lambda a: pl.BlockSpec(a.shape, lambda t, b: (0,) * a.ndim, pipeline_mode=pl.Buffered(1))
    ssm_spec = pl.BlockSpec((SSM_WIDTH // LANES, tm * n_b, LANES), lambda t, b: (0, t, 0))
    mirrored = isinstance(nsa_o, tuple)
    if mirrored:
        half = n_tiles // 2
        nsa_specs = [pl.BlockSpec((tm, NSA_WIDTH), lambda t, b: (b * half + jnp.minimum(t, half - 1), 0)),
                     pl.BlockSpec((tm, NSA_WIDTH), lambda t, b: (b * half + jnp.minimum(n_tiles - 1 - t, half - 1), 0))]
        nsa_args = list(nsa_o)
    else:
        nsa_specs, nsa_args = [row(NSA_WIDTH)], [nsa_o]
    return pl.pallas_call(
        functools.partial(_merge_kernel, mirrored=mirrored),
        grid=(n_tiles, n_b),
        in_specs=[row(D_MODEL), ssm_spec] + nsa_specs + [full(w) for w in weights],
        out_specs=[row(D_MODEL), row(X_WIDTH)],
        out_shape=[jax.ShapeDtypeStruct((n, D_MODEL), F32), jax.ShapeDtypeStruct((n, X_WIDTH), BF16)],
        compiler_params=_params("parallel", "arbitrary"),
        name="merge",
    )(x2d, ssm_slabs, *nsa_args, *weights)


def _xattn_tile(q, kv_ref, m_len):
    outs = []
    for h in range(X_HEADS):
        cols = slice(h * X_HEAD_DIM, (h + 1) * X_HEAD_DIM)
        k = kv_ref[0, pl.ds(h, m_len, stride=2 * X_HEADS), :].astype(BF16)
        v = kv_ref[0, pl.ds(X_HEADS + h, m_len, stride=2 * X_HEADS), :].astype(BF16)
        s = _dot_nt(q[:, cols], k)
        m = jnp.max(s, axis=-1, keepdims=True)
        p = jnp.exp(s - m)
        p = p / jnp.sum(p, axis=-1, keepdims=True)
        outs.append(_dot(p.astype(BF16), v))
    return jnp.concatenate(outs, axis=1)


def _xattn_kernel(q_ref, kv_ref, o_ref, *, m_len):
    t = q_ref.shape[1]
    n_rows = 2 * X_HEADS
    for s in range(q_ref.shape[0]):
        q = q_ref[s].astype(F32)
        z = jnp.zeros((t, X_HEAD_DIM), F32)
        q_bd = jnp.concatenate(
            [jnp.concatenate([q[:, h * X_HEAD_DIM:(h + 1) * X_HEAD_DIM] if j == h else z for j in range(X_HEADS)],
                             axis=1) for h in range(X_HEADS)], axis=0).astype(BF16)
        k = jnp.concatenate([kv_ref[s, pl.ds(h, m_len, stride=n_rows), :] for h in range(X_HEADS)],
                            axis=1).astype(BF16)
        v = jnp.concatenate([kv_ref[s, pl.ds(X_HEADS + h, m_len, stride=n_rows), :] for h in range(X_HEADS)],
                            axis=1).astype(BF16)
        sc = _dot_nt(q_bd, k)
        p = jnp.exp(sc - jnp.max(sc, axis=-1, keepdims=True))
        p = p / jnp.sum(p, axis=-1, keepdims=True)
        o = _dot(p.astype(BF16), v)
        o_ref[s] = jnp.concatenate([o[h * t:(h + 1) * t, h * X_HEAD_DIM:(h + 1) * X_HEAD_DIM]
                                    for h in range(X_HEADS)], axis=1).astype(o_ref.dtype)


def _xattn(q3, mem_kv_rows, *, seqs_per_step):
    n_seq, t, _ = q3.shape
    m_len = mem_kv_rows.shape[1] // (2 * X_HEADS)
    sb = seqs_per_step
    return pl.pallas_call(
        functools.partial(_xattn_kernel, m_len=m_len),
        grid=(n_seq // sb,),
        in_specs=[pl.BlockSpec((sb, t, X_WIDTH), lambda b: (b, 0, 0)),
                  pl.BlockSpec((sb, m_len * 2 * X_HEADS, X_HEAD_DIM), lambda b: (b, 0, 0))],
        out_specs=pl.BlockSpec((sb, t, X_WIDTH), lambda b: (b, 0, 0)),
        out_shape=jax.ShapeDtypeStruct((n_seq, t, X_WIDTH), q3.dtype),
        compiler_params=_params("parallel"),
        name="xattn",
    )(q3, mem_kv_rows)


def _mlp_kernel(x1_ref, o_ref, *refs, m_len):
    if m_len:
        kv_ref, refs = refs[0], refs[1:]
        o = _xattn_tile(o_ref[...], kv_ref, m_len).astype(BF16)
    else:
        o = o_ref[...]
    wxo_ref, gxp_ref, gm_ref, wup_ref, wdn_ref, gmp_ref, y_ref = refs
    x2 = x1_ref[...] + _rms(_dot(o, wxo_ref[...]), gxp_ref[...])
    m = _rms(x2, gm_ref[...]).astype(BF16)
    hid = jnp.maximum(_dot(m, wup_ref[...]), 0.0)
    hid = (hid * hid).astype(BF16)
    y_ref[...] = x2 + _rms(_dot(hid, wdn_ref[...]), gmp_ref[...])


def _mlp(x1, o, weights, *, tm, mem_kv_rows=None, tiles_per_seq=1):
    n = x1.shape[0]
    row = lambda w_: pl.BlockSpec((tm, w_), lambda i: (i, 0))
    full = lambda a: pl.BlockSpec(a.shape, lambda i: (0,) * a.ndim, pipeline_mode=pl.Buffered(1))
    m_len, kv_specs, kv_args = 0, [], []
    if mem_kv_rows is not None:
        m_len = mem_kv_rows.shape[1] // (2 * X_HEADS)
        kv_specs = [pl.BlockSpec((1,) + mem_kv_rows.shape[1:], lambda i: (i // tiles_per_seq, 0, 0))]
        kv_args = [mem_kv_rows]
    return pl.pallas_call(
        functools.partial(_mlp_kernel, m_len=m_len),
        grid=(n // tm,),
        in_specs=[row(D_MODEL), row(X_WIDTH)] + kv_specs + [full(w) for w in weights],
        out_specs=row(D_MODEL),
        out_shape=jax.ShapeDtypeStruct((n, D_MODEL), F32),
        compiler_params=_params("parallel"),
        name="mlp",
    )(x1, o, *kv_args, *weights)


def _memkv_kernel(m_ref, g_ref, w_ref, o_ref, *, tm):
    kv = _dot(_rms(m_ref[...], g_ref[...]).astype(BF16), w_ref[...])
    n_rows = 2 * X_HEADS
    for j in range(n_rows):
        o_ref[pl.ds(j, tm, stride=n_rows), :] = kv[:, j * X_HEAD_DIM:(j + 1) * X_HEAD_DIM]


def _memkv(mem2d, g, w, *, tm):
    n = mem2d.shape[0]
    n_rows = 2 * X_HEADS
    return pl.pallas_call(
        functools.partial(_memkv_kernel, tm=tm),
        grid=(n // tm,),
        in_specs=[pl.BlockSpec((tm, D_MODEL), lambda i: (i, 0)),
                  pl.BlockSpec(g.shape, lambda i: (0, 0)), pl.BlockSpec(w.shape, lambda i: (0, 0))],
        out_specs=pl.BlockSpec((tm * n_rows, X_HEAD_DIM), lambda i: (i, 0)),
        out_shape=jax.ShapeDtypeStruct((n * n_rows, X_HEAD_DIM), F32),
        compiler_params=_params("parallel"),
        name="memkv",
    )(mem2d, g, w)


def _row(v):
    return v.astype(F32).reshape(1, -1)


def kernel(x_prompt, x_sample, cache_nsa_kv, cache_win_kv, state_ssm, cache_mem_kv, page_table, mem_prompt, g_mix_pre, w_in, ssm_lam_re, ssm_lam_im, ssm_log_dt, ssm_b_re, ssm_b_im, ssm_c_re, ssm_c_im, ssm_d, w_glu, b_glu, cmp_pe_k, w_cmpk1, w_cmpk2, cmp_pe_v, w_cmpv1, w_cmpv2, w_br_ssm, w_br_nsa, w_out, g_mix_post, g_x_pre, g_mem, w_xq, w_xk, w_xv, w_xo, g_x_post, g_mlp_pre, w_up, w_down, g_mlp_post):
    depth = w_in.shape[0]
    n_seq_p, seq, _ = x_prompt.shape
    n_seq_s, nq, _ = x_sample.shape
    past = page_table.shape[1] * PAGE_SIZE
    assert depth == 1 and seq % BLK == 0 and nq <= SUBLANES and past % BLK == 0
    assert n_seq_p == SUBLANES and n_seq_s % SUBLANES == 0

    y_p = x_prompt.reshape(n_seq_p * seq, D_MODEL)
    y_s = x_sample.reshape(n_seq_s * nq, D_MODEL)
    l = 0

    w_proj = w_in[l, :, :N_PROJ].astype(BF16)
    w_gate = jnp.pad(w_in[l, :, N_PROJ:N_PROJ + N_GATE], ((0, 0), (0, LANES - N_GATE))).astype(BF16)
    w_merge = w_in[l, :, N_PROJ + N_GATE:].astype(BF16)
    lam_l, bm, cm = _ssm_params(ssm_lam_re[l], ssm_lam_im[l], ssm_log_dt[l], ssm_b_re[l], ssm_b_im[l],
                                ssm_c_re[l], ssm_c_im[l])
    ssm_w = (lam_l, bm, cm, _row(ssm_d[l]), w_glu[l].astype(BF16), _row(b_glu[l]))
    cmp_w = _compress_params(cmp_pe_k[l], w_cmpk1[l], w_cmpk2[l], cmp_pe_v[l], w_cmpv1[l], w_cmpv2[l])
    merge_w = (_row(g_mix_pre[l]), w_merge, w_br_ssm[l].astype(BF16), w_br_nsa[l].astype(BF16),
               w_out[l].astype(BF16), _row(g_mix_post[l]), _row(g_x_pre[l]), w_xq[l].astype(BF16))
    mlp_w = (w_xo[l].astype(BF16), _row(g_x_post[l]), _row(g_mlp_pre[l]), w_up[l].astype(BF16),
             w_down[l].astype(BF16), _row(g_mlp_post[l]))
    w_mem = jnp.concatenate([w_xk[l], w_xv[l]], axis=1).astype(BF16)

    tm_p = 512 if seq % 512 == 0 else seq
    nt_p = seq // tm_p
    tabs_p = _rope_tables(np.arange(seq))
    u_p, q_p, kvt_p, wint_p, kvb_p, gate_p = _proj(
        y_p, _row(g_mix_pre[l]), w_proj, w_gate, tabs_p, tm=tm_p, n_tab_blocks=nt_p, prompt=True)
    h0_p = jnp.zeros((1, SUBLANES, 2 * N_STATE), F32)
    tc_p = 64 if seq % 64 == 0 else seq
    ssm_p, hl_p = _ssm(u_p, h0_p, *ssm_w, n_groups=1, n_time=seq, tc=tc_p)

    assert seq % PAGE_SIZE == 0
    n_pages_p = n_seq_p * seq // PAGE_SIZE
    pe_t, w1_t = _compress_paged_params(cmp_pe_k[l], w_cmpk1[l], cmp_pe_v[l], w_cmpv1[l])
    cmp_p = _compress_paged(jnp.zeros((n_pages_p,), jnp.int32), kvt_p, pe_t, w1_t, cmp_w[2],
                            m=min(n_pages_p, 128), pages_per_seq=seq // PAGE_SIZE)
    nsa_p = _nsa_prompt(q_p, gate_p, cmp_p, kvb_p, n_seq=n_seq_p, seq=seq)

    mem_kv_p = _memkv(mem_prompt.reshape(-1, D_MODEL), _row(g_mem[l]), w_mem, tm=256)
    m_len = mem_prompt.shape[1]
    mem_kv_p3 = mem_kv_p.reshape(n_seq_p, m_len * 2 * X_HEADS, X_HEAD_DIM)

    if nt_p % 2:
        nsa_p = _unmirror(*nsa_p, n_seq_p)
    x1_p, qx_p = _merge(y_p, ssm_p, nsa_p, merge_w, tm=tm_p, n_tiles=nt_p)
    y_p = _mlp(x1_p, qx_p, mlp_w, tm=tm_p, mem_kv_rows=mem_kv_p3, tiles_per_seq=nt_p)

    n_s = n_seq_s * nq
    tabs_s = tuple(np.tile(t, (n_seq_s, 1)) for t in _rope_tables(past + np.arange(nq)))
    u_s, q_s, kv_s, win_s, kvb_s, gate_s = _proj(y_s, _row(g_mix_pre[l]), w_proj, w_gate, tabs_s,
                                                 tm=n_s, n_tab_blocks=1, prompt=False)
    n_grp = n_seq_s // SUBLANES
    n_slabs = SSM_WIDTH // LANES
    u_s = (u_s.reshape(n_grp, SUBLANES, nq, n_slabs, LANES).transpose(3, 0, 2, 1, 4)
           .reshape(n_slabs, n_s, LANES))
    st = state_ssm[l].astype(F32).reshape(n_seq_s, N_STATE, 2)
    h0_s = _state_lanes(st[..., 0], st[..., 1]).reshape(n_grp, SUBLANES, 2 * N_STATE)
    ssm_s, hl_s = _ssm(u_s, h0_s, *ssm_w, n_groups=n_grp, n_time=nq, tc=nq)
    ssm_s = (ssm_s.reshape(n_slabs, n_grp, nq, SUBLANES, LANES).transpose(0, 1, 3, 2, 4)
             .reshape(n_slabs, n_s, LANES))

    n_pages = page_table.shape[1]
    n_pool = cache_nsa_kv.shape[1]
    cache_t = jnp.transpose(cache_nsa_kv[l], (0, 2, 3, 4, 1)).reshape(n_pool, 4, KV_WIDTH, PAGE_SIZE)
    win_t = jnp.transpose(cache_win_kv[l], (0, 2, 3, 4, 1)).reshape(n_seq_s, 2, KV_WIDTH, -1)
    pt_flat = page_table.reshape(-1).astype(jnp.int32)
    nb_past = past // BLK
    cmp_pages = _compress_paged(pt_flat, cache_t, pe_t, w1_t, cmp_w[2], m=min(n_seq_s * n_pages, 128))
    cmp_past = cmp_pages
    new_rows = jnp.pad(jnp.concatenate([kv_s, win_s], axis=1).reshape(n_seq_s, nq, 6 * KV_WIDTH),
                       ((0, 0), (0, LANES - nq), (0, 0)))
    cmp_new = _compress(new_rows[:, :BLK].reshape(n_seq_s * BLK, 6 * KV_WIDTH), *cmp_w,
                        n_blocks=n_seq_s, tm=n_seq_s)
    nbp = -(-(nb_past + 1) // LANES) * LANES
    cmp_s = jnp.concatenate([cmp_past.reshape(2, n_seq_s, nb_past, KV_WIDTH), cmp_new[:, :, None, :],
                             jnp.zeros((2, n_seq_s, nbp - nb_past - 1, KV_WIDTH), F32)], axis=2)
    nsa_s = _nsa_sample(pt_flat, q_s.astype(F32).reshape(n_seq_s, nq, NSA_WIDTH),
                        gate_s.reshape(n_seq_s, nq, LANES), cmp_s[0], cmp_s[1], cache_t, new_rows, win_t,
                        _expand_matrix(nbp, past + LANES), past=past, tk=min(past, 1024))

    x1_s, qx_s = _merge(y_s, ssm_s, nsa_s.reshape(n_s, NSA_WIDTH).astype(BF16), merge_w, tm=n_s, n_tiles=1)
    mem_kv_s3 = cache_mem_kv[l].reshape(n_seq_s, m_len * 2 * X_HEADS, X_HEAD_DIM)
    o_s = _xattn(qx_s.astype(F32).reshape(n_seq_s, nq, X_WIDTH), mem_kv_s3, seqs_per_step=SUBLANES)
    y_s = _mlp(x1_s, o_s.reshape(-1, X_WIDTH).astype(BF16), mlp_w, tm=n_s)

    def ssm_state(hl, n_seq):
        re, im = _state_unlanes(hl.reshape(n_seq, 2 * N_STATE))
        return jnp.stack([re, im], axis=-1).reshape(1, n_seq, SSM_GROUPS, SSM_STATE, 2)

    def token_major(xt):
        n_seq, parts, _, t_len = xt.shape
        return xt.reshape(1, n_seq, parts, NSA_KV, HEAD_DIM, t_len).transpose(0, 1, 5, 2, 3, 4)

    w_keep = min(WINDOW, seq)
    win_new = win_s.reshape(n_seq_s, nq, 2, NSA_KV, HEAD_DIM).astype(cache_win_kv.dtype)
    win_sample = jnp.concatenate([cache_win_kv[l], win_new], axis=1)[:, nq:]
    return (y_p.reshape(n_seq_p, seq, D_MODEL),
            y_s.reshape(n_seq_s, nq, D_MODEL),
            token_major(kvt_p),
            kv_s.reshape(1, n_seq_s, nq, 4, NSA_KV, HEAD_DIM),
            token_major(wint_p[:, :, :, seq - w_keep:]),
            win_sample[None],
            ssm_state(hl_p, n_seq_p),
            ssm_state(hl_s, n_seq_s),
            mem_kv_p.reshape(1, n_seq_p, m_len, 2, X_HEADS, X_HEAD_DIM))
```

```python
import functools
import math

import jax
import jax.numpy as jnp
import numpy as np
from jax import lax
from jax.experimental import pallas as pl
from jax.experimental.pallas import tpu as pltpu

F32 = jnp.float32
BF16 = jnp.bfloat16

D_MODEL = 1024
SSM_WIDTH = 512
SSM_GROUP = 16
SSM_GROUPS = 32
SSM_STATE = 64
N_STATE = SSM_GROUPS * SSM_STATE
STATE_CHUNK = 512
N_CHUNKS = N_STATE // STATE_CHUNK
NSA_HEADS = 8
HEAD_DIM = 64
NSA_WIDTH = NSA_HEADS * HEAD_DIM
NSA_KV = 2
NSA_REP = NSA_HEADS // NSA_KV
KV_WIDTH = NSA_KV * HEAD_DIM
BLK = 64
N_SEL = 16
WINDOW = 512
ROT_DIM = 16
ROPE_THETA = 500000.0
PAGE_SIZE = 128
X_HEADS = 4
X_HEAD_DIM = 128
X_WIDTH = X_HEADS * X_HEAD_DIM
D_FF = 4 * D_MODEL
EPS = 1e-6
NEG_INF = -1e30
M_INIT = -1e29
FORCE_SCORE = 1e4
LANES = 128
SUBLANES = 8
VMEM_LIMIT = 56 * 1024 * 1024

N_PROJ = SSM_WIDTH + NSA_WIDTH + 6 * KV_WIDTH
N_GATE = 3 * NSA_HEADS
SEL_CHUNK = 512
PAGE_PITCH = PAGE_SIZE + SUBLANES


def _params(*sem):
    return pltpu.CompilerParams(dimension_semantics=sem, vmem_limit_bytes=VMEM_LIMIT)


def _rms(x, g):
    return x * lax.rsqrt(jnp.mean(x * x, axis=-1, keepdims=True) + EPS) * g


def _gelu(x):
    return 0.5 * x * (1.0 + jnp.tanh(math.sqrt(2.0 / math.pi) * (x + 0.044715 * (x * x * x))))


def _sigmoid(x):
    return 1.0 / (1.0 + jnp.exp(-x))


def _dot(a, b):
    return jnp.dot(a, b, preferred_element_type=F32)


def _dot_nt(a, b):
    return lax.dot_general(a, b, (((1,), (1,)), ((), ())), preferred_element_type=F32)


def _lockstep(*stagewise):
    results = [None] * len(stagewise)
    live = list(range(len(stagewise)))
    while live:
        for k in list(live):
            try:
                next(stagewise[k])
            except StopIteration as done:
                results[k] = done.value
                live.remove(k)
    return results


def _proj_kernel(x_ref, g_ref, w_ref, wg_ref, cos_ref, sp_ref, sm_ref,
                 u_ref, q_ref, kv_ref, win_ref, kvb_ref, gate_ref, *, token_minor):
    a = _rms(x_ref[...], g_ref[...]).astype(BF16)
    cos, sp, sm = cos_ref[...], sp_ref[...], sm_ref[...]

    def rope(blk):
        return blk * cos + pltpu.roll(blk, 8, 1) * sp + pltpu.roll(blk, LANES - 8, 1) * sm

    pairs = {}

    def lane_block(j):
        if j // 2 not in pairs:
            pairs[j // 2] = _dot(a, w_ref[:, (j // 2) * 2 * LANES:(j // 2 + 1) * 2 * LANES])
        return pairs[j // 2][:, (j % 2) * LANES:(j % 2 + 1) * LANES]

    if token_minor:
        tm = x_ref.shape[0]
        n_seq = u_ref.shape[1] // tm
        for s in range(SSM_WIDTH // LANES):
            u_ref[s, pl.ds(pl.program_id(1), tm, stride=n_seq), :] = lane_block(s)
    else:
        u_ref[...] = _dot(a, w_ref[:, 0:SSM_WIDTH])
    for j in range(NSA_WIDTH // LANES):
        q_ref[:, j * LANES:(j + 1) * LANES] = rope(lane_block(SSM_WIDTH // LANES + j)).astype(BF16)
    for j in range(6):
        blk = lane_block((SSM_WIDTH + NSA_WIDTH) // LANES + j)
        if j % 2 == 0:
            blk = rope(blk)
        out_ref, part = (kv_ref, j) if j < 4 else (win_ref, j - 4)
        if token_minor:
            out_ref[0, part] = blk.T
        else:
            out_ref[:, part * LANES:(part + 1) * LANES] = blk
        kvb_ref[:, j * LANES:(j + 1) * LANES] = blk.astype(BF16)
    gate_ref[...] = _sigmoid(_dot(a, wg_ref[...]))


def _proj(x2d, g, w, wg, tabs, *, tm, n_tab_blocks, prompt):
    n = x2d.shape[0]
    nt = n_tab_blocks
    n_b = n // (nt * tm)
    row = lambda w_: pl.BlockSpec((tm, w_), lambda t, b: (b * nt + t, 0))
    full = lambda a: pl.BlockSpec(a.shape, lambda t, b: (0,) * a.ndim)
    tab = pl.BlockSpec((tm, LANES), lambda t, b: (t, 0))
    if prompt:
        t_len = nt * tm
        n_slabs = SSM_WIDTH // LANES
        u_shape = jax.ShapeDtypeStruct((n_slabs, t_len * n_b, LANES), F32)
        u_spec = pl.BlockSpec((n_slabs, tm * n_b, LANES), lambda t, b: (0, t, 0))
        tok_minor = lambda parts: pl.BlockSpec((1, parts, KV_WIDTH, tm), lambda t, b: (b, 0, 0, t))
        kv_specs = [tok_minor(4), tok_minor(2)]
        kv_shapes = [jax.ShapeDtypeStruct((n_b, 4, KV_WIDTH, t_len), F32),
                     jax.ShapeDtypeStruct((n_b, 2, KV_WIDTH, t_len), F32)]
    else:
        u_shape = jax.ShapeDtypeStruct((n, SSM_WIDTH), F32)
        u_spec = row(SSM_WIDTH)
        kv_specs = [row(4 * KV_WIDTH), row(2 * KV_WIDTH)]
        kv_shapes = [jax.ShapeDtypeStruct((n, 4 * KV_WIDTH), F32), jax.ShapeDtypeStruct((n, 2 * KV_WIDTH), F32)]
    return pl.pallas_call(
        functools.partial(_proj_kernel, token_minor=prompt),
        grid=(nt, n_b),
        in_specs=[row(D_MODEL), full(g), full(w), full(wg), tab, tab, tab],
        out_specs=[u_spec, row(NSA_WIDTH)] + kv_specs + [row(6 * KV_WIDTH), row(LANES)],
        out_shape=[u_shape, jax.ShapeDtypeStruct((n, NSA_WIDTH), BF16)] + kv_shapes
                  + [jax.ShapeDtypeStruct((n, 6 * KV_WIDTH), BF16), jax.ShapeDtypeStruct((n, LANES), F32)],
        compiler_params=_params("parallel", "arbitrary"),
        name="proj",
    )(x2d, g, w, wg, *tabs)


def _rope_tables(pos):
    half = ROT_DIM // 2
    freqs = ROPE_THETA ** (-np.arange(half, dtype=np.float64) / half)
    ang = np.asarray(pos, np.float64)[:, None] * freqs[None, :]
    cos, sin = np.cos(ang), np.sin(ang)
    r = ang.shape[0]
    z8 = np.zeros((r, half))
    rest0 = np.zeros((r, HEAD_DIM - ROT_DIM))
    rest1 = np.ones((r, HEAD_DIM - ROT_DIM))
    c64 = np.concatenate([cos, cos, rest1], axis=1)
    sp64 = np.concatenate([z8, sin, rest0], axis=1)
    sm64 = np.concatenate([-sin, z8, rest0], axis=1)
    return tuple(np.tile(t, (1, LANES // HEAD_DIM)).astype(np.float32) for t in (c64, sp64, sm64))


def _ssm_kernel(u_ref, h0_ref, lam_ref, bm_ref, cm_ref, d_ref, wglu_ref, bglu_ref,
                y_ref, hlast_ref, hs_ref, hstate_ref, *, tc):
    j = pl.program_id(1)

    @pl.when(j == 0)
    def _():
        hstate_ref[...] = h0_ref[0]

    n_slabs = SSM_WIDTH // LANES
    u = jnp.concatenate([u_ref[s] for s in range(n_slabs)], axis=1)
    ub = u.astype(BF16)
    half_in = SSM_WIDTH // 2
    chunk_lanes = 2 * STATE_CHUNK
    chunks_per_half = N_CHUNKS // 2
    ys = [None, None]
    for c in range(N_CHUNKS):
        h = c // chunks_per_half
        re0 = c * chunk_lanes
        im0 = re0 + STATE_CHUNK
        u_half = ub[:, h * half_in:(h + 1) * half_in]
        hs_ref[:, re0:re0 + STATE_CHUNK] = _dot(u_half, bm_ref[c, 0])
        hs_ref[:, im0:im0 + STATE_CHUNK] = _dot(u_half, bm_ref[c, 1])
        lr = jnp.broadcast_to(lam_ref[0:1, re0:re0 + STATE_CHUNK], (SUBLANES, STATE_CHUNK))
        li = jnp.broadcast_to(lam_ref[0:1, im0:im0 + STATE_CHUNK], (SUBLANES, STATE_CHUNK))
        hr = hstate_ref[:, re0:re0 + STATE_CHUNK]
        hi = hstate_ref[:, im0:im0 + STATE_CHUNK]
        for t in range(tc):
            r0 = t * SUBLANES
            hr, hi = (lr * hr - li * hi + hs_ref[r0:r0 + SUBLANES, re0:re0 + STATE_CHUNK],
                      lr * hi + li * hr + hs_ref[r0:r0 + SUBLANES, im0:im0 + STATE_CHUNK])
            hs_ref[r0:r0 + SUBLANES, re0:re0 + STATE_CHUNK] = hr
            hs_ref[r0:r0 + SUBLANES, im0:im0 + STATE_CHUNK] = hi
        hstate_ref[:, re0:re0 + STATE_CHUNK] = hr
        hstate_ref[:, im0:im0 + STATE_CHUNK] = hi
        part = (_dot(hs_ref[:, re0:re0 + STATE_CHUNK].astype(BF16), cm_ref[c, 0])
                + _dot(hs_ref[:, im0:im0 + STATE_CHUNK].astype(BF16), cm_ref[c, 1]))
        ys[h] = part if ys[h] is None else ys[h] + part

    y = jnp.concatenate(ys, axis=1) + d_ref[...] * u
    y = _gelu(y)
    z = _dot(y.astype(BF16), wglu_ref[...]) + bglu_ref[...]
    out = y * _sigmoid(z)
    for s in range(n_slabs):
        y_ref[s] = out[:, s * LANES:(s + 1) * LANES]

    @pl.when(j == pl.num_programs(1) - 1)
    def _():
        hlast_ref[0] = hstate_ref[...]


def _ssm(u_tb, h0, lam, bm, cm, d, wglu, bglu, *, n_groups, n_time, tc):
    rows = tc * SUBLANES
    nt = n_time // tc
    n_slabs = SSM_WIDTH // LANES
    full = lambda a: pl.BlockSpec(a.shape, lambda g, j: (0,) * a.ndim)
    st = pl.BlockSpec((1, SUBLANES, 2 * N_STATE), lambda g, j: (g, 0, 0))
    slabs = pl.BlockSpec((n_slabs, rows, LANES), lambda g, j: (0, g * nt + j, 0))
    return pl.pallas_call(
        functools.partial(_ssm_kernel, tc=tc),
        grid=(n_groups, nt),
        in_specs=[slabs, st, full(lam), full(bm), full(cm), full(d), full(wglu), full(bglu)],
        out_specs=[slabs, st],
        out_shape=[jax.ShapeDtypeStruct((n_slabs, n_groups * n_time * SUBLANES, LANES), F32),
                   jax.ShapeDtypeStruct((n_groups, SUBLANES, 2 * N_STATE), F32)],
        scratch_shapes=[pltpu.VMEM((rows, 2 * N_STATE), F32), pltpu.VMEM((SUBLANES, 2 * N_STATE), F32)],
        compiler_params=_params("parallel", "arbitrary"),
        name="ssm",
    )(u_tb, h0, lam, bm, cm, d, wglu, bglu)


def _state_lanes(re, im):
    lead = re.shape[:-1]
    r = re.reshape(lead + (N_CHUNKS, 1, STATE_CHUNK))
    i = im.reshape(lead + (N_CHUNKS, 1, STATE_CHUNK))
    return jnp.concatenate([r, i], axis=-2).reshape(lead + (2 * N_STATE,))


def _state_unlanes(x):
    lead = x.shape[:-1]
    y = x.reshape(lead + (N_CHUNKS, 2, STATE_CHUNK))
    return y[..., 0, :].reshape(lead + (N_STATE,)), y[..., 1, :].reshape(lead + (N_STATE,))


def _ssm_params(lam_re, lam_im, log_dt, b_re, b_im, c_re, c_im):
    lr, li = lam_re.astype(F32), lam_im.astype(F32)
    dt = jnp.exp(log_dt.astype(F32))[:, None]
    mag = jnp.exp(lr * dt)
    bar_re, bar_im = mag * jnp.cos(li * dt), mag * jnp.sin(li * dt)
    den = lr * lr + li * li
    f_re = ((bar_re - 1.0) * lr + bar_im * li) / den
    f_im = (bar_im * lr - (bar_re - 1.0) * li) / den
    b_re, b_im = b_re.astype(F32), b_im.astype(F32)
    bb_re = f_re[..., None] * b_re - f_im[..., None] * b_im
    bb_im = f_re[..., None] * b_im + f_im[..., None] * b_re
    g_chunk = STATE_CHUNK // SSM_STATE
    g_half = SSM_GROUPS // 2

    def same_group(c):
        h = c // (N_CHUNKS // 2)
        return (h * g_half + np.arange(g_half)[:, None] == c * g_chunk + np.arange(g_chunk)[None, :]
                ).astype(np.float32)

    def in_blockdiag(x, c):
        t = x[c * g_chunk:(c + 1) * g_chunk].transpose(2, 0, 1)[None]
        return (t * same_group(c)[:, None, :, None]).reshape(SSM_WIDTH // 2, STATE_CHUNK)

    def out_blockdiag(x, c):
        t = x[c * g_chunk:(c + 1) * g_chunk].transpose(0, 2, 1)[:, :, None, :]
        return (t * same_group(c).T[:, None, :, None]).reshape(STATE_CHUNK, SSM_WIDTH // 2)

    bm = jnp.stack([jnp.stack([in_blockdiag(bb_re, c), in_blockdiag(bb_im, c)])
                    for c in range(N_CHUNKS)]).astype(BF16)
    cm = jnp.stack([jnp.stack([out_blockdiag(c_re.astype(F32), c), out_blockdiag(-c_im.astype(F32), c)])
                    for c in range(N_CHUNKS)]).astype(BF16)
    lam_l = _state_lanes(bar_re.reshape(1, N_STATE), bar_im.reshape(1, N_STATE))
    return lam_l, bm, cm


def _compress_kernel(x_ref, pe_ref, w1_ref, w2_ref, o_ref, *, tm):
    acc = jnp.zeros((tm, KV_WIDTH), F32)
    for sp in range(BLK // 2):
        s0 = 2 * sp
        xa = x_ref[pl.ds(s0, tm, stride=BLK), :] + pe_ref[0, s0:s0 + 1, :]
        xb = x_ref[pl.ds(s0 + 1, tm, stride=BLK), :] + pe_ref[0, s0 + 1:s0 + 2, :]
        acc = acc + _dot(jnp.concatenate([xa, xb], axis=1).astype(BF16), w1_ref[0, sp])
    o_ref[0] = _dot(_gelu(acc).astype(BF16), w2_ref[0])


def _compress(x2d, pe, w1, w2, *, n_blocks, tm):
    return pl.pallas_call(
        functools.partial(_compress_kernel, tm=tm),
        grid=(2, n_blocks // tm),
        in_specs=[pl.BlockSpec((tm * BLK, KV_WIDTH), lambda c, i: (i, c)),
                  pl.BlockSpec((1, BLK, KV_WIDTH), lambda c, i: (c, 0, 0)),
                  pl.BlockSpec((1, BLK // 2, 2 * KV_WIDTH, KV_WIDTH), lambda c, i: (c, 0, 0, 0)),
                  pl.BlockSpec((1, KV_WIDTH, KV_WIDTH), lambda c, i: (c, 0, 0))],
        out_specs=pl.BlockSpec((1, tm, KV_WIDTH), lambda c, i: (c, i, 0)),
        out_shape=jax.ShapeDtypeStruct((2, n_blocks, KV_WIDTH), F32),
        compiler_params=_params("parallel", "parallel"),
        name="compress",
    )(x2d, pe, w1, w2)


def _compress_params(pe_k, w1_k, w2_k, pe_v, w1_v, w2_v):
    def bd(w):
        z = jnp.zeros_like(w)
        return jnp.concatenate([jnp.concatenate([w, z], axis=-1), jnp.concatenate([z, w], axis=-1)], axis=-2)

    def one(pe, w1, w2):
        w1s = bd(w1.astype(F32).reshape(BLK, HEAD_DIM, HEAD_DIM))
        return (jnp.tile(pe.astype(F32), (1, NSA_KV)),
                w1s.reshape(BLK // 2, 2 * KV_WIDTH, KV_WIDTH).astype(BF16),
                bd(w2.astype(F32)).astype(BF16))

    k, v = one(pe_k, w1_k, w2_k), one(pe_v, w1_v, w2_v)
    return tuple(jnp.stack([a, b]) for a, b in zip(k, v))


def _compress_paged_kernel(pt_ref, cache_ref, pe_ref, w1_ref, w2_ref, o_ref, buf_ref, sems, *, m,
                           pages_per_seq):
    step = pl.program_id(0)

    def page_copy(s, j):
        slot = s % 2
        row0 = pl.multiple_of(j * PAGE_PITCH, SUBLANES)
        idx = s * m + j
        if pages_per_seq:
            tok0 = pl.multiple_of((idx % pages_per_seq) * PAGE_SIZE, PAGE_SIZE)
            src = cache_ref.at[idx // pages_per_seq, pl.ds(0, 2), :, pl.ds(tok0, PAGE_SIZE)]
        else:
            src = cache_ref.at[pt_ref[idx], pl.ds(0, 2)]
        return pltpu.make_async_copy(src, buf_ref.at[pl.ds(2 * slot, 2), pl.ds(row0, PAGE_SIZE), :],
                                     sems.at[slot])

    def start_all(s):
        def start(j, _):
            page_copy(s, j).start()
            return 0
        lax.fori_loop(0, m, start, 0, unroll=8)

    @pl.when(step == 0)
    def _():
        start_all(step)

    @pl.when(step + 1 < pl.num_programs(0))
    def _():
        start_all(step + 1)

    def wait(j, _):
        page_copy(step, j).wait()
        return 0

    lax.fori_loop(0, m, wait, 0, unroll=8)

    for c in range(2):
        tile = 2 * (step % 2) + c
        res = []
        for kv in range(NSA_KV):
            acc = jnp.zeros((m, PAGE_SIZE), F32)
            for dp in range(HEAD_DIM // 2):
                d0 = 2 * dp
                r0 = kv * HEAD_DIM + d0
                xa = buf_ref[tile, pl.ds(r0, m, stride=PAGE_PITCH), :] + pe_ref[c, d0:d0 + 1, :]
                xb = buf_ref[tile, pl.ds(r0 + 1, m, stride=PAGE_PITCH), :] + pe_ref[c, d0 + 1:d0 + 2, :]
                acc = acc + _dot(jnp.concatenate([xa, xb], axis=1).astype(BF16), w1_ref[c, dp])
            res.append(_dot(_gelu(acc).astype(BF16), w2_ref[c]))
        for blk in range(PAGE_SIZE // BLK):
            cols = slice(blk * HEAD_DIM, (blk + 1) * HEAD_DIM)
            o_ref[c, pl.ds(blk, m, stride=PAGE_SIZE // BLK), :] = jnp.concatenate(
                [r[:, cols] for r in res], axis=1)


def _compress_paged(pt_flat, cache_t, pe_t, w1_t, w2, *, m, pages_per_seq=0):
    n = pt_flat.shape[0]
    full = lambda a: pl.BlockSpec(a.shape, lambda i, pt: (0,) * a.ndim)
    grid_spec = pltpu.PrefetchScalarGridSpec(
        num_scalar_prefetch=1,
        grid=(n // m,),
        in_specs=[pl.BlockSpec(memory_space=pl.ANY), full(pe_t), full(w1_t), full(w2)],
        out_specs=pl.BlockSpec((2, m * (PAGE_SIZE // BLK), KV_WIDTH), lambda i, pt: (0, i, 0)),
        scratch_shapes=[pltpu.VMEM((4, m * PAGE_PITCH, PAGE_SIZE), F32), pltpu.SemaphoreType.DMA((2,))])
    return pl.pallas_call(
        functools.partial(_compress_paged_kernel, m=m, pages_per_seq=pages_per_seq),
        grid_spec=grid_spec,
        out_shape=jax.ShapeDtypeStruct((2, n * (PAGE_SIZE // BLK), KV_WIDTH), F32),
        compiler_params=_params("arbitrary"),
        name="compress_paged",
    )(pt_flat, cache_t, pe_t, w1_t, w2)


def _compress_paged_params(pe_k, w1_k, pe_v, w1_v):
    def bd(w):
        z = jnp.zeros_like(w)
        return jnp.concatenate([jnp.concatenate([w, z], axis=-1), jnp.concatenate([z, w], axis=-1)], axis=-2)

    def one(pe, w1):
        w1d = bd(w1.astype(F32).reshape(BLK, HEAD_DIM, HEAD_DIM).transpose(1, 0, 2))
        return (jnp.tile(pe.astype(F32).T, (1, PAGE_SIZE // BLK)),
                w1d.reshape(HEAD_DIM // 2, 2 * PAGE_SIZE, PAGE_SIZE).astype(BF16))

    k, v = one(pe_k, w1_k), one(pe_v, w1_v)
    return tuple(jnp.stack([a, b]) for a, b in zip(k, v))


def _stack_queries(q, nq):
    q = q.astype(F32)
    z = jnp.zeros((nq, HEAD_DIM), F32)
    rows = []
    for h in range(NSA_HEADS):
        blk = q[:, h * HEAD_DIM:(h + 1) * HEAD_DIM]
        rows.append(jnp.concatenate([blk, z] if h < NSA_REP else [z, blk], axis=1))
    return (jnp.concatenate(rows, axis=0) * (HEAD_DIM ** -0.5)).astype(BF16)


def _masked_softmax(s, valid):
    s = jnp.where(valid, s, NEG_INF)
    m = jnp.max(s, axis=-1, keepdims=True)
    p = jnp.exp(s - m) * valid.astype(F32)
    return p / jnp.maximum(jnp.sum(p, axis=-1, keepdims=True), 1e-30)


def _select_blocks(imp, n_ids, cur, nb):
    forced = (n_ids == 0) | (n_ids == cur) | (n_ids == cur - 1)
    imp = jnp.where(forced, FORCE_SCORE, imp)
    imp = jnp.where(n_ids <= cur, imp, -FORCE_SCORE)
    rank = jnp.zeros(imp.shape, F32)
    for m in range(nb):
        col = imp[:, m:m + 1]
        beats = (col > imp) | ((col == imp) & (n_ids > m))
        rank = rank + beats.astype(F32)
    return (rank < float(N_SEL)).astype(F32)


def _select_blocks_t(imp_t, cur, nb):
    n_t = lax.broadcasted_iota(jnp.int32, imp_t.shape, 0)
    forced = (n_t == 0) | (n_t == cur) | (n_t == cur - 1)
    imp_t = jnp.where(forced, FORCE_SCORE, imp_t)
    imp_t = jnp.where(n_t <= cur, imp_t, -FORCE_SCORE)
    rank = jnp.zeros(imp_t.shape, F32)
    for m in range(nb):
        row = imp_t[m:m + 1, :]
        beats = (row > imp_t) | ((row == imp_t) & (n_t > m))
        rank = rank + beats.astype(F32)
    return ((rank < float(N_SEL)) & (n_t <= cur)).astype(F32)


def _combine_start(gates, o_c, o_w, nq):
    parts = []
    for h in range(NSA_HEADS):
        rows = slice(h * nq, (h + 1) * nq)
        parts.append((gates[:, 3 * h:3 * h + 1] * o_c[rows] + gates[:, 3 * h + 2:3 * h + 3] * o_w[rows],
                      jnp.broadcast_to(gates[:, 3 * h + 1:3 * h + 2], (nq, KV_WIDTH))))
    return parts


def _combine_finish(parts, o_s, nq):
    outs = []
    for h, (rest, g_s) in enumerate(parts):
        o = rest + g_s * o_s[h * nq:(h + 1) * nq]
        g = h // NSA_REP
        outs.append(o[:, g * HEAD_DIM:(g + 1) * HEAD_DIM])
    return jnp.concatenate(outs, axis=1)


def _combine_heads(gates, o_c, o_s, o_w, nq):
    return _combine_finish(_combine_start(gates, o_c, o_w, nq), o_s, nq)


def _nsa_pair_kernel(qa_ref, qb_ref, ga_ref, gb_ref, kc_ref, vc_ref, ks_ref, vs_ref, kw_ref, vw_ref,
                     et_ref, tri_ref, wm_ref, oa_ref, ob_ref, q2_ref, etd_ref, s_ref, *, nb, ch, wk):
    i = pl.program_id(1)
    nq = BLK
    rows = NSA_HEADS * nq
    bpc = ch // BLK
    n_units = nb // bpc + 1
    chunk_of = (i, nb - 1 - i)
    n_a = i // bpc + 1
    row_q = lax.broadcasted_iota(jnp.int32, (rows, LANES), 0) % nq
    rq_minus_lane = row_q - lax.broadcasted_iota(jnp.int32, (rows, LANES), 1)

    def col_blocks(s):
        return [s[:, j * LANES:(j + 1) * LANES] for j in range(s.shape[1] // LANES)]

    def col_max(cols):
        m = cols[0]
        for c_ in cols[1:]:
            m = jnp.maximum(m, c_)
        return m

    def finish(acc):
        return acc[:, 0:KV_WIDTH] / jnp.maximum(acc[:, KV_WIDTH:2 * KV_WIDTH], 1e-30)

    def with_ones(v):
        return jnp.concatenate([v, jnp.ones((v.shape[0], LANES), BF16)], axis=1)

    q_onehot = jnp.where(rq_minus_lane == -BLK, NEG_INF, 0.0)

    def prepare(idx, q_ref, g_ref):
        ci = chunk_of[idx]
        q2 = _stack_queries(q_ref[...], nq)
        first = jnp.maximum(ci - WINDOW // BLK, 0) // (LANES // BLK)
        w0 = pl.multiple_of(first * LANES, LANES)
        delta = ci - first * (LANES // BLK)
        lhs = jnp.concatenate([q2, q_onehot.astype(BF16)], axis=1)
        rhs = jnp.concatenate([kw_ref[pl.ds(w0, wk), :], wm_ref[delta]], axis=1)
        cols = col_blocks(_dot_nt(lhs, rhs))
        yield
        q_pos = ci * BLK + lax.broadcasted_iota(jnp.int32, (rows, 1), 0) % nq
        n_ids = lax.broadcasted_iota(jnp.int32, (rows, nb), 1)
        s_c = _dot_nt(q2, kc_ref[0].astype(BF16))
        yield
        m_w = jnp.maximum(jnp.max(col_max(cols), axis=-1, keepdims=True), M_INIT)
        m_w = jnp.broadcast_to(m_w, (rows, LANES))
        p = jnp.concatenate([jnp.exp(c_ - m_w).astype(BF16) for c_ in cols], axis=1)
        yield
        p_c = _masked_softmax(s_c, (n_ids + 1) * BLK - 1 <= q_pos)
        yield
        o_w = finish(_dot(p, with_ones(vw_ref[pl.ds(w0, wk), :])))
        o_c = _dot(p_c.astype(BF16), vc_ref[0].astype(BF16))
        yield
        gated = _combine_start(g_ref[...], o_c, o_w, nq)
        yield

        imps = []
        for g in range(NSA_KV):
            imp = p_c[g * NSA_REP * nq:(g * NSA_REP + 1) * nq]
            for r in range(1, NSA_REP):
                imp = imp + p_c[(g * NSA_REP + r) * nq:(g * NSA_REP + r + 1) * nq]
            imps.append(imp)
        imp2 = jnp.concatenate([jnp.concatenate(imps, axis=0), jnp.zeros((LANES, LANES - nb), F32)], axis=1)
        imp_t = imp2.T[0:nb]
        yield
        sel_t = _select_blocks_t(imp_t, ci, nb)
        yield
        sel2 = jnp.concatenate([sel_t, jnp.ones((LANES - nb, LANES), F32)], axis=0).T
        neg = (sel2 - 1.0) * (-NEG_INF)
        neg_rows = jnp.concatenate([neg[g * nq:(g + 1) * nq] for g in range(NSA_KV) for _ in range(NSA_REP)],
                                   axis=0)
        q2_ref[idx] = jnp.concatenate([q2, (neg_rows + q_onehot).astype(BF16)], axis=1)
        in_diag = lax.broadcasted_iota(jnp.int32, (ch, LANES), 0) // BLK == ci % bpc
        etd_ref[idx] = et_ref[ci // bpc] + jnp.where(in_diag, tri_ref[...], jnp.zeros((), BF16))
        return gated

    gated_a, gated_b = _lockstep(prepare(0, qa_ref, ga_ref), prepare(1, qb_ref, gb_ref))

    n_a_max = (nb // 2 - 1) // bpc + 1
    mx = [jnp.full((rows, LANES), M_INIT, F32)] * 2
    units = []
    for u in range(n_units):
        which = 0 if u == 0 else 1 if u >= n_a_max else (u >= n_a).astype(jnp.int32)
        static = isinstance(which, int)
        kc = u - which * n_a
        k0 = kc * ch if isinstance(kc, int) else pl.multiple_of(kc * ch, ch)
        units.append((which, k0))
        owner = chunk_of[which] if static else jnp.where(which == 1, chunk_of[1], chunk_of[0])
        key_mask = jnp.where(kc == owner // bpc, etd_ref[which], et_ref[kc])
        rhs = jnp.concatenate([ks_ref[pl.ds(k0, ch), :], key_mask], axis=1)
        cols = col_blocks(_dot_nt(q2_ref[which], rhs))
        cm = col_max(cols)
        if static:
            mx[which] = jnp.maximum(mx[which], cm)
        else:
            mx = [jnp.maximum(mx[0], jnp.where(which == 0, cm, M_INIT)),
                  jnp.maximum(mx[1], jnp.where(which == 1, cm, M_INIT))]
        for j, c_ in enumerate(cols):
            s_ref[u, :, j * LANES:(j + 1) * LANES] = c_

    m = [jnp.broadcast_to(jnp.max(mx_i, axis=-1, keepdims=True), (rows, LANES)) for mx_i in mx]
    acc = [None, None]

    def add(total, part):
        return part if total is None else total + part

    for u, (which, k0) in enumerate(units):
        static = isinstance(which, int)
        m_u = m[which] if static else jnp.where(which == 1, m[1], m[0])
        p = jnp.concatenate([jnp.exp(s_ref[u, :, j * LANES:(j + 1) * LANES] - m_u).astype(BF16)
                             for j in range(ch // LANES)], axis=1)
        part = _dot(p, with_ones(vs_ref[pl.ds(k0, ch), :]))
        if static:
            acc[which] = add(acc[which], part)
        else:
            w_b = (which == 1).astype(F32)
            acc = [add(acc[0], part * (1.0 - w_b)), add(acc[1], part * w_b)]

    oa_ref[...] = _combine_finish(gated_a, finish(acc[0]), nq).astype(oa_ref.dtype)
    ob_ref[...] = _combine_finish(gated_b, finish(acc[1]), nq).astype(ob_ref.dtype)


def _nsa_prompt(q, gates, cmp_kv, kvb, *, n_seq, seq):
    nb = seq // BLK
    rows = NSA_HEADS * BLK
    ch = min(SEL_CHUNK, seq)
    wk = min(WINDOW + LANES, seq)
    bpc = ch // BLK
    assert seq % ch == 0 and wk % LANES == 0 and nb % 2 == 0 and nb <= LANES
    half = nb // 2
    n_units = nb // bpc + 1
    assert nb <= BLK
    et = np.zeros((seq // ch, ch, LANES), np.float32)
    for c in range(seq // ch):
        et[c, np.arange(ch), (c * ch + np.arange(ch)) // BLK] = 1.0
    key = np.arange(ch)[:, None]
    qry = np.arange(BLK)[None, :]
    tri = np.zeros((ch, LANES), np.float32)
    tri[:, BLK:] = (key % BLK) > qry
    n_delta = WINDOW // BLK + LANES // BLK
    wm = np.zeros((n_delta, wk, LANES), np.float32)
    for dl in range(n_delta):
        dist = dl * BLK + qry - np.arange(wk)[:, None]
        wm[dl, :, BLK:] = (dist < 0) | (dist >= WINDOW)
    chunk_a = lambda w_: pl.BlockSpec((BLK, w_), lambda b, i: (b * nb + i, 0))
    chunk_b = lambda w_: pl.BlockSpec((BLK, w_), lambda b, i: (b * nb + nb - 1 - i, 0))
    out_spec = pl.BlockSpec((BLK, NSA_WIDTH), lambda b, i: (b * half + i, 0))
    kv = lambda col: pl.BlockSpec((seq, KV_WIDTH), lambda b, i: (b, col))
    cmp_ = lambda which: pl.BlockSpec((1, nb, KV_WIDTH), lambda b, i: (which, b, 0))
    out = jax.ShapeDtypeStruct((n_seq * half * BLK, NSA_WIDTH), BF16)
    o_a, o_b = pl.pallas_call(
        functools.partial(_nsa_pair_kernel, nb=nb, ch=ch, wk=wk),
        grid=(n_seq, half),
        in_specs=[chunk_a(NSA_WIDTH), chunk_b(NSA_WIDTH), chunk_a(LANES), chunk_b(LANES),
                  cmp_(0), cmp_(1), kv(2), kv(3), kv(4), kv(5),
                  pl.BlockSpec(et.shape, lambda b, i: (0, 0, 0)),
                  pl.BlockSpec(tri.shape, lambda b, i: (0, 0)),
                  pl.BlockSpec(wm.shape, lambda b, i: (0, 0, 0))],
        out_specs=[out_spec, out_spec],
        out_shape=[out, out],
        scratch_shapes=[pltpu.VMEM((2, rows, 2 * LANES), BF16),
                        pltpu.VMEM((2, ch, LANES), BF16),
                        pltpu.VMEM((n_units, rows, ch), F32)],
        compiler_params=_params("parallel", "arbitrary"),
        name="nsa_prompt",
    )(q, q, gates, gates, cmp_kv, cmp_kv, kvb, kvb, kvb, kvb,
      jnp.asarray(et, dtype=BF16), jnp.asarray(tri, dtype=BF16), jnp.asarray(wm, dtype=BF16))
    return o_a, o_b


def _unmirror(o_a, o_b, n_seq):
    half = o_a.shape[0] // (n_seq * BLK)
    o_a = o_a.reshape(n_seq, half, BLK, NSA_WIDTH)
    o_b = o_b.reshape(n_seq, half, BLK, NSA_WIDTH)[:, ::-1]
    return jnp.concatenate([o_a, o_b], axis=1).reshape(-1, NSA_WIDTH)


def _nsa_sample_kernel(pt_ref, q_ref, gate_ref, kc_ref, vc_ref, cache_ref, nks_ref, nvs_ref,
                       wt_ref, nkw_ref, nvw_ref, e_ref, o_ref, kv_buf, s_scr, sems,
                       *, nq, past, nbp, tk):
    step = pl.program_id(0)
    sb = q_ref.shape[0]
    n_pages = past // PAGE_SIZE
    rows = NSA_HEADS * nq

    def page_copy(t, k, j):
        buf = (t % 2) * sb + k
        k0 = pl.multiple_of(j * PAGE_SIZE, PAGE_SIZE)
        return pltpu.make_async_copy(cache_ref.at[pt_ref[(t * sb + k) * n_pages + j], pl.ds(2, 2)],
                                     kv_buf.at[pl.ds(2 * buf, 2), :, pl.ds(k0, PAGE_SIZE)], sems.at[buf])

    def start_all(t):
        for k in range(sb):
            def start(j, _, k=k):
                page_copy(t, k, j).start()
                return 0
            lax.fori_loop(0, n_pages, start, 0, unroll=8)

    @pl.when(step == 0)
    def _():
        start_all(step)

    @pl.when(step + 1 < pl.num_programs(0))
    def _():
        start_all(step + 1)

    _lockstep(*[_nsa_sample_sequence(k, step, page_copy, q_ref, gate_ref, kc_ref, vc_ref, nks_ref, nvs_ref,
                                     wt_ref, nkw_ref, nvw_ref, e_ref, o_ref, kv_buf, s_scr,
                                     nq=nq, past=past, nbp=nbp, tk=tk) for k in range(sb)])


def _nsa_sample_sequence(k, step, page_copy, q_ref, gate_ref, kc_ref, vc_ref, nks_ref, nvs_ref,
                         wt_ref, nkw_ref, nvw_ref, e_ref, o_ref, kv_buf, s_scr, *, nq, past, nbp, tk):
    sb = q_ref.shape[0]
    n_pages = past // PAGE_SIZE
    rows = NSA_HEADS * nq
    buf = (step % 2) * sb + k
    q2 = _stack_queries(q_ref[k], nq)
    q_pos = past + lax.broadcasted_iota(jnp.int32, (rows, 1), 0) % nq
    cur = past // BLK

    n_ids = lax.broadcasted_iota(jnp.int32, (rows, nbp), 1)
    s_c = _dot_nt(q2, kc_ref[k].astype(BF16))
    yield
    p_c = _masked_softmax(s_c, (n_ids + 1) * BLK - 1 <= q_pos)
    o_c = _dot(p_c.astype(BF16), vc_ref[k].astype(BF16))
    yield

    n_sel = lax.broadcasted_iota(jnp.int32, (nq, nbp), 1)
    sels = []
    for g in range(NSA_KV):
        imp = p_c[g * NSA_REP * nq:(g * NSA_REP + 1) * nq]
        for r in range(1, NSA_REP):
            imp = imp + p_c[(g * NSA_REP + r) * nq:(g * NSA_REP + r + 1) * nq]
        sel = _select_blocks(imp, n_sel, cur, cur + 1)
        sels.extend([sel] * NSA_REP)
        yield
    keys = _dot(jnp.concatenate(sels, axis=0).astype(BF16), e_ref[...])
    bias = (keys - 1.0) * (-NEG_INF)
    yield

    def col_max(s):
        m = s[:, 0:LANES]
        for j in range(1, s.shape[1] // LANES):
            m = jnp.maximum(m, s[:, j * LANES:(j + 1) * LANES])
        return m

    def col_sum(p):
        t = p[:, 0:LANES]
        for j in range(1, p.shape[1] // LANES):
            t = t + p[:, j * LANES:(j + 1) * LANES]
        return t

    def row_max(mx):
        return jnp.broadcast_to(jnp.max(mx, axis=-1, keepdims=True), mx.shape)

    def tiled(m, width):
        return m if width == LANES else jnp.concatenate([m] * (width // LANES), axis=1)

    new_pos = past + lax.broadcasted_iota(jnp.int32, (rows, LANES), 1)

    wlen = wt_ref.shape[3]
    w_pos = past - wlen + lax.broadcasted_iota(jnp.int32, (rows, wlen), 1)
    d = q_pos - w_pos
    s_w = _dot(q2, wt_ref[k, 0].astype(BF16))
    s_w = jnp.where((d >= 0) & (d < WINDOW) & (w_pos >= 0), s_w, NEG_INF)
    d = q_pos - new_pos
    s_nw = _dot_nt(q2, nkw_ref[k].astype(BF16))
    s_nw = jnp.where((d >= 0) & (d < WINDOW), s_nw, NEG_INF)
    yield
    m_w = row_max(jnp.maximum(jnp.maximum(col_max(s_w), s_nw), M_INIT))
    p_w = jnp.exp(s_w - tiled(m_w, wlen))
    p_nw = jnp.exp(s_nw - m_w)
    acc = _dot_nt(p_w.astype(BF16), wt_ref[k, 1].astype(BF16)) + _dot(p_nw.astype(BF16), nvw_ref[k].astype(BF16))
    o_w = acc / jnp.maximum(jnp.sum(col_sum(p_w) + p_nw, axis=-1, keepdims=True), 1e-30)
    yield

    def wait(j, _):
        page_copy(step, k, j).wait()
        return 0

    lax.fori_loop(0, n_pages, wait, 0, unroll=8)

    mx = jnp.full((rows, LANES), M_INIT, F32)
    for t in range(past // tk):
        s = _dot(q2, kv_buf[2 * buf, :, t * tk:(t + 1) * tk].astype(BF16)) + bias[:, t * tk:(t + 1) * tk]
        s_scr[k, :, t * tk:(t + 1) * tk] = s
        mx = jnp.maximum(mx, col_max(s))
        yield
    s_new = _dot_nt(q2, nks_ref[k].astype(BF16)) + bias[:, past:past + LANES]
    s_new = jnp.where(new_pos <= q_pos, s_new, NEG_INF)
    m_s = row_max(jnp.maximum(mx, s_new))
    p_new = jnp.exp(s_new - m_s)
    acc = _dot(p_new.astype(BF16), nvs_ref[k].astype(BF16))
    lsum = p_new
    yield
    for t in range(past // tk):
        p = jnp.exp(s_scr[k, :, t * tk:(t + 1) * tk] - tiled(m_s, tk))
        lsum = lsum + col_sum(p)
        acc = acc + _dot_nt(p.astype(BF16), kv_buf[2 * buf + 1, :, t * tk:(t + 1) * tk].astype(BF16))
        yield
    o_s = acc / jnp.maximum(jnp.sum(lsum, axis=-1, keepdims=True), 1e-30)

    o_ref[k] = _combine_heads(gate_ref[k], o_c, o_s, o_w, nq).astype(o_ref.dtype)


def _nsa_sample(pt_flat, q3, gates3, kc, vc, cache_t, new_rows, win_t, expand, *, past, tk, seqs_per_step):
    n_seq, nq, _ = q3.shape
    nbp = kc.shape[1]
    rows = NSA_HEADS * nq
    sb = seqs_per_step
    per_seq = lambda a: pl.BlockSpec((sb,) + a.shape[1:], lambda b, pt: (b,) + (0,) * (a.ndim - 1))
    new = lambda col: pl.BlockSpec((sb, LANES, KV_WIDTH), lambda b, pt: (b, 0, col))
    grid_spec = pltpu.PrefetchScalarGridSpec(
        num_scalar_prefetch=1,
        grid=(n_seq // sb,),
        in_specs=[per_seq(q3), per_seq(gates3), per_seq(kc), per_seq(vc),
                  pl.BlockSpec(memory_space=pl.ANY),
                  new(2), new(3), per_seq(win_t), new(4), new(5),
                  pl.BlockSpec(expand.shape, lambda b, pt: (0, 0), pipeline_mode=pl.Buffered(1))],
        out_specs=pl.BlockSpec((sb, nq, NSA_WIDTH), lambda b, pt: (b, 0, 0)),
        scratch_shapes=[pltpu.VMEM((4 * sb, KV_WIDTH, past), F32),
                        pltpu.VMEM((sb, rows, past), F32), pltpu.SemaphoreType.DMA((2 * sb,))])
    return pl.pallas_call(
        functools.partial(_nsa_sample_kernel, nq=nq, past=past, nbp=nbp, tk=tk),
        grid_spec=grid_spec,
        out_shape=jax.ShapeDtypeStruct((n_seq, nq, NSA_WIDTH), F32),
        compiler_params=_params("arbitrary"),
        name="nsa_sample",
    )(pt_flat, q3, gates3, kc, vc, cache_t, new_rows, new_rows, win_t, new_rows, new_rows, expand)


def _expand_matrix(nb, n_keys):
    return jnp.asarray(np.arange(n_keys)[None, :] // BLK == np.arange(nb)[:, None], dtype=BF16)


def _merge_kernel(x_ref, ssm_ref, *refs, mirrored):
    nsa_refs = refs[:2] if mirrored else refs[:1]
    gpre_ref, wm_ref, wbs_ref, wbn_ref, wo_ref, gpost_ref, gx_ref, wxq_ref, x1_ref, qx_ref = refs[len(nsa_refs):]
    x = x_ref[...]
    tm = x.shape[0]
    if mirrored:
        upper = nsa_refs[1][...]
        n_chunks = tm // BLK
        upper = jnp.concatenate([upper[(n_chunks - 1 - s) * BLK:(n_chunks - s) * BLK] for s in range(n_chunks)],
                                axis=0)
        nsa = jnp.where(pl.program_id(0) < pl.num_programs(0) // 2, nsa_refs[0][...], upper)
    else:
        nsa = nsa_refs[0][...]
    n_seq = ssm_ref.shape[1] // tm
    ssm = jnp.concatenate([ssm_ref[s, pl.ds(pl.program_id(1), tm, stride=n_seq), :]
                           for s in range(SSM_WIDTH // LANES)], axis=1)
    a = _rms(x, gpre_ref[...]).astype(BF16)
    g_ssm = _sigmoid(_dot(a, wm_ref[:, 0:D_MODEL]))
    g_nsa = _sigmoid(_dot(a, wm_ref[:, D_MODEL:2 * D_MODEL]))
    merged = (g_ssm * _dot(ssm.astype(BF16), wbs_ref[...])
              + g_nsa * _dot(nsa, wbn_ref[...]))
    x1 = x + _rms(_dot(merged.astype(BF16), wo_ref[...]), gpost_ref[...])
    x1_ref[...] = x1
    c = _rms(x1, gx_ref[...]).astype(BF16)
    qx_ref[...] = (_dot(c, wxq_ref[...]) * (X_HEAD_DIM ** -0.5)).astype(BF16)


def _merge(x2d, ssm_slabs, nsa_o, weights, *, tm, n_tiles):
    n = x2d.shape[0]
    n_b = n // (n_tiles * tm)
    row = lambda w_: pl.BlockSpec((tm, w_), lambda t, b: (b * n_tiles + t, 0))
    full = lambda a: pl.BlockSpec(a.shape, lambda t, b: (0,) * a.ndim, pipeline_mode=pl.Buffered(1))
    ssm_spec = pl.BlockSpec((SSM_WIDTH // LANES, tm * n_b, LANES), lambda t, b: (0, t, 0))
    mirrored = isinstance(nsa_o, tuple)
    if mirrored:
        half = n_tiles // 2
        nsa_specs = [pl.BlockSpec((tm, NSA_WIDTH), lambda t, b: (b * half + jnp.minimum(t, half - 1), 0)),
                     pl.BlockSpec((tm, NSA_WIDTH), lambda t, b: (b * half + jnp.minimum(n_tiles - 1 - t, half - 1), 0))]
        nsa_args = list(nsa_o)
    else:
        nsa_specs, nsa_args = [row(NSA_WIDTH)], [nsa_o]
    return pl.pallas_call(
        functools.partial(_merge_kernel, mirrored=mirrored),
        grid=(n_tiles, n_b),
        in_specs=[row(D_MODEL), ssm_spec] + nsa_specs + [full(w) for w in weights],
        out_specs=[row(D_MODEL), row(X_WIDTH)],
        out_shape=[jax.ShapeDtypeStruct((n, D_MODEL), F32), jax.ShapeDtypeStruct((n, X_WIDTH), BF16)],
        compiler_params=_params("parallel", "arbitrary"),
        name="merge",
    )(x2d, ssm_slabs, *nsa_args, *weights)


def _xattn_tile(q, kv_ref, m_len):
    outs = []
    for h in range(X_HEADS):
        cols = slice(h * X_HEAD_DIM, (h + 1) * X_HEAD_DIM)
        k = kv_ref[0, pl.ds(h, m_len, stride=2 * X_HEADS), :].astype(BF16)
        v = kv_ref[0, pl.ds(X_HEADS + h, m_len, stride=2 * X_HEADS), :].astype(BF16)
        s = _dot_nt(q[:, cols], k)
        m = jnp.max(s, axis=-1, keepdims=True)
        p = jnp.exp(s - m)
        p = p / jnp.sum(p, axis=-1, keepdims=True)
        outs.append(_dot(p.astype(BF16), v))
    return jnp.concatenate(outs, axis=1)


def _xattn_kernel(q_ref, kv_ref, o_ref, *, m_len):
    t = q_ref.shape[1]
    n_rows = 2 * X_HEADS
    for s in range(q_ref.shape[0]):
        q = q_ref[s].astype(F32)
        z = jnp.zeros((t, X_HEAD_DIM), F32)
        q_bd = jnp.concatenate(
            [jnp.concatenate([q[:, h * X_HEAD_DIM:(h + 1) * X_HEAD_DIM] if j == h else z for j in range(X_HEADS)],
                             axis=1) for h in range(X_HEADS)], axis=0).astype(BF16)
        k = jnp.concatenate([kv_ref[s, pl.ds(h, m_len, stride=n_rows), :] for h in range(X_HEADS)],
                            axis=1).astype(BF16)
        v = jnp.concatenate([kv_ref[s, pl.ds(X_HEADS + h, m_len, stride=n_rows), :] for h in range(X_HEADS)],
                            axis=1).astype(BF16)
        sc = _dot_nt(q_bd, k)
        p = jnp.exp(sc - jnp.max(sc, axis=-1, keepdims=True))
        p = p / jnp.sum(p, axis=-1, keepdims=True)
        o = _dot(p.astype(BF16), v)
        o_ref[s] = jnp.concatenate([o[h * t:(h + 1) * t, h * X_HEAD_DIM:(h + 1) * X_HEAD_DIM]
                                    for h in range(X_HEADS)], axis=1).astype(o_ref.dtype)


def _xattn(q3, mem_kv_rows, *, seqs_per_step):
    n_seq, t, _ = q3.shape
    m_len = mem_kv_rows.shape[1] // (2 * X_HEADS)
    sb = seqs_per_step
    return pl.pallas_call(
        functools.partial(_xattn_kernel, m_len=m_len),
        grid=(n_seq // sb,),
        in_specs=[pl.BlockSpec((sb, t, X_WIDTH), lambda b: (b, 0, 0)),
                  pl.BlockSpec((sb, m_len * 2 * X_HEADS, X_HEAD_DIM), lambda b: (b, 0, 0))],
        out_specs=pl.BlockSpec((sb, t, X_WIDTH), lambda b: (b, 0, 0)),
        out_shape=jax.ShapeDtypeStruct((n_seq, t, X_WIDTH), q3.dtype),
        compiler_params=_params("parallel"),
        name="xattn",
    )(q3, mem_kv_rows)


def _mlp_kernel(x1_ref, o_ref, *refs, m_len):
    if m_len:
        kv_ref, refs = refs[0], refs[1:]
        o = _xattn_tile(o_ref[...], kv_ref, m_len).astype(BF16)
    else:
        o = o_ref[...]
    wxo_ref, gxp_ref, gm_ref, wup_ref, wdn_ref, gmp_ref, y_ref = refs
    x2 = x1_ref[...] + _rms(_dot(o, wxo_ref[...]), gxp_ref[...])
    m = _rms(x2, gm_ref[...]).astype(BF16)
    hid = jnp.maximum(_dot(m, wup_ref[...]), 0.0)
    hid = (hid * hid).astype(BF16)
    y_ref[...] = x2 + _rms(_dot(hid, wdn_ref[...]), gmp_ref[...])


def _mlp(x1, o, weights, *, tm, mem_kv_rows=None, tiles_per_seq=1):
    n = x1.shape[0]
    row = lambda w_: pl.BlockSpec((tm, w_), lambda i: (i, 0))
    full = lambda a: pl.BlockSpec(a.shape, lambda i: (0,) * a.ndim, pipeline_mode=pl.Buffered(1))
    m_len, kv_specs, kv_args = 0, [], []
    if mem_kv_rows is not None:
        m_len = mem_kv_rows.shape[1] // (2 * X_HEADS)
        kv_specs = [pl.BlockSpec((1,) + mem_kv_rows.shape[1:], lambda i: (i // tiles_per_seq, 0, 0))]
        kv_args = [mem_kv_rows]
    return pl.pallas_call(
        functools.partial(_mlp_kernel, m_len=m_len),
        grid=(n // tm,),
        in_specs=[row(D_MODEL), row(X_WIDTH)] + kv_specs + [full(w) for w in weights],
        out_specs=row(D_MODEL),
        out_shape=jax.ShapeDtypeStruct((n, D_MODEL), F32),
        compiler_params=_params("parallel"),
        name="mlp",
    )(x1, o, *kv_args, *weights)


def _memkv_kernel(m_ref, g_ref, w_ref, o_ref, *, tm):
    kv = _dot(_rms(m_ref[...], g_ref[...]).astype(BF16), w_ref[...])
    n_rows = 2 * X_HEADS
    for j in range(n_rows):
        o_ref[pl.ds(j, tm, stride=n_rows), :] = kv[:, j * X_HEAD_DIM:(j + 1) * X_HEAD_DIM]


def _memkv(mem2d, g, w, *, tm):
    n = mem2d.shape[0]
    n_rows = 2 * X_HEADS
    return pl.pallas_call(
        functools.partial(_memkv_kernel, tm=tm),
        grid=(n // tm,),
        in_specs=[pl.BlockSpec((tm, D_MODEL), lambda i: (i, 0)),
                  pl.BlockSpec(g.shape, lambda i: (0, 0)), pl.BlockSpec(w.shape, lambda i: (0, 0))],
        out_specs=pl.BlockSpec((tm * n_rows, X_HEAD_DIM), lambda i: (i, 0)),
        out_shape=jax.ShapeDtypeStruct((n * n_rows, X_HEAD_DIM), F32),
        compiler_params=_params("parallel"),
        name="memkv",
    )(mem2d, g, w)


def _row(v):
    return v.astype(F32).reshape(1, -1)


def kernel(x_prompt, x_sample, cache_nsa_kv, cache_win_kv, state_ssm, cache_mem_kv, page_table, mem_prompt, g_mix_pre, w_in, ssm_lam_re, ssm_lam_im, ssm_log_dt, ssm_b_re, ssm_b_im, ssm_c_re, ssm_c_im, ssm_d, w_glu, b_glu, cmp_pe_k, w_cmpk1, w_cmpk2, cmp_pe_v, w_cmpv1, w_cmpv2, w_br_ssm, w_br_nsa, w_out, g_mix_post, g_x_pre, g_mem, w_xq, w_xk, w_xv, w_xo, g_x_post, g_mlp_pre, w_up, w_down, g_mlp_post):
    depth = w_in.shape[0]
    n_seq_p, seq, _ = x_prompt.shape
    n_seq_s, nq, _ = x_sample.shape
    past = page_table.shape[1] * PAGE_SIZE
    assert depth == 1 and seq % BLK == 0 and nq <= SUBLANES and past % BLK == 0
    assert n_seq_p == SUBLANES and n_seq_s % SUBLANES == 0

    y_p = x_prompt.reshape(n_seq_p * seq, D_MODEL)
    y_s = x_sample.reshape(n_seq_s * nq, D_MODEL)
    l = 0

    w_proj = w_in[l, :, :N_PROJ].astype(BF16)
    w_gate = jnp.pad(w_in[l, :, N_PROJ:N_PROJ + N_GATE], ((0, 0), (0, LANES - N_GATE))).astype(BF16)
    w_merge = w_in[l, :, N_PROJ + N_GATE:].astype(BF16)
    lam_l, bm, cm = _ssm_params(ssm_lam_re[l], ssm_lam_im[l], ssm_log_dt[l], ssm_b_re[l], ssm_b_im[l],
                                ssm_c_re[l], ssm_c_im[l])
    ssm_w = (lam_l, bm, cm, _row(ssm_d[l]), w_glu[l].astype(BF16), _row(b_glu[l]))
    cmp_w = _compress_params(cmp_pe_k[l], w_cmpk1[l], w_cmpk2[l], cmp_pe_v[l], w_cmpv1[l], w_cmpv2[l])
    merge_w = (_row(g_mix_pre[l]), w_merge, w_br_ssm[l].astype(BF16), w_br_nsa[l].astype(BF16),
               w_out[l].astype(BF16), _row(g_mix_post[l]), _row(g_x_pre[l]), w_xq[l].astype(BF16))
    mlp_w = (w_xo[l].astype(BF16), _row(g_x_post[l]), _row(g_mlp_pre[l]), w_up[l].astype(BF16),
             w_down[l].astype(BF16), _row(g_mlp_post[l]))
    w_mem = jnp.concatenate([w_xk[l], w_xv[l]], axis=1).astype(BF16)

    tm_p = 512 if seq % 512 == 0 else seq
    nt_p = seq // tm_p
    tabs_p = _rope_tables(np.arange(seq))
    u_p, q_p, kvt_p, wint_p, kvb_p, gate_p = _proj(
        y_p, _row(g_mix_pre[l]), w_proj, w_gate, tabs_p, tm=tm_p, n_tab_blocks=nt_p, prompt=True)
    h0_p = jnp.zeros((1, SUBLANES, 2 * N_STATE), F32)
    tc_p = 64 if seq % 64 == 0 else seq
    ssm_p, hl_p = _ssm(u_p, h0_p, *ssm_w, n_groups=1, n_time=seq, tc=tc_p)

    assert seq % PAGE_SIZE == 0
    n_pages_p = n_seq_p * seq // PAGE_SIZE
    pe_t, w1_t = _compress_paged_params(cmp_pe_k[l], w_cmpk1[l], cmp_pe_v[l], w_cmpv1[l])
    cmp_p = _compress_paged(jnp.zeros((n_pages_p,), jnp.int32), kvt_p, pe_t, w1_t, cmp_w[2],
                            m=min(n_pages_p, 128), pages_per_seq=seq // PAGE_SIZE)
    nsa_p = _nsa_prompt(q_p, gate_p, cmp_p, kvb_p, n_seq=n_seq_p, seq=seq)

    mem_kv_p = _memkv(mem_prompt.reshape(-1, D_MODEL), _row(g_mem[l]), w_mem, tm=256)
    m_len = mem_prompt.shape[1]
    mem_kv_p3 = mem_kv_p.reshape(n_seq_p, m_len * 2 * X_HEADS, X_HEAD_DIM)

    if nt_p % 2:
        nsa_p = _unmirror(*nsa_p, n_seq_p)
    x1_p, qx_p = _merge(y_p, ssm_p, nsa_p, merge_w, tm=tm_p, n_tiles=nt_p)
    y_p = _mlp(x1_p, qx_p, mlp_w, tm=tm_p, mem_kv_rows=mem_kv_p3, tiles_per_seq=nt_p)

    n_s = n_seq_s * nq
    tabs_s = tuple(np.tile(t, (n_seq_s, 1)) for t in _rope_tables(past + np.arange(nq)))
    u_s, q_s, kv_s, win_s, kvb_s, gate_s = _proj(y_s, _row(g_mix_pre[l]), w_proj, w_gate, tabs_s,
                                                 tm=n_s, n_tab_blocks=1, prompt=False)
    n_grp = n_seq_s // SUBLANES
    n_slabs = SSM_WIDTH // LANES
    u_s = (u_s.reshape(n_grp, SUBLANES, nq, n_slabs, LANES).transpose(3, 0, 2, 1, 4)
           .reshape(n_slabs, n_s, LANES))
    st = state_ssm[l].astype(F32).reshape(n_seq_s, N_STATE, 2)
    h0_s = _state_lanes(st[..., 0], st[..., 1]).reshape(n_grp, SUBLANES, 2 * N_STATE)
    ssm_s, hl_s = _ssm(u_s, h0_s, *ssm_w, n_groups=n_grp, n_time=nq, tc=nq)
    ssm_s = (ssm_s.reshape(n_slabs, n_grp, nq, SUBLANES, LANES).transpose(0, 1, 3, 2, 4)
             .reshape(n_slabs, n_s, LANES))

    n_pages = page_table.shape[1]
    n_pool = cache_nsa_kv.shape[1]
    cache_t = jnp.transpose(cache_nsa_kv[l], (0, 2, 3, 4, 1)).reshape(n_pool, 4, KV_WIDTH, PAGE_SIZE)
    win_t = jnp.transpose(cache_win_kv[l], (0, 2, 3, 4, 1)).reshape(n_seq_s, 2, KV_WIDTH, -1)
    pt_flat = page_table.reshape(-1).astype(jnp.int32)
    nb_past = past // BLK
    cmp_pages = _compress_paged(pt_flat, cache_t, pe_t, w1_t, cmp_w[2], m=min(n_seq_s * n_pages, 128))
    cmp_past = cmp_pages
    new_rows = jnp.pad(jnp.concatenate([kv_s, win_s], axis=1).reshape(n_seq_s, nq, 6 * KV_WIDTH),
                       ((0, 0), (0, LANES - nq), (0, 0)))
    cmp_new = _compress(new_rows[:, :BLK].reshape(n_seq_s * BLK, 6 * KV_WIDTH), *cmp_w,
                        n_blocks=n_seq_s, tm=n_seq_s)
    nbp = -(-(nb_past + 1) // LANES) * LANES
    cmp_s = jnp.concatenate([cmp_past.reshape(2, n_seq_s, nb_past, KV_WIDTH), cmp_new[:, :, None, :],
                             jnp.zeros((2, n_seq_s, nbp - nb_past - 1, KV_WIDTH), F32)], axis=2)
    nsa_s = _nsa_sample(pt_flat, q_s.astype(F32).reshape(n_seq_s, nq, NSA_WIDTH),
                        gate_s.reshape(n_seq_s, nq, LANES), cmp_s[0], cmp_s[1], cache_t, new_rows, win_t,
                        _expand_matrix(nbp, past + LANES), past=past, tk=min(past, 1024), seqs_per_step=2)

    x1_s, qx_s = _merge(y_s, ssm_s, nsa_s.reshape(n_s, NSA_WIDTH).astype(BF16), merge_w, tm=n_s, n_tiles=1)
    mem_kv_s3 = cache_mem_kv[l].reshape(n_seq_s, m_len * 2 * X_HEADS, X_HEAD_DIM)
    o_s = _xattn(qx_s.astype(F32).reshape(n_seq_s, nq, X_WIDTH), mem_kv_s3, seqs_per_step=SUBLANES)
    y_s = _mlp(x1_s, o_s.reshape(-1, X_WIDTH).astype(BF16), mlp_w, tm=n_s)

    def ssm_state(hl, n_seq):
        re, im = _state_unlanes(hl.reshape(n_seq, 2 * N_STATE))
        return jnp.stack([re, im], axis=-1).reshape(1, n_seq, SSM_GROUPS, SSM_STATE, 2)

    def token_major(xt):
        n_seq, parts, _, t_len = xt.shape
        return xt.reshape(1, n_seq, parts, NSA_KV, HEAD_DIM, t_len).transpose(0, 1, 5, 2, 3, 4)

    w_keep = min(WINDOW, seq)
    win_new = win_s.reshape(n_seq_s, nq, 2, NSA_KV, HEAD_DIM).astype(cache_win_kv.dtype)
    win_sample = jnp.concatenate([cache_win_kv[l], win_new], axis=1)[:, nq:]
    return (y_p.reshape(n_seq_p, seq, D_MODEL),
            y_s.reshape(n_seq_s, nq, D_MODEL),
            token_major(kvt_p),
            kv_s.reshape(1, n_seq_s, nq, 4, NSA_KV, HEAD_DIM),
            token_major(wint_p[:, :, :, seq - w_keep:]),
            win_sample[None],
            ssm_state(hl_p, n_seq_p),
            ssm_state(hl_s, n_seq_s),
            mem_kv_p.reshape(1, n_seq_p, m_len, 2, X_HEADS, X_HEAD_DIM))
```

```python
import functools
import math

import jax
import jax.numpy as jnp
import numpy as np
from jax import lax
from jax.experimental import pallas as pl
from jax.experimental.pallas import tpu as pltpu

F32 = jnp.float32
BF16 = jnp.bfloat16

D_MODEL = 1024
SSM_WIDTH = 512
SSM_GROUP = 16
SSM_GROUPS = 32
SSM_STATE = 64
N_STATE = SSM_GROUPS * SSM_STATE
STATE_CHUNK = 512
N_CHUNKS = N_STATE // STATE_CHUNK
NSA_HEADS = 8
HEAD_DIM = 64
NSA_WIDTH = NSA_HEADS * HEAD_DIM
NSA_KV = 2
NSA_REP = NSA_HEADS // NSA_KV
KV_WIDTH = NSA_KV * HEAD_DIM
BLK = 64
N_SEL = 16
WINDOW = 512
ROT_DIM = 16
ROPE_THETA = 500000.0
PAGE_SIZE = 128
X_HEADS = 4
X_HEAD_DIM = 128
X_WIDTH = X_HEADS * X_HEAD_DIM
D_FF = 4 * D_MODEL
EPS = 1e-6
NEG_INF = -1e30
M_INIT = -1e29
FORCE_SCORE = 1e4
LANES = 128
SUBLANES = 8
VMEM_LIMIT = 56 * 1024 * 1024

N_PROJ = SSM_WIDTH + NSA_WIDTH + 6 * KV_WIDTH
N_GATE = 3 * NSA_HEADS
SEL_CHUNK = 512
PAGE_PITCH = PAGE_SIZE + SUBLANES


def _params(*sem):
    return pltpu.CompilerParams(dimension_semantics=sem, vmem_limit_bytes=VMEM_LIMIT)


def _rms(x, g):
    return x * lax.rsqrt(jnp.mean(x * x, axis=-1, keepdims=True) + EPS) * g


def _gelu(x):
    return 0.5 * x * (1.0 + jnp.tanh(math.sqrt(2.0 / math.pi) * (x + 0.044715 * (x * x * x))))


def _sigmoid(x):
    return 1.0 / (1.0 + jnp.exp(-x))


def _dot(a, b):
    return jnp.dot(a, b, preferred_element_type=F32)


def _dot_nt(a, b):
    return lax.dot_general(a, b, (((1,), (1,)), ((), ())), preferred_element_type=F32)


def _lockstep(*stagewise):
    results = [None] * len(stagewise)
    live = list(range(len(stagewise)))
    while live:
        for k in list(live):
            try:
                next(stagewise[k])
            except StopIteration as done:
                results[k] = done.value
                live.remove(k)
    return results


def _proj_kernel(x_ref, g_ref, w_ref, wg_ref, cos_ref, sp_ref, sm_ref,
                 u_ref, q_ref, kv_ref, win_ref, kvb_ref, gate_ref, *, token_minor):
    a = _rms(x_ref[...], g_ref[...]).astype(BF16)
    cos, sp, sm = cos_ref[...], sp_ref[...], sm_ref[...]

    def rope(blk):
        return blk * cos + pltpu.roll(blk, 8, 1) * sp + pltpu.roll(blk, LANES - 8, 1) * sm

    pairs = {}

    def lane_block(j):
        if j // 2 not in pairs:
            pairs[j // 2] = _dot(a, w_ref[:, (j // 2) * 2 * LANES:(j // 2 + 1) * 2 * LANES])
        return pairs[j // 2][:, (j % 2) * LANES:(j % 2 + 1) * LANES]

    if token_minor:
        tm = x_ref.shape[0]
        n_seq = u_ref.shape[1] // tm
        for s in range(SSM_WIDTH // LANES):
            u_ref[s, pl.ds(pl.program_id(1), tm, stride=n_seq), :] = lane_block(s)
    else:
        u_ref[...] = _dot(a, w_ref[:, 0:SSM_WIDTH])
    for j in range(NSA_WIDTH // LANES):
        q_ref[:, j * LANES:(j + 1) * LANES] = rope(lane_block(SSM_WIDTH // LANES + j)).astype(BF16)
    for j in range(6):
        blk = lane_block((SSM_WIDTH + NSA_WIDTH) // LANES + j)
        if j % 2 == 0:
            blk = rope(blk)
        out_ref, part = (kv_ref, j) if j < 4 else (win_ref, j - 4)
        if token_minor:
            out_ref[0, part] = blk.T
        else:
            out_ref[:, part * LANES:(part + 1) * LANES] = blk
        kvb_ref[:, j * LANES:(j + 1) * LANES] = blk.astype(BF16)
    gate_ref[...] = _sigmoid(_dot(a, wg_ref[...]))


def _proj(x2d, g, w, wg, tabs, *, tm, n_tab_blocks, prompt):
    n = x2d.shape[0]
    nt = n_tab_blocks
    n_b = n // (nt * tm)
    row = lambda w_: pl.BlockSpec((tm, w_), lambda t, b: (b * nt + t, 0))
    full = lambda a: pl.BlockSpec(a.shape, lambda t, b: (0,) * a.ndim)
    tab = pl.BlockSpec((tm, LANES), lambda t, b: (t, 0))
    if prompt:
        t_len = nt * tm
        n_slabs = SSM_WIDTH // LANES
        u_shape = jax.ShapeDtypeStruct((n_slabs, t_len * n_b, LANES), F32)
        u_spec = pl.BlockSpec((n_slabs, tm * n_b, LANES), lambda t, b: (0, t, 0))
        tok_minor = lambda parts: pl.BlockSpec((1, parts, KV_WIDTH, tm), lambda t, b: (b, 0, 0, t))
        kv_specs = [tok_minor(4), tok_minor(2)]
        kv_shapes = [jax.ShapeDtypeStruct((n_b, 4, KV_WIDTH, t_len), F32),
                     jax.ShapeDtypeStruct((n_b, 2, KV_WIDTH, t_len), F32)]
    else:
        u_shape = jax.ShapeDtypeStruct((n, SSM_WIDTH), F32)
        u_spec = row(SSM_WIDTH)
        kv_specs = [row(4 * KV_WIDTH), row(2 * KV_WIDTH)]
        kv_shapes = [jax.ShapeDtypeStruct((n, 4 * KV_WIDTH), F32), jax.ShapeDtypeStruct((n, 2 * KV_WIDTH), F32)]
    return pl.pallas_call(
        functools.partial(_proj_kernel, token_minor=prompt),
        grid=(nt, n_b),
        in_specs=[row(D_MODEL), full(g), full(w), full(wg), tab, tab, tab],
        out_specs=[u_spec, row(NSA_WIDTH)] + kv_specs + [row(6 * KV_WIDTH), row(LANES)],
        out_shape=[u_shape, jax.ShapeDtypeStruct((n, NSA_WIDTH), BF16)] + kv_shapes
                  + [jax.ShapeDtypeStruct((n, 6 * KV_WIDTH), BF16), jax.ShapeDtypeStruct((n, LANES), F32)],
        compiler_params=_params("parallel", "arbitrary"),
        name="proj",
    )(x2d, g, w, wg, *tabs)


def _rope_tables(pos):
    half = ROT_DIM // 2
    freqs = ROPE_THETA ** (-np.arange(half, dtype=np.float64) / half)
    ang = np.asarray(pos, np.float64)[:, None] * freqs[None, :]
    cos, sin = np.cos(ang), np.sin(ang)
    r = ang.shape[0]
    z8 = np.zeros((r, half))
    rest0 = np.zeros((r, HEAD_DIM - ROT_DIM))
    rest1 = np.ones((r, HEAD_DIM - ROT_DIM))
    c64 = np.concatenate([cos, cos, rest1], axis=1)
    sp64 = np.concatenate([z8, sin, rest0], axis=1)
    sm64 = np.concatenate([-sin, z8, rest0], axis=1)
    return tuple(np.tile(t, (1, LANES // HEAD_DIM)).astype(np.float32) for t in (c64, sp64, sm64))


def _ssm_kernel(u_ref, h0_ref, lam_ref, bm_ref, cm_ref, d_ref, wglu_ref, bglu_ref,
                y_ref, hlast_ref, hs_ref, hstate_ref, *, tc):
    j = pl.program_id(1)

    @pl.when(j == 0)
    def _():
        hstate_ref[...] = h0_ref[0]

    n_slabs = SSM_WIDTH // LANES
    u = jnp.concatenate([u_ref[s] for s in range(n_slabs)], axis=1)
    ub = u.astype(BF16)
    half_in = SSM_WIDTH // 2
    chunk_lanes = 2 * STATE_CHUNK
    chunks_per_half = N_CHUNKS // 2
    ys = [None, None]
    for c in range(N_CHUNKS):
        h = c // chunks_per_half
        re0 = c * chunk_lanes
        im0 = re0 + STATE_CHUNK
        u_half = ub[:, h * half_in:(h + 1) * half_in]
        hs_ref[:, re0:re0 + STATE_CHUNK] = _dot(u_half, bm_ref[c, 0])
        hs_ref[:, im0:im0 + STATE_CHUNK] = _dot(u_half, bm_ref[c, 1])
        lr = jnp.broadcast_to(lam_ref[0:1, re0:re0 + STATE_CHUNK], (SUBLANES, STATE_CHUNK))
        li = jnp.broadcast_to(lam_ref[0:1, im0:im0 + STATE_CHUNK], (SUBLANES, STATE_CHUNK))
        hr = hstate_ref[:, re0:re0 + STATE_CHUNK]
        hi = hstate_ref[:, im0:im0 + STATE_CHUNK]
        for t in range(tc):
            r0 = t * SUBLANES
            hr, hi = (lr * hr - li * hi + hs_ref[r0:r0 + SUBLANES, re0:re0 + STATE_CHUNK],
                      lr * hi + li * hr + hs_ref[r0:r0 + SUBLANES, im0:im0 + STATE_CHUNK])
            hs_ref[r0:r0 + SUBLANES, re0:re0 + STATE_CHUNK] = hr
            hs_ref[r0:r0 + SUBLANES, im0:im0 + STATE_CHUNK] = hi
        hstate_ref[:, re0:re0 + STATE_CHUNK] = hr
        hstate_ref[:, im0:im0 + STATE_CHUNK] = hi
        part = (_dot(hs_ref[:, re0:re0 + STATE_CHUNK].astype(BF16), cm_ref[c, 0])
                + _dot(hs_ref[:, im0:im0 + STATE_CHUNK].astype(BF16), cm_ref[c, 1]))
        ys[h] = part if ys[h] is None else ys[h] + part

    y = jnp.concatenate(ys, axis=1) + d_ref[...] * u
    y = _gelu(y)
    z = _dot(y.astype(BF16), wglu_ref[...]) + bglu_ref[...]
    out = y * _sigmoid(z)
    for s in range(n_slabs):
        y_ref[s] = out[:, s * LANES:(s + 1) * LANES]

    @pl.when(j == pl.num_programs(1) - 1)
    def _():
        hlast_ref[0] = hstate_ref[...]


def _ssm(u_tb, h0, lam, bm, cm, d, wglu, bglu, *, n_groups, n_time, tc):
    rows = tc * SUBLANES
    nt = n_time // tc
    n_slabs = SSM_WIDTH // LANES
    full = lambda a: pl.BlockSpec(a.shape, lambda g, j: (0,) * a.ndim)
    st = pl.BlockSpec((1, SUBLANES, 2 * N_STATE), lambda g, j: (g, 0, 0))
    slabs = pl.BlockSpec((n_slabs, rows, LANES), lambda g, j: (0, g * nt + j, 0))
    return pl.pallas_call(
        functools.partial(_ssm_kernel, tc=tc),
        grid=(n_groups, nt),
        in_specs=[slabs, st, full(lam), full(bm), full(cm), full(d), full(wglu), full(bglu)],
        out_specs=[slabs, st],
        out_shape=[jax.ShapeDtypeStruct((n_slabs, n_groups * n_time * SUBLANES, LANES), F32),
                   jax.ShapeDtypeStruct((n_groups, SUBLANES, 2 * N_STATE), F32)],
        scratch_shapes=[pltpu.VMEM((rows, 2 * N_STATE), F32), pltpu.VMEM((SUBLANES, 2 * N_STATE), F32)],
        compiler_params=_params("parallel", "arbitrary"),
        name="ssm",
    )(u_tb, h0, lam, bm, cm, d, wglu, bglu)


def _state_lanes(re, im):
    lead = re.shape[:-1]
    r = re.reshape(lead + (N_CHUNKS, 1, STATE_CHUNK))
    i = im.reshape(lead + (N_CHUNKS, 1, STATE_CHUNK))
    return jnp.concatenate([r, i], axis=-2).reshape(lead + (2 * N_STATE,))


def _state_unlanes(x):
    lead = x.shape[:-1]
    y = x.reshape(lead + (N_CHUNKS, 2, STATE_CHUNK))
    return y[..., 0, :].reshape(lead + (N_STATE,)), y[..., 1, :].reshape(lead + (N_STATE,))


def _ssm_params(lam_re, lam_im, log_dt, b_re, b_im, c_re, c_im):
    lr, li = lam_re.astype(F32), lam_im.astype(F32)
    dt = jnp.exp(log_dt.astype(F32))[:, None]
    mag = jnp.exp(lr * dt)
    bar_re, bar_im = mag * jnp.cos(li * dt), mag * jnp.sin(li * dt)
    den = lr * lr + li * li
    f_re = ((bar_re - 1.0) * lr + bar_im * li) / den
    f_im = (bar_im * lr - (bar_re - 1.0) * li) / den
    b_re, b_im = b_re.astype(F32), b_im.astype(F32)
    bb_re = f_re[..., None] * b_re - f_im[..., None] * b_im
    bb_im = f_re[..., None] * b_im + f_im[..., None] * b_re
    g_chunk = STATE_CHUNK // SSM_STATE
    g_half = SSM_GROUPS // 2

    def same_group(c):
        h = c // (N_CHUNKS // 2)
        return (h * g_half + np.arange(g_half)[:, None] == c * g_chunk + np.arange(g_chunk)[None, :]
                ).astype(np.float32)

    def in_blockdiag(x, c):
        t = x[c * g_chunk:(c + 1) * g_chunk].transpose(2, 0, 1)[None]
        return (t * same_group(c)[:, None, :, None]).reshape(SSM_WIDTH // 2, STATE_CHUNK)

    def out_blockdiag(x, c):
        t = x[c * g_chunk:(c + 1) * g_chunk].transpose(0, 2, 1)[:, :, None, :]
        return (t * same_group(c).T[:, None, :, None]).reshape(STATE_CHUNK, SSM_WIDTH // 2)

    bm = jnp.stack([jnp.stack([in_blockdiag(bb_re, c), in_blockdiag(bb_im, c)])
                    for c in range(N_CHUNKS)]).astype(BF16)
    cm = jnp.stack([jnp.stack([out_blockdiag(c_re.astype(F32), c), out_blockdiag(-c_im.astype(F32), c)])
                    for c in range(N_CHUNKS)]).astype(BF16)
    lam_l = _state_lanes(bar_re.reshape(1, N_STATE), bar_im.reshape(1, N_STATE))
    return lam_l, bm, cm


def _compress_kernel(x_ref, pe_ref, w1_ref, w2_ref, o_ref, *, tm):
    acc = jnp.zeros((tm, KV_WIDTH), F32)
    for sp in range(BLK // 2):
        s0 = 2 * sp
        xa = x_ref[pl.ds(s0, tm, stride=BLK), :] + pe_ref[0, s0:s0 + 1, :]
        xb = x_ref[pl.ds(s0 + 1, tm, stride=BLK), :] + pe_ref[0, s0 + 1:s0 + 2, :]
        acc = acc + _dot(jnp.concatenate([xa, xb], axis=1).astype(BF16), w1_ref[0, sp])
    o_ref[0] = _dot(_gelu(acc).astype(BF16), w2_ref[0])


def _compress(x2d, pe, w1, w2, *, n_blocks, tm):
    return pl.pallas_call(
        functools.partial(_compress_kernel, tm=tm),
        grid=(2, n_blocks // tm),
        in_specs=[pl.BlockSpec((tm * BLK, KV_WIDTH), lambda c, i: (i, c)),
                  pl.BlockSpec((1, BLK, KV_WIDTH), lambda c, i: (c, 0, 0)),
                  pl.BlockSpec((1, BLK // 2, 2 * KV_WIDTH, KV_WIDTH), lambda c, i: (c, 0, 0, 0)),
                  pl.BlockSpec((1, KV_WIDTH, KV_WIDTH), lambda c, i: (c, 0, 0))],
        out_specs=pl.BlockSpec((1, tm, KV_WIDTH), lambda c, i: (c, i, 0)),
        out_shape=jax.ShapeDtypeStruct((2, n_blocks, KV_WIDTH), F32),
        compiler_params=_params("parallel", "parallel"),
        name="compress",
    )(x2d, pe, w1, w2)


def _compress_params(pe_k, w1_k, w2_k, pe_v, w1_v, w2_v):
    def bd(w):
        z = jnp.zeros_like(w)
        return jnp.concatenate([jnp.concatenate([w, z], axis=-1), jnp.concatenate([z, w], axis=-1)], axis=-2)

    def one(pe, w1, w2):
        w1s = bd(w1.astype(F32).reshape(BLK, HEAD_DIM, HEAD_DIM))
        return (jnp.tile(pe.astype(F32), (1, NSA_KV)),
                w1s.reshape(BLK // 2, 2 * KV_WIDTH, KV_WIDTH).astype(BF16),
                bd(w2.astype(F32)).astype(BF16))

    k, v = one(pe_k, w1_k, w2_k), one(pe_v, w1_v, w2_v)
    return tuple(jnp.stack([a, b]) for a, b in zip(k, v))


def _compress_paged_kernel(pt_ref, cache_ref, pe_ref, w1_ref, w2_ref, o_ref, buf_ref, sems, *, m,
                           pages_per_seq):
    step = pl.program_id(0)

    def page_copy(s, j):
        slot = s % 2
        row0 = pl.multiple_of(j * PAGE_PITCH, SUBLANES)
        idx = s * m + j
        if pages_per_seq:
            tok0 = pl.multiple_of((idx % pages_per_seq) * PAGE_SIZE, PAGE_SIZE)
            src = cache_ref.at[idx // pages_per_seq, pl.ds(0, 2), :, pl.ds(tok0, PAGE_SIZE)]
        else:
            src = cache_ref.at[pt_ref[idx], pl.ds(0, 2)]
        return pltpu.make_async_copy(src, buf_ref.at[pl.ds(2 * slot, 2), pl.ds(row0, PAGE_SIZE), :],
                                     sems.at[slot])

    def start_all(s):
        def start(j, _):
            page_copy(s, j).start()
            return 0
        lax.fori_loop(0, m, start, 0, unroll=8)

    @pl.when(step == 0)
    def _():
        start_all(step)

    @pl.when(step + 1 < pl.num_programs(0))
    def _():
        start_all(step + 1)

    def wait(j, _):
        page_copy(step, j).wait()
        return 0

    lax.fori_loop(0, m, wait, 0, unroll=8)

    for c in range(2):
        tile = 2 * (step % 2) + c
        res = []
        for kv in range(NSA_KV):
            acc = jnp.zeros((m, PAGE_SIZE), F32)
            for dp in range(HEAD_DIM // 2):
                d0 = 2 * dp
                r0 = kv * HEAD_DIM + d0
                xa = buf_ref[tile, pl.ds(r0, m, stride=PAGE_PITCH), :] + pe_ref[c, d0:d0 + 1, :]
                xb = buf_ref[tile, pl.ds(r0 + 1, m, stride=PAGE_PITCH), :] + pe_ref[c, d0 + 1:d0 + 2, :]
                acc = acc + _dot(jnp.concatenate([xa, xb], axis=1).astype(BF16), w1_ref[c, dp])
            res.append(_dot(_gelu(acc).astype(BF16), w2_ref[c]))
        for blk in range(PAGE_SIZE // BLK):
            cols = slice(blk * HEAD_DIM, (blk + 1) * HEAD_DIM)
            o_ref[c, pl.ds(blk, m, stride=PAGE_SIZE // BLK), :] = jnp.concatenate(
                [r[:, cols] for r in res], axis=1)


def _compress_paged(pt_flat, cache_t, pe_t, w1_t, w2, *, m, pages_per_seq=0):
    n = pt_flat.shape[0]
    full = lambda a: pl.BlockSpec(a.shape, lambda i, pt: (0,) * a.ndim)
    grid_spec = pltpu.PrefetchScalarGridSpec(
        num_scalar_prefetch=1,
        grid=(n // m,),
        in_specs=[pl.BlockSpec(memory_space=pl.ANY), full(pe_t), full(w1_t), full(w2)],
        out_specs=pl.BlockSpec((2, m * (PAGE_SIZE // BLK), KV_WIDTH), lambda i, pt: (0, i, 0)),
        scratch_shapes=[pltpu.VMEM((4, m * PAGE_PITCH, PAGE_SIZE), F32), pltpu.SemaphoreType.DMA((2,))])
    return pl.pallas_call(
        functools.partial(_compress_paged_kernel, m=m, pages_per_seq=pages_per_seq),
        grid_spec=grid_spec,
        out_shape=jax.ShapeDtypeStruct((2, n * (PAGE_SIZE // BLK), KV_WIDTH), F32),
        compiler_params=_params("arbitrary"),
        name="compress_paged",
    )(pt_flat, cache_t, pe_t, w1_t, w2)


def _compress_paged_params(pe_k, w1_k, pe_v, w1_v):
    def bd(w):
        z = jnp.zeros_like(w)
        return jnp.concatenate([jnp.concatenate([w, z], axis=-1), jnp.concatenate([z, w], axis=-1)], axis=-2)

    def one(pe, w1):
        w1d = bd(w1.astype(F32).reshape(BLK, HEAD_DIM, HEAD_DIM).transpose(1, 0, 2))
        return (jnp.tile(pe.astype(F32).T, (1, PAGE_SIZE // BLK)),
                w1d.reshape(HEAD_DIM // 2, 2 * PAGE_SIZE, PAGE_SIZE).astype(BF16))

    k, v = one(pe_k, w1_k), one(pe_v, w1_v)
    return tuple(jnp.stack([a, b]) for a, b in zip(k, v))


def _stack_queries(q, nq):
    q = q.astype(F32)
    z = jnp.zeros((nq, HEAD_DIM), F32)
    rows = []
    for h in range(NSA_HEADS):
        blk = q[:, h * HEAD_DIM:(h + 1) * HEAD_DIM]
        rows.append(jnp.concatenate([blk, z] if h < NSA_REP else [z, blk], axis=1))
    return (jnp.concatenate(rows, axis=0) * (HEAD_DIM ** -0.5)).astype(BF16)


def _masked_softmax(s, valid):
    s = jnp.where(valid, s, NEG_INF)
    m = jnp.max(s, axis=-1, keepdims=True)
    p = jnp.exp(s - m) * valid.astype(F32)
    return p / jnp.maximum(jnp.sum(p, axis=-1, keepdims=True), 1e-30)


def _select_blocks(imp, n_ids, cur, nb):
    forced = (n_ids == 0) | (n_ids == cur) | (n_ids == cur - 1)
    imp = jnp.where(forced, FORCE_SCORE, imp)
    imp = jnp.where(n_ids <= cur, imp, -FORCE_SCORE)
    rank = jnp.zeros(imp.shape, F32)
    for m in range(nb):
        col = imp[:, m:m + 1]
        beats = (col > imp) | ((col == imp) & (n_ids > m))
        rank = rank + beats.astype(F32)
    return (rank < float(N_SEL)).astype(F32)


def _select_blocks_t(imp_t, cur, nb):
    n_t = lax.broadcasted_iota(jnp.int32, imp_t.shape, 0)
    forced = (n_t == 0) | (n_t == cur) | (n_t == cur - 1)
    imp_t = jnp.where(forced, FORCE_SCORE, imp_t)
    imp_t = jnp.where(n_t <= cur, imp_t, -FORCE_SCORE)
    rank = jnp.zeros(imp_t.shape, F32)
    for m in range(nb):
        row = imp_t[m:m + 1, :]
        beats = (row > imp_t) | ((row == imp_t) & (n_t > m))
        rank = rank + beats.astype(F32)
    return ((rank < float(N_SEL)) & (n_t <= cur)).astype(F32)


def _combine_start(gates, o_c, o_w, nq):
    parts = []
    for h in range(NSA_HEADS):
        rows = slice(h * nq, (h + 1) * nq)
        parts.append((gates[:, 3 * h:3 * h + 1] * o_c[rows] + gates[:, 3 * h + 2:3 * h + 3] * o_w[rows],
                      jnp.broadcast_to(gates[:, 3 * h + 1:3 * h + 2], (nq, KV_WIDTH))))
    return parts


def _combine_finish(parts, o_s, nq):
    outs = []
    for h, (rest, g_s) in enumerate(parts):
        o = rest + g_s * o_s[h * nq:(h + 1) * nq]
        g = h // NSA_REP
        outs.append(o[:, g * HEAD_DIM:(g + 1) * HEAD_DIM])
    return jnp.concatenate(outs, axis=1)


def _combine_heads(gates, o_c, o_s, o_w, nq):
    return _combine_finish(_combine_start(gates, o_c, o_w, nq), o_s, nq)


def _nsa_quad_kernel(qlo_ref, qhi_ref, glo_ref, ghi_ref, kc_ref, vc_ref, ks_ref, vs_ref, kw_ref, vw_ref,
                     et_ref, tri_ref, wm_ref, olo_ref, ohi_ref, q2_ref, etd_ref, s_ref, *, nb, ch, wk, n_ch):
    j = pl.program_id(1)
    nq = BLK
    rows = NSA_HEADS * nq
    bpc = ch // BLK
    quarter = nb // 4
    chunk_of = (j, quarter + j, 3 * quarter - 1 - j, 4 * quarter - 1 - j)
    q_of = (qlo_ref.at[0, 0, 0], qlo_ref.at[0, 1, 0], qhi_ref.at[0, 0, 0], qhi_ref.at[0, 1, 0])
    g_of = (glo_ref.at[0, 0, 0], glo_ref.at[0, 1, 0], ghi_ref.at[0, 0, 0], ghi_ref.at[0, 1, 0])
    o_of = (olo_ref.at[0, 0, 0], olo_ref.at[0, 1, 0], ohi_ref.at[0, 1, 0], ohi_ref.at[0, 0, 0])
    row_q = lax.broadcasted_iota(jnp.int32, (rows, LANES), 0) % nq
    rq_minus_lane = row_q - lax.broadcasted_iota(jnp.int32, (rows, LANES), 1)

    def col_blocks(s):
        return [s[:, j * LANES:(j + 1) * LANES] for j in range(s.shape[1] // LANES)]

    def col_max(cols):
        m = cols[0]
        for c_ in cols[1:]:
            m = jnp.maximum(m, c_)
        return m

    def finish(acc):
        return acc[:, 0:KV_WIDTH] / jnp.maximum(acc[:, KV_WIDTH:2 * KV_WIDTH], 1e-30)

    def with_ones(v):
        return jnp.concatenate([v, jnp.ones((v.shape[0], LANES), BF16)], axis=1)

    q_onehot = jnp.where(rq_minus_lane == -BLK, NEG_INF, 0.0)

    def prepare(idx):
        ci = chunk_of[idx]
        q2 = _stack_queries(q_of[idx][...], nq)
        first = jnp.maximum(ci - WINDOW // BLK, 0) // (LANES // BLK)
        w0 = pl.multiple_of(first * LANES, LANES)
        delta = ci - first * (LANES // BLK)
        lhs = jnp.concatenate([q2, q_onehot.astype(BF16)], axis=1)
        rhs = jnp.concatenate([kw_ref[pl.ds(w0, wk), :], wm_ref[delta]], axis=1)
        cols = col_blocks(_dot_nt(lhs, rhs))
        yield
        q_pos = ci * BLK + lax.broadcasted_iota(jnp.int32, (rows, 1), 0) % nq
        n_ids = lax.broadcasted_iota(jnp.int32, (rows, nb), 1)
        s_c = _dot_nt(q2, kc_ref[0].astype(BF16))
        yield
        m_w = jnp.maximum(jnp.max(col_max(cols), axis=-1, keepdims=True), M_INIT)
        m_w = jnp.broadcast_to(m_w, (rows, LANES))
        p = jnp.concatenate([jnp.exp(c_ - m_w).astype(BF16) for c_ in cols], axis=1)
        yield
        p_c = _masked_softmax(s_c, (n_ids + 1) * BLK - 1 <= q_pos)
        yield
        o_w = finish(_dot(p, with_ones(vw_ref[pl.ds(w0, wk), :])))
        o_c = _dot(p_c.astype(BF16), vc_ref[0].astype(BF16))
        yield
        gated = _combine_start(g_of[idx][...], o_c, o_w, nq)
        yield

        imps = []
        for g in range(NSA_KV):
            imp = p_c[g * NSA_REP * nq:(g * NSA_REP + 1) * nq]
            for r in range(1, NSA_REP):
                imp = imp + p_c[(g * NSA_REP + r) * nq:(g * NSA_REP + r + 1) * nq]
            imps.append(imp)
        imp2 = jnp.concatenate([jnp.concatenate(imps, axis=0), jnp.zeros((LANES, LANES - nb), F32)], axis=1)
        imp_t = imp2.T[0:nb]
        yield
        sel_t = _select_blocks_t(imp_t, ci, nb)
        yield
        sel2 = jnp.concatenate([sel_t, jnp.ones((LANES - nb, LANES), F32)], axis=0).T
        neg = (sel2 - 1.0) * (-NEG_INF)
        neg_rows = jnp.concatenate([neg[g * nq:(g + 1) * nq] for g in range(NSA_KV) for _ in range(NSA_REP)],
                                   axis=0)
        q2_ref[idx] = jnp.concatenate([q2, (neg_rows + q_onehot).astype(BF16)], axis=1)
        in_diag = lax.broadcasted_iota(jnp.int32, (ch, LANES), 0) // BLK == ci % bpc
        etd_ref[idx] = et_ref[n_ch[idx] - 1] + jnp.where(in_diag, tri_ref[...], jnp.zeros((), BF16))
        return gated

    gated = _lockstep(*[prepare(idx) for idx in range(4)])

    units = [(idx, kc) for idx in range(4) for kc in range(n_ch[idx])]
    mx = [jnp.full((rows, LANES), M_INIT, F32)] * 4
    for u, (idx, kc) in enumerate(units):
        key_mask = etd_ref[idx] if kc == n_ch[idx] - 1 else et_ref[kc]
        rhs = jnp.concatenate([ks_ref[kc * ch:(kc + 1) * ch, :], key_mask], axis=1)
        cols = col_blocks(_dot_nt(q2_ref[idx], rhs))
        mx[idx] = jnp.maximum(mx[idx], col_max(cols))
        for c, c_ in enumerate(cols):
            s_ref[u, :, c * LANES:(c + 1) * LANES] = c_

    m = [jnp.broadcast_to(jnp.max(mx_i, axis=-1, keepdims=True), (rows, LANES)) for mx_i in mx]
    acc = [None] * 4
    for u, (idx, kc) in enumerate(units):
        p = jnp.concatenate([jnp.exp(s_ref[u, :, c * LANES:(c + 1) * LANES] - m[idx]).astype(BF16)
                             for c in range(ch // LANES)], axis=1)
        part = _dot(p, with_ones(vs_ref[kc * ch:(kc + 1) * ch, :]))
        acc[idx] = part if acc[idx] is None else acc[idx] + part

    for idx in range(4):
        o_of[idx][...] = _combine_finish(gated[idx], finish(acc[idx]), nq).astype(olo_ref.dtype)


def _nsa_prompt(q, gates, cmp_kv, kvb, *, n_seq, seq):
    nb = seq // BLK
    rows = NSA_HEADS * BLK
    ch = min(SEL_CHUNK, seq)
    wk = min(WINDOW + LANES, seq)
    bpc = ch // BLK
    assert seq % ch == 0 and wk % LANES == 0 and nb % 4 == 0 and nb <= LANES
    quarter = nb // 4
    n_ch = tuple(k * quarter // bpc + 1 for k in range(4))
    assert all(((k + 1) * quarter - 1) // bpc + 1 == n_ch[k] for k in range(4))
    n_units = sum(n_ch)
    assert nb <= BLK
    et = np.zeros((seq // ch, ch, LANES), np.float32)
    for c in range(seq // ch):
        et[c, np.arange(ch), (c * ch + np.arange(ch)) // BLK] = 1.0
    key = np.arange(ch)[:, None]
    qry = np.arange(BLK)[None, :]
    tri = np.zeros((ch, LANES), np.float32)
    tri[:, BLK:] = (key % BLK) > qry
    n_delta = WINDOW // BLK + LANES // BLK
    wm = np.zeros((n_delta, wk, LANES), np.float32)
    for dl in range(n_delta):
        dist = dl * BLK + qry - np.arange(wk)[:, None]
        wm[dl, :, BLK:] = (dist < 0) | (dist >= WINDOW)
    lower = lambda w_: pl.BlockSpec((1, 2, 1, BLK, w_), lambda b, j: (b, 0, j, 0, 0))
    upper = lambda w_: pl.BlockSpec((1, 2, 1, BLK, w_), lambda b, j: (b, 1, quarter - 1 - j, 0, 0))
    kv = lambda col: pl.BlockSpec((seq, KV_WIDTH), lambda b, j: (b, col))
    cmp_ = lambda which: pl.BlockSpec((1, nb, KV_WIDTH), lambda b, j: (which, b, 0))
    const = lambda a: pl.BlockSpec(a.shape, lambda b, j: (0,) * a.ndim)
    out = jax.ShapeDtypeStruct((n_seq, 2, quarter, BLK, NSA_WIDTH), BF16)
    q5 = q.reshape(n_seq, 4, quarter, BLK, NSA_WIDTH)
    g5 = gates.reshape(n_seq, 4, quarter, BLK, LANES)
    o_lo, o_hi = pl.pallas_call(
        functools.partial(_nsa_quad_kernel, nb=nb, ch=ch, wk=wk, n_ch=n_ch),
        grid=(n_seq, quarter),
        in_specs=[lower(NSA_WIDTH), upper(NSA_WIDTH), lower(LANES), upper(LANES),
                  cmp_(0), cmp_(1), kv(2), kv(3), kv(4), kv(5), const(et), const(tri), const(wm)],
        out_specs=[lower(NSA_WIDTH), lower(NSA_WIDTH)],
        out_shape=[out, out],
        scratch_shapes=[pltpu.VMEM((4, rows, 2 * LANES), BF16),
                        pltpu.VMEM((4, ch, LANES), BF16),
                        pltpu.VMEM((n_units, rows, ch), F32)],
        compiler_params=_params("parallel", "arbitrary"),
        name="nsa_prompt",
    )(q5, q5, g5, g5, cmp_kv, cmp_kv, kvb, kvb, kvb, kvb,
      jnp.asarray(et, dtype=BF16), jnp.asarray(tri, dtype=BF16), jnp.asarray(wm, dtype=BF16))
    return o_lo.reshape(-1, NSA_WIDTH), o_hi.reshape(-1, NSA_WIDTH)


def _unmirror(o_a, o_b, n_seq):
    half = o_a.shape[0] // (n_seq * BLK)
    o_a = o_a.reshape(n_seq, half, BLK, NSA_WIDTH)
    o_b = o_b.reshape(n_seq, half, BLK, NSA_WIDTH)[:, ::-1]
    return jnp.concatenate([o_a, o_b], axis=1).reshape(-1, NSA_WIDTH)


def _nsa_sample_kernel(pt_ref, q_ref, gate_ref, kc_ref, vc_ref, cache_ref, nks_ref, nvs_ref,
                       wt_ref, nkw_ref, nvw_ref, e_ref, o_ref, kv_buf, s_scr, sems,
                       *, nq, past, nbp, tk):
    step = pl.program_id(0)
    sb = q_ref.shape[0]
    n_pages = past // PAGE_SIZE
    rows = NSA_HEADS * nq

    def page_copy(t, k, j):
        buf = (t % 2) * sb + k
        k0 = pl.multiple_of(j * PAGE_SIZE, PAGE_SIZE)
        return pltpu.make_async_copy(cache_ref.at[pt_ref[(t * sb + k) * n_pages + j], pl.ds(2, 2)],
                                     kv_buf.at[pl.ds(2 * buf, 2), :, pl.ds(k0, PAGE_SIZE)], sems.at[buf])

    def start_all(t):
        for k in range(sb):
            def start(j, _, k=k):
                page_copy(t, k, j).start()
                return 0
            lax.fori_loop(0, n_pages, start, 0, unroll=8)

    @pl.when(step == 0)
    def _():
        start_all(step)

    @pl.when(step + 1 < pl.num_programs(0))
    def _():
        start_all(step + 1)

    _lockstep(*[_nsa_sample_sequence(k, step, page_copy, q_ref, gate_ref, kc_ref, vc_ref, nks_ref, nvs_ref,
                                     wt_ref, nkw_ref, nvw_ref, e_ref, o_ref, kv_buf, s_scr,
                                     nq=nq, past=past, nbp=nbp, tk=tk) for k in range(sb)])


def _nsa_sample_sequence(k, step, page_copy, q_ref, gate_ref, kc_ref, vc_ref, nks_ref, nvs_ref,
                         wt_ref, nkw_ref, nvw_ref, e_ref, o_ref, kv_buf, s_scr, *, nq, past, nbp, tk):
    sb = q_ref.shape[0]
    n_pages = past // PAGE_SIZE
    rows = NSA_HEADS * nq
    buf = (step % 2) * sb + k
    q2 = _stack_queries(q_ref[k], nq)
    q_pos = past + lax.broadcasted_iota(jnp.int32, (rows, 1), 0) % nq
    cur = past // BLK

    n_ids = lax.broadcasted_iota(jnp.int32, (rows, nbp), 1)
    s_c = _dot_nt(q2, kc_ref[k].astype(BF16))
    yield
    p_c = _masked_softmax(s_c, (n_ids + 1) * BLK - 1 <= q_pos)
    o_c = _dot(p_c.astype(BF16), vc_ref[k].astype(BF16))
    yield

    n_sel = lax.broadcasted_iota(jnp.int32, (nq, nbp), 1)
    sels = []
    for g in range(NSA_KV):
        imp = p_c[g * NSA_REP * nq:(g * NSA_REP + 1) * nq]
        for r in range(1, NSA_REP):
            imp = imp + p_c[(g * NSA_REP + r) * nq:(g * NSA_REP + r + 1) * nq]
        sel = _select_blocks(imp, n_sel, cur, cur + 1)
        sels.extend([sel] * NSA_REP)
        yield
    keys = _dot(jnp.concatenate(sels, axis=0).astype(BF16), e_ref[...])
    bias = (keys - 1.0) * (-NEG_INF)
    yield

    def col_max(s):
        m = s[:, 0:LANES]
        for j in range(1, s.shape[1] // LANES):
            m = jnp.maximum(m, s[:, j * LANES:(j + 1) * LANES])
        return m

    def col_sum(p):
        t = p[:, 0:LANES]
        for j in range(1, p.shape[1] // LANES):
            t = t + p[:, j * LANES:(j + 1) * LANES]
        return t

    def row_max(mx):
        return jnp.broadcast_to(jnp.max(mx, axis=-1, keepdims=True), mx.shape)

    def tiled(m, width):
        return m if width == LANES else jnp.concatenate([m] * (width // LANES), axis=1)

    new_pos = past + lax.broadcasted_iota(jnp.int32, (rows, LANES), 1)

    wlen = wt_ref.shape[3]
    w_pos = past - wlen + lax.broadcasted_iota(jnp.int32, (rows, wlen), 1)
    d = q_pos - w_pos
    s_w = _dot(q2, wt_ref[k, 0].astype(BF16))
    s_w = jnp.where((d >= 0) & (d < WINDOW) & (w_pos >= 0), s_w, NEG_INF)
    d = q_pos - new_pos
    s_nw = _dot_nt(q2, nkw_ref[k].astype(BF16))
    s_nw = jnp.where((d >= 0) & (d < WINDOW), s_nw, NEG_INF)
    yield
    m_w = row_max(jnp.maximum(jnp.maximum(col_max(s_w), s_nw), M_INIT))
    p_w = jnp.exp(s_w - tiled(m_w, wlen))
    p_nw = jnp.exp(s_nw - m_w)
    acc = _dot_nt(p_w.astype(BF16), wt_ref[k, 1].astype(BF16)) + _dot(p_nw.astype(BF16), nvw_ref[k].astype(BF16))
    o_w = acc / jnp.maximum(jnp.sum(col_sum(p_w) + p_nw, axis=-1, keepdims=True), 1e-30)
    yield

    def wait(j, _):
        page_copy(step, k, j).wait()
        return 0

    lax.fori_loop(0, n_pages, wait, 0, unroll=8)

    mx = jnp.full((rows, LANES), M_INIT, F32)
    for t in range(past // tk):
        s = _dot(q2, kv_buf[2 * buf, :, t * tk:(t + 1) * tk].astype(BF16)) + bias[:, t * tk:(t + 1) * tk]
        s_scr[k, :, t * tk:(t + 1) * tk] = s
        mx = jnp.maximum(mx, col_max(s))
        yield
    s_new = _dot_nt(q2, nks_ref[k].astype(BF16)) + bias[:, past:past + LANES]
    s_new = jnp.where(new_pos <= q_pos, s_new, NEG_INF)
    m_s = row_max(jnp.maximum(mx, s_new))
    p_new = jnp.exp(s_new - m_s)
    acc = _dot(p_new.astype(BF16), nvs_ref[k].astype(BF16))
    lsum = p_new
    yield
    for t in range(past // tk):
        p = jnp.exp(s_scr[k, :, t * tk:(t + 1) * tk] - tiled(m_s, tk))
        lsum = lsum + col_sum(p)
        acc = acc + _dot_nt(p.astype(BF16), kv_buf[2 * buf + 1, :, t * tk:(t + 1) * tk].astype(BF16))
        yield
    o_s = acc / jnp.maximum(jnp.sum(lsum, axis=-1, keepdims=True), 1e-30)

    o_ref[k] = _combine_heads(gate_ref[k], o_c, o_s, o_w, nq).astype(o_ref.dtype)


def _nsa_sample(pt_flat, q3, gates3, kc, vc, cache_t, new_rows, win_t, expand, *, past, tk, seqs_per_step):
    n_seq, nq, _ = q3.shape
    nbp = kc.shape[1]
    rows = NSA_HEADS * nq
    sb = seqs_per_step
    per_seq = lambda a: pl.BlockSpec((sb,) + a.shape[1:], lambda b, pt: (b,) + (0,) * (a.ndim - 1))
    new = lambda col: pl.BlockSpec((sb, LANES, KV_WIDTH), lambda b, pt: (b, 0, col))
    grid_spec = pltpu.PrefetchScalarGridSpec(
        num_scalar_prefetch=1,
        grid=(n_seq // sb,),
        in_specs=[per_seq(q3), per_seq(gates3), per_seq(kc), per_seq(vc),
                  pl.BlockSpec(memory_space=pl.ANY),
                  new(2), new(3), per_seq(win_t), new(4), new(5),
                  pl.BlockSpec(expand.shape, lambda b, pt: (0, 0), pipeline_mode=pl.Buffered(1))],
        out_specs=pl.BlockSpec((sb, nq, NSA_WIDTH), lambda b, pt: (b, 0, 0)),
        scratch_shapes=[pltpu.VMEM((4 * sb, KV_WIDTH, past), F32),
                        pltpu.VMEM((sb, rows, past), F32), pltpu.SemaphoreType.DMA((2 * sb,))])
    return pl.pallas_call(
        functools.partial(_nsa_sample_kernel, nq=nq, past=past, nbp=nbp, tk=tk),
        grid_spec=grid_spec,
        out_shape=jax.ShapeDtypeStruct((n_seq, nq, NSA_WIDTH), F32),
        compiler_params=_params("arbitrary"),
        name="nsa_sample",
    )(pt_flat, q3, gates3, kc, vc, cache_t, new_rows, new_rows, win_t, new_rows, new_rows, expand)


def _expand_matrix(nb, n_keys):
    return jnp.asarray(np.arange(n_keys)[None, :] // BLK == np.arange(nb)[:, None], dtype=BF16)


def _merge_kernel(x_ref, ssm_ref, *refs, mirrored):
    nsa_refs = refs[:2] if mirrored else refs[:1]
    gpre_ref, wm_ref, wbs_ref, wbn_ref, wo_ref, gpost_ref, gx_ref, wxq_ref, x1_ref, qx_ref = refs[len(nsa_refs):]
    x = x_ref[...]
    tm = x.shape[0]
    if mirrored:
        upper = nsa_refs[1][...]
        n_chunks = tm // BLK
        upper = jnp.concatenate([upper[(n_chunks - 1 - s) * BLK:(n_chunks - s) * BLK] for s in range(n_chunks)],
                                axis=0)
        nsa = jnp.where(pl.program_id(0) < pl.num_programs(0) // 2, nsa_refs[0][...], upper)
    else:
        nsa = nsa_refs[0][...]
    n_seq = ssm_ref.shape[1] // tm
    ssm = jnp.concatenate([ssm_ref[s, pl.ds(pl.program_id(1), tm, stride=n_seq), :]
                           for s in range(SSM_WIDTH // LANES)], axis=1)
    a = _rms(x, gpre_ref[...]).astype(BF16)
    g_ssm = _sigmoid(_dot(a, wm_ref[:, 0:D_MODEL]))
    g_nsa = _sigmoid(_dot(a, wm_ref[:, D_MODEL:2 * D_MODEL]))
    merged = (g_ssm * _dot(ssm.astype(BF16), wbs_ref[...])
              + g_nsa * _dot(nsa, wbn_ref[...]))
    x1 = x + _rms(_dot(merged.astype(BF16), wo_ref[...]), gpost_ref[...])
    x1_ref[...] = x1
    c = _rms(x1, gx_ref[...]).astype(BF16)
    qx_ref[...] = (_dot(c, wxq_ref[...]) * (X_HEAD_DIM ** -0.5)).astype(BF16)


def _merge(x2d, ssm_slabs, nsa_o, weights, *, tm, n_tiles):
    n = x2d.shape[0]
    n_b = n // (n_tiles * tm)
    row = lambda w_: pl.BlockSpec((tm, w_), lambda t, b: (b * n_tiles + t, 0))
    full = lambda a: pl.BlockSpec(a.shape, lambda t, b: (0,) * a.ndim, pipeline_mode=pl.Buffered(1))
    ssm_spec = pl.BlockSpec((SSM_WIDTH // LANES, tm * n_b, LANES), lambda t, b: (0, t, 0))
    mirrored = isinstance(nsa_o, tuple)
    if mirrored:
        half = n_tiles // 2
        nsa_specs = [pl.BlockSpec((tm, NSA_WIDTH), lambda t, b: (b * half + jnp.minimum(t, half - 1), 0)),
                     pl.BlockSpec((tm, NSA_WIDTH), lambda t, b: (b * half + jnp.minimum(n_tiles - 1 - t, half - 1), 0))]
        nsa_args = list(nsa_o)
    else:
        nsa_specs, nsa_args = [row(NSA_WIDTH)], [nsa_o]
    return pl.pallas_call(
        functools.partial(_merge_kernel, mirrored=mirrored),
        grid=(n_tiles, n_b),
        in_specs=[row(D_MODEL), ssm_spec] + nsa_specs + [full(w) for w in weights],
        out_specs=[row(D_MODEL), row(X_WIDTH)],
        out_shape=[jax.ShapeDtypeStruct((n, D_MODEL), F32), jax.ShapeDtypeStruct((n, X_WIDTH), BF16)],
        compiler_params=_params("parallel", "arbitrary"),
        name="merge",
    )(x2d, ssm_slabs, *nsa_args, *weights)


def _xattn_tile(q, kv_ref, m_len):
    outs = []
    for h in range(X_HEADS):
        cols = slice(h * X_HEAD_DIM, (h + 1) * X_HEAD_DIM)
        k = kv_ref[0, pl.ds(h, m_len, stride=2 * X_HEADS), :].astype(BF16)
        v = kv_ref[0, pl.ds(X_HEADS + h, m_len, stride=2 * X_HEADS), :].astype(BF16)
        s = _dot_nt(q[:, cols], k)
        m = jnp.max(s, axis=-1, keepdims=True)
        p = jnp.exp(s - m)
        p = p / jnp.sum(p, axis=-1, keepdims=True)
        outs.append(_dot(p.astype(BF16), v))
    return jnp.concatenate(outs, axis=1)


def _xattn_kernel(q_ref, kv_ref, o_ref, *, m_len):
    t = q_ref.shape[1]
    n_rows = 2 * X_HEADS
    for s in range(q_ref.shape[0]):
        q = q_ref[s].astype(F32)
        z = jnp.zeros((t, X_HEAD_DIM), F32)
        q_bd = jnp.concatenate(
            [jnp.concatenate([q[:, h * X_HEAD_DIM:(h + 1) * X_HEAD_DIM] if j == h else z for j in range(X_HEADS)],
                             axis=1) for h in range(X_HEADS)], axis=0).astype(BF16)
        k = jnp.concatenate([kv_ref[s, pl.ds(h, m_len, stride=n_rows), :] for h in range(X_HEADS)],
                            axis=1).astype(BF16)
        v = jnp.concatenate([kv_ref[s, pl.ds(X_HEADS + h, m_len, stride=n_rows), :] for h in range(X_HEADS)],
                            axis=1).astype(BF16)
        sc = _dot_nt(q_bd, k)
        p = jnp.exp(sc - jnp.max(sc, axis=-1, keepdims=True))
        p = p / jnp.sum(p, axis=-1, keepdims=True)
        o = _dot(p.astype(BF16), v)
        o_ref[s] = jnp.concatenate([o[h * t:(h + 1) * t, h * X_HEAD_DIM:(h + 1) * X_HEAD_DIM]
                                    for h in range(X_HEADS)], axis=1).astype(o_ref.dtype)


def _xattn(q3, mem_kv_rows, *, seqs_per_step):
    n_seq, t, _ = q3.shape
    m_len = mem_kv_rows.shape[1] // (2 * X_HEADS)
    sb = seqs_per_step
    return pl.pallas_call(
        functools.partial(_xattn_kernel, m_len=m_len),
        grid=(n_seq // sb,),
        in_specs=[pl.BlockSpec((sb, t, X_WIDTH), lambda b: (b, 0, 0)),
                  pl.BlockSpec((sb, m_len * 2 * X_HEADS, X_HEAD_DIM), lambda b: (b, 0, 0))],
        out_specs=pl.BlockSpec((sb, t, X_WIDTH), lambda b: (b, 0, 0)),
        out_shape=jax.ShapeDtypeStruct((n_seq, t, X_WIDTH), q3.dtype),
        compiler_params=_params("parallel"),
        name="xattn",
    )(q3, mem_kv_rows)


def _mlp_kernel(x1_ref, o_ref, *refs, m_len):
    if m_len:
        kv_ref, refs = refs[0], refs[1:]
        o = _xattn_tile(o_ref[...], kv_ref, m_len).astype(BF16)
    else:
        o = o_ref[...]
    wxo_ref, gxp_ref, gm_ref, wup_ref, wdn_ref, gmp_ref, y_ref = refs
    x2 = x1_ref[...] + _rms(_dot(o, wxo_ref[...]), gxp_ref[...])
    m = _rms(x2, gm_ref[...]).astype(BF16)
    hid = jnp.maximum(_dot(m, wup_ref[...]), 0.0)
    hid = (hid * hid).astype(BF16)
    y_ref[...] = x2 + _rms(_dot(hid, wdn_ref[...]), gmp_ref[...])


def _mlp(x1, o, weights, *, tm, mem_kv_rows=None, tiles_per_seq=1):
    n = x1.shape[0]
    row = lambda w_: pl.BlockSpec((tm, w_), lambda i: (i, 0))
    full = lambda a: pl.BlockSpec(a.shape, lambda i: (0,) * a.ndim, pipeline_mode=pl.Buffered(1))
    m_len, kv_specs, kv_args = 0, [], []
    if mem_kv_rows is not None:
        m_len = mem_kv_rows.shape[1] // (2 * X_HEADS)
        kv_specs = [pl.BlockSpec((1,) + mem_kv_rows.shape[1:], lambda i: (i // tiles_per_seq, 0, 0))]
        kv_args = [mem_kv_rows]
    return pl.pallas_call(
        functools.partial(_mlp_kernel, m_len=m_len),
        grid=(n // tm,),
        in_specs=[row(D_MODEL), row(X_WIDTH)] + kv_specs + [full(w) for w in weights],
        out_specs=row(D_MODEL),
        out_shape=jax.ShapeDtypeStruct((n, D_MODEL), F32),
        compiler_params=_params("parallel"),
        name="mlp",
    )(x1, o, *kv_args, *weights)


def _memkv_kernel(m_ref, g_ref, w_ref, o_ref, *, tm):
    kv = _dot(_rms(m_ref[...], g_ref[...]).astype(BF16), w_ref[...])
    n_rows = 2 * X_HEADS
    for j in range(n_rows):
        o_ref[pl.ds(j, tm, stride=n_rows), :] = kv[:, j * X_HEAD_DIM:(j + 1) * X_HEAD_DIM]


def _memkv(mem2d, g, w, *, tm):
    n = mem2d.shape[0]
    n_rows = 2 * X_HEADS
    return pl.pallas_call(
        functools.partial(_memkv_kernel, tm=tm),
        grid=(n // tm,),
        in_specs=[pl.BlockSpec((tm, D_MODEL), lambda i: (i, 0)),
                  pl.BlockSpec(g.shape, lambda i: (0, 0)), pl.BlockSpec(w.shape, lambda i: (0, 0))],
        out_specs=pl.BlockSpec((tm * n_rows, X_HEAD_DIM), lambda i: (i, 0)),
        out_shape=jax.ShapeDtypeStruct((n * n_rows, X_HEAD_DIM), F32),
        compiler_params=_params("parallel"),
        name="memkv",
    )(mem2d, g, w)


def _row(v):
    return v.astype(F32).reshape(1, -1)


def kernel(x_prompt, x_sample, cache_nsa_kv, cache_win_kv, state_ssm, cache_mem_kv, page_table, mem_prompt, g_mix_pre, w_in, ssm_lam_re, ssm_lam_im, ssm_log_dt, ssm_b_re, ssm_b_im, ssm_c_re, ssm_c_im, ssm_d, w_glu, b_glu, cmp_pe_k, w_cmpk1, w_cmpk2, cmp_pe_v, w_cmpv1, w_cmpv2, w_br_ssm, w_br_nsa, w_out, g_mix_post, g_x_pre, g_mem, w_xq, w_xk, w_xv, w_xo, g_x_post, g_mlp_pre, w_up, w_down, g_mlp_post):
    depth = w_in.shape[0]
    n_seq_p, seq, _ = x_prompt.shape
    n_seq_s, nq, _ = x_sample.shape
    past = page_table.shape[1] * PAGE_SIZE
    assert depth == 1 and seq % BLK == 0 and nq <= SUBLANES and past % BLK == 0
    assert n_seq_p == SUBLANES and n_seq_s % SUBLANES == 0

    y_p = x_prompt.reshape(n_seq_p * seq, D_MODEL)
    y_s = x_sample.reshape(n_seq_s * nq, D_MODEL)
    l = 0

    w_proj = w_in[l, :, :N_PROJ].astype(BF16)
    w_gate = jnp.pad(w_in[l, :, N_PROJ:N_PROJ + N_GATE], ((0, 0), (0, LANES - N_GATE))).astype(BF16)
    w_merge = w_in[l, :, N_PROJ + N_GATE:].astype(BF16)
    lam_l, bm, cm = _ssm_params(ssm_lam_re[l], ssm_lam_im[l], ssm_log_dt[l], ssm_b_re[l], ssm_b_im[l],
                                ssm_c_re[l], ssm_c_im[l])
    ssm_w = (lam_l, bm, cm, _row(ssm_d[l]), w_glu[l].astype(BF16), _row(b_glu[l]))
    cmp_w = _compress_params(cmp_pe_k[l], w_cmpk1[l], w_cmpk2[l], cmp_pe_v[l], w_cmpv1[l], w_cmpv2[l])
    merge_w = (_row(g_mix_pre[l]), w_merge, w_br_ssm[l].astype(BF16), w_br_nsa[l].astype(BF16),
               w_out[l].astype(BF16), _row(g_mix_post[l]), _row(g_x_pre[l]), w_xq[l].astype(BF16))
    mlp_w = (w_xo[l].astype(BF16), _row(g_x_post[l]), _row(g_mlp_pre[l]), w_up[l].astype(BF16),
             w_down[l].astype(BF16), _row(g_mlp_post[l]))
    w_mem = jnp.concatenate([w_xk[l], w_xv[l]], axis=1).astype(BF16)

    tm_p = 512 if seq % 512 == 0 else seq
    nt_p = seq // tm_p
    tabs_p = _rope_tables(np.arange(seq))
    u_p, q_p, kvt_p, wint_p, kvb_p, gate_p = _proj(
        y_p, _row(g_mix_pre[l]), w_proj, w_gate, tabs_p, tm=tm_p, n_tab_blocks=nt_p, prompt=True)
    h0_p = jnp.zeros((1, SUBLANES, 2 * N_STATE), F32)
    tc_p = 64 if seq % 64 == 0 else seq
    ssm_p, hl_p = _ssm(u_p, h0_p, *ssm_w, n_groups=1, n_time=seq, tc=tc_p)

    assert seq % PAGE_SIZE == 0
    n_pages_p = n_seq_p * seq // PAGE_SIZE
    pe_t, w1_t = _compress_paged_params(cmp_pe_k[l], w_cmpk1[l], cmp_pe_v[l], w_cmpv1[l])
    cmp_p = _compress_paged(jnp.zeros((n_pages_p,), jnp.int32), kvt_p, pe_t, w1_t, cmp_w[2],
                            m=min(n_pages_p, 128), pages_per_seq=seq // PAGE_SIZE)
    nsa_p = _nsa_prompt(q_p, gate_p, cmp_p, kvb_p, n_seq=n_seq_p, seq=seq)

    mem_kv_p = _memkv(mem_prompt.reshape(-1, D_MODEL), _row(g_mem[l]), w_mem, tm=256)
    m_len = mem_prompt.shape[1]
    mem_kv_p3 = mem_kv_p.reshape(n_seq_p, m_len * 2 * X_HEADS, X_HEAD_DIM)

    if nt_p % 2:
        nsa_p = _unmirror(*nsa_p, n_seq_p)
    x1_p, qx_p = _merge(y_p, ssm_p, nsa_p, merge_w, tm=tm_p, n_tiles=nt_p)
    y_p = _mlp(x1_p, qx_p, mlp_w, tm=tm_p, mem_kv_rows=mem_kv_p3, tiles_per_seq=nt_p)

    n_s = n_seq_s * nq
    tabs_s = tuple(np.tile(t, (n_seq_s, 1)) for t in _rope_tables(past + np.arange(nq)))
    u_s, q_s, kv_s, win_s, kvb_s, gate_s = _proj(y_s, _row(g_mix_pre[l]), w_proj, w_gate, tabs_s,
                                                 tm=n_s, n_tab_blocks=1, prompt=False)
    n_grp = n_seq_s // SUBLANES
    n_slabs = SSM_WIDTH // LANES
    u_s = (u_s.reshape(n_grp, SUBLANES, nq, n_slabs, LANES).transpose(3, 0, 2, 1, 4)
           .reshape(n_slabs, n_s, LANES))
    st = state_ssm[l].astype(F32).reshape(n_seq_s, N_STATE, 2)
    h0_s = _state_lanes(st[..., 0], st[..., 1]).reshape(n_grp, SUBLANES, 2 * N_STATE)
    ssm_s, hl_s = _ssm(u_s, h0_s, *ssm_w, n_groups=n_grp, n_time=nq, tc=nq)
    ssm_s = (ssm_s.reshape(n_slabs, n_grp, nq, SUBLANES, LANES).transpose(0, 1, 3, 2, 4)
             .reshape(n_slabs, n_s, LANES))

    n_pages = page_table.shape[1]
    n_pool = cache_nsa_kv.shape[1]
    cache_t = jnp.transpose(cache_nsa_kv[l], (0, 2, 3, 4, 1)).reshape(n_pool, 4, KV_WIDTH, PAGE_SIZE)
    win_t = jnp.transpose(cache_win_kv[l], (0, 2, 3, 4, 1)).reshape(n_seq_s, 2, KV_WIDTH, -1)
    pt_flat = page_table.reshape(-1).astype(jnp.int32)
    nb_past = past // BLK
    cmp_pages = _compress_paged(pt_flat, cache_t, pe_t, w1_t, cmp_w[2], m=min(n_seq_s * n_pages, 128))
    cmp_past = cmp_pages
    new_rows = jnp.pad(jnp.concatenate([kv_s, win_s], axis=1).reshape(n_seq_s, nq, 6 * KV_WIDTH),
                       ((0, 0), (0, LANES - nq), (0, 0)))
    cmp_new = _compress(new_rows[:, :BLK].reshape(n_seq_s * BLK, 6 * KV_WIDTH), *cmp_w,
                        n_blocks=n_seq_s, tm=n_seq_s)
    nbp = -(-(nb_past + 1) // LANES) * LANES
    cmp_s = jnp.concatenate([cmp_past.reshape(2, n_seq_s, nb_past, KV_WIDTH), cmp_new[:, :, None, :],
                             jnp.zeros((2, n_seq_s, nbp - nb_past - 1, KV_WIDTH), F32)], axis=2)
    nsa_s = _nsa_sample(pt_flat, q_s.astype(F32).reshape(n_seq_s, nq, NSA_WIDTH),
                        gate_s.reshape(n_seq_s, nq, LANES), cmp_s[0], cmp_s[1], cache_t, new_rows, win_t,
                        _expand_matrix(nbp, past + LANES), past=past, tk=min(past, 1024), seqs_per_step=2)

    x1_s, qx_s = _merge(y_s, ssm_s, nsa_s.reshape(n_s, NSA_WIDTH).astype(BF16), merge_w, tm=n_s, n_tiles=1)
    mem_kv_s3 = cache_mem_kv[l].reshape(n_seq_s, m_len * 2 * X_HEADS, X_HEAD_DIM)
    o_s = _xattn(qx_s.astype(F32).reshape(n_seq_s, nq, X_WIDTH), mem_kv_s3, seqs_per_step=SUBLANES)
    y_s = _mlp(x1_s, o_s.reshape(-1, X_WIDTH).astype(BF16), mlp_w, tm=n_s)

    def ssm_state(hl, n_seq):
        re, im = _state_unlanes(hl.reshape(n_seq, 2 * N_STATE))
        return jnp.stack([re, im], axis=-1).reshape(1, n_seq, SSM_GROUPS, SSM_STATE, 2)

    def token_major(xt):
        n_seq, parts, _, t_len = xt.shape
        return xt.reshape(1, n_seq, parts, NSA_KV, HEAD_DIM, t_len).transpose(0, 1, 5, 2, 3, 4)

    w_keep = min(WINDOW, seq)
    win_new = win_s.reshape(n_seq_s, nq, 2, NSA_KV, HEAD_DIM).astype(cache_win_kv.dtype)
    win_sample = jnp.concatenate([cache_win_kv[l], win_new], axis=1)[:, nq:]
    return (y_p.reshape(n_seq_p, seq, D_MODEL),
            y_s.reshape(n_seq_s, nq, D_MODEL),
            token_major(kvt_p),
            kv_s.reshape(1, n_seq_s, nq, 4, NSA_KV, HEAD_DIM),
            token_major(wint_p[:, :, :, seq - w_keep:]),
            win_sample[None],
            ssm_state(hl_p, n_seq_p),
            ssm_state(hl_s, n_seq_s),
            mem_kv_p.reshape(1, n_seq_p, m_len, 2, X_HEADS, X_HEAD_DIM))
```

```python
import functools
import math

import jax
import jax.numpy as jnp
import numpy as np
from jax import lax
from jax.experimental import pallas as pl
from jax.experimental.pallas import tpu as pltpu

F32 = jnp.float32
BF16 = jnp.bfloat16

D_MODEL = 1024
SSM_WIDTH = 512
SSM_GROUP = 16
SSM_GROUPS = 32
SSM_STATE = 64
N_STATE = SSM_GROUPS * SSM_STATE
STATE_CHUNK = 512
N_CHUNKS = N_STATE // STATE_CHUNK
NSA_HEADS = 8
HEAD_DIM = 64
NSA_WIDTH = NSA_HEADS * HEAD_DIM
NSA_KV = 2
NSA_REP = NSA_HEADS // NSA_KV
KV_WIDTH = NSA_KV * HEAD_DIM
BLK = 64
N_SEL = 16
WINDOW = 512
ROT_DIM = 16
ROPE_THETA = 500000.0
PAGE_SIZE = 128
X_HEADS = 4
X_HEAD_DIM = 128
X_WIDTH = X_HEADS * X_HEAD_DIM
D_FF = 4 * D_MODEL
EPS = 1e-6
NEG_INF = -1e30
M_INIT = -1e29
FORCE_SCORE = 1e4
LANES = 128
SUBLANES = 8
VMEM_LIMIT = 56 * 1024 * 1024

N_PROJ = SSM_WIDTH + NSA_WIDTH + 6 * KV_WIDTH
N_GATE = 3 * NSA_HEADS
SEL_CHUNK = 512
PAGE_PITCH = PAGE_SIZE + SUBLANES


def _params(*sem):
    return pltpu.CompilerParams(dimension_semantics=sem, vmem_limit_bytes=VMEM_LIMIT)


def _rms(x, g):
    return x * lax.rsqrt(jnp.mean(x * x, axis=-1, keepdims=True) + EPS) * g


def _gelu(x):
    return 0.5 * x * (1.0 + jnp.tanh(math.sqrt(2.0 / math.pi) * (x + 0.044715 * (x * x * x))))


def _sigmoid(x):
    return 1.0 / (1.0 + jnp.exp(-x))


def _dot(a, b):
    return jnp.dot(a, b, preferred_element_type=F32)


def _dot_nt(a, b):
    return lax.dot_general(a, b, (((1,), (1,)), ((), ())), preferred_element_type=F32)


def _lockstep(*stagewise):
    results = [None] * len(stagewise)
    live = list(range(len(stagewise)))
    while live:
        for k in list(live):
            try:
                next(stagewise[k])
            except StopIteration as done:
                results[k] = done.value
                live.remove(k)
    return results


def _proj_kernel(x_ref, g_ref, w_ref, wg_ref, cos_ref, sp_ref, sm_ref,
                 u_ref, q_ref, kv_ref, win_ref, kvb_ref, gate_ref, *, token_minor):
    a = _rms(x_ref[...], g_ref[...]).astype(BF16)
    cos, sp, sm = cos_ref[...], sp_ref[...], sm_ref[...]

    def rope(blk):
        return blk * cos + pltpu.roll(blk, 8, 1) * sp + pltpu.roll(blk, LANES - 8, 1) * sm

    pairs = {}

    def lane_block(j):
        if j // 2 not in pairs:
            pairs[j // 2] = _dot(a, w_ref[:, (j // 2) * 2 * LANES:(j // 2 + 1) * 2 * LANES])
        return pairs[j // 2][:, (j % 2) * LANES:(j % 2 + 1) * LANES]

    if token_minor:
        tm = x_ref.shape[0]
        n_seq = u_ref.shape[1] // tm
        for s in range(SSM_WIDTH // LANES):
            u_ref[s, pl.ds(pl.program_id(1), tm, stride=n_seq), :] = lane_block(s)
    else:
        u_ref[...] = _dot(a, w_ref[:, 0:SSM_WIDTH])
    for j in range(NSA_WIDTH // LANES):
        q_ref[:, j * LANES:(j + 1) * LANES] = rope(lane_block(SSM_WIDTH // LANES + j)).astype(BF16)
    for j in range(6):
        blk = lane_block((SSM_WIDTH + NSA_WIDTH) // LANES + j)
        if j % 2 == 0:
            blk = rope(blk)
        out_ref, part = (kv_ref, j) if j < 4 else (win_ref, j - 4)
        if token_minor:
            out_ref[0, part] = blk.T
        else:
            out_ref[:, part * LANES:(part + 1) * LANES] = blk
        kvb_ref[:, j * LANES:(j + 1) * LANES] = blk.astype(BF16)
    gate_ref[...] = _sigmoid(_dot(a, wg_ref[...]))


def _proj(x2d, g, w, wg, tabs, *, tm, n_tab_blocks, prompt):
    n = x2d.shape[0]
    nt = n_tab_blocks
    n_b = n // (nt * tm)
    row = lambda w_: pl.BlockSpec((tm, w_), lambda t, b: (b * nt + t, 0))
    full = lambda a: pl.BlockSpec(a.shape, lambda t, b: (0,) * a.ndim)
    tab = pl.BlockSpec((tm, LANES), lambda t, b: (t, 0))
    if prompt:
        t_len = nt * tm
        n_slabs = SSM_WIDTH // LANES
        u_shape = jax.ShapeDtypeStruct((n_slabs, t_len * n_b, LANES), F32)
        u_spec = pl.BlockSpec((n_slabs, tm * n_b, LANES), lambda t, b: (0, t, 0))
        tok_minor = lambda parts: pl.BlockSpec((1, parts, KV_WIDTH, tm), lambda t, b: (b, 0, 0, t))
        kv_specs = [tok_minor(4), tok_minor(2)]
        kv_shapes = [jax.ShapeDtypeStruct((n_b, 4, KV_WIDTH, t_len), F32),
                     jax.ShapeDtypeStruct((n_b, 2, KV_WIDTH, t_len), F32)]
    else:
        u_shape = jax.ShapeDtypeStruct((n, SSM_WIDTH), F32)
        u_spec = row(SSM_WIDTH)
        kv_specs = [row(4 * KV_WIDTH), row(2 * KV_WIDTH)]
        kv_shapes = [jax.ShapeDtypeStruct((n, 4 * KV_WIDTH), F32), jax.ShapeDtypeStruct((n, 2 * KV_WIDTH), F32)]
    return pl.pallas_call(
        functools.partial(_proj_kernel, token_minor=prompt),
        grid=(nt, n_b),
        in_specs=[row(D_MODEL), full(g), full(w), full(wg), tab, tab, tab],
        out_specs=[u_spec, row(NSA_WIDTH)] + kv_specs + [row(6 * KV_WIDTH), row(LANES)],
        out_shape=[u_shape, jax.ShapeDtypeStruct((n, NSA_WIDTH), BF16)] + kv_shapes
                  + [jax.ShapeDtypeStruct((n, 6 * KV_WIDTH), BF16), jax.ShapeDtypeStruct((n, LANES), F32)],
        compiler_params=_params("parallel", "arbitrary"),
        name="proj",
    )(x2d, g, w, wg, *tabs)


def _rope_tables(pos):
    half = ROT_DIM // 2
    freqs = ROPE_THETA ** (-np.arange(half, dtype=np.float64) / half)
    ang = np.asarray(pos, np.float64)[:, None] * freqs[None, :]
    cos, sin = np.cos(ang), np.sin(ang)
    r = ang.shape[0]
    z8 = np.zeros((r, half))
    rest0 = np.zeros((r, HEAD_DIM - ROT_DIM))
    rest1 = np.ones((r, HEAD_DIM - ROT_DIM))
    c64 = np.concatenate([cos, cos, rest1], axis=1)
    sp64 = np.concatenate([z8, sin, rest0], axis=1)
    sm64 = np.concatenate([-sin, z8, rest0], axis=1)
    return tuple(np.tile(t, (1, LANES // HEAD_DIM)).astype(np.float32) for t in (c64, sp64, sm64))


def _ssm_kernel(u_ref, h0_ref, lam_ref, bm_ref, cm_ref, d_ref, wglu_ref, bglu_ref,
                y_ref, hlast_ref, hs_ref, hstate_ref, *, tc):
    j = pl.program_id(1)

    @pl.when(j == 0)
    def _():
        hstate_ref[...] = h0_ref[0]

    n_slabs = SSM_WIDTH // LANES
    u = jnp.concatenate([u_ref[s] for s in range(n_slabs)], axis=1)
    ub = u.astype(BF16)
    half_in = SSM_WIDTH // 2
    chunk_lanes = 2 * STATE_CHUNK
    chunks_per_half = N_CHUNKS // 2
    ys = [None, None]
    for c in range(N_CHUNKS):
        h = c // chunks_per_half
        re0 = c * chunk_lanes
        im0 = re0 + STATE_CHUNK
        u_half = ub[:, h * half_in:(h + 1) * half_in]
        hs_ref[:, re0:re0 + STATE_CHUNK] = _dot(u_half, bm_ref[c, 0])
        hs_ref[:, im0:im0 + STATE_CHUNK] = _dot(u_half, bm_ref[c, 1])
        lr = jnp.broadcast_to(lam_ref[0:1, re0:re0 + STATE_CHUNK], (SUBLANES, STATE_CHUNK))
        li = jnp.broadcast_to(lam_ref[0:1, im0:im0 + STATE_CHUNK], (SUBLANES, STATE_CHUNK))
        hr = hstate_ref[:, re0:re0 + STATE_CHUNK]
        hi = hstate_ref[:, im0:im0 + STATE_CHUNK]
        for t in range(tc):
            r0 = t * SUBLANES
            hr, hi = (lr * hr - li * hi + hs_ref[r0:r0 + SUBLANES, re0:re0 + STATE_CHUNK],
                      lr * hi + li * hr + hs_ref[r0:r0 + SUBLANES, im0:im0 + STATE_CHUNK])
            hs_ref[r0:r0 + SUBLANES, re0:re0 + STATE_CHUNK] = hr
            hs_ref[r0:r0 + SUBLANES, im0:im0 + STATE_CHUNK] = hi
        hstate_ref[:, re0:re0 + STATE_CHUNK] = hr
        hstate_ref[:, im0:im0 + STATE_CHUNK] = hi
        part = (_dot(hs_ref[:, re0:re0 + STATE_CHUNK].astype(BF16), cm_ref[c, 0])
                + _dot(hs_ref[:, im0:im0 + STATE_CHUNK].astype(BF16), cm_ref[c, 1]))
        ys[h] = part if ys[h] is None else ys[h] + part

    y = jnp.concatenate(ys, axis=1) + d_ref[...] * u
    y = _gelu(y)
    z = _dot(y.astype(BF16), wglu_ref[...]) + bglu_ref[...]
    out = y * _sigmoid(z)
    for s in range(n_slabs):
        y_ref[s] = out[:, s * LANES:(s + 1) * LANES]

    @pl.when(j == pl.num_programs(1) - 1)
    def _():
        hlast_ref[0] = hstate_ref[...]


def _ssm(u_tb, h0, lam, bm, cm, d, wglu, bglu, *, n_groups, n_time, tc):
    rows = tc * SUBLANES
    nt = n_time // tc
    n_slabs = SSM_WIDTH // LANES
    full = lambda a: pl.BlockSpec(a.shape, lambda g, j: (0,) * a.ndim)
    st = pl.BlockSpec((1, SUBLANES, 2 * N_STATE), lambda g, j: (g, 0, 0))
    slabs = pl.BlockSpec((n_slabs, rows, LANES), lambda g, j: (0, g * nt + j, 0))
    return pl.pallas_call(
        functools.partial(_ssm_kernel, tc=tc),
        grid=(n_groups, nt),
        in_specs=[slabs, st, full(lam), full(bm), full(cm), full(d), full(wglu), full(bglu)],
        out_specs=[slabs, st],
        out_shape=[jax.ShapeDtypeStruct((n_slabs, n_groups * n_time * SUBLANES, LANES), F32),
                   jax.ShapeDtypeStruct((n_groups, SUBLANES, 2 * N_STATE), F32)],
        scratch_shapes=[pltpu.VMEM((rows, 2 * N_STATE), F32), pltpu.VMEM((SUBLANES, 2 * N_STATE), F32)],
        compiler_params=_params("parallel", "arbitrary"),
        name="ssm",
    )(u_tb, h0, lam, bm, cm, d, wglu, bglu)


def _state_lanes(re, im):
    lead = re.shape[:-1]
    r = re.reshape(lead + (N_CHUNKS, 1, STATE_CHUNK))
    i = im.reshape(lead + (N_CHUNKS, 1, STATE_CHUNK))
    return jnp.concatenate([r, i], axis=-2).reshape(lead + (2 * N_STATE,))


def _state_unlanes(x):
    lead = x.shape[:-1]
    y = x.reshape(lead + (N_CHUNKS, 2, STATE_CHUNK))
    return y[..., 0, :].reshape(lead + (N_STATE,)), y[..., 1, :].reshape(lead + (N_STATE,))


def _ssm_params(lam_re, lam_im, log_dt, b_re, b_im, c_re, c_im):
    lr, li = lam_re.astype(F32), lam_im.astype(F32)
    dt = jnp.exp(log_dt.astype(F32))[:, None]
    mag = jnp.exp(lr * dt)
    bar_re, bar_im = mag * jnp.cos(li * dt), mag * jnp.sin(li * dt)
    den = lr * lr + li * li
    f_re = ((bar_re - 1.0) * lr + bar_im * li) / den
    f_im = (bar_im * lr - (bar_re - 1.0) * li) / den
    b_re, b_im = b_re.astype(F32), b_im.astype(F32)
    bb_re = f_re[..., None] * b_re - f_im[..., None] * b_im
    bb_im = f_re[..., None] * b_im + f_im[..., None] * b_re
    g_chunk = STATE_CHUNK // SSM_STATE
    g_half = SSM_GROUPS // 2

    def same_group(c):
        h = c // (N_CHUNKS // 2)
        return (h * g_half + np.arange(g_half)[:, None] == c * g_chunk + np.arange(g_chunk)[None, :]
                ).astype(np.float32)

    def in_blockdiag(x, c):
        t = x[c * g_chunk:(c + 1) * g_chunk].transpose(2, 0, 1)[None]
        return (t * same_group(c)[:, None, :, None]).reshape(SSM_WIDTH // 2, STATE_CHUNK)

    def out_blockdiag(x, c):
        t = x[c * g_chunk:(c + 1) * g_chunk].transpose(0, 2, 1)[:, :, None, :]
        return (t * same_group(c).T[:, None, :, None]).reshape(STATE_CHUNK, SSM_WIDTH // 2)

    bm = jnp.stack([jnp.stack([in_blockdiag(bb_re, c), in_blockdiag(bb_im, c)])
                    for c in range(N_CHUNKS)]).astype(BF16)
    cm = jnp.stack([jnp.stack([out_blockdiag(c_re.astype(F32), c), out_blockdiag(-c_im.astype(F32), c)])
                    for c in range(N_CHUNKS)]).astype(BF16)
    lam_l = _state_lanes(bar_re.reshape(1, N_STATE), bar_im.reshape(1, N_STATE))
    return lam_l, bm, cm


def _compress_kernel(x_ref, pe_ref, w1_ref, w2_ref, o_ref, *, tm):
    acc = jnp.zeros((tm, KV_WIDTH), F32)
    for sp in range(BLK // 2):
        s0 = 2 * sp
        xa = x_ref[pl.ds(s0, tm, stride=BLK), :] + pe_ref[0, s0:s0 + 1, :]
        xb = x_ref[pl.ds(s0 + 1, tm, stride=BLK), :] + pe_ref[0, s0 + 1:s0 + 2, :]
        acc = acc + _dot(jnp.concatenate([xa, xb], axis=1).astype(BF16), w1_ref[0, sp])
    o_ref[0] = _dot(_gelu(acc).astype(BF16), w2_ref[0])


def _compress(x2d, pe, w1, w2, *, n_blocks, tm):
    return pl.pallas_call(
        functools.partial(_compress_kernel, tm=tm),
        grid=(2, n_blocks // tm),
        in_specs=[pl.BlockSpec((tm * BLK, KV_WIDTH), lambda c, i: (i, c)),
                  pl.BlockSpec((1, BLK, KV_WIDTH), lambda c, i: (c, 0, 0)),
                  pl.BlockSpec((1, BLK // 2, 2 * KV_WIDTH, KV_WIDTH), lambda c, i: (c, 0, 0, 0)),
                  pl.BlockSpec((1, KV_WIDTH, KV_WIDTH), lambda c, i: (c, 0, 0))],
        out_specs=pl.BlockSpec((1, tm, KV_WIDTH), lambda c, i: (c, i, 0)),
        out_shape=jax.ShapeDtypeStruct((2, n_blocks, KV_WIDTH), F32),
        compiler_params=_params("parallel", "parallel"),
        name="compress",
    )(x2d, pe, w1, w2)


def _compress_params(pe_k, w1_k, w2_k, pe_v, w1_v, w2_v):
    def bd(w):
        z = jnp.zeros_like(w)
        return jnp.concatenate([jnp.concatenate([w, z], axis=-1), jnp.concatenate([z, w], axis=-1)], axis=-2)

    def one(pe, w1, w2):
        w1s = bd(w1.astype(F32).reshape(BLK, HEAD_DIM, HEAD_DIM))
        return (jnp.tile(pe.astype(F32), (1, NSA_KV)),
                w1s.reshape(BLK // 2, 2 * KV_WIDTH, KV_WIDTH).astype(BF16),
                bd(w2.astype(F32)).astype(BF16))

    k, v = one(pe_k, w1_k, w2_k), one(pe_v, w1_v, w2_v)
    return tuple(jnp.stack([a, b]) for a, b in zip(k, v))


def _compress_paged_kernel(pt_ref, cache_ref, pe_ref, w1_ref, w2_ref, o_ref, buf_ref, sems, *, m,
                           pages_per_seq):
    step = pl.program_id(0)

    def page_copy(s, j):
        slot = s % 2
        row0 = pl.multiple_of(j * PAGE_PITCH, SUBLANES)
        idx = s * m + j
        if pages_per_seq:
            tok0 = pl.multiple_of((idx % pages_per_seq) * PAGE_SIZE, PAGE_SIZE)
            src = cache_ref.at[idx // pages_per_seq, pl.ds(0, 2), :, pl.ds(tok0, PAGE_SIZE)]
        else:
            src = cache_ref.at[pt_ref[idx], pl.ds(0, 2)]
        return pltpu.make_async_copy(src, buf_ref.at[pl.ds(2 * slot, 2), pl.ds(row0, PAGE_SIZE), :],
                                     sems.at[slot])

    def start_all(s):
        def start(j, _):
            page_copy(s, j).start()
            return 0
        lax.fori_loop(0, m, start, 0, unroll=8)

    @pl.when(step == 0)
    def _():
        start_all(step)

    @pl.when(step + 1 < pl.num_programs(0))
    def _():
        start_all(step + 1)

    def wait(j, _):
        page_copy(step, j).wait()
        return 0

    lax.fori_loop(0, m, wait, 0, unroll=8)

    for c in range(2):
        tile = 2 * (step % 2) + c
        res = []
        for kv in range(NSA_KV):
            acc = jnp.zeros((m, PAGE_SIZE), F32)
            for dp in range(HEAD_DIM // 2):
                d0 = 2 * dp
                r0 = kv * HEAD_DIM + d0
                xa = buf_ref[tile, pl.ds(r0, m, stride=PAGE_PITCH), :] + pe_ref[c, d0:d0 + 1, :]
                xb = buf_ref[tile, pl.ds(r0 + 1, m, stride=PAGE_PITCH), :] + pe_ref[c, d0 + 1:d0 + 2, :]
                acc = acc + _dot(jnp.concatenate([xa, xb], axis=1).astype(BF16), w1_ref[c, dp])
            res.append(_dot(_gelu(acc).astype(BF16), w2_ref[c]))
        for blk in range(PAGE_SIZE // BLK):
            cols = slice(blk * HEAD_DIM, (blk + 1) * HEAD_DIM)
            o_ref[c, pl.ds(blk, m, stride=PAGE_SIZE // BLK), :] = jnp.concatenate(
                [r[:, cols] for r in res], axis=1)


def _compress_paged(pt_flat, cache_t, pe_t, w1_t, w2, *, m, pages_per_seq=0):
    n = pt_flat.shape[0]
    full = lambda a: pl.BlockSpec(a.shape, lambda i, pt: (0,) * a.ndim)
    grid_spec = pltpu.PrefetchScalarGridSpec(
        num_scalar_prefetch=1,
        grid=(n // m,),
        in_specs=[pl.BlockSpec(memory_space=pl.ANY), full(pe_t), full(w1_t), full(w2)],
        out_specs=pl.BlockSpec((2, m * (PAGE_SIZE // BLK), KV_WIDTH), lambda i, pt: (0, i, 0)),
        scratch_shapes=[pltpu.VMEM((4, m * PAGE_PITCH, PAGE_SIZE), F32), pltpu.SemaphoreType.DMA((2,))])
    return pl.pallas_call(
        functools.partial(_compress_paged_kernel, m=m, pages_per_seq=pages_per_seq),
        grid_spec=grid_spec,
        out_shape=jax.ShapeDtypeStruct((2, n * (PAGE_SIZE // BLK), KV_WIDTH), F32),
        compiler_params=_params("arbitrary"),
        name="compress_paged",
    )(pt_flat, cache_t, pe_t, w1_t, w2)


def _compress_paged_params(pe_k, w1_k, pe_v, w1_v):
    def bd(w):
        z = jnp.zeros_like(w)
        return jnp.concatenate([jnp.concatenate([w, z], axis=-1), jnp.concatenate([z, w], axis=-1)], axis=-2)

    def one(pe, w1):
        w1d = bd(w1.astype(F32).reshape(BLK, HEAD_DIM, HEAD_DIM).transpose(1, 0, 2))
        return (jnp.tile(pe.astype(F32).T, (1, PAGE_SIZE // BLK)),
                w1d.reshape(HEAD_DIM // 2, 2 * PAGE_SIZE, PAGE_SIZE).astype(BF16))

    k, v = one(pe_k, w1_k), one(pe_v, w1_v)
    return tuple(jnp.stack([a, b]) for a, b in zip(k, v))


def _stack_queries(q, nq):
    q = q.astype(F32)
    z = jnp.zeros((nq, HEAD_DIM), F32)
    rows = []
    for h in range(NSA_HEADS):
        blk = q[:, h * HEAD_DIM:(h + 1) * HEAD_DIM]
        rows.append(jnp.concatenate([blk, z] if h < NSA_REP else [z, blk], axis=1))
    return (jnp.concatenate(rows, axis=0) * (HEAD_DIM ** -0.5)).astype(BF16)


def _masked_softmax(s, valid):
    s = jnp.where(valid, s, NEG_INF)
    m = jnp.max(s, axis=-1, keepdims=True)
    p = jnp.exp(s - m) * valid.astype(F32)
    return p / jnp.maximum(jnp.sum(p, axis=-1, keepdims=True), 1e-30)


def _select_blocks(imp, n_ids, cur, nb):
    forced = (n_ids == 0) | (n_ids == cur) | (n_ids == cur - 1)
    imp = jnp.where(forced, FORCE_SCORE, imp)
    imp = jnp.where(n_ids <= cur, imp, -FORCE_SCORE)
    rank = jnp.zeros(imp.shape, F32)
    for m in range(nb):
        col = imp[:, m:m + 1]
        beats = (col > imp) | ((col == imp) & (n_ids > m))
        rank = rank + beats.astype(F32)
    return (rank < float(N_SEL)).astype(F32)


def _select_blocks_t(imp_t, cur, nb):
    n_t = lax.broadcasted_iota(jnp.int32, imp_t.shape, 0)
    forced = (n_t == 0) | (n_t == cur) | (n_t == cur - 1)
    imp_t = jnp.where(forced, FORCE_SCORE, imp_t)
    imp_t = jnp.where(n_t <= cur, imp_t, -FORCE_SCORE)
    rank = jnp.zeros(imp_t.shape, F32)
    for m in range(nb):
        row = imp_t[m:m + 1, :]
        beats = (row > imp_t) | ((row == imp_t) & (n_t > m))
        rank = rank + beats.astype(F32)
    return ((rank < float(N_SEL)) & (n_t <= cur)).astype(F32)


def _combine_start(gates, o_c, o_w, nq):
    parts = []
    for h in range(NSA_HEADS):
        rows = slice(h * nq, (h + 1) * nq)
        parts.append((gates[:, 3 * h:3 * h + 1] * o_c[rows] + gates[:, 3 * h + 2:3 * h + 3] * o_w[rows],
                      jnp.broadcast_to(gates[:, 3 * h + 1:3 * h + 2], (nq, KV_WIDTH))))
    return parts


def _combine_finish(parts, o_s, nq):
    outs = []
    for h, (rest, g_s) in enumerate(parts):
        o = rest + g_s * o_s[h * nq:(h + 1) * nq]
        g = h // NSA_REP
        outs.append(o[:, g * HEAD_DIM:(g + 1) * HEAD_DIM])
    return jnp.concatenate(outs, axis=1)


def _combine_heads(gates, o_c, o_s, o_w, nq):
    return _combine_finish(_combine_start(gates, o_c, o_w, nq), o_s, nq)


def _nsa_quad_kernel(qlo_ref, qhi_ref, glo_ref, ghi_ref, kc_ref, vc_ref, ks_ref, vs_ref, kw_ref, vw_ref,
                     et_ref, tri_ref, wm_ref, olo_ref, ohi_ref, q2_ref, etd_ref, s_ref, *, nb, ch, wk, n_ch):
    j = pl.program_id(1)
    nq = BLK
    rows = NSA_HEADS * nq
    bpc = ch // BLK
    quarter = nb // 4
    chunk_of = (j, quarter + j, 3 * quarter - 1 - j, 4 * quarter - 1 - j)
    q_of = (qlo_ref.at[0, 0, 0], qlo_ref.at[0, 1, 0], qhi_ref.at[0, 0, 0], qhi_ref.at[0, 1, 0])
    g_of = (glo_ref.at[0, 0, 0], glo_ref.at[0, 1, 0], ghi_ref.at[0, 0, 0], ghi_ref.at[0, 1, 0])
    o_of = (olo_ref.at[0, 0, 0], olo_ref.at[0, 1, 0], ohi_ref.at[0, 1, 0], ohi_ref.at[0, 0, 0])
    row_q = lax.broadcasted_iota(jnp.int32, (rows, LANES), 0) % nq
    rq_minus_lane = row_q - lax.broadcasted_iota(jnp.int32, (rows, LANES), 1)

    def col_blocks(s):
        return [s[:, j * LANES:(j + 1) * LANES] for j in range(s.shape[1] // LANES)]

    def col_max(cols):
        m = cols[0]
        for c_ in cols[1:]:
            m = jnp.maximum(m, c_)
        return m

    def finish(acc):
        return acc[:, 0:KV_WIDTH] / jnp.maximum(acc[:, KV_WIDTH:2 * KV_WIDTH], 1e-30)

    def with_ones(v):
        return jnp.concatenate([v, jnp.ones((v.shape[0], LANES), BF16)], axis=1)

    q_onehot = jnp.where(rq_minus_lane == -BLK, NEG_INF, 0.0)

    def prepare(idx):
        ci = chunk_of[idx]
        q2 = _stack_queries(q_of[idx][...], nq)
        first = jnp.maximum(ci - WINDOW // BLK, 0) // (LANES // BLK)
        w0 = pl.multiple_of(first * LANES, LANES)
        delta = ci - first * (LANES // BLK)
        lhs = jnp.concatenate([q2, q_onehot.astype(BF16)], axis=1)
        rhs = jnp.concatenate([kw_ref[pl.ds(w0, wk), :], wm_ref[delta]], axis=1)
        cols = col_blocks(_dot_nt(lhs, rhs))
        yield
        q_pos = ci * BLK + lax.broadcasted_iota(jnp.int32, (rows, 1), 0) % nq
        n_ids = lax.broadcasted_iota(jnp.int32, (rows, nb), 1)
        s_c = _dot_nt(q2, kc_ref[0].astype(BF16))
        yield
        m_w = jnp.maximum(jnp.max(col_max(cols), axis=-1, keepdims=True), M_INIT)
        m_w = jnp.broadcast_to(m_w, (rows, LANES))
        p = jnp.concatenate([jnp.exp(c_ - m_w).astype(BF16) for c_ in cols], axis=1)
        yield
        p_c = _masked_softmax(s_c, (n_ids + 1) * BLK - 1 <= q_pos)
        yield
        o_w = finish(_dot(p, with_ones(vw_ref[pl.ds(w0, wk), :])))
        o_c = _dot(p_c.astype(BF16), vc_ref[0].astype(BF16))
        yield
        gated = _combine_start(g_of[idx][...], o_c, o_w, nq)
        yield

        imps = []
        for g in range(NSA_KV):
            imp = p_c[g * NSA_REP * nq:(g * NSA_REP + 1) * nq]
            for r in range(1, NSA_REP):
                imp = imp + p_c[(g * NSA_REP + r) * nq:(g * NSA_REP + r + 1) * nq]
            imps.append(imp)
        imp2 = jnp.concatenate([jnp.concatenate(imps, axis=0), jnp.zeros((LANES, LANES - nb), F32)], axis=1)
        imp_t = imp2.T[0:nb]
        yield
        sel_t = _select_blocks_t(imp_t, ci, nb)
        yield
        sel2 = jnp.concatenate([sel_t, jnp.ones((LANES - nb, LANES), F32)], axis=0).T
        neg = (sel2 - 1.0) * (-NEG_INF)
        neg_rows = jnp.concatenate([neg[g * nq:(g + 1) * nq] for g in range(NSA_KV) for _ in range(NSA_REP)],
                                   axis=0)
        q2_ref[idx] = jnp.concatenate([q2, (neg_rows + q_onehot).astype(BF16)], axis=1)
        in_diag = lax.broadcasted_iota(jnp.int32, (ch, LANES), 0) // BLK == ci % bpc
        etd_ref[idx] = et_ref[n_ch[idx] - 1] + jnp.where(in_diag, tri_ref[...], jnp.zeros((), BF16))
        return gated

    gated = _lockstep(*[prepare(idx) for idx in range(4)])

    first_unit = [sum(n_ch[:idx]) for idx in range(4)]

    def selected(idx):
        mx = jnp.full((rows, LANES), M_INIT, F32)
        for kc in range(n_ch[idx]):
            key_mask = etd_ref[idx] if kc == n_ch[idx] - 1 else et_ref[kc]
            rhs = jnp.concatenate([ks_ref[kc * ch:(kc + 1) * ch, :], key_mask], axis=1)
            cols = col_blocks(_dot_nt(q2_ref[idx], rhs))
            mx = jnp.maximum(mx, col_max(cols))
            for c, c_ in enumerate(cols):
                s_ref[first_unit[idx] + kc, :, c * LANES:(c + 1) * LANES] = c_
            yield
        m = jnp.broadcast_to(jnp.max(mx, axis=-1, keepdims=True), (rows, LANES))
        acc = None
        for kc in range(n_ch[idx]):
            p = jnp.concatenate(
                [jnp.exp(s_ref[first_unit[idx] + kc, :, c * LANES:(c + 1) * LANES] - m).astype(BF16)
                 for c in range(ch // LANES)], axis=1)
            part = _dot(p, with_ones(vs_ref[kc * ch:(kc + 1) * ch, :]))
            acc = part if acc is None else acc + part
            yield
        o_of[idx][...] = _combine_finish(gated[idx], finish(acc), nq).astype(olo_ref.dtype)

    _lockstep(*[selected(idx) for idx in range(4)])


def _nsa_prompt(q, gates, cmp_kv, kvb, *, n_seq, seq):
    nb = seq // BLK
    rows = NSA_HEADS * BLK
    ch = min(SEL_CHUNK, seq)
    wk = min(WINDOW + LANES, seq)
    bpc = ch // BLK
    assert seq % ch == 0 and wk % LANES == 0 and nb % 4 == 0 and nb <= LANES
    quarter = nb // 4
    n_ch = tuple(k * quarter // bpc + 1 for k in range(4))
    assert all(((k + 1) * quarter - 1) // bpc + 1 == n_ch[k] for k in range(4))
    n_units = sum(n_ch)
    assert nb <= BLK
    et = np.zeros((seq // ch, ch, LANES), np.float32)
    for c in range(seq // ch):
        et[c, np.arange(ch), (c * ch + np.arange(ch)) // BLK] = 1.0
    key = np.arange(ch)[:, None]
    qry = np.arange(BLK)[None, :]
    tri = np.zeros((ch, LANES), np.float32)
    tri[:, BLK:] = (key % BLK) > qry
    n_delta = WINDOW // BLK + LANES // BLK
    wm = np.zeros((n_delta, wk, LANES), np.float32)
    for dl in range(n_delta):
        dist = dl * BLK + qry - np.arange(wk)[:, None]
        wm[dl, :, BLK:] = (dist < 0) | (dist >= WINDOW)
    lower = lambda w_: pl.BlockSpec((1, 2, 1, BLK, w_), lambda b, j: (b, 0, j, 0, 0))
    upper = lambda w_: pl.BlockSpec((1, 2, 1, BLK, w_), lambda b, j: (b, 1, quarter - 1 - j, 0, 0))
    kv = lambda col: pl.BlockSpec((seq, KV_WIDTH), lambda b, j: (b, col))
    cmp_ = lambda which: pl.BlockSpec((1, nb, KV_WIDTH), lambda b, j: (which, b, 0))
    const = lambda a: pl.BlockSpec(a.shape, lambda b, j: (0,) * a.ndim)
    out = jax.ShapeDtypeStruct((n_seq, 2, quarter, BLK, NSA_WIDTH), BF16)
    q5 = q.reshape(n_seq, 4, quarter, BLK, NSA_WIDTH)
    g5 = gates.reshape(n_seq, 4, quarter, BLK, LANES)
    o_lo, o_hi = pl.pallas_call(
        functools.partial(_nsa_quad_kernel, nb=nb, ch=ch, wk=wk, n_ch=n_ch),
        grid=(n_seq, quarter),
        in_specs=[lower(NSA_WIDTH), upper(NSA_WIDTH), lower(LANES), upper(LANES),
                  cmp_(0), cmp_(1), kv(2), kv(3), kv(4), kv(5), const(et), const(tri), const(wm)],
        out_specs=[lower(NSA_WIDTH), lower(NSA_WIDTH)],
        out_shape=[out, out],
        scratch_shapes=[pltpu.VMEM((4, rows, 2 * LANES), BF16),
                        pltpu.VMEM((4, ch, LANES), BF16),
                        pltpu.VMEM((n_units, rows, ch), F32)],
        compiler_params=_params("parallel", "arbitrary"),
        name="nsa_prompt",
    )(q5, q5, g5, g5, cmp_kv, cmp_kv, kvb, kvb, kvb, kvb,
      jnp.asarray(et, dtype=BF16), jnp.asarray(tri, dtype=BF16), jnp.asarray(wm, dtype=BF16))
    return o_lo.reshape(-1, NSA_WIDTH), o_hi.reshape(-1, NSA_WIDTH)


def _unmirror(o_a, o_b, n_seq):
    half = o_a.shape[0] // (n_seq * BLK)
    o_a = o_a.reshape(n_seq, half, BLK, NSA_WIDTH)
    o_b = o_b.reshape(n_seq, half, BLK, NSA_WIDTH)[:, ::-1]
    return jnp.concatenate([o_a, o_b], axis=1).reshape(-1, NSA_WIDTH)


def _nsa_sample_kernel(pt_ref, q_ref, gate_ref, kc_ref, vc_ref, cache_ref, nks_ref, nvs_ref,
                       wt_ref, nkw_ref, nvw_ref, e_ref, o_ref, kv_buf, s_scr, sems,
                       *, nq, past, nbp, tk):
    step = pl.program_id(0)
    sb = q_ref.shape[0]
    n_pages = past // PAGE_SIZE
    rows = NSA_HEADS * nq

    def page_copy(t, k, j):
        buf = (t % 2) * sb + k
        k0 = pl.multiple_of(j * PAGE_SIZE, PAGE_SIZE)
        return pltpu.make_async_copy(cache_ref.at[pt_ref[(t * sb + k) * n_pages + j], pl.ds(2, 2)],
                                     kv_buf.at[pl.ds(2 * buf, 2), :, pl.ds(k0, PAGE_SIZE)], sems.at[buf])

    def start_all(t):
        for k in range(sb):
            def start(j, _, k=k):
                page_copy(t, k, j).start()
                return 0
            lax.fori_loop(0, n_pages, start, 0, unroll=8)

    @pl.when(step == 0)
    def _():
        start_all(step)

    @pl.when(step + 1 < pl.num_programs(0))
    def _():
        start_all(step + 1)

    _lockstep(*[_nsa_sample_sequence(k, step, page_copy, q_ref, gate_ref, kc_ref, vc_ref, nks_ref, nvs_ref,
                                     wt_ref, nkw_ref, nvw_ref, e_ref, o_ref, kv_buf, s_scr,
                                     nq=nq, past=past, nbp=nbp, tk=tk) for k in range(sb)])


def _nsa_sample_sequence(k, step, page_copy, q_ref, gate_ref, kc_ref, vc_ref, nks_ref, nvs_ref,
                         wt_ref, nkw_ref, nvw_ref, e_ref, o_ref, kv_buf, s_scr, *, nq, past, nbp, tk):
    sb = q_ref.shape[0]
    n_pages = past // PAGE_SIZE
    rows = NSA_HEADS * nq
    buf = (step % 2) * sb + k
    q2 = _stack_queries(q_ref[k], nq)
    q_pos = past + lax.broadcasted_iota(jnp.int32, (rows, 1), 0) % nq
    cur = past // BLK

    n_ids = lax.broadcasted_iota(jnp.int32, (rows, nbp), 1)
    s_c = _dot_nt(q2, kc_ref[k].astype(BF16))
    yield
    p_c = _masked_softmax(s_c, (n_ids + 1) * BLK - 1 <= q_pos)
    o_c = _dot(p_c.astype(BF16), vc_ref[k].astype(BF16))
    yield

    n_sel = lax.broadcasted_iota(jnp.int32, (nq, nbp), 1)
    sels = []
    for g in range(NSA_KV):
        imp = p_c[g * NSA_REP * nq:(g * NSA_REP + 1) * nq]
        for r in range(1, NSA_REP):
            imp = imp + p_c[(g * NSA_REP + r) * nq:(g * NSA_REP + r + 1) * nq]
        sel = _select_blocks(imp, n_sel, cur, cur + 1)
        sels.extend([sel] * NSA_REP)
        yield
    keys = _dot(jnp.concatenate(sels, axis=0).astype(BF16), e_ref[...])
    bias = (keys - 1.0) * (-NEG_INF)
    yield

    def col_max(s):
        m = s[:, 0:LANES]
        for j in range(1, s.shape[1] // LANES):
            m = jnp.maximum(m, s[:, j * LANES:(j + 1) * LANES])
        return m

    def col_sum(p):
        t = p[:, 0:LANES]
        for j in range(1, p.shape[1] // LANES):
            t = t + p[:, j * LANES:(j + 1) * LANES]
        return t

    def row_max(mx):
        return jnp.broadcast_to(jnp.max(mx, axis=-1, keepdims=True), mx.shape)

    def tiled(m, width):
        return m if width == LANES else jnp.concatenate([m] * (width // LANES), axis=1)

    new_pos = past + lax.broadcasted_iota(jnp.int32, (rows, LANES), 1)

    wlen = wt_ref.shape[3]
    w_pos = past - wlen + lax.broadcasted_iota(jnp.int32, (rows, wlen), 1)
    d = q_pos - w_pos
    s_w = _dot(q2, wt_ref[k, 0].astype(BF16))
    s_w = jnp.where((d >= 0) & (d < WINDOW) & (w_pos >= 0), s_w, NEG_INF)
    d = q_pos - new_pos
    s_nw = _dot_nt(q2, nkw_ref[k].astype(BF16))
    s_nw = jnp.where((d >= 0) & (d < WINDOW), s_nw, NEG_INF)
    yield
    m_w = row_max(jnp.maximum(jnp.maximum(col_max(s_w), s_nw), M_INIT))
    p_w = jnp.exp(s_w - tiled(m_w, wlen))
    p_nw = jnp.exp(s_nw - m_w)
    acc = _dot_nt(p_w.astype(BF16), wt_ref[k, 1].astype(BF16)) + _dot(p_nw.astype(BF16), nvw_ref[k].astype(BF16))
    o_w = acc / jnp.maximum(jnp.sum(col_sum(p_w) + p_nw, axis=-1, keepdims=True), 1e-30)
    yield

    def wait(j, _):
        page_copy(step, k, j).wait()
        return 0

    lax.fori_loop(0, n_pages, wait, 0, unroll=8)

    mx = jnp.full((rows, LANES), M_INIT, F32)
    for t in range(past // tk):
        s = _dot(q2, kv_buf[2 * buf, :, t * tk:(t + 1) * tk].astype(BF16)) + bias[:, t * tk:(t + 1) * tk]
        s_scr[k, :, t * tk:(t + 1) * tk] = s
        mx = jnp.maximum(mx, col_max(s))
        yield
    s_new = _dot_nt(q2, nks_ref[k].astype(BF16)) + bias[:, past:past + LANES]
    s_new = jnp.where(new_pos <= q_pos, s_new, NEG_INF)
    m_s = row_max(jnp.maximum(mx, s_new))
    p_new = jnp.exp(s_new - m_s)
    acc = _dot(p_new.astype(BF16), nvs_ref[k].astype(BF16))
    lsum = p_new
    yield
    for t in range(past // tk):
        p = jnp.exp(s_scr[k, :, t * tk:(t + 1) * tk] - tiled(m_s, tk))
        lsum = lsum + col_sum(p)
        acc = acc + _dot_nt(p.astype(BF16), kv_buf[2 * buf + 1, :, t * tk:(t + 1) * tk].astype(BF16))
        yield
    o_s = acc / jnp.maximum(jnp.sum(lsum, axis=-1, keepdims=True), 1e-30)

    o_ref[k] = _combine_heads(gate_ref[k], o_c, o_s, o_w, nq).astype(o_ref.dtype)


def _nsa_sample(pt_flat, q3, gates3, kc, vc, cache_t, new_rows, win_t, expand, *, past, tk, seqs_per_step):
    n_seq, nq, _ = q3.shape
    nbp = kc.shape[1]
    rows = NSA_HEADS * nq
    sb = seqs_per_step
    per_seq = lambda a: pl.BlockSpec((sb,) + a.shape[1:], lambda b, pt: (b,) + (0,) * (a.ndim - 1))
    new = lambda col: pl.BlockSpec((sb, LANES, KV_WIDTH), lambda b, pt: (b, 0, col))
    grid_spec = pltpu.PrefetchScalarGridSpec(
        num_scalar_prefetch=1,
        grid=(n_seq // sb,),
        in_specs=[per_seq(q3), per_seq(gates3), per_seq(kc), per_seq(vc),
                  pl.BlockSpec(memory_space=pl.ANY),
                  new(2), new(3), per_seq(win_t), new(4), new(5),
                  pl.BlockSpec(expand.shape, lambda b, pt: (0, 0), pipeline_mode=pl.Buffered(1))],
        out_specs=pl.BlockSpec((sb, nq, NSA_WIDTH), lambda b, pt: (b, 0, 0)),
        scratch_shapes=[pltpu.VMEM((4 * sb, KV_WIDTH, past), F32),
                        pltpu.VMEM((sb, rows, past), F32), pltpu.SemaphoreType.DMA((2 * sb,))])
    return pl.pallas_call(
        functools.partial(_nsa_sample_kernel, nq=nq, past=past, nbp=nbp, tk=tk),
        grid_spec=grid_spec,
        out_shape=jax.ShapeDtypeStruct((n_seq, nq, NSA_WIDTH), F32),
        compiler_params=_params("arbitrary"),
        name="nsa_sample",
    )(pt_flat, q3, gates3, kc, vc, cache_t, new_rows, new_rows, win_t, new_rows, new_rows, expand)


def _expand_matrix(nb, n_keys):
    return jnp.asarray(np.arange(n_keys)[None, :] // BLK == np.arange(nb)[:, None], dtype=BF16)


def _merge_kernel(x_ref, ssm_ref, *refs, mirrored):
    nsa_refs = refs[:2] if mirrored else refs[:1]
    gpre_ref, wm_ref, wbs_ref, wbn_ref, wo_ref, gpost_ref, gx_ref, wxq_ref, x1_ref, qx_ref = refs[len(nsa_refs):]
    x = x_ref[...]
    tm = x.shape[0]
    if mirrored:
        upper = nsa_refs[1][...]
        n_chunks = tm // BLK
        upper = jnp.concatenate([upper[(n_chunks - 1 - s) * BLK:(n_chunks - s) * BLK] for s in range(n_chunks)],
                                axis=0)
        nsa = jnp.where(pl.program_id(0) < pl.num_programs(0) // 2, nsa_refs[0][...], upper)
    else:
        nsa = nsa_refs[0][...]
    n_seq = ssm_ref.shape[1] // tm
    ssm = jnp.concatenate([ssm_ref[s, pl.ds(pl.program_id(1), tm, stride=n_seq), :]
                           for s in range(SSM_WIDTH // LANES)], axis=1)
    a = _rms(x, gpre_ref[...]).astype(BF16)
    g_ssm = _sigmoid(_dot(a, wm_ref[:, 0:D_MODEL]))
    g_nsa = _sigmoid(_dot(a, wm_ref[:, D_MODEL:2 * D_MODEL]))
    merged = (g_ssm * _dot(ssm.astype(BF16), wbs_ref[...])
              + g_nsa * _dot(nsa, wbn_ref[...]))
    x1 = x + _rms(_dot(merged.astype(BF16), wo_ref[...]), gpost_ref[...])
    x1_ref[...] = x1
    c = _rms(x1, gx_ref[...]).astype(BF16)
    qx_ref[...] = (_dot(c, wxq_ref[...]) * (X_HEAD_DIM ** -0.5)).astype(BF16)


def _merge(x2d, ssm_slabs, nsa_o, weights, *, tm, n_tiles):
    n = x2d.shape[0]
    n_b = n // (n_tiles * tm)
    row = lambda w_: pl.BlockSpec((tm, w_), lambda t, b: (b * n_tiles + t, 0))
    full = lambda a: pl.BlockSpec(a.shape, lambda t, b: (0,) * a.ndim, pipeline_mode=pl.Buffered(1))
    ssm_spec = pl.BlockSpec((SSM_WIDTH // LANES, tm * n_b, LANES), lambda t, b: (0, t, 0))
    mirrored = isinstance(nsa_o, tuple)
    if mirrored:
        half = n_tiles // 2
        nsa_specs = [pl.BlockSpec((tm, NSA_WIDTH), lambda t, b: (b * half + jnp.minimum(t, half - 1), 0)),
                     pl.BlockSpec((tm, NSA_WIDTH), lambda t, b: (b * half + jnp.minimum(n_tiles - 1 - t, half - 1), 0))]
        nsa_args = list(nsa_o)
    else:
        nsa_specs, nsa_args = [row(NSA_WIDTH)], [nsa_o]
    return pl.pallas_call(
        functools.partial(_merge_kernel, mirrored=mirrored),
        grid=(n_tiles, n_b),
        in_specs=[row(D_MODEL), ssm_spec] + nsa_specs + [full(w) for w in weights],
        out_specs=[row(D_MODEL), row(X_WIDTH)],
        out_shape=[jax.ShapeDtypeStruct((n, D_MODEL), F32), jax.ShapeDtypeStruct((n, X_WIDTH), BF16)],
        compiler_params=_params("parallel", "arbitrary"),
        name="merge",
    )(x2d, ssm_slabs, *nsa_args, *weights)


def _xattn_tile(q, kv_ref, m_len):
    outs = []
    for h in range(X_HEADS):
        cols = slice(h * X_HEAD_DIM, (h + 1) * X_HEAD_DIM)
        k = kv_ref[0, pl.ds(h, m_len, stride=2 * X_HEADS), :].astype(BF16)
        v = kv_ref[0, pl.ds(X_HEADS + h, m_len, stride=2 * X_HEADS), :].astype(BF16)
        s = _dot_nt(q[:, cols], k)
        m = jnp.max(s, axis=-1, keepdims=True)
        p = jnp.exp(s - m)
        p = p / jnp.sum(p, axis=-1, keepdims=True)
        outs.append(_dot(p.astype(BF16), v))
    return jnp.concatenate(outs, axis=1)


def _xattn_kernel(q_ref, kv_ref, o_ref, *, m_len):
    t = q_ref.shape[1]
    n_rows = 2 * X_HEADS
    for s in range(q_ref.shape[0]):
        q = q_ref[s].astype(F32)
        z = jnp.zeros((t, X_HEAD_DIM), F32)
        q_bd = jnp.concatenate(
            [jnp.concatenate([q[:, h * X_HEAD_DIM:(h + 1) * X_HEAD_DIM] if j == h else z for j in range(X_HEADS)],
                             axis=1) for h in range(X_HEADS)], axis=0).astype(BF16)
        k = jnp.concatenate([kv_ref[s, pl.ds(h, m_len, stride=n_rows), :] for h in range(X_HEADS)],
                            axis=1).astype(BF16)
        v = jnp.concatenate([kv_ref[s, pl.ds(X_HEADS + h, m_len, stride=n_rows), :] for h in range(X_HEADS)],
                            axis=1).astype(BF16)
        sc = _dot_nt(q_bd, k)
        p = jnp.exp(sc - jnp.max(sc, axis=-1, keepdims=True))
        p = p / jnp.sum(p, axis=-1, keepdims=True)
        o = _dot(p.astype(BF16), v)
        o_ref[s] = jnp.concatenate([o[h * t:(h + 1) * t, h * X_HEAD_DIM:(h + 1) * X_HEAD_DIM]
                                    for h in range(X_HEADS)], axis=1).astype(o_ref.dtype)


def _xattn(q3, mem_kv_rows, *, seqs_per_step):
    n_seq, t, _ = q3.shape
    m_len = mem_kv_rows.shape[1] // (2 * X_HEADS)
    sb = seqs_per_step
    return pl.pallas_call(
        functools.partial(_xattn_kernel, m_len=m_len),
        grid=(n_seq // sb,),
        in_specs=[pl.BlockSpec((sb, t, X_WIDTH), lambda b: (b, 0, 0)),
                  pl.BlockSpec((sb, m_len * 2 * X_HEADS, X_HEAD_DIM), lambda b: (b, 0, 0))],
        out_specs=pl.BlockSpec((sb, t, X_WIDTH), lambda b: (b, 0, 0)),
        out_shape=jax.ShapeDtypeStruct((n_seq, t, X_WIDTH), q3.dtype),
        compiler_params=_params("parallel"),
        name="xattn",
    )(q3, mem_kv_rows)


def _mlp_kernel(x1_ref, o_ref, *refs, m_len):
    if m_len:
        kv_ref, refs = refs[0], refs[1:]
        o = _xattn_tile(o_ref[...], kv_ref, m_len).astype(BF16)
    else:
        o = o_ref[...]
    wxo_ref, gxp_ref, gm_ref, wup_ref, wdn_ref, gmp_ref, y_ref = refs
    x2 = x1_ref[...] + _rms(_dot(o, wxo_ref[...]), gxp_ref[...])
    m = _rms(x2, gm_ref[...]).astype(BF16)
    hid = jnp.maximum(_dot(m, wup_ref[...]), 0.0)
    hid = (hid * hid).astype(BF16)
    y_ref[...] = x2 + _rms(_dot(hid, wdn_ref[...]), gmp_ref[...])


def _mlp(x1, o, weights, *, tm, mem_kv_rows=None, tiles_per_seq=1):
    n = x1.shape[0]
    row = lambda w_: pl.BlockSpec((tm, w_), lambda i: (i, 0))
    full = lambda a: pl.BlockSpec(a.shape, lambda i: (0,) * a.ndim, pipeline_mode=pl.Buffered(1))
    m_len, kv_specs, kv_args = 0, [], []
    if mem_kv_rows is not None:
        m_len = mem_kv_rows.shape[1] // (2 * X_HEADS)
        kv_specs = [pl.BlockSpec((1,) + mem_kv_rows.shape[1:], lambda i: (i // tiles_per_seq, 0, 0))]
        kv_args = [mem_kv_rows]
    return pl.pallas_call(
        functools.partial(_mlp_kernel, m_len=m_len),
        grid=(n // tm,),
        in_specs=[row(D_MODEL), row(X_WIDTH)] + kv_specs + [full(w) for w in weights],
        out_specs=row(D_MODEL),
        out_shape=jax.ShapeDtypeStruct((n, D_MODEL), F32),
        compiler_params=_params("parallel"),
        name="mlp",
    )(x1, o, *kv_args, *weights)


def _memkv_kernel(m_ref, g_ref, w_ref, o_ref, *, tm):
    kv = _dot(_rms(m_ref[...], g_ref[...]).astype(BF16), w_ref[...])
    n_rows = 2 * X_HEADS
    for j in range(n_rows):
        o_ref[pl.ds(j, tm, stride=n_rows), :] = kv[:, j * X_HEAD_DIM:(j + 1) * X_HEAD_DIM]


def _memkv(mem2d, g, w, *, tm):
    n = mem2d.shape[0]
    n_rows = 2 * X_HEADS
    return pl.pallas_call(
        functools.partial(_memkv_kernel, tm=tm),
        grid=(n // tm,),
        in_specs=[pl.BlockSpec((tm, D_MODEL), lambda i: (i, 0)),
                  pl.BlockSpec(g.shape, lambda i: (0, 0)), pl.BlockSpec(w.shape, lambda i: (0, 0))],
        out_specs=pl.BlockSpec((tm * n_rows, X_HEAD_DIM), lambda i: (i, 0)),
        out_shape=jax.ShapeDtypeStruct((n * n_rows, X_HEAD_DIM), F32),
        compiler_params=_params("parallel"),
        name="memkv",
    )(mem2d, g, w)


def _row(v):
    return v.astype(F32).reshape(1, -1)


def kernel(x_prompt, x_sample, cache_nsa_kv, cache_win_kv, state_ssm, cache_mem_kv, page_table, mem_prompt, g_mix_pre, w_in, ssm_lam_re, ssm_lam_im, ssm_log_dt, ssm_b_re, ssm_b_im, ssm_c_re, ssm_c_im, ssm_d, w_glu, b_glu, cmp_pe_k, w_cmpk1, w_cmpk2, cmp_pe_v, w_cmpv1, w_cmpv2, w_br_ssm, w_br_nsa, w_out, g_mix_post, g_x_pre, g_mem, w_xq, w_xk, w_xv, w_xo, g_x_post, g_mlp_pre, w_up, w_down, g_mlp_post):
    depth = w_in.shape[0]
    n_seq_p, seq, _ = x_prompt.shape
    n_seq_s, nq, _ = x_sample.shape
    past = page_table.shape[1] * PAGE_SIZE
    assert depth == 1 and seq % BLK == 0 and nq <= SUBLANES and past % BLK == 0
    assert n_seq_p == SUBLANES and n_seq_s % SUBLANES == 0

    y_p = x_prompt.reshape(n_seq_p * seq, D_MODEL)
    y_s = x_sample.reshape(n_seq_s * nq, D_MODEL)
    l = 0

    w_proj = w_in[l, :, :N_PROJ].astype(BF16)
    w_gate = jnp.pad(w_in[l, :, N_PROJ:N_PROJ + N_GATE], ((0, 0), (0, LANES - N_GATE))).astype(BF16)
    w_merge = w_in[l, :, N_PROJ + N_GATE:].astype(BF16)
    lam_l, bm, cm = _ssm_params(ssm_lam_re[l], ssm_lam_im[l], ssm_log_dt[l], ssm_b_re[l], ssm_b_im[l],
                                ssm_c_re[l], ssm_c_im[l])
    ssm_w = (lam_l, bm, cm, _row(ssm_d[l]), w_glu[l].astype(BF16), _row(b_glu[l]))
    cmp_w = _compress_params(cmp_pe_k[l], w_cmpk1[l], w_cmpk2[l], cmp_pe_v[l], w_cmpv1[l], w_cmpv2[l])
    merge_w = (_row(g_mix_pre[l]), w_merge, w_br_ssm[l].astype(BF16), w_br_nsa[l].astype(BF16),
               w_out[l].astype(BF16), _row(g_mix_post[l]), _row(g_x_pre[l]), w_xq[l].astype(BF16))
    mlp_w = (w_xo[l].astype(BF16), _row(g_x_post[l]), _row(g_mlp_pre[l]), w_up[l].astype(BF16),
             w_down[l].astype(BF16), _row(g_mlp_post[l]))
    w_mem = jnp.concatenate([w_xk[l], w_xv[l]], axis=1).astype(BF16)

    tm_p = 512 if seq % 512 == 0 else seq
    nt_p = seq // tm_p
    tabs_p = _rope_tables(np.arange(seq))
    u_p, q_p, kvt_p, wint_p, kvb_p, gate_p = _proj(
        y_p, _row(g_mix_pre[l]), w_proj, w_gate, tabs_p, tm=tm_p, n_tab_blocks=nt_p, prompt=True)
    h0_p = jnp.zeros((1, SUBLANES, 2 * N_STATE), F32)
    tc_p = 64 if seq % 64 == 0 else seq
    ssm_p, hl_p = _ssm(u_p, h0_p, *ssm_w, n_groups=1, n_time=seq, tc=tc_p)

    assert seq % PAGE_SIZE == 0
    n_pages_p = n_seq_p * seq // PAGE_SIZE
    pe_t, w1_t = _compress_paged_params(cmp_pe_k[l], w_cmpk1[l], cmp_pe_v[l], w_cmpv1[l])
    cmp_p = _compress_paged(jnp.zeros((n_pages_p,), jnp.int32), kvt_p, pe_t, w1_t, cmp_w[2],
                            m=min(n_pages_p, 128), pages_per_seq=seq // PAGE_SIZE)
    nsa_p = _nsa_prompt(q_p, gate_p, cmp_p, kvb_p, n_seq=n_seq_p, seq=seq)

    mem_kv_p = _memkv(mem_prompt.reshape(-1, D_MODEL), _row(g_mem[l]), w_mem, tm=256)
    m_len = mem_prompt.shape[1]
    mem_kv_p3 = mem_kv_p.reshape(n_seq_p, m_len * 2 * X_HEADS, X_HEAD_DIM)

    if nt_p % 2:
        nsa_p = _unmirror(*nsa_p, n_seq_p)
    x1_p, qx_p = _merge(y_p, ssm_p, nsa_p, merge_w, tm=tm_p, n_tiles=nt_p)
    y_p = _mlp(x1_p, qx_p, mlp_w, tm=tm_p, mem_kv_rows=mem_kv_p3, tiles_per_seq=nt_p)

    n_s = n_seq_s * nq
    tabs_s = tuple(np.tile(t, (n_seq_s, 1)) for t in _rope_tables(past + np.arange(nq)))
    u_s, q_s, kv_s, win_s, kvb_s, gate_s = _proj(y_s, _row(g_mix_pre[l]), w_proj, w_gate, tabs_s,
                                                 tm=n_s, n_tab_blocks=1, prompt=False)
    n_grp = n_seq_s // SUBLANES
    n_slabs = SSM_WIDTH // LANES
    u_s = (u_s.reshape(n_grp, SUBLANES, nq, n_slabs, LANES).transpose(3, 0, 2, 1, 4)
           .reshape(n_slabs, n_s, LANES))
    st = state_ssm[l].astype(F32).reshape(n_seq_s, N_STATE, 2)
    h0_s = _state_lanes(st[..., 0], st[..., 1]).reshape(n_grp, SUBLANES, 2 * N_STATE)
    ssm_s, hl_s = _ssm(u_s, h0_s, *ssm_w, n_groups=n_grp, n_time=nq, tc=nq)
    ssm_s = (ssm_s.reshape(n_slabs, n_grp, nq, SUBLANES, LANES).transpose(0, 1, 3, 2, 4)
             .reshape(n_slabs, n_s, LANES))

    n_pages = page_table.shape[1]
    n_pool = cache_nsa_kv.shape[1]
    cache_t = jnp.transpose(cache_nsa_kv[l], (0, 2, 3, 4, 1)).reshape(n_pool, 4, KV_WIDTH, PAGE_SIZE)
    win_t = jnp.transpose(cache_win_kv[l], (0, 2, 3, 4, 1)).reshape(n_seq_s, 2, KV_WIDTH, -1)
    pt_flat = page_table.reshape(-1).astype(jnp.int32)
    nb_past = past // BLK
    cmp_pages = _compress_paged(pt_flat, cache_t, pe_t, w1_t, cmp_w[2], m=min(n_seq_s * n_pages, 128))
    cmp_past = cmp_pages
    new_rows = jnp.pad(jnp.concatenate([kv_s, win_s], axis=1).reshape(n_seq_s, nq, 6 * KV_WIDTH),
                       ((0, 0), (0, LANES - nq), (0, 0)))
    cmp_new = _compress(new_rows[:, :BLK].reshape(n_seq_s * BLK, 6 * KV_WIDTH), *cmp_w,
                        n_blocks=n_seq_s, tm=n_seq_s)
    nbp = -(-(nb_past + 1) // LANES) * LANES
    cmp_s = jnp.concatenate([cmp_past.reshape(2, n_seq_s, nb_past, KV_WIDTH), cmp_new[:, :, None, :],
                             jnp.zeros((2, n_seq_s, nbp - nb_past - 1, KV_WIDTH), F32)], axis=2)
    nsa_s = _nsa_sample(pt_flat, q_s.astype(F32).reshape(n_seq_s, nq, NSA_WIDTH),
                        gate_s.reshape(n_seq_s, nq, LANES), cmp_s[0], cmp_s[1], cache_t, new_rows, win_t,
                        _expand_matrix(nbp, past + LANES), past=past, tk=min(past, 1024), seqs_per_step=2)

    x1_s, qx_s = _merge(y_s, ssm_s, nsa_s.reshape(n_s, NSA_WIDTH).astype(BF16), merge_w, tm=n_s, n_tiles=1)
    mem_kv_s3 = cache_mem_kv[l].reshape(n_seq_s, m_len * 2 * X_HEADS, X_HEAD_DIM)
    o_s = _xattn(qx_s.astype(F32).reshape(n_seq_s, nq, X_WIDTH), mem_kv_s3, seqs_per_step=SUBLANES)
    y_s = _mlp(x1_s, o_s.reshape(-1, X_WIDTH).astype(BF16), mlp_w, tm=n_s)

    def ssm_state(hl, n_seq):
        re, im = _state_unlanes(hl.reshape(n_seq, 2 * N_STATE))
        return jnp.stack([re, im], axis=-1).reshape(1, n_seq, SSM_GROUPS, SSM_STATE, 2)

    def token_major(xt):
        n_seq, parts, _, t_len = xt.shape
        return xt.reshape(1, n_seq, parts, NSA_KV, HEAD_DIM, t_len).transpose(0, 1, 5, 2, 3, 4)

    w_keep = min(WINDOW, seq)
    win_new = win_s.reshape(n_seq_s, nq, 2, NSA_KV, HEAD_DIM).astype(cache_win_kv.dtype)
    win_sample = jnp.concatenate([cache_win_kv[l], win_new], axis=1)[:, nq:]
    return (y_p.reshape(n_seq_p, seq, D_MODEL),
            y_s.reshape(n_seq_s, nq, D_MODEL),
            token_major(kvt_p),
            kv_s.reshape(1, n_seq_s, nq, 4, NSA_KV, HEAD_DIM),
            token_major(wint_p[:, :, :, seq - w_keep:]),
            win_sample[None],
            ssm_state(hl_p, n_seq_p),
            ssm_state(hl_s, n_seq_s),
            mem_kv_p.reshape(1, n_seq_p, m_len, 2, X_HEADS, X_HEAD_DIM))
```

```python
import functools
import math

import jax
import jax.numpy as jnp
import numpy as np
from jax import lax
from jax.experimental import pallas as pl
from jax.experimental.pallas import tpu as pltpu

F32 = jnp.float32
BF16 = jnp.bfloat16

D_MODEL = 1024
SSM_WIDTH = 512
SSM_GROUP = 16
SSM_GROUPS = 32
SSM_STATE = 64
N_STATE = SSM_GROUPS * SSM_STATE
STATE_CHUNK = 512
N_CHUNKS = N_STATE // STATE_CHUNK
NSA_HEADS = 8
HEAD_DIM = 64
NSA_WIDTH = NSA_HEADS * HEAD_DIM
NSA_KV = 2
NSA_REP = NSA_HEADS // NSA_KV
KV_WIDTH = NSA_KV * HEAD_DIM
BLK = 64
N_SEL = 16
WINDOW = 512
ROT_DIM = 16
ROPE_THETA = 500000.0
PAGE_SIZE = 128
X_HEADS = 4
X_HEAD_DIM = 128
X_WIDTH = X_HEADS * X_HEAD_DIM
D_FF = 4 * D_MODEL
EPS = 1e-6
NEG_INF = -1e30
M_INIT = -1e29
FORCE_SCORE = 1e4
LANES = 128
SUBLANES = 8
VMEM_LIMIT = 56 * 1024 * 1024

N_PROJ = SSM_WIDTH + NSA_WIDTH + 6 * KV_WIDTH
N_GATE = 3 * NSA_HEADS
SEL_CHUNK = 512
NSA_POS_PER_STEP = 2
PAGE_PITCH = PAGE_SIZE + SUBLANES


def _params(*sem):
    return pltpu.CompilerParams(dimension_semantics=sem, vmem_limit_bytes=VMEM_LIMIT)


def _rms(x, g):
    return x * lax.rsqrt(jnp.mean(x * x, axis=-1, keepdims=True) + EPS) * g


def _gelu(x):
    return 0.5 * x * (1.0 + jnp.tanh(math.sqrt(2.0 / math.pi) * (x + 0.044715 * (x * x * x))))


def _sigmoid(x):
    return 1.0 / (1.0 + jnp.exp(-x))


def _dot(a, b):
    return jnp.dot(a, b, preferred_element_type=F32)


def _dot_nt(a, b):
    return lax.dot_general(a, b, (((1,), (1,)), ((), ())), preferred_element_type=F32)


def _lockstep(*stagewise):
    results = [None] * len(stagewise)
    live = list(range(len(stagewise)))
    while live:
        for k in list(live):
            try:
                next(stagewise[k])
            except StopIteration as done:
                results[k] = done.value
                live.remove(k)
    return results


def _proj_kernel(x_ref, g_ref, w_ref, wg_ref, cos_ref, sp_ref, sm_ref,
                 u_ref, q_ref, kv_ref, win_ref, kvb_ref, gate_ref, *, token_minor):
    a = _rms(x_ref[...], g_ref[...]).astype(BF16)
    cos, sp, sm = cos_ref[...], sp_ref[...], sm_ref[...]

    def rope(blk):
        return blk * cos + pltpu.roll(blk, 8, 1) * sp + pltpu.roll(blk, LANES - 8, 1) * sm

    pairs = {}

    def lane_block(j):
        if j // 2 not in pairs:
            pairs[j // 2] = _dot(a, w_ref[:, (j // 2) * 2 * LANES:(j // 2 + 1) * 2 * LANES])
        return pairs[j // 2][:, (j % 2) * LANES:(j % 2 + 1) * LANES]

    if token_minor:
        tm = x_ref.shape[0]
        n_seq = u_ref.shape[1] // tm
        for s in range(SSM_WIDTH // LANES):
            u_ref[s, pl.ds(pl.program_id(1), tm, stride=n_seq), :] = lane_block(s)
    else:
        u_ref[...] = _dot(a, w_ref[:, 0:SSM_WIDTH])
    for j in range(NSA_WIDTH // LANES):
        q_ref[:, j * LANES:(j + 1) * LANES] = rope(lane_block(SSM_WIDTH // LANES + j)).astype(BF16)
    for j in range(6):
        blk = lane_block((SSM_WIDTH + NSA_WIDTH) // LANES + j)
        if j % 2 == 0:
            blk = rope(blk)
        out_ref, part = (kv_ref, j) if j < 4 else (win_ref, j - 4)
        if token_minor:
            out_ref[0, part] = blk.T
        else:
            out_ref[:, part * LANES:(part + 1) * LANES] = blk
        kvb_ref[:, j * LANES:(j + 1) * LANES] = blk.astype(BF16)
    gate_ref[...] = _sigmoid(_dot(a, wg_ref[...]))


def _proj(x2d, g, w, wg, tabs, *, tm, n_tab_blocks, prompt):
    n = x2d.shape[0]
    nt = n_tab_blocks
    n_b = n // (nt * tm)
    row = lambda w_: pl.BlockSpec((tm, w_), lambda t, b: (b * nt + t, 0))
    full = lambda a: pl.BlockSpec(a.shape, lambda t, b: (0,) * a.ndim)
    tab = pl.BlockSpec((tm, LANES), lambda t, b: (t, 0))
    if prompt:
        t_len = nt * tm
        n_slabs = SSM_WIDTH // LANES
        u_shape = jax.ShapeDtypeStruct((n_slabs, t_len * n_b, LANES), F32)
        u_spec = pl.BlockSpec((n_slabs, tm * n_b, LANES), lambda t, b: (0, t, 0))
        tok_minor = lambda parts: pl.BlockSpec((1, parts, KV_WIDTH, tm), lambda t, b: (b, 0, 0, t))
        kv_specs = [tok_minor(4), tok_minor(2)]
        kv_shapes = [jax.ShapeDtypeStruct((n_b, 4, KV_WIDTH, t_len), F32),
                     jax.ShapeDtypeStruct((n_b, 2, KV_WIDTH, t_len), F32)]
    else:
        u_shape = jax.ShapeDtypeStruct((n, SSM_WIDTH), F32)
        u_spec = row(SSM_WIDTH)
        kv_specs = [row(4 * KV_WIDTH), row(2 * KV_WIDTH)]
        kv_shapes = [jax.ShapeDtypeStruct((n, 4 * KV_WIDTH), F32), jax.ShapeDtypeStruct((n, 2 * KV_WIDTH), F32)]
    return pl.pallas_call(
        functools.partial(_proj_kernel, token_minor=prompt),
        grid=(nt, n_b),
        in_specs=[row(D_MODEL), full(g), full(w), full(wg), tab, tab, tab],
        out_specs=[u_spec, row(NSA_WIDTH)] + kv_specs + [row(6 * KV_WIDTH), row(LANES)],
        out_shape=[u_shape, jax.ShapeDtypeStruct((n, NSA_WIDTH), BF16)] + kv_shapes
                  + [jax.ShapeDtypeStruct((n, 6 * KV_WIDTH), BF16), jax.ShapeDtypeStruct((n, LANES), F32)],
        compiler_params=_params("parallel", "arbitrary"),
        name="proj",
    )(x2d, g, w, wg, *tabs)


def _rope_tables(pos):
    half = ROT_DIM // 2
    freqs = ROPE_THETA ** (-np.arange(half, dtype=np.float64) / half)
    ang = np.asarray(pos, np.float64)[:, None] * freqs[None, :]
    cos, sin = np.cos(ang), np.sin(ang)
    r = ang.shape[0]
    z8 = np.zeros((r, half))
    rest0 = np.zeros((r, HEAD_DIM - ROT_DIM))
    rest1 = np.ones((r, HEAD_DIM - ROT_DIM))
    c64 = np.concatenate([cos, cos, rest1], axis=1)
    sp64 = np.concatenate([z8, sin, rest0], axis=1)
    sm64 = np.concatenate([-sin, z8, rest0], axis=1)
    return tuple(np.tile(t, (1, LANES // HEAD_DIM)).astype(np.float32) for t in (c64, sp64, sm64))


def _ssm_kernel(u_ref, h0_ref, lam_ref, bm_ref, cm_ref, d_ref, wglu_ref, bglu_ref,
                y_ref, hlast_ref, hs_ref, hstate_ref, *, tc):
    j = pl.program_id(1)

    @pl.when(j == 0)
    def _():
        hstate_ref[...] = h0_ref[0]

    n_slabs = SSM_WIDTH // LANES
    u = jnp.concatenate([u_ref[s] for s in range(n_slabs)], axis=1)
    ub = u.astype(BF16)
    half_in = SSM_WIDTH // 2
    chunk_lanes = 2 * STATE_CHUNK
    chunks_per_half = N_CHUNKS // 2
    ys = [None, None]
    for c in range(N_CHUNKS):
        h = c // chunks_per_half
        re0 = c * chunk_lanes
        im0 = re0 + STATE_CHUNK
        u_half = ub[:, h * half_in:(h + 1) * half_in]
        hs_ref[:, re0:re0 + STATE_CHUNK] = _dot(u_half, bm_ref[c, 0])
        hs_ref[:, im0:im0 + STATE_CHUNK] = _dot(u_half, bm_ref[c, 1])
        lr = jnp.broadcast_to(lam_ref[0:1, re0:re0 + STATE_CHUNK], (SUBLANES, STATE_CHUNK))
        li = jnp.broadcast_to(lam_ref[0:1, im0:im0 + STATE_CHUNK], (SUBLANES, STATE_CHUNK))
        hr = hstate_ref[:, re0:re0 + STATE_CHUNK]
        hi = hstate_ref[:, im0:im0 + STATE_CHUNK]
        for t in range(tc):
            r0 = t * SUBLANES
            hr, hi = (lr * hr - li * hi + hs_ref[r0:r0 + SUBLANES, re0:re0 + STATE_CHUNK],
                      lr * hi + li * hr + hs_ref[r0:r0 + SUBLANES, im0:im0 + STATE_CHUNK])
            hs_ref[r0:r0 + SUBLANES, re0:re0 + STATE_CHUNK] = hr
            hs_ref[r0:r0 + SUBLANES, im0:im0 + STATE_CHUNK] = hi
        hstate_ref[:, re0:re0 + STATE_CHUNK] = hr
        hstate_ref[:, im0:im0 + STATE_CHUNK] = hi
        part = (_dot(hs_ref[:, re0:re0 + STATE_CHUNK].astype(BF16), cm_ref[c, 0])
                + _dot(hs_ref[:, im0:im0 + STATE_CHUNK].astype(BF16), cm_ref[c, 1]))
        ys[h] = part if ys[h] is None else ys[h] + part

    y = jnp.concatenate(ys, axis=1) + d_ref[...] * u
    y = _gelu(y)
    z = _dot(y.astype(BF16), wglu_ref[...]) + bglu_ref[...]
    out = y * _sigmoid(z)
    for s in range(n_slabs):
        y_ref[s] = out[:, s * LANES:(s + 1) * LANES]

    @pl.when(j == pl.num_programs(1) - 1)
    def _():
        hlast_ref[0] = hstate_ref[...]


def _ssm(u_tb, h0, lam, bm, cm, d, wglu, bglu, *, n_groups, n_time, tc):
    rows = tc * SUBLANES
    nt = n_time // tc
    n_slabs = SSM_WIDTH // LANES
    full = lambda a: pl.BlockSpec(a.shape, lambda g, j: (0,) * a.ndim)
    st = pl.BlockSpec((1, SUBLANES, 2 * N_STATE), lambda g, j: (g, 0, 0))
    slabs = pl.BlockSpec((n_slabs, rows, LANES), lambda g, j: (0, g * nt + j, 0))
    return pl.pallas_call(
        functools.partial(_ssm_kernel, tc=tc),
        grid=(n_groups, nt),
        in_specs=[slabs, st, full(lam), full(bm), full(cm), full(d), full(wglu), full(bglu)],
        out_specs=[slabs, st],
        out_shape=[jax.ShapeDtypeStruct((n_slabs, n_groups * n_time * SUBLANES, LANES), F32),
                   jax.ShapeDtypeStruct((n_groups, SUBLANES, 2 * N_STATE), F32)],
        scratch_shapes=[pltpu.VMEM((rows, 2 * N_STATE), F32), pltpu.VMEM((SUBLANES, 2 * N_STATE), F32)],
        compiler_params=_params("parallel", "arbitrary"),
        name="ssm",
    )(u_tb, h0, lam, bm, cm, d, wglu, bglu)


def _state_lanes(re, im):
    lead = re.shape[:-1]
    r = re.reshape(lead + (N_CHUNKS, 1, STATE_CHUNK))
    i = im.reshape(lead + (N_CHUNKS, 1, STATE_CHUNK))
    return jnp.concatenate([r, i], axis=-2).reshape(lead + (2 * N_STATE,))


def _state_unlanes(x):
    lead = x.shape[:-1]
    y = x.reshape(lead + (N_CHUNKS, 2, STATE_CHUNK))
    return y[..., 0, :].reshape(lead + (N_STATE,)), y[..., 1, :].reshape(lead + (N_STATE,))


def _ssm_params(lam_re, lam_im, log_dt, b_re, b_im, c_re, c_im):
    lr, li = lam_re.astype(F32), lam_im.astype(F32)
    dt = jnp.exp(log_dt.astype(F32))[:, None]
    mag = jnp.exp(lr * dt)
    bar_re, bar_im = mag * jnp.cos(li * dt), mag * jnp.sin(li * dt)
    den = lr * lr + li * li
    f_re = ((bar_re - 1.0) * lr + bar_im * li) / den
    f_im = (bar_im * lr - (bar_re - 1.0) * li) / den
    b_re, b_im = b_re.astype(F32), b_im.astype(F32)
    bb_re = f_re[..., None] * b_re - f_im[..., None] * b_im
    bb_im = f_re[..., None] * b_im + f_im[..., None] * b_re
    g_chunk = STATE_CHUNK // SSM_STATE
    g_half = SSM_GROUPS // 2

    def same_group(c):
        h = c // (N_CHUNKS // 2)
        return (h * g_half + np.arange(g_half)[:, None] == c * g_chunk + np.arange(g_chunk)[None, :]
                ).astype(np.float32)

    def in_blockdiag(x, c):
        t = x[c * g_chunk:(c + 1) * g_chunk].transpose(2, 0, 1)[None]
        return (t * same_group(c)[:, None, :, None]).reshape(SSM_WIDTH // 2, STATE_CHUNK)

    def out_blockdiag(x, c):
        t = x[c * g_chunk:(c + 1) * g_chunk].transpose(0, 2, 1)[:, :, None, :]
        return (t * same_group(c).T[:, None, :, None]).reshape(STATE_CHUNK, SSM_WIDTH // 2)

    bm = jnp.stack([jnp.stack([in_blockdiag(bb_re, c), in_blockdiag(bb_im, c)])
                    for c in range(N_CHUNKS)]).astype(BF16)
    cm = jnp.stack([jnp.stack([out_blockdiag(c_re.astype(F32), c), out_blockdiag(-c_im.astype(F32), c)])
                    for c in range(N_CHUNKS)]).astype(BF16)
    lam_l = _state_lanes(bar_re.reshape(1, N_STATE), bar_im.reshape(1, N_STATE))
    return lam_l, bm, cm


def _compress_kernel(x_ref, pe_ref, w1_ref, w2_ref, o_ref, *, tm):
    acc = jnp.zeros((tm, KV_WIDTH), F32)
    for sp in range(BLK // 2):
        s0 = 2 * sp
        xa = x_ref[pl.ds(s0, tm, stride=BLK), :] + pe_ref[0, s0:s0 + 1, :]
        xb = x_ref[pl.ds(s0 + 1, tm, stride=BLK), :] + pe_ref[0, s0 + 1:s0 + 2, :]
        acc = acc + _dot(jnp.concatenate([xa, xb], axis=1).astype(BF16), w1_ref[0, sp])
    o_ref[0] = _dot(_gelu(acc).astype(BF16), w2_ref[0])


def _compress(x2d, pe, w1, w2, *, n_blocks, tm):
    return pl.pallas_call(
        functools.partial(_compress_kernel, tm=tm),
        grid=(2, n_blocks // tm),
        in_specs=[pl.BlockSpec((tm * BLK, KV_WIDTH), lambda c, i: (i, c)),
                  pl.BlockSpec((1, BLK, KV_WIDTH), lambda c, i: (c, 0, 0)),
                  pl.BlockSpec((1, BLK // 2, 2 * KV_WIDTH, KV_WIDTH), lambda c, i: (c, 0, 0, 0)),
                  pl.BlockSpec((1, KV_WIDTH, KV_WIDTH), lambda c, i: (c, 0, 0))],
        out_specs=pl.BlockSpec((1, tm, KV_WIDTH), lambda c, i: (c, i, 0)),
        out_shape=jax.ShapeDtypeStruct((2, n_blocks, KV_WIDTH), F32),
        compiler_params=_params("parallel", "parallel"),
        name="compress",
    )(x2d, pe, w1, w2)


def _compress_params(pe_k, w1_k, w2_k, pe_v, w1_v, w2_v):
    def bd(w):
        z = jnp.zeros_like(w)
        return jnp.concatenate([jnp.concatenate([w, z], axis=-1), jnp.concatenate([z, w], axis=-1)], axis=-2)

    def one(pe, w1, w2):
        w1s = bd(w1.astype(F32).reshape(BLK, HEAD_DIM, HEAD_DIM))
        return (jnp.tile(pe.astype(F32), (1, NSA_KV)),
                w1s.reshape(BLK // 2, 2 * KV_WIDTH, KV_WIDTH).astype(BF16),
                bd(w2.astype(F32)).astype(BF16))

    k, v = one(pe_k, w1_k, w2_k), one(pe_v, w1_v, w2_v)
    return tuple(jnp.stack([a, b]) for a, b in zip(k, v))


def _compress_paged_kernel(pt_ref, cache_ref, pe_ref, w1_ref, w2_ref, o_ref, buf_ref, sems, *, m,
                           pages_per_seq):
    step = pl.program_id(0)

    def page_copy(s, j):
        slot = s % 2
        row0 = pl.multiple_of(j * PAGE_PITCH, SUBLANES)
        idx = s * m + j
        if pages_per_seq:
            tok0 = pl.multiple_of((idx % pages_per_seq) * PAGE_SIZE, PAGE_SIZE)
            src = cache_ref.at[idx // pages_per_seq, pl.ds(0, 2), :, pl.ds(tok0, PAGE_SIZE)]
        else:
            src = cache_ref.at[pt_ref[idx], pl.ds(0, 2)]
        return pltpu.make_async_copy(src, buf_ref.at[pl.ds(2 * slot, 2), pl.ds(row0, PAGE_SIZE), :],
                                     sems.at[slot])

    def start_all(s):
        def start(j, _):
            page_copy(s, j).start()
            return 0
        lax.fori_loop(0, m, start, 0, unroll=8)

    @pl.when(step == 0)
    def _():
        start_all(step)

    @pl.when(step + 1 < pl.num_programs(0))
    def _():
        start_all(step + 1)

    def wait(j, _):
        page_copy(step, j).wait()
        return 0

    lax.fori_loop(0, m, wait, 0, unroll=8)

    for c in range(2):
        tile = 2 * (step % 2) + c
        res = []
        for kv in range(NSA_KV):
            acc = jnp.zeros((m, PAGE_SIZE), F32)
            for dp in range(HEAD_DIM // 2):
                d0 = 2 * dp
                r0 = kv * HEAD_DIM + d0
                xa = buf_ref[tile, pl.ds(r0, m, stride=PAGE_PITCH), :] + pe_ref[c, d0:d0 + 1, :]
                xb = buf_ref[tile, pl.ds(r0 + 1, m, stride=PAGE_PITCH), :] + pe_ref[c, d0 + 1:d0 + 2, :]
                acc = acc + _dot(jnp.concatenate([xa, xb], axis=1).astype(BF16), w1_ref[c, dp])
            res.append(_dot(_gelu(acc).astype(BF16), w2_ref[c]))
        for blk in range(PAGE_SIZE // BLK):
            cols = slice(blk * HEAD_DIM, (blk + 1) * HEAD_DIM)
            o_ref[c, pl.ds(blk, m, stride=PAGE_SIZE // BLK), :] = jnp.concatenate(
                [r[:, cols] for r in res], axis=1)


def _compress_paged(pt_flat, cache_t, pe_t, w1_t, w2, *, m, pages_per_seq=0):
    n = pt_flat.shape[0]
    full = lambda a: pl.BlockSpec(a.shape, lambda i, pt: (0,) * a.ndim)
    grid_spec = pltpu.PrefetchScalarGridSpec(
        num_scalar_prefetch=1,
        grid=(n // m,),
        in_specs=[pl.BlockSpec(memory_space=pl.ANY), full(pe_t), full(w1_t), full(w2)],
        out_specs=pl.BlockSpec((2, m * (PAGE_SIZE // BLK), KV_WIDTH), lambda i, pt: (0, i, 0)),
        scratch_shapes=[pltpu.VMEM((4, m * PAGE_PITCH, PAGE_SIZE), F32), pltpu.SemaphoreType.DMA((2,))])
    return pl.pallas_call(
        functools.partial(_compress_paged_kernel, m=m, pages_per_seq=pages_per_seq),
        grid_spec=grid_spec,
        out_shape=jax.ShapeDtypeStruct((2, n * (PAGE_SIZE // BLK), KV_WIDTH), F32),
        compiler_params=_params("arbitrary"),
        name="compress_paged",
    )(pt_flat, cache_t, pe_t, w1_t, w2)


def _compress_paged_params(pe_k, w1_k, pe_v, w1_v):
    def bd(w):
        z = jnp.zeros_like(w)
        return jnp.concatenate([jnp.concatenate([w, z], axis=-1), jnp.concatenate([z, w], axis=-1)], axis=-2)

    def one(pe, w1):
        w1d = bd(w1.astype(F32).reshape(BLK, HEAD_DIM, HEAD_DIM).transpose(1, 0, 2))
        return (jnp.tile(pe.astype(F32).T, (1, PAGE_SIZE // BLK)),
                w1d.reshape(HEAD_DIM // 2, 2 * PAGE_SIZE, PAGE_SIZE).astype(BF16))

    k, v = one(pe_k, w1_k), one(pe_v, w1_v)
    return tuple(jnp.stack([a, b]) for a, b in zip(k, v))


def _stack_queries(q, nq):
    q = q.astype(F32)
    z = jnp.zeros((nq, HEAD_DIM), F32)
    rows = []
    for h in range(NSA_HEADS):
        blk = q[:, h * HEAD_DIM:(h + 1) * HEAD_DIM]
        rows.append(jnp.concatenate([blk, z] if h < NSA_REP else [z, blk], axis=1))
    return (jnp.concatenate(rows, axis=0) * (HEAD_DIM ** -0.5)).astype(BF16)


def _masked_softmax(s, valid):
    s = jnp.where(valid, s, NEG_INF)
    m = jnp.max(s, axis=-1, keepdims=True)
    p = jnp.exp(s - m) * valid.astype(F32)
    return p / jnp.maximum(jnp.sum(p, axis=-1, keepdims=True), 1e-30)


def _select_blocks(imp, n_ids, cur, nb):
    forced = (n_ids == 0) | (n_ids == cur) | (n_ids == cur - 1)
    imp = jnp.where(forced, FORCE_SCORE, imp)
    imp = jnp.where(n_ids <= cur, imp, -FORCE_SCORE)
    rank = jnp.zeros(imp.shape, F32)
    for m in range(nb):
        col = imp[:, m:m + 1]
        beats = (col > imp) | ((col == imp) & (n_ids > m))
        rank = rank + beats.astype(F32)
    return (rank < float(N_SEL)).astype(F32)


def _select_blocks_t(imp_t, cur, nb):
    n_t = lax.broadcasted_iota(jnp.int32, imp_t.shape, 0)
    forced = (n_t == 0) | (n_t == cur) | (n_t == cur - 1)
    imp_t = jnp.where(forced, FORCE_SCORE, imp_t)
    imp_t = jnp.where(n_t <= cur, imp_t, -FORCE_SCORE)
    rank = jnp.zeros(imp_t.shape, F32)
    for m in range(nb):
        row = imp_t[m:m + 1, :]
        beats = (row > imp_t) | ((row == imp_t) & (n_t > m))
        rank = rank + beats.astype(F32)
    return ((rank < float(N_SEL)) & (n_t <= cur)).astype(F32)


def _combine_start(gates, o_c, o_w, nq):
    parts = []
    for h in range(NSA_HEADS):
        rows = slice(h * nq, (h + 1) * nq)
        parts.append((gates[:, 3 * h:3 * h + 1] * o_c[rows] + gates[:, 3 * h + 2:3 * h + 3] * o_w[rows],
                      jnp.broadcast_to(gates[:, 3 * h + 1:3 * h + 2], (nq, KV_WIDTH))))
    return parts


def _combine_finish(parts, o_s, nq):
    outs = []
    for h, (rest, g_s) in enumerate(parts):
        o = rest + g_s * o_s[h * nq:(h + 1) * nq]
        g = h // NSA_REP
        outs.append(o[:, g * HEAD_DIM:(g + 1) * HEAD_DIM])
    return jnp.concatenate(outs, axis=1)


def _combine_heads(gates, o_c, o_s, o_w, nq):
    return _combine_finish(_combine_start(gates, o_c, o_w, nq), o_s, nq)


def _nsa_quad_kernel(qlo_ref, qhi_ref, glo_ref, ghi_ref, kc_ref, vc_ref, ks_ref, vs_ref, kw_ref, vw_ref,
                     et_ref, tri_ref, wm_ref, olo_ref, ohi_ref, q2_ref, etd_ref, s_ref, *, nb, ch, wk, n_ch):
    j = pl.program_id(1)
    nq = BLK
    rows = NSA_HEADS * nq
    bpc = ch // BLK
    quarter = nb // 4
    n_pos = qlo_ref.shape[2]
    chunk_of, q_of, g_of, o_of, n_ch_of = [], [], [], [], []
    for g in range(n_pos):
        p = j * n_pos + g
        chunk_of += [p, quarter + p, 3 * quarter - 1 - p, 4 * quarter - 1 - p]
        q_of += [qlo_ref.at[0, 0, g], qlo_ref.at[0, 1, g],
                 qhi_ref.at[0, 0, n_pos - 1 - g], qhi_ref.at[0, 1, n_pos - 1 - g]]
        g_of += [glo_ref.at[0, 0, g], glo_ref.at[0, 1, g],
                 ghi_ref.at[0, 0, n_pos - 1 - g], ghi_ref.at[0, 1, n_pos - 1 - g]]
        o_of += [olo_ref.at[0, 0, g], olo_ref.at[0, 1, g], ohi_ref.at[0, 1, g], ohi_ref.at[0, 0, g]]
        n_ch_of += list(n_ch)
    n_chunks = 4 * n_pos
    row_q = lax.broadcasted_iota(jnp.int32, (rows, LANES), 0) % nq
    rq_minus_lane = row_q - lax.broadcasted_iota(jnp.int32, (rows, LANES), 1)

    def col_blocks(s):
        return [s[:, j * LANES:(j + 1) * LANES] for j in range(s.shape[1] // LANES)]

    def col_max(cols):
        m = cols[0]
        for c_ in cols[1:]:
            m = jnp.maximum(m, c_)
        return m

    def finish(acc):
        return acc[:, 0:KV_WIDTH] / jnp.maximum(acc[:, KV_WIDTH:2 * KV_WIDTH], 1e-30)

    def with_ones(v):
        return jnp.concatenate([v, jnp.ones((v.shape[0], LANES), BF16)], axis=1)

    q_onehot = jnp.where(rq_minus_lane == -BLK, NEG_INF, 0.0)

    def prepare(idx):
        ci = chunk_of[idx]
        q2 = _stack_queries(q_of[idx][...], nq)
        first = jnp.maximum(ci - WINDOW // BLK, 0) // (LANES // BLK)
        w0 = pl.multiple_of(first * LANES, LANES)
        delta = ci - first * (LANES // BLK)
        lhs = jnp.concatenate([q2, q_onehot.astype(BF16)], axis=1)
        rhs = jnp.concatenate([kw_ref[pl.ds(w0, wk), :], wm_ref[delta]], axis=1)
        cols = col_blocks(_dot_nt(lhs, rhs))
        yield
        q_pos = ci * BLK + lax.broadcasted_iota(jnp.int32, (rows, 1), 0) % nq
        n_ids = lax.broadcasted_iota(jnp.int32, (rows, nb), 1)
        s_c = _dot_nt(q2, kc_ref[0].astype(BF16))
        yield
        m_w = jnp.maximum(jnp.max(col_max(cols), axis=-1, keepdims=True), M_INIT)
        m_w = jnp.broadcast_to(m_w, (rows, LANES))
        p = jnp.concatenate([jnp.exp(c_ - m_w).astype(BF16) for c_ in cols], axis=1)
        yield
        p_c = _masked_softmax(s_c, (n_ids + 1) * BLK - 1 <= q_pos)
        yield
        o_w = finish(_dot(p, with_ones(vw_ref[pl.ds(w0, wk), :])))
        o_c = _dot(p_c.astype(BF16), vc_ref[0].astype(BF16))
        yield
        gated = _combine_start(g_of[idx][...], o_c, o_w, nq)
        yield

        imps = []
        for g in range(NSA_KV):
            imp = p_c[g * NSA_REP * nq:(g * NSA_REP + 1) * nq]
            for r in range(1, NSA_REP):
                imp = imp + p_c[(g * NSA_REP + r) * nq:(g * NSA_REP + r + 1) * nq]
            imps.append(imp)
        imp2 = jnp.concatenate([jnp.concatenate(imps, axis=0), jnp.zeros((LANES, LANES - nb), F32)], axis=1)
        imp_t = imp2.T[0:nb]
        yield
        sel_t = _select_blocks_t(imp_t, ci, nb)
        yield
        sel2 = jnp.concatenate([sel_t, jnp.ones((LANES - nb, LANES), F32)], axis=0).T
        neg = (sel2 - 1.0) * (-NEG_INF)
        neg_rows = jnp.concatenate([neg[g * nq:(g + 1) * nq] for g in range(NSA_KV) for _ in range(NSA_REP)],
                                   axis=0)
        q2_ref[idx] = jnp.concatenate([q2, (neg_rows + q_onehot).astype(BF16)], axis=1)
        in_diag = lax.broadcasted_iota(jnp.int32, (ch, LANES), 0) // BLK == ci % bpc
        etd_ref[idx] = et_ref[n_ch_of[idx] - 1] + jnp.where(in_diag, tri_ref[...], jnp.zeros((), BF16))
        return gated

    gated = _lockstep(*[prepare(idx) for idx in range(n_chunks)])

    first_unit = [sum(n_ch_of[:idx]) for idx in range(n_chunks)]

    def selected(idx):
        mx = jnp.full((rows, LANES), M_INIT, F32)
        for kc in range(n_ch_of[idx]):
            key_mask = etd_ref[idx] if kc == n_ch_of[idx] - 1 else et_ref[kc]
            rhs = jnp.concatenate([ks_ref[kc * ch:(kc + 1) * ch, :], key_mask], axis=1)
            cols = col_blocks(_dot_nt(q2_ref[idx], rhs))
            mx = jnp.maximum(mx, col_max(cols))
            for c, c_ in enumerate(cols):
                s_ref[first_unit[idx] + kc, :, c * LANES:(c + 1) * LANES] = c_
            yield
        m = jnp.broadcast_to(jnp.max(mx, axis=-1, keepdims=True), (rows, LANES))
        acc = None
        for kc in range(n_ch_of[idx]):
            p = jnp.concatenate(
                [jnp.exp(s_ref[first_unit[idx] + kc, :, c * LANES:(c + 1) * LANES] - m).astype(BF16)
                 for c in range(ch // LANES)], axis=1)
            part = _dot(p, with_ones(vs_ref[kc * ch:(kc + 1) * ch, :]))
            acc = part if acc is None else acc + part
            yield
        o_of[idx][...] = _combine_finish(gated[idx], finish(acc), nq).astype(olo_ref.dtype)

    _lockstep(*[selected(idx) for idx in range(n_chunks)])


def _nsa_prompt(q, gates, cmp_kv, kvb, *, n_seq, seq):
    nb = seq // BLK
    rows = NSA_HEADS * BLK
    ch = min(SEL_CHUNK, seq)
    wk = min(WINDOW + LANES, seq)
    bpc = ch // BLK
    assert seq % ch == 0 and wk % LANES == 0 and nb % 4 == 0 and nb <= LANES
    quarter = nb // 4
    n_ch = tuple(k * quarter // bpc + 1 for k in range(4))
    assert all(((k + 1) * quarter - 1) // bpc + 1 == n_ch[k] for k in range(4))
    n_units = sum(n_ch)
    assert nb <= BLK
    et = np.zeros((seq // ch, ch, LANES), np.float32)
    for c in range(seq // ch):
        et[c, np.arange(ch), (c * ch + np.arange(ch)) // BLK] = 1.0
    key = np.arange(ch)[:, None]
    qry = np.arange(BLK)[None, :]
    tri = np.zeros((ch, LANES), np.float32)
    tri[:, BLK:] = (key % BLK) > qry
    n_delta = WINDOW // BLK + LANES // BLK
    wm = np.zeros((n_delta, wk, LANES), np.float32)
    for dl in range(n_delta):
        dist = dl * BLK + qry - np.arange(wk)[:, None]
        wm[dl, :, BLK:] = (dist < 0) | (dist >= WINDOW)
    n_pos = NSA_POS_PER_STEP if quarter % NSA_POS_PER_STEP == 0 else 1
    steps = quarter // n_pos
    lower = lambda w_: pl.BlockSpec((1, 2, n_pos, BLK, w_), lambda b, j: (b, 0, j, 0, 0))
    upper = lambda w_: pl.BlockSpec((1, 2, n_pos, BLK, w_), lambda b, j: (b, 1, steps - 1 - j, 0, 0))
    kv = lambda col: pl.BlockSpec((seq, KV_WIDTH), lambda b, j: (b, col))
    cmp_ = lambda which: pl.BlockSpec((1, nb, KV_WIDTH), lambda b, j: (which, b, 0))
    const = lambda a: pl.BlockSpec(a.shape, lambda b, j: (0,) * a.ndim)
    out = jax.ShapeDtypeStruct((n_seq, 2, quarter, BLK, NSA_WIDTH), BF16)
    q5 = q.reshape(n_seq, 4, quarter, BLK, NSA_WIDTH)
    g5 = gates.reshape(n_seq, 4, quarter, BLK, LANES)
    o_lo, o_hi = pl.pallas_call(
        functools.partial(_nsa_quad_kernel, nb=nb, ch=ch, wk=wk, n_ch=n_ch),
        grid=(n_seq, steps),
        in_specs=[lower(NSA_WIDTH), upper(NSA_WIDTH), lower(LANES), upper(LANES),
                  cmp_(0), cmp_(1), kv(2), kv(3), kv(4), kv(5), const(et), const(tri), const(wm)],
        out_specs=[lower(NSA_WIDTH), lower(NSA_WIDTH)],
        out_shape=[out, out],
        scratch_shapes=[pltpu.VMEM((4 * n_pos, rows, 2 * LANES), BF16),
                        pltpu.VMEM((4 * n_pos, ch, LANES), BF16),
                        pltpu.VMEM((n_units * n_pos, rows, ch), F32)],
        compiler_params=_params("parallel", "arbitrary"),
        name="nsa_prompt",
    )(q5, q5, g5, g5, cmp_kv, cmp_kv, kvb, kvb, kvb, kvb,
      jnp.asarray(et, dtype=BF16), jnp.asarray(tri, dtype=BF16), jnp.asarray(wm, dtype=BF16))
    return o_lo.reshape(-1, NSA_WIDTH), o_hi.reshape(-1, NSA_WIDTH)


def _unmirror(o_a, o_b, n_seq):
    half = o_a.shape[0] // (n_seq * BLK)
    o_a = o_a.reshape(n_seq, half, BLK, NSA_WIDTH)
    o_b = o_b.reshape(n_seq, half, BLK, NSA_WIDTH)[:, ::-1]
    return jnp.concatenate([o_a, o_b], axis=1).reshape(-1, NSA_WIDTH)


def _nsa_sample_kernel(pt_ref, q_ref, gate_ref, kc_ref, vc_ref, cache_ref, nks_ref, nvs_ref,
                       wt_ref, nkw_ref, nvw_ref, e_ref, o_ref, kv_buf, s_scr, sems,
                       *, nq, past, nbp, tk):
    step = pl.program_id(0)
    sb = q_ref.shape[0]
    n_pages = past // PAGE_SIZE
    rows = NSA_HEADS * nq

    def page_copy(t, k, j):
        buf = (t % 2) * sb + k
        k0 = pl.multiple_of(j * PAGE_SIZE, PAGE_SIZE)
        return pltpu.make_async_copy(cache_ref.at[pt_ref[(t * sb + k) * n_pages + j], pl.ds(2, 2)],
                                     kv_buf.at[pl.ds(2 * buf, 2), :, pl.ds(k0, PAGE_SIZE)], sems.at[buf])

    def start_all(t):
        for k in range(sb):
            def start(j, _, k=k):
                page_copy(t, k, j).start()
                return 0
            lax.fori_loop(0, n_pages, start, 0, unroll=8)

    @pl.when(step == 0)
    def _():
        start_all(step)

    @pl.when(step + 1 < pl.num_programs(0))
    def _():
        start_all(step + 1)

    _lockstep(*[_nsa_sample_sequence(k, step, page_copy, q_ref, gate_ref, kc_ref, vc_ref, nks_ref, nvs_ref,
                                     wt_ref, nkw_ref, nvw_ref, e_ref, o_ref, kv_buf, s_scr,
                                     nq=nq, past=past, nbp=nbp, tk=tk) for k in range(sb)])


def _nsa_sample_sequence(k, step, page_copy, q_ref, gate_ref, kc_ref, vc_ref, nks_ref, nvs_ref,
                         wt_ref, nkw_ref, nvw_ref, e_ref, o_ref, kv_buf, s_scr, *, nq, past, nbp, tk):
    sb = q_ref.shape[0]
    n_pages = past // PAGE_SIZE
    rows = NSA_HEADS * nq
    buf = (step % 2) * sb + k
    q2 = _stack_queries(q_ref[k], nq)
    q_pos = past + lax.broadcasted_iota(jnp.int32, (rows, 1), 0) % nq
    cur = past // BLK

    n_ids = lax.broadcasted_iota(jnp.int32, (rows, nbp), 1)
    s_c = _dot_nt(q2, kc_ref[k].astype(BF16))
    yield
    p_c = _masked_softmax(s_c, (n_ids + 1) * BLK - 1 <= q_pos)
    o_c = _dot(p_c.astype(BF16), vc_ref[k].astype(BF16))
    yield

    n_sel = lax.broadcasted_iota(jnp.int32, (nq, nbp), 1)
    sels = []
    for g in range(NSA_KV):
        imp = p_c[g * NSA_REP * nq:(g * NSA_REP + 1) * nq]
        for r in range(1, NSA_REP):
            imp = imp + p_c[(g * NSA_REP + r) * nq:(g * NSA_REP + r + 1) * nq]
        sel = _select_blocks(imp, n_sel, cur, cur + 1)
        sels.extend([sel] * NSA_REP)
        yield
    keys = _dot(jnp.concatenate(sels, axis=0).astype(BF16), e_ref[...])
    bias = (keys - 1.0) * (-NEG_INF)
    yield

    def col_max(s):
        m = s[:, 0:LANES]
        for j in range(1, s.shape[1] // LANES):
            m = jnp.maximum(m, s[:, j * LANES:(j + 1) * LANES])
        return m

    def col_sum(p):
        t = p[:, 0:LANES]
        for j in range(1, p.shape[1] // LANES):
            t = t + p[:, j * LANES:(j + 1) * LANES]
        return t

    def row_max(mx):
        return jnp.broadcast_to(jnp.max(mx, axis=-1, keepdims=True), mx.shape)

    def tiled(m, width):
        return m if width == LANES else jnp.concatenate([m] * (width // LANES), axis=1)

    new_pos = past + lax.broadcasted_iota(jnp.int32, (rows, LANES), 1)

    wlen = wt_ref.shape[3]
    w_pos = past - wlen + lax.broadcasted_iota(jnp.int32, (rows, wlen), 1)
    d = q_pos - w_pos
    s_w = _dot(q2, wt_ref[k, 0].astype(BF16))
    s_w = jnp.where((d >= 0) & (d < WINDOW) & (w_pos >= 0), s_w, NEG_INF)
    d = q_pos - new_pos
    s_nw = _dot_nt(q2, nkw_ref[k].astype(BF16))
    s_nw = jnp.where((d >= 0) & (d < WINDOW), s_nw, NEG_INF)
    yield
    m_w = row_max(jnp.maximum(jnp.maximum(col_max(s_w), s_nw), M_INIT))
    p_w = jnp.exp(s_w - tiled(m_w, wlen))
    p_nw = jnp.exp(s_nw - m_w)
    acc = _dot_nt(p_w.astype(BF16), wt_ref[k, 1].astype(BF16)) + _dot(p_nw.astype(BF16), nvw_ref[k].astype(BF16))
    o_w = acc / jnp.maximum(jnp.sum(col_sum(p_w) + p_nw, axis=-1, keepdims=True), 1e-30)
    yield

    def wait(j, _):
        page_copy(step, k, j).wait()
        return 0

    lax.fori_loop(0, n_pages, wait, 0, unroll=8)

    mx = jnp.full((rows, LANES), M_INIT, F32)
    for t in range(past // tk):
        s = _dot(q2, kv_buf[2 * buf, :, t * tk:(t + 1) * tk].astype(BF16)) + bias[:, t * tk:(t + 1) * tk]
        s_scr[k, :, t * tk:(t + 1) * tk] = s
        mx = jnp.maximum(mx, col_max(s))
        yield
    s_new = _dot_nt(q2, nks_ref[k].astype(BF16)) + bias[:, past:past + LANES]
    s_new = jnp.where(new_pos <= q_pos, s_new, NEG_INF)
    m_s = row_max(jnp.maximum(mx, s_new))
    p_new = jnp.exp(s_new - m_s)
    acc = _dot(p_new.astype(BF16), nvs_ref[k].astype(BF16))
    lsum = p_new
    yield
    for t in range(past // tk):
        p = jnp.exp(s_scr[k, :, t * tk:(t + 1) * tk] - tiled(m_s, tk))
        lsum = lsum + col_sum(p)
        acc = acc + _dot_nt(p.astype(BF16), kv_buf[2 * buf + 1, :, t * tk:(t + 1) * tk].astype(BF16))
        yield
    o_s = acc / jnp.maximum(jnp.sum(lsum, axis=-1, keepdims=True), 1e-30)

    o_ref[k] = _combine_heads(gate_ref[k], o_c, o_s, o_w, nq).astype(o_ref.dtype)


def _nsa_sample(pt_flat, q3, gates3, kc, vc, cache_t, new_rows, win_t, expand, *, past, tk, seqs_per_step):
    n_seq, nq, _ = q3.shape
    nbp = kc.shape[1]
    rows = NSA_HEADS * nq
    sb = seqs_per_step
    per_seq = lambda a: pl.BlockSpec((sb,) + a.shape[1:], lambda b, pt: (b,) + (0,) * (a.ndim - 1))
    new = lambda col: pl.BlockSpec((sb, LANES, KV_WIDTH), lambda b, pt: (b, 0, col))
    grid_spec = pltpu.PrefetchScalarGridSpec(
        num_scalar_prefetch=1,
        grid=(n_seq // sb,),
        in_specs=[per_seq(q3), per_seq(gates3), per_seq(kc), per_seq(vc),
                  pl.BlockSpec(memory_space=pl.ANY),
                  new(2), new(3), per_seq(win_t), new(4), new(5),
                  pl.BlockSpec(expand.shape, lambda b, pt: (0, 0), pipeline_mode=pl.Buffered(1))],
        out_specs=pl.BlockSpec((sb, nq, NSA_WIDTH), lambda b, pt: (b, 0, 0)),
        scratch_shapes=[pltpu.VMEM((4 * sb, KV_WIDTH, past), F32),
                        pltpu.VMEM((sb, rows, past), F32), pltpu.SemaphoreType.DMA((2 * sb,))])
    return pl.pallas_call(
        functools.partial(_nsa_sample_kernel, nq=nq, past=past, nbp=nbp, tk=tk),
        grid_spec=grid_spec,
        out_shape=jax.ShapeDtypeStruct((n_seq, nq, NSA_WIDTH), F32),
        compiler_params=_params("arbitrary"),
        name="nsa_sample",
    )(pt_flat, q3, gates3, kc, vc, cache_t, new_rows, new_rows, win_t, new_rows, new_rows, expand)


def _expand_matrix(nb, n_keys):
    return jnp.asarray(np.arange(n_keys)[None, :] // BLK == np.arange(nb)[:, None], dtype=BF16)


def _merge_kernel(x_ref, ssm_ref, *refs, mirrored):
    nsa_refs = refs[:2] if mirrored else refs[:1]
    gpre_ref, wm_ref, wbs_ref, wbn_ref, wo_ref, gpost_ref, gx_ref, wxq_ref, x1_ref, qx_ref = refs[len(nsa_refs):]
    x = x_ref[...]
    tm = x.shape[0]
    if mirrored:
        upper = nsa_refs[1][...]
        n_chunks = tm // BLK
        upper = jnp.concatenate([upper[(n_chunks - 1 - s) * BLK:(n_chunks - s) * BLK] for s in range(n_chunks)],
                                axis=0)
        nsa = jnp.where(pl.program_id(0) < pl.num_programs(0) // 2, nsa_refs[0][...], upper)
    else:
        nsa = nsa_refs[0][...]
    n_seq = ssm_ref.shape[1] // tm
    ssm = jnp.concatenate([ssm_ref[s, pl.ds(pl.program_id(1), tm, stride=n_seq), :]
                           for s in range(SSM_WIDTH // LANES)], axis=1)
    a = _rms(x, gpre_ref[...]).astype(BF16)
    g_ssm = _sigmoid(_dot(a, wm_ref[:, 0:D_MODEL]))
    g_nsa = _sigmoid(_dot(a, wm_ref[:, D_MODEL:2 * D_MODEL]))
    merged = (g_ssm * _dot(ssm.astype(BF16), wbs_ref[...])
              + g_nsa * _dot(nsa, wbn_ref[...]))
    x1 = x + _rms(_dot(merged.astype(BF16), wo_ref[...]), gpost_ref[...])
    x1_ref[...] = x1
    c = _rms(x1, gx_ref[...]).astype(BF16)
    qx_ref[...] = (_dot(c, wxq_ref[...]) * (X_HEAD_DIM ** -0.5)).astype(BF16)


def _merge(x2d, ssm_slabs, nsa_o, weights, *, tm, n_tiles):
    n = x2d.shape[0]
    n_b = n // (n_tiles * tm)
    row = lambda w_: pl.BlockSpec((tm, w_), lambda t, b: (b * n_tiles + t, 0))
    full = lambda a: pl.BlockSpec(a.shape, lambda t, b: (0,) * a.ndim, pipeline_mode=pl.Buffered(1))
    ssm_spec = pl.BlockSpec((SSM_WIDTH // LANES, tm * n_b, LANES), lambda t, b: (0, t, 0))
    mirrored = isinstance(nsa_o, tuple)
    if mirrored:
        half = n_tiles // 2
        nsa_specs = [pl.BlockSpec((tm, NSA_WIDTH), lambda t, b: (b * half + jnp.minimum(t, half - 1), 0)),
                     pl.BlockSpec((tm, NSA_WIDTH), lambda t, b: (b * half + jnp.minimum(n_tiles - 1 - t, half - 1), 0))]
        nsa_args = list(nsa_o)
    else:
        nsa_specs, nsa_args = [row(NSA_WIDTH)], [nsa_o]
    return pl.pallas_call(
        functools.partial(_merge_kernel, mirrored=mirrored),
        grid=(n_tiles, n_b),
        in_specs=[row(D_MODEL), ssm_spec] + nsa_specs + [full(w) for w in weights],
        out_specs=[row(D_MODEL), row(X_WIDTH)],
        out_shape=[jax.ShapeDtypeStruct((n, D_MODEL), F32), jax.ShapeDtypeStruct((n, X_WIDTH), BF16)],
        compiler_params=_params("parallel", "arbitrary"),
        name="merge",
    )(x2d, ssm_slabs, *nsa_args, *weights)


def _xattn_tile(q, kv_ref, m_len):
    outs = []
    for h in range(X_HEADS):
        cols = slice(h * X_HEAD_DIM, (h + 1) * X_HEAD_DIM)
        k = kv_ref[0, pl.ds(h, m_len, stride=2 * X_HEADS), :].astype(BF16)
        v = kv_ref[0, pl.ds(X_HEADS + h, m_len, stride=2 * X_HEADS), :].astype(BF16)
        s = _dot_nt(q[:, cols], k)
        m = jnp.max(s, axis=-1, keepdims=True)
        p = jnp.exp(s - m)
        p = p / jnp.sum(p, axis=-1, keepdims=True)
        outs.append(_dot(p.astype(BF16), v))
    return jnp.concatenate(outs, axis=1)


def _xattn_kernel(q_ref, kv_ref, o_ref, *, m_len):
    t = q_ref.shape[1]
    n_rows = 2 * X_HEADS
    for s in range(q_ref.shape[0]):
        q = q_ref[s].astype(F32)
        z = jnp.zeros((t, X_HEAD_DIM), F32)
        q_bd = jnp.concatenate(
            [jnp.concatenate([q[:, h * X_HEAD_DIM:(h + 1) * X_HEAD_DIM] if j == h else z for j in range(X_HEADS)],
                             axis=1) for h in range(X_HEADS)], axis=0).astype(BF16)
        k = jnp.concatenate([kv_ref[s, pl.ds(h, m_len, stride=n_rows), :] for h in range(X_HEADS)],
                            axis=1).astype(BF16)
        v = jnp.concatenate([kv_ref[s, pl.ds(X_HEADS + h, m_len, stride=n_rows), :] for h in range(X_HEADS)],
                            axis=1).astype(BF16)
        sc = _dot_nt(q_bd, k)
        p = jnp.exp(sc - jnp.max(sc, axis=-1, keepdims=True))
        p = p / jnp.sum(p, axis=-1, keepdims=True)
        o = _dot(p.astype(BF16), v)
        o_ref[s] = jnp.concatenate([o[h * t:(h + 1) * t, h * X_HEAD_DIM:(h + 1) * X_HEAD_DIM]
                                    for h in range(X_HEADS)], axis=1).astype(o_ref.dtype)


def _xattn(q3, mem_kv_rows, *, seqs_per_step):
    n_seq, t, _ = q3.shape
    m_len = mem_kv_rows.shape[1] // (2 * X_HEADS)
    sb = seqs_per_step
    return pl.pallas_call(
        functools.partial(_xattn_kernel, m_len=m_len),
        grid=(n_seq // sb,),
        in_specs=[pl.BlockSpec((sb, t, X_WIDTH), lambda b: (b, 0, 0)),
                  pl.BlockSpec((sb, m_len * 2 * X_HEADS, X_HEAD_DIM), lambda b: (b, 0, 0))],
        out_specs=pl.BlockSpec((sb, t, X_WIDTH), lambda b: (b, 0, 0)),
        out_shape=jax.ShapeDtypeStruct((n_seq, t, X_WIDTH), q3.dtype),
        compiler_params=_params("parallel"),
        name="xattn",
    )(q3, mem_kv_rows)


def _mlp_kernel(x1_ref, o_ref, *refs, m_len):
    if m_len:
        kv_ref, refs = refs[0], refs[1:]
        o = _xattn_tile(o_ref[...], kv_ref, m_len).astype(BF16)
    else:
        o = o_ref[...]
    wxo_ref, gxp_ref, gm_ref, wup_ref, wdn_ref, gmp_ref, y_ref = refs
    x2 = x1_ref[...] + _rms(_dot(o, wxo_ref[...]), gxp_ref[...])
    m = _rms(x2, gm_ref[...]).astype(BF16)
    hid = jnp.maximum(_dot(m, wup_ref[...]), 0.0)
    hid = (hid * hid).astype(BF16)
    y_ref[...] = x2 + _rms(_dot(hid, wdn_ref[...]), gmp_ref[...])


def _mlp(x1, o, weights, *, tm, mem_kv_rows=None, tiles_per_seq=1):
    n = x1.shape[0]
    row = lambda w_: pl.BlockSpec((tm, w_), lambda i: (i, 0))
    full = lambda a: pl.BlockSpec(a.shape, lambda i: (0,) * a.ndim, pipeline_mode=pl.Buffered(1))
    m_len, kv_specs, kv_args = 0, [], []
    if mem_kv_rows is not None:
        m_len = mem_kv_rows.shape[1] // (2 * X_HEADS)
        kv_specs = [pl.BlockSpec((1,) + mem_kv_rows.shape[1:], lambda i: (i // tiles_per_seq, 0, 0))]
        kv_args = [mem_kv_rows]
    return pl.pallas_call(
        functools.partial(_mlp_kernel, m_len=m_len),
        grid=(n // tm,),
        in_specs=[row(D_MODEL), row(X_WIDTH)] + kv_specs + [full(w) for w in weights],
        out_specs=row(D_MODEL),
        out_shape=jax.ShapeDtypeStruct((n, D_MODEL), F32),
        compiler_params=_params("parallel"),
        name="mlp",
    )(x1, o, *kv_args, *weights)


def _memkv_kernel(m_ref, g_ref, w_ref, o_ref, *, tm):
    kv = _dot(_rms(m_ref[...], g_ref[...]).astype(BF16), w_ref[...])
    n_rows = 2 * X_HEADS
    for j in range(n_rows):
        o_ref[pl.ds(j, tm, stride=n_rows), :] = kv[:, j * X_HEAD_DIM:(j + 1) * X_HEAD_DIM]


def _memkv(mem2d, g, w, *, tm):
    n = mem2d.shape[0]
    n_rows = 2 * X_HEADS
    return pl.pallas_call(
        functools.partial(_memkv_kernel, tm=tm),
        grid=(n // tm,),
        in_specs=[pl.BlockSpec((tm, D_MODEL), lambda i: (i, 0)),
                  pl.BlockSpec(g.shape, lambda i: (0, 0)), pl.BlockSpec(w.shape, lambda i: (0, 0))],
        out_specs=pl.BlockSpec((tm * n_rows, X_HEAD_DIM), lambda i: (i, 0)),
        out_shape=jax.ShapeDtypeStruct((n * n_rows, X_HEAD_DIM), F32),
        compiler_params=_params("parallel"),
        name="memkv",
    )(mem2d, g, w)


def _row(v):
    return v.astype(F32).reshape(1, -1)


def kernel(x_prompt, x_sample, cache_nsa_kv, cache_win_kv, state_ssm, cache_mem_kv, page_table, mem_prompt, g_mix_pre, w_in, ssm_lam_re, ssm_lam_im, ssm_log_dt, ssm_b_re, ssm_b_im, ssm_c_re, ssm_c_im, ssm_d, w_glu, b_glu, cmp_pe_k, w_cmpk1, w_cmpk2, cmp_pe_v, w_cmpv1, w_cmpv2, w_br_ssm, w_br_nsa, w_out, g_mix_post, g_x_pre, g_mem, w_xq, w_xk, w_xv, w_xo, g_x_post, g_mlp_pre, w_up, w_down, g_mlp_post):
    depth = w_in.shape[0]
    n_seq_p, seq, _ = x_prompt.shape
    n_seq_s, nq, _ = x_sample.shape
    past = page_table.shape[1] * PAGE_SIZE
    assert depth == 1 and seq % BLK == 0 and nq <= SUBLANES and past % BLK == 0
    assert n_seq_p == SUBLANES and n_seq_s % SUBLANES == 0

    y_p = x_prompt.reshape(n_seq_p * seq, D_MODEL)
    y_s = x_sample.reshape(n_seq_s * nq, D_MODEL)
    l = 0

    w_proj = w_in[l, :, :N_PROJ].astype(BF16)
    w_gate = jnp.pad(w_in[l, :, N_PROJ:N_PROJ + N_GATE], ((0, 0), (0, LANES - N_GATE))).astype(BF16)
    w_merge = w_in[l, :, N_PROJ + N_GATE:].astype(BF16)
    lam_l, bm, cm = _ssm_params(ssm_lam_re[l], ssm_lam_im[l], ssm_log_dt[l], ssm_b_re[l], ssm_b_im[l],
                                ssm_c_re[l], ssm_c_im[l])
    ssm_w = (lam_l, bm, cm, _row(ssm_d[l]), w_glu[l].astype(BF16), _row(b_glu[l]))
    cmp_w = _compress_params(cmp_pe_k[l], w_cmpk1[l], w_cmpk2[l], cmp_pe_v[l], w_cmpv1[l], w_cmpv2[l])
    merge_w = (_row(g_mix_pre[l]), w_merge, w_br_ssm[l].astype(BF16), w_br_nsa[l].astype(BF16),
               w_out[l].astype(BF16), _row(g_mix_post[l]), _row(g_x_pre[l]), w_xq[l].astype(BF16))
    mlp_w = (w_xo[l].astype(BF16), _row(g_x_post[l]), _row(g_mlp_pre[l]), w_up[l].astype(BF16),
             w_down[l].astype(BF16), _row(g_mlp_post[l]))
    w_mem = jnp.concatenate([w_xk[l], w_xv[l]], axis=1).astype(BF16)

    tm_p = 512 if seq % 512 == 0 else seq
    nt_p = seq // tm_p
    tabs_p = _rope_tables(np.arange(seq))
    u_p, q_p, kvt_p, wint_p, kvb_p, gate_p = _proj(
        y_p, _row(g_mix_pre[l]), w_proj, w_gate, tabs_p, tm=tm_p, n_tab_blocks=nt_p, prompt=True)
    h0_p = jnp.zeros((1, SUBLANES, 2 * N_STATE), F32)
    tc_p = 64 if seq % 64 == 0 else seq
    ssm_p, hl_p = _ssm(u_p, h0_p, *ssm_w, n_groups=1, n_time=seq, tc=tc_p)

    assert seq % PAGE_SIZE == 0
    n_pages_p = n_seq_p * seq // PAGE_SIZE
    pe_t, w1_t = _compress_paged_params(cmp_pe_k[l], w_cmpk1[l], cmp_pe_v[l], w_cmpv1[l])
    cmp_p = _compress_paged(jnp.zeros((n_pages_p,), jnp.int32), kvt_p, pe_t, w1_t, cmp_w[2],
                            m=min(n_pages_p, 128), pages_per_seq=seq // PAGE_SIZE)
    nsa_p = _nsa_prompt(q_p, gate_p, cmp_p, kvb_p, n_seq=n_seq_p, seq=seq)

    mem_kv_p = _memkv(mem_prompt.reshape(-1, D_MODEL), _row(g_mem[l]), w_mem, tm=256)
    m_len = mem_prompt.shape[1]
    mem_kv_p3 = mem_kv_p.reshape(n_seq_p, m_len * 2 * X_HEADS, X_HEAD_DIM)

    if nt_p % 2:
        nsa_p = _unmirror(*nsa_p, n_seq_p)
    x1_p, qx_p = _merge(y_p, ssm_p, nsa_p, merge_w, tm=tm_p, n_tiles=nt_p)
    y_p = _mlp(x1_p, qx_p, mlp_w, tm=tm_p, mem_kv_rows=mem_kv_p3, tiles_per_seq=nt_p)

    n_s = n_seq_s * nq
    tabs_s = tuple(np.tile(t, (n_seq_s, 1)) for t in _rope_tables(past + np.arange(nq)))
    u_s, q_s, kv_s, win_s, kvb_s, gate_s = _proj(y_s, _row(g_mix_pre[l]), w_proj, w_gate, tabs_s,
                                                 tm=n_s, n_tab_blocks=1, prompt=False)
    n_grp = n_seq_s // SUBLANES
    n_slabs = SSM_WIDTH // LANES
    u_s = (u_s.reshape(n_grp, SUBLANES, nq, n_slabs, LANES).transpose(3, 0, 2, 1, 4)
           .reshape(n_slabs, n_s, LANES))
    st = state_ssm[l].astype(F32).reshape(n_seq_s, N_STATE, 2)
    h0_s = _state_lanes(st[..., 0], st[..., 1]).reshape(n_grp, SUBLANES, 2 * N_STATE)
    ssm_s, hl_s = _ssm(u_s, h0_s, *ssm_w, n_groups=n_grp, n_time=nq, tc=nq)
    ssm_s = (ssm_s.reshape(n_slabs, n_grp, nq, SUBLANES, LANES).transpose(0, 1, 3, 2, 4)
             .reshape(n_slabs, n_s, LANES))

    n_pages = page_table.shape[1]
    n_pool = cache_nsa_kv.shape[1]
    cache_t = jnp.transpose(cache_nsa_kv[l], (0, 2, 3, 4, 1)).reshape(n_pool, 4, KV_WIDTH, PAGE_SIZE)
    win_t = jnp.transpose(cache_win_kv[l], (0, 2, 3, 4, 1)).reshape(n_seq_s, 2, KV_WIDTH, -1)
    pt_flat = page_table.reshape(-1).astype(jnp.int32)
    nb_past = past // BLK
    cmp_pages = _compress_paged(pt_flat, cache_t, pe_t, w1_t, cmp_w[2], m=min(n_seq_s * n_pages, 128))
    cmp_past = cmp_pages
    new_rows = jnp.pad(jnp.concatenate([kv_s, win_s], axis=1).reshape(n_seq_s, nq, 6 * KV_WIDTH),
                       ((0, 0), (0, LANES - nq), (0, 0)))
    cmp_new = _compress(new_rows[:, :BLK].reshape(n_seq_s * BLK, 6 * KV_WIDTH), *cmp_w,
                        n_blocks=n_seq_s, tm=n_seq_s)
    nbp = -(-(nb_past + 1) // LANES) * LANES
    cmp_s = jnp.concatenate([cmp_past.reshape(2, n_seq_s, nb_past, KV_WIDTH), cmp_new[:, :, None, :],
                             jnp.zeros((2, n_seq_s, nbp - nb_past - 1, KV_WIDTH), F32)], axis=2)
    nsa_s = _nsa_sample(pt_flat, q_s.astype(F32).reshape(n_seq_s, nq, NSA_WIDTH),
                        gate_s.reshape(n_seq_s, nq, LANES), cmp_s[0], cmp_s[1], cache_t, new_rows, win_t,
                        _expand_matrix(nbp, past + LANES), past=past, tk=min(past, 1024), seqs_per_step=2)

    x1_s, qx_s = _merge(y_s, ssm_s, nsa_s.reshape(n_s, NSA_WIDTH).astype(BF16), merge_w, tm=n_s, n_tiles=1)
    mem_kv_s3 = cache_mem_kv[l].reshape(n_seq_s, m_len * 2 * X_HEADS, X_HEAD_DIM)
    o_s = _xattn(qx_s.astype(F32).reshape(n_seq_s, nq, X_WIDTH), mem_kv_s3, seqs_per_step=SUBLANES)
    y_s = _mlp(x1_s, o_s.reshape(-1, X_WIDTH).astype(BF16), mlp_w, tm=n_s)

    def ssm_state(hl, n_seq):
        re, im = _state_unlanes(hl.reshape(n_seq, 2 * N_STATE))
        return jnp.stack([re, im], axis=-1).reshape(1, n_seq, SSM_GROUPS, SSM_STATE, 2)

    def token_major(xt):
        n_seq, parts, _, t_len = xt.shape
        return xt.reshape(1, n_seq, parts, NSA_KV, HEAD_DIM, t_len).transpose(0, 1, 5, 2, 3, 4)

    w_keep = min(WINDOW, seq)
    win_new = win_s.reshape(n_seq_s, nq, 2, NSA_KV, HEAD_DIM).astype(cache_win_kv.dtype)
    win_sample = jnp.concatenate([cache_win_kv[l], win_new], axis=1)[:, nq:]
    return (y_p.reshape(n_seq_p, seq, D_MODEL),
            y_s.reshape(n_seq_s, nq, D_MODEL),
            token_major(kvt_p),
            kv_s.reshape(1, n_seq_s, nq, 4, NSA_KV, HEAD_DIM),
            token_major(wint_p[:, :, :, seq - w_keep:]),
            win_sample[None],
            ssm_state(hl_p, n_seq_p),
            ssm_state(hl_s, n_seq_s),
            mem_kv_p.reshape(1, n_seq_p, m_len, 2, X_HEADS, X_HEAD_DIM))
```
